```python
import math
import jax, jax.numpy as jnp
from jax import lax
import numpy as np

D_MODEL = 1024
BATCH = 8
SEQ = 2048
DEPTH = 2

CHUNK = 64
Q_BLOCK = 128
EPS = 1e-6

MLA_HEADS = 8
MLA_NOPE = 64
MLA_ROPE = 32
MLA_V = 64
MLA_Q_RANK = 256
MLA_KV_RANK = 128
MLA_WIDTH = MLA_HEADS * MLA_V
MLA_QK_DIM = MLA_NOPE + MLA_ROPE
ROPE_THETA = 10000.0

FOX_HEADS = 8
FOX_HEAD_DIM = 64
FOX_WIDTH = FOX_HEADS * FOX_HEAD_DIM

S5_WIDTH = 512
S5_GROUP = 16
S5_GROUPS = S5_WIDTH // S5_GROUP
S5_STATE = 64
DT_MIN = 1e-3
DT_MAX = 1e-1

BRANCH_WIDTH = MLA_WIDTH + FOX_WIDTH + S5_WIDTH
IN_SPLITS = (MLA_Q_RANK, MLA_KV_RANK, MLA_ROPE,
             FOX_WIDTH, FOX_WIDTH, FOX_WIDTH, FOX_HEADS,
             S5_WIDTH,
             MLA_WIDTH, FOX_WIDTH, S5_WIDTH,
             D_MODEL, D_MODEL, D_MODEL)
IN_WIDTH = MLA_Q_RANK + MLA_KV_RANK + MLA_ROPE + 3 * FOX_WIDTH + FOX_HEADS + S5_WIDTH + BRANCH_WIDTH + 3 * D_MODEL

kernel_name = "hybrid_mla_fox_s5_gated_trunk"


def rmsnorm(x, g):
    xf = x.astype(jnp.float32)
    y = xf * lax.rsqrt(jnp.mean(xf * xf, axis=-1, keepdims=True) + EPS)
    return (y * g.astype(jnp.float32)).astype(x.dtype)


def rope_tables(positions, dtype):
    inv = 1.0 / (ROPE_THETA ** (jnp.arange(0, MLA_ROPE, 2, dtype=jnp.float32) / MLA_ROPE))
    ang = positions.astype(jnp.float32)[..., None] * inv
    return jnp.cos(ang).astype(dtype)[:, :, None, :], jnp.sin(ang).astype(dtype)[:, :, None, :]


def apply_rope(x, cos, sin):
    x1, x2 = jnp.split(x, 2, axis=-1)
    return jnp.concatenate([x1 * cos - x2 * sin, x1 * sin + x2 * cos], axis=-1)


def swept_attention(q, k, v, scale, chunk_causal, cum_logf=None):
    S = q.shape[1]
    outs = []
    for i in range(S // Q_BLOCK):
        q0, q1 = i * Q_BLOCK, (i + 1) * Q_BLOCK
        kv_end = q1
        logits = jnp.einsum('bqhd,bkhd->bhqk', q[:, q0:q1], k[:, :kv_end]).astype(jnp.float32) * scale
        qpos = jnp.arange(q0, q1)[:, None]
        kpos = jnp.arange(kv_end)[None, :]
        if chunk_causal:
            allowed = (kpos // CHUNK) <= (qpos // CHUNK)
        else:
            allowed = kpos <= qpos
        if cum_logf is not None:
            c_q = jnp.transpose(cum_logf[:, q0:q1], (0, 2, 1))[..., :, None]
            c_k = jnp.transpose(cum_logf[:, :kv_end], (0, 2, 1))[..., None, :]
            logits = logits + (c_q - c_k)
        logits = jnp.where(allowed, logits, -jnp.inf)
        p = jax.nn.softmax(logits, axis=-1).astype(v.dtype)
        outs.append(jnp.einsum('bhqk,bkhd->bqhd', p, v[:, :kv_end]))
    return jnp.concatenate(outs, axis=1)


def mla_branch(cq, ckv, kpe, cos, sin, q_a_norm, w_q_up, kv_a_norm, w_kv_up, q_norm, k_norm):
    Bsz, S, _ = cq.shape
    q = (rmsnorm(cq, q_a_norm) @ w_q_up).reshape(Bsz, S, MLA_HEADS, MLA_QK_DIM)
    q_nope, q_pe = jnp.split(q, [MLA_NOPE], axis=-1)
    kv = (rmsnorm(ckv, kv_a_norm) @ w_kv_up).reshape(Bsz, S, MLA_HEADS, MLA_NOPE + MLA_V)
    k_nope, v = jnp.split(kv, [MLA_NOPE], axis=-1)
    k_pe = apply_rope(kpe[:, :, None, :], cos, sin)
    q = jnp.concatenate([q_nope, apply_rope(q_pe, cos, sin)], axis=-1)
    k = jnp.concatenate([k_nope, jnp.broadcast_to(k_pe, (Bsz, S, MLA_HEADS, MLA_ROPE))], axis=-1)
    q = rmsnorm(q, q_norm)
    k = rmsnorm(k, k_norm)
    y = swept_attention(q, k, v, 1.0 / math.sqrt(MLA_QK_DIM), chunk_causal=True)
    return y.reshape(Bsz, S, MLA_WIDTH)


def fox_branch(fq, fk, fv, ff, b_f, q_norm, k_norm):
    Bsz, S, _ = fq.shape
    q = rmsnorm(fq.reshape(Bsz, S, FOX_HEADS, FOX_HEAD_DIM), q_norm)
    k = rmsnorm(fk.reshape(Bsz, S, FOX_HEADS, FOX_HEAD_DIM), k_norm)
    v = fv.reshape(Bsz, S, FOX_HEADS, FOX_HEAD_DIM)
    log_f = jax.nn.log_sigmoid(ff.astype(jnp.float32) + b_f.astype(jnp.float32))
    cum = jnp.cumsum(log_f, axis=1)
    y = swept_attention(q, k, v, 1.0 / math.sqrt(FOX_HEAD_DIM), chunk_causal=False, cum_logf=cum)
    return y.reshape(Bsz, S, FOX_WIDTH)


def _ssm_combine(e1, e2):
    a1r, a1i, b1r, b1i = e1
    a2r, a2i, b2r, b2i = e2
    return (a2r * a1r - a2i * a1i,
            a2r * a1i + a2i * a1r,
            a2r * b1r - a2i * b1i + b2r,
            a2r * b1i + a2i * b1r + b2i)


def s5_branch(u, lam_re, lam_im, log_dt, b_re, b_im, c_re, c_im, d_skip, w_glu, b_glu):
    Bsz, S, _ = u.shape
    uf = u.astype(jnp.float32).reshape(Bsz, S, S5_GROUPS, S5_GROUP)
    dt = jnp.exp(log_dt.astype(jnp.float32))[:, None]
    lr = lam_re.astype(jnp.float32)
    li = lam_im.astype(jnp.float32)
    mag = jnp.exp(lr * dt)
    a_re = mag * jnp.cos(li * dt)
    a_im = mag * jnp.sin(li * dt)
    den = lr * lr + li * li
    f_re = ((a_re - 1.0) * lr + a_im * li) / den
    f_im = (a_im * lr - (a_re - 1.0) * li) / den
    br = b_re.astype(jnp.float32)
    bi = b_im.astype(jnp.float32)
    bb_re = f_re[..., None] * br - f_im[..., None] * bi
    bb_im = f_re[..., None] * bi + f_im[..., None] * br
    bu_re = jnp.einsum('bsgc,gnc->bsgn', uf, bb_re)
    bu_im = jnp.einsum('bsgc,gnc->bsgn', uf, bb_im)
    a_re_full = jnp.broadcast_to(a_re, bu_re.shape)
    a_im_full = jnp.broadcast_to(a_im, bu_re.shape)
    _, _, x_re, x_im = lax.associative_scan(_ssm_combine, (a_re_full, a_im_full, bu_re, bu_im), axis=1)
    y = (jnp.einsum('bsgn,gcn->bsgc', x_re, c_re.astype(jnp.float32))
         - jnp.einsum('bsgn,gcn->bsgc', x_im, c_im.astype(jnp.float32)))
    y = (y + d_skip.astype(jnp.float32).reshape(S5_GROUPS, S5_GROUP) * uf).reshape(Bsz, S, S5_WIDTH)
    z = jax.nn.gelu(y)
    z = z * jax.nn.sigmoid(z @ w_glu.astype(jnp.float32) + b_glu.astype(jnp.float32))
    return z.astype(u.dtype)


def hybrid_layer(x, cos, sin, norm_g, w_in,
                 mla_q_a_norm, mla_w_q_up, mla_kv_a_norm, mla_w_kv_up, mla_q_norm, mla_k_norm,
                 fox_b_f, fox_q_norm, fox_k_norm,
                 s5_lambda_re, s5_lambda_im, s5_log_dt, s5_b_re, s5_b_im, s5_c_re, s5_c_im,
                 s5_d, s5_w_glu, s5_b_glu, w_branch_out, w_out):
    h = rmsnorm(x, norm_g)
    proj = h @ w_in
    offsets = np.cumsum(IN_SPLITS)[:-1].tolist()
    (cq, ckv, kpe, fq, fk, fv, ff, s5u,
     g_mla, g_fox, g_s5, m_mla, m_fox, m_s5) = jnp.split(proj, offsets, axis=-1)

    y_mla = mla_branch(cq, ckv, kpe, cos, sin, mla_q_a_norm, mla_w_q_up,
                       mla_kv_a_norm, mla_w_kv_up, mla_q_norm, mla_k_norm)
    y_fox = fox_branch(fq, fk, fv, ff, fox_b_f, fox_q_norm, fox_k_norm)
    y_s5 = s5_branch(s5u, s5_lambda_re, s5_lambda_im, s5_log_dt, s5_b_re, s5_b_im,
                     s5_c_re, s5_c_im, s5_d, s5_w_glu, s5_b_glu)

    wo_mla, wo_fox, wo_s5 = jnp.split(w_branch_out, [MLA_WIDTH, MLA_WIDTH + FOX_WIDTH], axis=0)
    o_mla = (y_mla * jax.nn.silu(g_mla)) @ wo_mla
    o_fox = (y_fox * jax.nn.silu(g_fox)) @ wo_fox
    o_s5 = (y_s5 * jax.nn.silu(g_s5)) @ wo_s5
    merged = jax.nn.sigmoid(m_mla) * o_mla + jax.nn.sigmoid(m_fox) * o_fox + jax.nn.sigmoid(m_s5) * o_s5
    return x + merged @ w_out


def _fwd_setup_inputs(seed: int = 0) -> dict:
    key = jax.random.key(seed)
    ks = jax.random.split(key, 32)
    f32 = jnp.float32

    def nrm(k, shape, scale):
        return jax.random.normal(k, shape, f32) * scale

    def gain(k, shape):
        return 1.0 + 0.02 * jax.random.normal(k, shape, f32)

    L = DEPTH
    x = jax.random.normal(ks[0], (BATCH, SEQ, D_MODEL), f32)
    start = jax.random.randint(ks[1], (BATCH, 1), 0, 4096, dtype=jnp.int32)
    positions = start + jnp.arange(SEQ, dtype=jnp.int32)[None, :]
    n_idx = jnp.arange(S5_STATE, dtype=f32)
    log_dt = jax.random.uniform(ks[17], (L, S5_GROUPS), f32, math.log(DT_MIN), math.log(DT_MAX))
    return {
        "x": x,
        "positions": positions,
        "norm_g": gain(ks[2], (L, D_MODEL)),
        "w_in": nrm(ks[3], (L, D_MODEL, IN_WIDTH), D_MODEL ** -0.5),
        "mla_q_a_norm": gain(ks[4], (L, MLA_Q_RANK)),
        "mla_w_q_up": nrm(ks[5], (L, MLA_Q_RANK, MLA_HEADS * MLA_QK_DIM), MLA_Q_RANK ** -0.5),
        "mla_kv_a_norm": gain(ks[6], (L, MLA_KV_RANK)),
        "mla_w_kv_up": nrm(ks[7], (L, MLA_KV_RANK, MLA_HEADS * (MLA_NOPE + MLA_V)), MLA_KV_RANK ** -0.5),
        "mla_q_norm": gain(ks[8], (L, MLA_QK_DIM)),
        "mla_k_norm": gain(ks[9], (L, MLA_QK_DIM)),
        "fox_b_f": 1.0 + 0.5 * jax.random.normal(ks[10], (L, FOX_HEADS), f32),
        "fox_q_norm": gain(ks[11], (L, FOX_HEAD_DIM)),
        "fox_k_norm": gain(ks[12], (L, FOX_HEAD_DIM)),
        "s5_lambda_re": -0.5 * (1.0 + 0.02 * jax.random.normal(ks[13], (L, S5_GROUPS, S5_STATE), f32)),
        "s5_lambda_im": math.pi * n_idx + 0.01 * jax.random.normal(ks[14], (L, S5_GROUPS, S5_STATE), f32),
        "s5_log_dt": log_dt,
        "s5_b_re": nrm(ks[15], (L, S5_GROUPS, S5_STATE, S5_GROUP), (2.0 * S5_GROUP) ** -0.5),
        "s5_b_im": nrm(ks[16], (L, S5_GROUPS, S5_STATE, S5_GROUP), (2.0 * S5_GROUP) ** -0.5),
        "s5_c_re": nrm(ks[18], (L, S5_GROUPS, S5_GROUP, S5_STATE), (2.0 * S5_STATE) ** -0.5),
        "s5_c_im": nrm(ks[19], (L, S5_GROUPS, S5_GROUP, S5_STATE), (2.0 * S5_STATE) ** -0.5),
        "s5_d": nrm(ks[20], (L, S5_WIDTH), 1.0),
        "s5_w_glu": nrm(ks[21], (L, S5_WIDTH, S5_WIDTH), S5_WIDTH ** -0.5),
        "s5_b_glu": nrm(ks[22], (L, S5_WIDTH), 0.02),
        "w_branch_out": nrm(ks[23], (L, BRANCH_WIDTH, D_MODEL), MLA_WIDTH ** -0.5),
        "w_out": nrm(ks[24], (L, D_MODEL, D_MODEL), D_MODEL ** -0.5),
    }


def _fwd_reference(x, positions, norm_g, w_in,
              mla_q_a_norm, mla_w_q_up, mla_kv_a_norm, mla_w_kv_up, mla_q_norm, mla_k_norm,
              fox_b_f, fox_q_norm, fox_k_norm,
              s5_lambda_re, s5_lambda_im, s5_log_dt, s5_b_re, s5_b_im, s5_c_re, s5_c_im,
              s5_d, s5_w_glu, s5_b_glu, w_branch_out, w_out):
    cos, sin = rope_tables(positions, x.dtype)
    h = x
    for l in range(DEPTH):
        h = hybrid_layer(h, cos, sin, norm_g[l], w_in[l],
                         mla_q_a_norm[l], mla_w_q_up[l], mla_kv_a_norm[l], mla_w_kv_up[l],
                         mla_q_norm[l], mla_k_norm[l],
                         fox_b_f[l], fox_q_norm[l], fox_k_norm[l],
                         s5_lambda_re[l], s5_lambda_im[l], s5_log_dt[l], s5_b_re[l], s5_b_im[l],
                         s5_c_re[l], s5_c_im[l], s5_d[l], s5_w_glu[l], s5_b_glu[l],
                         w_branch_out[l], w_out[l])
    return h


import jax as _jax
import jax.numpy as _jnp

TWIN_FORMAT = 'train_step'
FWD_PARAMS = ['x', 'positions', 'norm_g', 'w_in', 'mla_q_a_norm', 'mla_w_q_up', 'mla_kv_a_norm', 'mla_w_kv_up', 'mla_q_norm', 'mla_k_norm', 'fox_b_f', 'fox_q_norm', 'fox_k_norm', 's5_lambda_re', 's5_lambda_im', 's5_log_dt', 's5_b_re', 's5_b_im', 's5_c_re', 's5_c_im', 's5_d', 's5_w_glu', 's5_b_glu', 'w_branch_out', 'w_out']
TWIN_WEIGHTS = ['norm_g', 'w_in', 'mla_q_a_norm', 'mla_w_q_up', 'mla_kv_a_norm', 'mla_w_kv_up', 'mla_q_norm', 'mla_k_norm', 'fox_b_f', 'fox_q_norm', 'fox_k_norm', 's5_lambda_re', 's5_lambda_im', 's5_log_dt', 's5_b_re', 's5_b_im', 's5_c_re', 's5_c_im', 's5_d', 's5_w_glu', 's5_b_glu', 'w_branch_out', 'w_out']
TWIN_DIFF_INPUT = 'x'
TWIN_INPUTS = ['x', 'positions', 'norm_g', 'w_in', 'mla_q_a_norm', 'mla_w_q_up', 'mla_kv_a_norm', 'mla_w_kv_up', 'mla_q_norm', 'mla_k_norm', 'fox_b_f', 'fox_q_norm', 'fox_k_norm', 's5_lambda_re', 's5_lambda_im', 's5_log_dt', 's5_b_re', 's5_b_im', 's5_c_re', 's5_c_im', 's5_d', 's5_w_glu', 's5_b_glu', 'w_branch_out', 'w_out', 'loss_target', 'm_norm_g', 'm_w_in', 'm_mla_q_a_norm', 'm_mla_w_q_up', 'm_mla_kv_a_norm', 'm_mla_w_kv_up', 'm_mla_q_norm', 'm_mla_k_norm', 'm_fox_b_f', 'm_fox_q_norm', 'm_fox_k_norm', 'm_s5_lambda_re', 'm_s5_lambda_im', 'm_s5_log_dt', 'm_s5_b_re', 'm_s5_b_im', 'm_s5_c_re', 'm_s5_c_im', 'm_s5_d', 'm_s5_w_glu', 'm_s5_b_glu', 'm_w_branch_out', 'm_w_out', 'v_norm_g', 'v_w_in', 'v_mla_q_a_norm', 'v_mla_w_q_up', 'v_mla_kv_a_norm', 'v_mla_w_kv_up', 'v_mla_q_norm', 'v_mla_k_norm', 'v_fox_b_f', 'v_fox_q_norm', 'v_fox_k_norm', 'v_s5_lambda_re', 'v_s5_lambda_im', 'v_s5_log_dt', 'v_s5_b_re', 'v_s5_b_im', 'v_s5_c_re', 'v_s5_c_im', 'v_s5_d', 'v_s5_w_glu', 'v_s5_b_glu', 'v_w_branch_out', 'v_w_out']
TWIN_OUTPUTS = ['loss', 'grad_x', 'grad_norm_g', 'grad_w_in', 'grad_mla_q_a_norm', 'grad_mla_w_q_up', 'grad_mla_kv_a_norm', 'grad_mla_w_kv_up', 'grad_mla_q_norm', 'grad_mla_k_norm', 'grad_fox_b_f', 'grad_fox_q_norm', 'grad_fox_k_norm', 'grad_s5_lambda_re', 'grad_s5_lambda_im', 'grad_s5_log_dt', 'grad_s5_b_re', 'grad_s5_b_im', 'grad_s5_c_re', 'grad_s5_c_im', 'grad_s5_d', 'grad_s5_w_glu', 'grad_s5_b_glu', 'grad_w_branch_out', 'grad_w_out', 'delta_norm_g', 'delta_w_in', 'delta_mla_q_a_norm', 'delta_mla_w_q_up', 'delta_mla_kv_a_norm', 'delta_mla_w_kv_up', 'delta_mla_q_norm', 'delta_mla_k_norm', 'delta_fox_b_f', 'delta_fox_q_norm', 'delta_fox_k_norm', 'delta_s5_lambda_re', 'delta_s5_lambda_im', 'delta_s5_log_dt', 'delta_s5_b_re', 'delta_s5_b_im', 'delta_s5_c_re', 'delta_s5_c_im', 'delta_s5_d', 'delta_s5_w_glu', 'delta_s5_b_glu', 'delta_w_branch_out', 'delta_w_out', 'new_m_norm_g', 'new_m_w_in', 'new_m_mla_q_a_norm', 'new_m_mla_w_q_up', 'new_m_mla_kv_a_norm', 'new_m_mla_w_kv_up', 'new_m_mla_q_norm', 'new_m_mla_k_norm', 'new_m_fox_b_f', 'new_m_fox_q_norm', 'new_m_fox_k_norm', 'new_m_s5_lambda_re', 'new_m_s5_lambda_im', 'new_m_s5_log_dt', 'new_m_s5_b_re', 'new_m_s5_b_im', 'new_m_s5_c_re', 'new_m_s5_c_im', 'new_m_s5_d', 'new_m_s5_w_glu', 'new_m_s5_b_glu', 'new_m_w_branch_out', 'new_m_w_out', 'new_v_norm_g', 'new_v_w_in', 'new_v_mla_q_a_norm', 'new_v_mla_w_q_up', 'new_v_mla_kv_a_norm', 'new_v_mla_w_kv_up', 'new_v_mla_q_norm', 'new_v_mla_k_norm', 'new_v_fox_b_f', 'new_v_fox_q_norm', 'new_v_fox_k_norm', 'new_v_s5_lambda_re', 'new_v_s5_lambda_im', 'new_v_s5_log_dt', 'new_v_s5_b_re', 'new_v_s5_b_im', 'new_v_s5_c_re', 'new_v_s5_c_im', 'new_v_s5_d', 'new_v_s5_w_glu', 'new_v_s5_b_glu', 'new_v_w_branch_out', 'new_v_w_out']
TWIN_LEAF_KINDS = {'loss': 'loss', 'grad_x': 'grad_x', 'grad_norm_g': 'grad_w', 'grad_w_in': 'grad_w', 'grad_mla_q_a_norm': 'grad_w', 'grad_mla_w_q_up': 'grad_w', 'grad_mla_kv_a_norm': 'grad_w', 'grad_mla_w_kv_up': 'grad_w', 'grad_mla_q_norm': 'grad_w', 'grad_mla_k_norm': 'grad_w', 'grad_fox_b_f': 'grad_w', 'grad_fox_q_norm': 'grad_w', 'grad_fox_k_norm': 'grad_w', 'grad_s5_lambda_re': 'grad_w', 'grad_s5_lambda_im': 'grad_w', 'grad_s5_log_dt': 'grad_w', 'grad_s5_b_re': 'grad_w', 'grad_s5_b_im': 'grad_w', 'grad_s5_c_re': 'grad_w', 'grad_s5_c_im': 'grad_w', 'grad_s5_d': 'grad_w', 'grad_s5_w_glu': 'grad_w', 'grad_s5_b_glu': 'grad_w', 'grad_w_branch_out': 'grad_w', 'grad_w_out': 'grad_w', 'delta_norm_g': 'delta_w', 'delta_w_in': 'delta_w', 'delta_mla_q_a_norm': 'delta_w', 'delta_mla_w_q_up': 'delta_w', 'delta_mla_kv_a_norm': 'delta_w', 'delta_mla_w_kv_up': 'delta_w', 'delta_mla_q_norm': 'delta_w', 'delta_mla_k_norm': 'delta_w', 'delta_fox_b_f': 'delta_w', 'delta_fox_q_norm': 'delta_w', 'delta_fox_k_norm': 'delta_w', 'delta_s5_lambda_re': 'delta_w', 'delta_s5_lambda_im': 'delta_w', 'delta_s5_log_dt': 'delta_w', 'delta_s5_b_re': 'delta_w', 'delta_s5_b_im': 'delta_w', 'delta_s5_c_re': 'delta_w', 'delta_s5_c_im': 'delta_w', 'delta_s5_d': 'delta_w', 'delta_s5_w_glu': 'delta_w', 'delta_s5_b_glu': 'delta_w', 'delta_w_branch_out': 'delta_w', 'delta_w_out': 'delta_w', 'new_m_norm_g': 'new_m', 'new_m_w_in': 'new_m', 'new_m_mla_q_a_norm': 'new_m', 'new_m_mla_w_q_up': 'new_m', 'new_m_mla_kv_a_norm': 'new_m', 'new_m_mla_w_kv_up': 'new_m', 'new_m_mla_q_norm': 'new_m', 'new_m_mla_k_norm': 'new_m', 'new_m_fox_b_f': 'new_m', 'new_m_fox_q_norm': 'new_m', 'new_m_fox_k_norm': 'new_m', 'new_m_s5_lambda_re': 'new_m', 'new_m_s5_lambda_im': 'new_m', 'new_m_s5_log_dt': 'new_m', 'new_m_s5_b_re': 'new_m', 'new_m_s5_b_im': 'new_m', 'new_m_s5_c_re': 'new_m', 'new_m_s5_c_im': 'new_m', 'new_m_s5_d': 'new_m', 'new_m_s5_w_glu': 'new_m', 'new_m_s5_b_glu': 'new_m', 'new_m_w_branch_out': 'new_m', 'new_m_w_out': 'new_m', 'new_v_norm_g': 'new_v', 'new_v_w_in': 'new_v', 'new_v_mla_q_a_norm': 'new_v', 'new_v_mla_w_q_up': 'new_v', 'new_v_mla_kv_a_norm': 'new_v', 'new_v_mla_w_kv_up': 'new_v', 'new_v_mla_q_norm': 'new_v', 'new_v_mla_k_norm': 'new_v', 'new_v_fox_b_f': 'new_v', 'new_v_fox_q_norm': 'new_v', 'new_v_fox_k_norm': 'new_v', 'new_v_s5_lambda_re': 'new_v', 'new_v_s5_lambda_im': 'new_v', 'new_v_s5_log_dt': 'new_v', 'new_v_s5_b_re': 'new_v', 'new_v_s5_b_im': 'new_v', 'new_v_s5_c_re': 'new_v', 'new_v_s5_c_im': 'new_v', 'new_v_s5_d': 'new_v', 'new_v_s5_w_glu': 'new_v', 'new_v_s5_b_glu': 'new_v', 'new_v_w_branch_out': 'new_v', 'new_v_w_out': 'new_v'}


def _forward(args):
    return _fwd_reference(*[args[k] for k in FWD_PARAMS])


def _output_shape():
    out = _jax.eval_shape(lambda: _forward(_fwd_setup_inputs(0)))
    return out.shape, out.dtype

N_MICROBATCH = 1
ADAM_LR = 0.001
ADAM_B1 = 0.9
ADAM_B2 = 0.999
ADAM_EPS = 1e-08
ADAM_WD = 0.01
ADAM_STEP = 10
PER_EXAMPLE_BATCH_AXIS = {'x': 0, 'positions': 0, 'loss_target': 0}
SHARED_INPUTS = []
_WEIGHT_DTYPES = {'norm_g': _jnp.float32, 'w_in': _jnp.float32, 'mla_q_a_norm': _jnp.float32, 'mla_w_q_up': _jnp.float32, 'mla_kv_a_norm': _jnp.float32, 'mla_w_kv_up': _jnp.float32, 'mla_q_norm': _jnp.float32, 'mla_k_norm': _jnp.float32, 'fox_b_f': _jnp.float32, 'fox_q_norm': _jnp.float32, 'fox_k_norm': _jnp.float32, 's5_lambda_re': _jnp.float32, 's5_lambda_im': _jnp.float32, 's5_log_dt': _jnp.float32, 's5_b_re': _jnp.float32, 's5_b_im': _jnp.float32, 's5_c_re': _jnp.float32, 's5_c_im': _jnp.float32, 's5_d': _jnp.float32, 's5_w_glu': _jnp.float32, 's5_b_glu': _jnp.float32, 'w_branch_out': _jnp.float32, 'w_out': _jnp.float32}
MOMENT_SCALE = {'norm_g': 1.581757e+00, 'w_in': 3.569430e-02, 'mla_q_a_norm': 1.903520e-02, 'mla_w_q_up': 1.136724e-02, 'mla_kv_a_norm': 1.225678e-01, 'mla_w_kv_up': 1.365945e-02, 'mla_q_norm': 1.344282e-01, 'mla_k_norm': 1.339718e-01, 'fox_b_f': 1.486458e+01, 'fox_q_norm': 1.727559e+00, 'fox_k_norm': 1.720855e+00, 's5_lambda_re': 1.868209e-03, 's5_lambda_im': 1.867337e-03, 's5_log_dt': 1.356594e+00, 's5_b_re': 1.332560e-03, 's5_b_im': 1.337195e-03, 's5_c_re': 2.676336e-03, 's5_c_im': 2.627203e-03, 's5_d': 4.800486e-01, 's5_w_glu': 9.398918e-02, 's5_b_glu': 2.874030e-01, 'w_branch_out': 3.002227e-02, 'w_out': 4.995096e-02}


def _to_microbatches(a, axis):
    t = _jnp.moveaxis(a, axis, 0)
    t = t.reshape((N_MICROBATCH, t.shape[0] // N_MICROBATCH) + t.shape[1:])
    return _jnp.moveaxis(t, 1, axis + 1)


def setup_inputs(seed: int = 0) -> dict:
    inp = _fwd_setup_inputs(seed)
    key = _jax.random.fold_in(_jax.random.key(seed), 7919)
    shape, _ = _output_shape()
    out = dict(inp)
    out["loss_target"] = _jax.random.normal(_jax.random.fold_in(key, 0), shape, _jnp.float32)
    for i, name in enumerate(TWIN_WEIGHTS):
        w = inp[name].astype(_jnp.float32)
        if MOMENT_SCALE is None:
            s = _jnp.sqrt(_jnp.mean(_jnp.square(w)) + 1e-30)
        else:
            s = MOMENT_SCALE[name]
        km, kv = _jax.random.split(_jax.random.fold_in(key, i + 1))
        out[name] = w
        out["m_" + name] = s * _jax.random.normal(km, w.shape, _jnp.float32)
        out["v_" + name] = (s * s) * _jax.random.uniform(kv, w.shape, _jnp.float32, 0.5, 1.5)
    if N_MICROBATCH > 1:
        for name, axis in PER_EXAMPLE_BATCH_AXIS.items():
            out[name] = _to_microbatches(out[name], axis)
    return {'x': out['x'], 'positions': out['positions'], 'norm_g': out['norm_g'], 'w_in': out['w_in'], 'mla_q_a_norm': out['mla_q_a_norm'], 'mla_w_q_up': out['mla_w_q_up'], 'mla_kv_a_norm': out['mla_kv_a_norm'], 'mla_w_kv_up': out['mla_w_kv_up'], 'mla_q_norm': out['mla_q_norm'], 'mla_k_norm': out['mla_k_norm'], 'fox_b_f': out['fox_b_f'], 'fox_q_norm': out['fox_q_norm'], 'fox_k_norm': out['fox_k_norm'], 's5_lambda_re': out['s5_lambda_re'], 's5_lambda_im': out['s5_lambda_im'], 's5_log_dt': out['s5_log_dt'], 's5_b_re': out['s5_b_re'], 's5_b_im': out['s5_b_im'], 's5_c_re': out['s5_c_re'], 's5_c_im': out['s5_c_im'], 's5_d': out['s5_d'], 's5_w_glu': out['s5_w_glu'], 's5_b_glu': out['s5_b_glu'], 'w_branch_out': out['w_branch_out'], 'w_out': out['w_out'], 'loss_target': out['loss_target'], 'm_norm_g': out['m_norm_g'], 'm_w_in': out['m_w_in'], 'm_mla_q_a_norm': out['m_mla_q_a_norm'], 'm_mla_w_q_up': out['m_mla_w_q_up'], 'm_mla_kv_a_norm': out['m_mla_kv_a_norm'], 'm_mla_w_kv_up': out['m_mla_w_kv_up'], 'm_mla_q_norm': out['m_mla_q_norm'], 'm_mla_k_norm': out['m_mla_k_norm'], 'm_fox_b_f': out['m_fox_b_f'], 'm_fox_q_norm': out['m_fox_q_norm'], 'm_fox_k_norm': out['m_fox_k_norm'], 'm_s5_lambda_re': out['m_s5_lambda_re'], 'm_s5_lambda_im': out['m_s5_lambda_im'], 'm_s5_log_dt': out['m_s5_log_dt'], 'm_s5_b_re': out['m_s5_b_re'], 'm_s5_b_im': out['m_s5_b_im'], 'm_s5_c_re': out['m_s5_c_re'], 'm_s5_c_im': out['m_s5_c_im'], 'm_s5_d': out['m_s5_d'], 'm_s5_w_glu': out['m_s5_w_glu'], 'm_s5_b_glu': out['m_s5_b_glu'], 'm_w_branch_out': out['m_w_branch_out'], 'm_w_out': out['m_w_out'], 'v_norm_g': out['v_norm_g'], 'v_w_in': out['v_w_in'], 'v_mla_q_a_norm': out['v_mla_q_a_norm'], 'v_mla_w_q_up': out['v_mla_w_q_up'], 'v_mla_kv_a_norm': out['v_mla_kv_a_norm'], 'v_mla_w_kv_up': out['v_mla_w_kv_up'], 'v_mla_q_norm': out['v_mla_q_norm'], 'v_mla_k_norm': out['v_mla_k_norm'], 'v_fox_b_f': out['v_fox_b_f'], 'v_fox_q_norm': out['v_fox_q_norm'], 'v_fox_k_norm': out['v_fox_k_norm'], 'v_s5_lambda_re': out['v_s5_lambda_re'], 'v_s5_lambda_im': out['v_s5_lambda_im'], 'v_s5_log_dt': out['v_s5_log_dt'], 'v_s5_b_re': out['v_s5_b_re'], 'v_s5_b_im': out['v_s5_b_im'], 'v_s5_c_re': out['v_s5_c_re'], 'v_s5_c_im': out['v_s5_c_im'], 'v_s5_d': out['v_s5_d'], 'v_s5_w_glu': out['v_s5_w_glu'], 'v_s5_b_glu': out['v_s5_b_glu'], 'v_w_branch_out': out['v_w_branch_out'], 'v_w_out': out['v_w_out']}


def _loss(weights, diff, rest, loss_target):
    with _jax.named_scope("forward"):
        args = {**rest, TWIN_DIFF_INPUT: diff, **{k: w.astype(_WEIGHT_DTYPES[k]) for k, w in weights.items()}}
        y = _forward(args)
    with _jax.named_scope("loss_head"):
        err = _jnp.square(y.astype(_jnp.float32) - loss_target)
        return 0.5 * _jnp.sum(_jnp.mean(err, axis=-1)) if err.ndim else 0.5 * err


def _adamw(w, g, m, v):
    m = ADAM_B1 * m + (1.0 - ADAM_B1) * g
    v = ADAM_B2 * v + (1.0 - ADAM_B2) * _jnp.square(g)
    m_hat = m / (1.0 - ADAM_B1 ** ADAM_STEP)
    v_hat = v / (1.0 - ADAM_B2 ** ADAM_STEP)
    delta = -ADAM_LR * (m_hat / (_jnp.sqrt(v_hat) + ADAM_EPS) + ADAM_WD * w)
    return delta, m, v


def reference(x, positions, norm_g, w_in, mla_q_a_norm, mla_w_q_up, mla_kv_a_norm, mla_w_kv_up, mla_q_norm, mla_k_norm, fox_b_f, fox_q_norm, fox_k_norm, s5_lambda_re, s5_lambda_im, s5_log_dt, s5_b_re, s5_b_im, s5_c_re, s5_c_im, s5_d, s5_w_glu, s5_b_glu, w_branch_out, w_out, loss_target, m_norm_g, m_w_in, m_mla_q_a_norm, m_mla_w_q_up, m_mla_kv_a_norm, m_mla_w_kv_up, m_mla_q_norm, m_mla_k_norm, m_fox_b_f, m_fox_q_norm, m_fox_k_norm, m_s5_lambda_re, m_s5_lambda_im, m_s5_log_dt, m_s5_b_re, m_s5_b_im, m_s5_c_re, m_s5_c_im, m_s5_d, m_s5_w_glu, m_s5_b_glu, m_w_branch_out, m_w_out, v_norm_g, v_w_in, v_mla_q_a_norm, v_mla_w_q_up, v_mla_kv_a_norm, v_mla_w_kv_up, v_mla_q_norm, v_mla_k_norm, v_fox_b_f, v_fox_q_norm, v_fox_k_norm, v_s5_lambda_re, v_s5_lambda_im, v_s5_log_dt, v_s5_b_re, v_s5_b_im, v_s5_c_re, v_s5_c_im, v_s5_d, v_s5_w_glu, v_s5_b_glu, v_w_branch_out, v_w_out):
    given = dict(x=x, positions=positions, norm_g=norm_g, w_in=w_in, mla_q_a_norm=mla_q_a_norm, mla_w_q_up=mla_w_q_up, mla_kv_a_norm=mla_kv_a_norm, mla_w_kv_up=mla_w_kv_up, mla_q_norm=mla_q_norm, mla_k_norm=mla_k_norm, fox_b_f=fox_b_f, fox_q_norm=fox_q_norm, fox_k_norm=fox_k_norm, s5_lambda_re=s5_lambda_re, s5_lambda_im=s5_lambda_im, s5_log_dt=s5_log_dt, s5_b_re=s5_b_re, s5_b_im=s5_b_im, s5_c_re=s5_c_re, s5_c_im=s5_c_im, s5_d=s5_d, s5_w_glu=s5_w_glu, s5_b_glu=s5_b_glu, w_branch_out=w_branch_out, w_out=w_out, loss_target=loss_target, m_norm_g=m_norm_g, m_w_in=m_w_in, m_mla_q_a_norm=m_mla_q_a_norm, m_mla_w_q_up=m_mla_w_q_up, m_mla_kv_a_norm=m_mla_kv_a_norm, m_mla_w_kv_up=m_mla_w_kv_up, m_mla_q_norm=m_mla_q_norm, m_mla_k_norm=m_mla_k_norm, m_fox_b_f=m_fox_b_f, m_fox_q_norm=m_fox_q_norm, m_fox_k_norm=m_fox_k_norm, m_s5_lambda_re=m_s5_lambda_re, m_s5_lambda_im=m_s5_lambda_im, m_s5_log_dt=m_s5_log_dt, m_s5_b_re=m_s5_b_re, m_s5_b_im=m_s5_b_im, m_s5_c_re=m_s5_c_re, m_s5_c_im=m_s5_c_im, m_s5_d=m_s5_d, m_s5_w_glu=m_s5_w_glu, m_s5_b_glu=m_s5_b_glu, m_w_branch_out=m_w_branch_out, m_w_out=m_w_out, v_norm_g=v_norm_g, v_w_in=v_w_in, v_mla_q_a_norm=v_mla_q_a_norm, v_mla_w_q_up=v_mla_w_q_up, v_mla_kv_a_norm=v_mla_kv_a_norm, v_mla_w_kv_up=v_mla_w_kv_up, v_mla_q_norm=v_mla_q_norm, v_mla_k_norm=v_mla_k_norm, v_fox_b_f=v_fox_b_f, v_fox_q_norm=v_fox_q_norm, v_fox_k_norm=v_fox_k_norm, v_s5_lambda_re=v_s5_lambda_re, v_s5_lambda_im=v_s5_lambda_im, v_s5_log_dt=v_s5_log_dt, v_s5_b_re=v_s5_b_re, v_s5_b_im=v_s5_b_im, v_s5_c_re=v_s5_c_re, v_s5_c_im=v_s5_c_im, v_s5_d=v_s5_d, v_s5_w_glu=v_s5_w_glu, v_s5_b_glu=v_s5_b_glu, v_w_branch_out=v_w_branch_out, v_w_out=v_w_out)
    weights = {n: given[n] for n in TWIN_WEIGHTS}
    shared = {n: given[n] for n in SHARED_INPUTS}
    per_example = {n: given[n] for n in ['x', 'positions']}
    grad_fn = _jax.value_and_grad(_loss, argnums=(0, 1))

    def one_microbatch(ex, loss_target):
        ex = dict(ex)
        diff = ex.pop(TWIN_DIFF_INPUT)
        return grad_fn(weights, diff, {**shared, **ex}, loss_target)

    if N_MICROBATCH == 1:
        loss, (grad_w, grad_x) = one_microbatch(per_example, given["loss_target"])
    else:
        def body(carry, xs):
            loss_sum, grad_sum = carry
            l_k, (gw_k, gx_k) = one_microbatch(xs[0], xs[1])
            with _jax.named_scope("update"):
                return (loss_sum + l_k, _jax.tree.map(_jnp.add, grad_sum, gw_k)), gx_k

        init = (_jnp.zeros((), _jnp.float32), _jax.tree.map(_jnp.zeros_like, weights))
        (loss, grad_w), grad_x = _jax.lax.scan(body, init, (per_example, given["loss_target"]))
    with _jax.named_scope("update"):
        delta_w, new_m, new_v = {}, {}, {}
        for n in TWIN_WEIGHTS:
            delta_w[n], new_m[n], new_v[n] = _adamw(weights[n], grad_w[n], given["m_" + n], given["v_" + n])
    return (loss, grad_x, *[grad_w[n] for n in TWIN_WEIGHTS], *[delta_w[n] for n in TWIN_WEIGHTS],
            *[new_m[n] for n in TWIN_WEIGHTS], *[new_v[n] for n in TWIN_WEIGHTS])
```

```python
import functools
import math

import jax
import jax.numpy as jnp
from jax import lax
from jax.experimental import pallas as pl
from jax.experimental.pallas import tpu as pltpu

f32 = jnp.float32
bf16 = jnp.bfloat16

D_MODEL = 1024
DEPTH = 2
EPS = 1e-6
HEADS = 8
MLA_QK = 96
MLA_Q_RANK = 256
MLA_KV_RANK = 128
ROPE = 32
ROPE_THETA = 10000.0
FOX_DIM = 64
S5_GROUPS = 32
S5_GROUP = 16
S5_STATE = 64
S5_LANES = S5_GROUPS * S5_STATE
LANE = 128
S5_BLOCKS = S5_LANES // LANE
IN_WIDTH = 7080
TOK = 256
VMEM_LIMIT = 56 * 1024 * 1024

ADAM_LR = 0.001
ADAM_B1 = 0.9
ADAM_B2 = 0.999
ADAM_EPS = 1e-08
ADAM_WD = 0.01
ADAM_STEP = 10

_ORIG = {}
_off = 0
for _n, _w in (("cq", 256), ("ckv", 128), ("kpe", 32), ("fq", 512), ("fk", 512), ("fv", 512), ("ff", 8), ("s5u", 512),
               ("g_mla", 512), ("g_fox", 512), ("g_s5", 512), ("m_mla", 1024), ("m_fox", 1024), ("m_s5", 1024)):
    _ORIG[_n] = (_off, _w)
    _off += _w
_PAD = {"m_mla": (0, 1024, 0), "m_fox": (1024, 1024, 0), "m_s5": (2048, 1024, 0),
        "fq": (3072, 512, 0), "fk": (3584, 512, 0), "fv": (4096, 512, 0), "s5u": (4608, 512, 0),
        "g_mla": (5120, 512, 0), "g_fox": (5632, 512, 0), "g_s5": (6144, 512, 0),
        "cq": (6656, 256, 0), "ckv": (6912, 128, 0), "kpe": (7040, 128, 64), "ff": (7168, 128, 0)}
NP = 7680
_PAD_ORDER = ("m_mla", "m_fox", "m_s5", "fq", "fk", "fv", "s5u", "g_mla", "g_fox", "g_s5", "cq", "ckv", "kpe", "ff")


def _seg(name):
    start, width, _ = _PAD[name]
    return width, start // width


def _nn(a, b):
    return lax.dot_general(a.astype(bf16), b.astype(bf16), (((1,), (0,)), ((), ())), preferred_element_type=f32)


def _nt(a, b):
    return lax.dot_general(a.astype(bf16), b.astype(bf16), (((1,), (1,)), ((), ())), preferred_element_type=f32)


def _tn(a, b):
    return lax.dot_general(a.astype(bf16), b.astype(bf16), (((0,), (0,)), ((), ())), preferred_element_type=f32)


def _rms(x, g, n):
    r = lax.rsqrt(jnp.sum(x * x, axis=-1, keepdims=True) * (1.0 / n) + EPS)
    return x * r * g, r


def _rms_bwd(dy, x, r, g, n):
    xh = x * r
    dg = jnp.sum(dy * xh, axis=0, keepdims=True)
    dxh = dy * g
    dx = r * (dxh - xh * (jnp.sum(dxh * xh, axis=-1, keepdims=True) * (1.0 / n)))
    return dx, dg


def _sigmoid(x):
    return 1.0 / (1.0 + jnp.exp(-x))


_GELU_C = math.sqrt(2.0 / math.pi)


def _gelu(x):
    t = jnp.tanh(_GELU_C * (x + 0.044715 * x * x * x))
    return 0.5 * x * (1.0 + t), t


def _gelu_grad(x, t):
    return 0.5 * (1.0 + t) + 0.5 * x * (1.0 - t * t) * _GELU_C * (1.0 + 3.0 * 0.044715 * x * x)


def _accumulate(ref, val):
    i = pl.program_id(0)

    @pl.when(i == 0)
    def _():
        ref[...] = val

    @pl.when(i > 0)
    def _():
        ref[...] += val


def _rope(x, c, s1, s2):
    return x * c + pltpu.roll(x, LANE - 16, 1) * s1 + pltpu.roll(x, 16, 1) * s2


def _rope_t(d, c, s1, s2):
    return d * c + pltpu.roll(d * s1, 16, 1) + pltpu.roll(d * s2, LANE - 16, 1)


def _const_map(ndim):
    return lambda *_: (0,) * ndim


def _rowwise(name, body, n_tok, tiled_in, full_in, tiled_out, acc_out, tile=TOK):
    in_specs, args = [], []
    for arr, width, blk in tiled_in:
        in_specs.append(pl.BlockSpec((tile, width), functools.partial(lambda i, b: (i, b), b=blk)))
        args.append(arr)
    for arr in full_in:
        in_specs.append(pl.BlockSpec(arr.shape, _const_map(arr.ndim)))
        args.append(arr)
    out_specs, out_shape = [], []
    for width, dt in tiled_out:
        out_specs.append(pl.BlockSpec((tile, width), lambda i: (i, 0)))
        out_shape.append(jax.ShapeDtypeStruct((n_tok, width), dt))
    for shape, dt in acc_out:
        out_specs.append(pl.BlockSpec(shape, _const_map(len(shape))))
        out_shape.append(jax.ShapeDtypeStruct(shape, dt))
    return pl.pallas_call(
        body, grid=(n_tok // tile,), in_specs=in_specs, out_specs=out_specs, out_shape=out_shape, name=name,
        compiler_params=pltpu.CompilerParams(dimension_semantics=("arbitrary",), vmem_limit_bytes=VMEM_LIMIT),
    )(*args)


def _mm(name, a, b, *, mode, grid, a_spec, b_spec, o_spec, out_shape, acc_shape, add=None, add_spec=None):
    nk = grid[2]

    def body(*refs):
        if add is None:
            a_ref, b_ref, o_ref, acc_ref = refs
        else:
            a_ref, b_ref, add_ref, o_ref, acc_ref = refs
        k = pl.program_id(2)

        @pl.when(k == 0)
        def _():
            acc_ref[...] = jnp.zeros_like(acc_ref)

        acc_ref[...] += {"nn": _nn, "nt": _nt, "tn": _tn}[mode](a_ref[...], b_ref[...])

        @pl.when(k == nk - 1)
        def _():
            r = acc_ref[...]
            if add is not None:
                r = r + add_ref[...]
            o_ref[...] = r.astype(o_ref.dtype)

    in_specs = [a_spec, b_spec] + ([add_spec] if add is not None else [])
    args = (a, b) + ((add,) if add is not None else ())
    return pl.pallas_call(
        body, grid=grid, in_specs=in_specs, out_specs=o_spec, out_shape=out_shape, name=name,
        scratch_shapes=[pltpu.VMEM(acc_shape, f32)],
        compiler_params=pltpu.CompilerParams(dimension_semantics=("arbitrary", "arbitrary", "arbitrary"), vmem_limit_bytes=VMEM_LIMIT),
    )(*args)


def _mm_nn(name, a, b, *, m, n, k, tm, tn, tk, out_dtype=f32, a_koff=0):
    return _mm(name, a, b, mode="nn", grid=(m // tm, n // tn, k // tk),
               a_spec=pl.BlockSpec((tm, tk), lambda i, j, kk: (i, kk + a_koff)),
               b_spec=pl.BlockSpec((tk, tn), lambda i, j, kk: (kk, j)),
               o_spec=pl.BlockSpec((tm, tn), lambda i, j, kk: (i, j)),
               out_shape=jax.ShapeDtypeStruct((m, n), out_dtype), acc_shape=(tm, tn))


def _mm_tn(name, a, b, *, m, n, k, tm, tn, tk, a_moff=0):
    return _mm(name, a, b, mode="tn", grid=(m // tm, n // tn, k // tk),
               a_spec=pl.BlockSpec((tk, tm), lambda i, j, kk: (kk, i + a_moff)),
               b_spec=pl.BlockSpec((tk, tn), lambda i, j, kk: (kk, j)),
               o_spec=pl.BlockSpec((tm, tn), lambda i, j, kk: (i, j)),
               out_shape=jax.ShapeDtypeStruct((m, n), f32), acc_shape=(tm, tn))


def _attn_common(mla):
    qw = 2 * LANE if mla else LANE
    scale = 1.0 / math.sqrt(MLA_QK if mla else FOX_DIM)
    return qw, scale


def _attn_fwd(name, q, k, v, bias_col, bias_row, *, mla, n_tok):
    qw, scale = _attn_common(mla)
    nq = n_tok // TOK
    has_bias = bias_col is not None

    def body(*refs):
        if has_bias:
            q_ref, k_ref, v_ref, bc_ref, br_ref, o_ref, lse_ref = refs
        else:
            q_ref, k_ref, v_ref, o_ref, lse_ref = refs
        p = pl.program_id(0)
        i = pl.program_id(1)
        lane = lax.broadcasted_iota(jnp.int32, (TOK, LANE), 1)
        qpos = i * TOK + lax.broadcasted_iota(jnp.int32, (TOK, TOK), 0)
        kiota = lax.broadcasted_iota(jnp.int32, (TOK, TOK), 1)
        o_tot = jnp.zeros((TOK, LANE), f32)
        lse_tot = jnp.zeros((TOK, LANE), f32)
        for e in (0, 1):
            half = (lane >= 64) if e else (lane < 64)
            if mla:
                qh = q_ref[:, e * LANE:(e + 1) * LANE]
            else:
                qh = jnp.where(half, q_ref[...], jnp.zeros((), bf16))
            h = 2 * p + e
            if has_bias:
                cq = jnp.sum(jnp.where(lane == h, bc_ref[...], 0.0), axis=-1, keepdims=True)

            def kv_step(j, carry, e=e, half=half, qh=qh, h=h, cq=cq if has_bias else None):
                m, l, acc = carry
                off = pl.multiple_of(j * TOK, TOK)
                if mla:
                    kj = k_ref[pl.ds(off, TOK), e * LANE:(e + 1) * LANE]
                else:
                    kj = k_ref[pl.ds(off, TOK), :]
                vj = jnp.where(half, v_ref[pl.ds(off, TOK), :], jnp.zeros((), bf16))
                s = _nt(qh, kj) * scale
                if has_bias:
                    s = s + (cq - br_ref[h, j])
                kpos = off + kiota
                allowed = ((kpos // 64) <= (qpos // 64)) if mla else (kpos <= qpos)
                s = jnp.where(allowed, s, -1e30)
                m_new = jnp.maximum(m, jnp.max(s, axis=-1, keepdims=True))
                alpha = jnp.exp(m - m_new)
                pe = jnp.exp(s - m_new)
                l = alpha * l + jnp.sum(pe, axis=-1, keepdims=True)
                acc = alpha * acc + _nn(pe, vj)
                return m_new, l, acc

            m, l, acc = lax.fori_loop(0, i + 1, kv_step,
                                      (jnp.full((TOK, 1), -1e30, f32), jnp.zeros((TOK, 1), f32), jnp.zeros((TOK, LANE), f32)))
            o_tot = o_tot + acc / l
            lse_tot = jnp.where(half, m + jnp.log(l), lse_tot)
        o_ref[...] = o_tot
        lse_ref[...] = lse_tot

    in_specs = [pl.BlockSpec((TOK, qw), lambda p, i: (i, p)),
                pl.BlockSpec((n_tok, qw), lambda p, i: (0, p)),
                pl.BlockSpec((n_tok, LANE), lambda p, i: (0, p))]
    args = [q, k, v]
    if has_bias:
        in_specs += [pl.BlockSpec((TOK, LANE), lambda p, i: (i, 0)), pl.BlockSpec(bias_row.shape, _const_map(4))]
        args += [bias_col, bias_row]
    return pl.pallas_call(
        body, grid=(4, nq), in_specs=in_specs,
        out_specs=[pl.BlockSpec((TOK, LANE), lambda p, i: (i, p)), pl.BlockSpec((TOK, LANE), lambda p, i: (i, p))],
        out_shape=[jax.ShapeDtypeStruct((n_tok, 512), f32), jax.ShapeDtypeStruct((n_tok, 512), f32)], name=name,
        compiler_params=pltpu.CompilerParams(dimension_semantics=("arbitrary", "arbitrary"), vmem_limit_bytes=VMEM_LIMIT),
    )(*args)


def _attn_bwd(name, q, k, v, o, lse, do, bias_col, bias_row, *, mla, n_tok):
    qw, scale = _attn_common(mla)
    nq = n_tok // TOK
    has_bias = bias_col is not None

    def body(*refs):
        if has_bias:
            q_ref, k_ref, v_ref, o_ref, lse_ref, do_ref, bc_ref, br_ref, dq_ref, dk_ref, dv_ref, dbc_ref, dbr_ref = refs
        else:
            q_ref, k_ref, v_ref, o_ref, lse_ref, do_ref, dq_ref, dk_ref, dv_ref = refs
        p = pl.program_id(0)
        i = pl.program_id(1)

        @pl.when(i == 0)
        def _():
            dk_ref[...] = jnp.zeros_like(dk_ref)
            dv_ref[...] = jnp.zeros_like(dv_ref)

        if has_bias:
            @pl.when(jnp.logical_and(i == 0, p == 0))
            def _():
                dbr_ref[...] = jnp.zeros_like(dbr_ref)

        lane = lax.broadcasted_iota(jnp.int32, (TOK, LANE), 1)
        qpos = i * TOK + lax.broadcasted_iota(jnp.int32, (TOK, TOK), 0)
        kiota = lax.broadcasted_iota(jnp.int32, (TOK, TOK), 1)
        do_blk = do_ref[...]
        prod = do_blk * o_ref[...]
        lse_blk = lse_ref[...]
        dq_tot = jnp.zeros((TOK, LANE), f32)
        dbc_tot = jnp.zeros((TOK, LANE), f32)
        for e in (0, 1):
            half = (lane >= 64) if e else (lane < 64)
            if mla:
                qh = q_ref[:, e * LANE:(e + 1) * LANE]
            else:
                qh = jnp.where(half, q_ref[...], jnp.zeros((), bf16))
            h = 2 * p + e
            delta = jnp.sum(jnp.where(half, prod, 0.0), axis=-1, keepdims=True)
            lse_h = lse_blk[:, 64 * e:64 * e + 1]
            do_h = jnp.where(half, do_blk, 0.0).astype(bf16)
            if has_bias:
                cq = jnp.sum(jnp.where(lane == h, bc_ref[...], 0.0), axis=-1, keepdims=True)

            def kv_step(j, carry, e=e, half=half, qh=qh, h=h, delta=delta, lse_h=lse_h, do_h=do_h,
                        cq=cq if has_bias else None):
                dq_acc, rs_acc = carry
                off = pl.multiple_of(j * TOK, TOK)
                if mla:
                    kj = k_ref[pl.ds(off, TOK), e * LANE:(e + 1) * LANE]
                else:
                    kj = k_ref[pl.ds(off, TOK), :]
                vj = jnp.where(half, v_ref[pl.ds(off, TOK), :], jnp.zeros((), bf16))
                s = _nt(qh, kj) * scale
                if has_bias:
                    s = s + (cq - br_ref[h, j])
                kpos = off + kiota
                allowed = ((kpos // 64) <= (qpos // 64)) if mla else (kpos <= qpos)
                pr = jnp.where(allowed, jnp.exp(s - lse_h), 0.0)
                dp = _nt(do_h, vj)
                ds = pr * (dp - delta)
                dq_acc = dq_acc + _nn(ds, kj) * scale
                dk_j = _tn(ds, qh) * scale
                if mla:
                    dk_ref[pl.ds(off, TOK), e * LANE:(e + 1) * LANE] += dk_j
                else:
                    dk_ref[pl.ds(off, TOK), :] += dk_j
                dv_ref[pl.ds(off, TOK), :] += _tn(pr, do_h)
                if has_bias:
                    rs_acc = rs_acc + jnp.sum(ds, axis=-1, keepdims=True)
                    dbr_ref[h, j] += jnp.broadcast_to(-jnp.sum(ds, axis=0, keepdims=True), (8, TOK))
                return dq_acc, rs_acc

            dq_acc, rs_acc = lax.fori_loop(0, i + 1, kv_step, (jnp.zeros((TOK, LANE), f32), jnp.zeros((TOK, 1), f32)))
            if mla:
                dq_ref[:, e * LANE:(e + 1) * LANE] = dq_acc
            else:
                dq_tot = dq_tot + jnp.where(half, dq_acc, 0.0)
            if has_bias:
                dbc_tot = jnp.where(half, rs_acc, dbc_tot)
        if not mla:
            dq_ref[...] = dq_tot
        if has_bias:
            dbc_ref[...] = dbc_tot

    tile_q = pl.BlockSpec((TOK, qw), lambda p, i: (i, p))
    tile_v = pl.BlockSpec((TOK, LANE), lambda p, i: (i, p))
    full_k = pl.BlockSpec((n_tok, qw), lambda p, i: (0, p))
    full_v = pl.BlockSpec((n_tok, LANE), lambda p, i: (0, p))
    in_specs = [tile_q, full_k, full_v, tile_v, tile_v, tile_v]
    args = [q, k, v, o, lse, do]
    out_specs = [tile_q, full_k, full_v]
    out_shape = [jax.ShapeDtypeStruct((n_tok, 4 * qw), f32), jax.ShapeDtypeStruct((n_tok, 4 * qw), f32),
                 jax.ShapeDtypeStruct((n_tok, 512), f32)]
    if has_bias:
        in_specs += [pl.BlockSpec((TOK, LANE), lambda p, i: (i, 0)), pl.BlockSpec(bias_row.shape, _const_map(4))]
        args += [bias_col, bias_row]
        out_specs += [tile_v, pl.BlockSpec((HEADS, nq, 8, TOK), _const_map(4))]
        out_shape += [jax.ShapeDtypeStruct((n_tok, 512), f32), jax.ShapeDtypeStruct((HEADS, nq, 8, TOK), f32)]
    return pl.pallas_call(
        body, grid=(4, nq), in_specs=in_specs, out_specs=out_specs, out_shape=out_shape, name=name,
        compiler_params=pltpu.CompilerParams(dimension_semantics=("arbitrary", "arbitrary"), vmem_limit_bytes=VMEM_LIMIT),
    )(*args)


def _s5_disc(lr, li, ldt):
    dt = jnp.exp(ldt)
    mag = jnp.exp(lr * dt)
    a_re = mag * jnp.cos(li * dt)
    a_im = mag * jnp.sin(li * dt)
    den = lr * lr + li * li
    f_re = ((a_re - 1.0) * lr + a_im * li) / den
    f_im = (a_im * lr - (a_re - 1.0) * li) / den
    return a_re, a_im, f_re, f_im


def _s5_param_fwd(lr, li, ldt, b_re, b_im):
    def body(lr_ref, li_ref, ldt_ref, br_ref, bi_ref, ar_ref, ai_ref, bbr_ref, bbi_ref):
        a_re, a_im, f_re, f_im = _s5_disc(lr_ref[...], li_ref[...], ldt_ref[...])
        ar_ref[...] = a_re
        ai_ref[...] = a_im
        br, bi = br_ref[...], bi_ref[...]
        bbr_ref[...] = f_re * br - f_im * bi
        bbi_ref[...] = f_re * bi + f_im * br

    col = jax.ShapeDtypeStruct((S5_LANES, 1), f32)
    mat = jax.ShapeDtypeStruct((S5_LANES, S5_GROUP), f32)
    return pl.pallas_call(body, out_shape=[col, col, mat, mat], name="s5_param_fwd")(lr, li, ldt, b_re, b_im)


def _s5_param_bwd(lr, li, ldt, b_re, b_im, da_re, da_im, dbb_re, dbb_im):
    def body(lr_ref, li_ref, ldt_ref, br_ref, bi_ref, dar_ref, dai_ref, gbr_ref, gbi_ref,
             dlr_ref, dli_ref, dldt_ref, dbr_ref, dbi_ref):
        (a_re, a_im, f_re, f_im), vjp = jax.vjp(_s5_disc, lr_ref[...], li_ref[...], ldt_ref[...])
        br, bi, gr, gi = br_ref[...], bi_ref[...], gbr_ref[...], gbi_ref[...]
        dbr_ref[...] = f_re * gr + f_im * gi
        dbi_ref[...] = f_re * gi - f_im * gr
        dfr = jnp.sum(br * gr + bi * gi, axis=-1, keepdims=True)
        dfi = jnp.sum(br * gi - bi * gr, axis=-1, keepdims=True)
        dlr, dli, dldt = vjp((dar_ref[...], dai_ref[...], dfr, dfi))
        dlr_ref[...] = dlr
        dli_ref[...] = dli
        dldt_ref[...] = jnp.sum(dldt.reshape(S5_GROUPS, S5_STATE, 1), axis=1)

    col = jax.ShapeDtypeStruct((S5_LANES, 1), f32)
    mat = jax.ShapeDtypeStruct((S5_LANES, S5_GROUP), f32)
    return pl.pallas_call(body, out_shape=[col, col, jax.ShapeDtypeStruct((S5_GROUPS, 1), f32), mat, mat],
                          name="s5_param_bwd")(lr, li, ldt, b_re, b_im, da_re, da_im, dbb_re, dbb_im)


_SCAN_NB = 2


def _s5_scan(name, bu, a_re8, a_im8, *, reverse, n_tok):
    rows = n_tok // 8
    nb = _SCAN_NB

    def body(bu_ref, ar_ref, ai_ref, x_ref):
        a_r = [ar_ref[b] for b in range(nb)]
        a_i = [ai_ref[b] for b in range(nb)]
        zero = jnp.zeros((8, LANE), f32)
        one = jnp.ones((8, LANE), f32)

        def rows_at(r):
            rr = (rows - 1 - r) if reverse else r
            return pl.ds(rr, 8, stride=rows)

        def pass1(r, carry):
            out = []
            sl = rows_at(r)
            for b in range(nb):
                xr, xi, mr, mi = carry[b]
                nr = a_r[b] * xr - a_i[b] * xi + bu_ref[0, b, sl, :]
                ni = a_r[b] * xi + a_i[b] * xr + bu_ref[1, b, sl, :]
                x_ref[0, b, sl, :] = nr
                x_ref[1, b, sl, :] = ni
                out.append((nr, ni, a_r[b] * mr - a_i[b] * mi, a_r[b] * mi + a_i[b] * mr))
            return tuple(out)

        carry = lax.fori_loop(0, rows, pass1, tuple((zero, zero, one, zero) for _ in range(nb)))
        sub = lax.broadcasted_iota(jnp.int32, (8, LANE), 0)
        feed = []
        for b in range(nb):
            lr_, li_, pr, pi = carry[b]
            fr, fi = zero, zero
            for _ in range(7):
                tr = lr_ + pr * fr - pi * fi
                ti = li_ + pr * fi + pi * fr
                if reverse:
                    fr = jnp.where(sub < 7, pltpu.roll(tr, 7, 0), 0.0)
                    fi = jnp.where(sub < 7, pltpu.roll(ti, 7, 0), 0.0)
                else:
                    fr = jnp.where(sub > 0, pltpu.roll(tr, 1, 0), 0.0)
                    fi = jnp.where(sub > 0, pltpu.roll(ti, 1, 0), 0.0)
            feed.append((fr, fi))

        def pass2(r, carry):
            out = []
            sl = rows_at(r)
            for b in range(nb):
                mr, mi = carry[b]
                fr, fi = feed[b]
                x_ref[0, b, sl, :] += mr * fr - mi * fi
                x_ref[1, b, sl, :] += mr * fi + mi * fr
                out.append((a_r[b] * mr - a_i[b] * mi, a_r[b] * mi + a_i[b] * mr))
            return tuple(out)

        lax.fori_loop(0, rows, pass2, tuple((a_r[b], a_i[b]) for b in range(nb)))

    blk = pl.BlockSpec((2, nb, n_tok, LANE), lambda g: (0, g, 0, 0))
    ablk = pl.BlockSpec((nb, 8, LANE), lambda g: (g, 0, 0))
    return pl.pallas_call(
        body, grid=(S5_BLOCKS // nb,), in_specs=[blk, ablk, ablk], out_specs=blk,
        out_shape=jax.ShapeDtypeStruct((2, S5_BLOCKS, n_tok, LANE), f32), name=name,
        compiler_params=pltpu.CompilerParams(dimension_semantics=("arbitrary",), vmem_limit_bytes=VMEM_LIMIT),
    )(bu, a_re8, a_im8)


def _s5_da(xs, gx, *, n_tok):
    def body(x_ref, g_ref, o_ref):
        t = lax.broadcasted_iota(jnp.int32, (n_tok, LANE), 0)
        xr = jnp.where(t >= 1, pltpu.roll(x_ref[0, 0], 1, 0), 0.0)
        xi = jnp.where(t >= 1, pltpu.roll(x_ref[1, 0], 1, 0), 0.0)
        gr, gi = g_ref[0, 0], g_ref[1, 0]
        o_ref[0, 0:1, :] = jnp.sum(xr * gr + xi * gi, axis=0, keepdims=True)
        o_ref[0, 1:2, :] = jnp.sum(xr * gi - xi * gr, axis=0, keepdims=True)

    blk = pl.BlockSpec((2, 1, n_tok, LANE), lambda g: (0, g, 0, 0))
    return pl.pallas_call(
        body, grid=(S5_BLOCKS,), in_specs=[blk, blk], out_specs=pl.BlockSpec((1, 2, LANE), lambda g: (g, 0, 0)),
        out_shape=jax.ShapeDtypeStruct((S5_BLOCKS, 2, LANE), f32), name="s5_da",
        compiler_params=pltpu.CompilerParams(dimension_semantics=("arbitrary",), vmem_limit_bytes=VMEM_LIMIT),
    )(xs, gx)


S5_Q = 4


def _bd8(t):
    _, a, b = t.shape
    t = t.reshape(S5_Q, 8, a, 1, b)
    eye = jnp.eye(8, dtype=jnp.bool_).reshape(1, 8, 1, 8, 1)
    return jnp.where(eye, jnp.broadcast_to(t, (S5_Q, 8, a, 8, b)), jnp.zeros((), t.dtype)).reshape(S5_Q, 8 * a, 8 * b)


def _bd8_diag(m, a, b):
    m = m.reshape(S5_Q, 8, a, 8, b)
    eye = jnp.eye(8, dtype=jnp.bool_).reshape(1, 8, 1, 8, 1)
    return jnp.sum(jnp.where(eye, m, 0.0), axis=3).reshape(S5_GROUPS, a, b)


def _s5_expand(name, a, a_blk0, wq, *, n_tok):
    def body(a_ref, w_ref, o_ref):
        r = _nn(a_ref[...], w_ref[...])
        for k in range(4):
            o_ref[k] = r[:, k * LANE:(k + 1) * LANE]

    return pl.pallas_call(
        body, grid=(2, S5_Q),
        in_specs=[pl.BlockSpec((n_tok, LANE), lambda ri, q: (0, a_blk0 + q)),
                  pl.BlockSpec((None, None, LANE, 512), lambda ri, q: (ri, q, 0, 0))],
        out_specs=pl.BlockSpec((None, 4, n_tok, LANE), lambda ri, q: (ri, q, 0, 0)),
        out_shape=jax.ShapeDtypeStruct((2, S5_BLOCKS, n_tok, LANE), f32), name=name,
        compiler_params=pltpu.CompilerParams(dimension_semantics=("arbitrary", "arbitrary"), vmem_limit_bytes=VMEM_LIMIT),
    )(a, wq)


def _s5_contract(name, xs, wq, add, out_dtype, *, n_tok):
    def body(*refs):
        if add is None:
            x_ref, w_ref, o_ref, acc_ref = refs
        else:
            x_ref, w_ref, add_ref, o_ref, acc_ref = refs
        ri = pl.program_id(1)
        r = _nn(x_ref[0], w_ref[0:LANE, :])
        for k in range(1, 4):
            r = r + _nn(x_ref[k], w_ref[k * LANE:(k + 1) * LANE, :])

        @pl.when(ri == 0)
        def _():
            acc_ref[...] = r

        @pl.when(ri == 1)
        def _():
            tot = acc_ref[...] + r
            if add is not None:
                tot = tot + add_ref[...]
            o_ref[...] = tot.astype(o_ref.dtype)

    col = pl.BlockSpec((n_tok, LANE), lambda q, ri: (0, q))
    in_specs = [pl.BlockSpec((None, 4, n_tok, LANE), lambda q, ri: (ri, q, 0, 0)),
                pl.BlockSpec((None, None, 512, LANE), lambda q, ri: (ri, q, 0, 0))]
    args = [xs, wq]
    if add is not None:
        in_specs.append(col)
        args.append(add)
    return pl.pallas_call(
        body, grid=(S5_Q, 2), in_specs=in_specs, out_specs=col, out_shape=jax.ShapeDtypeStruct((n_tok, 512), out_dtype), name=name,
        scratch_shapes=[pltpu.VMEM((n_tok, LANE), f32)],
        compiler_params=pltpu.CompilerParams(dimension_semantics=("arbitrary", "arbitrary"), vmem_limit_bytes=VMEM_LIMIT),
    )(*args)


def _s5_wgrad_states(name, xs, d, *, n_tok):
    def body(x_ref, d_ref, o_ref):
        for k in range(4):
            o_ref[k * LANE:(k + 1) * LANE, :] = _tn(x_ref[k], d_ref[...])

    return pl.pallas_call(
        body, grid=(2, S5_Q),
        in_specs=[pl.BlockSpec((None, 4, n_tok, LANE), lambda ri, q: (ri, q, 0, 0)), pl.BlockSpec((n_tok, LANE), lambda ri, q: (0, q))],
        out_specs=pl.BlockSpec((None, None, 512, LANE), lambda ri, q: (ri, q, 0, 0)),
        out_shape=jax.ShapeDtypeStruct((2, S5_Q, 512, LANE), f32), name=name,
        compiler_params=pltpu.CompilerParams(dimension_semantics=("arbitrary", "arbitrary"), vmem_limit_bytes=VMEM_LIMIT),
    )(xs, d)


def _s5_wgrad_channels(name, a, a_blk0, gx, *, n_tok):
    def body(a_ref, g_ref, o_ref):
        for k in range(4):
            o_ref[:, k * LANE:(k + 1) * LANE] = _tn(a_ref[...], g_ref[k])

    return pl.pallas_call(
        body, grid=(2, S5_Q),
        in_specs=[pl.BlockSpec((n_tok, LANE), lambda ri, q: (0, a_blk0 + q)), pl.BlockSpec((None, 4, n_tok, LANE), lambda ri, q: (ri, q, 0, 0))],
        out_specs=pl.BlockSpec((None, None, LANE, 512), lambda ri, q: (ri, q, 0, 0)),
        out_shape=jax.ShapeDtypeStruct((2, S5_Q, LANE, 512), f32), name=name,
        compiler_params=pltpu.CompilerParams(dimension_semantics=("arbitrary", "arbitrary"), vmem_limit_bytes=VMEM_LIMIT),
    )(a, gx)


def _pad_w_in(w):
    pieces, pos = [], 0
    for name in _PAD_ORDER:
        start, width, inner = _PAD[name]
        o0, ow = _ORIG[name]
        if start + inner > pos:
            pieces.append(jnp.zeros((w.shape[0], start + inner - pos), w.dtype))
        pieces.append(w[:, o0:o0 + ow])
        pos = start + inner + ow
    pieces.append(jnp.zeros((w.shape[0], NP - pos), w.dtype))
    return jnp.concatenate(pieces, axis=1)


def _unpad_w_in(wp):
    out = []
    for name, (o0, ow) in _ORIG.items():
        start, _, inner = _PAD[name]
        out.append(wp[:, start + inner:start + inner + ow])
    return jnp.concatenate(out, axis=1)


def _layer_weights(l, small, big):
    w = {}
    w_in = _pad_w_in(big["w_in"][l])
    w["w_in"] = w_in
    wq = big["mla_w_q_up"][l].reshape(MLA_Q_RANK, HEADS, MLA_QK)
    w["wq"] = jnp.pad(wq, ((0, 0), (0, 0), (0, LANE - MLA_QK))).reshape(MLA_Q_RANK, HEADS * LANE)
    wkv = big["mla_w_kv_up"][l].reshape(MLA_KV_RANK, HEADS, 128)
    wk = jnp.pad(wkv[:, :, :64], ((0, 0), (0, 0), (0, 64))).reshape(MLA_KV_RANK, HEADS * LANE)
    wv = wkv[:, :, 64:].reshape(MLA_KV_RANK, 512)
    w["wkv"] = jnp.concatenate([wk, wv], axis=1)
    w["w_glu"] = big["s5_w_glu"][l]
    w["wo"] = big["w_branch_out"][l]
    w["w_out"] = big["w_out"][l]
    row = lambda a: a.reshape(1, -1).astype(f32)
    w["norm_g"] = row(small["norm_g"][l])
    w["qa_g"] = row(small["mla_q_a_norm"][l])
    w["kva_g"] = row(small["mla_kv_a_norm"][l])
    w["qn_g"] = jnp.pad(row(small["mla_q_norm"][l]), ((0, 0), (0, LANE - MLA_QK)))
    w["kn_g"] = jnp.pad(row(small["mla_k_norm"][l]), ((0, 0), (0, LANE - MLA_QK)))
    w["fq_g"] = jnp.tile(row(small["fox_q_norm"][l]), (1, 2))
    w["fk_g"] = jnp.tile(row(small["fox_k_norm"][l]), (1, 2))
    w["b_f"] = jnp.pad(row(small["fox_b_f"][l]), ((0, 0), (0, LANE - HEADS)))
    w["lr"] = small["s5_lambda_re"][l].reshape(S5_LANES, 1)
    w["li"] = small["s5_lambda_im"][l].reshape(S5_LANES, 1)
    w["ldt"] = jnp.repeat(small["s5_log_dt"][l], S5_STATE).reshape(S5_LANES, 1)
    w["b_re"] = small["s5_b_re"][l].reshape(S5_LANES, S5_GROUP)
    w["b_im"] = small["s5_b_im"][l].reshape(S5_LANES, S5_GROUP)
    w["c_re"] = small["s5_c_re"][l]
    w["c_im"] = small["s5_c_im"][l]
    w["s5_d"] = row(small["s5_d"][l])
    w["b_glu"] = row(small["s5_b_glu"][l])
    return w


def _fox_halves(x, lane):
    sq = x * x
    lo = jnp.sum(jnp.where(lane < 64, sq, 0.0), axis=-1, keepdims=True)
    hi = jnp.sum(sq, axis=-1, keepdims=True) - lo
    return jnp.where(lane < 64, lax.rsqrt(lo * (1.0 / 64) + EPS), lax.rsqrt(hi * (1.0 / 64) + EPS))


def _fox_halves_bwd(dy, x, r, g, lane):
    xh = x * r
    dxh = dy * g
    pr = dxh * xh
    lo = jnp.sum(jnp.where(lane < 64, pr, 0.0), axis=-1, keepdims=True)
    hi = jnp.sum(pr, axis=-1, keepdims=True) - lo
    mean = jnp.where(lane < 64, lo, hi) * (1.0 / 64)
    return r * (dxh - xh * mean), jnp.sum(dy * xh, axis=0, keepdims=True)


def _mla_recompute(cq, ckv, kpe, c, s1, s2, qa_g, kva_g, wq, wkv):
    cqn, r_cq = _rms(cq, qa_g, MLA_Q_RANK)
    ckvn, r_ckv = _rms(ckv, kva_g, MLA_KV_RANK)
    cqn_b = cqn.astype(bf16)
    ckvn_b = ckvn.astype(bf16)
    q_raw = _nn(cqn_b, wq)
    kv_raw = _nn(ckvn_b, wkv)
    kpe_rot = _rope(kpe, c, s1, s2)
    return cqn_b, r_cq, ckvn_b, r_ckv, q_raw, kv_raw, kpe_rot


def _layer_fwd(x, w, rope_tabs, n_tok):
    c_tab, s1_tab, s2_tab = rope_tabs
    saved = {"x": x}

    def norm_body(x_ref, g_ref, h_ref):
        h_ref[...] = _rms(x_ref[...], g_ref[...], D_MODEL)[0].astype(bf16)

    (h,) = _rowwise("norm_fwd", norm_body, n_tok, [(x, D_MODEL, 0)], [w["norm_g"]], [(D_MODEL, bf16)], [])
    proj = _mm_nn("in_proj", h, w["w_in"], m=n_tok, n=NP, k=D_MODEL, tm=n_tok, tn=512, tk=D_MODEL)
    saved["h"], saved["proj"] = h, proj

    def mla_prep_body(cq_ref, ckv_ref, kpe_ref, c_ref, s1_ref, s2_ref, qa_ref, kva_ref, wq_ref, wkv_ref, qn_g_ref, kn_g_ref,
                      qn_ref, kn_ref, v_ref):
        c, s1, s2 = c_ref[...], s1_ref[...], s2_ref[...]
        _, _, _, _, q_raw, kv_raw, kpe_rot = _mla_recompute(cq_ref[...], ckv_ref[...], kpe_ref[...], c, s1, s2,
                                                            qa_ref[...], kva_ref[...], wq_ref[...], wkv_ref[...])
        for hd in range(HEADS):
            sl = slice(hd * LANE, (hd + 1) * LANE)
            qn_ref[:, sl] = _rms(_rope(q_raw[:, sl], c, s1, s2), qn_g_ref[...], MLA_QK)[0].astype(bf16)
            kn_ref[:, sl] = _rms(kv_raw[:, sl] + kpe_rot, kn_g_ref[...], MLA_QK)[0].astype(bf16)
        v_ref[...] = kv_raw[:, HEADS * LANE:].astype(bf16)

    qn, kn, v_mla = _rowwise(
        "mla_prep", mla_prep_body, n_tok,
        [(proj, *_seg("cq")), (proj, *_seg("ckv")), (proj, *_seg("kpe")), (c_tab, LANE, 0), (s1_tab, LANE, 0), (s2_tab, LANE, 0)],
        [w["qa_g"], w["kva_g"], w["wq"], w["wkv"], w["qn_g"], w["kn_g"]],
        [(HEADS * LANE, bf16), (HEADS * LANE, bf16), (512, bf16)], [])
    y_mla, lse_mla = _attn_fwd("mla_attn_fwd", qn, kn, v_mla, None, None, mla=True, n_tok=n_tok)
    saved.update(qn=qn, kn=kn, v_mla=v_mla, y_mla=y_mla, lse_mla=lse_mla)

    def fox_prep_body(fq_ref, fk_ref, fv_ref, ff_ref, qg_ref, kg_ref, bf_ref, fqn_ref, fkn_ref, fvb_ref, logf_ref):
        lane = lax.broadcasted_iota(jnp.int32, (TOK, LANE), 1)
        for blk in range(4):
            sl = slice(blk * LANE, (blk + 1) * LANE)
            xq = fq_ref[:, sl]
            fqn_ref[:, sl] = (xq * _fox_halves(xq, lane) * qg_ref[...]).astype(bf16)
            xk = fk_ref[:, sl]
            fkn_ref[:, sl] = (xk * _fox_halves(xk, lane) * kg_ref[...]).astype(bf16)
        fvb_ref[...] = fv_ref[...].astype(bf16)
        z = ff_ref[...] + bf_ref[...]
        logf_ref[...] = jnp.minimum(z, 0.0) - jnp.log(1.0 + jnp.exp(-jnp.abs(z)))

    fqn, fkn, fvb, logf = _rowwise(
        "fox_prep", fox_prep_body, n_tok,
        [(proj, *_seg("fq")), (proj, *_seg("fk")), (proj, *_seg("fv")), (proj, *_seg("ff"))],
        [w["fq_g"], w["fk_g"], w["b_f"]],
        [(512, bf16), (512, bf16), (512, bf16), (LANE, f32)], [])

    def cum_body(x_ref, cum_ref, cum_t_ref):
        x = x_ref[...]
        t = lax.broadcasted_iota(jnp.int32, x.shape, 0)
        s = 1
        while s < n_tok:
            x = x + jnp.where(t >= s, pltpu.roll(x, s, 0), 0.0)
            s *= 2
        cum_ref[...] = x
        cum_t_ref[...] = x.T[0:HEADS, :]

    cum, cum_t = pl.pallas_call(cum_body, out_shape=[jax.ShapeDtypeStruct((n_tok, LANE), f32), jax.ShapeDtypeStruct((HEADS, n_tok), f32)],
                                name="fox_cum")(logf)
    cum_row = cum_t.reshape(HEADS, n_tok // TOK, 1, TOK)
    y_fox, lse_fox = _attn_fwd("fox_attn_fwd", fqn, fkn, fvb, cum, cum_row, mla=False, n_tok=n_tok)
    saved.update(fqn=fqn, fkn=fkn, fvb=fvb, cum=cum, cum_row=cum_row, y_fox=y_fox, lse_fox=lse_fox)

    a_re, a_im, bb_re, bb_im = _s5_param_fwd(w["lr"], w["li"], w["ldt"], w["b_re"], w["b_im"])
    per_group = lambda m: m.reshape(S5_GROUPS, S5_STATE, S5_GROUP)
    b_cn = jnp.stack([_bd8(jnp.swapaxes(per_group(bb_re), 1, 2)), _bd8(jnp.swapaxes(per_group(bb_im), 1, 2))]).astype(bf16)
    b_nc = jnp.stack([_bd8(per_group(bb_re)), _bd8(per_group(bb_im))]).astype(bf16)
    c_nc = jnp.stack([_bd8(jnp.swapaxes(w["c_re"], 1, 2)), -_bd8(jnp.swapaxes(w["c_im"], 1, 2))]).astype(bf16)
    c_cn = jnp.stack([_bd8(w["c_re"]), -_bd8(w["c_im"])]).astype(bf16)
    a_re8 = jnp.broadcast_to(a_re.reshape(S5_BLOCKS, 1, LANE), (S5_BLOCKS, 8, LANE))
    a_im8 = jnp.broadcast_to(a_im.reshape(S5_BLOCKS, 1, LANE), (S5_BLOCKS, 8, LANE))
    u_w, u_blk = _seg("s5u")
    u_blk128 = u_blk * (u_w // LANE)
    bu = _s5_expand("s5_bu", proj, u_blk128, b_cn, n_tok=n_tok)
    xs = _s5_scan("s5_scan_fwd", bu, a_re8, a_im8, reverse=False, n_tok=n_tok)
    ylin = _s5_contract("s5_y", xs, c_nc, None, f32, n_tok=n_tok)

    def s5_post_body(yl_ref, u_ref, d_ref, wg_ref, bg_ref, out_ref):
        y = yl_ref[...] + d_ref[...] * u_ref[...]
        z, _ = _gelu(y)
        out_ref[...] = z * _sigmoid(_nn(z, wg_ref[...]) + bg_ref[...])

    (y_s5,) = _rowwise("s5_post", s5_post_body, n_tok, [(ylin, 512, 0), (proj, u_w, u_blk)],
                       [w["s5_d"], w["w_glu"], w["b_glu"]], [(512, f32)], [])
    saved.update(xs=xs, ylin=ylin, y_s5=y_s5, b_nc=b_nc, c_cn=c_cn, a_re8=a_re8, a_im8=a_im8)

    def merge_body(ym_ref, yf_ref, ys_ref, gm_ref, gf_ref, gs_ref, mm_ref, mf_ref, ms_ref, x_ref, wo_ref, wout_ref, out_ref):
        merged = jnp.zeros((TOK, D_MODEL), f32)
        for b, (y_ref, g_ref, m_ref) in enumerate(((ym_ref, gm_ref, mm_ref), (yf_ref, gf_ref, mf_ref), (ys_ref, gs_ref, ms_ref))):
            g = g_ref[...]
            a = y_ref[...] * (g * _sigmoid(g))
            merged = merged + _sigmoid(m_ref[...]) * _nn(a, wo_ref[b * 512:(b + 1) * 512, :])
        out_ref[...] = x_ref[...] + _nn(merged, wout_ref[...])

    (out,) = _rowwise(
        "merge_fwd", merge_body, n_tok,
        [(y_mla, 512, 0), (y_fox, 512, 0), (y_s5, 512, 0), (proj, *_seg("g_mla")), (proj, *_seg("g_fox")), (proj, *_seg("g_s5")),
         (proj, *_seg("m_mla")), (proj, *_seg("m_fox")), (proj, *_seg("m_s5")), (x, D_MODEL, 0)],
        [w["wo"], w["w_out"]], [(D_MODEL, f32)], [])
    return out, saved


def _layer_bwd(dout, w, sv, rope_tabs, n_tok):
    c_tab, s1_tab, s2_tab = rope_tabs
    proj, x = sv["proj"], sv["x"]
    grads = {}

    def merge_bwd_body(ym_ref, yf_ref, ys_ref, gm_ref, gf_ref, gs_ref, mm_ref, mf_ref, ms_ref, do_ref, wo_ref, wout_ref,
                       dym_ref, dyf_ref, dys_ref, dgm_ref, dgf_ref, dgs_ref, dmm_ref, dmf_ref, dms_ref, dwo_ref, dwout_ref):
        do = do_ref[...]
        branches = ((ym_ref, gm_ref, mm_ref, dym_ref, dgm_ref, dmm_ref), (yf_ref, gf_ref, mf_ref, dyf_ref, dgf_ref, dmf_ref),
                    (ys_ref, gs_ref, ms_ref, dys_ref, dgs_ref, dms_ref))
        acts, outs, sigs = [], [], []
        merged = jnp.zeros((TOK, D_MODEL), f32)
        for b, (y_ref, g_ref, m_ref, _, _, _) in enumerate(branches):
            g = g_ref[...]
            a = (y_ref[...] * (g * _sigmoid(g))).astype(bf16)
            o = _nn(a, wo_ref[b * 512:(b + 1) * 512, :])
            s = _sigmoid(m_ref[...])
            merged = merged + s * o
            acts.append(a)
            outs.append(o)
            sigs.append(s)
        dmerged = _nt(do, wout_ref[...])
        _accumulate(dwout_ref, _tn(merged, do))
        dwo = []
        for b, (y_ref, g_ref, m_ref, dy_ref, dg_ref, dm_ref) in enumerate(branches):
            s, o = sigs[b], outs[b]
            dm_ref[...] = (dmerged * o * s * (1.0 - s)).astype(bf16)
            d_o = dmerged * s
            da = _nt(d_o, wo_ref[b * 512:(b + 1) * 512, :])
            dwo.append(_tn(acts[b], d_o))
            g = g_ref[...]
            sg = _sigmoid(g)
            dy_ref[...] = da * (g * sg)
            dg_ref[...] = (da * y_ref[...] * (sg * (1.0 + g * (1.0 - sg)))).astype(bf16)
        _accumulate(dwo_ref, jnp.concatenate(dwo, axis=0))

    (dy_mla, dy_fox, dy_s5, dg_mla, dg_fox, dg_s5, dm_mla, dm_fox, dm_s5, dwo, dwout) = _rowwise(
        "merge_bwd", merge_bwd_body, n_tok,
        [(sv["y_mla"], 512, 0), (sv["y_fox"], 512, 0), (sv["y_s5"], 512, 0), (proj, *_seg("g_mla")), (proj, *_seg("g_fox")),
         (proj, *_seg("g_s5")), (proj, *_seg("m_mla")), (proj, *_seg("m_fox")), (proj, *_seg("m_s5")), (dout, D_MODEL, 0)],
        [w["wo"], w["w_out"]],
        [(512, f32)] * 3 + [(512, bf16)] * 3 + [(D_MODEL, bf16)] * 3, [((1536, D_MODEL), f32), ((D_MODEL, D_MODEL), f32)])
    grads["w_branch_out"], grads["w_out"] = dwo, dwout

    u_w, u_blk = _seg("s5u")

    def s5_post_bwd_body(yl_ref, u_ref, do_ref, d_ref, wg_ref, bg_ref, dyl_ref, dus_ref, dd_ref, dwg_ref, dbg_ref):
        u = u_ref[...]
        y = yl_ref[...] + d_ref[...] * u
        z, t = _gelu(y)
        s = _sigmoid(_nn(z, wg_ref[...]) + bg_ref[...])
        do = do_ref[...]
        dgl = do * z * s * (1.0 - s)
        dz = do * s + _nt(dgl, wg_ref[...])
        dy = dz * _gelu_grad(y, t)
        dyl_ref[...] = dy.astype(bf16)
        dus_ref[...] = dy * d_ref[...]
        _accumulate(dd_ref, jnp.sum(dy * u, axis=0, keepdims=True))
        _accumulate(dwg_ref, _tn(z, dgl))
        _accumulate(dbg_ref, jnp.sum(dgl, axis=0, keepdims=True))

    dylin, du_skip, dd, dwglu, dbglu = _rowwise(
        "s5_post_bwd", s5_post_bwd_body, n_tok, [(sv["ylin"], 512, 0), (proj, u_w, u_blk), (dy_s5, 512, 0)],
        [w["s5_d"], w["w_glu"], w["b_glu"]], [(512, bf16), (512, f32)], [((1, 512), f32), ((512, 512), f32), ((1, 512), f32)])
    grads["s5_d"], grads["s5_w_glu"], grads["s5_b_glu"] = dd.reshape(512), dwglu, dbglu.reshape(512)

    dxs = _s5_expand("s5_dxs", dylin, 0, sv["c_cn"], n_tok=n_tok)
    dc_nc = _s5_wgrad_states("s5_dc", sv["xs"], dylin, n_tok=n_tok)
    gx = _s5_scan("s5_scan_bwd", dxs, sv["a_re8"], -sv["a_im8"], reverse=True, n_tok=n_tok)
    da = _s5_da(sv["xs"], gx, n_tok=n_tok)
    ds5u = _s5_contract("s5_du", gx, sv["b_nc"], du_skip, bf16, n_tok=n_tok)
    db_cn = _s5_wgrad_channels("s5_db", proj, u_blk * (u_w // LANE), gx, n_tok=n_tok)
    diag_b = lambda m: jnp.swapaxes(_bd8_diag(m, S5_GROUP, S5_STATE), 1, 2).reshape(S5_LANES, S5_GROUP)
    diag_c = lambda m: jnp.swapaxes(_bd8_diag(m, S5_STATE, S5_GROUP), 1, 2)
    dlr, dli, dldt, db_re, db_im = _s5_param_bwd(
        w["lr"], w["li"], w["ldt"], w["b_re"], w["b_im"], da[:, 0, :].reshape(S5_LANES, 1), da[:, 1, :].reshape(S5_LANES, 1),
        diag_b(db_cn[0]), diag_b(db_cn[1]))
    grads["s5_lambda_re"] = dlr.reshape(S5_GROUPS, S5_STATE)
    grads["s5_lambda_im"] = dli.reshape(S5_GROUPS, S5_STATE)
    grads["s5_log_dt"] = dldt.reshape(S5_GROUPS)
    grads["s5_b_re"] = db_re.reshape(S5_GROUPS, S5_STATE, S5_GROUP)
    grads["s5_b_im"] = db_im.reshape(S5_GROUPS, S5_STATE, S5_GROUP)
    grads["s5_c_re"] = diag_c(dc_nc[0])
    grads["s5_c_im"] = -diag_c(dc_nc[1])

    dfqn, dfkn, dfv, dbc, dbr = _attn_bwd("fox_attn_bwd", sv["fqn"], sv["fkn"], sv["fvb"], sv["y_fox"], sv["lse_fox"], dy_fox,
                                          sv["cum"], sv["cum_row"], mla=False, n_tok=n_tok)
    dcq8 = jnp.pad(dbc[:, 0::64], ((0, 0), (0, LANE - HEADS)))
    dck8 = jnp.pad(dbr[:, :, 0, :].reshape(HEADS, n_tok).T, ((0, 0), (0, LANE - HEADS)))

    def fox_gate_bwd_body(dq_ref, dk_ref, ff_ref, bf_ref, dff_ref, dbf_ref):
        xg = dq_ref[...] + dk_ref[...]
        t = lax.broadcasted_iota(jnp.int32, xg.shape, 0)
        s = 1
        while s < n_tok:
            xg = xg + jnp.where(t < n_tok - s, pltpu.roll(xg, n_tok - s, 0), 0.0)
            s *= 2
        dff = xg * _sigmoid(-(ff_ref[...] + bf_ref[...]))
        dff_ref[...] = dff.astype(bf16)
        dbf_ref[...] = jnp.sum(dff, axis=0, keepdims=True)

    ff_w, ff_blk = _seg("ff")
    dff, dbf = pl.pallas_call(
        fox_gate_bwd_body, grid=(1,),
        in_specs=[pl.BlockSpec((n_tok, LANE), lambda i: (0, 0)), pl.BlockSpec((n_tok, LANE), lambda i: (0, 0)),
                  pl.BlockSpec((n_tok, ff_w), lambda i: (0, ff_blk)), pl.BlockSpec((1, LANE), lambda i: (0, 0))],
        out_specs=[pl.BlockSpec((n_tok, LANE), lambda i: (0, 0)), pl.BlockSpec((1, LANE), lambda i: (0, 0))],
        out_shape=[jax.ShapeDtypeStruct((n_tok, LANE), bf16), jax.ShapeDtypeStruct((1, LANE), f32)], name="fox_gate_bwd",
    )(dcq8, dck8, proj, w["b_f"])
    grads["fox_b_f"] = dbf[0, :HEADS]

    def fox_prep_bwd_body(fq_ref, fk_ref, dqn_ref, dkn_ref, dv_ref, qg_ref, kg_ref, dfq_ref, dfk_ref, dfv_ref, dqg_ref, dkg_ref):
        lane = lax.broadcasted_iota(jnp.int32, (TOK, LANE), 1)
        dqg = jnp.zeros((1, LANE), f32)
        dkg = jnp.zeros((1, LANE), f32)
        for blk in range(4):
            sl = slice(blk * LANE, (blk + 1) * LANE)
            xq = fq_ref[:, sl]
            dx, dg = _fox_halves_bwd(dqn_ref[:, sl], xq, _fox_halves(xq, lane), qg_ref[...], lane)
            dfq_ref[:, sl] = dx.astype(bf16)
            dqg = dqg + dg
            xk = fk_ref[:, sl]
            dx, dg = _fox_halves_bwd(dkn_ref[:, sl], xk, _fox_halves(xk, lane), kg_ref[...], lane)
            dfk_ref[:, sl] = dx.astype(bf16)
            dkg = dkg + dg
        dfv_ref[...] = dv_ref[...].astype(bf16)
        _accumulate(dqg_ref, dqg + pltpu.roll(dqg, 64, 1))
        _accumulate(dkg_ref, dkg + pltpu.roll(dkg, 64, 1))

    dfq, dfk, dfvb, dfqg, dfkg = _rowwise(
        "fox_prep_bwd", fox_prep_bwd_body, n_tok,
        [(proj, *_seg("fq")), (proj, *_seg("fk")), (dfqn, 512, 0), (dfkn, 512, 0), (dfv, 512, 0)],
        [w["fq_g"], w["fk_g"]], [(512, bf16)] * 3, [((1, LANE), f32)] * 2)
    grads["fox_q_norm"], grads["fox_k_norm"] = dfqg[0, :FOX_DIM], dfkg[0, :FOX_DIM]

    dqn, dkn, dv_mla = _attn_bwd("mla_attn_bwd", sv["qn"], sv["kn"], sv["v_mla"], sv["y_mla"], sv["lse_mla"], dy_mla,
                                 None, None, mla=True, n_tok=n_tok)

    def mla_prep_bwd_body(cq_ref, ckv_ref, kpe_ref, c_ref, s1_ref, s2_ref, dqn_ref, dkn_ref, dv_ref,
                          qa_ref, kva_ref, wq_ref, wkv_ref, qn_g_ref, kn_g_ref,
                          dcq_ref, dckv_ref, dkpe_ref, dwq_ref, dwkv_ref, dqa_ref, dkva_ref, dqng_ref, dkng_ref):
        c, s1, s2 = c_ref[...], s1_ref[...], s2_ref[...]
        cq, ckv = cq_ref[...], ckv_ref[...]
        cqn_b, r_cq, ckvn_b, r_ckv, q_raw, kv_raw, kpe_rot = _mla_recompute(
            cq, ckv, kpe_ref[...], c, s1, s2, qa_ref[...], kva_ref[...], wq_ref[...], wkv_ref[...])
        lane = lax.broadcasted_iota(jnp.int32, (TOK, LANE), 1)
        dq_raw, dk_raw = [], []
        dkpe_rot = jnp.zeros((TOK, LANE), f32)
        dqng = jnp.zeros((1, LANE), f32)
        dkng = jnp.zeros((1, LANE), f32)
        for hd in range(HEADS):
            sl = slice(hd * LANE, (hd + 1) * LANE)
            q_rot = _rope(q_raw[:, sl], c, s1, s2)
            r = lax.rsqrt(jnp.sum(q_rot * q_rot, axis=-1, keepdims=True) * (1.0 / MLA_QK) + EPS)
            dx, dg = _rms_bwd(dqn_ref[:, sl], q_rot, r, qn_g_ref[...], MLA_QK)
            dqng = dqng + dg
            dq_raw.append(_rope_t(dx, c, s1, s2))
            k_full = kv_raw[:, sl] + kpe_rot
            r = lax.rsqrt(jnp.sum(k_full * k_full, axis=-1, keepdims=True) * (1.0 / MLA_QK) + EPS)
            dx, dg = _rms_bwd(dkn_ref[:, sl], k_full, r, kn_g_ref[...], MLA_QK)
            dkng = dkng + dg
            dk_raw.append(jnp.where(lane < 64, dx, 0.0))
            dkpe_rot = dkpe_rot + dx
        dkpe = _rope_t(dkpe_rot, c, s1, s2)
        dkpe_ref[...] = jnp.where(jnp.logical_and(lane >= 64, lane < 64 + ROPE), dkpe, 0.0).astype(bf16)
        dq_raw = jnp.concatenate(dq_raw, axis=1).astype(bf16)
        dkv_raw = jnp.concatenate(dk_raw + [dv_ref[...]], axis=1).astype(bf16)
        dcqn = _nt(dq_raw, wq_ref[...])
        dckvn = _nt(dkv_raw, wkv_ref[...])
        dx, dg = _rms_bwd(dcqn, cq, r_cq, qa_ref[...], MLA_Q_RANK)
        dcq_ref[...] = dx.astype(bf16)
        _accumulate(dqa_ref, dg)
        dx, dg = _rms_bwd(dckvn, ckv, r_ckv, kva_ref[...], MLA_KV_RANK)
        dckv_ref[...] = dx.astype(bf16)
        _accumulate(dkva_ref, dg)
        _accumulate(dwq_ref, _tn(cqn_b, dq_raw))
        _accumulate(dwkv_ref, _tn(ckvn_b, dkv_raw))
        _accumulate(dqng_ref, dqng)
        _accumulate(dkng_ref, dkng)

    dcq, dckv, dkpe, dwq, dwkv, dqa, dkva, dqng, dkng = _rowwise(
        "mla_prep_bwd", mla_prep_bwd_body, n_tok,
        [(proj, *_seg("cq")), (proj, *_seg("ckv")), (proj, *_seg("kpe")), (c_tab, LANE, 0), (s1_tab, LANE, 0), (s2_tab, LANE, 0),
         (dqn, HEADS * LANE, 0), (dkn, HEADS * LANE, 0), (dv_mla, 512, 0)],
        [w["qa_g"], w["kva_g"], w["wq"], w["wkv"], w["qn_g"], w["kn_g"]],
        [(MLA_Q_RANK, bf16), (LANE, bf16), (LANE, bf16)],
        [((MLA_Q_RANK, HEADS * LANE), f32), ((MLA_KV_RANK, HEADS * LANE + 512), f32), ((1, MLA_Q_RANK), f32),
         ((1, MLA_KV_RANK), f32), ((1, LANE), f32), ((1, LANE), f32)])
    grads["mla_w_q_up"] = dwq.reshape(MLA_Q_RANK, HEADS, LANE)[:, :, :MLA_QK].reshape(MLA_Q_RANK, HEADS * MLA_QK)
    dwk = dwkv[:, :HEADS * LANE].reshape(MLA_KV_RANK, HEADS, LANE)[:, :, :64]
    dwv = dwkv[:, HEADS * LANE:].reshape(MLA_KV_RANK, HEADS, 64)
    grads["mla_w_kv_up"] = jnp.concatenate([dwk, dwv], axis=2).reshape(MLA_KV_RANK, HEADS * 128)
    grads["mla_q_a_norm"], grads["mla_kv_a_norm"] = dqa.reshape(-1), dkva.reshape(-1)
    grads["mla_q_norm"], grads["mla_k_norm"] = dqng[0, :MLA_QK], dkng[0, :MLA_QK]

    dproj = jnp.concatenate([dm_mla, dm_fox, dm_s5, dfq, dfk, dfvb, ds5u, dg_mla, dg_fox, dg_s5, dcq, dckv, dkpe, dff,
                             jnp.zeros((n_tok, NP - 7296), bf16)], axis=1)
    dh = _mm("in_proj_dgrad", dproj, w["w_in"], mode="nt", grid=(1, 1, NP // 512),
             a_spec=pl.BlockSpec((n_tok, 512), lambda i, j, kk: (0, kk)),
             b_spec=pl.BlockSpec((D_MODEL, 512), lambda i, j, kk: (0, kk)),
             o_spec=pl.BlockSpec((n_tok, D_MODEL), lambda i, j, kk: (0, 0)),
             out_shape=jax.ShapeDtypeStruct((n_tok, D_MODEL), f32), acc_shape=(n_tok, D_MODEL))
    dw_in = _mm_tn("in_proj_wgrad", sv["h"], dproj, m=D_MODEL, n=NP, k=n_tok, tm=D_MODEL, tn=512, tk=n_tok)
    grads["w_in"] = _unpad_w_in(dw_in)

    def norm_bwd_body(dh_ref, x_ref, do_ref, g_ref, dx_ref, dg_ref):
        xv = x_ref[...]
        r = lax.rsqrt(jnp.sum(xv * xv, axis=-1, keepdims=True) * (1.0 / D_MODEL) + EPS)
        dx, dg = _rms_bwd(dh_ref[...], xv, r, g_ref[...], D_MODEL)
        dx_ref[...] = do_ref[...] + dx
        _accumulate(dg_ref, dg)

    dx, dng = _rowwise("norm_bwd", norm_bwd_body, n_tok, [(dh, D_MODEL, 0), (x, D_MODEL, 0), (dout, D_MODEL, 0)],
                       [w["norm_g"]], [(D_MODEL, f32)], [((1, D_MODEL), f32)])
    grads["norm_g"] = dng.reshape(D_MODEL)
    return dx, grads


def _rope_tables(positions):
    inv = 1.0 / (ROPE_THETA ** (jnp.arange(0, ROPE, 2, dtype=f32) / ROPE))
    ang = positions.astype(f32).reshape(-1, 1) * inv
    cos, sin = jnp.cos(ang), jnp.sin(ang)
    n = ang.shape[0]
    z16, z32, z64 = jnp.zeros((n, 16), f32), jnp.zeros((n, 32), f32), jnp.zeros((n, 64), f32)
    c = jnp.concatenate([jnp.ones((n, 64), f32), cos, cos, z32], axis=1)
    s1 = jnp.concatenate([z64, -sin, z16, z32], axis=1)
    s2 = jnp.concatenate([z64, z16, sin, z32], axis=1)
    return c, s1, s2


BIG = ("w_in", "mla_w_q_up", "mla_w_kv_up", "s5_w_glu", "w_branch_out", "w_out")
SMALL = ("norm_g", "mla_q_a_norm", "mla_kv_a_norm", "mla_q_norm", "mla_k_norm", "fox_b_f", "fox_q_norm", "fox_k_norm",
         "s5_lambda_re", "s5_lambda_im", "s5_log_dt", "s5_b_re", "s5_b_im", "s5_c_re", "s5_c_im", "s5_d", "s5_b_glu")
WEIGHTS = ("norm_g", "w_in", "mla_q_a_norm", "mla_w_q_up", "mla_kv_a_norm", "mla_w_kv_up", "mla_q_norm", "mla_k_norm",
           "fox_b_f", "fox_q_norm", "fox_k_norm", "s5_lambda_re", "s5_lambda_im", "s5_log_dt", "s5_b_re", "s5_b_im",
           "s5_c_re", "s5_c_im", "s5_d", "s5_w_glu", "s5_b_glu", "w_branch_out", "w_out")


def _local_step(x, positions, loss_target, small, big):
    n_tok = x.shape[0]
    tabs = _rope_tables(positions)
    ws, saves = [], []
    hcur = x
    for l in range(DEPTH):
        w = _layer_weights(l, small, big)
        hcur, sv = _layer_fwd(hcur, w, tabs, n_tok)
        ws.append(w)
        saves.append(sv)

    def loss_body(y_ref, t_ref, d_ref, l_ref):
        err = y_ref[...] - t_ref[...]
        d_ref[...] = err * (1.0 / D_MODEL)
        tot = jnp.sum(jnp.sum(err * err, axis=-1, keepdims=True), axis=0, keepdims=True)
        _accumulate(l_ref, jnp.broadcast_to(tot * (0.5 / D_MODEL), (1, LANE)))

    dcur, loss = _rowwise("loss", loss_body, n_tok, [(hcur, D_MODEL, 0), (loss_target, D_MODEL, 0)], [], [(D_MODEL, f32)],
                          [((1, LANE), f32)])
    layer_grads = [None] * DEPTH
    for l in reversed(range(DEPTH)):
        dcur, layer_grads[l] = _layer_bwd(dcur, ws[l], saves[l], tabs, n_tok)
    grads = {n: jnp.stack([layer_grads[l][n] for l in range(DEPTH)]) for n in WEIGHTS}
    return loss[0, 0], dcur, grads


N_CHIPS = 4
N_DEV = 8
_ANY = pl.BlockSpec(memory_space=pl.ANY)
_MESH = pl.DeviceIdType.MESH
_BIG_AXIS = {"w_in": 2, "mla_w_q_up": 2, "mla_w_kv_up": 2, "s5_w_glu": 1, "w_branch_out": 1, "w_out": 1}


def _all_gather8(name, blk):
    m = blk.shape[0]

    def body(x_ref, out_ref, send_sems, recv_sems, local_sem):
        x, y, c = lax.axis_index("x"), lax.axis_index("y"), lax.axis_index("c")
        me, sibling = (x, y, c), (x, y, 1 - c)
        chips = [(1 - x, y), (x, 1 - y), (1 - x, 1 - y)]

        def slot(px, py, pc):
            return out_ref.at[4 * px + 2 * py + pc]

        def copy(k, block, to, src=None):
            return pltpu.make_async_remote_copy(
                src_ref=slot(*block) if src is None else src, dst_ref=slot(*block),
                send_sem=send_sems.at[k], recv_sem=recv_sems.at[k], device_id=to, device_id_type=_MESH)

        mine = pltpu.make_async_copy(x_ref, slot(*me), local_sem)
        mine.start()
        first = [copy(0, me, sibling, src=x_ref)]
        first += [copy(1 + j, me, (*chip, c), src=x_ref) for j, chip in enumerate(chips)]
        for cp in first:
            cp.start()
        passed = [copy(4 + j, (*chip, c), sibling) for j, chip in enumerate(chips)]
        for j, chip in enumerate(chips):
            copy(1 + j, (*chip, c), me).wait_recv()
            passed[j].start()
        copy(0, sibling, me).wait_recv()
        for j, chip in enumerate(chips):
            copy(4 + j, (*chip, 1 - c), me).wait_recv()
        for cp in first + passed:
            cp.wait_send()
        mine.wait()

    return pl.pallas_call(
        body, out_shape=jax.ShapeDtypeStruct((N_DEV, m, LANE), blk.dtype), in_specs=[_ANY], out_specs=_ANY, name=name,
        scratch_shapes=[pltpu.SemaphoreType.DMA((7,)), pltpu.SemaphoreType.DMA((7,)), pltpu.SemaphoreType.DMA],
    )(blk)


def _gather_layers(name, shards):
    n = len(shards)

    def body(*refs):
        x_refs, out_refs = refs[:n], refs[n:2 * n]
        send_sems, recv_sems, local_sems = refs[2 * n:]
        x, y, c = lax.axis_index("x"), lax.axis_index("y"), lax.axis_index("c")
        me, sibling = (x, y, c), (x, y, 1 - c)
        chips = [(1 - x, y), (x, 1 - y), (1 - x, 1 - y)]

        def copy(w, k, block, to, src=None):
            px, py, pc = block
            slot = out_refs[w].at[2 * px + py, pc]
            return pltpu.make_async_remote_copy(
                src_ref=slot if src is None else src, dst_ref=slot, send_sem=send_sems.at[7 * w + k],
                recv_sem=recv_sems.at[7 * w + k], device_id=to, device_id_type=_MESH)

        started, local = [], []
        for w in range(n):
            src = x_refs[w].at[c]
            mine = pltpu.make_async_copy(src, out_refs[w].at[2 * x + y, c], local_sems.at[w])
            mine.start()
            local.append(mine)
            first = [copy(w, 0, me, sibling, src=src)] + [copy(w, 1 + j, me, (*chip, c), src=src) for j, chip in enumerate(chips)]
            for cp in first:
                cp.start()
            started += first
        for w in range(n):
            for j, chip in enumerate(chips):
                copy(w, 1 + j, (*chip, c), me).wait_recv()
                onward = copy(w, 4 + j, (*chip, c), sibling)
                onward.start()
                started.append(onward)
        for w in range(n):
            copy(w, 0, sibling, me).wait_recv()
            for j, chip in enumerate(chips):
                copy(w, 4 + j, (*chip, 1 - c), me).wait_recv()
        for cp in started:
            cp.wait_send()
        for cp in local:
            cp.wait()

    return pl.pallas_call(
        body, out_shape=[jax.ShapeDtypeStruct((N_CHIPS,) + s.shape, s.dtype) for s in shards],
        in_specs=[_ANY] * n, out_specs=[_ANY] * n, name=name,
        scratch_shapes=[pltpu.SemaphoreType.DMA((7 * n,)), pltpu.SemaphoreType.DMA((7 * n,)), pltpu.SemaphoreType.DMA((n,))],
    )(*shards)


def _swap_layers(name, parts):
    n = len(parts)

    def body(*refs):
        p_refs, keep_refs, got_refs = refs[:n], refs[n:2 * n], refs[2 * n:3 * n]
        send_sems, recv_sems, local_sems = refs[3 * n:]
        x, y, c = lax.axis_index("x"), lax.axis_index("y"), lax.axis_index("c")
        copies, local = [], []
        for w in range(n):
            for j in range(N_CHIPS):
                cp = pltpu.make_async_remote_copy(
                    src_ref=p_refs[w].at[j, 1 - c], dst_ref=got_refs[w].at[j], send_sem=send_sems.at[N_CHIPS * w + j],
                    recv_sem=recv_sems.at[N_CHIPS * w + j], device_id=(x, y, 1 - c), device_id_type=_MESH)
                cp.start()
                copies.append(cp)
                lc = pltpu.make_async_copy(p_refs[w].at[j, c], keep_refs[w].at[j], local_sems.at[N_CHIPS * w + j])
                lc.start()
                local.append(lc)
        for cp in copies:
            cp.wait()
        for lc in local:
            lc.wait()

    half = [jax.ShapeDtypeStruct((N_CHIPS,) + p.shape[2:], p.dtype) for p in parts]
    res = pl.pallas_call(
        body, out_shape=half + half, in_specs=[_ANY] * n, out_specs=[_ANY] * (2 * n), name=name,
        scratch_shapes=[pltpu.SemaphoreType.DMA((N_CHIPS * n,)), pltpu.SemaphoreType.DMA((N_CHIPS * n,)),
                        pltpu.SemaphoreType.DMA((N_CHIPS * n,))],
    )(*parts)
    return res[:n], res[n:]


def _scatter_to_chips(name, parts):
    n = len(parts)

    def body(*refs):
        p_refs, out_refs = refs[:n], refs[n:2 * n]
        send_sems, recv_sems, local_sems = refs[2 * n:]
        x, y, c = lax.axis_index("x"), lax.axis_index("y"), lax.axis_index("c")
        jme = 2 * x + y
        chips = [(1 - x, y), (x, 1 - y), (1 - x, 1 - y)]
        sends, local = [], []
        for w in range(n):
            mine = pltpu.make_async_copy(p_refs[w].at[jme], out_refs[w].at[jme], local_sems.at[w])
            mine.start()
            local.append(mine)
            for k, (tx, ty) in enumerate(chips):
                cp = pltpu.make_async_remote_copy(
                    src_ref=p_refs[w].at[2 * tx + ty], dst_ref=out_refs[w].at[jme], send_sem=send_sems.at[3 * w + k],
                    recv_sem=recv_sems.at[3 * w + k], device_id=(tx, ty, c), device_id_type=_MESH)
                cp.start()
                sends.append(cp)
        for w in range(n):
            for k, (tx, ty) in enumerate(chips):
                pltpu.make_async_remote_copy(
                    src_ref=p_refs[w].at[jme], dst_ref=out_refs[w].at[2 * tx + ty], send_sem=send_sems.at[3 * w + k],
                    recv_sem=recv_sems.at[3 * w + k], device_id=(tx, ty, c), device_id_type=_MESH).wait_recv()
        for cp in sends:
            cp.wait_send()
        for cp in local:
            cp.wait()

    return pl.pallas_call(
        body, out_shape=[jax.ShapeDtypeStruct(p.shape, p.dtype) for p in parts], in_specs=[_ANY] * n, out_specs=[_ANY] * n, name=name,
        scratch_shapes=[pltpu.SemaphoreType.DMA((3 * n,)), pltpu.SemaphoreType.DMA((3 * n,)), pltpu.SemaphoreType.DMA((n,))],
    )(*parts)


def _share_layers(name, totals):
    n = len(totals)

    def body(*refs):
        t_refs, out_refs = refs[:n], refs[n:2 * n]
        send_sems, recv_sems, local_sems = refs[2 * n:]
        x, y, c = lax.axis_index("x"), lax.axis_index("y"), lax.axis_index("c")
        copies, local = [], []
        for w in range(n):
            lc = pltpu.make_async_copy(t_refs[w], out_refs[w].at[c], local_sems.at[w])
            lc.start()
            local.append(lc)
            cp = pltpu.make_async_remote_copy(src_ref=t_refs[w], dst_ref=out_refs[w].at[c], send_sem=send_sems.at[w],
                                              recv_sem=recv_sems.at[w], device_id=(x, y, 1 - c), device_id_type=_MESH)
            cp.start()
            copies.append(cp)
        for w in range(n):
            pltpu.make_async_remote_copy(src_ref=t_refs[w], dst_ref=out_refs[w].at[1 - c], send_sem=send_sems.at[w],
                                         recv_sem=recv_sems.at[w], device_id=(x, y, 1 - c), device_id_type=_MESH).wait_recv()
        for cp in copies:
            cp.wait_send()
        for lc in local:
            lc.wait()

    return pl.pallas_call(
        body, out_shape=[jax.ShapeDtypeStruct((2,) + t.shape, t.dtype) for t in totals], in_specs=[_ANY] * n, out_specs=[_ANY] * n,
        name=name, scratch_shapes=[pltpu.SemaphoreType.DMA((n,)), pltpu.SemaphoreType.DMA((n,)), pltpu.SemaphoreType.DMA((n,))],
    )(*totals)


def _row_tile(rows, cols):
    best = 8
    for t in range(8, rows + 1, 8):
        if rows % t == 0 and t * cols * 4 <= 2 * 1024 * 1024:
            best = t
    return best


def _add_pair(name, a, b):
    _, r, c = a.shape
    t = _row_tile(r, c)

    def body(a_ref, b_ref, o_ref):
        o_ref[...] = a_ref[...] + b_ref[...]

    spec = pl.BlockSpec((None, t, c), lambda j, i: (j, i, 0))
    return pl.pallas_call(body, grid=(N_CHIPS, r // t), in_specs=[spec, spec], out_specs=spec,
                          out_shape=jax.ShapeDtypeStruct(a.shape, f32), name=name,
                          compiler_params=pltpu.CompilerParams(dimension_semantics=("arbitrary", "arbitrary")))(a, b)


def _add_four(name, a):
    _, r, c = a.shape
    t = _row_tile(r, c)

    def body(a0, a1, a2, a3, o_ref):
        o_ref[...] = ((a0[...] + a1[...]) + a2[...]) + a3[...]

    specs = [pl.BlockSpec((None, t, c), functools.partial(lambda i, k: (k, i, 0), k=k)) for k in range(N_CHIPS)]
    return pl.pallas_call(body, grid=(r // t,), in_specs=specs, out_specs=pl.BlockSpec((t, c), lambda i: (i, 0)),
                          out_shape=jax.ShapeDtypeStruct((r, c), f32), name=name,
                          compiler_params=pltpu.CompilerParams(dimension_semantics=("arbitrary",)))(a, a, a, a)


def _adamw(name, w, g, m, v, tile):
    r, c = w.shape
    c1 = 1.0 - ADAM_B1 ** ADAM_STEP
    c2 = 1.0 - ADAM_B2 ** ADAM_STEP

    def body(w_ref, g_ref, m_ref, v_ref, d_ref, nm_ref, nv_ref):
        gv = g_ref[...]
        nm = ADAM_B1 * m_ref[...] + (1.0 - ADAM_B1) * gv
        nv = ADAM_B2 * v_ref[...] + (1.0 - ADAM_B2) * (gv * gv)
        m_hat = nm / c1
        v_hat = nv / c2
        d_ref[...] = -ADAM_LR * (m_hat / (jnp.sqrt(v_hat) + ADAM_EPS) + ADAM_WD * w_ref[...])
        nm_ref[...] = nm
        nv_ref[...] = nv

    spec = pl.BlockSpec((tile, c), lambda i: (i, 0))
    sds = jax.ShapeDtypeStruct((r, c), f32)
    return pl.pallas_call(body, grid=(r // tile,), in_specs=[spec] * 4, out_specs=[spec] * 3, out_shape=[sds] * 3, name=name,
                          compiler_params=pltpu.CompilerParams(dimension_semantics=("arbitrary",), vmem_limit_bytes=VMEM_LIMIT),
                          )(w, g, m, v)


def _split4(a, axis):
    shp = a.shape
    a = a.reshape(shp[:axis] + (N_CHIPS, shp[axis] // N_CHIPS) + shp[axis + 1:])
    return jnp.moveaxis(a, axis, 0)


def _join4(a, axis):
    a = jnp.moveaxis(a, 0, axis)
    shp = a.shape
    return a.reshape(shp[:axis] + (shp[axis] * shp[axis + 1],) + shp[axis + 2:])


def _pad_rows(flat, rows):
    return jnp.pad(flat, (0, rows * LANE - flat.shape[0])).reshape(rows, LANE)


def kernel(x, positions, norm_g, w_in, mla_q_a_norm, mla_w_q_up, mla_kv_a_norm, mla_w_kv_up, mla_q_norm, mla_k_norm, fox_b_f, fox_q_norm, fox_k_norm, s5_lambda_re, s5_lambda_im, s5_log_dt, s5_b_re, s5_b_im, s5_c_re, s5_c_im, s5_d, s5_w_glu, s5_b_glu, w_branch_out, w_out, loss_target, m_norm_g, m_w_in, m_mla_q_a_norm, m_mla_w_q_up, m_mla_kv_a_norm, m_mla_w_kv_up, m_mla_q_norm, m_mla_k_norm, m_fox_b_f, m_fox_q_norm, m_fox_k_norm, m_s5_lambda_re, m_s5_lambda_im, m_s5_log_dt, m_s5_b_re, m_s5_b_im, m_s5_c_re, m_s5_c_im, m_s5_d, m_s5_w_glu, m_s5_b_glu, m_w_branch_out, m_w_out, v_norm_g, v_w_in, v_mla_q_a_norm, v_mla_w_q_up, v_mla_kv_a_norm, v_mla_w_kv_up, v_mla_q_norm, v_mla_k_norm, v_fox_b_f, v_fox_q_norm, v_fox_k_norm, v_s5_lambda_re, v_s5_lambda_im, v_s5_log_dt, v_s5_b_re, v_s5_b_im, v_s5_c_re, v_s5_c_im, v_s5_d, v_s5_w_glu, v_s5_b_glu, v_w_branch_out, v_w_out):
    given = dict(locals())
    wts = {n: given[n] for n in WEIGHTS}
    mom1 = {n: given["m_" + n] for n in WEIGHTS}
    mom2 = {n: given["v_" + n] for n in WEIGHTS}

    gathered = _gather_layers("gather_weights", [wts[n].astype(bf16) for n in BIG])
    big = {n: _join4(g, _BIG_AXIS[n]) for n, g in zip(BIG, gathered)}
    small = {n: wts[n] for n in SMALL}

    loss_local, grad_x, grads = _local_step(x[0], positions, loss_target[0], small, big)
    loss = lax.psum(loss_local, ("x", "y", "c"))

    small_flat = jnp.concatenate([grads[n].reshape(-1) for n in SMALL])
    small_rows = -(-small_flat.shape[0] // (N_DEV * 8 * LANE)) * 8
    parts = [_split4(grads[n], _BIG_AXIS[n]) for n in BIG]
    parts.append(_pad_rows(small_flat, N_DEV * small_rows).reshape(N_CHIPS, 2, small_rows, LANE))
    keep, got = _swap_layers("grads_to_sibling", parts)
    pair = [_add_pair("grads_pair_sum_%d" % i, a, b) for i, (a, b) in enumerate(zip(keep, got))]
    landed = _scatter_to_chips("grads_to_chips", pair)
    total = [_add_four("grads_chip_sum_%d" % i, a) for i, a in enumerate(landed)]
    shared = _share_layers("grads_share", total[:-1])
    small_all = _all_gather8("gather_small_grads", total[-1]).reshape(-1)

    g_out = dict(zip(BIG, shared))
    pos = 0
    for n in SMALL:
        g_out[n] = small_all[pos:pos + wts[n].size].reshape(wts[n].shape)
        pos += wts[n].size

    delta, new_m, new_v = {}, {}, {}
    for n in BIG:
        shp = wts[n].shape
        as2d = lambda a: a.reshape(-1, shp[-1])
        d_, m_, v_ = _adamw("adamw_" + n, as2d(wts[n]), as2d(g_out[n]), as2d(mom1[n]), as2d(mom2[n]), tile=256)
        delta[n], new_m[n], new_v[n] = d_.reshape(shp), m_.reshape(shp), v_.reshape(shp)
    n_small_rows = N_DEV * small_rows
    pack = lambda d: _pad_rows(jnp.concatenate([d[n].reshape(-1) for n in SMALL]), n_small_rows)
    packed = _adamw("adamw_small", pack(wts), small_all.reshape(n_small_rows, LANE), pack(mom1), pack(mom2), tile=n_small_rows)
    for out, res in zip((delta, new_m, new_v), packed):
        res = res.reshape(-1)
        pos = 0
        for n in SMALL:
            out[n] = res[pos:pos + wts[n].size].reshape(wts[n].shape)
            pos += wts[n].size

    return (loss, grad_x[None], *[g_out[n] for n in WEIGHTS], *[delta[n] for n in WEIGHTS],
            *[new_m[n] for n in WEIGHTS], *[new_v[n] for n in WEIGHTS])
```

```python
import functools
import math

import jax
import jax.numpy as jnp
from jax import lax
from jax.experimental import pallas as pl
from jax.experimental.pallas import tpu as pltpu

f32 = jnp.float32
bf16 = jnp.bfloat16

D_MODEL = 1024
DEPTH = 2
EPS = 1e-6
HEADS = 8
MLA_QK = 96
MLA_Q_RANK = 256
MLA_KV_RANK = 128
ROPE = 32
ROPE_THETA = 10000.0
FOX_DIM = 64
S5_GROUPS = 32
S5_GROUP = 16
S5_STATE = 64
S5_LANES = S5_GROUPS * S5_STATE
LANE = 128
S5_BLOCKS = S5_LANES // LANE
IN_WIDTH = 7080
TOK = 256
VMEM_LIMIT = 56 * 1024 * 1024

ADAM_LR = 0.001
ADAM_B1 = 0.9
ADAM_B2 = 0.999
ADAM_EPS = 1e-08
ADAM_WD = 0.01
ADAM_STEP = 10

_ORIG = {}
_off = 0
for _n, _w in (("cq", 256), ("ckv", 128), ("kpe", 32), ("fq", 512), ("fk", 512), ("fv", 512), ("ff", 8), ("s5u", 512),
               ("g_mla", 512), ("g_fox", 512), ("g_s5", 512), ("m_mla", 1024), ("m_fox", 1024), ("m_s5", 1024)):
    _ORIG[_n] = (_off, _w)
    _off += _w
_PAD = {"m_mla": (0, 1024, 0), "m_fox": (1024, 1024, 0), "m_s5": (2048, 1024, 0),
        "fq": (3072, 512, 0), "fk": (3584, 512, 0), "fv": (4096, 512, 0), "s5u": (4608, 512, 0),
        "g_mla": (5120, 512, 0), "g_fox": (5632, 512, 0), "g_s5": (6144, 512, 0),
        "cq": (6656, 256, 0), "ckv": (6912, 128, 0), "kpe": (7040, 128, 64), "ff": (7168, 128, 0)}
NP = 7680
_PAD_ORDER = ("m_mla", "m_fox", "m_s5", "fq", "fk", "fv", "s5u", "g_mla", "g_fox", "g_s5", "cq", "ckv", "kpe", "ff")


def _seg(name):
    start, width, _ = _PAD[name]
    return width, start // width


def _nn(a, b):
    return lax.dot_general(a.astype(bf16), b.astype(bf16), (((1,), (0,)), ((), ())), preferred_element_type=f32)


def _nt(a, b):
    return lax.dot_general(a.astype(bf16), b.astype(bf16), (((1,), (1,)), ((), ())), preferred_element_type=f32)


def _tn(a, b):
    return lax.dot_general(a.astype(bf16), b.astype(bf16), (((0,), (0,)), ((), ())), preferred_element_type=f32)


def _rms(x, g, n):
    r = lax.rsqrt(jnp.sum(x * x, axis=-1, keepdims=True) * (1.0 / n) + EPS)
    return x * r * g, r


def _rms_bwd(dy, x, r, g, n):
    xh = x * r
    dg = jnp.sum(dy * xh, axis=0, keepdims=True)
    dxh = dy * g
    dx = r * (dxh - xh * (jnp.sum(dxh * xh, axis=-1, keepdims=True) * (1.0 / n)))
    return dx, dg


def _sigmoid(x):
    return 1.0 / (1.0 + jnp.exp(-x))


_GELU_C = math.sqrt(2.0 / math.pi)


def _gelu(x):
    t = jnp.tanh(_GELU_C * (x + 0.044715 * x * x * x))
    return 0.5 * x * (1.0 + t), t


def _gelu_grad(x, t):
    return 0.5 * (1.0 + t) + 0.5 * x * (1.0 - t * t) * _GELU_C * (1.0 + 3.0 * 0.044715 * x * x)


def _accumulate(ref, val):
    i = pl.program_id(0)

    @pl.when(i == 0)
    def _():
        ref[...] = val

    @pl.when(i > 0)
    def _():
        ref[...] += val


def _rope(x, c, s1, s2):
    return x * c + pltpu.roll(x, LANE - 16, 1) * s1 + pltpu.roll(x, 16, 1) * s2


def _rope_t(d, c, s1, s2):
    return d * c + pltpu.roll(d * s1, 16, 1) + pltpu.roll(d * s2, LANE - 16, 1)


def _const_map(ndim):
    return lambda *_: (0,) * ndim


def _rowwise(name, body, n_tok, tiled_in, full_in, tiled_out, acc_out, tile=TOK):
    in_specs, args = [], []
    for arr, width, blk in tiled_in:
        in_specs.append(pl.BlockSpec((tile, width), functools.partial(lambda i, b: (i, b), b=blk)))
        args.append(arr)
    for arr in full_in:
        in_specs.append(pl.BlockSpec(arr.shape, _const_map(arr.ndim)))
        args.append(arr)
    out_specs, out_shape = [], []
    for width, dt in tiled_out:
        out_specs.append(pl.BlockSpec((tile, width), lambda i: (i, 0)))
        out_shape.append(jax.ShapeDtypeStruct((n_tok, width), dt))
    for shape, dt in acc_out:
        out_specs.append(pl.BlockSpec(shape, _const_map(len(shape))))
        out_shape.append(jax.ShapeDtypeStruct(shape, dt))
    return pl.pallas_call(
        body, grid=(n_tok // tile,), in_specs=in_specs, out_specs=out_specs, out_shape=out_shape, name=name,
        compiler_params=pltpu.CompilerParams(dimension_semantics=("arbitrary",), vmem_limit_bytes=VMEM_LIMIT),
    )(*args)


def _mm(name, a, b, *, mode, grid, a_spec, b_spec, o_spec, out_shape, acc_shape, add=None, add_spec=None):
    nk = grid[2]

    def body(*refs):
        if add is None:
            a_ref, b_ref, o_ref, acc_ref = refs
        else:
            a_ref, b_ref, add_ref, o_ref, acc_ref = refs
        k = pl.program_id(2)

        @pl.when(k == 0)
        def _():
            acc_ref[...] = jnp.zeros_like(acc_ref)

        acc_ref[...] += {"nn": _nn, "nt": _nt, "tn": _tn}[mode](a_ref[...], b_ref[...])

        @pl.when(k == nk - 1)
        def _():
            r = acc_ref[...]
            if add is not None:
                r = r + add_ref[...]
            o_ref[...] = r.astype(o_ref.dtype)

    in_specs = [a_spec, b_spec] + ([add_spec] if add is not None else [])
    args = (a, b) + ((add,) if add is not None else ())
    return pl.pallas_call(
        body, grid=grid, in_specs=in_specs, out_specs=o_spec, out_shape=out_shape, name=name,
        scratch_shapes=[pltpu.VMEM(acc_shape, f32)],
        compiler_params=pltpu.CompilerParams(dimension_semantics=("arbitrary", "arbitrary", "arbitrary"), vmem_limit_bytes=VMEM_LIMIT),
    )(*args)


def _mm_nn(name, a, b, *, m, n, k, tm, tn, tk, out_dtype=f32, a_koff=0):
    return _mm(name, a, b, mode="nn", grid=(m // tm, n // tn, k // tk),
               a_spec=pl.BlockSpec((tm, tk), lambda i, j, kk: (i, kk + a_koff)),
               b_spec=pl.BlockSpec((tk, tn), lambda i, j, kk: (kk, j)),
               o_spec=pl.BlockSpec((tm, tn), lambda i, j, kk: (i, j)),
               out_shape=jax.ShapeDtypeStruct((m, n), out_dtype), acc_shape=(tm, tn))


def _mm_tn(name, a, b, *, m, n, k, tm, tn, tk, a_moff=0):
    return _mm(name, a, b, mode="tn", grid=(m // tm, n // tn, k // tk),
               a_spec=pl.BlockSpec((tk, tm), lambda i, j, kk: (kk, i + a_moff)),
               b_spec=pl.BlockSpec((tk, tn), lambda i, j, kk: (kk, j)),
               o_spec=pl.BlockSpec((tm, tn), lambda i, j, kk: (i, j)),
               out_shape=jax.ShapeDtypeStruct((m, n), f32), acc_shape=(tm, tn))


def _attn_common(mla):
    qw = 2 * LANE if mla else LANE
    scale = 1.0 / math.sqrt(MLA_QK if mla else FOX_DIM)
    return qw, scale


def _attn_fwd(name, q, k, v, bias_col, bias_row, *, mla, n_tok):
    qw, scale = _attn_common(mla)
    nq = n_tok // TOK
    has_bias = bias_col is not None

    def body(*refs):
        if has_bias:
            q_ref, k_ref, v_ref, bc_ref, br_ref, o_ref, lse_ref = refs
        else:
            q_ref, k_ref, v_ref, o_ref, lse_ref = refs
        p = pl.program_id(0)
        i = pl.program_id(1)
        lane = lax.broadcasted_iota(jnp.int32, (TOK, LANE), 1)
        qpos = i * TOK + lax.broadcasted_iota(jnp.int32, (TOK, TOK), 0)
        kiota = lax.broadcasted_iota(jnp.int32, (TOK, TOK), 1)
        o_tot = jnp.zeros((TOK, LANE), f32)
        lse_tot = jnp.zeros((TOK, LANE), f32)
        for e in (0, 1):
            half = (lane >= 64) if e else (lane < 64)
            if mla:
                qh = q_ref[:, e * LANE:(e + 1) * LANE]
            else:
                qh = jnp.where(half, q_ref[...], jnp.zeros((), bf16))
            h = 2 * p + e
            if has_bias:
                cq = jnp.sum(jnp.where(lane == h, bc_ref[...], 0.0), axis=-1, keepdims=True)

            def kv_step(j, carry, e=e, half=half, qh=qh, h=h, cq=cq if has_bias else None):
                m, l, acc = carry
                off = pl.multiple_of(j * TOK, TOK)
                if mla:
                    kj = k_ref[pl.ds(off, TOK), e * LANE:(e + 1) * LANE]
                else:
                    kj = k_ref[pl.ds(off, TOK), :]
                vj = jnp.where(half, v_ref[pl.ds(off, TOK), :], jnp.zeros((), bf16))
                s = _nt(qh, kj) * scale
                if has_bias:
                    s = s + (cq - br_ref[h, j])
                kpos = off + kiota
                allowed = ((kpos // 64) <= (qpos // 64)) if mla else (kpos <= qpos)
                s = jnp.where(allowed, s, -1e30)
                m_new = jnp.maximum(m, jnp.max(s, axis=-1, keepdims=True))
                alpha = jnp.exp(m - m_new)
                pe = jnp.exp(s - m_new)
                l = alpha * l + jnp.sum(pe, axis=-1, keepdims=True)
                acc = alpha * acc + _nn(pe, vj)
                return m_new, l, acc

            m, l, acc = lax.fori_loop(0, i + 1, kv_step,
                                      (jnp.full((TOK, 1), -1e30, f32), jnp.zeros((TOK, 1), f32), jnp.zeros((TOK, LANE), f32)))
            o_tot = o_tot + acc / l
            lse_tot = jnp.where(half, m + jnp.log(l), lse_tot)
        o_ref[...] = o_tot
        lse_ref[...] = lse_tot

    in_specs = [pl.BlockSpec((TOK, qw), lambda p, i: (i, p)),
                pl.BlockSpec((n_tok, qw), lambda p, i: (0, p)),
                pl.BlockSpec((n_tok, LANE), lambda p, i: (0, p))]
    args = [q, k, v]
    if has_bias:
        in_specs += [pl.BlockSpec((TOK, LANE), lambda p, i: (i, 0)), pl.BlockSpec(bias_row.shape, _const_map(4))]
        args += [bias_col, bias_row]
    return pl.pallas_call(
        body, grid=(4, nq), in_specs=in_specs,
        out_specs=[pl.BlockSpec((TOK, LANE), lambda p, i: (i, p)), pl.BlockSpec((TOK, LANE), lambda p, i: (i, p))],
        out_shape=[jax.ShapeDtypeStruct((n_tok, 512), f32), jax.ShapeDtypeStruct((n_tok, 512), f32)], name=name,
        compiler_params=pltpu.CompilerParams(dimension_semantics=("arbitrary", "arbitrary"), vmem_limit_bytes=VMEM_LIMIT),
    )(*args)


def _attn_bwd(name, q, k, v, o, lse, do, bias_col, bias_row, *, mla, n_tok):
    qw, scale = _attn_common(mla)
    nq = n_tok // TOK
    has_bias = bias_col is not None

    def body(*refs):
        if has_bias:
            q_ref, k_ref, v_ref, o_ref, lse_ref, do_ref, bc_ref, br_ref, dq_ref, dk_ref, dv_ref, dbc_ref, dbr_ref = refs
        else:
            q_ref, k_ref, v_ref, o_ref, lse_ref, do_ref, dq_ref, dk_ref, dv_ref = refs
        p = pl.program_id(0)
        i = pl.program_id(1)

        @pl.when(i == 0)
        def _():
            dk_ref[...] = jnp.zeros_like(dk_ref)
            dv_ref[...] = jnp.zeros_like(dv_ref)

        if has_bias:
            @pl.when(jnp.logical_and(i == 0, p == 0))
            def _():
                dbr_ref[...] = jnp.zeros_like(dbr_ref)

        lane = lax.broadcasted_iota(jnp.int32, (TOK, LANE), 1)
        qpos = i * TOK + lax.broadcasted_iota(jnp.int32, (TOK, TOK), 0)
        kiota = lax.broadcasted_iota(jnp.int32, (TOK, TOK), 1)
        do_blk = do_ref[...]
        prod = do_blk * o_ref[...]
        lse_blk = lse_ref[...]
        dq_tot = jnp.zeros((TOK, LANE), f32)
        dbc_tot = jnp.zeros((TOK, LANE), f32)
        for e in (0, 1):
            half = (lane >= 64) if e else (lane < 64)
            if mla:
                qh = q_ref[:, e * LANE:(e + 1) * LANE]
            else:
                qh = jnp.where(half, q_ref[...], jnp.zeros((), bf16))
            h = 2 * p + e
            delta = jnp.sum(jnp.where(half, prod, 0.0), axis=-1, keepdims=True)
            lse_h = lse_blk[:, 64 * e:64 * e + 1]
            do_h = jnp.where(half, do_blk, 0.0).astype(bf16)
            if has_bias:
                cq = jnp.sum(jnp.where(lane == h, bc_ref[...], 0.0), axis=-1, keepdims=True)

            def kv_step(j, carry, e=e, half=half, qh=qh, h=h, delta=delta, lse_h=lse_h, do_h=do_h,
                        cq=cq if has_bias else None):
                dq_acc, rs_acc = carry
                off = pl.multiple_of(j * TOK, TOK)
                if mla:
                    kj = k_ref[pl.ds(off, TOK), e * LANE:(e + 1) * LANE]
                else:
                    kj = k_ref[pl.ds(off, TOK), :]
                vj = jnp.where(half, v_ref[pl.ds(off, TOK), :], jnp.zeros((), bf16))
                s = _nt(qh, kj) * scale
                if has_bias:
                    s = s + (cq - br_ref[h, j])
                kpos = off + kiota
                allowed = ((kpos // 64) <= (qpos // 64)) if mla else (kpos <= qpos)
                pr = jnp.where(allowed, jnp.exp(s - lse_h), 0.0)
                dp = _nt(do_h, vj)
                ds = pr * (dp - delta)
                dq_acc = dq_acc + _nn(ds, kj) * scale
                dk_j = _tn(ds, qh) * scale
                if mla:
                    dk_ref[pl.ds(off, TOK), e * LANE:(e + 1) * LANE] += dk_j
                else:
                    dk_ref[pl.ds(off, TOK), :] += dk_j
                dv_ref[pl.ds(off, TOK), :] += _tn(pr, do_h)
                if has_bias:
                    rs_acc = rs_acc + jnp.sum(ds, axis=-1, keepdims=True)
                    dbr_ref[h, j] += jnp.broadcast_to(-jnp.sum(ds, axis=0, keepdims=True), (8, TOK))
                return dq_acc, rs_acc

            dq_acc, rs_acc = lax.fori_loop(0, i + 1, kv_step, (jnp.zeros((TOK, LANE), f32), jnp.zeros((TOK, 1), f32)))
            if mla:
                dq_ref[:, e * LANE:(e + 1) * LANE] = dq_acc
            else:
                dq_tot = dq_tot + jnp.where(half, dq_acc, 0.0)
            if has_bias:
                dbc_tot = jnp.where(half, rs_acc, dbc_tot)
        if not mla:
            dq_ref[...] = dq_tot
        if has_bias:
            dbc_ref[...] = dbc_tot

    tile_q = pl.BlockSpec((TOK, qw), lambda p, i: (i, p))
    tile_v = pl.BlockSpec((TOK, LANE), lambda p, i: (i, p))
    full_k = pl.BlockSpec((n_tok, qw), lambda p, i: (0, p))
    full_v = pl.BlockSpec((n_tok, LANE), lambda p, i: (0, p))
    in_specs = [tile_q, full_k, full_v, tile_v, tile_v, tile_v]
    args = [q, k, v, o, lse, do]
    out_specs = [tile_q, full_k, full_v]
    out_shape = [jax.ShapeDtypeStruct((n_tok, 4 * qw), f32), jax.ShapeDtypeStruct((n_tok, 4 * qw), f32),
                 jax.ShapeDtypeStruct((n_tok, 512), f32)]
    if has_bias:
        in_specs += [pl.BlockSpec((TOK, LANE), lambda p, i: (i, 0)), pl.BlockSpec(bias_row.shape, _const_map(4))]
        args += [bias_col, bias_row]
        out_specs += [tile_v, pl.BlockSpec((HEADS, nq, 8, TOK), _const_map(4))]
        out_shape += [jax.ShapeDtypeStruct((n_tok, 512), f32), jax.ShapeDtypeStruct((HEADS, nq, 8, TOK), f32)]
    return pl.pallas_call(
        body, grid=(4, nq), in_specs=in_specs, out_specs=out_specs, out_shape=out_shape, name=name,
        compiler_params=pltpu.CompilerParams(dimension_semantics=("arbitrary", "arbitrary"), vmem_limit_bytes=VMEM_LIMIT),
    )(*args)


def _s5_disc(lr, li, ldt):
    dt = jnp.exp(ldt)
    mag = jnp.exp(lr * dt)
    a_re = mag * jnp.cos(li * dt)
    a_im = mag * jnp.sin(li * dt)
    den = lr * lr + li * li
    f_re = ((a_re - 1.0) * lr + a_im * li) / den
    f_im = (a_im * lr - (a_re - 1.0) * li) / den
    return a_re, a_im, f_re, f_im


def _s5_param_fwd(lr, li, ldt, b_re, b_im):
    def body(lr_ref, li_ref, ldt_ref, br_ref, bi_ref, ar_ref, ai_ref, bbr_ref, bbi_ref):
        a_re, a_im, f_re, f_im = _s5_disc(lr_ref[...], li_ref[...], ldt_ref[...])
        ar_ref[...] = a_re
        ai_ref[...] = a_im
        br, bi = br_ref[...], bi_ref[...]
        bbr_ref[...] = f_re * br - f_im * bi
        bbi_ref[...] = f_re * bi + f_im * br

    col = jax.ShapeDtypeStruct((S5_LANES, 1), f32)
    mat = jax.ShapeDtypeStruct((S5_LANES, S5_GROUP), f32)
    return pl.pallas_call(body, out_shape=[col, col, mat, mat], name="s5_param_fwd")(lr, li, ldt, b_re, b_im)


def _s5_param_bwd(lr, li, ldt, b_re, b_im, da_re, da_im, dbb_re, dbb_im):
    def body(lr_ref, li_ref, ldt_ref, br_ref, bi_ref, dar_ref, dai_ref, gbr_ref, gbi_ref,
             dlr_ref, dli_ref, dldt_ref, dbr_ref, dbi_ref):
        (a_re, a_im, f_re, f_im), vjp = jax.vjp(_s5_disc, lr_ref[...], li_ref[...], ldt_ref[...])
        br, bi, gr, gi = br_ref[...], bi_ref[...], gbr_ref[...], gbi_ref[...]
        dbr_ref[...] = f_re * gr + f_im * gi
        dbi_ref[...] = f_re * gi - f_im * gr
        dfr = jnp.sum(br * gr + bi * gi, axis=-1, keepdims=True)
        dfi = jnp.sum(br * gi - bi * gr, axis=-1, keepdims=True)
        dlr, dli, dldt = vjp((dar_ref[...], dai_ref[...], dfr, dfi))
        dlr_ref[...] = dlr
        dli_ref[...] = dli
        dldt_ref[...] = jnp.sum(dldt.reshape(S5_GROUPS, S5_STATE, 1), axis=1)

    col = jax.ShapeDtypeStruct((S5_LANES, 1), f32)
    mat = jax.ShapeDtypeStruct((S5_LANES, S5_GROUP), f32)
    return pl.pallas_call(body, out_shape=[col, col, jax.ShapeDtypeStruct((S5_GROUPS, 1), f32), mat, mat],
                          name="s5_param_bwd")(lr, li, ldt, b_re, b_im, da_re, da_im, dbb_re, dbb_im)


_SCAN_NB = 2


def _s5_scan(name, bu, a_re8, a_im8, *, reverse, n_tok):
    rows = n_tok // 8
    nb = _SCAN_NB

    def body(bu_ref, ar_ref, ai_ref, x_ref):
        a_r = [ar_ref[b] for b in range(nb)]
        a_i = [ai_ref[b] for b in range(nb)]
        zero = jnp.zeros((8, LANE), f32)
        one = jnp.ones((8, LANE), f32)

        def rows_at(r):
            rr = (rows - 1 - r) if reverse else r
            return pl.ds(rr, 8, stride=rows)

        def pass1(r, carry):
            out = []
            sl = rows_at(r)
            for b in range(nb):
                xr, xi, mr, mi = carry[b]
                nr = a_r[b] * xr - a_i[b] * xi + bu_ref[0, b, sl, :]
                ni = a_r[b] * xi + a_i[b] * xr + bu_ref[1, b, sl, :]
                x_ref[0, b, sl, :] = nr
                x_ref[1, b, sl, :] = ni
                out.append((nr, ni, a_r[b] * mr - a_i[b] * mi, a_r[b] * mi + a_i[b] * mr))
            return tuple(out)

        carry = lax.fori_loop(0, rows, pass1, tuple((zero, zero, one, zero) for _ in range(nb)))
        sub = lax.broadcasted_iota(jnp.int32, (8, LANE), 0)
        feed = []
        for b in range(nb):
            lr_, li_, pr, pi = carry[b]
            fr, fi = zero, zero
            for _ in range(7):
                tr = lr_ + pr * fr - pi * fi
                ti = li_ + pr * fi + pi * fr
                if reverse:
                    fr = jnp.where(sub < 7, pltpu.roll(tr, 7, 0), 0.0)
                    fi = jnp.where(sub < 7, pltpu.roll(ti, 7, 0), 0.0)
                else:
                    fr = jnp.where(sub > 0, pltpu.roll(tr, 1, 0), 0.0)
                    fi = jnp.where(sub > 0, pltpu.roll(ti, 1, 0), 0.0)
            feed.append((fr, fi))

        def pass2(r, carry):
            out = []
            sl = rows_at(r)
            for b in range(nb):
                mr, mi = carry[b]
                fr, fi = feed[b]
                x_ref[0, b, sl, :] += mr * fr - mi * fi
                x_ref[1, b, sl, :] += mr * fi + mi * fr
                out.append((a_r[b] * mr - a_i[b] * mi, a_r[b] * mi + a_i[b] * mr))
            return tuple(out)

        lax.fori_loop(0, rows, pass2, tuple((a_r[b], a_i[b]) for b in range(nb)))

    blk = pl.BlockSpec((2, nb, n_tok, LANE), lambda g: (0, g, 0, 0))
    ablk = pl.BlockSpec((nb, 8, LANE), lambda g: (g, 0, 0))
    return pl.pallas_call(
        body, grid=(S5_BLOCKS // nb,), in_specs=[blk, ablk, ablk], out_specs=blk,
        out_shape=jax.ShapeDtypeStruct((2, S5_BLOCKS, n_tok, LANE), f32), name=name,
        compiler_params=pltpu.CompilerParams(dimension_semantics=("arbitrary",), vmem_limit_bytes=VMEM_LIMIT),
    )(bu, a_re8, a_im8)


def _s5_da(xs, gx, *, n_tok):
    def body(x_ref, g_ref, o_ref):
        t = lax.broadcasted_iota(jnp.int32, (n_tok, LANE), 0)
        xr = jnp.where(t >= 1, pltpu.roll(x_ref[0, 0], 1, 0), 0.0)
        xi = jnp.where(t >= 1, pltpu.roll(x_ref[1, 0], 1, 0), 0.0)
        gr, gi = g_ref[0, 0], g_ref[1, 0]
        o_ref[0, 0:1, :] = jnp.sum(xr * gr + xi * gi, axis=0, keepdims=True)
        o_ref[0, 1:2, :] = jnp.sum(xr * gi - xi * gr, axis=0, keepdims=True)

    blk = pl.BlockSpec((2, 1, n_tok, LANE), lambda g: (0, g, 0, 0))
    return pl.pallas_call(
        body, grid=(S5_BLOCKS,), in_specs=[blk, blk], out_specs=pl.BlockSpec((1, 2, LANE), lambda g: (g, 0, 0)),
        out_shape=jax.ShapeDtypeStruct((S5_BLOCKS, 2, LANE), f32), name="s5_da",
        compiler_params=pltpu.CompilerParams(dimension_semantics=("arbitrary",), vmem_limit_bytes=VMEM_LIMIT),
    )(xs, gx)


S5_Q = 4


def _bd8(t):
    _, a, b = t.shape
    t = t.reshape(S5_Q, 8, a, 1, b)
    eye = jnp.eye(8, dtype=jnp.bool_).reshape(1, 8, 1, 8, 1)
    return jnp.where(eye, jnp.broadcast_to(t, (S5_Q, 8, a, 8, b)), jnp.zeros((), t.dtype)).reshape(S5_Q, 8 * a, 8 * b)


def _bd8_diag(m, a, b):
    m = m.reshape(S5_Q, 8, a, 8, b)
    eye = jnp.eye(8, dtype=jnp.bool_).reshape(1, 8, 1, 8, 1)
    return jnp.sum(jnp.where(eye, m, 0.0), axis=3).reshape(S5_GROUPS, a, b)


def _s5_expand(name, a, a_blk0, wq, *, n_tok):
    def body(a_ref, w_ref, o_ref):
        r = _nn(a_ref[...], w_ref[...])
        for k in range(4):
            o_ref[k] = r[:, k * LANE:(k + 1) * LANE]

    return pl.pallas_call(
        body, grid=(2, S5_Q),
        in_specs=[pl.BlockSpec((n_tok, LANE), lambda ri, q: (0, a_blk0 + q)),
                  pl.BlockSpec((None, None, LANE, 512), lambda ri, q: (ri, q, 0, 0))],
        out_specs=pl.BlockSpec((None, 4, n_tok, LANE), lambda ri, q: (ri, q, 0, 0)),
        out_shape=jax.ShapeDtypeStruct((2, S5_BLOCKS, n_tok, LANE), f32), name=name,
        compiler_params=pltpu.CompilerParams(dimension_semantics=("arbitrary", "arbitrary"), vmem_limit_bytes=VMEM_LIMIT),
    )(a, wq)


def _s5_contract(name, xs, wq, add, out_dtype, *, n_tok):
    def body(*refs):
        if add is None:
            x_ref, w_ref, o_ref, acc_ref = refs
        else:
            x_ref, w_ref, add_ref, o_ref, acc_ref = refs
        ri = pl.program_id(1)
        r = _nn(x_ref[0], w_ref[0:LANE, :])
        for k in range(1, 4):
            r = r + _nn(x_ref[k], w_ref[k * LANE:(k + 1) * LANE, :])

        @pl.when(ri == 0)
        def _():
            acc_ref[...] = r

        @pl.when(ri == 1)
        def _():
            tot = acc_ref[...] + r
            if add is not None:
                tot = tot + add_ref[...]
            o_ref[...] = tot.astype(o_ref.dtype)

    col = pl.BlockSpec((n_tok, LANE), lambda q, ri: (0, q))
    in_specs = [pl.BlockSpec((None, 4, n_tok, LANE), lambda q, ri: (ri, q, 0, 0)),
                pl.BlockSpec((None, None, 512, LANE), lambda q, ri: (ri, q, 0, 0))]
    args = [xs, wq]
    if add is not None:
        in_specs.append(col)
        args.append(add)
    return pl.pallas_call(
        body, grid=(S5_Q, 2), in_specs=in_specs, out_specs=col, out_shape=jax.ShapeDtypeStruct((n_tok, 512), out_dtype), name=name,
        scratch_shapes=[pltpu.VMEM((n_tok, LANE), f32)],
        compiler_params=pltpu.CompilerParams(dimension_semantics=("arbitrary", "arbitrary"), vmem_limit_bytes=VMEM_LIMIT),
    )(*args)


def _s5_wgrad_states(name, xs, d, *, n_tok):
    def body(x_ref, d_ref, o_ref):
        for k in range(4):
            o_ref[k * LANE:(k + 1) * LANE, :] = _tn(x_ref[k], d_ref[...])

    return pl.pallas_call(
        body, grid=(2, S5_Q),
        in_specs=[pl.BlockSpec((None, 4, n_tok, LANE), lambda ri, q: (ri, q, 0, 0)), pl.BlockSpec((n_tok, LANE), lambda ri, q: (0, q))],
        out_specs=pl.BlockSpec((None, None, 512, LANE), lambda ri, q: (ri, q, 0, 0)),
        out_shape=jax.ShapeDtypeStruct((2, S5_Q, 512, LANE), f32), name=name,
        compiler_params=pltpu.CompilerParams(dimension_semantics=("arbitrary", "arbitrary"), vmem_limit_bytes=VMEM_LIMIT),
    )(xs, d)


def _s5_wgrad_channels(name, a, a_blk0, gx, *, n_tok):
    def body(a_ref, g_ref, o_ref):
        for k in range(4):
            o_ref[:, k * LANE:(k + 1) * LANE] = _tn(a_ref[...], g_ref[k])

    return pl.pallas_call(
        body, grid=(2, S5_Q),
        in_specs=[pl.BlockSpec((n_tok, LANE), lambda ri, q: (0, a_blk0 + q)), pl.BlockSpec((None, 4, n_tok, LANE), lambda ri, q: (ri, q, 0, 0))],
        out_specs=pl.BlockSpec((None, None, LANE, 512), lambda ri, q: (ri, q, 0, 0)),
        out_shape=jax.ShapeDtypeStruct((2, S5_Q, LANE, 512), f32), name=name,
        compiler_params=pltpu.CompilerParams(dimension_semantics=("arbitrary", "arbitrary"), vmem_limit_bytes=VMEM_LIMIT),
    )(a, gx)


N_CHIPS = 4
_BIG_SHARD = {"w_in": (1, 1024, 1770, 1792), "mla_w_q_up": (1, 256, 192, 256), "mla_w_kv_up": (1, 128, 256, 256),
              "s5_w_glu": (0, 128, 512, 512), "w_branch_out": (0, 384, 1024, 1024), "w_out": (0, 256, 1024, 1024)}


def _to_shards(name, m):
    axis, r, c, cp = _BIG_SHARD[name]
    if axis == 0:
        return m.reshape(N_CHIPS, r, c)
    return jnp.stack([jnp.pad(m[:, j * c:(j + 1) * c], ((0, 0), (0, cp - c))) for j in range(N_CHIPS)])


def _from_shards(name, s):
    axis, r, c, cp = _BIG_SHARD[name]
    if axis == 0:
        return s.reshape(N_CHIPS * r, c)
    return jnp.concatenate([s[j, :, :c] for j in range(N_CHIPS)], axis=1)


def _pad_w_in(w):
    pieces, pos = [], 0
    for name in _PAD_ORDER:
        start, width, inner = _PAD[name]
        o0, ow = _ORIG[name]
        if start + inner > pos:
            pieces.append(jnp.zeros((w.shape[0], start + inner - pos), w.dtype))
        pieces.append(w[:, o0:o0 + ow])
        pos = start + inner + ow
    pieces.append(jnp.zeros((w.shape[0], NP - pos), w.dtype))
    return jnp.concatenate(pieces, axis=1)


def _layer_weights(l, small, big):
    w = {}
    w["w_in_shards"] = big["w_in"][l]
    w["w_in"] = _pad_w_in(_from_shards("w_in", big["w_in"][l]))
    wq = _from_shards("mla_w_q_up", big["mla_w_q_up"][l]).reshape(MLA_Q_RANK, HEADS, MLA_QK)
    w["wq"] = jnp.pad(wq, ((0, 0), (0, 0), (0, LANE - MLA_QK))).reshape(MLA_Q_RANK, HEADS * LANE)
    wkv = _from_shards("mla_w_kv_up", big["mla_w_kv_up"][l]).reshape(MLA_KV_RANK, HEADS, 128)
    wk = jnp.pad(wkv[:, :, :64], ((0, 0), (0, 0), (0, 64))).reshape(MLA_KV_RANK, HEADS * LANE)
    wv = wkv[:, :, 64:].reshape(MLA_KV_RANK, 512)
    w["wkv"] = jnp.concatenate([wk, wv], axis=1)
    w["w_glu"] = _from_shards("s5_w_glu", big["s5_w_glu"][l])
    w["wo"] = _from_shards("w_branch_out", big["w_branch_out"][l])
    w["w_out"] = _from_shards("w_out", big["w_out"][l])
    row = lambda a: a.reshape(1, -1).astype(f32)
    w["norm_g"] = row(small["norm_g"][l])
    w["qa_g"] = row(small["mla_q_a_norm"][l])
    w["kva_g"] = row(small["mla_kv_a_norm"][l])
    w["qn_g"] = jnp.pad(row(small["mla_q_norm"][l]), ((0, 0), (0, LANE - MLA_QK)))
    w["kn_g"] = jnp.pad(row(small["mla_k_norm"][l]), ((0, 0), (0, LANE - MLA_QK)))
    w["fq_g"] = jnp.tile(row(small["fox_q_norm"][l]), (1, 2))
    w["fk_g"] = jnp.tile(row(small["fox_k_norm"][l]), (1, 2))
    w["b_f"] = jnp.pad(row(small["fox_b_f"][l]), ((0, 0), (0, LANE - HEADS)))
    w["lr"] = small["s5_lambda_re"][l].reshape(S5_LANES, 1)
    w["li"] = small["s5_lambda_im"][l].reshape(S5_LANES, 1)
    w["ldt"] = jnp.repeat(small["s5_log_dt"][l], S5_STATE).reshape(S5_LANES, 1)
    w["b_re"] = small["s5_b_re"][l].reshape(S5_LANES, S5_GROUP)
    w["b_im"] = small["s5_b_im"][l].reshape(S5_LANES, S5_GROUP)
    w["c_re"] = small["s5_c_re"][l]
    w["c_im"] = small["s5_c_im"][l]
    w["s5_d"] = row(small["s5_d"][l])
    w["b_glu"] = row(small["s5_b_glu"][l])
    return w


def _fox_halves(x, lane):
    sq = x * x
    lo = jnp.sum(jnp.where(lane < 64, sq, 0.0), axis=-1, keepdims=True)
    hi = jnp.sum(sq, axis=-1, keepdims=True) - lo
    return jnp.where(lane < 64, lax.rsqrt(lo * (1.0 / 64) + EPS), lax.rsqrt(hi * (1.0 / 64) + EPS))


def _fox_halves_bwd(dy, x, r, g, lane):
    xh = x * r
    dxh = dy * g
    pr = dxh * xh
    lo = jnp.sum(jnp.where(lane < 64, pr, 0.0), axis=-1, keepdims=True)
    hi = jnp.sum(pr, axis=-1, keepdims=True) - lo
    mean = jnp.where(lane < 64, lo, hi) * (1.0 / 64)
    return r * (dxh - xh * mean), jnp.sum(dy * xh, axis=0, keepdims=True)


def _mla_recompute(cq, ckv, kpe, c, s1, s2, qa_g, kva_g, wq, wkv):
    cqn, r_cq = _rms(cq, qa_g, MLA_Q_RANK)
    ckvn, r_ckv = _rms(ckv, kva_g, MLA_KV_RANK)
    cqn_b = cqn.astype(bf16)
    ckvn_b = ckvn.astype(bf16)
    q_raw = _nn(cqn_b, wq)
    kv_raw = _nn(ckvn_b, wkv)
    kpe_rot = _rope(kpe, c, s1, s2)
    return cqn_b, r_cq, ckvn_b, r_ckv, q_raw, kv_raw, kpe_rot


def _layer_fwd(x, w, rope_tabs, n_tok):
    c_tab, s1_tab, s2_tab = rope_tabs
    saved = {"x": x}

    def norm_body(x_ref, g_ref, h_ref):
        h_ref[...] = _rms(x_ref[...], g_ref[...], D_MODEL)[0].astype(bf16)

    (h,) = _rowwise("norm_fwd", norm_body, n_tok, [(x, D_MODEL, 0)], [w["norm_g"]], [(D_MODEL, bf16)], [])
    proj = _mm_nn("in_proj", h, w["w_in"], m=n_tok, n=NP, k=D_MODEL, tm=n_tok, tn=512, tk=D_MODEL)
    saved["h"], saved["proj"] = h, proj

    def mla_prep_body(cq_ref, ckv_ref, kpe_ref, c_ref, s1_ref, s2_ref, qa_ref, kva_ref, wq_ref, wkv_ref, qn_g_ref, kn_g_ref,
                      qn_ref, kn_ref, v_ref):
        c, s1, s2 = c_ref[...], s1_ref[...], s2_ref[...]
        _, _, _, _, q_raw, kv_raw, kpe_rot = _mla_recompute(cq_ref[...], ckv_ref[...], kpe_ref[...], c, s1, s2,
                                                            qa_ref[...], kva_ref[...], wq_ref[...], wkv_ref[...])
        for hd in range(HEADS):
            sl = slice(hd * LANE, (hd + 1) * LANE)
            qn_ref[:, sl] = _rms(_rope(q_raw[:, sl], c, s1, s2), qn_g_ref[...], MLA_QK)[0].astype(bf16)
            kn_ref[:, sl] = _rms(kv_raw[:, sl] + kpe_rot, kn_g_ref[...], MLA_QK)[0].astype(bf16)
        v_ref[...] = kv_raw[:, HEADS * LANE:].astype(bf16)

    qn, kn, v_mla = _rowwise(
        "mla_prep", mla_prep_body, n_tok,
        [(proj, *_seg("cq")), (proj, *_seg("ckv")), (proj, *_seg("kpe")), (c_tab, LANE, 0), (s1_tab, LANE, 0), (s2_tab, LANE, 0)],
        [w["qa_g"], w["kva_g"], w["wq"], w["wkv"], w["qn_g"], w["kn_g"]],
        [(HEADS * LANE, bf16), (HEADS * LANE, bf16), (512, bf16)], [])
    y_mla, lse_mla = _attn_fwd("mla_attn_fwd", qn, kn, v_mla, None, None, mla=True, n_tok=n_tok)
    saved.update(qn=qn, kn=kn, v_mla=v_mla, y_mla=y_mla, lse_mla=lse_mla)

    def fox_prep_body(fq_ref, fk_ref, fv_ref, ff_ref, qg_ref, kg_ref, bf_ref, fqn_ref, fkn_ref, fvb_ref, logf_ref):
        lane = lax.broadcasted_iota(jnp.int32, (TOK, LANE), 1)
        for blk in range(4):
            sl = slice(blk * LANE, (blk + 1) * LANE)
            xq = fq_ref[:, sl]
            fqn_ref[:, sl] = (xq * _fox_halves(xq, lane) * qg_ref[...]).astype(bf16)
            xk = fk_ref[:, sl]
            fkn_ref[:, sl] = (xk * _fox_halves(xk, lane) * kg_ref[...]).astype(bf16)
        fvb_ref[...] = fv_ref[...].astype(bf16)
        z = ff_ref[...] + bf_ref[...]
        logf_ref[...] = jnp.minimum(z, 0.0) - jnp.log(1.0 + jnp.exp(-jnp.abs(z)))

    fqn, fkn, fvb, logf = _rowwise(
        "fox_prep", fox_prep_body, n_tok,
        [(proj, *_seg("fq")), (proj, *_seg("fk")), (proj, *_seg("fv")), (proj, *_seg("ff"))],
        [w["fq_g"], w["fk_g"], w["b_f"]],
        [(512, bf16), (512, bf16), (512, bf16), (LANE, f32)], [])

    def cum_body(x_ref, cum_ref, cum_t_ref):
        x = x_ref[...]
        t = lax.broadcasted_iota(jnp.int32, x.shape, 0)
        s = 1
        while s < n_tok:
            x = x + jnp.where(t >= s, pltpu.roll(x, s, 0), 0.0)
            s *= 2
        cum_ref[...] = x
        cum_t_ref[...] = x.T[0:HEADS, :]

    cum, cum_t = pl.pallas_call(cum_body, out_shape=[jax.ShapeDtypeStruct((n_tok, LANE), f32), jax.ShapeDtypeStruct((HEADS, n_tok), f32)],
                                name="fox_cum")(logf)
    cum_row = cum_t.reshape(HEADS, n_tok // TOK, 1, TOK)
    y_fox, lse_fox = _attn_fwd("fox_attn_fwd", fqn, fkn, fvb, cum, cum_row, mla=False, n_tok=n_tok)
    saved.update(fqn=fqn, fkn=fkn, fvb=fvb, cum=cum, cum_row=cum_row, y_fox=y_fox, lse_fox=lse_fox)

    a_re, a_im, bb_re, bb_im = _s5_param_fwd(w["lr"], w["li"], w["ldt"], w["b_re"], w["b_im"])
    per_group = lambda m: m.reshape(S5_GROUPS, S5_STATE, S5_GROUP)
    b_cn = jnp.stack([_bd8(jnp.swapaxes(per_group(bb_re), 1, 2)), _bd8(jnp.swapaxes(per_group(bb_im), 1, 2))]).astype(bf16)
    b_nc = jnp.stack([_bd8(per_group(bb_re)), _bd8(per_group(bb_im))]).astype(bf16)
    c_nc = jnp.stack([_bd8(jnp.swapaxes(w["c_re"], 1, 2)), -_bd8(jnp.swapaxes(w["c_im"], 1, 2))]).astype(bf16)
    c_cn = jnp.stack([_bd8(w["c_re"]), -_bd8(w["c_im"])]).astype(bf16)
    a_re8 = jnp.broadcast_to(a_re.reshape(S5_BLOCKS, 1, LANE), (S5_BLOCKS, 8, LANE))
    a_im8 = jnp.broadcast_to(a_im.reshape(S5_BLOCKS, 1, LANE), (S5_BLOCKS, 8, LANE))
    u_w, u_blk = _seg("s5u")
    u_blk128 = u_blk * (u_w // LANE)
    bu = _s5_expand("s5_bu", proj, u_blk128, b_cn, n_tok=n_tok)
    xs = _s5_scan("s5_scan_fwd", bu, a_re8, a_im8, reverse=False, n_tok=n_tok)
    ylin = _s5_contract("s5_y", xs, c_nc, None, f32, n_tok=n_tok)

    def s5_post_body(yl_ref, u_ref, d_ref, wg_ref, bg_ref, out_ref):
        y = yl_ref[...] + d_ref[...] * u_ref[...]
        z, _ = _gelu(y)
        out_ref[...] = z * _sigmoid(_nn(z, wg_ref[...]) + bg_ref[...])

    (y_s5,) = _rowwise("s5_post", s5_post_body, n_tok, [(ylin, 512, 0), (proj, u_w, u_blk)],
                       [w["s5_d"], w["w_glu"], w["b_glu"]], [(512, f32)], [])
    saved.update(xs=xs, ylin=ylin, y_s5=y_s5, b_nc=b_nc, c_cn=c_cn, a_re8=a_re8, a_im8=a_im8)

    def merge_body(ym_ref, yf_ref, ys_ref, gm_ref, gf_ref, gs_ref, mm_ref, mf_ref, ms_ref, x_ref, wo_ref, wout_ref, out_ref):
        merged = jnp.zeros((TOK, D_MODEL), f32)
        for b, (y_ref, g_ref, m_ref) in enumerate(((ym_ref, gm_ref, mm_ref), (yf_ref, gf_ref, mf_ref), (ys_ref, gs_ref, ms_ref))):
            g = g_ref[...]
            a = y_ref[...] * (g * _sigmoid(g))
            merged = merged + _sigmoid(m_ref[...]) * _nn(a, wo_ref[b * 512:(b + 1) * 512, :])
        out_ref[...] = x_ref[...] + _nn(merged, wout_ref[...])

    (out,) = _rowwise(
        "merge_fwd", merge_body, n_tok,
        [(y_mla, 512, 0), (y_fox, 512, 0), (y_s5, 512, 0), (proj, *_seg("g_mla")), (proj, *_seg("g_fox")), (proj, *_seg("g_s5")),
         (proj, *_seg("m_mla")), (proj, *_seg("m_fox")), (proj, *_seg("m_s5")), (x, D_MODEL, 0)],
        [w["wo"], w["w_out"]], [(D_MODEL, f32)], [])
    return out, saved


def _layer_bwd(dout, w, sv, rope_tabs, n_tok):
    c_tab, s1_tab, s2_tab = rope_tabs
    proj, x = sv["proj"], sv["x"]
    grads = {}

    def merge_bwd_body(ym_ref, yf_ref, ys_ref, gm_ref, gf_ref, gs_ref, mm_ref, mf_ref, ms_ref, do_ref, wo_ref, wout_ref,
                       dym_ref, dyf_ref, dys_ref, dgm_ref, dgf_ref, dgs_ref, dmm_ref, dmf_ref, dms_ref, dwo_ref, dwout_ref):
        do = do_ref[...]
        branches = ((ym_ref, gm_ref, mm_ref, dym_ref, dgm_ref, dmm_ref), (yf_ref, gf_ref, mf_ref, dyf_ref, dgf_ref, dmf_ref),
                    (ys_ref, gs_ref, ms_ref, dys_ref, dgs_ref, dms_ref))
        acts, outs, sigs = [], [], []
        merged = jnp.zeros((TOK, D_MODEL), f32)
        for b, (y_ref, g_ref, m_ref, _, _, _) in enumerate(branches):
            g = g_ref[...]
            a = (y_ref[...] * (g * _sigmoid(g))).astype(bf16)
            o = _nn(a, wo_ref[b * 512:(b + 1) * 512, :])
            s = _sigmoid(m_ref[...])
            merged = merged + s * o
            acts.append(a)
            outs.append(o)
            sigs.append(s)
        dmerged = _nt(do, wout_ref[...])
        _accumulate(dwout_ref, _tn(merged, do))
        dwo = []
        for b, (y_ref, g_ref, m_ref, dy_ref, dg_ref, dm_ref) in enumerate(branches):
            s, o = sigs[b], outs[b]
            dm_ref[...] = (dmerged * o * s * (1.0 - s)).astype(bf16)
            d_o = dmerged * s
            da = _nt(d_o, wo_ref[b * 512:(b + 1) * 512, :])
            dwo.append(_tn(acts[b], d_o))
            g = g_ref[...]
            sg = _sigmoid(g)
            dy_ref[...] = da * (g * sg)
            dg_ref[...] = (da * y_ref[...] * (sg * (1.0 + g * (1.0 - sg)))).astype(bf16)
        _accumulate(dwo_ref, jnp.concatenate(dwo, axis=0))

    (dy_mla, dy_fox, dy_s5, dg_mla, dg_fox, dg_s5, dm_mla, dm_fox, dm_s5, dwo, dwout) = _rowwise(
        "merge_bwd", merge_bwd_body, n_tok,
        [(sv["y_mla"], 512, 0), (sv["y_fox"], 512, 0), (sv["y_s5"], 512, 0), (proj, *_seg("g_mla")), (proj, *_seg("g_fox")),
         (proj, *_seg("g_s5")), (proj, *_seg("m_mla")), (proj, *_seg("m_fox")), (proj, *_seg("m_s5")), (dout, D_MODEL, 0)],
        [w["wo"], w["w_out"]],
        [(512, f32)] * 3 + [(512, bf16)] * 3 + [(D_MODEL, bf16)] * 3, [((1536, D_MODEL), f32), ((D_MODEL, D_MODEL), f32)])
    grads["w_branch_out"], grads["w_out"] = dwo, dwout

    u_w, u_blk = _seg("s5u")

    def s5_post_bwd_body(yl_ref, u_ref, do_ref, d_ref, wg_ref, bg_ref, dyl_ref, dus_ref, dd_ref, dwg_ref, dbg_ref):
        u = u_ref[...]
        y = yl_ref[...] + d_ref[...] * u
        z, t = _gelu(y)
        s = _sigmoid(_nn(z, wg_ref[...]) + bg_ref[...])
        do = do_ref[...]
        dgl = do * z * s * (1.0 - s)
        dz = do * s + _nt(dgl, wg_ref[...])
        dy = dz * _gelu_grad(y, t)
        dyl_ref[...] = dy.astype(bf16)
        dus_ref[...] = dy * d_ref[...]
        _accumulate(dd_ref, jnp.sum(dy * u, axis=0, keepdims=True))
        _accumulate(dwg_ref, _tn(z, dgl))
        _accumulate(dbg_ref, jnp.sum(dgl, axis=0, keepdims=True))

    dylin, du_skip, dd, dwglu, dbglu = _rowwise(
        "s5_post_bwd", s5_post_bwd_body, n_tok, [(sv["ylin"], 512, 0), (proj, u_w, u_blk), (dy_s5, 512, 0)],
        [w["s5_d"], w["w_glu"], w["b_glu"]], [(512, bf16), (512, f32)], [((1, 512), f32), ((512, 512), f32), ((1, 512), f32)])
    grads["s5_d"], grads["s5_w_glu"], grads["s5_b_glu"] = dd.reshape(512), dwglu, dbglu.reshape(512)

    dxs = _s5_expand("s5_dxs", dylin, 0, sv["c_cn"], n_tok=n_tok)
    dc_nc = _s5_wgrad_states("s5_dc", sv["xs"], dylin, n_tok=n_tok)
    gx = _s5_scan("s5_scan_bwd", dxs, sv["a_re8"], -sv["a_im8"], reverse=True, n_tok=n_tok)
    da = _s5_da(sv["xs"], gx, n_tok=n_tok)
    ds5u = _s5_contract("s5_du", gx, sv["b_nc"], du_skip, bf16, n_tok=n_tok)
    db_cn = _s5_wgrad_channels("s5_db", proj, u_blk * (u_w // LANE), gx, n_tok=n_tok)
    diag_b = lambda m: jnp.swapaxes(_bd8_diag(m, S5_GROUP, S5_STATE), 1, 2).reshape(S5_LANES, S5_GROUP)
    diag_c = lambda m: jnp.swapaxes(_bd8_diag(m, S5_STATE, S5_GROUP), 1, 2)
    dlr, dli, dldt, db_re, db_im = _s5_param_bwd(
        w["lr"], w["li"], w["ldt"], w["b_re"], w["b_im"], da[:, 0, :].reshape(S5_LANES, 1), da[:, 1, :].reshape(S5_LANES, 1),
        diag_b(db_cn[0]), diag_b(db_cn[1]))
    grads["s5_lambda_re"] = dlr.reshape(S5_GROUPS, S5_STATE)
    grads["s5_lambda_im"] = dli.reshape(S5_GROUPS, S5_STATE)
    grads["s5_log_dt"] = dldt.reshape(S5_GROUPS)
    grads["s5_b_re"] = db_re.reshape(S5_GROUPS, S5_STATE, S5_GROUP)
    grads["s5_b_im"] = db_im.reshape(S5_GROUPS, S5_STATE, S5_GROUP)
    grads["s5_c_re"] = diag_c(dc_nc[0])
    grads["s5_c_im"] = -diag_c(dc_nc[1])

    dfqn, dfkn, dfv, dbc, dbr = _attn_bwd("fox_attn_bwd", sv["fqn"], sv["fkn"], sv["fvb"], sv["y_fox"], sv["lse_fox"], dy_fox,
                                          sv["cum"], sv["cum_row"], mla=False, n_tok=n_tok)
    dcq8 = jnp.pad(dbc[:, 0::64], ((0, 0), (0, LANE - HEADS)))
    dck8 = jnp.pad(dbr[:, :, 0, :].reshape(HEADS, n_tok).T, ((0, 0), (0, LANE - HEADS)))

    def fox_gate_bwd_body(dq_ref, dk_ref, ff_ref, bf_ref, dff_ref, dbf_ref):
        xg = dq_ref[...] + dk_ref[...]
        t = lax.broadcasted_iota(jnp.int32, xg.shape, 0)
        s = 1
        while s < n_tok:
            xg = xg + jnp.where(t < n_tok - s, pltpu.roll(xg, n_tok - s, 0), 0.0)
            s *= 2
        dff = xg * _sigmoid(-(ff_ref[...] + bf_ref[...]))
        dff_ref[...] = dff.astype(bf16)
        dbf_ref[...] = jnp.sum(dff, axis=0, keepdims=True)

    ff_w, ff_blk = _seg("ff")
    dff, dbf = pl.pallas_call(
        fox_gate_bwd_body, grid=(1,),
        in_specs=[pl.BlockSpec((n_tok, LANE), lambda i: (0, 0)), pl.BlockSpec((n_tok, LANE), lambda i: (0, 0)),
                  pl.BlockSpec((n_tok, ff_w), lambda i: (0, ff_blk)), pl.BlockSpec((1, LANE), lambda i: (0, 0))],
        out_specs=[pl.BlockSpec((n_tok, LANE), lambda i: (0, 0)), pl.BlockSpec((1, LANE), lambda i: (0, 0))],
        out_shape=[jax.ShapeDtypeStruct((n_tok, LANE), bf16), jax.ShapeDtypeStruct((1, LANE), f32)], name="fox_gate_bwd",
    )(dcq8, dck8, proj, w["b_f"])
    grads["fox_b_f"] = dbf[0, :HEADS]

    def fox_prep_bwd_body(fq_ref, fk_ref, dqn_ref, dkn_ref, dv_ref, qg_ref, kg_ref, dfq_ref, dfk_ref, dfv_ref, dqg_ref, dkg_ref):
        lane = lax.broadcasted_iota(jnp.int32, (TOK, LANE), 1)
        dqg = jnp.zeros((1, LANE), f32)
        dkg = jnp.zeros((1, LANE), f32)
        for blk in range(4):
            sl = slice(blk * LANE, (blk + 1) * LANE)
            xq = fq_ref[:, sl]
            dx, dg = _fox_halves_bwd(dqn_ref[:, sl], xq, _fox_halves(xq, lane), qg_ref[...], lane)
            dfq_ref[:, sl] = dx.astype(bf16)
            dqg = dqg + dg
            xk = fk_ref[:, sl]
            dx, dg = _fox_halves_bwd(dkn_ref[:, sl], xk, _fox_halves(xk, lane), kg_ref[...], lane)
            dfk_ref[:, sl] = dx.astype(bf16)
            dkg = dkg + dg
        dfv_ref[...] = dv_ref[...].astype(bf16)
        _accumulate(dqg_ref, dqg + pltpu.roll(dqg, 64, 1))
        _accumulate(dkg_ref, dkg + pltpu.roll(dkg, 64, 1))

    dfq, dfk, dfvb, dfqg, dfkg = _rowwise(
        "fox_prep_bwd", fox_prep_bwd_body, n_tok,
        [(proj, *_seg("fq")), (proj, *_seg("fk")), (dfqn, 512, 0), (dfkn, 512, 0), (dfv, 512, 0)],
        [w["fq_g"], w["fk_g"]], [(512, bf16)] * 3, [((1, LANE), f32)] * 2)
    grads["fox_q_norm"], grads["fox_k_norm"] = dfqg[0, :FOX_DIM], dfkg[0, :FOX_DIM]

    dqn, dkn, dv_mla = _attn_bwd("mla_attn_bwd", sv["qn"], sv["kn"], sv["v_mla"], sv["y_mla"], sv["lse_mla"], dy_mla,
                                 None, None, mla=True, n_tok=n_tok)

    def mla_prep_bwd_body(cq_ref, ckv_ref, kpe_ref, c_ref, s1_ref, s2_ref, dqn_ref, dkn_ref, dv_ref,
                          qa_ref, kva_ref, wq_ref, wkv_ref, qn_g_ref, kn_g_ref,
                          dcq_ref, dckv_ref, dkpe_ref, dwq_ref, dwkv_ref, dqa_ref, dkva_ref, dqng_ref, dkng_ref):
        c, s1, s2 = c_ref[...], s1_ref[...], s2_ref[...]
        cq, ckv = cq_ref[...], ckv_ref[...]
        cqn_b, r_cq, ckvn_b, r_ckv, q_raw, kv_raw, kpe_rot = _mla_recompute(
            cq, ckv, kpe_ref[...], c, s1, s2, qa_ref[...], kva_ref[...], wq_ref[...], wkv_ref[...])
        lane = lax.broadcasted_iota(jnp.int32, (TOK, LANE), 1)
        dq_raw, dk_raw = [], []
        dkpe_rot = jnp.zeros((TOK, LANE), f32)
        dqng = jnp.zeros((1, LANE), f32)
        dkng = jnp.zeros((1, LANE), f32)
        for hd in range(HEADS):
            sl = slice(hd * LANE, (hd + 1) * LANE)
            q_rot = _rope(q_raw[:, sl], c, s1, s2)
            r = lax.rsqrt(jnp.sum(q_rot * q_rot, axis=-1, keepdims=True) * (1.0 / MLA_QK) + EPS)
            dx, dg = _rms_bwd(dqn_ref[:, sl], q_rot, r, qn_g_ref[...], MLA_QK)
            dqng = dqng + dg
            dq_raw.append(_rope_t(dx, c, s1, s2))
            k_full = kv_raw[:, sl] + kpe_rot
            r = lax.rsqrt(jnp.sum(k_full * k_full, axis=-1, keepdims=True) * (1.0 / MLA_QK) + EPS)
            dx, dg = _rms_bwd(dkn_ref[:, sl], k_full, r, kn_g_ref[...], MLA_QK)
            dkng = dkng + dg
            dk_raw.append(jnp.where(lane < 64, dx, 0.0))
            dkpe_rot = dkpe_rot + dx
        dkpe = _rope_t(dkpe_rot, c, s1, s2)
        dkpe_ref[...] = jnp.where(jnp.logical_and(lane >= 64, lane < 64 + ROPE), dkpe, 0.0).astype(bf16)
        dq_raw = jnp.concatenate(dq_raw, axis=1).astype(bf16)
        dkv_raw = jnp.concatenate(dk_raw + [dv_ref[...]], axis=1).astype(bf16)
        dcqn = _nt(dq_raw, wq_ref[...])
        dckvn = _nt(dkv_raw, wkv_ref[...])
        dx, dg = _rms_bwd(dcqn, cq, r_cq, qa_ref[...], MLA_Q_RANK)
        dcq_ref[...] = dx.astype(bf16)
        _accumulate(dqa_ref, dg)
        dx, dg = _rms_bwd(dckvn, ckv, r_ckv, kva_ref[...], MLA_KV_RANK)
        dckv_ref[...] = dx.astype(bf16)
        _accumulate(dkva_ref, dg)
        _accumulate(dwq_ref, _tn(cqn_b, dq_raw))
        _accumulate(dwkv_ref, _tn(ckvn_b, dkv_raw))
        _accumulate(dqng_ref, dqng)
        _accumulate(dkng_ref, dkng)

    dcq, dckv, dkpe, dwq, dwkv, dqa, dkva, dqng, dkng = _rowwise(
        "mla_prep_bwd", mla_prep_bwd_body, n_tok,
        [(proj, *_seg("cq")), (proj, *_seg("ckv")), (proj, *_seg("kpe")), (c_tab, LANE, 0), (s1_tab, LANE, 0), (s2_tab, LANE, 0),
         (dqn, HEADS * LANE, 0), (dkn, HEADS * LANE, 0), (dv_mla, 512, 0)],
        [w["qa_g"], w["kva_g"], w["wq"], w["wkv"], w["qn_g"], w["kn_g"]],
        [(MLA_Q_RANK, bf16), (LANE, bf16), (LANE, bf16)],
        [((MLA_Q_RANK, HEADS * LANE), f32), ((MLA_KV_RANK, HEADS * LANE + 512), f32), ((1, MLA_Q_RANK), f32),
         ((1, MLA_KV_RANK), f32), ((1, LANE), f32), ((1, LANE), f32)])
    grads["mla_w_q_up"] = dwq.reshape(MLA_Q_RANK, HEADS, LANE)[:, :, :MLA_QK].reshape(MLA_Q_RANK, HEADS * MLA_QK)
    dwk = dwkv[:, :HEADS * LANE].reshape(MLA_KV_RANK, HEADS, LANE)[:, :, :64]
    dwv = dwkv[:, HEADS * LANE:].reshape(MLA_KV_RANK, HEADS, 64)
    grads["mla_w_kv_up"] = jnp.concatenate([dwk, dwv], axis=2).reshape(MLA_KV_RANK, HEADS * 128)
    grads["mla_q_a_norm"], grads["mla_kv_a_norm"] = dqa.reshape(-1), dkva.reshape(-1)
    grads["mla_q_norm"], grads["mla_k_norm"] = dqng[0, :MLA_QK], dkng[0, :MLA_QK]

    _, _, shard_c, shard_cp = _BIG_SHARD["w_in"]
    kpe0 = _PAD["kpe"][2]
    dproj = jnp.concatenate([dcq, dckv, dkpe[:, kpe0:kpe0 + ROPE], dfq, dfk, dfvb, dff[:, :HEADS], ds5u, dg_mla, dg_fox, dg_s5,
                             dm_mla, dm_fox, dm_s5], axis=1)
    gap = jnp.zeros((n_tok, shard_cp - shard_c), bf16)
    dproj = jnp.concatenate([p for j in range(N_CHIPS) for p in (dproj[:, j * shard_c:(j + 1) * shard_c], gap)], axis=1)
    ct = 256
    per = shard_cp // ct
    dh = _mm("in_proj_dgrad", dproj, w["w_in_shards"], mode="nt", grid=(1, 1, N_CHIPS * per),
             a_spec=pl.BlockSpec((n_tok, ct), lambda i, j, kk: (0, kk)),
             b_spec=pl.BlockSpec((None, D_MODEL, ct), lambda i, j, kk: (kk // per, 0, kk % per)),
             o_spec=pl.BlockSpec((n_tok, D_MODEL), lambda i, j, kk: (0, 0)),
             out_shape=jax.ShapeDtypeStruct((n_tok, D_MODEL), f32), acc_shape=(n_tok, D_MODEL))
    grads["w_in"] = _mm("in_proj_wgrad", sv["h"], dproj, mode="tn", grid=(1, N_CHIPS * per, 1),
                        a_spec=pl.BlockSpec((n_tok, D_MODEL), lambda i, j, kk: (0, 0)),
                        b_spec=pl.BlockSpec((n_tok, ct), lambda i, j, kk: (0, j)),
                        o_spec=pl.BlockSpec((None, D_MODEL, ct), lambda i, j, kk: (j // per, 0, j % per)),
                        out_shape=jax.ShapeDtypeStruct((N_CHIPS, D_MODEL, shard_cp), f32), acc_shape=(D_MODEL, ct))

    def norm_bwd_body(dh_ref, x_ref, do_ref, g_ref, dx_ref, dg_ref):
        xv = x_ref[...]
        r = lax.rsqrt(jnp.sum(xv * xv, axis=-1, keepdims=True) * (1.0 / D_MODEL) + EPS)
        dx, dg = _rms_bwd(dh_ref[...], xv, r, g_ref[...], D_MODEL)
        dx_ref[...] = do_ref[...] + dx
        _accumulate(dg_ref, dg)

    dx, dng = _rowwise("norm_bwd", norm_bwd_body, n_tok, [(dh, D_MODEL, 0), (x, D_MODEL, 0), (dout, D_MODEL, 0)],
                       [w["norm_g"]], [(D_MODEL, f32)], [((1, D_MODEL), f32)])
    grads["norm_g"] = dng.reshape(D_MODEL)
    return dx, grads


def _rope_tables(positions):
    inv = 1.0 / (ROPE_THETA ** (jnp.arange(0, ROPE, 2, dtype=f32) / ROPE))
    ang = positions.astype(f32).reshape(-1, 1) * inv
    cos, sin = jnp.cos(ang), jnp.sin(ang)
    n = ang.shape[0]
    z16, z32, z64 = jnp.zeros((n, 16), f32), jnp.zeros((n, 32), f32), jnp.zeros((n, 64), f32)
    c = jnp.concatenate([jnp.ones((n, 64), f32), cos, cos, z32], axis=1)
    s1 = jnp.concatenate([z64, -sin, z16, z32], axis=1)
    s2 = jnp.concatenate([z64, z16, sin, z32], axis=1)
    return c, s1, s2


BIG = ("w_in", "mla_w_q_up", "mla_w_kv_up", "s5_w_glu", "w_branch_out", "w_out")
SMALL = ("norm_g", "mla_q_a_norm", "mla_kv_a_norm", "mla_q_norm", "mla_k_norm", "fox_b_f", "fox_q_norm", "fox_k_norm",
         "s5_lambda_re", "s5_lambda_im", "s5_log_dt", "s5_b_re", "s5_b_im", "s5_c_re", "s5_c_im", "s5_d", "s5_b_glu")
WEIGHTS = ("norm_g", "w_in", "mla_q_a_norm", "mla_w_q_up", "mla_kv_a_norm", "mla_w_kv_up", "mla_q_norm", "mla_k_norm",
           "fox_b_f", "fox_q_norm", "fox_k_norm", "s5_lambda_re", "s5_lambda_im", "s5_log_dt", "s5_b_re", "s5_b_im",
           "s5_c_re", "s5_c_im", "s5_d", "s5_w_glu", "s5_b_glu", "w_branch_out", "w_out")


def _local_step(x, positions, loss_target, small, big):
    n_tok = x.shape[0]
    tabs = _rope_tables(positions)
    ws, saves = [], []
    hcur = x
    for l in range(DEPTH):
        w = _layer_weights(l, small, big)
        hcur, sv = _layer_fwd(hcur, w, tabs, n_tok)
        ws.append(w)
        saves.append(sv)

    def loss_body(y_ref, t_ref, d_ref, l_ref):
        err = y_ref[...] - t_ref[...]
        d_ref[...] = err * (1.0 / D_MODEL)
        tot = jnp.sum(jnp.sum(err * err, axis=-1, keepdims=True), axis=0, keepdims=True)
        _accumulate(l_ref, jnp.broadcast_to(tot * (0.5 / D_MODEL), (1, LANE)))

    dcur, loss = _rowwise("loss", loss_body, n_tok, [(hcur, D_MODEL, 0), (loss_target, D_MODEL, 0)], [], [(D_MODEL, f32)],
                          [((1, LANE), f32)])
    layer_grads = [None] * DEPTH
    for l in reversed(range(DEPTH)):
        dcur, layer_grads[l] = _layer_bwd(dcur, ws[l], saves[l], tabs, n_tok)
    grads = {n: jnp.stack([layer_grads[l][n] for l in range(DEPTH)]) for n in WEIGHTS}
    return loss[0, 0], dcur, grads


N_DEV = 8
_ANY = pl.BlockSpec(memory_space=pl.ANY)
_MESH = pl.DeviceIdType.MESH


def _all_gather8(name, blk):
    m = blk.shape[0]

    def body(x_ref, out_ref, send_sems, recv_sems, local_sem):
        x, y, c = lax.axis_index("x"), lax.axis_index("y"), lax.axis_index("c")
        me, sibling = (x, y, c), (x, y, 1 - c)
        chips = [(1 - x, y), (x, 1 - y), (1 - x, 1 - y)]

        def slot(px, py, pc):
            return out_ref.at[4 * px + 2 * py + pc]

        def copy(k, block, to, src=None):
            return pltpu.make_async_remote_copy(
                src_ref=slot(*block) if src is None else src, dst_ref=slot(*block),
                send_sem=send_sems.at[k], recv_sem=recv_sems.at[k], device_id=to, device_id_type=_MESH)

        mine = pltpu.make_async_copy(x_ref, slot(*me), local_sem)
        mine.start()
        first = [copy(0, me, sibling, src=x_ref)]
        first += [copy(1 + j, me, (*chip, c), src=x_ref) for j, chip in enumerate(chips)]
        for cp in first:
            cp.start()
        passed = [copy(4 + j, (*chip, c), sibling) for j, chip in enumerate(chips)]
        for j, chip in enumerate(chips):
            copy(1 + j, (*chip, c), me).wait_recv()
            passed[j].start()
        copy(0, sibling, me).wait_recv()
        for j, chip in enumerate(chips):
            copy(4 + j, (*chip, 1 - c), me).wait_recv()
        for cp in first + passed:
            cp.wait_send()
        mine.wait()

    return pl.pallas_call(
        body, out_shape=jax.ShapeDtypeStruct((N_DEV, m, LANE), blk.dtype), in_specs=[_ANY], out_specs=_ANY, name=name,
        scratch_shapes=[pltpu.SemaphoreType.DMA((7,)), pltpu.SemaphoreType.DMA((7,)), pltpu.SemaphoreType.DMA],
    )(blk)


def _gather_layers(name, shards):
    n = len(shards)

    def body(*refs):
        x_refs, out_refs = refs[:n], refs[n:2 * n]
        send_sems, recv_sems, local_sems = refs[2 * n:]
        x, y, c = lax.axis_index("x"), lax.axis_index("y"), lax.axis_index("c")
        me, sibling = (x, y, c), (x, y, 1 - c)
        chips = [(1 - x, y), (x, 1 - y), (1 - x, 1 - y)]

        def copy(w, k, block, to, src=None):
            px, py, pc = block
            slot = out_refs[w].at[pc, 2 * px + py]
            return pltpu.make_async_remote_copy(
                src_ref=slot if src is None else src, dst_ref=slot, send_sem=send_sems.at[7 * w + k],
                recv_sem=recv_sems.at[7 * w + k], device_id=to, device_id_type=_MESH)

        started, local = [], []
        for w in range(n):
            src = x_refs[w].at[c]
            mine = pltpu.make_async_copy(src, out_refs[w].at[c, 2 * x + y], local_sems.at[w])
            mine.start()
            local.append(mine)
            first = [copy(w, 0, me, sibling, src=src)] + [copy(w, 1 + j, me, (*chip, c), src=src) for j, chip in enumerate(chips)]
            for cp in first:
                cp.start()
            started += first
        for w in range(n):
            for j, chip in enumerate(chips):
                copy(w, 1 + j, (*chip, c), me).wait_recv()
                onward = copy(w, 4 + j, (*chip, c), sibling)
                onward.start()
                started.append(onward)
        for w in range(n):
            copy(w, 0, sibling, me).wait_recv()
            for j, chip in enumerate(chips):
                copy(w, 4 + j, (*chip, 1 - c), me).wait_recv()
        for cp in started:
            cp.wait_send()
        for cp in local:
            cp.wait()

    return pl.pallas_call(
        body, out_shape=[jax.ShapeDtypeStruct((2, N_CHIPS) + s.shape[1:], s.dtype) for s in shards],
        in_specs=[_ANY] * n, out_specs=[_ANY] * n, name=name,
        scratch_shapes=[pltpu.SemaphoreType.DMA((7 * n,)), pltpu.SemaphoreType.DMA((7 * n,)), pltpu.SemaphoreType.DMA((n,))],
    )(*shards)


def _swap_layers(name, parts):
    n = len(parts)

    def body(*refs):
        p_refs, keep_refs, got_refs = refs[:n], refs[n:2 * n], refs[2 * n:3 * n]
        send_sems, recv_sems, local_sems = refs[3 * n:]
        x, y, c = lax.axis_index("x"), lax.axis_index("y"), lax.axis_index("c")
        copies, local = [], []
        for w in range(n):
            for j in range(N_CHIPS):
                cp = pltpu.make_async_remote_copy(
                    src_ref=p_refs[w].at[1 - c, j], dst_ref=got_refs[w].at[j], send_sem=send_sems.at[N_CHIPS * w + j],
                    recv_sem=recv_sems.at[N_CHIPS * w + j], device_id=(x, y, 1 - c), device_id_type=_MESH)
                cp.start()
                copies.append(cp)
                lc = pltpu.make_async_copy(p_refs[w].at[c, j], keep_refs[w].at[j], local_sems.at[N_CHIPS * w + j])
                lc.start()
                local.append(lc)
        for cp in copies:
            cp.wait()
        for lc in local:
            lc.wait()

    half = [jax.ShapeDtypeStruct((N_CHIPS,) + p.shape[2:], p.dtype) for p in parts]
    res = pl.pallas_call(
        body, out_shape=half + half, in_specs=[_ANY] * n, out_specs=[_ANY] * (2 * n), name=name,
        scratch_shapes=[pltpu.SemaphoreType.DMA((N_CHIPS * n,)), pltpu.SemaphoreType.DMA((N_CHIPS * n,)),
                        pltpu.SemaphoreType.DMA((N_CHIPS * n,))],
    )(*parts)
    return res[:n], res[n:]


def _scatter_to_chips(name, parts):
    n = len(parts)

    def body(*refs):
        p_refs, out_refs = refs[:n], refs[n:2 * n]
        send_sems, recv_sems, local_sems = refs[2 * n:]
        x, y, c = lax.axis_index("x"), lax.axis_index("y"), lax.axis_index("c")
        jme = 2 * x + y
        chips = [(1 - x, y), (x, 1 - y), (1 - x, 1 - y)]
        sends, local = [], []
        for w in range(n):
            mine = pltpu.make_async_copy(p_refs[w].at[jme], out_refs[w].at[jme], local_sems.at[w])
            mine.start()
            local.append(mine)
            for k, (tx, ty) in enumerate(chips):
                cp = pltpu.make_async_remote_copy(
                    src_ref=p_refs[w].at[2 * tx + ty], dst_ref=out_refs[w].at[jme], send_sem=send_sems.at[3 * w + k],
                    recv_sem=recv_sems.at[3 * w + k], device_id=(tx, ty, c), device_id_type=_MESH)
                cp.start()
                sends.append(cp)
        for w in range(n):
            for k, (tx, ty) in enumerate(chips):
                pltpu.make_async_remote_copy(
                    src_ref=p_refs[w].at[jme], dst_ref=out_refs[w].at[2 * tx + ty], send_sem=send_sems.at[3 * w + k],
                    recv_sem=recv_sems.at[3 * w + k], device_id=(tx, ty, c), device_id_type=_MESH).wait_recv()
        for cp in sends:
            cp.wait_send()
        for cp in local:
            cp.wait()

    return pl.pallas_call(
        body, out_shape=[jax.ShapeDtypeStruct(p.shape, p.dtype) for p in parts], in_specs=[_ANY] * n, out_specs=[_ANY] * n, name=name,
        scratch_shapes=[pltpu.SemaphoreType.DMA((3 * n,)), pltpu.SemaphoreType.DMA((3 * n,)), pltpu.SemaphoreType.DMA((n,))],
    )(*parts)


def _share_layers(name, totals):
    n = len(totals)

    def body(*refs):
        t_refs, out_refs = refs[:n], refs[n:2 * n]
        send_sems, recv_sems, local_sems = refs[2 * n:]
        x, y, c = lax.axis_index("x"), lax.axis_index("y"), lax.axis_index("c")
        copies, local = [], []
        for w in range(n):
            lc = pltpu.make_async_copy(t_refs[w], out_refs[w].at[c], local_sems.at[w])
            lc.start()
            local.append(lc)
            cp = pltpu.make_async_remote_copy(src_ref=t_refs[w], dst_ref=out_refs[w].at[c], send_sem=send_sems.at[w],
                                              recv_sem=recv_sems.at[w], device_id=(x, y, 1 - c), device_id_type=_MESH)
            cp.start()
            copies.append(cp)
        for w in range(n):
            pltpu.make_async_remote_copy(src_ref=t_refs[w], dst_ref=out_refs[w].at[1 - c], send_sem=send_sems.at[w],
                                         recv_sem=recv_sems.at[w], device_id=(x, y, 1 - c), device_id_type=_MESH).wait_recv()
        for cp in copies:
            cp.wait_send()
        for lc in local:
            lc.wait()

    return pl.pallas_call(
        body, out_shape=[jax.ShapeDtypeStruct((2,) + t.shape, t.dtype) for t in totals], in_specs=[_ANY] * n, out_specs=[_ANY] * n,
        name=name, scratch_shapes=[pltpu.SemaphoreType.DMA((n,)), pltpu.SemaphoreType.DMA((n,)), pltpu.SemaphoreType.DMA((n,))],
    )(*totals)


def _row_tile(rows, cols):
    best = 16
    for t in range(16, rows + 1, 16):
        if rows % t == 0 and t * cols * 4 <= 2 * 1024 * 1024:
            best = t
    return best


def _add_pair(name, a, b, out_dtype):
    _, r, c = a.shape
    t = _row_tile(r, c)

    def body(a_ref, b_ref, o_ref):
        o_ref[...] = (a_ref[...] + b_ref[...]).astype(o_ref.dtype)

    spec = pl.BlockSpec((None, t, c), lambda j, i: (j, i, 0))
    return pl.pallas_call(body, grid=(N_CHIPS, r // t), in_specs=[spec, spec], out_specs=spec,
                          out_shape=jax.ShapeDtypeStruct(a.shape, out_dtype), name=name,
                          compiler_params=pltpu.CompilerParams(dimension_semantics=("arbitrary", "arbitrary")))(a, b)


def _add_four(name, a):
    _, r, c = a.shape
    t = _row_tile(r, c)

    def body(a0, a1, a2, a3, o_ref):
        o_ref[...] = ((a0[...].astype(f32) + a1[...].astype(f32)) + a2[...].astype(f32)) + a3[...].astype(f32)

    specs = [pl.BlockSpec((None, t, c), functools.partial(lambda i, k: (k, i, 0), k=k)) for k in range(N_CHIPS)]
    return pl.pallas_call(body, grid=(r // t,), in_specs=specs, out_specs=pl.BlockSpec((t, c), lambda i: (i, 0)),
                          out_shape=jax.ShapeDtypeStruct((r, c), f32), name=name,
                          compiler_params=pltpu.CompilerParams(dimension_semantics=("arbitrary",)))(a, a, a, a)


def _adamw(name, w, g, m, v, tile):
    r, c = w.shape
    c1 = 1.0 - ADAM_B1 ** ADAM_STEP
    c2 = 1.0 - ADAM_B2 ** ADAM_STEP

    def body(w_ref, g_ref, m_ref, v_ref, d_ref, nm_ref, nv_ref):
        gv = g_ref[...]
        nm = ADAM_B1 * m_ref[...] + (1.0 - ADAM_B1) * gv
        nv = ADAM_B2 * v_ref[...] + (1.0 - ADAM_B2) * (gv * gv)
        m_hat = nm / c1
        v_hat = nv / c2
        d_ref[...] = -ADAM_LR * (m_hat / (jnp.sqrt(v_hat) + ADAM_EPS) + ADAM_WD * w_ref[...])
        nm_ref[...] = nm
        nv_ref[...] = nv

    spec = pl.BlockSpec((tile, c), lambda i: (i, 0))
    sds = jax.ShapeDtypeStruct((r, c), f32)
    return pl.pallas_call(body, grid=(r // tile,), in_specs=[spec] * 4, out_specs=[spec] * 3, out_shape=[sds] * 3, name=name,
                          compiler_params=pltpu.CompilerParams(dimension_semantics=("arbitrary",), vmem_limit_bytes=VMEM_LIMIT),
                          )(w, g, m, v)


def _pad_rows(flat, rows):
    return jnp.pad(flat, (0, rows * LANE - flat.shape[0])).reshape(rows, LANE)


def kernel(x, positions, norm_g, w_in, mla_q_a_norm, mla_w_q_up, mla_kv_a_norm, mla_w_kv_up, mla_q_norm, mla_k_norm, fox_b_f, fox_q_norm, fox_k_norm, s5_lambda_re, s5_lambda_im, s5_log_dt, s5_b_re, s5_b_im, s5_c_re, s5_c_im, s5_d, s5_w_glu, s5_b_glu, w_branch_out, w_out, loss_target, m_norm_g, m_w_in, m_mla_q_a_norm, m_mla_w_q_up, m_mla_kv_a_norm, m_mla_w_kv_up, m_mla_q_norm, m_mla_k_norm, m_fox_b_f, m_fox_q_norm, m_fox_k_norm, m_s5_lambda_re, m_s5_lambda_im, m_s5_log_dt, m_s5_b_re, m_s5_b_im, m_s5_c_re, m_s5_c_im, m_s5_d, m_s5_w_glu, m_s5_b_glu, m_w_branch_out, m_w_out, v_norm_g, v_w_in, v_mla_q_a_norm, v_mla_w_q_up, v_mla_kv_a_norm, v_mla_w_kv_up, v_mla_q_norm, v_mla_k_norm, v_fox_b_f, v_fox_q_norm, v_fox_k_norm, v_s5_lambda_re, v_s5_lambda_im, v_s5_log_dt, v_s5_b_re, v_s5_b_im, v_s5_c_re, v_s5_c_im, v_s5_d, v_s5_w_glu, v_s5_b_glu, v_w_branch_out, v_w_out):
    given = dict(locals())
    wts = {n: given[n] for n in WEIGHTS}
    mom1 = {n: given["m_" + n] for n in WEIGHTS}
    mom2 = {n: given["v_" + n] for n in WEIGHTS}

    def lanes(n, a):
        _, _, c, cp = _BIG_SHARD[n]
        return jnp.pad(a, ((0, 0), (0, 0), (0, cp - c)))

    gathered = _gather_layers("gather_weights", [lanes(n, wts[n].astype(bf16)) for n in BIG])
    big = dict(zip(BIG, gathered))
    small = {n: wts[n] for n in SMALL}

    loss_local, grad_x, grads = _local_step(x[0], positions, loss_target[0], small, big)
    loss = lax.psum(loss_local, ("x", "y", "c"))

    small_flat = jnp.concatenate([grads[n].reshape(-1) for n in SMALL])
    small_rows = -(-small_flat.shape[0] // (N_DEV * 16 * LANE)) * 16
    parts = [grads[n] if n == "w_in" else jnp.stack([_to_shards(n, grads[n][l]) for l in range(DEPTH)]) for n in BIG]
    parts.append(jnp.swapaxes(_pad_rows(small_flat, N_DEV * small_rows).reshape(N_CHIPS, 2, small_rows, LANE), 0, 1))
    keep, got = _swap_layers("grads_to_sibling", parts)
    hop = [bf16] * len(BIG) + [f32]
    pair = [_add_pair("grads_pair_sum_%d" % i, a, b, dt) for i, (a, b, dt) in enumerate(zip(keep, got, hop))]
    landed = _scatter_to_chips("grads_to_chips", pair)
    total = [_add_four("grads_chip_sum_%d" % i, a) for i, a in enumerate(landed)]
    shared = _share_layers("grads_share", total[:-1])
    small_all = _all_gather8("gather_small_grads", total[-1]).reshape(-1)

    g_out = {n: s[:, :, :_BIG_SHARD[n][2]] for n, s in zip(BIG, shared)}
    pos = 0
    for n in SMALL:
        g_out[n] = small_all[pos:pos + wts[n].size].reshape(wts[n].shape)
        pos += wts[n].size

    delta, new_m, new_v = {}, {}, {}
    for n in BIG:
        shp = wts[n].shape
        as2d = lambda a: a.reshape(-1, shp[-1])
        d_, m_, v_ = _adamw("adamw_" + n, as2d(wts[n]), as2d(g_out[n]), as2d(mom1[n]), as2d(mom2[n]), tile=256)
        delta[n], new_m[n], new_v[n] = d_.reshape(shp), m_.reshape(shp), v_.reshape(shp)
    n_small_rows = N_DEV * small_rows
    pack = lambda d: _pad_rows(jnp.concatenate([d[n].reshape(-1) for n in SMALL]), n_small_rows)
    packed = _adamw("adamw_small", pack(wts), small_all.reshape(n_small_rows, LANE), pack(mom1), pack(mom2), tile=n_small_rows)
    for out, res in zip((delta, new_m, new_v), packed):
        res = res.reshape(-1)
        pos = 0
        for n in SMALL:
            out[n] = res[pos:pos + wts[n].size].reshape(wts[n].shape)
            pos += wts[n].size

    return (loss, grad_x[None], *[g_out[n] for n in WEIGHTS], *[delta[n] for n in WEIGHTS],
            *[new_m[n] for n in WEIGHTS], *[new_v[n] for n in WEIGHTS])
```

```python
import functools
import math

import jax
import jax.numpy as jnp
from jax import lax
from jax.experimental import pallas as pl
from jax.experimental.pallas import tpu as pltpu

f32 = jnp.float32
bf16 = jnp.bfloat16

D_MODEL = 1024
DEPTH = 2
EPS = 1e-6
HEADS = 8
MLA_QK = 96
MLA_Q_RANK = 256
MLA_KV_RANK = 128
ROPE = 32
ROPE_THETA = 10000.0
FOX_DIM = 64
S5_GROUPS = 32
S5_GROUP = 16
S5_STATE = 64
S5_LANES = S5_GROUPS * S5_STATE
LANE = 128
S5_BLOCKS = S5_LANES // LANE
IN_WIDTH = 7080
TOK = 256
VMEM_LIMIT = 56 * 1024 * 1024

ADAM_LR = 0.001
ADAM_B1 = 0.9
ADAM_B2 = 0.999
ADAM_EPS = 1e-08
ADAM_WD = 0.01
ADAM_STEP = 10

_ORIG = {}
_off = 0
for _n, _w in (("cq", 256), ("ckv", 128), ("kpe", 32), ("fq", 512), ("fk", 512), ("fv", 512), ("ff", 8), ("s5u", 512),
               ("g_mla", 512), ("g_fox", 512), ("g_s5", 512), ("m_mla", 1024), ("m_fox", 1024), ("m_s5", 1024)):
    _ORIG[_n] = (_off, _w)
    _off += _w
_PAD = {"m_mla": (0, 1024, 0), "m_fox": (1024, 1024, 0), "m_s5": (2048, 1024, 0),
        "fq": (3072, 512, 0), "fk": (3584, 512, 0), "fv": (4096, 512, 0), "s5u": (4608, 512, 0),
        "g_mla": (5120, 512, 0), "g_fox": (5632, 512, 0), "g_s5": (6144, 512, 0),
        "cq": (6656, 256, 0), "ckv": (6912, 128, 0), "kpe": (7040, 128, 64), "ff": (7168, 128, 0)}
NP = 7680
_PAD_ORDER = ("m_mla", "m_fox", "m_s5", "fq", "fk", "fv", "s5u", "g_mla", "g_fox", "g_s5", "cq", "ckv", "kpe", "ff")


def _seg(name):
    start, width, _ = _PAD[name]
    return width, start // width


def _nn(a, b):
    return lax.dot_general(a.astype(bf16), b.astype(bf16), (((1,), (0,)), ((), ())), preferred_element_type=f32)


def _nt(a, b):
    return lax.dot_general(a.astype(bf16), b.astype(bf16), (((1,), (1,)), ((), ())), preferred_element_type=f32)


def _tn(a, b):
    return lax.dot_general(a.astype(bf16), b.astype(bf16), (((0,), (0,)), ((), ())), preferred_element_type=f32)


def _rms(x, g, n):
    r = lax.rsqrt(jnp.sum(x * x, axis=-1, keepdims=True) * (1.0 / n) + EPS)
    return x * r * g, r


def _rms_bwd(dy, x, r, g, n):
    xh = x * r
    dg = jnp.sum(dy * xh, axis=0, keepdims=True)
    dxh = dy * g
    dx = r * (dxh - xh * (jnp.sum(dxh * xh, axis=-1, keepdims=True) * (1.0 / n)))
    return dx, dg


def _sigmoid(x):
    return 1.0 / (1.0 + jnp.exp(-x))


_GELU_C = math.sqrt(2.0 / math.pi)


def _gelu(x):
    t = jnp.tanh(_GELU_C * (x + 0.044715 * x * x * x))
    return 0.5 * x * (1.0 + t), t


def _gelu_grad(x, t):
    return 0.5 * (1.0 + t) + 0.5 * x * (1.0 - t * t) * _GELU_C * (1.0 + 3.0 * 0.044715 * x * x)


def _accumulate(ref, val):
    i = pl.program_id(0)

    @pl.when(i == 0)
    def _():
        ref[...] = val

    @pl.when(i > 0)
    def _():
        ref[...] += val


def _rope(x, c, s1, s2):
    return x * c + pltpu.roll(x, LANE - 16, 1) * s1 + pltpu.roll(x, 16, 1) * s2


def _rope_t(d, c, s1, s2):
    return d * c + pltpu.roll(d * s1, 16, 1) + pltpu.roll(d * s2, LANE - 16, 1)


def _const_map(ndim):
    return lambda *_: (0,) * ndim


def _rowwise(name, body, n_tok, tiled_in, full_in, tiled_out, acc_out, tile=TOK):
    in_specs, args = [], []
    for arr, width, blk in tiled_in:
        in_specs.append(pl.BlockSpec((tile, width), functools.partial(lambda i, b: (i, b), b=blk)))
        args.append(arr)
    for arr in full_in:
        in_specs.append(pl.BlockSpec(arr.shape, _const_map(arr.ndim)))
        args.append(arr)
    out_specs, out_shape = [], []
    for width, dt in tiled_out:
        out_specs.append(pl.BlockSpec((tile, width), lambda i: (i, 0)))
        out_shape.append(jax.ShapeDtypeStruct((n_tok, width), dt))
    for shape, dt in acc_out:
        out_specs.append(pl.BlockSpec(shape, _const_map(len(shape))))
        out_shape.append(jax.ShapeDtypeStruct(shape, dt))
    return pl.pallas_call(
        body, grid=(n_tok // tile,), in_specs=in_specs, out_specs=out_specs, out_shape=out_shape, name=name,
        compiler_params=pltpu.CompilerParams(dimension_semantics=("arbitrary",), vmem_limit_bytes=VMEM_LIMIT),
    )(*args)


def _mm(name, a, b, *, mode, grid, a_spec, b_spec, o_spec, out_shape, acc_shape, add=None, add_spec=None):
    nk = grid[2]

    def body(*refs):
        if add is None:
            a_ref, b_ref, o_ref, acc_ref = refs
        else:
            a_ref, b_ref, add_ref, o_ref, acc_ref = refs
        k = pl.program_id(2)

        @pl.when(k == 0)
        def _():
            acc_ref[...] = jnp.zeros_like(acc_ref)

        acc_ref[...] += {"nn": _nn, "nt": _nt, "tn": _tn}[mode](a_ref[...], b_ref[...])

        @pl.when(k == nk - 1)
        def _():
            r = acc_ref[...]
            if add is not None:
                r = r + add_ref[...]
            o_ref[...] = r.astype(o_ref.dtype)

    in_specs = [a_spec, b_spec] + ([add_spec] if add is not None else [])
    args = (a, b) + ((add,) if add is not None else ())
    return pl.pallas_call(
        body, grid=grid, in_specs=in_specs, out_specs=o_spec, out_shape=out_shape, name=name,
        scratch_shapes=[pltpu.VMEM(acc_shape, f32)],
        compiler_params=pltpu.CompilerParams(dimension_semantics=("arbitrary", "arbitrary", "arbitrary"), vmem_limit_bytes=VMEM_LIMIT),
    )(*args)


def _mm_nn(name, a, b, *, m, n, k, tm, tn, tk, out_dtype=f32, a_koff=0):
    return _mm(name, a, b, mode="nn", grid=(m // tm, n // tn, k // tk),
               a_spec=pl.BlockSpec((tm, tk), lambda i, j, kk: (i, kk + a_koff)),
               b_spec=pl.BlockSpec((tk, tn), lambda i, j, kk: (kk, j)),
               o_spec=pl.BlockSpec((tm, tn), lambda i, j, kk: (i, j)),
               out_shape=jax.ShapeDtypeStruct((m, n), out_dtype), acc_shape=(tm, tn))


def _mm_tn(name, a, b, *, m, n, k, tm, tn, tk, a_moff=0):
    return _mm(name, a, b, mode="tn", grid=(m // tm, n // tn, k // tk),
               a_spec=pl.BlockSpec((tk, tm), lambda i, j, kk: (kk, i + a_moff)),
               b_spec=pl.BlockSpec((tk, tn), lambda i, j, kk: (kk, j)),
               o_spec=pl.BlockSpec((tm, tn), lambda i, j, kk: (i, j)),
               out_shape=jax.ShapeDtypeStruct((m, n), f32), acc_shape=(tm, tn))


def _attn_common(mla):
    qw = 2 * LANE if mla else LANE
    scale = 1.0 / math.sqrt(MLA_QK if mla else FOX_DIM)
    return qw, scale


def _attn_fwd(name, q, k, v, bias_col, bias_row, *, mla, n_tok):
    qw, scale = _attn_common(mla)
    nq = n_tok // TOK
    has_bias = bias_col is not None

    def body(*refs):
        if has_bias:
            q_ref, k_ref, v_ref, bc_ref, br_ref, o_ref, lse_ref = refs
        else:
            q_ref, k_ref, v_ref, o_ref, lse_ref = refs
        p = pl.program_id(0)
        i = pl.program_id(1)
        lane = lax.broadcasted_iota(jnp.int32, (TOK, LANE), 1)
        qpos = i * TOK + lax.broadcasted_iota(jnp.int32, (TOK, TOK), 0)
        kiota = lax.broadcasted_iota(jnp.int32, (TOK, TOK), 1)
        o_tot = jnp.zeros((TOK, LANE), f32)
        lse_tot = jnp.zeros((TOK, LANE), f32)
        for e in (0, 1):
            half = (lane >= 64) if e else (lane < 64)
            if mla:
                qh = q_ref[:, e * LANE:(e + 1) * LANE]
            else:
                qh = jnp.where(half, q_ref[...], jnp.zeros((), bf16))
            h = 2 * p + e
            if has_bias:
                cq = jnp.sum(jnp.where(lane == h, bc_ref[...], 0.0), axis=-1, keepdims=True)

            def kv_step(j, carry, e=e, half=half, qh=qh, h=h, cq=cq if has_bias else None):
                m, l, acc = carry
                off = pl.multiple_of(j * TOK, TOK)
                if mla:
                    kj = k_ref[pl.ds(off, TOK), e * LANE:(e + 1) * LANE]
                else:
                    kj = k_ref[pl.ds(off, TOK), :]
                vj = jnp.where(half, v_ref[pl.ds(off, TOK), :], jnp.zeros((), bf16))
                s = _nt(qh, kj) * scale
                if has_bias:
                    s = s + (cq - br_ref[h, j])
                kpos = off + kiota
                allowed = ((kpos // 64) <= (qpos // 64)) if mla else (kpos <= qpos)
                s = jnp.where(allowed, s, -1e30)
                m_new = jnp.maximum(m, jnp.max(s, axis=-1, keepdims=True))
                alpha = jnp.exp(m - m_new)
                pe = jnp.exp(s - m_new)
                l = alpha * l + jnp.sum(pe, axis=-1, keepdims=True)
                acc = alpha * acc + _nn(pe, vj)
                return m_new, l, acc

            m, l, acc = lax.fori_loop(0, i + 1, kv_step,
                                      (jnp.full((TOK, 1), -1e30, f32), jnp.zeros((TOK, 1), f32), jnp.zeros((TOK, LANE), f32)))
            o_tot = o_tot + acc / l
            lse_tot = jnp.where(half, m + jnp.log(l), lse_tot)
        o_ref[...] = o_tot
        lse_ref[...] = lse_tot

    in_specs = [pl.BlockSpec((TOK, qw), lambda p, i: (i, p)),
                pl.BlockSpec((n_tok, qw), lambda p, i: (0, p)),
                pl.BlockSpec((n_tok, LANE), lambda p, i: (0, p))]
    args = [q, k, v]
    if has_bias:
        in_specs += [pl.BlockSpec((TOK, LANE), lambda p, i: (i, 0)), pl.BlockSpec(bias_row.shape, _const_map(4))]
        args += [bias_col, bias_row]
    return pl.pallas_call(
        body, grid=(4, nq), in_specs=in_specs,
        out_specs=[pl.BlockSpec((TOK, LANE), lambda p, i: (i, p)), pl.BlockSpec((TOK, LANE), lambda p, i: (i, p))],
        out_shape=[jax.ShapeDtypeStruct((n_tok, 512), f32), jax.ShapeDtypeStruct((n_tok, 512), f32)], name=name,
        compiler_params=pltpu.CompilerParams(dimension_semantics=("arbitrary", "arbitrary"), vmem_limit_bytes=VMEM_LIMIT),
    )(*args)


def _attn_bwd(name, q, k, v, o, lse, do, bias_col, bias_row, *, mla, n_tok):
    qw, scale = _attn_common(mla)
    nq = n_tok // TOK
    has_bias = bias_col is not None

    def body(*refs):
        if has_bias:
            q_ref, k_ref, v_ref, o_ref, lse_ref, do_ref, bc_ref, br_ref, dq_ref, dk_ref, dv_ref, dbc_ref, dbr_ref = refs
        else:
            q_ref, k_ref, v_ref, o_ref, lse_ref, do_ref, dq_ref, dk_ref, dv_ref = refs
        p = pl.program_id(0)
        i = pl.program_id(1)

        @pl.when(i == 0)
        def _():
            dk_ref[...] = jnp.zeros_like(dk_ref)
            dv_ref[...] = jnp.zeros_like(dv_ref)

        if has_bias:
            @pl.when(jnp.logical_and(i == 0, p == 0))
            def _():
                dbr_ref[...] = jnp.zeros_like(dbr_ref)

        lane = lax.broadcasted_iota(jnp.int32, (TOK, LANE), 1)
        qpos = i * TOK + lax.broadcasted_iota(jnp.int32, (TOK, TOK), 0)
        kiota = lax.broadcasted_iota(jnp.int32, (TOK, TOK), 1)
        do_blk = do_ref[...]
        prod = do_blk * o_ref[...]
        lse_blk = lse_ref[...]
        dq_tot = jnp.zeros((TOK, LANE), f32)
        dbc_tot = jnp.zeros((TOK, LANE), f32)
        for e in (0, 1):
            half = (lane >= 64) if e else (lane < 64)
            if mla:
                qh = q_ref[:, e * LANE:(e + 1) * LANE]
            else:
                qh = jnp.where(half, q_ref[...], jnp.zeros((), bf16))
            h = 2 * p + e
            delta = jnp.sum(jnp.where(half, prod, 0.0), axis=-1, keepdims=True)
            lse_h = lse_blk[:, 64 * e:64 * e + 1]
            do_h = jnp.where(half, do_blk, 0.0).astype(bf16)
            if has_bias:
                cq = jnp.sum(jnp.where(lane == h, bc_ref[...], 0.0), axis=-1, keepdims=True)

            def kv_step(j, carry, e=e, half=half, qh=qh, h=h, delta=delta, lse_h=lse_h, do_h=do_h,
                        cq=cq if has_bias else None):
                dq_acc, rs_acc = carry
                off = pl.multiple_of(j * TOK, TOK)
                if mla:
                    kj = k_ref[pl.ds(off, TOK), e * LANE:(e + 1) * LANE]
                else:
                    kj = k_ref[pl.ds(off, TOK), :]
                vj = jnp.where(half, v_ref[pl.ds(off, TOK), :], jnp.zeros((), bf16))
                s = _nt(qh, kj) * scale
                if has_bias:
                    s = s + (cq - br_ref[h, j])
                kpos = off + kiota
                allowed = ((kpos // 64) <= (qpos // 64)) if mla else (kpos <= qpos)
                pr = jnp.where(allowed, jnp.exp(s - lse_h), 0.0)
                dp = _nt(do_h, vj)
                ds = pr * (dp - delta)
                dq_acc = dq_acc + _nn(ds, kj) * scale
                dk_j = _tn(ds, qh) * scale
                if mla:
                    dk_ref[pl.ds(off, TOK), e * LANE:(e + 1) * LANE] += dk_j
                else:
                    dk_ref[pl.ds(off, TOK), :] += dk_j
                dv_ref[pl.ds(off, TOK), :] += _tn(pr, do_h)
                if has_bias:
                    rs_acc = rs_acc + jnp.sum(ds, axis=-1, keepdims=True)
                    dbr_ref[h, j] += jnp.broadcast_to(-jnp.sum(ds, axis=0, keepdims=True), (8, TOK))
                return dq_acc, rs_acc

            dq_acc, rs_acc = lax.fori_loop(0, i + 1, kv_step, (jnp.zeros((TOK, LANE), f32), jnp.zeros((TOK, 1), f32)))
            if mla:
                dq_ref[:, e * LANE:(e + 1) * LANE] = dq_acc
            else:
                dq_tot = dq_tot + jnp.where(half, dq_acc, 0.0)
            if has_bias:
                dbc_tot = jnp.where(half, rs_acc, dbc_tot)
        if not mla:
            dq_ref[...] = dq_tot
        if has_bias:
            dbc_ref[...] = dbc_tot

    tile_q = pl.BlockSpec((TOK, qw), lambda p, i: (i, p))
    tile_v = pl.BlockSpec((TOK, LANE), lambda p, i: (i, p))
    full_k = pl.BlockSpec((n_tok, qw), lambda p, i: (0, p))
    full_v = pl.BlockSpec((n_tok, LANE), lambda p, i: (0, p))
    in_specs = [tile_q, full_k, full_v, tile_v, tile_v, tile_v]
    args = [q, k, v, o, lse, do]
    out_specs = [tile_q, full_k, full_v]
    out_shape = [jax.ShapeDtypeStruct((n_tok, 4 * qw), f32), jax.ShapeDtypeStruct((n_tok, 4 * qw), f32),
                 jax.ShapeDtypeStruct((n_tok, 512), f32)]
    if has_bias:
        in_specs += [pl.BlockSpec((TOK, LANE), lambda p, i: (i, 0)), pl.BlockSpec(bias_row.shape, _const_map(4))]
        args += [bias_col, bias_row]
        out_specs += [tile_v, pl.BlockSpec((HEADS, nq, 8, TOK), _const_map(4))]
        out_shape += [jax.ShapeDtypeStruct((n_tok, 512), f32), jax.ShapeDtypeStruct((HEADS, nq, 8, TOK), f32)]
    return pl.pallas_call(
        body, grid=(4, nq), in_specs=in_specs, out_specs=out_specs, out_shape=out_shape, name=name,
        compiler_params=pltpu.CompilerParams(dimension_semantics=("arbitrary", "arbitrary"), vmem_limit_bytes=VMEM_LIMIT),
    )(*args)


def _s5_disc(lr, li, ldt):
    dt = jnp.exp(ldt)
    mag = jnp.exp(lr * dt)
    a_re = mag * jnp.cos(li * dt)
    a_im = mag * jnp.sin(li * dt)
    den = lr * lr + li * li
    f_re = ((a_re - 1.0) * lr + a_im * li) / den
    f_im = (a_im * lr - (a_re - 1.0) * li) / den
    return a_re, a_im, f_re, f_im


def _s5_param_fwd(lr, li, ldt, b_re, b_im):
    def body(lr_ref, li_ref, ldt_ref, br_ref, bi_ref, ar_ref, ai_ref, bbr_ref, bbi_ref):
        a_re, a_im, f_re, f_im = _s5_disc(lr_ref[...], li_ref[...], ldt_ref[...])
        ar_ref[...] = a_re
        ai_ref[...] = a_im
        br, bi = br_ref[...], bi_ref[...]
        bbr_ref[...] = f_re * br - f_im * bi
        bbi_ref[...] = f_re * bi + f_im * br

    col = jax.ShapeDtypeStruct((S5_LANES, 1), f32)
    mat = jax.ShapeDtypeStruct((S5_LANES, S5_GROUP), f32)
    return pl.pallas_call(body, out_shape=[col, col, mat, mat], name="s5_param_fwd")(lr, li, ldt, b_re, b_im)


def _s5_param_bwd(lr, li, ldt, b_re, b_im, da_re, da_im, dbb_re, dbb_im):
    def body(lr_ref, li_ref, ldt_ref, br_ref, bi_ref, dar_ref, dai_ref, gbr_ref, gbi_ref,
             dlr_ref, dli_ref, dldt_ref, dbr_ref, dbi_ref):
        (a_re, a_im, f_re, f_im), vjp = jax.vjp(_s5_disc, lr_ref[...], li_ref[...], ldt_ref[...])
        br, bi, gr, gi = br_ref[...], bi_ref[...], gbr_ref[...], gbi_ref[...]
        dbr_ref[...] = f_re * gr + f_im * gi
        dbi_ref[...] = f_re * gi - f_im * gr
        dfr = jnp.sum(br * gr + bi * gi, axis=-1, keepdims=True)
        dfi = jnp.sum(br * gi - bi * gr, axis=-1, keepdims=True)
        dlr, dli, dldt = vjp((dar_ref[...], dai_ref[...], dfr, dfi))
        dlr_ref[...] = dlr
        dli_ref[...] = dli
        dldt_ref[...] = jnp.sum(dldt.reshape(S5_GROUPS, S5_STATE, 1), axis=1)

    col = jax.ShapeDtypeStruct((S5_LANES, 1), f32)
    mat = jax.ShapeDtypeStruct((S5_LANES, S5_GROUP), f32)
    return pl.pallas_call(body, out_shape=[col, col, jax.ShapeDtypeStruct((S5_GROUPS, 1), f32), mat, mat],
                          name="s5_param_bwd")(lr, li, ldt, b_re, b_im, da_re, da_im, dbb_re, dbb_im)


_SCAN_NB = 2


def _s5_scan(name, bu, a_re8, a_im8, *, reverse, n_tok):
    rows = n_tok // 8
    nb = _SCAN_NB

    def body(bu_ref, ar_ref, ai_ref, x_ref):
        a_r = [ar_ref[b] for b in range(nb)]
        a_i = [ai_ref[b] for b in range(nb)]
        zero = jnp.zeros((8, LANE), f32)
        one = jnp.ones((8, LANE), f32)

        def rows_at(r):
            rr = (rows - 1 - r) if reverse else r
            return pl.ds(rr, 8, stride=rows)

        def pass1(r, carry):
            out = []
            sl = rows_at(r)
            for b in range(nb):
                xr, xi, mr, mi = carry[b]
                nr = a_r[b] * xr - a_i[b] * xi + bu_ref[0, b, sl, :]
                ni = a_r[b] * xi + a_i[b] * xr + bu_ref[1, b, sl, :]
                x_ref[0, b, sl, :] = nr
                x_ref[1, b, sl, :] = ni
                out.append((nr, ni, a_r[b] * mr - a_i[b] * mi, a_r[b] * mi + a_i[b] * mr))
            return tuple(out)

        carry = lax.fori_loop(0, rows, pass1, tuple((zero, zero, one, zero) for _ in range(nb)))
        sub = lax.broadcasted_iota(jnp.int32, (8, LANE), 0)
        feed = []
        for b in range(nb):
            lr_, li_, pr, pi = carry[b]
            fr, fi = zero, zero
            for _ in range(7):
                tr = lr_ + pr * fr - pi * fi
                ti = li_ + pr * fi + pi * fr
                if reverse:
                    fr = jnp.where(sub < 7, pltpu.roll(tr, 7, 0), 0.0)
                    fi = jnp.where(sub < 7, pltpu.roll(ti, 7, 0), 0.0)
                else:
                    fr = jnp.where(sub > 0, pltpu.roll(tr, 1, 0), 0.0)
                    fi = jnp.where(sub > 0, pltpu.roll(ti, 1, 0), 0.0)
            feed.append((fr, fi))

        def pass2(r, carry):
            out = []
            sl = rows_at(r)
            for b in range(nb):
                mr, mi = carry[b]
                fr, fi = feed[b]
                x_ref[0, b, sl, :] += mr * fr - mi * fi
                x_ref[1, b, sl, :] += mr * fi + mi * fr
                out.append((a_r[b] * mr - a_i[b] * mi, a_r[b] * mi + a_i[b] * mr))
            return tuple(out)

        lax.fori_loop(0, rows, pass2, tuple((a_r[b], a_i[b]) for b in range(nb)))

    blk = pl.BlockSpec((2, nb, n_tok, LANE), lambda g: (0, g, 0, 0))
    ablk = pl.BlockSpec((nb, 8, LANE), lambda g: (g, 0, 0))
    return pl.pallas_call(
        body, grid=(S5_BLOCKS // nb,), in_specs=[blk, ablk, ablk], out_specs=blk,
        out_shape=jax.ShapeDtypeStruct((2, S5_BLOCKS, n_tok, LANE), f32), name=name,
        compiler_params=pltpu.CompilerParams(dimension_semantics=("arbitrary",), vmem_limit_bytes=VMEM_LIMIT),
    )(bu, a_re8, a_im8)


def _s5_da(xs, gx, *, n_tok):
    def body(x_ref, g_ref, o_ref):
        t = lax.broadcasted_iota(jnp.int32, (n_tok, LANE), 0)
        xr = jnp.where(t >= 1, pltpu.roll(x_ref[0, 0], 1, 0), 0.0)
        xi = jnp.where(t >= 1, pltpu.roll(x_ref[1, 0], 1, 0), 0.0)
        gr, gi = g_ref[0, 0], g_ref[1, 0]
        o_ref[0, 0:1, :] = jnp.sum(xr * gr + xi * gi, axis=0, keepdims=True)
        o_ref[0, 1:2, :] = jnp.sum(xr * gi - xi * gr, axis=0, keepdims=True)

    blk = pl.BlockSpec((2, 1, n_tok, LANE), lambda g: (0, g, 0, 0))
    return pl.pallas_call(
        body, grid=(S5_BLOCKS,), in_specs=[blk, blk], out_specs=pl.BlockSpec((1, 2, LANE), lambda g: (g, 0, 0)),
        out_shape=jax.ShapeDtypeStruct((S5_BLOCKS, 2, LANE), f32), name="s5_da",
        compiler_params=pltpu.CompilerParams(dimension_semantics=("arbitrary",), vmem_limit_bytes=VMEM_LIMIT),
    )(xs, gx)


S5_Q = 4


def _bd8(t):
    _, a, b = t.shape
    t = t.reshape(S5_Q, 8, a, 1, b)
    eye = jnp.eye(8, dtype=jnp.bool_).reshape(1, 8, 1, 8, 1)
    return jnp.where(eye, jnp.broadcast_to(t, (S5_Q, 8, a, 8, b)), jnp.zeros((), t.dtype)).reshape(S5_Q, 8 * a, 8 * b)


def _bd8_diag(m, a, b):
    m = m.reshape(S5_Q, 8, a, 8, b)
    eye = jnp.eye(8, dtype=jnp.bool_).reshape(1, 8, 1, 8, 1)
    return jnp.sum(jnp.where(eye, m, 0.0), axis=3).reshape(S5_GROUPS, a, b)


def _s5_expand(name, a, a_blk0, wq, *, n_tok):
    def body(a_ref, w_ref, o_ref):
        r = _nn(a_ref[...], w_ref[...])
        for k in range(4):
            o_ref[k] = r[:, k * LANE:(k + 1) * LANE]

    return pl.pallas_call(
        body, grid=(2, S5_Q),
        in_specs=[pl.BlockSpec((n_tok, LANE), lambda ri, q: (0, a_blk0 + q)),
                  pl.BlockSpec((None, None, LANE, 512), lambda ri, q: (ri, q, 0, 0))],
        out_specs=pl.BlockSpec((None, 4, n_tok, LANE), lambda ri, q: (ri, q, 0, 0)),
        out_shape=jax.ShapeDtypeStruct((2, S5_BLOCKS, n_tok, LANE), f32), name=name,
        compiler_params=pltpu.CompilerParams(dimension_semantics=("arbitrary", "arbitrary"), vmem_limit_bytes=VMEM_LIMIT),
    )(a, wq)


def _s5_contract(name, xs, wq, add, out_dtype, *, n_tok):
    def body(*refs):
        if add is None:
            x_ref, w_ref, o_ref, acc_ref = refs
        else:
            x_ref, w_ref, add_ref, o_ref, acc_ref = refs
        ri = pl.program_id(1)
        r = _nn(x_ref[0], w_ref[0:LANE, :])
        for k in range(1, 4):
            r = r + _nn(x_ref[k], w_ref[k * LANE:(k + 1) * LANE, :])

        @pl.when(ri == 0)
        def _():
            acc_ref[...] = r

        @pl.when(ri == 1)
        def _():
            tot = acc_ref[...] + r
            if add is not None:
                tot = tot + add_ref[...]
            o_ref[...] = tot.astype(o_ref.dtype)

    col = pl.BlockSpec((n_tok, LANE), lambda q, ri: (0, q))
    in_specs = [pl.BlockSpec((None, 4, n_tok, LANE), lambda q, ri: (ri, q, 0, 0)),
                pl.BlockSpec((None, None, 512, LANE), lambda q, ri: (ri, q, 0, 0))]
    args = [xs, wq]
    if add is not None:
        in_specs.append(col)
        args.append(add)
    return pl.pallas_call(
        body, grid=(S5_Q, 2), in_specs=in_specs, out_specs=col, out_shape=jax.ShapeDtypeStruct((n_tok, 512), out_dtype), name=name,
        scratch_shapes=[pltpu.VMEM((n_tok, LANE), f32)],
        compiler_params=pltpu.CompilerParams(dimension_semantics=("arbitrary", "arbitrary"), vmem_limit_bytes=VMEM_LIMIT),
    )(*args)


def _s5_wgrad_states(name, xs, d, *, n_tok):
    def body(x_ref, d_ref, o_ref):
        for k in range(4):
            o_ref[k * LANE:(k + 1) * LANE, :] = _tn(x_ref[k], d_ref[...])

    return pl.pallas_call(
        body, grid=(2, S5_Q),
        in_specs=[pl.BlockSpec((None, 4, n_tok, LANE), lambda ri, q: (ri, q, 0, 0)), pl.BlockSpec((n_tok, LANE), lambda ri, q: (0, q))],
        out_specs=pl.BlockSpec((None, None, 512, LANE), lambda ri, q: (ri, q, 0, 0)),
        out_shape=jax.ShapeDtypeStruct((2, S5_Q, 512, LANE), f32), name=name,
        compiler_params=pltpu.CompilerParams(dimension_semantics=("arbitrary", "arbitrary"), vmem_limit_bytes=VMEM_LIMIT),
    )(xs, d)


def _s5_wgrad_channels(name, a, a_blk0, gx, *, n_tok):
    def body(a_ref, g_ref, o_ref):
        for k in range(4):
            o_ref[:, k * LANE:(k + 1) * LANE] = _tn(a_ref[...], g_ref[k])

    return pl.pallas_call(
        body, grid=(2, S5_Q),
        in_specs=[pl.BlockSpec((n_tok, LANE), lambda ri, q: (0, a_blk0 + q)), pl.BlockSpec((None, 4, n_tok, LANE), lambda ri, q: (ri, q, 0, 0))],
        out_specs=pl.BlockSpec((None, None, LANE, 512), lambda ri, q: (ri, q, 0, 0)),
        out_shape=jax.ShapeDtypeStruct((2, S5_Q, LANE, 512), f32), name=name,
        compiler_params=pltpu.CompilerParams(dimension_semantics=("arbitrary", "arbitrary"), vmem_limit_bytes=VMEM_LIMIT),
    )(a, gx)


N_CHIPS = 4
_BIG_SHARD = {"w_in": (1, 1024, 1770, 1792), "mla_w_q_up": (1, 256, 192, 256), "mla_w_kv_up": (1, 128, 256, 256),
              "s5_w_glu": (0, 128, 512, 512), "w_branch_out": (0, 384, 1024, 1024), "w_out": (0, 256, 1024, 1024)}


def _to_shards(name, m):
    axis, r, c, cp = _BIG_SHARD[name]
    if axis == 0:
        return m.reshape(N_CHIPS, r, c)
    return jnp.stack([jnp.pad(m[:, j * c:(j + 1) * c], ((0, 0), (0, cp - c))) for j in range(N_CHIPS)])


def _from_shards(name, s):
    axis, r, c, cp = _BIG_SHARD[name]
    if axis == 0:
        return s.reshape(N_CHIPS * r, c)
    return jnp.concatenate([s[j, :, :c] for j in range(N_CHIPS)], axis=1)


def _pad_w_in(w):
    pieces, pos = [], 0
    for name in _PAD_ORDER:
        start, width, inner = _PAD[name]
        o0, ow = _ORIG[name]
        if start + inner > pos:
            pieces.append(jnp.zeros((w.shape[0], start + inner - pos), w.dtype))
        pieces.append(w[:, o0:o0 + ow])
        pos = start + inner + ow
    pieces.append(jnp.zeros((w.shape[0], NP - pos), w.dtype))
    return jnp.concatenate(pieces, axis=1)


def _layer_weights(l, small, big):
    w = {}
    w["w_in_shards"] = big["w_in"][l]
    w["w_in"] = _pad_w_in(_from_shards("w_in", big["w_in"][l]))
    wq = _from_shards("mla_w_q_up", big["mla_w_q_up"][l]).reshape(MLA_Q_RANK, HEADS, MLA_QK)
    w["wq"] = jnp.pad(wq, ((0, 0), (0, 0), (0, LANE - MLA_QK))).reshape(MLA_Q_RANK, HEADS * LANE)
    wkv = _from_shards("mla_w_kv_up", big["mla_w_kv_up"][l]).reshape(MLA_KV_RANK, HEADS, 128)
    wk = jnp.pad(wkv[:, :, :64], ((0, 0), (0, 0), (0, 64))).reshape(MLA_KV_RANK, HEADS * LANE)
    wv = wkv[:, :, 64:].reshape(MLA_KV_RANK, 512)
    w["wkv"] = jnp.concatenate([wk, wv], axis=1)
    w["w_glu"] = _from_shards("s5_w_glu", big["s5_w_glu"][l])
    w["wo"] = _from_shards("w_branch_out", big["w_branch_out"][l])
    w["w_out"] = _from_shards("w_out", big["w_out"][l])
    row = lambda a: a.reshape(1, -1).astype(f32)
    w["norm_g"] = row(small["norm_g"][l])
    w["qa_g"] = row(small["mla_q_a_norm"][l])
    w["kva_g"] = row(small["mla_kv_a_norm"][l])
    w["qn_g"] = jnp.pad(row(small["mla_q_norm"][l]), ((0, 0), (0, LANE - MLA_QK)))
    w["kn_g"] = jnp.pad(row(small["mla_k_norm"][l]), ((0, 0), (0, LANE - MLA_QK)))
    w["fq_g"] = jnp.tile(row(small["fox_q_norm"][l]), (1, 2))
    w["fk_g"] = jnp.tile(row(small["fox_k_norm"][l]), (1, 2))
    w["b_f"] = jnp.pad(row(small["fox_b_f"][l]), ((0, 0), (0, LANE - HEADS)))
    w["lr"] = small["s5_lambda_re"][l].reshape(S5_LANES, 1)
    w["li"] = small["s5_lambda_im"][l].reshape(S5_LANES, 1)
    w["ldt"] = jnp.repeat(small["s5_log_dt"][l], S5_STATE).reshape(S5_LANES, 1)
    w["b_re"] = small["s5_b_re"][l].reshape(S5_LANES, S5_GROUP)
    w["b_im"] = small["s5_b_im"][l].reshape(S5_LANES, S5_GROUP)
    w["c_re"] = small["s5_c_re"][l]
    w["c_im"] = small["s5_c_im"][l]
    w["s5_d"] = row(small["s5_d"][l])
    w["b_glu"] = row(small["s5_b_glu"][l])
    return w


def _fox_halves(x, lane):
    sq = x * x
    lo = jnp.sum(jnp.where(lane < 64, sq, 0.0), axis=-1, keepdims=True)
    hi = jnp.sum(sq, axis=-1, keepdims=True) - lo
    return jnp.where(lane < 64, lax.rsqrt(lo * (1.0 / 64) + EPS), lax.rsqrt(hi * (1.0 / 64) + EPS))


def _fox_halves_bwd(dy, x, r, g, lane):
    xh = x * r
    dxh = dy * g
    pr = dxh * xh
    lo = jnp.sum(jnp.where(lane < 64, pr, 0.0), axis=-1, keepdims=True)
    hi = jnp.sum(pr, axis=-1, keepdims=True) - lo
    mean = jnp.where(lane < 64, lo, hi) * (1.0 / 64)
    return r * (dxh - xh * mean), jnp.sum(dy * xh, axis=0, keepdims=True)


def _mla_recompute(cq, ckv, kpe, c, s1, s2, qa_g, kva_g, wq, wkv):
    cqn, r_cq = _rms(cq, qa_g, MLA_Q_RANK)
    ckvn, r_ckv = _rms(ckv, kva_g, MLA_KV_RANK)
    cqn_b = cqn.astype(bf16)
    ckvn_b = ckvn.astype(bf16)
    q_raw = _nn(cqn_b, wq)
    kv_raw = _nn(ckvn_b, wkv)
    kpe_rot = _rope(kpe, c, s1, s2)
    return cqn_b, r_cq, ckvn_b, r_ckv, q_raw, kv_raw, kpe_rot


def _layer_fwd(x, w, rope_tabs, n_tok):
    c_tab, s1_tab, s2_tab = rope_tabs
    saved = {"x": x}

    def norm_body(x_ref, g_ref, h_ref):
        h_ref[...] = _rms(x_ref[...], g_ref[...], D_MODEL)[0].astype(bf16)

    (h,) = _rowwise("norm_fwd", norm_body, n_tok, [(x, D_MODEL, 0)], [w["norm_g"]], [(D_MODEL, bf16)], [])
    proj = _mm_nn("in_proj", h, w["w_in"], m=n_tok, n=NP, k=D_MODEL, tm=n_tok, tn=512, tk=D_MODEL)
    saved["h"], saved["proj"] = h, proj

    def mla_prep_body(cq_ref, ckv_ref, kpe_ref, c_ref, s1_ref, s2_ref, qa_ref, kva_ref, wq_ref, wkv_ref, qn_g_ref, kn_g_ref,
                      qn_ref, kn_ref, v_ref):
        c, s1, s2 = c_ref[...], s1_ref[...], s2_ref[...]
        _, _, _, _, q_raw, kv_raw, kpe_rot = _mla_recompute(cq_ref[...], ckv_ref[...], kpe_ref[...], c, s1, s2,
                                                            qa_ref[...], kva_ref[...], wq_ref[...], wkv_ref[...])
        for hd in range(HEADS):
            sl = slice(hd * LANE, (hd + 1) * LANE)
            qn_ref[:, sl] = _rms(_rope(q_raw[:, sl], c, s1, s2), qn_g_ref[...], MLA_QK)[0].astype(bf16)
            kn_ref[:, sl] = _rms(kv_raw[:, sl] + kpe_rot, kn_g_ref[...], MLA_QK)[0].astype(bf16)
        v_ref[...] = kv_raw[:, HEADS * LANE:].astype(bf16)

    qn, kn, v_mla = _rowwise(
        "mla_prep", mla_prep_body, n_tok,
        [(proj, *_seg("cq")), (proj, *_seg("ckv")), (proj, *_seg("kpe")), (c_tab, LANE, 0), (s1_tab, LANE, 0), (s2_tab, LANE, 0)],
        [w["qa_g"], w["kva_g"], w["wq"], w["wkv"], w["qn_g"], w["kn_g"]],
        [(HEADS * LANE, bf16), (HEADS * LANE, bf16), (512, bf16)], [])
    y_mla, lse_mla = _attn_fwd("mla_attn_fwd", qn, kn, v_mla, None, None, mla=True, n_tok=n_tok)
    saved.update(qn=qn, kn=kn, v_mla=v_mla, y_mla=y_mla, lse_mla=lse_mla)

    def fox_prep_body(fq_ref, fk_ref, fv_ref, ff_ref, qg_ref, kg_ref, bf_ref, fqn_ref, fkn_ref, fvb_ref, logf_ref):
        lane = lax.broadcasted_iota(jnp.int32, (TOK, LANE), 1)
        for blk in range(4):
            sl = slice(blk * LANE, (blk + 1) * LANE)
            xq = fq_ref[:, sl]
            fqn_ref[:, sl] = (xq * _fox_halves(xq, lane) * qg_ref[...]).astype(bf16)
            xk = fk_ref[:, sl]
            fkn_ref[:, sl] = (xk * _fox_halves(xk, lane) * kg_ref[...]).astype(bf16)
        fvb_ref[...] = fv_ref[...].astype(bf16)
        z = ff_ref[...] + bf_ref[...]
        logf_ref[...] = jnp.minimum(z, 0.0) - jnp.log(1.0 + jnp.exp(-jnp.abs(z)))

    fqn, fkn, fvb, logf = _rowwise(
        "fox_prep", fox_prep_body, n_tok,
        [(proj, *_seg("fq")), (proj, *_seg("fk")), (proj, *_seg("fv")), (proj, *_seg("ff"))],
        [w["fq_g"], w["fk_g"], w["b_f"]],
        [(512, bf16), (512, bf16), (512, bf16), (LANE, f32)], [])

    def cum_body(x_ref, cum_ref, cum_t_ref):
        x = x_ref[...]
        t = lax.broadcasted_iota(jnp.int32, x.shape, 0)
        s = 1
        while s < n_tok:
            x = x + jnp.where(t >= s, pltpu.roll(x, s, 0), 0.0)
            s *= 2
        cum_ref[...] = x
        cum_t_ref[...] = x.T[0:HEADS, :]

    cum, cum_t = pl.pallas_call(cum_body, out_shape=[jax.ShapeDtypeStruct((n_tok, LANE), f32), jax.ShapeDtypeStruct((HEADS, n_tok), f32)],
                                name="fox_cum")(logf)
    cum_row = cum_t.reshape(HEADS, n_tok // TOK, 1, TOK)
    y_fox, lse_fox = _attn_fwd("fox_attn_fwd", fqn, fkn, fvb, cum, cum_row, mla=False, n_tok=n_tok)
    saved.update(fqn=fqn, fkn=fkn, fvb=fvb, cum=cum, cum_row=cum_row, y_fox=y_fox, lse_fox=lse_fox)

    a_re, a_im, bb_re, bb_im = _s5_param_fwd(w["lr"], w["li"], w["ldt"], w["b_re"], w["b_im"])
    per_group = lambda m: m.reshape(S5_GROUPS, S5_STATE, S5_GROUP)
    b_cn = jnp.stack([_bd8(jnp.swapaxes(per_group(bb_re), 1, 2)), _bd8(jnp.swapaxes(per_group(bb_im), 1, 2))]).astype(bf16)
    b_nc = jnp.stack([_bd8(per_group(bb_re)), _bd8(per_group(bb_im))]).astype(bf16)
    c_nc = jnp.stack([_bd8(jnp.swapaxes(w["c_re"], 1, 2)), -_bd8(jnp.swapaxes(w["c_im"], 1, 2))]).astype(bf16)
    c_cn = jnp.stack([_bd8(w["c_re"]), -_bd8(w["c_im"])]).astype(bf16)
    a_re8 = jnp.broadcast_to(a_re.reshape(S5_BLOCKS, 1, LANE), (S5_BLOCKS, 8, LANE))
    a_im8 = jnp.broadcast_to(a_im.reshape(S5_BLOCKS, 1, LANE), (S5_BLOCKS, 8, LANE))
    u_w, u_blk = _seg("s5u")
    u_blk128 = u_blk * (u_w // LANE)
    bu = _s5_expand("s5_bu", proj, u_blk128, b_cn, n_tok=n_tok)
    xs = _s5_scan("s5_scan_fwd", bu, a_re8, a_im8, reverse=False, n_tok=n_tok)
    ylin = _s5_contract("s5_y", xs, c_nc, None, f32, n_tok=n_tok)

    def s5_post_body(yl_ref, u_ref, d_ref, wg_ref, bg_ref, out_ref):
        y = yl_ref[...] + d_ref[...] * u_ref[...]
        z, _ = _gelu(y)
        out_ref[...] = z * _sigmoid(_nn(z, wg_ref[...]) + bg_ref[...])

    (y_s5,) = _rowwise("s5_post", s5_post_body, n_tok, [(ylin, 512, 0), (proj, u_w, u_blk)],
                       [w["s5_d"], w["w_glu"], w["b_glu"]], [(512, f32)], [])
    saved.update(xs=xs, ylin=ylin, y_s5=y_s5, b_nc=b_nc, c_cn=c_cn, a_re8=a_re8, a_im8=a_im8)

    def merge_body(ym_ref, yf_ref, ys_ref, gm_ref, gf_ref, gs_ref, mm_ref, mf_ref, ms_ref, x_ref, wo_ref, wout_ref, out_ref):
        merged = jnp.zeros((TOK, D_MODEL), f32)
        for b, (y_ref, g_ref, m_ref) in enumerate(((ym_ref, gm_ref, mm_ref), (yf_ref, gf_ref, mf_ref), (ys_ref, gs_ref, ms_ref))):
            g = g_ref[...]
            a = y_ref[...] * (g * _sigmoid(g))
            merged = merged + _sigmoid(m_ref[...]) * _nn(a, wo_ref[b * 512:(b + 1) * 512, :])
        out_ref[...] = x_ref[...] + _nn(merged, wout_ref[...])

    (out,) = _rowwise(
        "merge_fwd", merge_body, n_tok,
        [(y_mla, 512, 0), (y_fox, 512, 0), (y_s5, 512, 0), (proj, *_seg("g_mla")), (proj, *_seg("g_fox")), (proj, *_seg("g_s5")),
         (proj, *_seg("m_mla")), (proj, *_seg("m_fox")), (proj, *_seg("m_s5")), (x, D_MODEL, 0)],
        [w["wo"], w["w_out"]], [(D_MODEL, f32)], [])
    return out, saved


def _layer_bwd(dout, w, sv, rope_tabs, n_tok):
    c_tab, s1_tab, s2_tab = rope_tabs
    proj, x = sv["proj"], sv["x"]
    grads = {}

    def merge_bwd_body(ym_ref, yf_ref, ys_ref, gm_ref, gf_ref, gs_ref, mm_ref, mf_ref, ms_ref, do_ref, wo_ref, wout_ref,
                       dym_ref, dyf_ref, dys_ref, dgm_ref, dgf_ref, dgs_ref, dmm_ref, dmf_ref, dms_ref, dwo_ref, dwout_ref):
        do = do_ref[...]
        branches = ((ym_ref, gm_ref, mm_ref, dym_ref, dgm_ref, dmm_ref), (yf_ref, gf_ref, mf_ref, dyf_ref, dgf_ref, dmf_ref),
                    (ys_ref, gs_ref, ms_ref, dys_ref, dgs_ref, dms_ref))
        acts, outs, sigs = [], [], []
        merged = jnp.zeros((TOK, D_MODEL), f32)
        for b, (y_ref, g_ref, m_ref, _, _, _) in enumerate(branches):
            g = g_ref[...]
            a = (y_ref[...] * (g * _sigmoid(g))).astype(bf16)
            o = _nn(a, wo_ref[b * 512:(b + 1) * 512, :])
            s = _sigmoid(m_ref[...])
            merged = merged + s * o
            acts.append(a)
            outs.append(o)
            sigs.append(s)
        dmerged = _nt(do, wout_ref[...])
        _accumulate(dwout_ref, _tn(merged, do))
        dwo = []
        for b, (y_ref, g_ref, m_ref, dy_ref, dg_ref, dm_ref) in enumerate(branches):
            s, o = sigs[b], outs[b]
            dm_ref[...] = (dmerged * o * s * (1.0 - s)).astype(bf16)
            d_o = dmerged * s
            da = _nt(d_o, wo_ref[b * 512:(b + 1) * 512, :])
            dwo.append(_tn(acts[b], d_o))
            g = g_ref[...]
            sg = _sigmoid(g)
            dy_ref[...] = da * (g * sg)
            dg_ref[...] = (da * y_ref[...] * (sg * (1.0 + g * (1.0 - sg)))).astype(bf16)
        _accumulate(dwo_ref, jnp.concatenate(dwo, axis=0))

    (dy_mla, dy_fox, dy_s5, dg_mla, dg_fox, dg_s5, dm_mla, dm_fox, dm_s5, dwo, dwout) = _rowwise(
        "merge_bwd", merge_bwd_body, n_tok,
        [(sv["y_mla"], 512, 0), (sv["y_fox"], 512, 0), (sv["y_s5"], 512, 0), (proj, *_seg("g_mla")), (proj, *_seg("g_fox")),
         (proj, *_seg("g_s5")), (proj, *_seg("m_mla")), (proj, *_seg("m_fox")), (proj, *_seg("m_s5")), (dout, D_MODEL, 0)],
        [w["wo"], w["w_out"]],
        [(512, f32)] * 3 + [(512, bf16)] * 3 + [(D_MODEL, bf16)] * 3, [((1536, D_MODEL), f32), ((D_MODEL, D_MODEL), f32)])
    grads["w_branch_out"], grads["w_out"] = dwo, dwout

    u_w, u_blk = _seg("s5u")

    def s5_post_bwd_body(yl_ref, u_ref, do_ref, d_ref, wg_ref, bg_ref, dyl_ref, dus_ref, dd_ref, dwg_ref, dbg_ref):
        u = u_ref[...]
        y = yl_ref[...] + d_ref[...] * u
        z, t = _gelu(y)
        s = _sigmoid(_nn(z, wg_ref[...]) + bg_ref[...])
        do = do_ref[...]
        dgl = do * z * s * (1.0 - s)
        dz = do * s + _nt(dgl, wg_ref[...])
        dy = dz * _gelu_grad(y, t)
        dyl_ref[...] = dy.astype(bf16)
        dus_ref[...] = dy * d_ref[...]
        _accumulate(dd_ref, jnp.sum(dy * u, axis=0, keepdims=True))
        _accumulate(dwg_ref, _tn(z, dgl))
        _accumulate(dbg_ref, jnp.sum(dgl, axis=0, keepdims=True))

    dylin, du_skip, dd, dwglu, dbglu = _rowwise(
        "s5_post_bwd", s5_post_bwd_body, n_tok, [(sv["ylin"], 512, 0), (proj, u_w, u_blk), (dy_s5, 512, 0)],
        [w["s5_d"], w["w_glu"], w["b_glu"]], [(512, bf16), (512, f32)], [((1, 512), f32), ((512, 512), f32), ((1, 512), f32)])
    grads["s5_d"], grads["s5_w_glu"], grads["s5_b_glu"] = dd.reshape(512), dwglu, dbglu.reshape(512)

    dxs = _s5_expand("s5_dxs", dylin, 0, sv["c_cn"], n_tok=n_tok)
    dc_nc = _s5_wgrad_states("s5_dc", sv["xs"], dylin, n_tok=n_tok)
    gx = _s5_scan("s5_scan_bwd", dxs, sv["a_re8"], -sv["a_im8"], reverse=True, n_tok=n_tok)
    da = _s5_da(sv["xs"], gx, n_tok=n_tok)
    ds5u = _s5_contract("s5_du", gx, sv["b_nc"], du_skip, bf16, n_tok=n_tok)
    db_cn = _s5_wgrad_channels("s5_db", proj, u_blk * (u_w // LANE), gx, n_tok=n_tok)
    diag_b = lambda m: jnp.swapaxes(_bd8_diag(m, S5_GROUP, S5_STATE), 1, 2).reshape(S5_LANES, S5_GROUP)
    diag_c = lambda m: jnp.swapaxes(_bd8_diag(m, S5_STATE, S5_GROUP), 1, 2)
    dlr, dli, dldt, db_re, db_im = _s5_param_bwd(
        w["lr"], w["li"], w["ldt"], w["b_re"], w["b_im"], da[:, 0, :].reshape(S5_LANES, 1), da[:, 1, :].reshape(S5_LANES, 1),
        diag_b(db_cn[0]), diag_b(db_cn[1]))
    grads["s5_lambda_re"] = dlr.reshape(S5_GROUPS, S5_STATE)
    grads["s5_lambda_im"] = dli.reshape(S5_GROUPS, S5_STATE)
    grads["s5_log_dt"] = dldt.reshape(S5_GROUPS)
    grads["s5_b_re"] = db_re.reshape(S5_GROUPS, S5_STATE, S5_GROUP)
    grads["s5_b_im"] = db_im.reshape(S5_GROUPS, S5_STATE, S5_GROUP)
    grads["s5_c_re"] = diag_c(dc_nc[0])
    grads["s5_c_im"] = -diag_c(dc_nc[1])

    dfqn, dfkn, dfv, dbc, dbr = _attn_bwd("fox_attn_bwd", sv["fqn"], sv["fkn"], sv["fvb"], sv["y_fox"], sv["lse_fox"], dy_fox,
                                          sv["cum"], sv["cum_row"], mla=False, n_tok=n_tok)
    dcq8 = jnp.pad(dbc[:, 0::64], ((0, 0), (0, LANE - HEADS)))
    dck8 = jnp.pad(dbr[:, :, 0, :].reshape(HEADS, n_tok).T, ((0, 0), (0, LANE - HEADS)))

    def fox_gate_bwd_body(dq_ref, dk_ref, ff_ref, bf_ref, dff_ref, dbf_ref):
        xg = dq_ref[...] + dk_ref[...]
        t = lax.broadcasted_iota(jnp.int32, xg.shape, 0)
        s = 1
        while s < n_tok:
            xg = xg + jnp.where(t < n_tok - s, pltpu.roll(xg, n_tok - s, 0), 0.0)
            s *= 2
        dff = xg * _sigmoid(-(ff_ref[...] + bf_ref[...]))
        dff_ref[...] = dff.astype(bf16)
        dbf_ref[...] = jnp.sum(dff, axis=0, keepdims=True)

    ff_w, ff_blk = _seg("ff")
    dff, dbf = pl.pallas_call(
        fox_gate_bwd_body, grid=(1,),
        in_specs=[pl.BlockSpec((n_tok, LANE), lambda i: (0, 0)), pl.BlockSpec((n_tok, LANE), lambda i: (0, 0)),
                  pl.BlockSpec((n_tok, ff_w), lambda i: (0, ff_blk)), pl.BlockSpec((1, LANE), lambda i: (0, 0))],
        out_specs=[pl.BlockSpec((n_tok, LANE), lambda i: (0, 0)), pl.BlockSpec((1, LANE), lambda i: (0, 0))],
        out_shape=[jax.ShapeDtypeStruct((n_tok, LANE), bf16), jax.ShapeDtypeStruct((1, LANE), f32)], name="fox_gate_bwd",
    )(dcq8, dck8, proj, w["b_f"])
    grads["fox_b_f"] = dbf[0, :HEADS]

    def fox_prep_bwd_body(fq_ref, fk_ref, dqn_ref, dkn_ref, dv_ref, qg_ref, kg_ref, dfq_ref, dfk_ref, dfv_ref, dqg_ref, dkg_ref):
        lane = lax.broadcasted_iota(jnp.int32, (TOK, LANE), 1)
        dqg = jnp.zeros((1, LANE), f32)
        dkg = jnp.zeros((1, LANE), f32)
        for blk in range(4):
            sl = slice(blk * LANE, (blk + 1) * LANE)
            xq = fq_ref[:, sl]
            dx, dg = _fox_halves_bwd(dqn_ref[:, sl], xq, _fox_halves(xq, lane), qg_ref[...], lane)
            dfq_ref[:, sl] = dx.astype(bf16)
            dqg = dqg + dg
            xk = fk_ref[:, sl]
            dx, dg = _fox_halves_bwd(dkn_ref[:, sl], xk, _fox_halves(xk, lane), kg_ref[...], lane)
            dfk_ref[:, sl] = dx.astype(bf16)
            dkg = dkg + dg
        dfv_ref[...] = dv_ref[...].astype(bf16)
        _accumulate(dqg_ref, dqg + pltpu.roll(dqg, 64, 1))
        _accumulate(dkg_ref, dkg + pltpu.roll(dkg, 64, 1))

    dfq, dfk, dfvb, dfqg, dfkg = _rowwise(
        "fox_prep_bwd", fox_prep_bwd_body, n_tok,
        [(proj, *_seg("fq")), (proj, *_seg("fk")), (dfqn, 512, 0), (dfkn, 512, 0), (dfv, 512, 0)],
        [w["fq_g"], w["fk_g"]], [(512, bf16)] * 3, [((1, LANE), f32)] * 2)
    grads["fox_q_norm"], grads["fox_k_norm"] = dfqg[0, :FOX_DIM], dfkg[0, :FOX_DIM]

    dqn, dkn, dv_mla = _attn_bwd("mla_attn_bwd", sv["qn"], sv["kn"], sv["v_mla"], sv["y_mla"], sv["lse_mla"], dy_mla,
                                 None, None, mla=True, n_tok=n_tok)

    def mla_prep_bwd_body(cq_ref, ckv_ref, kpe_ref, c_ref, s1_ref, s2_ref, dqn_ref, dkn_ref, dv_ref,
                          qa_ref, kva_ref, wq_ref, wkv_ref, qn_g_ref, kn_g_ref,
                          dcq_ref, dckv_ref, dkpe_ref, dwq_ref, dwkv_ref, dqa_ref, dkva_ref, dqng_ref, dkng_ref):
        c, s1, s2 = c_ref[...], s1_ref[...], s2_ref[...]
        cq, ckv = cq_ref[...], ckv_ref[...]
        cqn_b, r_cq, ckvn_b, r_ckv, q_raw, kv_raw, kpe_rot = _mla_recompute(
            cq, ckv, kpe_ref[...], c, s1, s2, qa_ref[...], kva_ref[...], wq_ref[...], wkv_ref[...])
        lane = lax.broadcasted_iota(jnp.int32, (TOK, LANE), 1)
        dq_raw, dk_raw = [], []
        dkpe_rot = jnp.zeros((TOK, LANE), f32)
        dqng = jnp.zeros((1, LANE), f32)
        dkng = jnp.zeros((1, LANE), f32)
        for hd in range(HEADS):
            sl = slice(hd * LANE, (hd + 1) * LANE)
            q_rot = _rope(q_raw[:, sl], c, s1, s2)
            r = lax.rsqrt(jnp.sum(q_rot * q_rot, axis=-1, keepdims=True) * (1.0 / MLA_QK) + EPS)
            dx, dg = _rms_bwd(dqn_ref[:, sl], q_rot, r, qn_g_ref[...], MLA_QK)
            dqng = dqng + dg
            dq_raw.append(_rope_t(dx, c, s1, s2))
            k_full = kv_raw[:, sl] + kpe_rot
            r = lax.rsqrt(jnp.sum(k_full * k_full, axis=-1, keepdims=True) * (1.0 / MLA_QK) + EPS)
            dx, dg = _rms_bwd(dkn_ref[:, sl], k_full, r, kn_g_ref[...], MLA_QK)
            dkng = dkng + dg
            dk_raw.append(jnp.where(lane < 64, dx, 0.0))
            dkpe_rot = dkpe_rot + dx
        dkpe = _rope_t(dkpe_rot, c, s1, s2)
        dkpe_ref[...] = jnp.where(jnp.logical_and(lane >= 64, lane < 64 + ROPE), dkpe, 0.0).astype(bf16)
        dq_raw = jnp.concatenate(dq_raw, axis=1).astype(bf16)
        dkv_raw = jnp.concatenate(dk_raw + [dv_ref[...]], axis=1).astype(bf16)
        dcqn = _nt(dq_raw, wq_ref[...])
        dckvn = _nt(dkv_raw, wkv_ref[...])
        dx, dg = _rms_bwd(dcqn, cq, r_cq, qa_ref[...], MLA_Q_RANK)
        dcq_ref[...] = dx.astype(bf16)
        _accumulate(dqa_ref, dg)
        dx, dg = _rms_bwd(dckvn, ckv, r_ckv, kva_ref[...], MLA_KV_RANK)
        dckv_ref[...] = dx.astype(bf16)
        _accumulate(dkva_ref, dg)
        _accumulate(dwq_ref, _tn(cqn_b, dq_raw))
        _accumulate(dwkv_ref, _tn(ckvn_b, dkv_raw))
        _accumulate(dqng_ref, dqng)
        _accumulate(dkng_ref, dkng)

    dcq, dckv, dkpe, dwq, dwkv, dqa, dkva, dqng, dkng = _rowwise(
        "mla_prep_bwd", mla_prep_bwd_body, n_tok,
        [(proj, *_seg("cq")), (proj, *_seg("ckv")), (proj, *_seg("kpe")), (c_tab, LANE, 0), (s1_tab, LANE, 0), (s2_tab, LANE, 0),
         (dqn, HEADS * LANE, 0), (dkn, HEADS * LANE, 0), (dv_mla, 512, 0)],
        [w["qa_g"], w["kva_g"], w["wq"], w["wkv"], w["qn_g"], w["kn_g"]],
        [(MLA_Q_RANK, bf16), (LANE, bf16), (LANE, bf16)],
        [((MLA_Q_RANK, HEADS * LANE), f32), ((MLA_KV_RANK, HEADS * LANE + 512), f32), ((1, MLA_Q_RANK), f32),
         ((1, MLA_KV_RANK), f32), ((1, LANE), f32), ((1, LANE), f32)])
    grads["mla_w_q_up"] = dwq.reshape(MLA_Q_RANK, HEADS, LANE)[:, :, :MLA_QK].reshape(MLA_Q_RANK, HEADS * MLA_QK)
    dwk = dwkv[:, :HEADS * LANE].reshape(MLA_KV_RANK, HEADS, LANE)[:, :, :64]
    dwv = dwkv[:, HEADS * LANE:].reshape(MLA_KV_RANK, HEADS, 64)
    grads["mla_w_kv_up"] = jnp.concatenate([dwk, dwv], axis=2).reshape(MLA_KV_RANK, HEADS * 128)
    grads["mla_q_a_norm"], grads["mla_kv_a_norm"] = dqa.reshape(-1), dkva.reshape(-1)
    grads["mla_q_norm"], grads["mla_k_norm"] = dqng[0, :MLA_QK], dkng[0, :MLA_QK]

    _, _, shard_c, shard_cp = _BIG_SHARD["w_in"]
    kpe0 = _PAD["kpe"][2]
    dproj = jnp.concatenate([dcq, dckv, dkpe[:, kpe0:kpe0 + ROPE], dfq, dfk, dfvb, dff[:, :HEADS], ds5u, dg_mla, dg_fox, dg_s5,
                             dm_mla, dm_fox, dm_s5], axis=1)
    gap = jnp.zeros((n_tok, shard_cp - shard_c), bf16)
    dproj = jnp.concatenate([p for j in range(N_CHIPS) for p in (dproj[:, j * shard_c:(j + 1) * shard_c], gap)], axis=1)
    ct = 256
    per = shard_cp // ct
    dh = _mm("in_proj_dgrad", dproj, w["w_in_shards"], mode="nt", grid=(1, 1, N_CHIPS * per),
             a_spec=pl.BlockSpec((n_tok, ct), lambda i, j, kk: (0, kk)),
             b_spec=pl.BlockSpec((None, D_MODEL, ct), lambda i, j, kk: (kk // per, 0, kk % per)),
             o_spec=pl.BlockSpec((n_tok, D_MODEL), lambda i, j, kk: (0, 0)),
             out_shape=jax.ShapeDtypeStruct((n_tok, D_MODEL), f32), acc_shape=(n_tok, D_MODEL))
    grads["w_in"] = _mm("in_proj_wgrad", sv["h"], dproj, mode="tn", grid=(1, N_CHIPS * per, 1),
                        a_spec=pl.BlockSpec((n_tok, D_MODEL), lambda i, j, kk: (0, 0)),
                        b_spec=pl.BlockSpec((n_tok, ct), lambda i, j, kk: (0, j)),
                        o_spec=pl.BlockSpec((None, D_MODEL, ct), lambda i, j, kk: (j // per, 0, j % per)),
                        out_shape=jax.ShapeDtypeStruct((N_CHIPS, D_MODEL, shard_cp), f32), acc_shape=(D_MODEL, ct))

    def norm_bwd_body(dh_ref, x_ref, do_ref, g_ref, dx_ref, dg_ref):
        xv = x_ref[...]
        r = lax.rsqrt(jnp.sum(xv * xv, axis=-1, keepdims=True) * (1.0 / D_MODEL) + EPS)
        dx, dg = _rms_bwd(dh_ref[...], xv, r, g_ref[...], D_MODEL)
        dx_ref[...] = do_ref[...] + dx
        _accumulate(dg_ref, dg)

    dx, dng = _rowwise("norm_bwd", norm_bwd_body, n_tok, [(dh, D_MODEL, 0), (x, D_MODEL, 0), (dout, D_MODEL, 0)],
                       [w["norm_g"]], [(D_MODEL, f32)], [((1, D_MODEL), f32)])
    grads["norm_g"] = dng.reshape(D_MODEL)
    return dx, grads


def _rope_tables(positions):
    inv = 1.0 / (ROPE_THETA ** (jnp.arange(0, ROPE, 2, dtype=f32) / ROPE))
    ang = positions.astype(f32).reshape(-1, 1) * inv
    cos, sin = jnp.cos(ang), jnp.sin(ang)
    n = ang.shape[0]
    z16, z32, z64 = jnp.zeros((n, 16), f32), jnp.zeros((n, 32), f32), jnp.zeros((n, 64), f32)
    c = jnp.concatenate([jnp.ones((n, 64), f32), cos, cos, z32], axis=1)
    s1 = jnp.concatenate([z64, -sin, z16, z32], axis=1)
    s2 = jnp.concatenate([z64, z16, sin, z32], axis=1)
    return c, s1, s2


BIG = ("w_in", "mla_w_q_up", "mla_w_kv_up", "s5_w_glu", "w_branch_out", "w_out")
SMALL = ("norm_g", "mla_q_a_norm", "mla_kv_a_norm", "mla_q_norm", "mla_k_norm", "fox_b_f", "fox_q_norm", "fox_k_norm",
         "s5_lambda_re", "s5_lambda_im", "s5_log_dt", "s5_b_re", "s5_b_im", "s5_c_re", "s5_c_im", "s5_d", "s5_b_glu")
WEIGHTS = ("norm_g", "w_in", "mla_q_a_norm", "mla_w_q_up", "mla_kv_a_norm", "mla_w_kv_up", "mla_q_norm", "mla_k_norm",
           "fox_b_f", "fox_q_norm", "fox_k_norm", "s5_lambda_re", "s5_lambda_im", "s5_log_dt", "s5_b_re", "s5_b_im",
           "s5_c_re", "s5_c_im", "s5_d", "s5_w_glu", "s5_b_glu", "w_branch_out", "w_out")


def _local_step(x, positions, loss_target, small, big):
    n_tok = x.shape[0]
    tabs = _rope_tables(positions)
    ws, saves = [], []
    hcur = x
    for l in range(DEPTH):
        w = _layer_weights(l, small, big)
        hcur, sv = _layer_fwd(hcur, w, tabs, n_tok)
        ws.append(w)
        saves.append(sv)

    def loss_body(y_ref, t_ref, d_ref, l_ref):
        err = y_ref[...] - t_ref[...]
        d_ref[...] = err * (1.0 / D_MODEL)
        tot = jnp.sum(jnp.sum(err * err, axis=-1, keepdims=True), axis=0, keepdims=True)
        _accumulate(l_ref, jnp.broadcast_to(tot * (0.5 / D_MODEL), (1, LANE)))

    dcur, loss = _rowwise("loss", loss_body, n_tok, [(hcur, D_MODEL, 0), (loss_target, D_MODEL, 0)], [], [(D_MODEL, f32)],
                          [((1, LANE), f32)])
    layer_grads = [None] * DEPTH
    for l in reversed(range(DEPTH)):
        dcur, layer_grads[l] = _layer_bwd(dcur, ws[l], saves[l], tabs, n_tok)
    grads = {n: jnp.stack([layer_grads[l][n] for l in range(DEPTH)]) for n in WEIGHTS}
    return loss[0, 0], dcur, grads


N_DEV = 8
_ANY = pl.BlockSpec(memory_space=pl.ANY)
_MESH = pl.DeviceIdType.MESH


def _all_gather8(name, blk):
    m = blk.shape[0]

    def body(x_ref, out_ref, send_sems, recv_sems, local_sem):
        x, y, c = lax.axis_index("x"), lax.axis_index("y"), lax.axis_index("c")
        me, sibling = (x, y, c), (x, y, 1 - c)
        chips = [(1 - x, y), (x, 1 - y), (1 - x, 1 - y)]

        def slot(px, py, pc):
            return out_ref.at[4 * px + 2 * py + pc]

        def copy(k, block, to, src=None):
            return pltpu.make_async_remote_copy(
                src_ref=slot(*block) if src is None else src, dst_ref=slot(*block),
                send_sem=send_sems.at[k], recv_sem=recv_sems.at[k], device_id=to, device_id_type=_MESH)

        mine = pltpu.make_async_copy(x_ref, slot(*me), local_sem)
        mine.start()
        first = [copy(0, me, sibling, src=x_ref)]
        first += [copy(1 + j, me, (*chip, c), src=x_ref) for j, chip in enumerate(chips)]
        for cp in first:
            cp.start()
        passed = [copy(4 + j, (*chip, c), sibling) for j, chip in enumerate(chips)]
        for j, chip in enumerate(chips):
            copy(1 + j, (*chip, c), me).wait_recv()
            passed[j].start()
        copy(0, sibling, me).wait_recv()
        for j, chip in enumerate(chips):
            copy(4 + j, (*chip, 1 - c), me).wait_recv()
        for cp in first + passed:
            cp.wait_send()
        mine.wait()

    return pl.pallas_call(
        body, out_shape=jax.ShapeDtypeStruct((N_DEV, m, LANE), blk.dtype), in_specs=[_ANY], out_specs=_ANY, name=name,
        scratch_shapes=[pltpu.SemaphoreType.DMA((7,)), pltpu.SemaphoreType.DMA((7,)), pltpu.SemaphoreType.DMA],
    )(blk)


def _gather_layers(name, shards):
    n = len(shards)

    def body(*refs):
        x_refs, out_refs = refs[:n], refs[n:2 * n]
        send_sems, recv_sems, local_sems = refs[2 * n:]
        x, y, c = lax.axis_index("x"), lax.axis_index("y"), lax.axis_index("c")
        me, sibling = (x, y, c), (x, y, 1 - c)
        chips = [(1 - x, y), (x, 1 - y), (1 - x, 1 - y)]

        def copy(w, k, block, to, src=None):
            px, py, pc = block
            slot = out_refs[w].at[pc, 2 * px + py]
            return pltpu.make_async_remote_copy(
                src_ref=slot if src is None else src, dst_ref=slot, send_sem=send_sems.at[7 * w + k],
                recv_sem=recv_sems.at[7 * w + k], device_id=to, device_id_type=_MESH)

        started, local = [], []
        for w in range(n):
            src = x_refs[w].at[c]
            mine = pltpu.make_async_copy(src, out_refs[w].at[c, 2 * x + y], local_sems.at[w])
            mine.start()
            local.append(mine)
            first = [copy(w, 0, me, sibling, src=src)] + [copy(w, 1 + j, me, (*chip, c), src=src) for j, chip in enumerate(chips)]
            for cp in first:
                cp.start()
            started += first
        for w in range(n):
            for j, chip in enumerate(chips):
                copy(w, 1 + j, (*chip, c), me).wait_recv()
                onward = copy(w, 4 + j, (*chip, c), sibling)
                onward.start()
                started.append(onward)
        for w in range(n):
            copy(w, 0, sibling, me).wait_recv()
            for j, chip in enumerate(chips):
                copy(w, 4 + j, (*chip, 1 - c), me).wait_recv()
        for cp in started:
            cp.wait_send()
        for cp in local:
            cp.wait()

    return pl.pallas_call(
        body, out_shape=[jax.ShapeDtypeStruct((2, N_CHIPS) + s.shape[1:], s.dtype) for s in shards],
        in_specs=[_ANY] * n, out_specs=[_ANY] * n, name=name,
        scratch_shapes=[pltpu.SemaphoreType.DMA((7 * n,)), pltpu.SemaphoreType.DMA((7 * n,)), pltpu.SemaphoreType.DMA((n,))],
    )(*shards)


def _swap_layers(name, parts):
    n = len(parts)

    def body(*refs):
        p_refs, got_refs = refs[:n], refs[n:2 * n]
        send_sems, recv_sems = refs[2 * n:]
        x, y, c = lax.axis_index("x"), lax.axis_index("y"), lax.axis_index("c")
        copies = []
        for w in range(n):
            cp = pltpu.make_async_remote_copy(
                src_ref=p_refs[w].at[1 - c], dst_ref=got_refs[w], send_sem=send_sems.at[w], recv_sem=recv_sems.at[w],
                device_id=(x, y, 1 - c), device_id_type=_MESH)
            cp.start()
            copies.append(cp)
        for cp in copies:
            cp.wait()

    return pl.pallas_call(
        body, out_shape=[jax.ShapeDtypeStruct(p.shape[1:], p.dtype) for p in parts], in_specs=[_ANY] * n, out_specs=[_ANY] * n,
        name=name, scratch_shapes=[pltpu.SemaphoreType.DMA((n,)), pltpu.SemaphoreType.DMA((n,))],
    )(*parts)


def _scatter_to_chips(name, parts):
    n = len(parts)

    def body(*refs):
        p_refs, out_refs = refs[:n], refs[n:2 * n]
        send_sems, recv_sems, local_sems = refs[2 * n:]
        x, y, c = lax.axis_index("x"), lax.axis_index("y"), lax.axis_index("c")
        jme = 2 * x + y
        chips = [(1 - x, y), (x, 1 - y), (1 - x, 1 - y)]
        sends, local = [], []
        for w in range(n):
            mine = pltpu.make_async_copy(p_refs[w].at[jme], out_refs[w].at[jme], local_sems.at[w])
            mine.start()
            local.append(mine)
            for k, (tx, ty) in enumerate(chips):
                cp = pltpu.make_async_remote_copy(
                    src_ref=p_refs[w].at[2 * tx + ty], dst_ref=out_refs[w].at[jme], send_sem=send_sems.at[3 * w + k],
                    recv_sem=recv_sems.at[3 * w + k], device_id=(tx, ty, c), device_id_type=_MESH)
                cp.start()
                sends.append(cp)
        for w in range(n):
            for k, (tx, ty) in enumerate(chips):
                pltpu.make_async_remote_copy(
                    src_ref=p_refs[w].at[jme], dst_ref=out_refs[w].at[2 * tx + ty], send_sem=send_sems.at[3 * w + k],
                    recv_sem=recv_sems.at[3 * w + k], device_id=(tx, ty, c), device_id_type=_MESH).wait_recv()
        for cp in sends:
            cp.wait_send()
        for cp in local:
            cp.wait()

    return pl.pallas_call(
        body, out_shape=[jax.ShapeDtypeStruct(p.shape, p.dtype) for p in parts], in_specs=[_ANY] * n, out_specs=[_ANY] * n, name=name,
        scratch_shapes=[pltpu.SemaphoreType.DMA((3 * n,)), pltpu.SemaphoreType.DMA((3 * n,)), pltpu.SemaphoreType.DMA((n,))],
    )(*parts)


def _share_layers(name, bufs):
    n = len(bufs)

    def body(*refs):
        out_refs = refs[n:2 * n]
        send_sems, recv_sems = refs[2 * n:]
        x, y, c = lax.axis_index("x"), lax.axis_index("y"), lax.axis_index("c")
        copies = []
        for w in range(n):
            cp = pltpu.make_async_remote_copy(src_ref=out_refs[w].at[c], dst_ref=out_refs[w].at[c], send_sem=send_sems.at[w],
                                              recv_sem=recv_sems.at[w], device_id=(x, y, 1 - c), device_id_type=_MESH)
            cp.start()
            copies.append(cp)
        for w in range(n):
            pltpu.make_async_remote_copy(src_ref=out_refs[w].at[c], dst_ref=out_refs[w].at[1 - c], send_sem=send_sems.at[w],
                                         recv_sem=recv_sems.at[w], device_id=(x, y, 1 - c), device_id_type=_MESH).wait_recv()
        for cp in copies:
            cp.wait_send()

    return pl.pallas_call(
        body, out_shape=[jax.ShapeDtypeStruct(b.shape, b.dtype) for b in bufs], in_specs=[_ANY] * n, out_specs=[_ANY] * n,
        input_output_aliases={w: w for w in range(n)}, name=name,
        scratch_shapes=[pltpu.SemaphoreType.DMA((n,)), pltpu.SemaphoreType.DMA((n,))],
    )(*bufs)


def _row_tile(rows, cols):
    best = 16
    for t in range(16, rows + 1, 16):
        if rows % t == 0 and t * cols * 4 <= 2 * 1024 * 1024:
            best = t
    return best


def _add_pair(name, core, parts, got, out_dtype):
    _, _, r, c = parts.shape
    t = _row_tile(r, c)

    def body(core_ref, a_ref, b_ref, o_ref):
        o_ref[...] = (a_ref[...] + b_ref[...]).astype(o_ref.dtype)

    spec = pl.BlockSpec((None, t, c), lambda j, i, core_ref: (j, i, 0))
    grid_spec = pltpu.PrefetchScalarGridSpec(
        num_scalar_prefetch=1, grid=(N_CHIPS, r // t),
        in_specs=[pl.BlockSpec((None, None, t, c), lambda j, i, core_ref: (core_ref[0], j, i, 0)), spec], out_specs=spec)
    return pl.pallas_call(body, grid_spec=grid_spec, out_shape=jax.ShapeDtypeStruct(got.shape, out_dtype), name=name,
                          compiler_params=pltpu.CompilerParams(dimension_semantics=("arbitrary", "arbitrary")))(core, parts, got)


def _add_four(name, core, a):
    _, r, c = a.shape
    t = _row_tile(r, c)

    def body(core_ref, a0, a1, a2, a3, o_ref):
        o_ref[...] = ((a0[...].astype(f32) + a1[...].astype(f32)) + a2[...].astype(f32)) + a3[...].astype(f32)

    specs = [pl.BlockSpec((None, t, c), functools.partial(lambda i, core_ref, k: (k, i, 0), k=k)) for k in range(N_CHIPS)]
    grid_spec = pltpu.PrefetchScalarGridSpec(
        num_scalar_prefetch=1, grid=(r // t,), in_specs=specs,
        out_specs=pl.BlockSpec((None, t, c), lambda i, core_ref: (core_ref[0], i, 0)))
    return pl.pallas_call(body, grid_spec=grid_spec, out_shape=jax.ShapeDtypeStruct((2, r, c), f32), name=name,
                          compiler_params=pltpu.CompilerParams(dimension_semantics=("arbitrary",)))(core, a, a, a, a)


def _adamw(name, w, g, m, v, tile):
    r, c = w.shape
    c1 = 1.0 - ADAM_B1 ** ADAM_STEP
    c2 = 1.0 - ADAM_B2 ** ADAM_STEP

    def body(w_ref, g_ref, m_ref, v_ref, d_ref, nm_ref, nv_ref):
        gv = g_ref[...]
        nm = ADAM_B1 * m_ref[...] + (1.0 - ADAM_B1) * gv
        nv = ADAM_B2 * v_ref[...] + (1.0 - ADAM_B2) * (gv * gv)
        m_hat = nm / c1
        v_hat = nv / c2
        d_ref[...] = -ADAM_LR * (m_hat / (jnp.sqrt(v_hat) + ADAM_EPS) + ADAM_WD * w_ref[...])
        nm_ref[...] = nm
        nv_ref[...] = nv

    spec = pl.BlockSpec((tile, c), lambda i: (i, 0))
    sds = jax.ShapeDtypeStruct((r, c), f32)
    return pl.pallas_call(body, grid=(r // tile,), in_specs=[spec] * 4, out_specs=[spec] * 3, out_shape=[sds] * 3, name=name,
                          compiler_params=pltpu.CompilerParams(dimension_semantics=("arbitrary",), vmem_limit_bytes=VMEM_LIMIT),
                          )(w, g, m, v)


def _pad_rows(flat, rows):
    return jnp.pad(flat, (0, rows * LANE - flat.shape[0])).reshape(rows, LANE)


def kernel(x, positions, norm_g, w_in, mla_q_a_norm, mla_w_q_up, mla_kv_a_norm, mla_w_kv_up, mla_q_norm, mla_k_norm, fox_b_f, fox_q_norm, fox_k_norm, s5_lambda_re, s5_lambda_im, s5_log_dt, s5_b_re, s5_b_im, s5_c_re, s5_c_im, s5_d, s5_w_glu, s5_b_glu, w_branch_out, w_out, loss_target, m_norm_g, m_w_in, m_mla_q_a_norm, m_mla_w_q_up, m_mla_kv_a_norm, m_mla_w_kv_up, m_mla_q_norm, m_mla_k_norm, m_fox_b_f, m_fox_q_norm, m_fox_k_norm, m_s5_lambda_re, m_s5_lambda_im, m_s5_log_dt, m_s5_b_re, m_s5_b_im, m_s5_c_re, m_s5_c_im, m_s5_d, m_s5_w_glu, m_s5_b_glu, m_w_branch_out, m_w_out, v_norm_g, v_w_in, v_mla_q_a_norm, v_mla_w_q_up, v_mla_kv_a_norm, v_mla_w_kv_up, v_mla_q_norm, v_mla_k_norm, v_fox_b_f, v_fox_q_norm, v_fox_k_norm, v_s5_lambda_re, v_s5_lambda_im, v_s5_log_dt, v_s5_b_re, v_s5_b_im, v_s5_c_re, v_s5_c_im, v_s5_d, v_s5_w_glu, v_s5_b_glu, v_w_branch_out, v_w_out):
    given = dict(locals())
    wts = {n: given[n] for n in WEIGHTS}
    mom1 = {n: given["m_" + n] for n in WEIGHTS}
    mom2 = {n: given["v_" + n] for n in WEIGHTS}

    def lanes(n, a):
        _, _, c, cp = _BIG_SHARD[n]
        return jnp.pad(a, ((0, 0), (0, 0), (0, cp - c)))

    gathered = _gather_layers("gather_weights", [lanes(n, wts[n].astype(bf16)) for n in BIG])
    big = dict(zip(BIG, gathered))
    small = {n: wts[n] for n in SMALL}

    loss_local, grad_x, grads = _local_step(x[0], positions, loss_target[0], small, big)
    loss = lax.psum(loss_local, ("x", "y", "c"))

    small_flat = jnp.concatenate([grads[n].reshape(-1) for n in SMALL])
    small_rows = -(-small_flat.shape[0] // (N_DEV * 16 * LANE)) * 16
    parts = [grads[n] if n == "w_in" else jnp.stack([_to_shards(n, grads[n][l]) for l in range(DEPTH)]) for n in BIG]
    parts.append(jnp.swapaxes(_pad_rows(small_flat, N_DEV * small_rows).reshape(N_CHIPS, 2, small_rows, LANE), 0, 1))
    core = lax.axis_index("c")
    core1 = core.reshape(1).astype(jnp.int32)
    got = _swap_layers("grads_to_sibling", parts)
    hop = [bf16] * len(BIG) + [f32]
    pair = [_add_pair("grads_pair_sum_%d" % i, core1, a, b, dt) for i, (a, b, dt) in enumerate(zip(parts, got, hop))]
    landed = _scatter_to_chips("grads_to_chips", pair)
    total = [_add_four("grads_chip_sum_%d" % i, core1, a) for i, a in enumerate(landed)]
    shared = _share_layers("grads_share", total[:-1])
    small_mine = lax.dynamic_index_in_dim(total[-1], core, 0, keepdims=False)
    small_all = _all_gather8("gather_small_grads", small_mine).reshape(-1)

    g_out = {n: s[:, :, :_BIG_SHARD[n][2]] for n, s in zip(BIG, shared)}
    pos = 0
    for n in SMALL:
        g_out[n] = small_all[pos:pos + wts[n].size].reshape(wts[n].shape)
        pos += wts[n].size

    delta, new_m, new_v = {}, {}, {}
    for n in BIG:
        shp = wts[n].shape
        as2d = lambda a: a.reshape(-1, shp[-1])
        d_, m_, v_ = _adamw("adamw_" + n, as2d(wts[n]), as2d(g_out[n]), as2d(mom1[n]), as2d(mom2[n]), tile=256)
        delta[n], new_m[n], new_v[n] = d_.reshape(shp), m_.reshape(shp), v_.reshape(shp)
    n_small_rows = N_DEV * small_rows
    pack = lambda d: _pad_rows(jnp.concatenate([d[n].reshape(-1) for n in SMALL]), n_small_rows)
    packed = _adamw("adamw_small", pack(wts), small_all.reshape(n_small_rows, LANE), pack(mom1), pack(mom2), tile=n_small_rows)
    for out, res in zip((delta, new_m, new_v), packed):
        res = res.reshape(-1)
        pos = 0
        for n in SMALL:
            out[n] = res[pos:pos + wts[n].size].reshape(wts[n].shape)
            pos += wts[n].size

    return (loss, grad_x[None], *[g_out[n] for n in WEIGHTS], *[delta[n] for n in WEIGHTS],
            *[new_m[n] for n in WEIGHTS], *[new_v[n] for n in WEIGHTS])
```

```python
import functools
import math

import jax
import jax.numpy as jnp
from jax import lax
from jax.experimental import pallas as pl
from jax.experimental.pallas import tpu as pltpu

f32 = jnp.float32
bf16 = jnp.bfloat16

D_MODEL = 1024
DEPTH = 2
EPS = 1e-6
HEADS = 8
MLA_QK = 96
MLA_Q_RANK = 256
MLA_KV_RANK = 128
ROPE = 32
ROPE_THETA = 10000.0
FOX_DIM = 64
S5_GROUPS = 32
S5_GROUP = 16
S5_STATE = 64
S5_LANES = S5_GROUPS * S5_STATE
LANE = 128
S5_BLOCKS = S5_LANES // LANE
IN_WIDTH = 7080
TOK = 256
VMEM_LIMIT = 56 * 1024 * 1024

ADAM_LR = 0.001
ADAM_B1 = 0.9
ADAM_B2 = 0.999
ADAM_EPS = 1e-08
ADAM_WD = 0.01
ADAM_STEP = 10

_ORIG = {}
_off = 0
for _n, _w in (("cq", 256), ("ckv", 128), ("kpe", 32), ("fq", 512), ("fk", 512), ("fv", 512), ("ff", 8), ("s5u", 512),
               ("g_mla", 512), ("g_fox", 512), ("g_s5", 512), ("m_mla", 1024), ("m_fox", 1024), ("m_s5", 1024)):
    _ORIG[_n] = (_off, _w)
    _off += _w
_PAD = {"m_mla": (0, 1024, 0), "m_fox": (1024, 1024, 0), "m_s5": (2048, 1024, 0),
        "fq": (3072, 512, 0), "fk": (3584, 512, 0), "fv": (4096, 512, 0), "s5u": (4608, 512, 0),
        "g_mla": (5120, 512, 0), "g_fox": (5632, 512, 0), "g_s5": (6144, 512, 0),
        "cq": (6656, 256, 0), "ckv": (6912, 128, 0), "kpe": (7040, 128, 64), "ff": (7168, 128, 0)}
NP = 7680
_PAD_ORDER = ("m_mla", "m_fox", "m_s5", "fq", "fk", "fv", "s5u", "g_mla", "g_fox", "g_s5", "cq", "ckv", "kpe", "ff")


def _seg(name):
    start, width, _ = _PAD[name]
    return width, start // width


def _nn(a, b):
    return lax.dot_general(a.astype(bf16), b.astype(bf16), (((1,), (0,)), ((), ())), preferred_element_type=f32)


def _nt(a, b):
    return lax.dot_general(a.astype(bf16), b.astype(bf16), (((1,), (1,)), ((), ())), preferred_element_type=f32)


def _tn(a, b):
    return lax.dot_general(a.astype(bf16), b.astype(bf16), (((0,), (0,)), ((), ())), preferred_element_type=f32)


def _rms(x, g, n):
    r = lax.rsqrt(jnp.sum(x * x, axis=-1, keepdims=True) * (1.0 / n) + EPS)
    return x * r * g, r


def _rms_bwd(dy, x, r, g, n):
    xh = x * r
    dg = jnp.sum(dy * xh, axis=0, keepdims=True)
    dxh = dy * g
    dx = r * (dxh - xh * (jnp.sum(dxh * xh, axis=-1, keepdims=True) * (1.0 / n)))
    return dx, dg


def _sigmoid(x):
    return 1.0 / (1.0 + jnp.exp(-x))


_GELU_C = math.sqrt(2.0 / math.pi)


def _gelu(x):
    t = jnp.tanh(_GELU_C * (x + 0.044715 * x * x * x))
    return 0.5 * x * (1.0 + t), t


def _gelu_grad(x, t):
    return 0.5 * (1.0 + t) + 0.5 * x * (1.0 - t * t) * _GELU_C * (1.0 + 3.0 * 0.044715 * x * x)


def _accumulate(ref, val):
    i = pl.program_id(0)

    @pl.when(i == 0)
    def _():
        ref[...] = val

    @pl.when(i > 0)
    def _():
        ref[...] += val


def _rope(x, c, s1, s2):
    return x * c + pltpu.roll(x, LANE - 16, 1) * s1 + pltpu.roll(x, 16, 1) * s2


def _rope_t(d, c, s1, s2):
    return d * c + pltpu.roll(d * s1, 16, 1) + pltpu.roll(d * s2, LANE - 16, 1)


def _const_map(ndim):
    return lambda *_: (0,) * ndim


def _rowwise(name, body, n_tok, tiled_in, full_in, tiled_out, acc_out, tile=TOK):
    in_specs, args = [], []
    for arr, width, blk in tiled_in:
        in_specs.append(pl.BlockSpec((tile, width), functools.partial(lambda i, b: (i, b), b=blk)))
        args.append(arr)
    for arr in full_in:
        in_specs.append(pl.BlockSpec(arr.shape, _const_map(arr.ndim)))
        args.append(arr)
    out_specs, out_shape = [], []
    for width, dt in tiled_out:
        out_specs.append(pl.BlockSpec((tile, width), lambda i: (i, 0)))
        out_shape.append(jax.ShapeDtypeStruct((n_tok, width), dt))
    for shape, dt in acc_out:
        out_specs.append(pl.BlockSpec(shape, _const_map(len(shape))))
        out_shape.append(jax.ShapeDtypeStruct(shape, dt))
    return pl.pallas_call(
        body, grid=(n_tok // tile,), in_specs=in_specs, out_specs=out_specs, out_shape=out_shape, name=name,
        compiler_params=pltpu.CompilerParams(dimension_semantics=("arbitrary",), vmem_limit_bytes=VMEM_LIMIT),
    )(*args)


def _mm(name, a, b, *, mode, grid, a_spec, b_spec, o_spec, out_shape, acc_shape, add=None, add_spec=None):
    nk = grid[2]

    def body(*refs):
        if add is None:
            a_ref, b_ref, o_ref, acc_ref = refs
        else:
            a_ref, b_ref, add_ref, o_ref, acc_ref = refs
        k = pl.program_id(2)

        @pl.when(k == 0)
        def _():
            acc_ref[...] = jnp.zeros_like(acc_ref)

        acc_ref[...] += {"nn": _nn, "nt": _nt, "tn": _tn}[mode](a_ref[...], b_ref[...])

        @pl.when(k == nk - 1)
        def _():
            r = acc_ref[...]
            if add is not None:
                r = r + add_ref[...]
            o_ref[...] = r.astype(o_ref.dtype)

    in_specs = [a_spec, b_spec] + ([add_spec] if add is not None else [])
    args = (a, b) + ((add,) if add is not None else ())
    return pl.pallas_call(
        body, grid=grid, in_specs=in_specs, out_specs=o_spec, out_shape=out_shape, name=name,
        scratch_shapes=[pltpu.VMEM(acc_shape, f32)],
        compiler_params=pltpu.CompilerParams(dimension_semantics=("arbitrary", "arbitrary", "arbitrary"), vmem_limit_bytes=VMEM_LIMIT),
    )(*args)


def _mm_nn(name, a, b, *, m, n, k, tm, tn, tk, out_dtype=f32, a_koff=0):
    return _mm(name, a, b, mode="nn", grid=(m // tm, n // tn, k // tk),
               a_spec=pl.BlockSpec((tm, tk), lambda i, j, kk: (i, kk + a_koff)),
               b_spec=pl.BlockSpec((tk, tn), lambda i, j, kk: (kk, j)),
               o_spec=pl.BlockSpec((tm, tn), lambda i, j, kk: (i, j)),
               out_shape=jax.ShapeDtypeStruct((m, n), out_dtype), acc_shape=(tm, tn))


def _mm_tn(name, a, b, *, m, n, k, tm, tn, tk, a_moff=0):
    return _mm(name, a, b, mode="tn", grid=(m // tm, n // tn, k // tk),
               a_spec=pl.BlockSpec((tk, tm), lambda i, j, kk: (kk, i + a_moff)),
               b_spec=pl.BlockSpec((tk, tn), lambda i, j, kk: (kk, j)),
               o_spec=pl.BlockSpec((tm, tn), lambda i, j, kk: (i, j)),
               out_shape=jax.ShapeDtypeStruct((m, n), f32), acc_shape=(tm, tn))


ATT_KV = 256
ATT_Q = 512


def _attn_common(mla, n_tok):
    qw = 2 * LANE if mla else LANE
    scale = 1.0 / math.sqrt(MLA_QK if mla else FOX_DIM)
    return qw, scale, min(ATT_Q, n_tok)


def _attn_heads(q_ref, mla):
    out = []
    if mla:
        for e in (0, 1):
            qe = q_ref[:, e * LANE:(e + 1) * LANE]
            out.append((qe.astype(f32).T.astype(bf16), qe))
        return out
    q = q_ref[...]
    tq = q.shape[0]
    qt = q.astype(f32).T
    row = lax.broadcasted_iota(jnp.int32, (LANE, tq), 0)
    lane = lax.broadcasted_iota(jnp.int32, (tq, LANE), 1)
    for e in (0, 1):
        out.append((jnp.where((row >= 64) == bool(e), qt, 0.0).astype(bf16),
                    jnp.where((lane >= 64) == bool(e), q, jnp.zeros((), bf16))))
    return out


def _attn_allowed(off, i, tq, mla):
    kpos = off + lax.broadcasted_iota(jnp.int32, (ATT_KV, tq), 0)
    qpos = i * tq + lax.broadcasted_iota(jnp.int32, (ATT_KV, tq), 1)
    return ((kpos // 64) <= (qpos // 64)) if mla else (kpos <= qpos)


def _attn_fwd(name, q, k, v, cum_b, *, mla, n_tok):
    qw, scale, tq = _attn_common(mla, n_tok)
    nq = n_tok // tq
    nkv = n_tok // ATT_KV
    has_bias = cum_b is not None

    def body(*refs):
        if has_bias:
            q_ref, k_ref, v_ref, cb_ref, o_ref, lse_ref, vt_ref = refs
        else:
            q_ref, k_ref, v_ref, o_ref, lse_ref, vt_ref = refs
        i = pl.program_id(1)

        @pl.when(i == 0)
        def _():
            for jb in range(nkv):
                vt_ref[jb] = v_ref[jb * ATT_KV:(jb + 1) * ATT_KV, :].astype(f32).T.astype(bf16)

        heads = _attn_heads(q_ref, mla)

        def step(j, carry, masked):
            off = pl.multiple_of(j * ATT_KV, ATT_KV)
            allowed = _attn_allowed(off, i, tq, mla) if masked else None
            vt = vt_ref[j]
            sts = []
            for e in (0, 1):
                kb = k_ref[pl.ds(off, ATT_KV), e * LANE:(e + 1) * LANE] if mla else k_ref[pl.ds(off, ATT_KV), :]
                sts.append(_nn(kb, heads[e][0]))
            stats = []
            for e in (0, 1):
                m, l, _ = carry[e]
                st = sts[e] * scale
                if has_bias:
                    st = st - jnp.tile(cb_ref[e, pl.ds(off, ATT_KV), :], (1, tq // LANE))
                if masked:
                    st = jnp.where(allowed, st, -1e30)
                m_new = jnp.maximum(m, jnp.max(st, axis=0, keepdims=True))
                alpha = jnp.exp(m - m_new)
                pt = jnp.exp(st - m_new)
                stats.append((m_new, alpha * l + jnp.sum(pt, axis=0, keepdims=True), alpha, pt.astype(bf16)))
            new = []
            for e in (0, 1):
                m_new, l, alpha, pt = stats[e]
                new.append((m_new, l, alpha * carry[e][2] + _nn(vt[64 * e:64 * e + 64, :], pt)))
            return tuple(new)

        init = tuple((jnp.full((1, tq), -1e30, f32), jnp.zeros((1, tq), f32), jnp.zeros((64, tq), f32)) for _ in (0, 1))
        n_full = i * (tq // ATT_KV)
        carry = lax.fori_loop(0, n_full, functools.partial(step, masked=False), init)
        for d in range(tq // ATT_KV):
            carry = step(n_full + d, carry, True)
        o_ref[...] = jnp.concatenate([carry[e][2] / carry[e][1] for e in (0, 1)], axis=0).T
        lse_ref[...] = jnp.zeros_like(lse_ref)
        for e in (0, 1):
            lse_ref[e:e + 1, :] = carry[e][0] + jnp.log(carry[e][1])

    in_specs = [pl.BlockSpec((tq, qw), lambda p, i: (i, p)),
                pl.BlockSpec((n_tok, qw), lambda p, i: (0, p)),
                pl.BlockSpec((n_tok, LANE), lambda p, i: (0, p))]
    args = [q, k, v]
    if has_bias:
        in_specs.append(pl.BlockSpec((2, n_tok, LANE), lambda p, i: (p, 0, 0)))
        args.append(cum_b)
    return pl.pallas_call(
        body, grid=(4, nq), in_specs=in_specs,
        out_specs=[pl.BlockSpec((tq, LANE), lambda p, i: (i, p)), pl.BlockSpec((None, 8, tq), lambda p, i: (p, 0, i))],
        out_shape=[jax.ShapeDtypeStruct((n_tok, 512), f32), jax.ShapeDtypeStruct((4, 8, n_tok), f32)], name=name,
        scratch_shapes=[pltpu.VMEM((nkv, LANE, ATT_KV), bf16)],
        compiler_params=pltpu.CompilerParams(dimension_semantics=("arbitrary", "arbitrary"), vmem_limit_bytes=VMEM_LIMIT),
    )(*args)


def _attn_bwd(name, q, k, v, o, lse, do, cum_b, *, mla, n_tok):
    qw, scale, tq = _attn_common(mla, n_tok)
    nq = n_tok // tq
    nkv = n_tok // ATT_KV
    has_bias = cum_b is not None

    def body(*refs):
        if has_bias:
            q_ref, k_ref, v_ref, o_ref, lse_ref, do_ref, cb_ref, dq_ref, dk_ref, dv_ref, dck_ref, dcq_ref, kt_ref = refs
        else:
            q_ref, k_ref, v_ref, o_ref, lse_ref, do_ref, dq_ref, dk_ref, dv_ref, kt_ref = refs
        p = pl.program_id(0)
        i = pl.program_id(1)

        @pl.when(i == 0)
        def _():
            dk_ref[...] = jnp.zeros_like(dk_ref)
            dv_ref[...] = jnp.zeros_like(dv_ref)
            for jb in range(nkv):
                for c0 in range(0, qw, LANE):
                    kt_ref[jb, c0:c0 + LANE, :] = k_ref[jb * ATT_KV:(jb + 1) * ATT_KV, c0:c0 + LANE].astype(f32).T.astype(bf16)

        if has_bias:
            @pl.when(jnp.logical_and(i == 0, p == 0))
            def _():
                dck_ref[...] = jnp.zeros_like(dck_ref)

        heads = _attn_heads(q_ref, mla)
        do = do_ref[...]
        do_t = do.T
        prod_t = (do * o_ref[...]).T
        row = lax.broadcasted_iota(jnp.int32, (LANE, tq), 0)
        lane = lax.broadcasted_iota(jnp.int32, (tq, LANE), 1)
        lane_k = lax.broadcasted_iota(jnp.int32, (ATT_KV, LANE), 1)
        per_head = []
        for e in (0, 1):
            sel_r = (row >= 64) == bool(e)
            per_head.append((jnp.where(sel_r, do_t, 0.0).astype(bf16),
                             jnp.where((lane >= 64) == bool(e), do, 0.0).astype(bf16),
                             jnp.sum(jnp.where(sel_r, prod_t, 0.0), axis=0, keepdims=True),
                             lse_ref[e:e + 1, :]))
        dq_rows = LANE if mla else 64

        def step(j, carry, masked):
            off = pl.multiple_of(j * ATT_KV, ATT_KV)
            allowed = _attn_allowed(off, i, tq, mla) if masked else None
            vb = v_ref[pl.ds(off, ATT_KV), :]
            kt = kt_ref[j]
            cols = [slice(e * LANE, (e + 1) * LANE) if mla else slice(None) for e in (0, 1)]
            sts = [_nn(k_ref[pl.ds(off, ATT_KV), cols[e]], heads[e][0]) for e in (0, 1)]
            dpts = [_nn(vb, per_head[e][0]) for e in (0, 1)]
            mids = []
            for e in (0, 1):
                _, _, delta, lse_e = per_head[e]
                st = sts[e] * scale
                if has_bias:
                    st = st - jnp.tile(cb_ref[e, pl.ds(off, ATT_KV), :], (1, tq // LANE))
                pt = jnp.exp(st - lse_e)
                if masked:
                    pt = jnp.where(allowed, pt, 0.0)
                dst = pt * (dpts[e] - delta)
                qsum = carry[e][1]
                if has_bias:
                    rs = jnp.sum(dst, axis=1, keepdims=True)
                    dck_ref[pl.ds(off, ATT_KV), :] += jnp.where(lane_k == 2 * p + e, -rs, 0.0)
                    qsum = qsum + jnp.sum(dst, axis=0, keepdims=True)
                mids.append((pt.astype(bf16), dst.astype(bf16), qsum))
            new = []
            for e in (0, 1):
                pt, dst, qsum = mids[e]
                kt_e = kt[e * LANE:(e + 1) * LANE, :] if mla else kt[64 * e:64 * e + 64, :]
                new.append((carry[e][0] + _nn(kt_e, dst) * scale, qsum))
                dk_ref[pl.ds(off, ATT_KV), cols[e]] += _nn(dst, heads[e][1]) * scale
                dv_ref[pl.ds(off, ATT_KV), :] += _nn(pt, per_head[e][1])
            return tuple(new)

        init = tuple((jnp.zeros((dq_rows, tq), f32), jnp.zeros((1, tq), f32)) for _ in (0, 1))
        n_full = i * (tq // ATT_KV)
        carry = lax.fori_loop(0, n_full, functools.partial(step, masked=False), init)
        for d in range(tq // ATT_KV):
            carry = step(n_full + d, carry, True)
        if mla:
            for e in (0, 1):
                dq_ref[:, e * LANE:(e + 1) * LANE] = carry[e][0].T
        else:
            dq_ref[...] = jnp.concatenate([carry[0][0], carry[1][0]], axis=0).T
        if has_bias:
            dcq_ref[...] = jnp.zeros_like(dcq_ref)
            for e in (0, 1):
                dcq_ref[e:e + 1, :] = carry[e][1]

    tile_q = pl.BlockSpec((tq, qw), lambda p, i: (i, p))
    tile_v = pl.BlockSpec((tq, LANE), lambda p, i: (i, p))
    full_k = pl.BlockSpec((n_tok, qw), lambda p, i: (0, p))
    full_v = pl.BlockSpec((n_tok, LANE), lambda p, i: (0, p))
    in_specs = [tile_q, full_k, full_v, tile_v, pl.BlockSpec((None, 8, tq), lambda p, i: (p, 0, i)), tile_v]
    args = [q, k, v, o, lse, do]
    out_specs = [tile_q, full_k, full_v]
    out_shape = [jax.ShapeDtypeStruct((n_tok, 4 * qw), f32), jax.ShapeDtypeStruct((n_tok, 4 * qw), f32),
                 jax.ShapeDtypeStruct((n_tok, 512), f32)]
    if has_bias:
        in_specs.append(pl.BlockSpec((2, n_tok, LANE), lambda p, i: (p, 0, 0)))
        args.append(cum_b)
        out_specs += [pl.BlockSpec((n_tok, LANE), _const_map(2)), pl.BlockSpec((None, 8, tq), lambda p, i: (p, 0, i))]
        out_shape += [jax.ShapeDtypeStruct((n_tok, LANE), f32), jax.ShapeDtypeStruct((4, 8, n_tok), f32)]
    return pl.pallas_call(
        body, grid=(4, nq), in_specs=in_specs, out_specs=out_specs, out_shape=out_shape, name=name,
        scratch_shapes=[pltpu.VMEM((nkv, qw, ATT_KV), bf16)],
        compiler_params=pltpu.CompilerParams(dimension_semantics=("arbitrary", "arbitrary"), vmem_limit_bytes=VMEM_LIMIT),
    )(*args)


def _s5_disc(lr, li, ldt):
    dt = jnp.exp(ldt)
    mag = jnp.exp(lr * dt)
    a_re = mag * jnp.cos(li * dt)
    a_im = mag * jnp.sin(li * dt)
    den = lr * lr + li * li
    f_re = ((a_re - 1.0) * lr + a_im * li) / den
    f_im = (a_im * lr - (a_re - 1.0) * li) / den
    return a_re, a_im, f_re, f_im


def _s5_param_fwd(lr, li, ldt, b_re, b_im):
    def body(lr_ref, li_ref, ldt_ref, br_ref, bi_ref, ar_ref, ai_ref, bbr_ref, bbi_ref):
        a_re, a_im, f_re, f_im = _s5_disc(lr_ref[...], li_ref[...], ldt_ref[...])
        ar_ref[...] = a_re
        ai_ref[...] = a_im
        br, bi = br_ref[...], bi_ref[...]
        bbr_ref[...] = f_re * br - f_im * bi
        bbi_ref[...] = f_re * bi + f_im * br

    col = jax.ShapeDtypeStruct((S5_LANES, 1), f32)
    mat = jax.ShapeDtypeStruct((S5_LANES, S5_GROUP), f32)
    return pl.pallas_call(body, out_shape=[col, col, mat, mat], name="s5_param_fwd")(lr, li, ldt, b_re, b_im)


def _s5_param_bwd(lr, li, ldt, b_re, b_im, da_re, da_im, dbb_re, dbb_im):
    def body(lr_ref, li_ref, ldt_ref, br_ref, bi_ref, dar_ref, dai_ref, gbr_ref, gbi_ref,
             dlr_ref, dli_ref, dldt_ref, dbr_ref, dbi_ref):
        (a_re, a_im, f_re, f_im), vjp = jax.vjp(_s5_disc, lr_ref[...], li_ref[...], ldt_ref[...])
        br, bi, gr, gi = br_ref[...], bi_ref[...], gbr_ref[...], gbi_ref[...]
        dbr_ref[...] = f_re * gr + f_im * gi
        dbi_ref[...] = f_re * gi - f_im * gr
        dfr = jnp.sum(br * gr + bi * gi, axis=-1, keepdims=True)
        dfi = jnp.sum(br * gi - bi * gr, axis=-1, keepdims=True)
        dlr, dli, dldt = vjp((dar_ref[...], dai_ref[...], dfr, dfi))
        dlr_ref[...] = dlr
        dli_ref[...] = dli
        dldt_ref[...] = jnp.sum(dldt.reshape(S5_GROUPS, S5_STATE, 1), axis=1)

    col = jax.ShapeDtypeStruct((S5_LANES, 1), f32)
    mat = jax.ShapeDtypeStruct((S5_LANES, S5_GROUP), f32)
    return pl.pallas_call(body, out_shape=[col, col, jax.ShapeDtypeStruct((S5_GROUPS, 1), f32), mat, mat],
                          name="s5_param_bwd")(lr, li, ldt, b_re, b_im, da_re, da_im, dbb_re, dbb_im)


_SCAN_NB = 4


def _to_streams(a):
    s, c = a.shape
    return jnp.swapaxes(a.reshape(8, s // 8, c), 0, 1).reshape(s, c)


def _from_streams(a):
    s, c = a.shape
    return jnp.swapaxes(a.reshape(s // 8, 8, c), 0, 1).reshape(s, c)


def _s5_scan(name, bu, a_re8, a_im8, *, reverse, n_tok):
    rows = n_tok // 8
    nb = _SCAN_NB

    def body(bu_ref, ar_ref, ai_ref, x_ref):
        a_r = [ar_ref[b] for b in range(nb)]
        a_i = [ai_ref[b] for b in range(nb)]
        zero = jnp.zeros((8, LANE), f32)
        one = jnp.ones((8, LANE), f32)

        def rows_at(r):
            rr = (rows - 1 - r) if reverse else r
            return pl.ds(pl.multiple_of(rr * 8, 8), 8)

        def pass1(r, carry):
            out = []
            sl = rows_at(r)
            for b in range(nb):
                xr, xi, mr, mi = carry[b]
                nr = a_r[b] * xr - a_i[b] * xi + bu_ref[0, b, sl, :]
                ni = a_r[b] * xi + a_i[b] * xr + bu_ref[1, b, sl, :]
                x_ref[0, b, sl, :] = nr
                x_ref[1, b, sl, :] = ni
                out.append((nr, ni, a_r[b] * mr - a_i[b] * mi, a_r[b] * mi + a_i[b] * mr))
            return tuple(out)

        carry = lax.fori_loop(0, rows, pass1, tuple((zero, zero, one, zero) for _ in range(nb)))
        sub = lax.broadcasted_iota(jnp.int32, (8, LANE), 0)
        feed = []
        for b in range(nb):
            lr_, li_, pr, pi = carry[b]
            fr, fi = zero, zero
            for _ in range(7):
                tr = lr_ + pr * fr - pi * fi
                ti = li_ + pr * fi + pi * fr
                if reverse:
                    fr = jnp.where(sub < 7, pltpu.roll(tr, 7, 0), 0.0)
                    fi = jnp.where(sub < 7, pltpu.roll(ti, 7, 0), 0.0)
                else:
                    fr = jnp.where(sub > 0, pltpu.roll(tr, 1, 0), 0.0)
                    fi = jnp.where(sub > 0, pltpu.roll(ti, 1, 0), 0.0)
            feed.append((fr, fi))

        def pass2(r, carry):
            out = []
            sl = rows_at(r)
            for b in range(nb):
                mr, mi = carry[b]
                fr, fi = feed[b]
                x_ref[0, b, sl, :] += mr * fr - mi * fi
                x_ref[1, b, sl, :] += mr * fi + mi * fr
                out.append((a_r[b] * mr - a_i[b] * mi, a_r[b] * mi + a_i[b] * mr))
            return tuple(out)

        lax.fori_loop(0, rows, pass2, tuple((a_r[b], a_i[b]) for b in range(nb)))

    blk = pl.BlockSpec((2, nb, n_tok, LANE), lambda g: (0, g, 0, 0))
    ablk = pl.BlockSpec((nb, 8, LANE), lambda g: (g, 0, 0))
    return pl.pallas_call(
        body, grid=(S5_BLOCKS // nb,), in_specs=[blk, ablk, ablk], out_specs=blk,
        out_shape=jax.ShapeDtypeStruct((2, S5_BLOCKS, n_tok, LANE), f32), name=name,
        compiler_params=pltpu.CompilerParams(dimension_semantics=("arbitrary",), vmem_limit_bytes=VMEM_LIMIT),
    )(bu, a_re8, a_im8)


def _s5_da(xs, gx, *, n_tok):
    def body(x_ref, g_ref, o_ref):
        t = lax.broadcasted_iota(jnp.int32, (n_tok, LANE), 0)
        sub = lax.broadcasted_iota(jnp.int32, (8, LANE), 0)

        def prev(v):
            return (jnp.where(t >= 8, pltpu.roll(v, 8, 0), 0.0),
                    jnp.where(sub > 0, pltpu.roll(v[n_tok - 8:, :], 1, 0), 0.0))

        (xr, hr), (xi, hi) = prev(x_ref[0, 0]), prev(x_ref[1, 0])
        gr, gi = g_ref[0, 0], g_ref[1, 0]
        gr0, gi0 = gr[0:8, :], gi[0:8, :]
        o_ref[0, 0:1, :] = (jnp.sum(xr * gr + xi * gi, axis=0, keepdims=True)
                            + jnp.sum(hr * gr0 + hi * gi0, axis=0, keepdims=True))
        o_ref[0, 1:2, :] = (jnp.sum(xr * gi - xi * gr, axis=0, keepdims=True)
                            + jnp.sum(hr * gi0 - hi * gr0, axis=0, keepdims=True))

    blk = pl.BlockSpec((2, 1, n_tok, LANE), lambda g: (0, g, 0, 0))
    return pl.pallas_call(
        body, grid=(S5_BLOCKS,), in_specs=[blk, blk], out_specs=pl.BlockSpec((1, 2, LANE), lambda g: (g, 0, 0)),
        out_shape=jax.ShapeDtypeStruct((S5_BLOCKS, 2, LANE), f32), name="s5_da",
        compiler_params=pltpu.CompilerParams(dimension_semantics=("arbitrary",), vmem_limit_bytes=VMEM_LIMIT),
    )(xs, gx)


S5_Q = 4


def _bd8(t):
    _, a, b = t.shape
    t = t.reshape(S5_Q, 8, a, 1, b)
    eye = jnp.eye(8, dtype=jnp.bool_).reshape(1, 8, 1, 8, 1)
    return jnp.where(eye, jnp.broadcast_to(t, (S5_Q, 8, a, 8, b)), jnp.zeros((), t.dtype)).reshape(S5_Q, 8 * a, 8 * b)


def _bd8_diag(m, a, b):
    m = m.reshape(S5_Q, 8, a, 8, b)
    eye = jnp.eye(8, dtype=jnp.bool_).reshape(1, 8, 1, 8, 1)
    return jnp.sum(jnp.where(eye, m, 0.0), axis=3).reshape(S5_GROUPS, a, b)


def _s5_expand(name, a, a_blk0, wq, *, n_tok):
    def body(a_ref, w_ref, o_ref):
        r = _nn(a_ref[...], w_ref[...])
        for k in range(4):
            o_ref[k] = r[:, k * LANE:(k + 1) * LANE]

    return pl.pallas_call(
        body, grid=(2, S5_Q),
        in_specs=[pl.BlockSpec((n_tok, LANE), lambda ri, q: (0, a_blk0 + q)),
                  pl.BlockSpec((None, None, LANE, 512), lambda ri, q: (ri, q, 0, 0))],
        out_specs=pl.BlockSpec((None, 4, n_tok, LANE), lambda ri, q: (ri, q, 0, 0)),
        out_shape=jax.ShapeDtypeStruct((2, S5_BLOCKS, n_tok, LANE), f32), name=name,
        compiler_params=pltpu.CompilerParams(dimension_semantics=("arbitrary", "arbitrary"), vmem_limit_bytes=VMEM_LIMIT),
    )(a, wq)


def _s5_contract(name, xs, wq, add, out_dtype, *, n_tok):
    def body(*refs):
        if add is None:
            x_ref, w_ref, o_ref, acc_ref = refs
        else:
            x_ref, w_ref, add_ref, o_ref, acc_ref = refs
        ri = pl.program_id(1)
        r = _nn(x_ref[0], w_ref[0:LANE, :])
        for k in range(1, 4):
            r = r + _nn(x_ref[k], w_ref[k * LANE:(k + 1) * LANE, :])

        @pl.when(ri == 0)
        def _():
            acc_ref[...] = r

        @pl.when(ri == 1)
        def _():
            tot = acc_ref[...] + r
            if add is not None:
                tot = tot + add_ref[...]
            o_ref[...] = tot.astype(o_ref.dtype)

    col = pl.BlockSpec((n_tok, LANE), lambda q, ri: (0, q))
    in_specs = [pl.BlockSpec((None, 4, n_tok, LANE), lambda q, ri: (ri, q, 0, 0)),
                pl.BlockSpec((None, None, 512, LANE), lambda q, ri: (ri, q, 0, 0))]
    args = [xs, wq]
    if add is not None:
        in_specs.append(col)
        args.append(add)
    return pl.pallas_call(
        body, grid=(S5_Q, 2), in_specs=in_specs, out_specs=col, out_shape=jax.ShapeDtypeStruct((n_tok, 512), out_dtype), name=name,
        scratch_shapes=[pltpu.VMEM((n_tok, LANE), f32)],
        compiler_params=pltpu.CompilerParams(dimension_semantics=("arbitrary", "arbitrary"), vmem_limit_bytes=VMEM_LIMIT),
    )(*args)


def _s5_wgrad_states(name, xs, d, *, n_tok):
    def body(x_ref, d_ref, o_ref):
        for k in range(4):
            o_ref[k * LANE:(k + 1) * LANE, :] = _tn(x_ref[k], d_ref[...])

    return pl.pallas_call(
        body, grid=(2, S5_Q),
        in_specs=[pl.BlockSpec((None, 4, n_tok, LANE), lambda ri, q: (ri, q, 0, 0)), pl.BlockSpec((n_tok, LANE), lambda ri, q: (0, q))],
        out_specs=pl.BlockSpec((None, None, 512, LANE), lambda ri, q: (ri, q, 0, 0)),
        out_shape=jax.ShapeDtypeStruct((2, S5_Q, 512, LANE), f32), name=name,
        compiler_params=pltpu.CompilerParams(dimension_semantics=("arbitrary", "arbitrary"), vmem_limit_bytes=VMEM_LIMIT),
    )(xs, d)


def _s5_wgrad_channels(name, a, a_blk0, gx, *, n_tok):
    def body(a_ref, g_ref, o_ref):
        for k in range(4):
            o_ref[:, k * LANE:(k + 1) * LANE] = _tn(a_ref[...], g_ref[k])

    return pl.pallas_call(
        body, grid=(2, S5_Q),
        in_specs=[pl.BlockSpec((n_tok, LANE), lambda ri, q: (0, a_blk0 + q)), pl.BlockSpec((None, 4, n_tok, LANE), lambda ri, q: (ri, q, 0, 0))],
        out_specs=pl.BlockSpec((None, None, LANE, 512), lambda ri, q: (ri, q, 0, 0)),
        out_shape=jax.ShapeDtypeStruct((2, S5_Q, LANE, 512), f32), name=name,
        compiler_params=pltpu.CompilerParams(dimension_semantics=("arbitrary", "arbitrary"), vmem_limit_bytes=VMEM_LIMIT),
    )(a, gx)


N_CHIPS = 4
_BIG_SHARD = {"w_in": (1, 1024, 1770, 1792), "mla_w_q_up": (1, 256, 192, 256), "mla_w_kv_up": (1, 128, 256, 256),
              "s5_w_glu": (0, 128, 512, 512), "w_branch_out": (0, 384, 1024, 1024), "w_out": (0, 256, 1024, 1024)}


def _to_shards(name, m):
    axis, r, c, cp = _BIG_SHARD[name]
    if axis == 0:
        return m.reshape(N_CHIPS, r, c)
    return jnp.stack([jnp.pad(m[:, j * c:(j + 1) * c], ((0, 0), (0, cp - c))) for j in range(N_CHIPS)])


def _from_shards(name, s):
    axis, r, c, cp = _BIG_SHARD[name]
    if axis == 0:
        return s.reshape(N_CHIPS * r, c)
    return jnp.concatenate([s[j, :, :c] for j in range(N_CHIPS)], axis=1)


def _pad_w_in(w):
    pieces, pos = [], 0
    for name in _PAD_ORDER:
        start, width, inner = _PAD[name]
        o0, ow = _ORIG[name]
        if start + inner > pos:
            pieces.append(jnp.zeros((w.shape[0], start + inner - pos), w.dtype))
        pieces.append(w[:, o0:o0 + ow])
        pos = start + inner + ow
    pieces.append(jnp.zeros((w.shape[0], NP - pos), w.dtype))
    return jnp.concatenate(pieces, axis=1)


def _layer_weights(l, small, big):
    w = {}
    w["w_in_shards"] = big["w_in"][l]
    w["w_in"] = _pad_w_in(_from_shards("w_in", big["w_in"][l]))
    wq = _from_shards("mla_w_q_up", big["mla_w_q_up"][l]).reshape(MLA_Q_RANK, HEADS, MLA_QK)
    w["wq"] = jnp.pad(wq, ((0, 0), (0, 0), (0, LANE - MLA_QK))).reshape(MLA_Q_RANK, HEADS * LANE)
    wkv = _from_shards("mla_w_kv_up", big["mla_w_kv_up"][l]).reshape(MLA_KV_RANK, HEADS, 128)
    wk = jnp.pad(wkv[:, :, :64], ((0, 0), (0, 0), (0, 64))).reshape(MLA_KV_RANK, HEADS * LANE)
    wv = wkv[:, :, 64:].reshape(MLA_KV_RANK, 512)
    w["wkv"] = jnp.concatenate([wk, wv], axis=1)
    w["w_glu"] = _from_shards("s5_w_glu", big["s5_w_glu"][l])
    w["wo"] = _from_shards("w_branch_out", big["w_branch_out"][l])
    w["w_out"] = _from_shards("w_out", big["w_out"][l])
    row = lambda a: a.reshape(1, -1).astype(f32)
    w["norm_g"] = row(small["norm_g"][l])
    w["qa_g"] = row(small["mla_q_a_norm"][l])
    w["kva_g"] = row(small["mla_kv_a_norm"][l])
    w["qn_g"] = jnp.pad(row(small["mla_q_norm"][l]), ((0, 0), (0, LANE - MLA_QK)))
    w["kn_g"] = jnp.pad(row(small["mla_k_norm"][l]), ((0, 0), (0, LANE - MLA_QK)))
    w["fq_g"] = jnp.tile(row(small["fox_q_norm"][l]), (1, 2))
    w["fk_g"] = jnp.tile(row(small["fox_k_norm"][l]), (1, 2))
    w["b_f"] = jnp.pad(row(small["fox_b_f"][l]), ((0, 0), (0, LANE - HEADS)))
    w["lr"] = small["s5_lambda_re"][l].reshape(S5_LANES, 1)
    w["li"] = small["s5_lambda_im"][l].reshape(S5_LANES, 1)
    w["ldt"] = jnp.repeat(small["s5_log_dt"][l], S5_STATE).reshape(S5_LANES, 1)
    w["b_re"] = small["s5_b_re"][l].reshape(S5_LANES, S5_GROUP)
    w["b_im"] = small["s5_b_im"][l].reshape(S5_LANES, S5_GROUP)
    w["c_re"] = small["s5_c_re"][l]
    w["c_im"] = small["s5_c_im"][l]
    w["s5_d"] = row(small["s5_d"][l])
    w["b_glu"] = row(small["s5_b_glu"][l])
    return w


def _fox_halves(x, lane):
    sq = x * x
    lo = jnp.sum(jnp.where(lane < 64, sq, 0.0), axis=-1, keepdims=True)
    hi = jnp.sum(sq, axis=-1, keepdims=True) - lo
    return jnp.where(lane < 64, lax.rsqrt(lo * (1.0 / 64) + EPS), lax.rsqrt(hi * (1.0 / 64) + EPS))


def _fox_halves_bwd(dy, x, r, g, lane):
    xh = x * r
    dxh = dy * g
    pr = dxh * xh
    lo = jnp.sum(jnp.where(lane < 64, pr, 0.0), axis=-1, keepdims=True)
    hi = jnp.sum(pr, axis=-1, keepdims=True) - lo
    mean = jnp.where(lane < 64, lo, hi) * (1.0 / 64)
    return r * (dxh - xh * mean), jnp.sum(dy * xh, axis=0, keepdims=True)


def _mla_recompute(cq, ckv, kpe, c, s1, s2, qa_g, kva_g, wq, wkv):
    cqn, r_cq = _rms(cq, qa_g, MLA_Q_RANK)
    ckvn, r_ckv = _rms(ckv, kva_g, MLA_KV_RANK)
    cqn_b = cqn.astype(bf16)
    ckvn_b = ckvn.astype(bf16)
    q_raw = _nn(cqn_b, wq)
    kv_raw = _nn(ckvn_b, wkv)
    kpe_rot = _rope(kpe, c, s1, s2)
    return cqn_b, r_cq, ckvn_b, r_ckv, q_raw, kv_raw, kpe_rot


def _layer_fwd(x, w, rope_tabs, n_tok):
    c_tab, s1_tab, s2_tab = rope_tabs
    saved = {"x": x}

    def norm_body(x_ref, g_ref, h_ref):
        h_ref[...] = _rms(x_ref[...], g_ref[...], D_MODEL)[0].astype(bf16)

    (h,) = _rowwise("norm_fwd", norm_body, n_tok, [(x, D_MODEL, 0)], [w["norm_g"]], [(D_MODEL, bf16)], [])
    proj = _mm_nn("in_proj", h, w["w_in"], m=n_tok, n=NP, k=D_MODEL, tm=n_tok, tn=512, tk=D_MODEL)
    saved["h"], saved["proj"] = h, proj

    def mla_prep_body(cq_ref, ckv_ref, kpe_ref, c_ref, s1_ref, s2_ref, qa_ref, kva_ref, wq_ref, wkv_ref, qn_g_ref, kn_g_ref,
                      qn_ref, kn_ref, v_ref):
        c, s1, s2 = c_ref[...], s1_ref[...], s2_ref[...]
        _, _, _, _, q_raw, kv_raw, kpe_rot = _mla_recompute(cq_ref[...], ckv_ref[...], kpe_ref[...], c, s1, s2,
                                                            qa_ref[...], kva_ref[...], wq_ref[...], wkv_ref[...])
        for hd in range(HEADS):
            sl = slice(hd * LANE, (hd + 1) * LANE)
            qn_ref[:, sl] = _rms(_rope(q_raw[:, sl], c, s1, s2), qn_g_ref[...], MLA_QK)[0].astype(bf16)
            kn_ref[:, sl] = _rms(kv_raw[:, sl] + kpe_rot, kn_g_ref[...], MLA_QK)[0].astype(bf16)
        v_ref[...] = kv_raw[:, HEADS * LANE:].astype(bf16)

    qn, kn, v_mla = _rowwise(
        "mla_prep", mla_prep_body, n_tok,
        [(proj, *_seg("cq")), (proj, *_seg("ckv")), (proj, *_seg("kpe")), (c_tab, LANE, 0), (s1_tab, LANE, 0), (s2_tab, LANE, 0)],
        [w["qa_g"], w["kva_g"], w["wq"], w["wkv"], w["qn_g"], w["kn_g"]],
        [(HEADS * LANE, bf16), (HEADS * LANE, bf16), (512, bf16)], [])
    y_mla, lse_mla = _attn_fwd("mla_attn_fwd", qn, kn, v_mla, None, mla=True, n_tok=n_tok)
    saved.update(qn=qn, kn=kn, v_mla=v_mla, y_mla=y_mla, lse_mla=lse_mla)

    def fox_prep_body(fq_ref, fk_ref, fv_ref, ff_ref, qg_ref, kg_ref, bf_ref, fqn_ref, fkn_ref, fvb_ref, logf_ref):
        lane = lax.broadcasted_iota(jnp.int32, (TOK, LANE), 1)
        for blk in range(4):
            sl = slice(blk * LANE, (blk + 1) * LANE)
            xq = fq_ref[:, sl]
            fqn_ref[:, sl] = (xq * _fox_halves(xq, lane) * qg_ref[...]).astype(bf16)
            xk = fk_ref[:, sl]
            fkn_ref[:, sl] = (xk * _fox_halves(xk, lane) * kg_ref[...]).astype(bf16)
        fvb_ref[...] = fv_ref[...].astype(bf16)
        z = ff_ref[...] + bf_ref[...]
        logf_ref[...] = jnp.minimum(z, 0.0) - jnp.log(1.0 + jnp.exp(-jnp.abs(z)))

    fqn, fkn, fvb, logf = _rowwise(
        "fox_prep", fox_prep_body, n_tok,
        [(proj, *_seg("fq")), (proj, *_seg("fk")), (proj, *_seg("fv")), (proj, *_seg("ff"))],
        [w["fq_g"], w["fk_g"], w["b_f"]],
        [(512, bf16), (512, bf16), (512, bf16), (LANE, f32)], [])

    def cum_body(x_ref, cum_ref):
        x = x_ref[...]
        t = lax.broadcasted_iota(jnp.int32, x.shape, 0)
        s = 1
        while s < n_tok:
            x = x + jnp.where(t >= s, pltpu.roll(x, s, 0), 0.0)
            s *= 2
        for hd in range(HEADS):
            cum_ref[hd] = jnp.broadcast_to(x[:, hd:hd + 1], (n_tok, LANE))

    cum_b = pl.pallas_call(cum_body, out_shape=jax.ShapeDtypeStruct((HEADS, n_tok, LANE), f32), name="fox_cum")(logf)
    y_fox, lse_fox = _attn_fwd("fox_attn_fwd", fqn, fkn, fvb, cum_b, mla=False, n_tok=n_tok)
    saved.update(fqn=fqn, fkn=fkn, fvb=fvb, cum_b=cum_b, y_fox=y_fox, lse_fox=lse_fox)

    a_re, a_im, bb_re, bb_im = _s5_param_fwd(w["lr"], w["li"], w["ldt"], w["b_re"], w["b_im"])
    per_group = lambda m: m.reshape(S5_GROUPS, S5_STATE, S5_GROUP)
    b_cn = jnp.stack([_bd8(jnp.swapaxes(per_group(bb_re), 1, 2)), _bd8(jnp.swapaxes(per_group(bb_im), 1, 2))]).astype(bf16)
    b_nc = jnp.stack([_bd8(per_group(bb_re)), _bd8(per_group(bb_im))]).astype(bf16)
    c_nc = jnp.stack([_bd8(jnp.swapaxes(w["c_re"], 1, 2)), -_bd8(jnp.swapaxes(w["c_im"], 1, 2))]).astype(bf16)
    c_cn = jnp.stack([_bd8(w["c_re"]), -_bd8(w["c_im"])]).astype(bf16)
    a_re8 = jnp.broadcast_to(a_re.reshape(S5_BLOCKS, 1, LANE), (S5_BLOCKS, 8, LANE))
    a_im8 = jnp.broadcast_to(a_im.reshape(S5_BLOCKS, 1, LANE), (S5_BLOCKS, 8, LANE))
    u_w, u_blk = _seg("s5u")
    u_streams = _to_streams(proj[:, u_blk * u_w:(u_blk + 1) * u_w])
    bu = _s5_expand("s5_bu", u_streams, 0, b_cn, n_tok=n_tok)
    xs = _s5_scan("s5_scan_fwd", bu, a_re8, a_im8, reverse=False, n_tok=n_tok)
    ylin = _from_streams(_s5_contract("s5_y", xs, c_nc, None, f32, n_tok=n_tok))

    def s5_post_body(yl_ref, u_ref, d_ref, wg_ref, bg_ref, out_ref):
        y = yl_ref[...] + d_ref[...] * u_ref[...]
        z, _ = _gelu(y)
        out_ref[...] = z * _sigmoid(_nn(z, wg_ref[...]) + bg_ref[...])

    (y_s5,) = _rowwise("s5_post", s5_post_body, n_tok, [(ylin, 512, 0), (proj, u_w, u_blk)],
                       [w["s5_d"], w["w_glu"], w["b_glu"]], [(512, f32)], [])
    saved.update(xs=xs, ylin=ylin, y_s5=y_s5, b_nc=b_nc, c_cn=c_cn, a_re8=a_re8, a_im8=a_im8, u_streams=u_streams)

    def merge_body(ym_ref, yf_ref, ys_ref, gm_ref, gf_ref, gs_ref, mm_ref, mf_ref, ms_ref, x_ref, wo_ref, wout_ref, out_ref):
        merged = jnp.zeros((TOK, D_MODEL), f32)
        for b, (y_ref, g_ref, m_ref) in enumerate(((ym_ref, gm_ref, mm_ref), (yf_ref, gf_ref, mf_ref), (ys_ref, gs_ref, ms_ref))):
            g = g_ref[...]
            a = y_ref[...] * (g * _sigmoid(g))
            merged = merged + _sigmoid(m_ref[...]) * _nn(a, wo_ref[b * 512:(b + 1) * 512, :])
        out_ref[...] = x_ref[...] + _nn(merged, wout_ref[...])

    (out,) = _rowwise(
        "merge_fwd", merge_body, n_tok,
        [(y_mla, 512, 0), (y_fox, 512, 0), (y_s5, 512, 0), (proj, *_seg("g_mla")), (proj, *_seg("g_fox")), (proj, *_seg("g_s5")),
         (proj, *_seg("m_mla")), (proj, *_seg("m_fox")), (proj, *_seg("m_s5")), (x, D_MODEL, 0)],
        [w["wo"], w["w_out"]], [(D_MODEL, f32)], [])
    return out, saved


def _layer_bwd(dout, w, sv, rope_tabs, n_tok):
    c_tab, s1_tab, s2_tab = rope_tabs
    proj, x = sv["proj"], sv["x"]
    grads = {}

    def merge_bwd_body(ym_ref, yf_ref, ys_ref, gm_ref, gf_ref, gs_ref, mm_ref, mf_ref, ms_ref, do_ref, wo_ref, wout_ref,
                       dym_ref, dyf_ref, dys_ref, dgm_ref, dgf_ref, dgs_ref, dmm_ref, dmf_ref, dms_ref, dwo_ref, dwout_ref):
        do = do_ref[...]
        branches = ((ym_ref, gm_ref, mm_ref, dym_ref, dgm_ref, dmm_ref), (yf_ref, gf_ref, mf_ref, dyf_ref, dgf_ref, dmf_ref),
                    (ys_ref, gs_ref, ms_ref, dys_ref, dgs_ref, dms_ref))
        acts, outs, sigs = [], [], []
        merged = jnp.zeros((TOK, D_MODEL), f32)
        for b, (y_ref, g_ref, m_ref, _, _, _) in enumerate(branches):
            g = g_ref[...]
            a = (y_ref[...] * (g * _sigmoid(g))).astype(bf16)
            o = _nn(a, wo_ref[b * 512:(b + 1) * 512, :])
            s = _sigmoid(m_ref[...])
            merged = merged + s * o
            acts.append(a)
            outs.append(o)
            sigs.append(s)
        dmerged = _nt(do, wout_ref[...])
        _accumulate(dwout_ref, _tn(merged, do))
        dwo = []
        for b, (y_ref, g_ref, m_ref, dy_ref, dg_ref, dm_ref) in enumerate(branches):
            s, o = sigs[b], outs[b]
            dm_ref[...] = (dmerged * o * s * (1.0 - s)).astype(bf16)
            d_o = dmerged * s
            da = _nt(d_o, wo_ref[b * 512:(b + 1) * 512, :])
            dwo.append(_tn(acts[b], d_o))
            g = g_ref[...]
            sg = _sigmoid(g)
            dy_ref[...] = da * (g * sg)
            dg_ref[...] = (da * y_ref[...] * (sg * (1.0 + g * (1.0 - sg)))).astype(bf16)
        _accumulate(dwo_ref, jnp.concatenate(dwo, axis=0))

    (dy_mla, dy_fox, dy_s5, dg_mla, dg_fox, dg_s5, dm_mla, dm_fox, dm_s5, dwo, dwout) = _rowwise(
        "merge_bwd", merge_bwd_body, n_tok,
        [(sv["y_mla"], 512, 0), (sv["y_fox"], 512, 0), (sv["y_s5"], 512, 0), (proj, *_seg("g_mla")), (proj, *_seg("g_fox")),
         (proj, *_seg("g_s5")), (proj, *_seg("m_mla")), (proj, *_seg("m_fox")), (proj, *_seg("m_s5")), (dout, D_MODEL, 0)],
        [w["wo"], w["w_out"]],
        [(512, f32)] * 3 + [(512, bf16)] * 3 + [(D_MODEL, bf16)] * 3, [((1536, D_MODEL), f32), ((D_MODEL, D_MODEL), f32)])
    grads["w_branch_out"], grads["w_out"] = dwo, dwout

    u_w, u_blk = _seg("s5u")

    def s5_post_bwd_body(yl_ref, u_ref, do_ref, d_ref, wg_ref, bg_ref, dyl_ref, dus_ref, dd_ref, dwg_ref, dbg_ref):
        u = u_ref[...]
        y = yl_ref[...] + d_ref[...] * u
        z, t = _gelu(y)
        s = _sigmoid(_nn(z, wg_ref[...]) + bg_ref[...])
        do = do_ref[...]
        dgl = do * z * s * (1.0 - s)
        dz = do * s + _nt(dgl, wg_ref[...])
        dy = dz * _gelu_grad(y, t)
        dyl_ref[...] = dy.astype(bf16)
        dus_ref[...] = dy * d_ref[...]
        _accumulate(dd_ref, jnp.sum(dy * u, axis=0, keepdims=True))
        _accumulate(dwg_ref, _tn(z, dgl))
        _accumulate(dbg_ref, jnp.sum(dgl, axis=0, keepdims=True))

    dylin, du_skip, dd, dwglu, dbglu = _rowwise(
        "s5_post_bwd", s5_post_bwd_body, n_tok, [(sv["ylin"], 512, 0), (proj, u_w, u_blk), (dy_s5, 512, 0)],
        [w["s5_d"], w["w_glu"], w["b_glu"]], [(512, bf16), (512, f32)], [((1, 512), f32), ((512, 512), f32), ((1, 512), f32)])
    grads["s5_d"], grads["s5_w_glu"], grads["s5_b_glu"] = dd.reshape(512), dwglu, dbglu.reshape(512)

    dylin = _to_streams(dylin)
    dxs = _s5_expand("s5_dxs", dylin, 0, sv["c_cn"], n_tok=n_tok)
    dc_nc = _s5_wgrad_states("s5_dc", sv["xs"], dylin, n_tok=n_tok)
    gx = _s5_scan("s5_scan_bwd", dxs, sv["a_re8"], -sv["a_im8"], reverse=True, n_tok=n_tok)
    da = _s5_da(sv["xs"], gx, n_tok=n_tok)
    ds5u = _from_streams(_s5_contract("s5_du", gx, sv["b_nc"], _to_streams(du_skip), bf16, n_tok=n_tok))
    db_cn = _s5_wgrad_channels("s5_db", sv["u_streams"], 0, gx, n_tok=n_tok)
    diag_b = lambda m: jnp.swapaxes(_bd8_diag(m, S5_GROUP, S5_STATE), 1, 2).reshape(S5_LANES, S5_GROUP)
    diag_c = lambda m: jnp.swapaxes(_bd8_diag(m, S5_STATE, S5_GROUP), 1, 2)
    dlr, dli, dldt, db_re, db_im = _s5_param_bwd(
        w["lr"], w["li"], w["ldt"], w["b_re"], w["b_im"], da[:, 0, :].reshape(S5_LANES, 1), da[:, 1, :].reshape(S5_LANES, 1),
        diag_b(db_cn[0]), diag_b(db_cn[1]))
    grads["s5_lambda_re"] = dlr.reshape(S5_GROUPS, S5_STATE)
    grads["s5_lambda_im"] = dli.reshape(S5_GROUPS, S5_STATE)
    grads["s5_log_dt"] = dldt.reshape(S5_GROUPS)
    grads["s5_b_re"] = db_re.reshape(S5_GROUPS, S5_STATE, S5_GROUP)
    grads["s5_b_im"] = db_im.reshape(S5_GROUPS, S5_STATE, S5_GROUP)
    grads["s5_c_re"] = diag_c(dc_nc[0])
    grads["s5_c_im"] = -diag_c(dc_nc[1])

    dfqn, dfkn, dfv, dck, dcq = _attn_bwd("fox_attn_bwd", sv["fqn"], sv["fkn"], sv["fvb"], sv["y_fox"], sv["lse_fox"], dy_fox,
                                          sv["cum_b"], mla=False, n_tok=n_tok)
    dcq = jnp.pad(dcq[:, :2, :].reshape(HEADS, n_tok).T, ((0, 0), (0, LANE - HEADS)))

    def fox_gate_bwd_body(dk_ref, dq_ref, ff_ref, bf_ref, dff_ref, dbf_ref):
        xg = dk_ref[...] + dq_ref[...]
        t = lax.broadcasted_iota(jnp.int32, xg.shape, 0)
        s = 1
        while s < n_tok:
            xg = xg + jnp.where(t < n_tok - s, pltpu.roll(xg, n_tok - s, 0), 0.0)
            s *= 2
        dff = xg * _sigmoid(-(ff_ref[...] + bf_ref[...]))
        dff_ref[...] = dff.astype(bf16)
        dbf_ref[...] = jnp.sum(dff, axis=0, keepdims=True)

    ff_w, ff_blk = _seg("ff")
    dff, dbf = pl.pallas_call(
        fox_gate_bwd_body, grid=(1,),
        in_specs=[pl.BlockSpec((n_tok, LANE), lambda i: (0, 0)), pl.BlockSpec((n_tok, LANE), lambda i: (0, 0)),
                  pl.BlockSpec((n_tok, ff_w), lambda i: (0, ff_blk)), pl.BlockSpec((1, LANE), lambda i: (0, 0))],
        out_specs=[pl.BlockSpec((n_tok, LANE), lambda i: (0, 0)), pl.BlockSpec((1, LANE), lambda i: (0, 0))],
        out_shape=[jax.ShapeDtypeStruct((n_tok, LANE), bf16), jax.ShapeDtypeStruct((1, LANE), f32)], name="fox_gate_bwd",
    )(dck, dcq, proj, w["b_f"])
    grads["fox_b_f"] = dbf[0, :HEADS]

    def fox_prep_bwd_body(fq_ref, fk_ref, dqn_ref, dkn_ref, dv_ref, qg_ref, kg_ref, dfq_ref, dfk_ref, dfv_ref, dqg_ref, dkg_ref):
        lane = lax.broadcasted_iota(jnp.int32, (TOK, LANE), 1)
        dqg = jnp.zeros((1, LANE), f32)
        dkg = jnp.zeros((1, LANE), f32)
        for blk in range(4):
            sl = slice(blk * LANE, (blk + 1) * LANE)
            xq = fq_ref[:, sl]
            dx, dg = _fox_halves_bwd(dqn_ref[:, sl], xq, _fox_halves(xq, lane), qg_ref[...], lane)
            dfq_ref[:, sl] = dx.astype(bf16)
            dqg = dqg + dg
            xk = fk_ref[:, sl]
            dx, dg = _fox_halves_bwd(dkn_ref[:, sl], xk, _fox_halves(xk, lane), kg_ref[...], lane)
            dfk_ref[:, sl] = dx.astype(bf16)
            dkg = dkg + dg
        dfv_ref[...] = dv_ref[...].astype(bf16)
        _accumulate(dqg_ref, dqg + pltpu.roll(dqg, 64, 1))
        _accumulate(dkg_ref, dkg + pltpu.roll(dkg, 64, 1))

    dfq, dfk, dfvb, dfqg, dfkg = _rowwise(
        "fox_prep_bwd", fox_prep_bwd_body, n_tok,
        [(proj, *_seg("fq")), (proj, *_seg("fk")), (dfqn, 512, 0), (dfkn, 512, 0), (dfv, 512, 0)],
        [w["fq_g"], w["fk_g"]], [(512, bf16)] * 3, [((1, LANE), f32)] * 2)
    grads["fox_q_norm"], grads["fox_k_norm"] = dfqg[0, :FOX_DIM], dfkg[0, :FOX_DIM]

    dqn, dkn, dv_mla = _attn_bwd("mla_attn_bwd", sv["qn"], sv["kn"], sv["v_mla"], sv["y_mla"], sv["lse_mla"], dy_mla,
                                 None, mla=True, n_tok=n_tok)

    def mla_prep_bwd_body(cq_ref, ckv_ref, kpe_ref, c_ref, s1_ref, s2_ref, dqn_ref, dkn_ref, dv_ref,
                          qa_ref, kva_ref, wq_ref, wkv_ref, qn_g_ref, kn_g_ref,
                          dcq_ref, dckv_ref, dkpe_ref, dwq_ref, dwkv_ref, dqa_ref, dkva_ref, dqng_ref, dkng_ref):
        c, s1, s2 = c_ref[...], s1_ref[...], s2_ref[...]
        cq, ckv = cq_ref[...], ckv_ref[...]
        cqn_b, r_cq, ckvn_b, r_ckv, q_raw, kv_raw, kpe_rot = _mla_recompute(
            cq, ckv, kpe_ref[...], c, s1, s2, qa_ref[...], kva_ref[...], wq_ref[...], wkv_ref[...])
        lane = lax.broadcasted_iota(jnp.int32, (TOK, LANE), 1)
        dq_raw, dk_raw = [], []
        dkpe_rot = jnp.zeros((TOK, LANE), f32)
        dqng = jnp.zeros((1, LANE), f32)
        dkng = jnp.zeros((1, LANE), f32)
        for hd in range(HEADS):
            sl = slice(hd * LANE, (hd + 1) * LANE)
            q_rot = _rope(q_raw[:, sl], c, s1, s2)
            r = lax.rsqrt(jnp.sum(q_rot * q_rot, axis=-1, keepdims=True) * (1.0 / MLA_QK) + EPS)
            dx, dg = _rms_bwd(dqn_ref[:, sl], q_rot, r, qn_g_ref[...], MLA_QK)
            dqng = dqng + dg
            dq_raw.append(_rope_t(dx, c, s1, s2))
            k_full = kv_raw[:, sl] + kpe_rot
            r = lax.rsqrt(jnp.sum(k_full * k_full, axis=-1, keepdims=True) * (1.0 / MLA_QK) + EPS)
            dx, dg = _rms_bwd(dkn_ref[:, sl], k_full, r, kn_g_ref[...], MLA_QK)
            dkng = dkng + dg
            dk_raw.append(jnp.where(lane < 64, dx, 0.0))
            dkpe_rot = dkpe_rot + dx
        dkpe = _rope_t(dkpe_rot, c, s1, s2)
        dkpe_ref[...] = jnp.where(jnp.logical_and(lane >= 64, lane < 64 + ROPE), dkpe, 0.0).astype(bf16)
        dq_raw = jnp.concatenate(dq_raw, axis=1).astype(bf16)
        dkv_raw = jnp.concatenate(dk_raw + [dv_ref[...]], axis=1).astype(bf16)
        dcqn = _nt(dq_raw, wq_ref[...])
        dckvn = _nt(dkv_raw, wkv_ref[...])
        dx, dg = _rms_bwd(dcqn, cq, r_cq, qa_ref[...], MLA_Q_RANK)
        dcq_ref[...] = dx.astype(bf16)
        _accumulate(dqa_ref, dg)
        dx, dg = _rms_bwd(dckvn, ckv, r_ckv, kva_ref[...], MLA_KV_RANK)
        dckv_ref[...] = dx.astype(bf16)
        _accumulate(dkva_ref, dg)
        _accumulate(dwq_ref, _tn(cqn_b, dq_raw))
        _accumulate(dwkv_ref, _tn(ckvn_b, dkv_raw))
        _accumulate(dqng_ref, dqng)
        _accumulate(dkng_ref, dkng)

    dcq, dckv, dkpe, dwq, dwkv, dqa, dkva, dqng, dkng = _rowwise(
        "mla_prep_bwd", mla_prep_bwd_body, n_tok,
        [(proj, *_seg("cq")), (proj, *_seg("ckv")), (proj, *_seg("kpe")), (c_tab, LANE, 0), (s1_tab, LANE, 0), (s2_tab, LANE, 0),
         (dqn, HEADS * LANE, 0), (dkn, HEADS * LANE, 0), (dv_mla, 512, 0)],
        [w["qa_g"], w["kva_g"], w["wq"], w["wkv"], w["qn_g"], w["kn_g"]],
        [(MLA_Q_RANK, bf16), (LANE, bf16), (LANE, bf16)],
        [((MLA_Q_RANK, HEADS * LANE), f32), ((MLA_KV_RANK, HEADS * LANE + 512), f32), ((1, MLA_Q_RANK), f32),
         ((1, MLA_KV_RANK), f32), ((1, LANE), f32), ((1, LANE), f32)])
    grads["mla_w_q_up"] = dwq.reshape(MLA_Q_RANK, HEADS, LANE)[:, :, :MLA_QK].reshape(MLA_Q_RANK, HEADS * MLA_QK)
    dwk = dwkv[:, :HEADS * LANE].reshape(MLA_KV_RANK, HEADS, LANE)[:, :, :64]
    dwv = dwkv[:, HEADS * LANE:].reshape(MLA_KV_RANK, HEADS, 64)
    grads["mla_w_kv_up"] = jnp.concatenate([dwk, dwv], axis=2).reshape(MLA_KV_RANK, HEADS * 128)
    grads["mla_q_a_norm"], grads["mla_kv_a_norm"] = dqa.reshape(-1), dkva.reshape(-1)
    grads["mla_q_norm"], grads["mla_k_norm"] = dqng[0, :MLA_QK], dkng[0, :MLA_QK]

    _, _, shard_c, shard_cp = _BIG_SHARD["w_in"]
    kpe0 = _PAD["kpe"][2]
    dproj = jnp.concatenate([dcq, dckv, dkpe[:, kpe0:kpe0 + ROPE], dfq, dfk, dfvb, dff[:, :HEADS], ds5u, dg_mla, dg_fox, dg_s5,
                             dm_mla, dm_fox, dm_s5], axis=1)
    gap = jnp.zeros((n_tok, shard_cp - shard_c), bf16)
    dproj = jnp.concatenate([p for j in range(N_CHIPS) for p in (dproj[:, j * shard_c:(j + 1) * shard_c], gap)], axis=1)
    ct = 256
    per = shard_cp // ct
    dh = _mm("in_proj_dgrad", dproj, w["w_in_shards"], mode="nt", grid=(1, 1, N_CHIPS * per),
             a_spec=pl.BlockSpec((n_tok, ct), lambda i, j, kk: (0, kk)),
             b_spec=pl.BlockSpec((None, D_MODEL, ct), lambda i, j, kk: (kk // per, 0, kk % per)),
             o_spec=pl.BlockSpec((n_tok, D_MODEL), lambda i, j, kk: (0, 0)),
             out_shape=jax.ShapeDtypeStruct((n_tok, D_MODEL), f32), acc_shape=(n_tok, D_MODEL))
    grads["w_in"] = _mm("in_proj_wgrad", sv["h"], dproj, mode="tn", grid=(1, N_CHIPS * per, 1),
                        a_spec=pl.BlockSpec((n_tok, D_MODEL), lambda i, j, kk: (0, 0)),
                        b_spec=pl.BlockSpec((n_tok, ct), lambda i, j, kk: (0, j)),
                        o_spec=pl.BlockSpec((None, D_MODEL, ct), lambda i, j, kk: (j // per, 0, j % per)),
                        out_shape=jax.ShapeDtypeStruct((N_CHIPS, D_MODEL, shard_cp), f32), acc_shape=(D_MODEL, ct))

    def norm_bwd_body(dh_ref, x_ref, do_ref, g_ref, dx_ref, dg_ref):
        xv = x_ref[...]
        r = lax.rsqrt(jnp.sum(xv * xv, axis=-1, keepdims=True) * (1.0 / D_MODEL) + EPS)
        dx, dg = _rms_bwd(dh_ref[...], xv, r, g_ref[...], D_MODEL)
        dx_ref[...] = do_ref[...] + dx
        _accumulate(dg_ref, dg)

    dx, dng = _rowwise("norm_bwd", norm_bwd_body, n_tok, [(dh, D_MODEL, 0), (x, D_MODEL, 0), (dout, D_MODEL, 0)],
                       [w["norm_g"]], [(D_MODEL, f32)], [((1, D_MODEL), f32)])
    grads["norm_g"] = dng.reshape(D_MODEL)
    return dx, grads


def _rope_tables(positions):
    inv = 1.0 / (ROPE_THETA ** (jnp.arange(0, ROPE, 2, dtype=f32) / ROPE))
    ang = positions.astype(f32).reshape(-1, 1) * inv
    cos, sin = jnp.cos(ang), jnp.sin(ang)
    n = ang.shape[0]
    z16, z32, z64 = jnp.zeros((n, 16), f32), jnp.zeros((n, 32), f32), jnp.zeros((n, 64), f32)
    c = jnp.concatenate([jnp.ones((n, 64), f32), cos, cos, z32], axis=1)
    s1 = jnp.concatenate([z64, -sin, z16, z32], axis=1)
    s2 = jnp.concatenate([z64, z16, sin, z32], axis=1)
    return c, s1, s2


BIG = ("w_in", "mla_w_q_up", "mla_w_kv_up", "s5_w_glu", "w_branch_out", "w_out")
SMALL = ("norm_g", "mla_q_a_norm", "mla_kv_a_norm", "mla_q_norm", "mla_k_norm", "fox_b_f", "fox_q_norm", "fox_k_norm",
         "s5_lambda_re", "s5_lambda_im", "s5_log_dt", "s5_b_re", "s5_b_im", "s5_c_re", "s5_c_im", "s5_d", "s5_b_glu")
WEIGHTS = ("norm_g", "w_in", "mla_q_a_norm", "mla_w_q_up", "mla_kv_a_norm", "mla_w_kv_up", "mla_q_norm", "mla_k_norm",
           "fox_b_f", "fox_q_norm", "fox_k_norm", "s5_lambda_re", "s5_lambda_im", "s5_log_dt", "s5_b_re", "s5_b_im",
           "s5_c_re", "s5_c_im", "s5_d", "s5_w_glu", "s5_b_glu", "w_branch_out", "w_out")


def _local_step(x, positions, loss_target, small, big):
    n_tok = x.shape[0]
    tabs = _rope_tables(positions)
    ws, saves = [], []
    hcur = x
    for l in range(DEPTH):
        w = _layer_weights(l, small, big)
        hcur, sv = _layer_fwd(hcur, w, tabs, n_tok)
        ws.append(w)
        saves.append(sv)

    def loss_body(y_ref, t_ref, d_ref, l_ref):
        err = y_ref[...] - t_ref[...]
        d_ref[...] = err * (1.0 / D_MODEL)
        tot = jnp.sum(jnp.sum(err * err, axis=-1, keepdims=True), axis=0, keepdims=True)
        _accumulate(l_ref, jnp.broadcast_to(tot * (0.5 / D_MODEL), (1, LANE)))

    dcur, loss = _rowwise("loss", loss_body, n_tok, [(hcur, D_MODEL, 0), (loss_target, D_MODEL, 0)], [], [(D_MODEL, f32)],
                          [((1, LANE), f32)])
    layer_grads = [None] * DEPTH
    for l in reversed(range(DEPTH)):
        dcur, layer_grads[l] = _layer_bwd(dcur, ws[l], saves[l], tabs, n_tok)
    grads = {n: jnp.stack([layer_grads[l][n] for l in range(DEPTH)]) for n in WEIGHTS}
    return loss[0, 0], dcur, grads


N_DEV = 8
_ANY = pl.BlockSpec(memory_space=pl.ANY)
_MESH = pl.DeviceIdType.MESH


def _all_gather8(name, blk):
    m = blk.shape[0]

    def body(x_ref, out_ref, send_sems, recv_sems, local_sem):
        x, y, c = lax.axis_index("x"), lax.axis_index("y"), lax.axis_index("c")
        me, sibling = (x, y, c), (x, y, 1 - c)
        chips = [(1 - x, y), (x, 1 - y), (1 - x, 1 - y)]

        def slot(px, py, pc):
            return out_ref.at[4 * px + 2 * py + pc]

        def copy(k, block, to, src=None):
            return pltpu.make_async_remote_copy(
                src_ref=slot(*block) if src is None else src, dst_ref=slot(*block),
                send_sem=send_sems.at[k], recv_sem=recv_sems.at[k], device_id=to, device_id_type=_MESH)

        mine = pltpu.make_async_copy(x_ref, slot(*me), local_sem)
        mine.start()
        first = [copy(0, me, sibling, src=x_ref)]
        first += [copy(1 + j, me, (*chip, c), src=x_ref) for j, chip in enumerate(chips)]
        for cp in first:
            cp.start()
        passed = [copy(4 + j, (*chip, c), sibling) for j, chip in enumerate(chips)]
        for j, chip in enumerate(chips):
            copy(1 + j, (*chip, c), me).wait_recv()
            passed[j].start()
        copy(0, sibling, me).wait_recv()
        for j, chip in enumerate(chips):
            copy(4 + j, (*chip, 1 - c), me).wait_recv()
        for cp in first + passed:
            cp.wait_send()
        mine.wait()

    return pl.pallas_call(
        body, out_shape=jax.ShapeDtypeStruct((N_DEV, m, LANE), blk.dtype), in_specs=[_ANY], out_specs=_ANY, name=name,
        scratch_shapes=[pltpu.SemaphoreType.DMA((7,)), pltpu.SemaphoreType.DMA((7,)), pltpu.SemaphoreType.DMA],
    )(blk)


def _gather_layers(name, shards):
    n = len(shards)

    def body(*refs):
        x_refs, out_refs = refs[:n], refs[n:2 * n]
        send_sems, recv_sems, local_sems = refs[2 * n:]
        x, y, c = lax.axis_index("x"), lax.axis_index("y"), lax.axis_index("c")
        me, sibling = (x, y, c), (x, y, 1 - c)
        chips = [(1 - x, y), (x, 1 - y), (1 - x, 1 - y)]

        def copy(w, k, block, to, src=None):
            px, py, pc = block
            slot = out_refs[w].at[pc, 2 * px + py]
            return pltpu.make_async_remote_copy(
                src_ref=slot if src is None else src, dst_ref=slot, send_sem=send_sems.at[7 * w + k],
                recv_sem=recv_sems.at[7 * w + k], device_id=to, device_id_type=_MESH)

        started, local = [], []
        for w in range(n):
            src = x_refs[w].at[c]
            mine = pltpu.make_async_copy(src, out_refs[w].at[c, 2 * x + y], local_sems.at[w])
            mine.start()
            local.append(mine)
            first = [copy(w, 0, me, sibling, src=src)] + [copy(w, 1 + j, me, (*chip, c), src=src) for j, chip in enumerate(chips)]
            for cp in first:
                cp.start()
            started += first
        for w in range(n):
            for j, chip in enumerate(chips):
                copy(w, 1 + j, (*chip, c), me).wait_recv()
                onward = copy(w, 4 + j, (*chip, c), sibling)
                onward.start()
                started.append(onward)
        for w in range(n):
            copy(w, 0, sibling, me).wait_recv()
            for j, chip in enumerate(chips):
                copy(w, 4 + j, (*chip, 1 - c), me).wait_recv()
        for cp in started:
            cp.wait_send()
        for cp in local:
            cp.wait()

    return pl.pallas_call(
        body, out_shape=[jax.ShapeDtypeStruct((2, N_CHIPS) + s.shape[1:], s.dtype) for s in shards],
        in_specs=[_ANY] * n, out_specs=[_ANY] * n, name=name,
        scratch_shapes=[pltpu.SemaphoreType.DMA((7 * n,)), pltpu.SemaphoreType.DMA((7 * n,)), pltpu.SemaphoreType.DMA((n,))],
    )(*shards)


def _swap_layers(name, parts):
    n = len(parts)

    def body(*refs):
        p_refs, got_refs = refs[:n], refs[n:2 * n]
        send_sems, recv_sems = refs[2 * n:]
        x, y, c = lax.axis_index("x"), lax.axis_index("y"), lax.axis_index("c")
        copies = []
        for w in range(n):
            cp = pltpu.make_async_remote_copy(
                src_ref=p_refs[w].at[1 - c], dst_ref=got_refs[w], send_sem=send_sems.at[w], recv_sem=recv_sems.at[w],
                device_id=(x, y, 1 - c), device_id_type=_MESH)
            cp.start()
            copies.append(cp)
        for cp in copies:
            cp.wait()

    return pl.pallas_call(
        body, out_shape=[jax.ShapeDtypeStruct(p.shape[1:], p.dtype) for p in parts], in_specs=[_ANY] * n, out_specs=[_ANY] * n,
        name=name, scratch_shapes=[pltpu.SemaphoreType.DMA((n,)), pltpu.SemaphoreType.DMA((n,))],
    )(*parts)


def _scatter_to_chips(name, parts):
    n = len(parts)

    def body(*refs):
        p_refs, out_refs = refs[:n], refs[n:2 * n]
        send_sems, recv_sems, local_sems = refs[2 * n:]
        x, y, c = lax.axis_index("x"), lax.axis_index("y"), lax.axis_index("c")
        jme = 2 * x + y
        chips = [(1 - x, y), (x, 1 - y), (1 - x, 1 - y)]
        sends, local = [], []
        for w in range(n):
            mine = pltpu.make_async_copy(p_refs[w].at[jme], out_refs[w].at[jme], local_sems.at[w])
            mine.start()
            local.append(mine)
            for k, (tx, ty) in enumerate(chips):
                cp = pltpu.make_async_remote_copy(
                    src_ref=p_refs[w].at[2 * tx + ty], dst_ref=out_refs[w].at[jme], send_sem=send_sems.at[3 * w + k],
                    recv_sem=recv_sems.at[3 * w + k], device_id=(tx, ty, c), device_id_type=_MESH)
                cp.start()
                sends.append(cp)
        for w in range(n):
            for k, (tx, ty) in enumerate(chips):
                pltpu.make_async_remote_copy(
                    src_ref=p_refs[w].at[jme], dst_ref=out_refs[w].at[2 * tx + ty], send_sem=send_sems.at[3 * w + k],
                    recv_sem=recv_sems.at[3 * w + k], device_id=(tx, ty, c), device_id_type=_MESH).wait_recv()
        for cp in sends:
            cp.wait_send()
        for cp in local:
            cp.wait()

    return pl.pallas_call(
        body, out_shape=[jax.ShapeDtypeStruct(p.shape, p.dtype) for p in parts], in_specs=[_ANY] * n, out_specs=[_ANY] * n, name=name,
        scratch_shapes=[pltpu.SemaphoreType.DMA((3 * n,)), pltpu.SemaphoreType.DMA((3 * n,)), pltpu.SemaphoreType.DMA((n,))],
    )(*parts)


def _share_layers(name, bufs):
    n = len(bufs)

    def body(*refs):
        out_refs = refs[n:2 * n]
        send_sems, recv_sems = refs[2 * n:]
        x, y, c = lax.axis_index("x"), lax.axis_index("y"), lax.axis_index("c")
        copies = []
        for w in range(n):
            cp = pltpu.make_async_remote_copy(src_ref=out_refs[w].at[c], dst_ref=out_refs[w].at[c], send_sem=send_sems.at[w],
                                              recv_sem=recv_sems.at[w], device_id=(x, y, 1 - c), device_id_type=_MESH)
            cp.start()
            copies.append(cp)
        for w in range(n):
            pltpu.make_async_remote_copy(src_ref=out_refs[w].at[c], dst_ref=out_refs[w].at[1 - c], send_sem=send_sems.at[w],
                                         recv_sem=recv_sems.at[w], device_id=(x, y, 1 - c), device_id_type=_MESH).wait_recv()
        for cp in copies:
            cp.wait_send()

    return pl.pallas_call(
        body, out_shape=[jax.ShapeDtypeStruct(b.shape, b.dtype) for b in bufs], in_specs=[_ANY] * n, out_specs=[_ANY] * n,
        input_output_aliases={w: w for w in range(n)}, name=name,
        scratch_shapes=[pltpu.SemaphoreType.DMA((n,)), pltpu.SemaphoreType.DMA((n,))],
    )(*bufs)


def _row_tile(rows, cols):
    best = 16
    for t in range(16, rows + 1, 16):
        if rows % t == 0 and t * cols * 4 <= 2 * 1024 * 1024:
            best = t
    return best


def _add_pair(name, core, parts, got, out_dtype):
    _, _, r, c = parts.shape
    t = _row_tile(r, c)

    def body(core_ref, a_ref, b_ref, o_ref):
        o_ref[...] = (a_ref[...] + b_ref[...]).astype(o_ref.dtype)

    spec = pl.BlockSpec((None, t, c), lambda j, i, core_ref: (j, i, 0))
    grid_spec = pltpu.PrefetchScalarGridSpec(
        num_scalar_prefetch=1, grid=(N_CHIPS, r // t),
        in_specs=[pl.BlockSpec((None, None, t, c), lambda j, i, core_ref: (core_ref[0], j, i, 0)), spec], out_specs=spec)
    return pl.pallas_call(body, grid_spec=grid_spec, out_shape=jax.ShapeDtypeStruct(got.shape, out_dtype), name=name,
                          compiler_params=pltpu.CompilerParams(dimension_semantics=("arbitrary", "arbitrary")))(core, parts, got)


def _add_four(name, core, a):
    _, r, c = a.shape
    t = _row_tile(r, c)

    def body(core_ref, a0, a1, a2, a3, o_ref):
        o_ref[...] = ((a0[...].astype(f32) + a1[...].astype(f32)) + a2[...].astype(f32)) + a3[...].astype(f32)

    specs = [pl.BlockSpec((None, t, c), functools.partial(lambda i, core_ref, k: (k, i, 0), k=k)) for k in range(N_CHIPS)]
    grid_spec = pltpu.PrefetchScalarGridSpec(
        num_scalar_prefetch=1, grid=(r // t,), in_specs=specs,
        out_specs=pl.BlockSpec((None, t, c), lambda i, core_ref: (core_ref[0], i, 0)))
    return pl.pallas_call(body, grid_spec=grid_spec, out_shape=jax.ShapeDtypeStruct((2, r, c), f32), name=name,
                          compiler_params=pltpu.CompilerParams(dimension_semantics=("arbitrary",)))(core, a, a, a, a)


def _adamw(name, w, g, m, v, tile):
    r, c = w.shape
    c1 = 1.0 - ADAM_B1 ** ADAM_STEP
    c2 = 1.0 - ADAM_B2 ** ADAM_STEP

    def body(w_ref, g_ref, m_ref, v_ref, d_ref, nm_ref, nv_ref):
        gv = g_ref[...]
        nm = ADAM_B1 * m_ref[...] + (1.0 - ADAM_B1) * gv
        nv = ADAM_B2 * v_ref[...] + (1.0 - ADAM_B2) * (gv * gv)
        m_hat = nm / c1
        v_hat = nv / c2
        d_ref[...] = -ADAM_LR * (m_hat / (jnp.sqrt(v_hat) + ADAM_EPS) + ADAM_WD * w_ref[...])
        nm_ref[...] = nm
        nv_ref[...] = nv

    spec = pl.BlockSpec((tile, c), lambda i: (i, 0))
    sds = jax.ShapeDtypeStruct((r, c), f32)
    return pl.pallas_call(body, grid=(r // tile,), in_specs=[spec] * 4, out_specs=[spec] * 3, out_shape=[sds] * 3, name=name,
                          compiler_params=pltpu.CompilerParams(dimension_semantics=("arbitrary",), vmem_limit_bytes=VMEM_LIMIT),
                          )(w, g, m, v)


def _pad_rows(flat, rows):
    return jnp.pad(flat, (0, rows * LANE - flat.shape[0])).reshape(rows, LANE)


def kernel(x, positions, norm_g, w_in, mla_q_a_norm, mla_w_q_up, mla_kv_a_norm, mla_w_kv_up, mla_q_norm, mla_k_norm, fox_b_f, fox_q_norm, fox_k_norm, s5_lambda_re, s5_lambda_im, s5_log_dt, s5_b_re, s5_b_im, s5_c_re, s5_c_im, s5_d, s5_w_glu, s5_b_glu, w_branch_out, w_out, loss_target, m_norm_g, m_w_in, m_mla_q_a_norm, m_mla_w_q_up, m_mla_kv_a_norm, m_mla_w_kv_up, m_mla_q_norm, m_mla_k_norm, m_fox_b_f, m_fox_q_norm, m_fox_k_norm, m_s5_lambda_re, m_s5_lambda_im, m_s5_log_dt, m_s5_b_re, m_s5_b_im, m_s5_c_re, m_s5_c_im, m_s5_d, m_s5_w_glu, m_s5_b_glu, m_w_branch_out, m_w_out, v_norm_g, v_w_in, v_mla_q_a_norm, v_mla_w_q_up, v_mla_kv_a_norm, v_mla_w_kv_up, v_mla_q_norm, v_mla_k_norm, v_fox_b_f, v_fox_q_norm, v_fox_k_norm, v_s5_lambda_re, v_s5_lambda_im, v_s5_log_dt, v_s5_b_re, v_s5_b_im, v_s5_c_re, v_s5_c_im, v_s5_d, v_s5_w_glu, v_s5_b_glu, v_w_branch_out, v_w_out):
    given = dict(locals())
    wts = {n: given[n] for n in WEIGHTS}
    mom1 = {n: given["m_" + n] for n in WEIGHTS}
    mom2 = {n: given["v_" + n] for n in WEIGHTS}

    def lanes(n, a):
        _, _, c, cp = _BIG_SHARD[n]
        return jnp.pad(a, ((0, 0), (0, 0), (0, cp - c)))

    gathered = _gather_layers("gather_weights", [lanes(n, wts[n].astype(bf16)) for n in BIG])
    big = dict(zip(BIG, gathered))
    small = {n: wts[n] for n in SMALL}

    loss_local, grad_x, grads = _local_step(x[0], positions, loss_target[0], small, big)
    loss = lax.psum(loss_local, ("x", "y", "c"))

    small_flat = jnp.concatenate([grads[n].reshape(-1) for n in SMALL])
    small_rows = -(-small_flat.shape[0] // (N_DEV * 16 * LANE)) * 16
    parts = [grads[n] if n == "w_in" else jnp.stack([_to_shards(n, grads[n][l]) for l in range(DEPTH)]) for n in BIG]
    parts.append(jnp.swapaxes(_pad_rows(small_flat, N_DEV * small_rows).reshape(N_CHIPS, 2, small_rows, LANE), 0, 1))
    core = lax.axis_index("c")
    core1 = core.reshape(1).astype(jnp.int32)
    got = _swap_layers("grads_to_sibling", parts)
    hop = [bf16] * len(BIG) + [f32]
    pair = [_add_pair("grads_pair_sum_%d" % i, core1, a, b, dt) for i, (a, b, dt) in enumerate(zip(parts, got, hop))]
    landed = _scatter_to_chips("grads_to_chips", pair)
    total = [_add_four("grads_chip_sum_%d" % i, core1, a) for i, a in enumerate(landed)]
    shared = _share_layers("grads_share", total[:-1])
    small_mine = lax.dynamic_index_in_dim(total[-1], core, 0, keepdims=False)
    small_all = _all_gather8("gather_small_grads", small_mine).reshape(-1)

    g_out = {n: s[:, :, :_BIG_SHARD[n][2]] for n, s in zip(BIG, shared)}
    pos = 0
    for n in SMALL:
        g_out[n] = small_all[pos:pos + wts[n].size].reshape(wts[n].shape)
        pos += wts[n].size

    delta, new_m, new_v = {}, {}, {}
    for n in BIG:
        shp = wts[n].shape
        as2d = lambda a: a.reshape(-1, shp[-1])
        d_, m_, v_ = _adamw("adamw_" + n, as2d(wts[n]), as2d(g_out[n]), as2d(mom1[n]), as2d(mom2[n]), tile=256)
        delta[n], new_m[n], new_v[n] = d_.reshape(shp), m_.reshape(shp), v_.reshape(shp)
    n_small_rows = N_DEV * small_rows
    pack = lambda d: _pad_rows(jnp.concatenate([d[n].reshape(-1) for n in SMALL]), n_small_rows)
    packed = _adamw("adamw_small", pack(wts), small_all.reshape(n_small_rows, LANE), pack(mom1), pack(mom2), tile=n_small_rows)
    for out, res in zip((delta, new_m, new_v), packed):
        res = res.reshape(-1)
        pos = 0
        for n in SMALL:
            out[n] = res[pos:pos + wts[n].size].reshape(wts[n].shape)
            pos += wts[n].size

    return (loss, grad_x[None], *[g_out[n] for n in WEIGHTS], *[delta[n] for n in WEIGHTS],
            *[new_m[n] for n in WEIGHTS], *[new_v[n] for n in WEIGHTS])
```

```python
import functools
import math

import jax
import jax.numpy as jnp
from jax import lax
from jax.experimental import pallas as pl
from jax.experimental.pallas import tpu as pltpu

f32 = jnp.float32
bf16 = jnp.bfloat16

D_MODEL = 1024
DEPTH = 2
EPS = 1e-6
HEADS = 8
MLA_QK = 96
MLA_Q_RANK = 256
MLA_KV_RANK = 128
ROPE = 32
ROPE_THETA = 10000.0
FOX_DIM = 64
S5_GROUPS = 32
S5_GROUP = 16
S5_STATE = 64
S5_LANES = S5_GROUPS * S5_STATE
LANE = 128
S5_BLOCKS = S5_LANES // LANE
IN_WIDTH = 7080
TOK = 256
VMEM_LIMIT = 56 * 1024 * 1024

ADAM_LR = 0.001
ADAM_B1 = 0.9
ADAM_B2 = 0.999
ADAM_EPS = 1e-08
ADAM_WD = 0.01
ADAM_STEP = 10

_ORIG = {}
_off = 0
for _n, _w in (("cq", 256), ("ckv", 128), ("kpe", 32), ("fq", 512), ("fk", 512), ("fv", 512), ("ff", 8), ("s5u", 512),
               ("g_mla", 512), ("g_fox", 512), ("g_s5", 512), ("m_mla", 1024), ("m_fox", 1024), ("m_s5", 1024)):
    _ORIG[_n] = (_off, _w)
    _off += _w
_PAD = {"m_mla": (0, 1024, 0), "m_fox": (1024, 1024, 0), "m_s5": (2048, 1024, 0),
        "fq": (3072, 512, 0), "fk": (3584, 512, 0), "fv": (4096, 512, 0), "s5u": (4608, 512, 0),
        "g_mla": (5120, 512, 0), "g_fox": (5632, 512, 0), "g_s5": (6144, 512, 0),
        "cq": (6656, 256, 0), "ckv": (6912, 128, 0), "kpe": (7040, 128, 64), "ff": (7168, 128, 0)}
NP = 7680
_PAD_ORDER = ("m_mla", "m_fox", "m_s5", "fq", "fk", "fv", "s5u", "g_mla", "g_fox", "g_s5", "cq", "ckv", "kpe", "ff")


def _seg(name):
    start, width, _ = _PAD[name]
    return width, start // width


def _nn(a, b):
    return lax.dot_general(a.astype(bf16), b.astype(bf16), (((1,), (0,)), ((), ())), preferred_element_type=f32)


def _nt(a, b):
    return lax.dot_general(a.astype(bf16), b.astype(bf16), (((1,), (1,)), ((), ())), preferred_element_type=f32)


def _tn(a, b):
    return lax.dot_general(a.astype(bf16), b.astype(bf16), (((0,), (0,)), ((), ())), preferred_element_type=f32)


def _rms(x, g, n):
    r = lax.rsqrt(jnp.sum(x * x, axis=-1, keepdims=True) * (1.0 / n) + EPS)
    return x * r * g, r


def _rms_bwd(dy, x, r, g, n):
    xh = x * r
    dg = jnp.sum(dy * xh, axis=0, keepdims=True)
    dxh = dy * g
    dx = r * (dxh - xh * (jnp.sum(dxh * xh, axis=-1, keepdims=True) * (1.0 / n)))
    return dx, dg


def _sigmoid(x):
    return 1.0 / (1.0 + jnp.exp(-x))


_GELU_C = math.sqrt(2.0 / math.pi)


def _gelu(x):
    t = jnp.tanh(_GELU_C * (x + 0.044715 * x * x * x))
    return 0.5 * x * (1.0 + t), t


def _gelu_grad(x, t):
    return 0.5 * (1.0 + t) + 0.5 * x * (1.0 - t * t) * _GELU_C * (1.0 + 3.0 * 0.044715 * x * x)


def _accumulate(ref, val):
    i = pl.program_id(0)

    @pl.when(i == 0)
    def _():
        ref[...] = val

    @pl.when(i > 0)
    def _():
        ref[...] += val


def _rope(x, c, s1, s2):
    return x * c + pltpu.roll(x, LANE - 16, 1) * s1 + pltpu.roll(x, 16, 1) * s2


def _rope_t(d, c, s1, s2):
    return d * c + pltpu.roll(d * s1, 16, 1) + pltpu.roll(d * s2, LANE - 16, 1)


def _const_map(ndim):
    return lambda *_: (0,) * ndim


def _rowwise(name, body, n_tok, tiled_in, full_in, tiled_out, acc_out, tile=TOK):
    in_specs, args = [], []
    for arr, width, blk in tiled_in:
        in_specs.append(pl.BlockSpec((tile, width), functools.partial(lambda i, b: (i, b), b=blk)))
        args.append(arr)
    for arr in full_in:
        in_specs.append(pl.BlockSpec(arr.shape, _const_map(arr.ndim)))
        args.append(arr)
    out_specs, out_shape = [], []
    for width, dt in tiled_out:
        out_specs.append(pl.BlockSpec((tile, width), lambda i: (i, 0)))
        out_shape.append(jax.ShapeDtypeStruct((n_tok, width), dt))
    for shape, dt in acc_out:
        out_specs.append(pl.BlockSpec(shape, _const_map(len(shape))))
        out_shape.append(jax.ShapeDtypeStruct(shape, dt))
    return pl.pallas_call(
        body, grid=(n_tok // tile,), in_specs=in_specs, out_specs=out_specs, out_shape=out_shape, name=name,
        compiler_params=pltpu.CompilerParams(dimension_semantics=("arbitrary",), vmem_limit_bytes=VMEM_LIMIT),
    )(*args)


def _mm(name, a, b, *, mode, grid, a_spec, b_spec, o_spec, out_shape, acc_shape, add=None, add_spec=None):
    nk = grid[2]

    def body(*refs):
        if add is None:
            a_ref, b_ref, o_ref, acc_ref = refs
        else:
            a_ref, b_ref, add_ref, o_ref, acc_ref = refs
        k = pl.program_id(2)

        @pl.when(k == 0)
        def _():
            acc_ref[...] = jnp.zeros_like(acc_ref)

        acc_ref[...] += {"nn": _nn, "nt": _nt, "tn": _tn}[mode](a_ref[...], b_ref[...])

        @pl.when(k == nk - 1)
        def _():
            r = acc_ref[...]
            if add is not None:
                r = r + add_ref[...]
            o_ref[...] = r.astype(o_ref.dtype)

    in_specs = [a_spec, b_spec] + ([add_spec] if add is not None else [])
    args = (a, b) + ((add,) if add is not None else ())
    return pl.pallas_call(
        body, grid=grid, in_specs=in_specs, out_specs=o_spec, out_shape=out_shape, name=name,
        scratch_shapes=[pltpu.VMEM(acc_shape, f32)],
        compiler_params=pltpu.CompilerParams(dimension_semantics=("arbitrary", "arbitrary", "arbitrary"), vmem_limit_bytes=VMEM_LIMIT),
    )(*args)


def _mm_nn(name, a, b, *, m, n, k, tm, tn, tk, out_dtype=f32, a_koff=0):
    return _mm(name, a, b, mode="nn", grid=(m // tm, n // tn, k // tk),
               a_spec=pl.BlockSpec((tm, tk), lambda i, j, kk: (i, kk + a_koff)),
               b_spec=pl.BlockSpec((tk, tn), lambda i, j, kk: (kk, j)),
               o_spec=pl.BlockSpec((tm, tn), lambda i, j, kk: (i, j)),
               out_shape=jax.ShapeDtypeStruct((m, n), out_dtype), acc_shape=(tm, tn))


def _mm_tn(name, a, b, *, m, n, k, tm, tn, tk, a_moff=0):
    return _mm(name, a, b, mode="tn", grid=(m // tm, n // tn, k // tk),
               a_spec=pl.BlockSpec((tk, tm), lambda i, j, kk: (kk, i + a_moff)),
               b_spec=pl.BlockSpec((tk, tn), lambda i, j, kk: (kk, j)),
               o_spec=pl.BlockSpec((tm, tn), lambda i, j, kk: (i, j)),
               out_shape=jax.ShapeDtypeStruct((m, n), f32), acc_shape=(tm, tn))


ATT_KV = 256
ATT_Q = 512


def _attn_common(mla, n_tok):
    qw = 2 * LANE if mla else LANE
    scale = 1.0 / math.sqrt(MLA_QK if mla else FOX_DIM)
    return qw, scale, min(ATT_Q, n_tok)


def _attn_heads(q_ref, mla):
    out = []
    if mla:
        for e in (0, 1):
            qe = q_ref[:, e * LANE:(e + 1) * LANE]
            out.append((qe.astype(f32).T.astype(bf16), qe))
        return out
    q = q_ref[...]
    tq = q.shape[0]
    qt = q.astype(f32).T
    row = lax.broadcasted_iota(jnp.int32, (LANE, tq), 0)
    lane = lax.broadcasted_iota(jnp.int32, (tq, LANE), 1)
    for e in (0, 1):
        out.append((jnp.where((row >= 64) == bool(e), qt, 0.0).astype(bf16),
                    jnp.where((lane >= 64) == bool(e), q, jnp.zeros((), bf16))))
    return out


def _attn_allowed(off, i, tq, mla):
    kpos = off + lax.broadcasted_iota(jnp.int32, (ATT_KV, tq), 0)
    qpos = i * tq + lax.broadcasted_iota(jnp.int32, (ATT_KV, tq), 1)
    return ((kpos // 64) <= (qpos // 64)) if mla else (kpos <= qpos)


def _attn_fwd(name, q, k, v, cum_b, *, mla, n_tok):
    qw, scale, tq = _attn_common(mla, n_tok)
    nq = n_tok // tq
    nkv = n_tok // ATT_KV
    has_bias = cum_b is not None

    def body(*refs):
        if has_bias:
            q_ref, k_ref, v_ref, cb_ref, o_ref, lse_ref, vt_ref = refs
        else:
            q_ref, k_ref, v_ref, o_ref, lse_ref, vt_ref = refs
        i = pl.program_id(1)

        @pl.when(i == 0)
        def _():
            for jb in range(nkv):
                vt_ref[jb] = v_ref[jb * ATT_KV:(jb + 1) * ATT_KV, :].astype(f32).T.astype(bf16)

        heads = _attn_heads(q_ref, mla)

        def step(j, carry, masked):
            off = pl.multiple_of(j * ATT_KV, ATT_KV)
            allowed = _attn_allowed(off, i, tq, mla) if masked else None
            vt = vt_ref[j]
            sts = []
            for e in (0, 1):
                kb = k_ref[pl.ds(off, ATT_KV), e * LANE:(e + 1) * LANE] if mla else k_ref[pl.ds(off, ATT_KV), :]
                sts.append(_nn(kb, heads[e][0]))
            stats = []
            for e in (0, 1):
                m, l, _ = carry[e]
                st = sts[e] * scale
                if has_bias:
                    st = st - jnp.tile(cb_ref[e, pl.ds(off, ATT_KV), :], (1, tq // LANE))
                if masked:
                    st = jnp.where(allowed, st, -1e30)
                m_new = jnp.maximum(m, jnp.max(st, axis=0, keepdims=True))
                alpha = jnp.exp(m - m_new)
                pt = jnp.exp(st - m_new)
                stats.append((m_new, alpha * l + jnp.sum(pt, axis=0, keepdims=True), alpha, pt.astype(bf16)))
            new = []
            for e in (0, 1):
                m_new, l, alpha, pt = stats[e]
                new.append((m_new, l, alpha * carry[e][2] + _nn(vt[64 * e:64 * e + 64, :], pt)))
            return tuple(new)

        init = tuple((jnp.full((1, tq), -1e30, f32), jnp.zeros((1, tq), f32), jnp.zeros((64, tq), f32)) for _ in (0, 1))
        n_full = i * (tq // ATT_KV)
        carry = lax.fori_loop(0, n_full, functools.partial(step, masked=False), init)
        for d in range(tq // ATT_KV):
            carry = step(n_full + d, carry, True)
        o_ref[...] = jnp.concatenate([carry[e][2] / carry[e][1] for e in (0, 1)], axis=0).T
        lse_ref[...] = jnp.zeros_like(lse_ref)
        for e in (0, 1):
            lse_ref[e:e + 1, :] = carry[e][0] + jnp.log(carry[e][1])

    in_specs = [pl.BlockSpec((tq, qw), lambda p, i: (i, p)),
                pl.BlockSpec((n_tok, qw), lambda p, i: (0, p)),
                pl.BlockSpec((n_tok, LANE), lambda p, i: (0, p))]
    args = [q, k, v]
    if has_bias:
        in_specs.append(pl.BlockSpec((2, n_tok, LANE), lambda p, i: (p, 0, 0)))
        args.append(cum_b)
    return pl.pallas_call(
        body, grid=(4, nq), in_specs=in_specs,
        out_specs=[pl.BlockSpec((tq, LANE), lambda p, i: (i, p)), pl.BlockSpec((None, 8, tq), lambda p, i: (p, 0, i))],
        out_shape=[jax.ShapeDtypeStruct((n_tok, 512), f32), jax.ShapeDtypeStruct((4, 8, n_tok), f32)], name=name,
        scratch_shapes=[pltpu.VMEM((nkv, LANE, ATT_KV), bf16)],
        compiler_params=pltpu.CompilerParams(dimension_semantics=("arbitrary", "arbitrary"), vmem_limit_bytes=VMEM_LIMIT),
    )(*args)


def _attn_bwd(name, q, k, v, o, lse, do, cum_b, *, mla, n_tok):
    qw, scale, tq = _attn_common(mla, n_tok)
    nq = n_tok // tq
    nkv = n_tok // ATT_KV
    has_bias = cum_b is not None

    def body(*refs):
        if has_bias:
            q_ref, k_ref, v_ref, o_ref, lse_ref, do_ref, cb_ref, dq_ref, dk_ref, dv_ref, dck_ref, dcq_ref, kt_ref = refs
        else:
            q_ref, k_ref, v_ref, o_ref, lse_ref, do_ref, dq_ref, dk_ref, dv_ref, kt_ref = refs
        p = pl.program_id(0)
        i = pl.program_id(1)

        @pl.when(i == 0)
        def _():
            dk_ref[...] = jnp.zeros_like(dk_ref)
            dv_ref[...] = jnp.zeros_like(dv_ref)
            for jb in range(nkv):
                for c0 in range(0, qw, LANE):
                    kt_ref[jb, c0:c0 + LANE, :] = k_ref[jb * ATT_KV:(jb + 1) * ATT_KV, c0:c0 + LANE].astype(f32).T.astype(bf16)

        if has_bias:
            @pl.when(jnp.logical_and(i == 0, p == 0))
            def _():
                dck_ref[...] = jnp.zeros_like(dck_ref)

        heads = _attn_heads(q_ref, mla)
        do = do_ref[...]
        do_t = do.T
        prod_t = (do * o_ref[...]).T
        row = lax.broadcasted_iota(jnp.int32, (LANE, tq), 0)
        lane = lax.broadcasted_iota(jnp.int32, (tq, LANE), 1)
        lane_k = lax.broadcasted_iota(jnp.int32, (ATT_KV, LANE), 1)
        per_head = []
        for e in (0, 1):
            sel_r = (row >= 64) == bool(e)
            per_head.append((jnp.where(sel_r, do_t, 0.0).astype(bf16),
                             jnp.where((lane >= 64) == bool(e), do, 0.0).astype(bf16),
                             jnp.sum(jnp.where(sel_r, prod_t, 0.0), axis=0, keepdims=True),
                             lse_ref[e:e + 1, :]))
        dq_rows = LANE if mla else 64

        def step(j, carry, masked):
            off = pl.multiple_of(j * ATT_KV, ATT_KV)
            allowed = _attn_allowed(off, i, tq, mla) if masked else None
            vb = v_ref[pl.ds(off, ATT_KV), :]
            kt = kt_ref[j]
            cols = [slice(e * LANE, (e + 1) * LANE) if mla else slice(None) for e in (0, 1)]
            sts = [_nn(k_ref[pl.ds(off, ATT_KV), cols[e]], heads[e][0]) for e in (0, 1)]
            dpts = [_nn(vb, per_head[e][0]) for e in (0, 1)]
            mids = []
            for e in (0, 1):
                _, _, delta, lse_e = per_head[e]
                st = sts[e] * scale
                if has_bias:
                    st = st - jnp.tile(cb_ref[e, pl.ds(off, ATT_KV), :], (1, tq // LANE))
                pt = jnp.exp(st - lse_e)
                if masked:
                    pt = jnp.where(allowed, pt, 0.0)
                dst = pt * (dpts[e] - delta)
                qsum = carry[e][1]
                if has_bias:
                    rs = jnp.sum(dst, axis=1, keepdims=True)
                    dck_ref[pl.ds(off, ATT_KV), :] += jnp.where(lane_k == 2 * p + e, -rs, 0.0)
                    qsum = qsum + jnp.sum(dst, axis=0, keepdims=True)
                mids.append((pt.astype(bf16), dst.astype(bf16), qsum))
            new = []
            for e in (0, 1):
                pt, dst, qsum = mids[e]
                kt_e = kt[e * LANE:(e + 1) * LANE, :] if mla else kt[64 * e:64 * e + 64, :]
                new.append((carry[e][0] + _nn(kt_e, dst) * scale, qsum))
                dk_ref[pl.ds(off, ATT_KV), cols[e]] += _nn(dst, heads[e][1]) * scale
                dv_ref[pl.ds(off, ATT_KV), :] += _nn(pt, per_head[e][1])
            return tuple(new)

        init = tuple((jnp.zeros((dq_rows, tq), f32), jnp.zeros((1, tq), f32)) for _ in (0, 1))
        n_full = i * (tq // ATT_KV)
        carry = lax.fori_loop(0, n_full, functools.partial(step, masked=False), init)
        for d in range(tq // ATT_KV):
            carry = step(n_full + d, carry, True)
        if mla:
            for e in (0, 1):
                dq_ref[:, e * LANE:(e + 1) * LANE] = carry[e][0].T
        else:
            dq_ref[...] = jnp.concatenate([carry[0][0], carry[1][0]], axis=0).T
        if has_bias:
            dcq_ref[...] = jnp.zeros_like(dcq_ref)
            for e in (0, 1):
                dcq_ref[e:e + 1, :] = carry[e][1]

    tile_q = pl.BlockSpec((tq, qw), lambda p, i: (i, p))
    tile_v = pl.BlockSpec((tq, LANE), lambda p, i: (i, p))
    full_k = pl.BlockSpec((n_tok, qw), lambda p, i: (0, p))
    full_v = pl.BlockSpec((n_tok, LANE), lambda p, i: (0, p))
    in_specs = [tile_q, full_k, full_v, tile_v, pl.BlockSpec((None, 8, tq), lambda p, i: (p, 0, i)), tile_v]
    args = [q, k, v, o, lse, do]
    out_specs = [tile_q, full_k, full_v]
    out_shape = [jax.ShapeDtypeStruct((n_tok, 4 * qw), f32), jax.ShapeDtypeStruct((n_tok, 4 * qw), f32),
                 jax.ShapeDtypeStruct((n_tok, 512), f32)]
    if has_bias:
        in_specs.append(pl.BlockSpec((2, n_tok, LANE), lambda p, i: (p, 0, 0)))
        args.append(cum_b)
        out_specs += [pl.BlockSpec((n_tok, LANE), _const_map(2)), pl.BlockSpec((None, 8, tq), lambda p, i: (p, 0, i))]
        out_shape += [jax.ShapeDtypeStruct((n_tok, LANE), f32), jax.ShapeDtypeStruct((4, 8, n_tok), f32)]
    return pl.pallas_call(
        body, grid=(4, nq), in_specs=in_specs, out_specs=out_specs, out_shape=out_shape, name=name,
        scratch_shapes=[pltpu.VMEM((nkv, qw, ATT_KV), bf16)],
        compiler_params=pltpu.CompilerParams(dimension_semantics=("arbitrary", "arbitrary"), vmem_limit_bytes=VMEM_LIMIT),
    )(*args)


def _s5_disc(lr, li, ldt):
    dt = jnp.exp(ldt)
    mag = jnp.exp(lr * dt)
    a_re = mag * jnp.cos(li * dt)
    a_im = mag * jnp.sin(li * dt)
    den = lr * lr + li * li
    f_re = ((a_re - 1.0) * lr + a_im * li) / den
    f_im = (a_im * lr - (a_re - 1.0) * li) / den
    return a_re, a_im, f_re, f_im


def _s5_param_fwd(lr, li, ldt, b_re, b_im):
    def body(lr_ref, li_ref, ldt_ref, br_ref, bi_ref, ar_ref, ai_ref, bbr_ref, bbi_ref):
        a_re, a_im, f_re, f_im = _s5_disc(lr_ref[...], li_ref[...], ldt_ref[...])
        ar_ref[...] = a_re
        ai_ref[...] = a_im
        br, bi = br_ref[...], bi_ref[...]
        bbr_ref[...] = f_re * br - f_im * bi
        bbi_ref[...] = f_re * bi + f_im * br

    col = jax.ShapeDtypeStruct((S5_LANES, 1), f32)
    mat = jax.ShapeDtypeStruct((S5_LANES, S5_GROUP), f32)
    return pl.pallas_call(body, out_shape=[col, col, mat, mat], name="s5_param_fwd")(lr, li, ldt, b_re, b_im)


def _s5_param_bwd(lr, li, ldt, b_re, b_im, da_re, da_im, dbb_re, dbb_im):
    def body(lr_ref, li_ref, ldt_ref, br_ref, bi_ref, dar_ref, dai_ref, gbr_ref, gbi_ref,
             dlr_ref, dli_ref, dldt_ref, dbr_ref, dbi_ref):
        (a_re, a_im, f_re, f_im), vjp = jax.vjp(_s5_disc, lr_ref[...], li_ref[...], ldt_ref[...])
        br, bi, gr, gi = br_ref[...], bi_ref[...], gbr_ref[...], gbi_ref[...]
        dbr_ref[...] = f_re * gr + f_im * gi
        dbi_ref[...] = f_re * gi - f_im * gr
        dfr = jnp.sum(br * gr + bi * gi, axis=-1, keepdims=True)
        dfi = jnp.sum(br * gi - bi * gr, axis=-1, keepdims=True)
        dlr, dli, dldt = vjp((dar_ref[...], dai_ref[...], dfr, dfi))
        dlr_ref[...] = dlr
        dli_ref[...] = dli
        dldt_ref[...] = jnp.sum(dldt.reshape(S5_GROUPS, S5_STATE, 1), axis=1)

    col = jax.ShapeDtypeStruct((S5_LANES, 1), f32)
    mat = jax.ShapeDtypeStruct((S5_LANES, S5_GROUP), f32)
    return pl.pallas_call(body, out_shape=[col, col, jax.ShapeDtypeStruct((S5_GROUPS, 1), f32), mat, mat],
                          name="s5_param_bwd")(lr, li, ldt, b_re, b_im, da_re, da_im, dbb_re, dbb_im)


_SCAN_NB = 4


def _to_streams(a):
    s, c = a.shape
    return jnp.swapaxes(a.reshape(8, s // 8, c), 0, 1).reshape(s, c)


def _from_streams(a):
    s, c = a.shape
    return jnp.swapaxes(a.reshape(s // 8, 8, c), 0, 1).reshape(s, c)


def _s5_scan(name, bu, a_re8, a_im8, *, reverse, n_tok):
    rows = n_tok // 8
    nb = _SCAN_NB

    def body(bu_ref, ar_ref, ai_ref, x_ref):
        a_r = [ar_ref[b] for b in range(nb)]
        a_i = [ai_ref[b] for b in range(nb)]
        zero = jnp.zeros((8, LANE), f32)
        one = jnp.ones((8, LANE), f32)

        def rows_at(r):
            rr = (rows - 1 - r) if reverse else r
            return pl.ds(pl.multiple_of(rr * 8, 8), 8)

        def pass1(r, carry):
            out = []
            sl = rows_at(r)
            for b in range(nb):
                xr, xi, mr, mi = carry[b]
                nr = a_r[b] * xr - a_i[b] * xi + bu_ref[0, b, sl, :]
                ni = a_r[b] * xi + a_i[b] * xr + bu_ref[1, b, sl, :]
                x_ref[0, b, sl, :] = nr
                x_ref[1, b, sl, :] = ni
                out.append((nr, ni, a_r[b] * mr - a_i[b] * mi, a_r[b] * mi + a_i[b] * mr))
            return tuple(out)

        carry = lax.fori_loop(0, rows, pass1, tuple((zero, zero, one, zero) for _ in range(nb)))
        sub = lax.broadcasted_iota(jnp.int32, (8, LANE), 0)
        feed = []
        for b in range(nb):
            lr_, li_, pr, pi = carry[b]
            fr, fi = zero, zero
            for _ in range(7):
                tr = lr_ + pr * fr - pi * fi
                ti = li_ + pr * fi + pi * fr
                if reverse:
                    fr = jnp.where(sub < 7, pltpu.roll(tr, 7, 0), 0.0)
                    fi = jnp.where(sub < 7, pltpu.roll(ti, 7, 0), 0.0)
                else:
                    fr = jnp.where(sub > 0, pltpu.roll(tr, 1, 0), 0.0)
                    fi = jnp.where(sub > 0, pltpu.roll(ti, 1, 0), 0.0)
            feed.append((fr, fi))

        def pass2(r, carry):
            out = []
            sl = rows_at(r)
            for b in range(nb):
                mr, mi = carry[b]
                fr, fi = feed[b]
                x_ref[0, b, sl, :] += mr * fr - mi * fi
                x_ref[1, b, sl, :] += mr * fi + mi * fr
                out.append((a_r[b] * mr - a_i[b] * mi, a_r[b] * mi + a_i[b] * mr))
            return tuple(out)

        lax.fori_loop(0, rows, pass2, tuple((a_r[b], a_i[b]) for b in range(nb)))

    blk = pl.BlockSpec((2, nb, n_tok, LANE), lambda g: (0, g, 0, 0))
    ablk = pl.BlockSpec((nb, 8, LANE), lambda g: (g, 0, 0))
    return pl.pallas_call(
        body, grid=(S5_BLOCKS // nb,), in_specs=[blk, ablk, ablk], out_specs=blk,
        out_shape=jax.ShapeDtypeStruct((2, S5_BLOCKS, n_tok, LANE), f32), name=name,
        compiler_params=pltpu.CompilerParams(dimension_semantics=("arbitrary",), vmem_limit_bytes=VMEM_LIMIT),
    )(bu, a_re8, a_im8)


def _s5_da(xs, gx, *, n_tok):
    def body(x_ref, g_ref, o_ref):
        t = lax.broadcasted_iota(jnp.int32, (n_tok, LANE), 0)
        sub = lax.broadcasted_iota(jnp.int32, (8, LANE), 0)

        def prev(v):
            return (jnp.where(t >= 8, pltpu.roll(v, 8, 0), 0.0),
                    jnp.where(sub > 0, pltpu.roll(v[n_tok - 8:, :], 1, 0), 0.0))

        (xr, hr), (xi, hi) = prev(x_ref[0, 0]), prev(x_ref[1, 0])
        gr, gi = g_ref[0, 0], g_ref[1, 0]
        gr0, gi0 = gr[0:8, :], gi[0:8, :]
        o_ref[0, 0:1, :] = (jnp.sum(xr * gr + xi * gi, axis=0, keepdims=True)
                            + jnp.sum(hr * gr0 + hi * gi0, axis=0, keepdims=True))
        o_ref[0, 1:2, :] = (jnp.sum(xr * gi - xi * gr, axis=0, keepdims=True)
                            + jnp.sum(hr * gi0 - hi * gr0, axis=0, keepdims=True))

    blk = pl.BlockSpec((2, 1, n_tok, LANE), lambda g: (0, g, 0, 0))
    return pl.pallas_call(
        body, grid=(S5_BLOCKS,), in_specs=[blk, blk], out_specs=pl.BlockSpec((1, 2, LANE), lambda g: (g, 0, 0)),
        out_shape=jax.ShapeDtypeStruct((S5_BLOCKS, 2, LANE), f32), name="s5_da",
        compiler_params=pltpu.CompilerParams(dimension_semantics=("arbitrary",), vmem_limit_bytes=VMEM_LIMIT),
    )(xs, gx)


S5_Q = 4


def _bd8(t):
    _, a, b = t.shape
    t = t.reshape(S5_Q, 8, a, 1, b)
    eye = jnp.eye(8, dtype=jnp.bool_).reshape(1, 8, 1, 8, 1)
    return jnp.where(eye, jnp.broadcast_to(t, (S5_Q, 8, a, 8, b)), jnp.zeros((), t.dtype)).reshape(S5_Q, 8 * a, 8 * b)


def _bd8_diag(m, a, b):
    m = m.reshape(S5_Q, 8, a, 8, b)
    eye = jnp.eye(8, dtype=jnp.bool_).reshape(1, 8, 1, 8, 1)
    return jnp.sum(jnp.where(eye, m, 0.0), axis=3).reshape(S5_GROUPS, a, b)


def _s5_expand(name, a, a_blk0, wq, *, n_tok):
    def body(a_ref, w_ref, o_ref):
        r = _nn(a_ref[...], w_ref[...])
        for k in range(4):
            o_ref[k] = r[:, k * LANE:(k + 1) * LANE]

    return pl.pallas_call(
        body, grid=(2, S5_Q),
        in_specs=[pl.BlockSpec((n_tok, LANE), lambda ri, q: (0, a_blk0 + q)),
                  pl.BlockSpec((None, None, LANE, 512), lambda ri, q: (ri, q, 0, 0))],
        out_specs=pl.BlockSpec((None, 4, n_tok, LANE), lambda ri, q: (ri, q, 0, 0)),
        out_shape=jax.ShapeDtypeStruct((2, S5_BLOCKS, n_tok, LANE), f32), name=name,
        compiler_params=pltpu.CompilerParams(dimension_semantics=("arbitrary", "arbitrary"), vmem_limit_bytes=VMEM_LIMIT),
    )(a, wq)


def _s5_contract(name, xs, wq, add, out_dtype, *, n_tok):
    def body(*refs):
        if add is None:
            x_ref, w_ref, o_ref, acc_ref = refs
        else:
            x_ref, w_ref, add_ref, o_ref, acc_ref = refs
        ri = pl.program_id(1)
        r = _nn(x_ref[0], w_ref[0:LANE, :])
        for k in range(1, 4):
            r = r + _nn(x_ref[k], w_ref[k * LANE:(k + 1) * LANE, :])

        @pl.when(ri == 0)
        def _():
            acc_ref[...] = r

        @pl.when(ri == 1)
        def _():
            tot = acc_ref[...] + r
            if add is not None:
                tot = tot + add_ref[...]
            o_ref[...] = tot.astype(o_ref.dtype)

    col = pl.BlockSpec((n_tok, LANE), lambda q, ri: (0, q))
    in_specs = [pl.BlockSpec((None, 4, n_tok, LANE), lambda q, ri: (ri, q, 0, 0)),
                pl.BlockSpec((None, None, 512, LANE), lambda q, ri: (ri, q, 0, 0))]
    args = [xs, wq]
    if add is not None:
        in_specs.append(col)
        args.append(add)
    return pl.pallas_call(
        body, grid=(S5_Q, 2), in_specs=in_specs, out_specs=col, out_shape=jax.ShapeDtypeStruct((n_tok, 512), out_dtype), name=name,
        scratch_shapes=[pltpu.VMEM((n_tok, LANE), f32)],
        compiler_params=pltpu.CompilerParams(dimension_semantics=("arbitrary", "arbitrary"), vmem_limit_bytes=VMEM_LIMIT),
    )(*args)


def _s5_wgrad_states(name, xs, d, *, n_tok):
    def body(x_ref, d_ref, o_ref):
        for k in range(4):
            o_ref[k * LANE:(k + 1) * LANE, :] = _tn(x_ref[k], d_ref[...])

    return pl.pallas_call(
        body, grid=(2, S5_Q),
        in_specs=[pl.BlockSpec((None, 4, n_tok, LANE), lambda ri, q: (ri, q, 0, 0)), pl.BlockSpec((n_tok, LANE), lambda ri, q: (0, q))],
        out_specs=pl.BlockSpec((None, None, 512, LANE), lambda ri, q: (ri, q, 0, 0)),
        out_shape=jax.ShapeDtypeStruct((2, S5_Q, 512, LANE), f32), name=name,
        compiler_params=pltpu.CompilerParams(dimension_semantics=("arbitrary", "arbitrary"), vmem_limit_bytes=VMEM_LIMIT),
    )(xs, d)


def _s5_wgrad_channels(name, a, a_blk0, gx, *, n_tok):
    def body(a_ref, g_ref, o_ref):
        for k in range(4):
            o_ref[:, k * LANE:(k + 1) * LANE] = _tn(a_ref[...], g_ref[k])

    return pl.pallas_call(
        body, grid=(2, S5_Q),
        in_specs=[pl.BlockSpec((n_tok, LANE), lambda ri, q: (0, a_blk0 + q)), pl.BlockSpec((None, 4, n_tok, LANE), lambda ri, q: (ri, q, 0, 0))],
        out_specs=pl.BlockSpec((None, None, LANE, 512), lambda ri, q: (ri, q, 0, 0)),
        out_shape=jax.ShapeDtypeStruct((2, S5_Q, LANE, 512), f32), name=name,
        compiler_params=pltpu.CompilerParams(dimension_semantics=("arbitrary", "arbitrary"), vmem_limit_bytes=VMEM_LIMIT),
    )(a, gx)


N_CHIPS = 4
_BIG_SHARD = {"w_in": (1, 1024, 1770, 1792), "mla_w_q_up": (1, 256, 192, 256), "mla_w_kv_up": (1, 128, 256, 256),
              "s5_w_glu": (0, 128, 512, 512), "w_branch_out": (0, 384, 1024, 1024), "w_out": (0, 256, 1024, 1024)}


def _to_shards(name, m):
    axis, r, c, cp = _BIG_SHARD[name]
    if axis == 0:
        return m.reshape(N_CHIPS, r, c)
    return jnp.stack([jnp.pad(m[:, j * c:(j + 1) * c], ((0, 0), (0, cp - c))) for j in range(N_CHIPS)])


def _from_shards(name, s):
    axis, r, c, cp = _BIG_SHARD[name]
    if axis == 0:
        return s.reshape(N_CHIPS * r, c)
    return jnp.concatenate([s[j, :, :c] for j in range(N_CHIPS)], axis=1)


def _pad_w_in(w):
    pieces, pos = [], 0
    for name in _PAD_ORDER:
        start, width, inner = _PAD[name]
        o0, ow = _ORIG[name]
        if start + inner > pos:
            pieces.append(jnp.zeros((w.shape[0], start + inner - pos), w.dtype))
        pieces.append(w[:, o0:o0 + ow])
        pos = start + inner + ow
    pieces.append(jnp.zeros((w.shape[0], NP - pos), w.dtype))
    return jnp.concatenate(pieces, axis=1)


def _layer_weights(l, small, big):
    w = {}
    w["w_in_shards"] = big["w_in"][l]
    w["w_in"] = _pad_w_in(_from_shards("w_in", big["w_in"][l]))
    wq = _from_shards("mla_w_q_up", big["mla_w_q_up"][l]).reshape(MLA_Q_RANK, HEADS, MLA_QK)
    w["wq"] = jnp.pad(wq, ((0, 0), (0, 0), (0, LANE - MLA_QK))).reshape(MLA_Q_RANK, HEADS * LANE)
    wkv = _from_shards("mla_w_kv_up", big["mla_w_kv_up"][l]).reshape(MLA_KV_RANK, HEADS, 128)
    wk = jnp.pad(wkv[:, :, :64], ((0, 0), (0, 0), (0, 64))).reshape(MLA_KV_RANK, HEADS * LANE)
    wv = wkv[:, :, 64:].reshape(MLA_KV_RANK, 512)
    w["wkv"] = jnp.concatenate([wk, wv], axis=1)
    w["w_glu"] = _from_shards("s5_w_glu", big["s5_w_glu"][l])
    w["wo"] = _from_shards("w_branch_out", big["w_branch_out"][l])
    w["w_out"] = _from_shards("w_out", big["w_out"][l])
    row = lambda a: a.reshape(1, -1).astype(f32)
    w["norm_g"] = row(small["norm_g"][l])
    w["qa_g"] = row(small["mla_q_a_norm"][l])
    w["kva_g"] = row(small["mla_kv_a_norm"][l])
    w["qn_g"] = jnp.pad(row(small["mla_q_norm"][l]), ((0, 0), (0, LANE - MLA_QK)))
    w["kn_g"] = jnp.pad(row(small["mla_k_norm"][l]), ((0, 0), (0, LANE - MLA_QK)))
    w["fq_g"] = jnp.tile(row(small["fox_q_norm"][l]), (1, 2))
    w["fk_g"] = jnp.tile(row(small["fox_k_norm"][l]), (1, 2))
    w["b_f"] = jnp.pad(row(small["fox_b_f"][l]), ((0, 0), (0, LANE - HEADS)))
    w["lr"] = small["s5_lambda_re"][l].reshape(S5_LANES, 1)
    w["li"] = small["s5_lambda_im"][l].reshape(S5_LANES, 1)
    w["ldt"] = jnp.repeat(small["s5_log_dt"][l], S5_STATE).reshape(S5_LANES, 1)
    w["b_re"] = small["s5_b_re"][l].reshape(S5_LANES, S5_GROUP)
    w["b_im"] = small["s5_b_im"][l].reshape(S5_LANES, S5_GROUP)
    w["c_re"] = small["s5_c_re"][l]
    w["c_im"] = small["s5_c_im"][l]
    w["s5_d"] = row(small["s5_d"][l])
    w["b_glu"] = row(small["s5_b_glu"][l])
    return w


def _fox_halves(x, lane):
    sq = x * x
    lo = jnp.sum(jnp.where(lane < 64, sq, 0.0), axis=-1, keepdims=True)
    hi = jnp.sum(sq, axis=-1, keepdims=True) - lo
    return jnp.where(lane < 64, lax.rsqrt(lo * (1.0 / 64) + EPS), lax.rsqrt(hi * (1.0 / 64) + EPS))


def _fox_halves_bwd(dy, x, r, g, lane):
    xh = x * r
    dxh = dy * g
    pr = dxh * xh
    lo = jnp.sum(jnp.where(lane < 64, pr, 0.0), axis=-1, keepdims=True)
    hi = jnp.sum(pr, axis=-1, keepdims=True) - lo
    mean = jnp.where(lane < 64, lo, hi) * (1.0 / 64)
    return r * (dxh - xh * mean), jnp.sum(dy * xh, axis=0, keepdims=True)


def _mla_recompute(cq, ckv, kpe, c, s1, s2, qa_g, kva_g, wq, wkv):
    cqn, r_cq = _rms(cq, qa_g, MLA_Q_RANK)
    ckvn, r_ckv = _rms(ckv, kva_g, MLA_KV_RANK)
    cqn_b = cqn.astype(bf16)
    ckvn_b = ckvn.astype(bf16)
    q_raw = _nn(cqn_b, wq)
    kv_raw = _nn(ckvn_b, wkv)
    kpe_rot = _rope(kpe, c, s1, s2)
    return cqn_b, r_cq, ckvn_b, r_ckv, q_raw, kv_raw, kpe_rot


def _layer_fwd(x, w, rope_tabs, n_tok):
    c_tab, s1_tab, s2_tab = rope_tabs
    saved = {"x": x}

    def norm_body(x_ref, g_ref, h_ref):
        h_ref[...] = _rms(x_ref[...], g_ref[...], D_MODEL)[0].astype(bf16)

    (h,) = _rowwise("norm_fwd", norm_body, n_tok, [(x, D_MODEL, 0)], [w["norm_g"]], [(D_MODEL, bf16)], [])
    proj = _mm_nn("in_proj", h, w["w_in"], m=n_tok, n=NP, k=D_MODEL, tm=n_tok, tn=512, tk=D_MODEL)
    saved["h"], saved["proj"] = h, proj

    def mla_prep_body(cq_ref, ckv_ref, kpe_ref, c_ref, s1_ref, s2_ref, qa_ref, kva_ref, wq_ref, wkv_ref, qn_g_ref, kn_g_ref,
                      qn_ref, kn_ref, v_ref):
        c, s1, s2 = c_ref[...], s1_ref[...], s2_ref[...]
        _, _, _, _, q_raw, kv_raw, kpe_rot = _mla_recompute(cq_ref[...], ckv_ref[...], kpe_ref[...], c, s1, s2,
                                                            qa_ref[...], kva_ref[...], wq_ref[...], wkv_ref[...])
        for hd in range(HEADS):
            sl = slice(hd * LANE, (hd + 1) * LANE)
            qn_ref[:, sl] = _rms(_rope(q_raw[:, sl], c, s1, s2), qn_g_ref[...], MLA_QK)[0].astype(bf16)
            kn_ref[:, sl] = _rms(kv_raw[:, sl] + kpe_rot, kn_g_ref[...], MLA_QK)[0].astype(bf16)
        v_ref[...] = kv_raw[:, HEADS * LANE:].astype(bf16)

    qn, kn, v_mla = _rowwise(
        "mla_prep", mla_prep_body, n_tok,
        [(proj, *_seg("cq")), (proj, *_seg("ckv")), (proj, *_seg("kpe")), (c_tab, LANE, 0), (s1_tab, LANE, 0), (s2_tab, LANE, 0)],
        [w["qa_g"], w["kva_g"], w["wq"], w["wkv"], w["qn_g"], w["kn_g"]],
        [(HEADS * LANE, bf16), (HEADS * LANE, bf16), (512, bf16)], [])
    y_mla, lse_mla = _attn_fwd("mla_attn_fwd", qn, kn, v_mla, None, mla=True, n_tok=n_tok)
    saved.update(qn=qn, kn=kn, v_mla=v_mla, y_mla=y_mla, lse_mla=lse_mla)

    def fox_prep_body(fq_ref, fk_ref, fv_ref, ff_ref, qg_ref, kg_ref, bf_ref, fqn_ref, fkn_ref, fvb_ref, logf_ref):
        lane = lax.broadcasted_iota(jnp.int32, (TOK, LANE), 1)
        for blk in range(4):
            sl = slice(blk * LANE, (blk + 1) * LANE)
            xq = fq_ref[:, sl]
            fqn_ref[:, sl] = (xq * _fox_halves(xq, lane) * qg_ref[...]).astype(bf16)
            xk = fk_ref[:, sl]
            fkn_ref[:, sl] = (xk * _fox_halves(xk, lane) * kg_ref[...]).astype(bf16)
        fvb_ref[...] = fv_ref[...].astype(bf16)
        z = ff_ref[...] + bf_ref[...]
        logf_ref[...] = jnp.minimum(z, 0.0) - jnp.log(1.0 + jnp.exp(-jnp.abs(z)))

    fqn, fkn, fvb, logf = _rowwise(
        "fox_prep", fox_prep_body, n_tok,
        [(proj, *_seg("fq")), (proj, *_seg("fk")), (proj, *_seg("fv")), (proj, *_seg("ff"))],
        [w["fq_g"], w["fk_g"], w["b_f"]],
        [(512, bf16), (512, bf16), (512, bf16), (LANE, f32)], [])

    def cum_body(x_ref, cum_ref):
        x = x_ref[...]
        t = lax.broadcasted_iota(jnp.int32, x.shape, 0)
        s = 1
        while s < n_tok:
            x = x + jnp.where(t >= s, pltpu.roll(x, s, 0), 0.0)
            s *= 2
        for hd in range(HEADS):
            cum_ref[hd] = jnp.broadcast_to(x[:, hd:hd + 1], (n_tok, LANE))

    cum_b = pl.pallas_call(cum_body, out_shape=jax.ShapeDtypeStruct((HEADS, n_tok, LANE), f32), name="fox_cum")(logf)
    y_fox, lse_fox = _attn_fwd("fox_attn_fwd", fqn, fkn, fvb, cum_b, mla=False, n_tok=n_tok)
    saved.update(fqn=fqn, fkn=fkn, fvb=fvb, cum_b=cum_b, y_fox=y_fox, lse_fox=lse_fox)

    a_re, a_im, bb_re, bb_im = _s5_param_fwd(w["lr"], w["li"], w["ldt"], w["b_re"], w["b_im"])
    per_group = lambda m: m.reshape(S5_GROUPS, S5_STATE, S5_GROUP)
    b_cn = jnp.stack([_bd8(jnp.swapaxes(per_group(bb_re), 1, 2)), _bd8(jnp.swapaxes(per_group(bb_im), 1, 2))]).astype(bf16)
    b_nc = jnp.stack([_bd8(per_group(bb_re)), _bd8(per_group(bb_im))]).astype(bf16)
    c_nc = jnp.stack([_bd8(jnp.swapaxes(w["c_re"], 1, 2)), -_bd8(jnp.swapaxes(w["c_im"], 1, 2))]).astype(bf16)
    c_cn = jnp.stack([_bd8(w["c_re"]), -_bd8(w["c_im"])]).astype(bf16)
    a_re8 = jnp.broadcast_to(a_re.reshape(S5_BLOCKS, 1, LANE), (S5_BLOCKS, 8, LANE))
    a_im8 = jnp.broadcast_to(a_im.reshape(S5_BLOCKS, 1, LANE), (S5_BLOCKS, 8, LANE))
    u_w, u_blk = _seg("s5u")
    u_streams = _to_streams(proj[:, u_blk * u_w:(u_blk + 1) * u_w])
    bu = _s5_expand("s5_bu", u_streams, 0, b_cn, n_tok=n_tok)
    xs = _s5_scan("s5_scan_fwd", bu, a_re8, a_im8, reverse=False, n_tok=n_tok)
    ylin = _from_streams(_s5_contract("s5_y", xs, c_nc, None, f32, n_tok=n_tok))

    def s5_post_body(yl_ref, u_ref, d_ref, wg_ref, bg_ref, out_ref):
        y = yl_ref[...] + d_ref[...] * u_ref[...]
        z, _ = _gelu(y)
        out_ref[...] = z * _sigmoid(_nn(z, wg_ref[...]) + bg_ref[...])

    (y_s5,) = _rowwise("s5_post", s5_post_body, n_tok, [(ylin, 512, 0), (proj, u_w, u_blk)],
                       [w["s5_d"], w["w_glu"], w["b_glu"]], [(512, f32)], [])
    saved.update(xs=xs, ylin=ylin, y_s5=y_s5, b_nc=b_nc, c_cn=c_cn, a_re8=a_re8, a_im8=a_im8, u_streams=u_streams)

    def merge_body(ym_ref, yf_ref, ys_ref, gm_ref, gf_ref, gs_ref, mm_ref, mf_ref, ms_ref, x_ref, wo_ref, wout_ref, out_ref):
        merged = jnp.zeros((TOK, D_MODEL), f32)
        for b, (y_ref, g_ref, m_ref) in enumerate(((ym_ref, gm_ref, mm_ref), (yf_ref, gf_ref, mf_ref), (ys_ref, gs_ref, ms_ref))):
            g = g_ref[...]
            a = y_ref[...] * (g * _sigmoid(g))
            merged = merged + _sigmoid(m_ref[...]) * _nn(a, wo_ref[b * 512:(b + 1) * 512, :])
        out_ref[...] = x_ref[...] + _nn(merged, wout_ref[...])

    (out,) = _rowwise(
        "merge_fwd", merge_body, n_tok,
        [(y_mla, 512, 0), (y_fox, 512, 0), (y_s5, 512, 0), (proj, *_seg("g_mla")), (proj, *_seg("g_fox")), (proj, *_seg("g_s5")),
         (proj, *_seg("m_mla")), (proj, *_seg("m_fox")), (proj, *_seg("m_s5")), (x, D_MODEL, 0)],
        [w["wo"], w["w_out"]], [(D_MODEL, f32)], [])
    return out, saved


def _layer_bwd(dout, w, sv, rope_tabs, n_tok):
    c_tab, s1_tab, s2_tab = rope_tabs
    proj, x = sv["proj"], sv["x"]
    grads = {}

    def merge_bwd_body(ym_ref, yf_ref, ys_ref, gm_ref, gf_ref, gs_ref, mm_ref, mf_ref, ms_ref, do_ref, wo_ref, wout_ref,
                       dym_ref, dyf_ref, dys_ref, dgm_ref, dgf_ref, dgs_ref, dmm_ref, dmf_ref, dms_ref, dwo_ref, dwout_ref):
        do = do_ref[...]
        branches = ((ym_ref, gm_ref, mm_ref, dym_ref, dgm_ref, dmm_ref), (yf_ref, gf_ref, mf_ref, dyf_ref, dgf_ref, dmf_ref),
                    (ys_ref, gs_ref, ms_ref, dys_ref, dgs_ref, dms_ref))
        acts, outs, sigs = [], [], []
        merged = jnp.zeros((TOK, D_MODEL), f32)
        for b, (y_ref, g_ref, m_ref, _, _, _) in enumerate(branches):
            g = g_ref[...]
            a = (y_ref[...] * (g * _sigmoid(g))).astype(bf16)
            o = _nn(a, wo_ref[b * 512:(b + 1) * 512, :])
            s = _sigmoid(m_ref[...])
            merged = merged + s * o
            acts.append(a)
            outs.append(o)
            sigs.append(s)
        dmerged = _nt(do, wout_ref[...])
        _accumulate(dwout_ref, _tn(merged, do))
        dwo = []
        for b, (y_ref, g_ref, m_ref, dy_ref, dg_ref, dm_ref) in enumerate(branches):
            s, o = sigs[b], outs[b]
            dm_ref[...] = (dmerged * o * s * (1.0 - s)).astype(bf16)
            d_o = dmerged * s
            da = _nt(d_o, wo_ref[b * 512:(b + 1) * 512, :])
            dwo.append(_tn(acts[b], d_o))
            g = g_ref[...]
            sg = _sigmoid(g)
            dy_ref[...] = da * (g * sg)
            dg_ref[...] = (da * y_ref[...] * (sg * (1.0 + g * (1.0 - sg)))).astype(bf16)
        _accumulate(dwo_ref, jnp.concatenate(dwo, axis=0))

    (dy_mla, dy_fox, dy_s5, dg_mla, dg_fox, dg_s5, dm_mla, dm_fox, dm_s5, dwo, dwout) = _rowwise(
        "merge_bwd", merge_bwd_body, n_tok,
        [(sv["y_mla"], 512, 0), (sv["y_fox"], 512, 0), (sv["y_s5"], 512, 0), (proj, *_seg("g_mla")), (proj, *_seg("g_fox")),
         (proj, *_seg("g_s5")), (proj, *_seg("m_mla")), (proj, *_seg("m_fox")), (proj, *_seg("m_s5")), (dout, D_MODEL, 0)],
        [w["wo"], w["w_out"]],
        [(512, f32)] * 3 + [(512, bf16)] * 3 + [(D_MODEL, bf16)] * 3, [((1536, D_MODEL), f32), ((D_MODEL, D_MODEL), f32)])
    grads["w_branch_out"], grads["w_out"] = dwo, dwout

    u_w, u_blk = _seg("s5u")

    def s5_post_bwd_body(yl_ref, u_ref, do_ref, d_ref, wg_ref, bg_ref, dyl_ref, dus_ref, dd_ref, dwg_ref, dbg_ref):
        u = u_ref[...]
        y = yl_ref[...] + d_ref[...] * u
        z, t = _gelu(y)
        s = _sigmoid(_nn(z, wg_ref[...]) + bg_ref[...])
        do = do_ref[...]
        dgl = do * z * s * (1.0 - s)
        dz = do * s + _nt(dgl, wg_ref[...])
        dy = dz * _gelu_grad(y, t)
        dyl_ref[...] = dy.astype(bf16)
        dus_ref[...] = dy * d_ref[...]
        _accumulate(dd_ref, jnp.sum(dy * u, axis=0, keepdims=True))
        _accumulate(dwg_ref, _tn(z, dgl))
        _accumulate(dbg_ref, jnp.sum(dgl, axis=0, keepdims=True))

    dylin, du_skip, dd, dwglu, dbglu = _rowwise(
        "s5_post_bwd", s5_post_bwd_body, n_tok, [(sv["ylin"], 512, 0), (proj, u_w, u_blk), (dy_s5, 512, 0)],
        [w["s5_d"], w["w_glu"], w["b_glu"]], [(512, bf16), (512, f32)], [((1, 512), f32), ((512, 512), f32), ((1, 512), f32)])
    grads["s5_d"], grads["s5_w_glu"], grads["s5_b_glu"] = dd.reshape(512), dwglu, dbglu.reshape(512)

    dylin = _to_streams(dylin)
    dxs = _s5_expand("s5_dxs", dylin, 0, sv["c_cn"], n_tok=n_tok)
    dc_nc = _s5_wgrad_states("s5_dc", sv["xs"], dylin, n_tok=n_tok)
    gx = _s5_scan("s5_scan_bwd", dxs, sv["a_re8"], -sv["a_im8"], reverse=True, n_tok=n_tok)
    da = _s5_da(sv["xs"], gx, n_tok=n_tok)
    ds5u = _from_streams(_s5_contract("s5_du", gx, sv["b_nc"], _to_streams(du_skip), bf16, n_tok=n_tok))
    db_cn = _s5_wgrad_channels("s5_db", sv["u_streams"], 0, gx, n_tok=n_tok)
    diag_b = lambda m: jnp.swapaxes(_bd8_diag(m, S5_GROUP, S5_STATE), 1, 2).reshape(S5_LANES, S5_GROUP)
    diag_c = lambda m: jnp.swapaxes(_bd8_diag(m, S5_STATE, S5_GROUP), 1, 2)
    dlr, dli, dldt, db_re, db_im = _s5_param_bwd(
        w["lr"], w["li"], w["ldt"], w["b_re"], w["b_im"], da[:, 0, :].reshape(S5_LANES, 1), da[:, 1, :].reshape(S5_LANES, 1),
        diag_b(db_cn[0]), diag_b(db_cn[1]))
    grads["s5_lambda_re"] = dlr.reshape(S5_GROUPS, S5_STATE)
    grads["s5_lambda_im"] = dli.reshape(S5_GROUPS, S5_STATE)
    grads["s5_log_dt"] = dldt.reshape(S5_GROUPS)
    grads["s5_b_re"] = db_re.reshape(S5_GROUPS, S5_STATE, S5_GROUP)
    grads["s5_b_im"] = db_im.reshape(S5_GROUPS, S5_STATE, S5_GROUP)
    grads["s5_c_re"] = diag_c(dc_nc[0])
    grads["s5_c_im"] = -diag_c(dc_nc[1])

    dfqn, dfkn, dfv, dck, dcq = _attn_bwd("fox_attn_bwd", sv["fqn"], sv["fkn"], sv["fvb"], sv["y_fox"], sv["lse_fox"], dy_fox,
                                          sv["cum_b"], mla=False, n_tok=n_tok)
    dcq = jnp.pad(dcq[:, :2, :].reshape(HEADS, n_tok).T, ((0, 0), (0, LANE - HEADS)))

    def fox_gate_bwd_body(dk_ref, dq_ref, ff_ref, bf_ref, dff_ref, dbf_ref):
        xg = dk_ref[...] + dq_ref[...]
        t = lax.broadcasted_iota(jnp.int32, xg.shape, 0)
        s = 1
        while s < n_tok:
            xg = xg + jnp.where(t < n_tok - s, pltpu.roll(xg, n_tok - s, 0), 0.0)
            s *= 2
        dff = xg * _sigmoid(-(ff_ref[...] + bf_ref[...]))
        dff_ref[...] = dff.astype(bf16)
        dbf_ref[...] = jnp.sum(dff, axis=0, keepdims=True)

    ff_w, ff_blk = _seg("ff")
    dff, dbf = pl.pallas_call(
        fox_gate_bwd_body, grid=(1,),
        in_specs=[pl.BlockSpec((n_tok, LANE), lambda i: (0, 0)), pl.BlockSpec((n_tok, LANE), lambda i: (0, 0)),
                  pl.BlockSpec((n_tok, ff_w), lambda i: (0, ff_blk)), pl.BlockSpec((1, LANE), lambda i: (0, 0))],
        out_specs=[pl.BlockSpec((n_tok, LANE), lambda i: (0, 0)), pl.BlockSpec((1, LANE), lambda i: (0, 0))],
        out_shape=[jax.ShapeDtypeStruct((n_tok, LANE), bf16), jax.ShapeDtypeStruct((1, LANE), f32)], name="fox_gate_bwd",
    )(dck, dcq, proj, w["b_f"])
    grads["fox_b_f"] = dbf[0, :HEADS]

    def fox_prep_bwd_body(fq_ref, fk_ref, dqn_ref, dkn_ref, dv_ref, qg_ref, kg_ref, dfq_ref, dfk_ref, dfv_ref, dqg_ref, dkg_ref):
        lane = lax.broadcasted_iota(jnp.int32, (TOK, LANE), 1)
        dqg = jnp.zeros((1, LANE), f32)
        dkg = jnp.zeros((1, LANE), f32)
        for blk in range(4):
            sl = slice(blk * LANE, (blk + 1) * LANE)
            xq = fq_ref[:, sl]
            dx, dg = _fox_halves_bwd(dqn_ref[:, sl], xq, _fox_halves(xq, lane), qg_ref[...], lane)
            dfq_ref[:, sl] = dx.astype(bf16)
            dqg = dqg + dg
            xk = fk_ref[:, sl]
            dx, dg = _fox_halves_bwd(dkn_ref[:, sl], xk, _fox_halves(xk, lane), kg_ref[...], lane)
            dfk_ref[:, sl] = dx.astype(bf16)
            dkg = dkg + dg
        dfv_ref[...] = dv_ref[...].astype(bf16)
        _accumulate(dqg_ref, dqg + pltpu.roll(dqg, 64, 1))
        _accumulate(dkg_ref, dkg + pltpu.roll(dkg, 64, 1))

    dfq, dfk, dfvb, dfqg, dfkg = _rowwise(
        "fox_prep_bwd", fox_prep_bwd_body, n_tok,
        [(proj, *_seg("fq")), (proj, *_seg("fk")), (dfqn, 512, 0), (dfkn, 512, 0), (dfv, 512, 0)],
        [w["fq_g"], w["fk_g"]], [(512, bf16)] * 3, [((1, LANE), f32)] * 2)
    grads["fox_q_norm"], grads["fox_k_norm"] = dfqg[0, :FOX_DIM], dfkg[0, :FOX_DIM]

    dqn, dkn, dv_mla = _attn_bwd("mla_attn_bwd", sv["qn"], sv["kn"], sv["v_mla"], sv["y_mla"], sv["lse_mla"], dy_mla,
                                 None, mla=True, n_tok=n_tok)

    def mla_prep_bwd_body(cq_ref, ckv_ref, kpe_ref, c_ref, s1_ref, s2_ref, dqn_ref, dkn_ref, dv_ref,
                          qa_ref, kva_ref, wq_ref, wkv_ref, qn_g_ref, kn_g_ref,
                          dcq_ref, dckv_ref, dkpe_ref, dwq_ref, dwkv_ref, dqa_ref, dkva_ref, dqng_ref, dkng_ref):
        c, s1, s2 = c_ref[...], s1_ref[...], s2_ref[...]
        cq, ckv = cq_ref[...], ckv_ref[...]
        cqn_b, r_cq, ckvn_b, r_ckv, q_raw, kv_raw, kpe_rot = _mla_recompute(
            cq, ckv, kpe_ref[...], c, s1, s2, qa_ref[...], kva_ref[...], wq_ref[...], wkv_ref[...])
        lane = lax.broadcasted_iota(jnp.int32, (TOK, LANE), 1)
        dq_raw, dk_raw = [], []
        dkpe_rot = jnp.zeros((TOK, LANE), f32)
        dqng = jnp.zeros((1, LANE), f32)
        dkng = jnp.zeros((1, LANE), f32)
        for hd in range(HEADS):
            sl = slice(hd * LANE, (hd + 1) * LANE)
            q_rot = _rope(q_raw[:, sl], c, s1, s2)
            r = lax.rsqrt(jnp.sum(q_rot * q_rot, axis=-1, keepdims=True) * (1.0 / MLA_QK) + EPS)
            dx, dg = _rms_bwd(dqn_ref[:, sl], q_rot, r, qn_g_ref[...], MLA_QK)
            dqng = dqng + dg
            dq_raw.append(_rope_t(dx, c, s1, s2))
            k_full = kv_raw[:, sl] + kpe_rot
            r = lax.rsqrt(jnp.sum(k_full * k_full, axis=-1, keepdims=True) * (1.0 / MLA_QK) + EPS)
            dx, dg = _rms_bwd(dkn_ref[:, sl], k_full, r, kn_g_ref[...], MLA_QK)
            dkng = dkng + dg
            dk_raw.append(jnp.where(lane < 64, dx, 0.0))
            dkpe_rot = dkpe_rot + dx
        dkpe = _rope_t(dkpe_rot, c, s1, s2)
        dkpe_ref[...] = jnp.where(jnp.logical_and(lane >= 64, lane < 64 + ROPE), dkpe, 0.0).astype(bf16)
        dq_raw = jnp.concatenate(dq_raw, axis=1).astype(bf16)
        dkv_raw = jnp.concatenate(dk_raw + [dv_ref[...]], axis=1).astype(bf16)
        dcqn = _nt(dq_raw, wq_ref[...])
        dckvn = _nt(dkv_raw, wkv_ref[...])
        dx, dg = _rms_bwd(dcqn, cq, r_cq, qa_ref[...], MLA_Q_RANK)
        dcq_ref[...] = dx.astype(bf16)
        _accumulate(dqa_ref, dg)
        dx, dg = _rms_bwd(dckvn, ckv, r_ckv, kva_ref[...], MLA_KV_RANK)
        dckv_ref[...] = dx.astype(bf16)
        _accumulate(dkva_ref, dg)
        _accumulate(dwq_ref, _tn(cqn_b, dq_raw))
        _accumulate(dwkv_ref, _tn(ckvn_b, dkv_raw))
        _accumulate(dqng_ref, dqng)
        _accumulate(dkng_ref, dkng)

    dcq, dckv, dkpe, dwq, dwkv, dqa, dkva, dqng, dkng = _rowwise(
        "mla_prep_bwd", mla_prep_bwd_body, n_tok,
        [(proj, *_seg("cq")), (proj, *_seg("ckv")), (proj, *_seg("kpe")), (c_tab, LANE, 0), (s1_tab, LANE, 0), (s2_tab, LANE, 0),
         (dqn, HEADS * LANE, 0), (dkn, HEADS * LANE, 0), (dv_mla, 512, 0)],
        [w["qa_g"], w["kva_g"], w["wq"], w["wkv"], w["qn_g"], w["kn_g"]],
        [(MLA_Q_RANK, bf16), (LANE, bf16), (LANE, bf16)],
        [((MLA_Q_RANK, HEADS * LANE), f32), ((MLA_KV_RANK, HEADS * LANE + 512), f32), ((1, MLA_Q_RANK), f32),
         ((1, MLA_KV_RANK), f32), ((1, LANE), f32), ((1, LANE), f32)])
    grads["mla_w_q_up"] = dwq.reshape(MLA_Q_RANK, HEADS, LANE)[:, :, :MLA_QK].reshape(MLA_Q_RANK, HEADS * MLA_QK)
    dwk = dwkv[:, :HEADS * LANE].reshape(MLA_KV_RANK, HEADS, LANE)[:, :, :64]
    dwv = dwkv[:, HEADS * LANE:].reshape(MLA_KV_RANK, HEADS, 64)
    grads["mla_w_kv_up"] = jnp.concatenate([dwk, dwv], axis=2).reshape(MLA_KV_RANK, HEADS * 128)
    grads["mla_q_a_norm"], grads["mla_kv_a_norm"] = dqa.reshape(-1), dkva.reshape(-1)
    grads["mla_q_norm"], grads["mla_k_norm"] = dqng[0, :MLA_QK], dkng[0, :MLA_QK]

    _, _, shard_c, shard_cp = _BIG_SHARD["w_in"]
    kpe0 = _PAD["kpe"][2]
    dproj = jnp.concatenate([dcq, dckv, dkpe[:, kpe0:kpe0 + ROPE], dfq, dfk, dfvb, dff[:, :HEADS], ds5u, dg_mla, dg_fox, dg_s5,
                             dm_mla, dm_fox, dm_s5], axis=1)
    gap = jnp.zeros((n_tok, shard_cp - shard_c), bf16)
    dproj = jnp.concatenate([p for j in range(N_CHIPS) for p in (dproj[:, j * shard_c:(j + 1) * shard_c], gap)], axis=1)
    ct = 256
    per = shard_cp // ct
    dh = _mm("in_proj_dgrad", dproj, w["w_in_shards"], mode="nt", grid=(1, 1, N_CHIPS * per),
             a_spec=pl.BlockSpec((n_tok, ct), lambda i, j, kk: (0, kk)),
             b_spec=pl.BlockSpec((None, D_MODEL, ct), lambda i, j, kk: (kk // per, 0, kk % per)),
             o_spec=pl.BlockSpec((n_tok, D_MODEL), lambda i, j, kk: (0, 0)),
             out_shape=jax.ShapeDtypeStruct((n_tok, D_MODEL), f32), acc_shape=(n_tok, D_MODEL))
    grads["w_in"] = _mm("in_proj_wgrad", sv["h"], dproj, mode="tn", grid=(1, N_CHIPS * per, 1),
                        a_spec=pl.BlockSpec((n_tok, D_MODEL), lambda i, j, kk: (0, 0)),
                        b_spec=pl.BlockSpec((n_tok, ct), lambda i, j, kk: (0, j)),
                        o_spec=pl.BlockSpec((None, D_MODEL, ct), lambda i, j, kk: (j // per, 0, j % per)),
                        out_shape=jax.ShapeDtypeStruct((N_CHIPS, D_MODEL, shard_cp), f32), acc_shape=(D_MODEL, ct))

    def norm_bwd_body(dh_ref, x_ref, do_ref, g_ref, dx_ref, dg_ref):
        xv = x_ref[...]
        r = lax.rsqrt(jnp.sum(xv * xv, axis=-1, keepdims=True) * (1.0 / D_MODEL) + EPS)
        dx, dg = _rms_bwd(dh_ref[...], xv, r, g_ref[...], D_MODEL)
        dx_ref[...] = do_ref[...] + dx
        _accumulate(dg_ref, dg)

    dx, dng = _rowwise("norm_bwd", norm_bwd_body, n_tok, [(dh, D_MODEL, 0), (x, D_MODEL, 0), (dout, D_MODEL, 0)],
                       [w["norm_g"]], [(D_MODEL, f32)], [((1, D_MODEL), f32)])
    grads["norm_g"] = dng.reshape(D_MODEL)
    return dx, grads


def _rope_tables(positions):
    inv = 1.0 / (ROPE_THETA ** (jnp.arange(0, ROPE, 2, dtype=f32) / ROPE))
    ang = positions.astype(f32).reshape(-1, 1) * inv
    cos, sin = jnp.cos(ang), jnp.sin(ang)
    n = ang.shape[0]
    z16, z32, z64 = jnp.zeros((n, 16), f32), jnp.zeros((n, 32), f32), jnp.zeros((n, 64), f32)
    c = jnp.concatenate([jnp.ones((n, 64), f32), cos, cos, z32], axis=1)
    s1 = jnp.concatenate([z64, -sin, z16, z32], axis=1)
    s2 = jnp.concatenate([z64, z16, sin, z32], axis=1)
    return c, s1, s2


BIG = ("w_in", "mla_w_q_up", "mla_w_kv_up", "s5_w_glu", "w_branch_out", "w_out")
SMALL = ("norm_g", "mla_q_a_norm", "mla_kv_a_norm", "mla_q_norm", "mla_k_norm", "fox_b_f", "fox_q_norm", "fox_k_norm",
         "s5_lambda_re", "s5_lambda_im", "s5_log_dt", "s5_b_re", "s5_b_im", "s5_c_re", "s5_c_im", "s5_d", "s5_b_glu")
WEIGHTS = ("norm_g", "w_in", "mla_q_a_norm", "mla_w_q_up", "mla_kv_a_norm", "mla_w_kv_up", "mla_q_norm", "mla_k_norm",
           "fox_b_f", "fox_q_norm", "fox_k_norm", "s5_lambda_re", "s5_lambda_im", "s5_log_dt", "s5_b_re", "s5_b_im",
           "s5_c_re", "s5_c_im", "s5_d", "s5_w_glu", "s5_b_glu", "w_branch_out", "w_out")


def _local_step(x, positions, loss_target, small, big):
    n_tok = x.shape[0]
    tabs = _rope_tables(positions)
    ws, saves = [], []
    hcur = x
    for l in range(DEPTH):
        w = _layer_weights(l, small, big)
        hcur, sv = _layer_fwd(hcur, w, tabs, n_tok)
        ws.append(w)
        saves.append(sv)

    def loss_body(y_ref, t_ref, d_ref, l_ref):
        err = y_ref[...] - t_ref[...]
        d_ref[...] = err * (1.0 / D_MODEL)
        tot = jnp.sum(jnp.sum(err * err, axis=-1, keepdims=True), axis=0, keepdims=True)
        _accumulate(l_ref, jnp.broadcast_to(tot * (0.5 / D_MODEL), (1, LANE)))

    dcur, loss = _rowwise("loss", loss_body, n_tok, [(hcur, D_MODEL, 0), (loss_target, D_MODEL, 0)], [], [(D_MODEL, f32)],
                          [((1, LANE), f32)])
    layer_grads = [None] * DEPTH
    for l in reversed(range(DEPTH)):
        dcur, layer_grads[l] = _layer_bwd(dcur, ws[l], saves[l], tabs, n_tok)
    grads = {n: jnp.stack([layer_grads[l][n] for l in range(DEPTH)]) for n in WEIGHTS}
    return loss[0, 0], dcur, grads


N_DEV = 8
_ANY = pl.BlockSpec(memory_space=pl.ANY)
_MESH = pl.DeviceIdType.MESH


def _all_gather8(name, blk):
    m = blk.shape[0]

    def body(x_ref, out_ref, send_sems, recv_sems, local_sem):
        x, y, c = lax.axis_index("x"), lax.axis_index("y"), lax.axis_index("c")
        me, sibling = (x, y, c), (x, y, 1 - c)
        chips = [(1 - x, y), (x, 1 - y), (1 - x, 1 - y)]

        def slot(px, py, pc):
            return out_ref.at[4 * px + 2 * py + pc]

        def copy(k, block, to, src=None):
            return pltpu.make_async_remote_copy(
                src_ref=slot(*block) if src is None else src, dst_ref=slot(*block),
                send_sem=send_sems.at[k], recv_sem=recv_sems.at[k], device_id=to, device_id_type=_MESH)

        mine = pltpu.make_async_copy(x_ref, slot(*me), local_sem)
        mine.start()
        first = [copy(0, me, sibling, src=x_ref)]
        first += [copy(1 + j, me, (*chip, c), src=x_ref) for j, chip in enumerate(chips)]
        for cp in first:
            cp.start()
        passed = [copy(4 + j, (*chip, c), sibling) for j, chip in enumerate(chips)]
        for j, chip in enumerate(chips):
            copy(1 + j, (*chip, c), me).wait_recv()
            passed[j].start()
        copy(0, sibling, me).wait_recv()
        for j, chip in enumerate(chips):
            copy(4 + j, (*chip, 1 - c), me).wait_recv()
        for cp in first + passed:
            cp.wait_send()
        mine.wait()

    return pl.pallas_call(
        body, out_shape=jax.ShapeDtypeStruct((N_DEV, m, LANE), blk.dtype), in_specs=[_ANY], out_specs=_ANY, name=name,
        scratch_shapes=[pltpu.SemaphoreType.DMA((7,)), pltpu.SemaphoreType.DMA((7,)), pltpu.SemaphoreType.DMA],
    )(blk)


def _gather_layers(name, shards):
    n = len(shards)

    def body(*refs):
        x_refs, out_refs = refs[:n], refs[n:2 * n]
        send_sems, recv_sems, local_sems = refs[2 * n:]
        x, y, c = lax.axis_index("x"), lax.axis_index("y"), lax.axis_index("c")
        me, sibling = (x, y, c), (x, y, 1 - c)
        chips = [(1 - x, y), (x, 1 - y), (1 - x, 1 - y)]

        def copy(w, k, block, to, src=None):
            px, py, pc = block
            slot = out_refs[w].at[pc, 2 * px + py]
            return pltpu.make_async_remote_copy(
                src_ref=slot if src is None else src, dst_ref=slot, send_sem=send_sems.at[7 * w + k],
                recv_sem=recv_sems.at[7 * w + k], device_id=to, device_id_type=_MESH)

        started, local = [], []
        for w in range(n):
            src = x_refs[w].at[c]
            mine = pltpu.make_async_copy(src, out_refs[w].at[c, 2 * x + y], local_sems.at[w])
            mine.start()
            local.append(mine)
            first = [copy(w, 0, me, sibling, src=src)] + [copy(w, 1 + j, me, (*chip, c), src=src) for j, chip in enumerate(chips)]
            for cp in first:
                cp.start()
            started += first
        for w in range(n):
            for j, chip in enumerate(chips):
                copy(w, 1 + j, (*chip, c), me).wait_recv()
                onward = copy(w, 4 + j, (*chip, c), sibling)
                onward.start()
                started.append(onward)
        for w in range(n):
            copy(w, 0, sibling, me).wait_recv()
            for j, chip in enumerate(chips):
                copy(w, 4 + j, (*chip, 1 - c), me).wait_recv()
        for cp in started:
            cp.wait_send()
        for cp in local:
            cp.wait()

    return pl.pallas_call(
        body, out_shape=[jax.ShapeDtypeStruct((2, N_CHIPS) + s.shape[1:], s.dtype) for s in shards],
        in_specs=[_ANY] * n, out_specs=[_ANY] * n, name=name,
        scratch_shapes=[pltpu.SemaphoreType.DMA((7 * n,)), pltpu.SemaphoreType.DMA((7 * n,)), pltpu.SemaphoreType.DMA((n,))],
    )(*shards)


def _swap_layers(name, parts):
    n = len(parts)

    def body(*refs):
        p_refs, got_refs = refs[:n], refs[n:2 * n]
        send_sems, recv_sems = refs[2 * n:]
        x, y, c = lax.axis_index("x"), lax.axis_index("y"), lax.axis_index("c")
        copies = []
        for w in range(n):
            cp = pltpu.make_async_remote_copy(
                src_ref=p_refs[w].at[1 - c], dst_ref=got_refs[w], send_sem=send_sems.at[w], recv_sem=recv_sems.at[w],
                device_id=(x, y, 1 - c), device_id_type=_MESH)
            cp.start()
            copies.append(cp)
        for cp in copies:
            cp.wait()

    return pl.pallas_call(
        body, out_shape=[jax.ShapeDtypeStruct(p.shape[1:], p.dtype) for p in parts], in_specs=[_ANY] * n, out_specs=[_ANY] * n,
        name=name, scratch_shapes=[pltpu.SemaphoreType.DMA((n,)), pltpu.SemaphoreType.DMA((n,))],
    )(*parts)


def _scatter_to_chips(name, parts):
    n = len(parts)

    def body(*refs):
        p_refs, out_refs = refs[:n], refs[n:2 * n]
        send_sems, recv_sems, local_sems = refs[2 * n:]
        x, y, c = lax.axis_index("x"), lax.axis_index("y"), lax.axis_index("c")
        jme = 2 * x + y
        chips = [(1 - x, y), (x, 1 - y), (1 - x, 1 - y)]
        sends, local = [], []
        for w in range(n):
            mine = pltpu.make_async_copy(p_refs[w].at[jme], out_refs[w].at[jme], local_sems.at[w])
            mine.start()
            local.append(mine)
            for k, (tx, ty) in enumerate(chips):
                cp = pltpu.make_async_remote_copy(
                    src_ref=p_refs[w].at[2 * tx + ty], dst_ref=out_refs[w].at[jme], send_sem=send_sems.at[3 * w + k],
                    recv_sem=recv_sems.at[3 * w + k], device_id=(tx, ty, c), device_id_type=_MESH)
                cp.start()
                sends.append(cp)
        for w in range(n):
            for k, (tx, ty) in enumerate(chips):
                pltpu.make_async_remote_copy(
                    src_ref=p_refs[w].at[jme], dst_ref=out_refs[w].at[2 * tx + ty], send_sem=send_sems.at[3 * w + k],
                    recv_sem=recv_sems.at[3 * w + k], device_id=(tx, ty, c), device_id_type=_MESH).wait_recv()
        for cp in sends:
            cp.wait_send()
        for cp in local:
            cp.wait()

    return pl.pallas_call(
        body, out_shape=[jax.ShapeDtypeStruct(p.shape, p.dtype) for p in parts], in_specs=[_ANY] * n, out_specs=[_ANY] * n, name=name,
        scratch_shapes=[pltpu.SemaphoreType.DMA((3 * n,)), pltpu.SemaphoreType.DMA((3 * n,)), pltpu.SemaphoreType.DMA((n,))],
    )(*parts)


def _share_layers(name, bufs):
    n = len(bufs)

    def body(*refs):
        out_refs = refs[n:2 * n]
        send_sems, recv_sems = refs[2 * n:]
        x, y, c = lax.axis_index("x"), lax.axis_index("y"), lax.axis_index("c")
        copies = []
        for w in range(n):
            cp = pltpu.make_async_remote_copy(src_ref=out_refs[w].at[c], dst_ref=out_refs[w].at[c], send_sem=send_sems.at[w],
                                              recv_sem=recv_sems.at[w], device_id=(x, y, 1 - c), device_id_type=_MESH)
            cp.start()
            copies.append(cp)
        for w in range(n):
            pltpu.make_async_remote_copy(src_ref=out_refs[w].at[c], dst_ref=out_refs[w].at[1 - c], send_sem=send_sems.at[w],
                                         recv_sem=recv_sems.at[w], device_id=(x, y, 1 - c), device_id_type=_MESH).wait_recv()
        for cp in copies:
            cp.wait_send()

    return pl.pallas_call(
        body, out_shape=[jax.ShapeDtypeStruct(b.shape, b.dtype) for b in bufs], in_specs=[_ANY] * n, out_specs=[_ANY] * n,
        input_output_aliases={w: w for w in range(n)}, name=name,
        scratch_shapes=[pltpu.SemaphoreType.DMA((n,)), pltpu.SemaphoreType.DMA((n,))],
    )(*bufs)


def _row_tile(rows, cols):
    best = 16
    for t in range(16, rows + 1, 16):
        if rows % t == 0 and t * cols * 4 <= 2 * 1024 * 1024:
            best = t
    return best


def _add_pair(name, core, parts, got, out_dtype):
    _, _, r, c = parts.shape
    t = _row_tile(r, c)

    def body(core_ref, a_ref, b_ref, o_ref):
        o_ref[...] = (a_ref[...] + b_ref[...]).astype(o_ref.dtype)

    spec = pl.BlockSpec((None, t, c), lambda j, i, core_ref: (j, i, 0))
    grid_spec = pltpu.PrefetchScalarGridSpec(
        num_scalar_prefetch=1, grid=(N_CHIPS, r // t),
        in_specs=[pl.BlockSpec((None, None, t, c), lambda j, i, core_ref: (core_ref[0], j, i, 0)), spec], out_specs=spec)
    return pl.pallas_call(body, grid_spec=grid_spec, out_shape=jax.ShapeDtypeStruct(got.shape, out_dtype), name=name,
                          compiler_params=pltpu.CompilerParams(dimension_semantics=("arbitrary", "arbitrary")))(core, parts, got)


def _add_four(name, core, a):
    _, r, c = a.shape
    t = _row_tile(r, c)

    def body(core_ref, a0, a1, a2, a3, o_ref):
        o_ref[...] = ((a0[...].astype(f32) + a1[...].astype(f32)) + a2[...].astype(f32)) + a3[...].astype(f32)

    specs = [pl.BlockSpec((None, t, c), functools.partial(lambda i, core_ref, k: (k, i, 0), k=k)) for k in range(N_CHIPS)]
    grid_spec = pltpu.PrefetchScalarGridSpec(
        num_scalar_prefetch=1, grid=(r // t,), in_specs=specs,
        out_specs=pl.BlockSpec((None, t, c), lambda i, core_ref: (core_ref[0], i, 0)))
    return pl.pallas_call(body, grid_spec=grid_spec, out_shape=jax.ShapeDtypeStruct((2, r, c), f32), name=name,
                          compiler_params=pltpu.CompilerParams(dimension_semantics=("arbitrary",)))(core, a, a, a, a)


def _adamw(name, w, g, m, v, row_tile=None):
    c1 = 1.0 - ADAM_B1 ** ADAM_STEP
    c2 = 1.0 - ADAM_B2 ** ADAM_STEP

    def body(w_ref, g_ref, m_ref, v_ref, d_ref, nm_ref, nv_ref):
        gv = g_ref[...]
        nm = ADAM_B1 * m_ref[...] + (1.0 - ADAM_B1) * gv
        nv = ADAM_B2 * v_ref[...] + (1.0 - ADAM_B2) * (gv * gv)
        m_hat = nm / c1
        v_hat = nv / c2
        d_ref[...] = -ADAM_LR * (m_hat / (jnp.sqrt(v_hat) + ADAM_EPS) + ADAM_WD * w_ref[...])
        nm_ref[...] = nm
        nv_ref[...] = nv

    sds = jax.ShapeDtypeStruct(w.shape, f32)
    if row_tile is None:
        return pl.pallas_call(body, out_shape=[sds] * 3, name=name)(w, g, m, v)
    _, r, c = w.shape
    spec = pl.BlockSpec((None, row_tile, c), lambda l, i: (l, i, 0))
    return pl.pallas_call(body, grid=(DEPTH, r // row_tile), in_specs=[spec] * 4, out_specs=[spec] * 3, out_shape=[sds] * 3, name=name,
                          compiler_params=pltpu.CompilerParams(dimension_semantics=("arbitrary", "arbitrary"), vmem_limit_bytes=VMEM_LIMIT),
                          )(w, g, m, v)


def _pad_rows(flat, rows):
    return jnp.pad(flat, (0, rows * LANE - flat.shape[0])).reshape(rows, LANE)


def kernel(x, positions, norm_g, w_in, mla_q_a_norm, mla_w_q_up, mla_kv_a_norm, mla_w_kv_up, mla_q_norm, mla_k_norm, fox_b_f, fox_q_norm, fox_k_norm, s5_lambda_re, s5_lambda_im, s5_log_dt, s5_b_re, s5_b_im, s5_c_re, s5_c_im, s5_d, s5_w_glu, s5_b_glu, w_branch_out, w_out, loss_target, m_norm_g, m_w_in, m_mla_q_a_norm, m_mla_w_q_up, m_mla_kv_a_norm, m_mla_w_kv_up, m_mla_q_norm, m_mla_k_norm, m_fox_b_f, m_fox_q_norm, m_fox_k_norm, m_s5_lambda_re, m_s5_lambda_im, m_s5_log_dt, m_s5_b_re, m_s5_b_im, m_s5_c_re, m_s5_c_im, m_s5_d, m_s5_w_glu, m_s5_b_glu, m_w_branch_out, m_w_out, v_norm_g, v_w_in, v_mla_q_a_norm, v_mla_w_q_up, v_mla_kv_a_norm, v_mla_w_kv_up, v_mla_q_norm, v_mla_k_norm, v_fox_b_f, v_fox_q_norm, v_fox_k_norm, v_s5_lambda_re, v_s5_lambda_im, v_s5_log_dt, v_s5_b_re, v_s5_b_im, v_s5_c_re, v_s5_c_im, v_s5_d, v_s5_w_glu, v_s5_b_glu, v_w_branch_out, v_w_out):
    given = dict(locals())
    wts = {n: given[n] for n in WEIGHTS}
    mom1 = {n: given["m_" + n] for n in WEIGHTS}
    mom2 = {n: given["v_" + n] for n in WEIGHTS}

    def lanes(n, a):
        _, _, c, cp = _BIG_SHARD[n]
        return jnp.pad(a, ((0, 0), (0, 0), (0, cp - c)))

    gathered = _gather_layers("gather_weights", [lanes(n, wts[n].astype(bf16)) for n in BIG])
    big = dict(zip(BIG, gathered))
    small = {n: wts[n] for n in SMALL}

    loss_local, grad_x, grads = _local_step(x[0], positions, loss_target[0], small, big)
    loss = lax.psum(loss_local, ("x", "y", "c"))

    small_flat = jnp.concatenate([grads[n].reshape(-1) for n in SMALL])
    small_rows = -(-small_flat.shape[0] // (N_DEV * 16 * LANE)) * 16
    parts = [grads[n] if n == "w_in" else jnp.stack([_to_shards(n, grads[n][l]) for l in range(DEPTH)]) for n in BIG]
    parts.append(jnp.swapaxes(_pad_rows(small_flat, N_DEV * small_rows).reshape(N_CHIPS, 2, small_rows, LANE), 0, 1))
    core = lax.axis_index("c")
    core1 = core.reshape(1).astype(jnp.int32)
    got = _swap_layers("grads_to_sibling", parts)
    hop = [bf16] * len(BIG) + [f32]
    pair = [_add_pair("grads_pair_sum_%d" % i, core1, a, b, dt) for i, (a, b, dt) in enumerate(zip(parts, got, hop))]
    landed = _scatter_to_chips("grads_to_chips", pair)
    total = [_add_four("grads_chip_sum_%d" % i, core1, a) for i, a in enumerate(landed)]
    shared = _share_layers("grads_share", total[:-1])
    small_mine = lax.dynamic_index_in_dim(total[-1], core, 0, keepdims=False)
    small_all = _all_gather8("gather_small_grads", small_mine).reshape(-1)

    g_out = {n: s[:, :, :_BIG_SHARD[n][2]] for n, s in zip(BIG, shared)}
    pos = 0
    for n in SMALL:
        g_out[n] = small_all[pos:pos + wts[n].size].reshape(wts[n].shape)
        pos += wts[n].size

    delta, new_m, new_v = {}, {}, {}
    for n in WEIGHTS:
        row_tile = _row_tile(*wts[n].shape[1:]) if n in BIG else None
        delta[n], new_m[n], new_v[n] = _adamw("adamw_" + n, wts[n], g_out[n], mom1[n], mom2[n], row_tile)

    return (loss, grad_x[None], *[g_out[n] for n in WEIGHTS], *[delta[n] for n in WEIGHTS],
            *[new_m[n] for n in WEIGHTS], *[new_v[n] for n in WEIGHTS])
```

```python
import functools
import math

import jax
import jax.numpy as jnp
from jax import lax
from jax.experimental import pallas as pl
from jax.experimental.pallas import tpu as pltpu

f32 = jnp.float32
bf16 = jnp.bfloat16

D_MODEL = 1024
DEPTH = 2
EPS = 1e-6
HEADS = 8
MLA_QK = 96
MLA_Q_RANK = 256
MLA_KV_RANK = 128
ROPE = 32
ROPE_THETA = 10000.0
FOX_DIM = 64
S5_GROUPS = 32
S5_GROUP = 16
S5_STATE = 64
S5_LANES = S5_GROUPS * S5_STATE
LANE = 128
S5_BLOCKS = S5_LANES // LANE
IN_WIDTH = 7080
TOK = 256
VMEM_LIMIT = 56 * 1024 * 1024

ADAM_LR = 0.001
ADAM_B1 = 0.9
ADAM_B2 = 0.999
ADAM_EPS = 1e-08
ADAM_WD = 0.01
ADAM_STEP = 10

_ORIG = {}
_off = 0
for _n, _w in (("cq", 256), ("ckv", 128), ("kpe", 32), ("fq", 512), ("fk", 512), ("fv", 512), ("ff", 8), ("s5u", 512),
               ("g_mla", 512), ("g_fox", 512), ("g_s5", 512), ("m_mla", 1024), ("m_fox", 1024), ("m_s5", 1024)):
    _ORIG[_n] = (_off, _w)
    _off += _w
_PAD = {"m_mla": (0, 1024, 0), "m_fox": (1024, 1024, 0), "m_s5": (2048, 1024, 0),
        "fq": (3072, 512, 0), "fk": (3584, 512, 0), "fv": (4096, 512, 0), "s5u": (4608, 512, 0),
        "g_mla": (5120, 512, 0), "g_fox": (5632, 512, 0), "g_s5": (6144, 512, 0),
        "cq": (6656, 256, 0), "ckv": (6912, 128, 0), "kpe": (7040, 128, 64), "ff": (7168, 128, 0)}
NP = 7680
_PAD_ORDER = ("m_mla", "m_fox", "m_s5", "fq", "fk", "fv", "s5u", "g_mla", "g_fox", "g_s5", "cq", "ckv", "kpe", "ff")


def _seg(name):
    start, width, _ = _PAD[name]
    return width, start // width


def _nn(a, b):
    return lax.dot_general(a.astype(bf16), b.astype(bf16), (((1,), (0,)), ((), ())), preferred_element_type=f32)


def _nt(a, b):
    return lax.dot_general(a.astype(bf16), b.astype(bf16), (((1,), (1,)), ((), ())), preferred_element_type=f32)


def _tn(a, b):
    return lax.dot_general(a.astype(bf16), b.astype(bf16), (((0,), (0,)), ((), ())), preferred_element_type=f32)


def _rms(x, g, n):
    r = lax.rsqrt(jnp.sum(x * x, axis=-1, keepdims=True) * (1.0 / n) + EPS)
    return x * r * g, r


def _rms_bwd(dy, x, r, g, n):
    xh = x * r
    dg = jnp.sum(dy * xh, axis=0, keepdims=True)
    dxh = dy * g
    dx = r * (dxh - xh * (jnp.sum(dxh * xh, axis=-1, keepdims=True) * (1.0 / n)))
    return dx, dg


def _sigmoid(x):
    return 1.0 / (1.0 + jnp.exp(-x))


_GELU_C = math.sqrt(2.0 / math.pi)


def _gelu(x):
    t = jnp.tanh(_GELU_C * (x + 0.044715 * x * x * x))
    return 0.5 * x * (1.0 + t), t


def _gelu_grad(x, t):
    return 0.5 * (1.0 + t) + 0.5 * x * (1.0 - t * t) * _GELU_C * (1.0 + 3.0 * 0.044715 * x * x)


def _accumulate(ref, val):
    i = pl.program_id(0)

    @pl.when(i == 0)
    def _():
        ref[...] = val

    @pl.when(i > 0)
    def _():
        ref[...] += val


def _rope(x, c, s1, s2):
    return x * c + pltpu.roll(x, LANE - 16, 1) * s1 + pltpu.roll(x, 16, 1) * s2


def _rope_t(d, c, s1, s2):
    return d * c + pltpu.roll(d * s1, 16, 1) + pltpu.roll(d * s2, LANE - 16, 1)


def _const_map(ndim):
    return lambda *_: (0,) * ndim


def _rowwise(name, body, n_tok, tiled_in, full_in, tiled_out, acc_out, tile=TOK):
    in_specs, args = [], []
    for arr, width, blk in tiled_in:
        in_specs.append(pl.BlockSpec((tile, width), functools.partial(lambda i, b: (i, b), b=blk)))
        args.append(arr)
    for arr in full_in:
        in_specs.append(pl.BlockSpec(arr.shape, _const_map(arr.ndim)))
        args.append(arr)
    out_specs, out_shape = [], []
    for width, dt in tiled_out:
        out_specs.append(pl.BlockSpec((tile, width), lambda i: (i, 0)))
        out_shape.append(jax.ShapeDtypeStruct((n_tok, width), dt))
    for shape, dt in acc_out:
        out_specs.append(pl.BlockSpec(shape, _const_map(len(shape))))
        out_shape.append(jax.ShapeDtypeStruct(shape, dt))
    return pl.pallas_call(
        body, grid=(n_tok // tile,), in_specs=in_specs, out_specs=out_specs, out_shape=out_shape, name=name,
        compiler_params=pltpu.CompilerParams(dimension_semantics=("arbitrary",), vmem_limit_bytes=VMEM_LIMIT),
    )(*args)


def _mm(name, a, b, *, mode, grid, a_spec, b_spec, o_spec, out_shape, acc_shape, add=None, add_spec=None):
    nk = grid[2]

    def body(*refs):
        if add is None:
            a_ref, b_ref, o_ref, acc_ref = refs
        else:
            a_ref, b_ref, add_ref, o_ref, acc_ref = refs
        k = pl.program_id(2)

        @pl.when(k == 0)
        def _():
            acc_ref[...] = jnp.zeros_like(acc_ref)

        acc_ref[...] += {"nn": _nn, "nt": _nt, "tn": _tn}[mode](a_ref[...], b_ref[...])

        @pl.when(k == nk - 1)
        def _():
            r = acc_ref[...]
            if add is not None:
                r = r + add_ref[...]
            o_ref[...] = r.astype(o_ref.dtype)

    in_specs = [a_spec, b_spec] + ([add_spec] if add is not None else [])
    args = (a, b) + ((add,) if add is not None else ())
    return pl.pallas_call(
        body, grid=grid, in_specs=in_specs, out_specs=o_spec, out_shape=out_shape, name=name,
        scratch_shapes=[pltpu.VMEM(acc_shape, f32)],
        compiler_params=pltpu.CompilerParams(dimension_semantics=("arbitrary", "arbitrary", "arbitrary"), vmem_limit_bytes=VMEM_LIMIT),
    )(*args)


def _mm_nn(name, a, b, *, m, n, k, tm, tn, tk, out_dtype=f32, a_koff=0):
    return _mm(name, a, b, mode="nn", grid=(m // tm, n // tn, k // tk),
               a_spec=pl.BlockSpec((tm, tk), lambda i, j, kk: (i, kk + a_koff)),
               b_spec=pl.BlockSpec((tk, tn), lambda i, j, kk: (kk, j)),
               o_spec=pl.BlockSpec((tm, tn), lambda i, j, kk: (i, j)),
               out_shape=jax.ShapeDtypeStruct((m, n), out_dtype), acc_shape=(tm, tn))


def _mm_tn(name, a, b, *, m, n, k, tm, tn, tk, a_moff=0):
    return _mm(name, a, b, mode="tn", grid=(m // tm, n // tn, k // tk),
               a_spec=pl.BlockSpec((tk, tm), lambda i, j, kk: (kk, i + a_moff)),
               b_spec=pl.BlockSpec((tk, tn), lambda i, j, kk: (kk, j)),
               o_spec=pl.BlockSpec((tm, tn), lambda i, j, kk: (i, j)),
               out_shape=jax.ShapeDtypeStruct((m, n), f32), acc_shape=(tm, tn))


ATT_KV = 256
ATT_Q = 512


def _attn_common(mla, n_tok):
    qw = 2 * LANE if mla else LANE
    scale = 1.0 / math.sqrt(MLA_QK if mla else FOX_DIM)
    return qw, scale, min(ATT_Q, n_tok)


def _attn_heads(q_ref, mla):
    out = []
    if mla:
        for e in (0, 1):
            qe = q_ref[:, e * LANE:(e + 1) * LANE]
            out.append((qe.astype(f32).T.astype(bf16), qe))
        return out
    q = q_ref[...]
    tq = q.shape[0]
    qt = q.astype(f32).T
    row = lax.broadcasted_iota(jnp.int32, (LANE, tq), 0)
    lane = lax.broadcasted_iota(jnp.int32, (tq, LANE), 1)
    for e in (0, 1):
        out.append((jnp.where((row >= 64) == bool(e), qt, 0.0).astype(bf16),
                    jnp.where((lane >= 64) == bool(e), q, jnp.zeros((), bf16))))
    return out


def _attn_allowed(off, i, tq, mla):
    kpos = off + lax.broadcasted_iota(jnp.int32, (ATT_KV, tq), 0)
    qpos = i * tq + lax.broadcasted_iota(jnp.int32, (ATT_KV, tq), 1)
    return ((kpos // 64) <= (qpos // 64)) if mla else (kpos <= qpos)


def _attn_fwd(name, q, k, v, cum_b, *, mla, n_tok):
    qw, scale, tq = _attn_common(mla, n_tok)
    nq = n_tok // tq
    nkv = n_tok // ATT_KV
    has_bias = cum_b is not None

    def body(*refs):
        if has_bias:
            q_ref, k_ref, v_ref, cb_ref, o_ref, lse_ref, vt_ref = refs
        else:
            q_ref, k_ref, v_ref, o_ref, lse_ref, vt_ref = refs
        i = pl.program_id(1)

        @pl.when(i == 0)
        def _():
            for jb in range(nkv):
                vt_ref[jb] = v_ref[jb * ATT_KV:(jb + 1) * ATT_KV, :].astype(f32).T.astype(bf16)

        heads = _attn_heads(q_ref, mla)

        def step(j, carry, masked):
            off = pl.multiple_of(j * ATT_KV, ATT_KV)
            allowed = _attn_allowed(off, i, tq, mla) if masked else None
            vt = vt_ref[j]
            sts = []
            for e in (0, 1):
                kb = k_ref[pl.ds(off, ATT_KV), e * LANE:(e + 1) * LANE] if mla else k_ref[pl.ds(off, ATT_KV), :]
                sts.append(_nn(kb, heads[e][0]))
            stats = []
            for e in (0, 1):
                m, l, _ = carry[e]
                st = sts[e] * scale
                if has_bias:
                    st = st - jnp.tile(cb_ref[e, pl.ds(off, ATT_KV), :], (1, tq // LANE))
                if masked:
                    st = jnp.where(allowed, st, -1e30)
                m_new = jnp.maximum(m, jnp.max(st, axis=0, keepdims=True))
                alpha = jnp.exp(m - m_new)
                pt = jnp.exp(st - m_new)
                stats.append((m_new, alpha * l + jnp.sum(pt, axis=0, keepdims=True), alpha, pt.astype(bf16)))
            new = []
            for e in (0, 1):
                m_new, l, alpha, pt = stats[e]
                new.append((m_new, l, alpha * carry[e][2] + _nn(vt[64 * e:64 * e + 64, :], pt)))
            return tuple(new)

        init = tuple((jnp.full((1, tq), -1e30, f32), jnp.zeros((1, tq), f32), jnp.zeros((64, tq), f32)) for _ in (0, 1))
        n_full = i * (tq // ATT_KV)
        carry = lax.fori_loop(0, n_full, functools.partial(step, masked=False), init)
        for d in range(tq // ATT_KV):
            carry = step(n_full + d, carry, True)
        o_ref[...] = jnp.concatenate([carry[e][2] / carry[e][1] for e in (0, 1)], axis=0).T
        lse_ref[...] = jnp.zeros_like(lse_ref)
        for e in (0, 1):
            lse_ref[e:e + 1, :] = carry[e][0] + jnp.log(carry[e][1])

    in_specs = [pl.BlockSpec((tq, qw), lambda p, i: (i, p)),
                pl.BlockSpec((n_tok, qw), lambda p, i: (0, p)),
                pl.BlockSpec((n_tok, LANE), lambda p, i: (0, p))]
    args = [q, k, v]
    if has_bias:
        in_specs.append(pl.BlockSpec((2, n_tok, LANE), lambda p, i: (p, 0, 0)))
        args.append(cum_b)
    return pl.pallas_call(
        body, grid=(4, nq), in_specs=in_specs,
        out_specs=[pl.BlockSpec((tq, LANE), lambda p, i: (i, p)), pl.BlockSpec((None, 8, tq), lambda p, i: (p, 0, i))],
        out_shape=[jax.ShapeDtypeStruct((n_tok, 512), f32), jax.ShapeDtypeStruct((4, 8, n_tok), f32)], name=name,
        scratch_shapes=[pltpu.VMEM((nkv, LANE, ATT_KV), bf16)],
        compiler_params=pltpu.CompilerParams(dimension_semantics=("arbitrary", "arbitrary"), vmem_limit_bytes=VMEM_LIMIT),
    )(*args)


def _attn_bwd(name, q, k, v, o, lse, do, cum_b, *, mla, n_tok):
    qw, scale, tq = _attn_common(mla, n_tok)
    nq = n_tok // tq
    nkv = n_tok // ATT_KV
    has_bias = cum_b is not None

    def body(*refs):
        if has_bias:
            q_ref, k_ref, v_ref, o_ref, lse_ref, do_ref, cb_ref, dq_ref, dk_ref, dv_ref, dck_ref, dcq_ref, kt_ref = refs
        else:
            q_ref, k_ref, v_ref, o_ref, lse_ref, do_ref, dq_ref, dk_ref, dv_ref, kt_ref = refs
        p = pl.program_id(0)
        i = pl.program_id(1)

        @pl.when(i == 0)
        def _():
            dk_ref[...] = jnp.zeros_like(dk_ref)
            dv_ref[...] = jnp.zeros_like(dv_ref)
            for jb in range(nkv):
                for c0 in range(0, qw, LANE):
                    kt_ref[jb, c0:c0 + LANE, :] = k_ref[jb * ATT_KV:(jb + 1) * ATT_KV, c0:c0 + LANE].astype(f32).T.astype(bf16)

        if has_bias:
            @pl.when(jnp.logical_and(i == 0, p == 0))
            def _():
                dck_ref[...] = jnp.zeros_like(dck_ref)

        heads = _attn_heads(q_ref, mla)
        do = do_ref[...]
        do_t = do.T
        prod_t = (do * o_ref[...]).T
        row = lax.broadcasted_iota(jnp.int32, (LANE, tq), 0)
        lane = lax.broadcasted_iota(jnp.int32, (tq, LANE), 1)
        lane_k = lax.broadcasted_iota(jnp.int32, (ATT_KV, LANE), 1)
        per_head = []
        for e in (0, 1):
            sel_r = (row >= 64) == bool(e)
            per_head.append((jnp.where(sel_r, do_t, 0.0).astype(bf16),
                             jnp.where((lane >= 64) == bool(e), do, 0.0).astype(bf16),
                             jnp.sum(jnp.where(sel_r, prod_t, 0.0), axis=0, keepdims=True),
                             lse_ref[e:e + 1, :]))
        dq_rows = LANE if mla else 64

        def step(j, carry, masked):
            off = pl.multiple_of(j * ATT_KV, ATT_KV)
            allowed = _attn_allowed(off, i, tq, mla) if masked else None
            vb = v_ref[pl.ds(off, ATT_KV), :]
            kt = kt_ref[j]
            cols = [slice(e * LANE, (e + 1) * LANE) if mla else slice(None) for e in (0, 1)]
            sts = [_nn(k_ref[pl.ds(off, ATT_KV), cols[e]], heads[e][0]) for e in (0, 1)]
            dpts = [_nn(vb, per_head[e][0]) for e in (0, 1)]
            mids = []
            for e in (0, 1):
                _, _, delta, lse_e = per_head[e]
                st = sts[e] * scale
                if has_bias:
                    st = st - jnp.tile(cb_ref[e, pl.ds(off, ATT_KV), :], (1, tq // LANE))
                pt = jnp.exp(st - lse_e)
                if masked:
                    pt = jnp.where(allowed, pt, 0.0)
                dst = pt * (dpts[e] - delta)
                qsum = carry[e][1]
                if has_bias:
                    rs = jnp.sum(dst, axis=1, keepdims=True)
                    dck_ref[pl.ds(off, ATT_KV), :] += jnp.where(lane_k == 2 * p + e, -rs, 0.0)
                    qsum = qsum + jnp.sum(dst, axis=0, keepdims=True)
                mids.append((pt.astype(bf16), dst.astype(bf16), qsum))
            new = []
            for e in (0, 1):
                pt, dst, qsum = mids[e]
                kt_e = kt[e * LANE:(e + 1) * LANE, :] if mla else kt[64 * e:64 * e + 64, :]
                new.append((carry[e][0] + _nn(kt_e, dst) * scale, qsum))
                dk_ref[pl.ds(off, ATT_KV), cols[e]] += _nn(dst, heads[e][1]) * scale
                dv_ref[pl.ds(off, ATT_KV), :] += _nn(pt, per_head[e][1])
            return tuple(new)

        init = tuple((jnp.zeros((dq_rows, tq), f32), jnp.zeros((1, tq), f32)) for _ in (0, 1))
        n_full = i * (tq // ATT_KV)
        carry = lax.fori_loop(0, n_full, functools.partial(step, masked=False), init)
        for d in range(tq // ATT_KV):
            carry = step(n_full + d, carry, True)
        if mla:
            for e in (0, 1):
                dq_ref[:, e * LANE:(e + 1) * LANE] = carry[e][0].T
        else:
            dq_ref[...] = jnp.concatenate([carry[0][0], carry[1][0]], axis=0).T
        if has_bias:
            dcq_ref[...] = jnp.zeros_like(dcq_ref)
            for e in (0, 1):
                dcq_ref[e:e + 1, :] = carry[e][1]

    tile_q = pl.BlockSpec((tq, qw), lambda p, i: (i, p))
    tile_v = pl.BlockSpec((tq, LANE), lambda p, i: (i, p))
    full_k = pl.BlockSpec((n_tok, qw), lambda p, i: (0, p))
    full_v = pl.BlockSpec((n_tok, LANE), lambda p, i: (0, p))
    in_specs = [tile_q, full_k, full_v, tile_v, pl.BlockSpec((None, 8, tq), lambda p, i: (p, 0, i)), tile_v]
    args = [q, k, v, o, lse, do]
    out_specs = [tile_q, full_k, full_v]
    out_shape = [jax.ShapeDtypeStruct((n_tok, 4 * qw), f32), jax.ShapeDtypeStruct((n_tok, 4 * qw), f32),
                 jax.ShapeDtypeStruct((n_tok, 512), f32)]
    if has_bias:
        in_specs.append(pl.BlockSpec((2, n_tok, LANE), lambda p, i: (p, 0, 0)))
        args.append(cum_b)
        out_specs += [pl.BlockSpec((n_tok, LANE), _const_map(2)), pl.BlockSpec((None, 8, tq), lambda p, i: (p, 0, i))]
        out_shape += [jax.ShapeDtypeStruct((n_tok, LANE), f32), jax.ShapeDtypeStruct((4, 8, n_tok), f32)]
    return pl.pallas_call(
        body, grid=(4, nq), in_specs=in_specs, out_specs=out_specs, out_shape=out_shape, name=name,
        scratch_shapes=[pltpu.VMEM((nkv, qw, ATT_KV), bf16)],
        compiler_params=pltpu.CompilerParams(dimension_semantics=("arbitrary", "arbitrary"), vmem_limit_bytes=VMEM_LIMIT),
    )(*args)


def _s5_disc(lr, li, ldt):
    dt = jnp.exp(ldt)
    mag = jnp.exp(lr * dt)
    a_re = mag * jnp.cos(li * dt)
    a_im = mag * jnp.sin(li * dt)
    den = lr * lr + li * li
    f_re = ((a_re - 1.0) * lr + a_im * li) / den
    f_im = (a_im * lr - (a_re - 1.0) * li) / den
    return a_re, a_im, f_re, f_im


def _s5_param_fwd(lr, li, ldt, b_re, b_im):
    def body(lr_ref, li_ref, ldt_ref, br_ref, bi_ref, ar_ref, ai_ref, bbr_ref, bbi_ref):
        a_re, a_im, f_re, f_im = _s5_disc(lr_ref[...], li_ref[...], ldt_ref[...])
        ar_ref[...] = a_re
        ai_ref[...] = a_im
        br, bi = br_ref[...], bi_ref[...]
        bbr_ref[...] = f_re * br - f_im * bi
        bbi_ref[...] = f_re * bi + f_im * br

    col = jax.ShapeDtypeStruct(lr.shape, f32)
    mat = jax.ShapeDtypeStruct(b_re.shape, f32)
    return pl.pallas_call(body, out_shape=[col, col, mat, mat], name="s5_param_fwd")(lr, li, ldt, b_re, b_im)


def _s5_param_bwd(lr, li, ldt, b_re, b_im, da_re, da_im, dbb_re, dbb_im):
    def body(lr_ref, li_ref, ldt_ref, br_ref, bi_ref, dar_ref, dai_ref, gbr_ref, gbi_ref,
             dlr_ref, dli_ref, dldt_ref, dbr_ref, dbi_ref):
        (a_re, a_im, f_re, f_im), vjp = jax.vjp(_s5_disc, lr_ref[...], li_ref[...], ldt_ref[...])
        br, bi, gr, gi = br_ref[...], bi_ref[...], gbr_ref[...], gbi_ref[...]
        dbr_ref[...] = f_re * gr + f_im * gi
        dbi_ref[...] = f_re * gi - f_im * gr
        dfr = jnp.sum(br * gr + bi * gi, axis=-1, keepdims=True)
        dfi = jnp.sum(br * gi - bi * gr, axis=-1, keepdims=True)
        dlr, dli, dldt = vjp((dar_ref[...], dai_ref[...], dfr, dfi))
        dlr_ref[...] = dlr
        dli_ref[...] = dli
        dldt_ref[...] = jnp.sum(dldt.reshape(S5_GROUPS, S5_STATE, 1), axis=1)

    col = jax.ShapeDtypeStruct((S5_LANES, 1), f32)
    mat = jax.ShapeDtypeStruct((S5_LANES, S5_GROUP), f32)
    return pl.pallas_call(body, out_shape=[col, col, jax.ShapeDtypeStruct((S5_GROUPS, 1), f32), mat, mat],
                          name="s5_param_bwd")(lr, li, ldt, b_re, b_im, da_re, da_im, dbb_re, dbb_im)


_SCAN_NB = 4


def _to_streams(a):
    s, c = a.shape
    return jnp.swapaxes(a.reshape(8, s // 8, c), 0, 1).reshape(s, c)


def _from_streams(a):
    s, c = a.shape
    return jnp.swapaxes(a.reshape(s // 8, 8, c), 0, 1).reshape(s, c)


def _s5_scan(name, bu, a_re8, a_im8, *, reverse, n_tok):
    rows = n_tok // 8
    nb = _SCAN_NB

    def body(bu_ref, ar_ref, ai_ref, x_ref):
        a_r = [ar_ref[b] for b in range(nb)]
        a_i = [ai_ref[b] for b in range(nb)]
        zero = jnp.zeros((8, LANE), f32)
        one = jnp.ones((8, LANE), f32)

        def rows_at(r):
            rr = (rows - 1 - r) if reverse else r
            return pl.ds(pl.multiple_of(rr * 8, 8), 8)

        def pass1(r, carry):
            out = []
            sl = rows_at(r)
            for b in range(nb):
                xr, xi, mr, mi = carry[b]
                nr = a_r[b] * xr - a_i[b] * xi + bu_ref[0, b, sl, :]
                ni = a_r[b] * xi + a_i[b] * xr + bu_ref[1, b, sl, :]
                x_ref[0, b, sl, :] = nr
                x_ref[1, b, sl, :] = ni
                out.append((nr, ni, a_r[b] * mr - a_i[b] * mi, a_r[b] * mi + a_i[b] * mr))
            return tuple(out)

        carry = lax.fori_loop(0, rows, pass1, tuple((zero, zero, one, zero) for _ in range(nb)))
        sub = lax.broadcasted_iota(jnp.int32, (8, LANE), 0)
        feed = []
        for b in range(nb):
            lr_, li_, pr, pi = carry[b]
            fr, fi = zero, zero
            for _ in range(7):
                tr = lr_ + pr * fr - pi * fi
                ti = li_ + pr * fi + pi * fr
                if reverse:
                    fr = jnp.where(sub < 7, pltpu.roll(tr, 7, 0), 0.0)
                    fi = jnp.where(sub < 7, pltpu.roll(ti, 7, 0), 0.0)
                else:
                    fr = jnp.where(sub > 0, pltpu.roll(tr, 1, 0), 0.0)
                    fi = jnp.where(sub > 0, pltpu.roll(ti, 1, 0), 0.0)
            feed.append((fr, fi))

        def pass2(r, carry):
            out = []
            sl = rows_at(r)
            for b in range(nb):
                mr, mi = carry[b]
                fr, fi = feed[b]
                x_ref[0, b, sl, :] += mr * fr - mi * fi
                x_ref[1, b, sl, :] += mr * fi + mi * fr
                out.append((a_r[b] * mr - a_i[b] * mi, a_r[b] * mi + a_i[b] * mr))
            return tuple(out)

        lax.fori_loop(0, rows, pass2, tuple((a_r[b], a_i[b]) for b in range(nb)))

    blk = pl.BlockSpec((2, nb, n_tok, LANE), lambda g: (0, g, 0, 0))
    ablk = pl.BlockSpec((nb, 8, LANE), lambda g: (g, 0, 0))
    return pl.pallas_call(
        body, grid=(S5_BLOCKS // nb,), in_specs=[blk, ablk, ablk], out_specs=blk,
        out_shape=jax.ShapeDtypeStruct((2, S5_BLOCKS, n_tok, LANE), f32), name=name,
        compiler_params=pltpu.CompilerParams(dimension_semantics=("arbitrary",), vmem_limit_bytes=VMEM_LIMIT),
    )(bu, a_re8, a_im8)


def _s5_da(xs, gx, *, n_tok):
    def body(x_ref, g_ref, o_ref):
        t = lax.broadcasted_iota(jnp.int32, (n_tok, LANE), 0)
        sub = lax.broadcasted_iota(jnp.int32, (8, LANE), 0)

        def prev(v):
            return (jnp.where(t >= 8, pltpu.roll(v, 8, 0), 0.0),
                    jnp.where(sub > 0, pltpu.roll(v[n_tok - 8:, :], 1, 0), 0.0))

        (xr, hr), (xi, hi) = prev(x_ref[0, 0]), prev(x_ref[1, 0])
        gr, gi = g_ref[0, 0], g_ref[1, 0]
        gr0, gi0 = gr[0:8, :], gi[0:8, :]
        o_ref[0, 0:1, :] = (jnp.sum(xr * gr + xi * gi, axis=0, keepdims=True)
                            + jnp.sum(hr * gr0 + hi * gi0, axis=0, keepdims=True))
        o_ref[0, 1:2, :] = (jnp.sum(xr * gi - xi * gr, axis=0, keepdims=True)
                            + jnp.sum(hr * gi0 - hi * gr0, axis=0, keepdims=True))

    blk = pl.BlockSpec((2, 1, n_tok, LANE), lambda g: (0, g, 0, 0))
    return pl.pallas_call(
        body, grid=(S5_BLOCKS,), in_specs=[blk, blk], out_specs=pl.BlockSpec((1, 2, LANE), lambda g: (g, 0, 0)),
        out_shape=jax.ShapeDtypeStruct((S5_BLOCKS, 2, LANE), f32), name="s5_da",
        compiler_params=pltpu.CompilerParams(dimension_semantics=("arbitrary",), vmem_limit_bytes=VMEM_LIMIT),
    )(xs, gx)


S5_Q = 4


def _bd8(t):
    _, a, b = t.shape
    t = t.reshape(S5_Q, 8, a, 1, b)
    eye = jnp.eye(8, dtype=jnp.bool_).reshape(1, 8, 1, 8, 1)
    return jnp.where(eye, jnp.broadcast_to(t, (S5_Q, 8, a, 8, b)), jnp.zeros((), t.dtype)).reshape(S5_Q, 8 * a, 8 * b)


def _bd8_diag(m, a, b):
    m = m.reshape(S5_Q, 8, a, 8, b)
    eye = jnp.eye(8, dtype=jnp.bool_).reshape(1, 8, 1, 8, 1)
    return jnp.sum(jnp.where(eye, m, 0.0), axis=3).reshape(S5_GROUPS, a, b)


def _s5_expand(name, a, a_blk0, wq, *, n_tok):
    def body(a_ref, w_ref, o_ref):
        r = _nn(a_ref[...], w_ref[...])
        for k in range(4):
            o_ref[k] = r[:, k * LANE:(k + 1) * LANE]

    return pl.pallas_call(
        body, grid=(2, S5_Q),
        in_specs=[pl.BlockSpec((n_tok, LANE), lambda ri, q: (0, a_blk0 + q)),
                  pl.BlockSpec((None, None, LANE, 512), lambda ri, q: (ri, q, 0, 0))],
        out_specs=pl.BlockSpec((None, 4, n_tok, LANE), lambda ri, q: (ri, q, 0, 0)),
        out_shape=jax.ShapeDtypeStruct((2, S5_BLOCKS, n_tok, LANE), f32), name=name,
        compiler_params=pltpu.CompilerParams(dimension_semantics=("arbitrary", "arbitrary"), vmem_limit_bytes=VMEM_LIMIT),
    )(a, wq)


def _s5_contract(name, xs, wq, add, out_dtype, *, n_tok):
    def body(*refs):
        if add is None:
            x_ref, w_ref, o_ref, acc_ref = refs
        else:
            x_ref, w_ref, add_ref, o_ref, acc_ref = refs
        ri = pl.program_id(1)
        r = _nn(x_ref[0], w_ref[0:LANE, :])
        for k in range(1, 4):
            r = r + _nn(x_ref[k], w_ref[k * LANE:(k + 1) * LANE, :])

        @pl.when(ri == 0)
        def _():
            acc_ref[...] = r

        @pl.when(ri == 1)
        def _():
            tot = acc_ref[...] + r
            if add is not None:
                tot = tot + add_ref[...]
            o_ref[...] = tot.astype(o_ref.dtype)

    col = pl.BlockSpec((n_tok, LANE), lambda q, ri: (0, q))
    in_specs = [pl.BlockSpec((None, 4, n_tok, LANE), lambda q, ri: (ri, q, 0, 0)),
                pl.BlockSpec((None, None, 512, LANE), lambda q, ri: (ri, q, 0, 0))]
    args = [xs, wq]
    if add is not None:
        in_specs.append(col)
        args.append(add)
    return pl.pallas_call(
        body, grid=(S5_Q, 2), in_specs=in_specs, out_specs=col, out_shape=jax.ShapeDtypeStruct((n_tok, 512), out_dtype), name=name,
        scratch_shapes=[pltpu.VMEM((n_tok, LANE), f32)],
        compiler_params=pltpu.CompilerParams(dimension_semantics=("arbitrary", "arbitrary"), vmem_limit_bytes=VMEM_LIMIT),
    )(*args)


def _s5_wgrad_states(name, xs, d, *, n_tok):
    def body(x_ref, d_ref, o_ref):
        for k in range(4):
            o_ref[k * LANE:(k + 1) * LANE, :] = _tn(x_ref[k], d_ref[...])

    return pl.pallas_call(
        body, grid=(2, S5_Q),
        in_specs=[pl.BlockSpec((None, 4, n_tok, LANE), lambda ri, q: (ri, q, 0, 0)), pl.BlockSpec((n_tok, LANE), lambda ri, q: (0, q))],
        out_specs=pl.BlockSpec((None, None, 512, LANE), lambda ri, q: (ri, q, 0, 0)),
        out_shape=jax.ShapeDtypeStruct((2, S5_Q, 512, LANE), f32), name=name,
        compiler_params=pltpu.CompilerParams(dimension_semantics=("arbitrary", "arbitrary"), vmem_limit_bytes=VMEM_LIMIT),
    )(xs, d)


def _s5_wgrad_channels(name, a, a_blk0, gx, *, n_tok):
    def body(a_ref, g_ref, o_ref):
        for k in range(4):
            o_ref[:, k * LANE:(k + 1) * LANE] = _tn(a_ref[...], g_ref[k])

    return pl.pallas_call(
        body, grid=(2, S5_Q),
        in_specs=[pl.BlockSpec((n_tok, LANE), lambda ri, q: (0, a_blk0 + q)), pl.BlockSpec((None, 4, n_tok, LANE), lambda ri, q: (ri, q, 0, 0))],
        out_specs=pl.BlockSpec((None, None, LANE, 512), lambda ri, q: (ri, q, 0, 0)),
        out_shape=jax.ShapeDtypeStruct((2, S5_Q, LANE, 512), f32), name=name,
        compiler_params=pltpu.CompilerParams(dimension_semantics=("arbitrary", "arbitrary"), vmem_limit_bytes=VMEM_LIMIT),
    )(a, gx)


N_CHIPS = 4
_BIG_SHARD = {"w_in": (1, 1024, 1770, 1792), "mla_w_q_up": (1, 256, 192, 256), "mla_w_kv_up": (1, 128, 256, 256),
              "s5_w_glu": (0, 128, 512, 512), "w_branch_out": (0, 384, 1024, 1024), "w_out": (0, 256, 1024, 1024)}


def _to_shards(name, m):
    axis, r, c, cp = _BIG_SHARD[name]
    if axis == 0:
        return m.reshape(N_CHIPS, r, c)
    return jnp.stack([jnp.pad(m[:, j * c:(j + 1) * c], ((0, 0), (0, cp - c))) for j in range(N_CHIPS)])


def _from_shards(name, s):
    axis, r, c, cp = _BIG_SHARD[name]
    if axis == 0:
        return s.reshape(N_CHIPS * r, c)
    return jnp.concatenate([s[j, :, :c] for j in range(N_CHIPS)], axis=1)


def _pad_w_in(w):
    pieces, pos = [], 0
    for name in _PAD_ORDER:
        start, width, inner = _PAD[name]
        o0, ow = _ORIG[name]
        if start + inner > pos:
            pieces.append(jnp.zeros((w.shape[0], start + inner - pos), w.dtype))
        pieces.append(w[:, o0:o0 + ow])
        pos = start + inner + ow
    pieces.append(jnp.zeros((w.shape[0], NP - pos), w.dtype))
    return jnp.concatenate(pieces, axis=1)


def _prep_weights(small, big):
    per_layer = jax.vmap
    w = {}
    w["w_in_shards"] = big["w_in"]
    w["w_in"] = per_layer(lambda s: _pad_w_in(_from_shards("w_in", s)))(big["w_in"])

    def q_up(s):
        wq = _from_shards("mla_w_q_up", s).reshape(MLA_Q_RANK, HEADS, MLA_QK)
        return jnp.pad(wq, ((0, 0), (0, 0), (0, LANE - MLA_QK))).reshape(MLA_Q_RANK, HEADS * LANE)

    def kv_up(s):
        wkv = _from_shards("mla_w_kv_up", s).reshape(MLA_KV_RANK, HEADS, 128)
        wk = jnp.pad(wkv[:, :, :64], ((0, 0), (0, 0), (0, 64))).reshape(MLA_KV_RANK, HEADS * LANE)
        return jnp.concatenate([wk, wkv[:, :, 64:].reshape(MLA_KV_RANK, 512)], axis=1)

    w["wq"] = per_layer(q_up)(big["mla_w_q_up"])
    w["wkv"] = per_layer(kv_up)(big["mla_w_kv_up"])
    for name, key in (("w_glu", "s5_w_glu"), ("wo", "w_branch_out"), ("w_out", "w_out")):
        w[name] = per_layer(functools.partial(_from_shards, key))(big[key])
    row = lambda a: a.astype(f32)[:, None, :]
    lanes = lambda a, n: jnp.pad(row(a), ((0, 0), (0, 0), (0, LANE - n)))
    w["norm_g"] = row(small["norm_g"])
    w["qa_g"] = row(small["mla_q_a_norm"])
    w["kva_g"] = row(small["mla_kv_a_norm"])
    w["qn_g"] = lanes(small["mla_q_norm"], MLA_QK)
    w["kn_g"] = lanes(small["mla_k_norm"], MLA_QK)
    w["fq_g"] = jnp.tile(row(small["fox_q_norm"]), (1, 1, 2))
    w["fk_g"] = jnp.tile(row(small["fox_k_norm"]), (1, 1, 2))
    w["b_f"] = lanes(small["fox_b_f"], HEADS)
    w["lr"] = small["s5_lambda_re"].reshape(DEPTH, S5_LANES, 1)
    w["li"] = small["s5_lambda_im"].reshape(DEPTH, S5_LANES, 1)
    w["ldt"] = jnp.repeat(small["s5_log_dt"], S5_STATE, axis=1).reshape(DEPTH, S5_LANES, 1)
    w["b_re"] = small["s5_b_re"].reshape(DEPTH, S5_LANES, S5_GROUP)
    w["b_im"] = small["s5_b_im"].reshape(DEPTH, S5_LANES, S5_GROUP)
    w["s5_d"] = row(small["s5_d"])
    w["b_glu"] = row(small["s5_b_glu"])
    a_re, a_im, bb_re, bb_im = _s5_param_fwd(w["lr"], w["li"], w["ldt"], w["b_re"], w["b_im"])
    per_group = lambda m: m.reshape(S5_GROUPS, S5_STATE, S5_GROUP)
    pair = lambda f: per_layer(lambda re, im: jnp.stack([f(re), f(im)]).astype(bf16))
    c_re, c_im = small["s5_c_re"], -small["s5_c_im"]
    w["b_cn"] = pair(lambda m: _bd8(jnp.swapaxes(per_group(m), 1, 2)))(bb_re, bb_im)
    w["b_nc"] = pair(lambda m: _bd8(per_group(m)))(bb_re, bb_im)
    w["c_nc"] = pair(lambda m: _bd8(jnp.swapaxes(m, 1, 2)))(c_re, c_im)
    w["c_cn"] = pair(_bd8)(c_re, c_im)
    sublanes = lambda a: jnp.broadcast_to(a.reshape(DEPTH, S5_BLOCKS, 1, LANE), (DEPTH, S5_BLOCKS, 8, LANE))
    w["a_re8"], w["a_im8"], w["a_im8_neg"] = sublanes(a_re), sublanes(a_im), sublanes(-a_im)
    return w


def _fox_halves(x, lane):
    sq = x * x
    lo = jnp.sum(jnp.where(lane < 64, sq, 0.0), axis=-1, keepdims=True)
    hi = jnp.sum(sq, axis=-1, keepdims=True) - lo
    return jnp.where(lane < 64, lax.rsqrt(lo * (1.0 / 64) + EPS), lax.rsqrt(hi * (1.0 / 64) + EPS))


def _fox_halves_bwd(dy, x, r, g, lane):
    xh = x * r
    dxh = dy * g
    pr = dxh * xh
    lo = jnp.sum(jnp.where(lane < 64, pr, 0.0), axis=-1, keepdims=True)
    hi = jnp.sum(pr, axis=-1, keepdims=True) - lo
    mean = jnp.where(lane < 64, lo, hi) * (1.0 / 64)
    return r * (dxh - xh * mean), jnp.sum(dy * xh, axis=0, keepdims=True)


def _mla_recompute(cq, ckv, kpe, c, s1, s2, qa_g, kva_g, wq, wkv):
    cqn, r_cq = _rms(cq, qa_g, MLA_Q_RANK)
    ckvn, r_ckv = _rms(ckv, kva_g, MLA_KV_RANK)
    cqn_b = cqn.astype(bf16)
    ckvn_b = ckvn.astype(bf16)
    q_raw = _nn(cqn_b, wq)
    kv_raw = _nn(ckvn_b, wkv)
    kpe_rot = _rope(kpe, c, s1, s2)
    return cqn_b, r_cq, ckvn_b, r_ckv, q_raw, kv_raw, kpe_rot


def _layer_fwd(x, w, rope_tabs, n_tok):
    c_tab, s1_tab, s2_tab = rope_tabs
    saved = {"x": x}

    def norm_body(x_ref, g_ref, h_ref):
        h_ref[...] = _rms(x_ref[...], g_ref[...], D_MODEL)[0].astype(bf16)

    (h,) = _rowwise("norm_fwd", norm_body, n_tok, [(x, D_MODEL, 0)], [w["norm_g"]], [(D_MODEL, bf16)], [])
    proj = _mm_nn("in_proj", h, w["w_in"], m=n_tok, n=NP, k=D_MODEL, tm=n_tok, tn=512, tk=D_MODEL)
    saved["h"], saved["proj"] = h, proj

    def mla_prep_body(cq_ref, ckv_ref, kpe_ref, c_ref, s1_ref, s2_ref, qa_ref, kva_ref, wq_ref, wkv_ref, qn_g_ref, kn_g_ref,
                      qn_ref, kn_ref, v_ref):
        c, s1, s2 = c_ref[...], s1_ref[...], s2_ref[...]
        _, _, _, _, q_raw, kv_raw, kpe_rot = _mla_recompute(cq_ref[...], ckv_ref[...], kpe_ref[...], c, s1, s2,
                                                            qa_ref[...], kva_ref[...], wq_ref[...], wkv_ref[...])
        for hd in range(HEADS):
            sl = slice(hd * LANE, (hd + 1) * LANE)
            qn_ref[:, sl] = _rms(_rope(q_raw[:, sl], c, s1, s2), qn_g_ref[...], MLA_QK)[0].astype(bf16)
            kn_ref[:, sl] = _rms(kv_raw[:, sl] + kpe_rot, kn_g_ref[...], MLA_QK)[0].astype(bf16)
        v_ref[...] = kv_raw[:, HEADS * LANE:].astype(bf16)

    qn, kn, v_mla = _rowwise(
        "mla_prep", mla_prep_body, n_tok,
        [(proj, *_seg("cq")), (proj, *_seg("ckv")), (proj, *_seg("kpe")), (c_tab, LANE, 0), (s1_tab, LANE, 0), (s2_tab, LANE, 0)],
        [w["qa_g"], w["kva_g"], w["wq"], w["wkv"], w["qn_g"], w["kn_g"]],
        [(HEADS * LANE, bf16), (HEADS * LANE, bf16), (512, bf16)], [])
    y_mla, lse_mla = _attn_fwd("mla_attn_fwd", qn, kn, v_mla, None, mla=True, n_tok=n_tok)
    saved.update(qn=qn, kn=kn, v_mla=v_mla, y_mla=y_mla, lse_mla=lse_mla)

    def fox_prep_body(fq_ref, fk_ref, fv_ref, ff_ref, qg_ref, kg_ref, bf_ref, fqn_ref, fkn_ref, fvb_ref, logf_ref):
        lane = lax.broadcasted_iota(jnp.int32, (TOK, LANE), 1)
        for blk in range(4):
            sl = slice(blk * LANE, (blk + 1) * LANE)
            xq = fq_ref[:, sl]
            fqn_ref[:, sl] = (xq * _fox_halves(xq, lane) * qg_ref[...]).astype(bf16)
            xk = fk_ref[:, sl]
            fkn_ref[:, sl] = (xk * _fox_halves(xk, lane) * kg_ref[...]).astype(bf16)
        fvb_ref[...] = fv_ref[...].astype(bf16)
        z = ff_ref[...] + bf_ref[...]
        logf_ref[...] = jnp.minimum(z, 0.0) - jnp.log(1.0 + jnp.exp(-jnp.abs(z)))

    fqn, fkn, fvb, logf = _rowwise(
        "fox_prep", fox_prep_body, n_tok,
        [(proj, *_seg("fq")), (proj, *_seg("fk")), (proj, *_seg("fv")), (proj, *_seg("ff"))],
        [w["fq_g"], w["fk_g"], w["b_f"]],
        [(512, bf16), (512, bf16), (512, bf16), (LANE, f32)], [])

    def cum_body(x_ref, cum_ref):
        x = x_ref[...]
        t = lax.broadcasted_iota(jnp.int32, x.shape, 0)
        s = 1
        while s < n_tok:
            x = x + jnp.where(t >= s, pltpu.roll(x, s, 0), 0.0)
            s *= 2
        for hd in range(HEADS):
            cum_ref[hd] = jnp.broadcast_to(x[:, hd:hd + 1], (n_tok, LANE))

    cum_b = pl.pallas_call(cum_body, out_shape=jax.ShapeDtypeStruct((HEADS, n_tok, LANE), f32), name="fox_cum")(logf)
    y_fox, lse_fox = _attn_fwd("fox_attn_fwd", fqn, fkn, fvb, cum_b, mla=False, n_tok=n_tok)
    saved.update(fqn=fqn, fkn=fkn, fvb=fvb, cum_b=cum_b, y_fox=y_fox, lse_fox=lse_fox)

    u_w, u_blk = _seg("s5u")
    u_streams = _to_streams(proj[:, u_blk * u_w:(u_blk + 1) * u_w])
    bu = _s5_expand("s5_bu", u_streams, 0, w["b_cn"], n_tok=n_tok)
    xs = _s5_scan("s5_scan_fwd", bu, w["a_re8"], w["a_im8"], reverse=False, n_tok=n_tok)
    ylin = _from_streams(_s5_contract("s5_y", xs, w["c_nc"], None, f32, n_tok=n_tok))

    def s5_post_body(yl_ref, u_ref, d_ref, wg_ref, bg_ref, out_ref):
        y = yl_ref[...] + d_ref[...] * u_ref[...]
        z, _ = _gelu(y)
        out_ref[...] = z * _sigmoid(_nn(z, wg_ref[...]) + bg_ref[...])

    (y_s5,) = _rowwise("s5_post", s5_post_body, n_tok, [(ylin, 512, 0), (proj, u_w, u_blk)],
                       [w["s5_d"], w["w_glu"], w["b_glu"]], [(512, f32)], [])
    saved.update(xs=xs, ylin=ylin, y_s5=y_s5, u_streams=u_streams)

    def merge_body(ym_ref, yf_ref, ys_ref, gm_ref, gf_ref, gs_ref, mm_ref, mf_ref, ms_ref, x_ref, wo_ref, wout_ref, out_ref):
        merged = jnp.zeros((TOK, D_MODEL), f32)
        for b, (y_ref, g_ref, m_ref) in enumerate(((ym_ref, gm_ref, mm_ref), (yf_ref, gf_ref, mf_ref), (ys_ref, gs_ref, ms_ref))):
            g = g_ref[...]
            a = y_ref[...] * (g * _sigmoid(g))
            merged = merged + _sigmoid(m_ref[...]) * _nn(a, wo_ref[b * 512:(b + 1) * 512, :])
        out_ref[...] = x_ref[...] + _nn(merged, wout_ref[...])

    (out,) = _rowwise(
        "merge_fwd", merge_body, n_tok,
        [(y_mla, 512, 0), (y_fox, 512, 0), (y_s5, 512, 0), (proj, *_seg("g_mla")), (proj, *_seg("g_fox")), (proj, *_seg("g_s5")),
         (proj, *_seg("m_mla")), (proj, *_seg("m_fox")), (proj, *_seg("m_s5")), (x, D_MODEL, 0)],
        [w["wo"], w["w_out"]], [(D_MODEL, f32)], [])
    return out, saved


def _layer_bwd(dout, w, sv, rope_tabs, n_tok):
    c_tab, s1_tab, s2_tab = rope_tabs
    proj, x = sv["proj"], sv["x"]
    grads = {}

    def merge_bwd_body(ym_ref, yf_ref, ys_ref, gm_ref, gf_ref, gs_ref, mm_ref, mf_ref, ms_ref, do_ref, wo_ref, wout_ref,
                       dym_ref, dyf_ref, dys_ref, dgm_ref, dgf_ref, dgs_ref, dmm_ref, dmf_ref, dms_ref, dwo_ref, dwout_ref):
        do = do_ref[...]
        branches = ((ym_ref, gm_ref, mm_ref, dym_ref, dgm_ref, dmm_ref), (yf_ref, gf_ref, mf_ref, dyf_ref, dgf_ref, dmf_ref),
                    (ys_ref, gs_ref, ms_ref, dys_ref, dgs_ref, dms_ref))
        acts, outs, sigs = [], [], []
        merged = jnp.zeros((TOK, D_MODEL), f32)
        for b, (y_ref, g_ref, m_ref, _, _, _) in enumerate(branches):
            g = g_ref[...]
            a = (y_ref[...] * (g * _sigmoid(g))).astype(bf16)
            o = _nn(a, wo_ref[b * 512:(b + 1) * 512, :])
            s = _sigmoid(m_ref[...])
            merged = merged + s * o
            acts.append(a)
            outs.append(o)
            sigs.append(s)
        dmerged = _nt(do, wout_ref[...])
        _accumulate(dwout_ref, _tn(merged, do))
        dwo = []
        for b, (y_ref, g_ref, m_ref, dy_ref, dg_ref, dm_ref) in enumerate(branches):
            s, o = sigs[b], outs[b]
            dm_ref[...] = (dmerged * o * s * (1.0 - s)).astype(bf16)
            d_o = dmerged * s
            da = _nt(d_o, wo_ref[b * 512:(b + 1) * 512, :])
            dwo.append(_tn(acts[b], d_o))
            g = g_ref[...]
            sg = _sigmoid(g)
            dy_ref[...] = da * (g * sg)
            dg_ref[...] = (da * y_ref[...] * (sg * (1.0 + g * (1.0 - sg)))).astype(bf16)
        _accumulate(dwo_ref, jnp.concatenate(dwo, axis=0))

    (dy_mla, dy_fox, dy_s5, dg_mla, dg_fox, dg_s5, dm_mla, dm_fox, dm_s5, dwo, dwout) = _rowwise(
        "merge_bwd", merge_bwd_body, n_tok,
        [(sv["y_mla"], 512, 0), (sv["y_fox"], 512, 0), (sv["y_s5"], 512, 0), (proj, *_seg("g_mla")), (proj, *_seg("g_fox")),
         (proj, *_seg("g_s5")), (proj, *_seg("m_mla")), (proj, *_seg("m_fox")), (proj, *_seg("m_s5")), (dout, D_MODEL, 0)],
        [w["wo"], w["w_out"]],
        [(512, f32)] * 3 + [(512, bf16)] * 3 + [(D_MODEL, bf16)] * 3, [((1536, D_MODEL), f32), ((D_MODEL, D_MODEL), f32)])
    grads["w_branch_out"], grads["w_out"] = dwo, dwout

    u_w, u_blk = _seg("s5u")

    def s5_post_bwd_body(yl_ref, u_ref, do_ref, d_ref, wg_ref, bg_ref, dyl_ref, dus_ref, dd_ref, dwg_ref, dbg_ref):
        u = u_ref[...]
        y = yl_ref[...] + d_ref[...] * u
        z, t = _gelu(y)
        s = _sigmoid(_nn(z, wg_ref[...]) + bg_ref[...])
        do = do_ref[...]
        dgl = do * z * s * (1.0 - s)
        dz = do * s + _nt(dgl, wg_ref[...])
        dy = dz * _gelu_grad(y, t)
        dyl_ref[...] = dy.astype(bf16)
        dus_ref[...] = dy * d_ref[...]
        _accumulate(dd_ref, jnp.sum(dy * u, axis=0, keepdims=True))
        _accumulate(dwg_ref, _tn(z, dgl))
        _accumulate(dbg_ref, jnp.sum(dgl, axis=0, keepdims=True))

    dylin, du_skip, dd, dwglu, dbglu = _rowwise(
        "s5_post_bwd", s5_post_bwd_body, n_tok, [(sv["ylin"], 512, 0), (proj, u_w, u_blk), (dy_s5, 512, 0)],
        [w["s5_d"], w["w_glu"], w["b_glu"]], [(512, bf16), (512, f32)], [((1, 512), f32), ((512, 512), f32), ((1, 512), f32)])
    grads["s5_d"], grads["s5_w_glu"], grads["s5_b_glu"] = dd.reshape(512), dwglu, dbglu.reshape(512)

    dylin = _to_streams(dylin)
    dxs = _s5_expand("s5_dxs", dylin, 0, w["c_cn"], n_tok=n_tok)
    dc_nc = _s5_wgrad_states("s5_dc", sv["xs"], dylin, n_tok=n_tok)
    gx = _s5_scan("s5_scan_bwd", dxs, w["a_re8"], w["a_im8_neg"], reverse=True, n_tok=n_tok)
    da = _s5_da(sv["xs"], gx, n_tok=n_tok)
    ds5u = _from_streams(_s5_contract("s5_du", gx, w["b_nc"], _to_streams(du_skip), bf16, n_tok=n_tok))
    db_cn = _s5_wgrad_channels("s5_db", sv["u_streams"], 0, gx, n_tok=n_tok)
    diag_b = lambda m: jnp.swapaxes(_bd8_diag(m, S5_GROUP, S5_STATE), 1, 2).reshape(S5_LANES, S5_GROUP)
    diag_c = lambda m: jnp.swapaxes(_bd8_diag(m, S5_STATE, S5_GROUP), 1, 2)
    dlr, dli, dldt, db_re, db_im = _s5_param_bwd(
        w["lr"], w["li"], w["ldt"], w["b_re"], w["b_im"], da[:, 0, :].reshape(S5_LANES, 1), da[:, 1, :].reshape(S5_LANES, 1),
        diag_b(db_cn[0]), diag_b(db_cn[1]))
    grads["s5_lambda_re"] = dlr.reshape(S5_GROUPS, S5_STATE)
    grads["s5_lambda_im"] = dli.reshape(S5_GROUPS, S5_STATE)
    grads["s5_log_dt"] = dldt.reshape(S5_GROUPS)
    grads["s5_b_re"] = db_re.reshape(S5_GROUPS, S5_STATE, S5_GROUP)
    grads["s5_b_im"] = db_im.reshape(S5_GROUPS, S5_STATE, S5_GROUP)
    grads["s5_c_re"] = diag_c(dc_nc[0])
    grads["s5_c_im"] = -diag_c(dc_nc[1])

    dfqn, dfkn, dfv, dck, dcq = _attn_bwd("fox_attn_bwd", sv["fqn"], sv["fkn"], sv["fvb"], sv["y_fox"], sv["lse_fox"], dy_fox,
                                          sv["cum_b"], mla=False, n_tok=n_tok)
    dcq = jnp.pad(dcq[:, :2, :].reshape(HEADS, n_tok).T, ((0, 0), (0, LANE - HEADS)))

    def fox_gate_bwd_body(dk_ref, dq_ref, ff_ref, bf_ref, dff_ref, dbf_ref):
        xg = dk_ref[...] + dq_ref[...]
        t = lax.broadcasted_iota(jnp.int32, xg.shape, 0)
        s = 1
        while s < n_tok:
            xg = xg + jnp.where(t < n_tok - s, pltpu.roll(xg, n_tok - s, 0), 0.0)
            s *= 2
        dff = xg * _sigmoid(-(ff_ref[...] + bf_ref[...]))
        dff_ref[...] = dff.astype(bf16)
        dbf_ref[...] = jnp.sum(dff, axis=0, keepdims=True)

    ff_w, ff_blk = _seg("ff")
    dff, dbf = pl.pallas_call(
        fox_gate_bwd_body, grid=(1,),
        in_specs=[pl.BlockSpec((n_tok, LANE), lambda i: (0, 0)), pl.BlockSpec((n_tok, LANE), lambda i: (0, 0)),
                  pl.BlockSpec((n_tok, ff_w), lambda i: (0, ff_blk)), pl.BlockSpec((1, LANE), lambda i: (0, 0))],
        out_specs=[pl.BlockSpec((n_tok, LANE), lambda i: (0, 0)), pl.BlockSpec((1, LANE), lambda i: (0, 0))],
        out_shape=[jax.ShapeDtypeStruct((n_tok, LANE), bf16), jax.ShapeDtypeStruct((1, LANE), f32)], name="fox_gate_bwd",
    )(dck, dcq, proj, w["b_f"])
    grads["fox_b_f"] = dbf[0, :HEADS]

    def fox_prep_bwd_body(fq_ref, fk_ref, dqn_ref, dkn_ref, dv_ref, qg_ref, kg_ref, dfq_ref, dfk_ref, dfv_ref, dqg_ref, dkg_ref):
        lane = lax.broadcasted_iota(jnp.int32, (TOK, LANE), 1)
        dqg = jnp.zeros((1, LANE), f32)
        dkg = jnp.zeros((1, LANE), f32)
        for blk in range(4):
            sl = slice(blk * LANE, (blk + 1) * LANE)
            xq = fq_ref[:, sl]
            dx, dg = _fox_halves_bwd(dqn_ref[:, sl], xq, _fox_halves(xq, lane), qg_ref[...], lane)
            dfq_ref[:, sl] = dx.astype(bf16)
            dqg = dqg + dg
            xk = fk_ref[:, sl]
            dx, dg = _fox_halves_bwd(dkn_ref[:, sl], xk, _fox_halves(xk, lane), kg_ref[...], lane)
            dfk_ref[:, sl] = dx.astype(bf16)
            dkg = dkg + dg
        dfv_ref[...] = dv_ref[...].astype(bf16)
        _accumulate(dqg_ref, dqg + pltpu.roll(dqg, 64, 1))
        _accumulate(dkg_ref, dkg + pltpu.roll(dkg, 64, 1))

    dfq, dfk, dfvb, dfqg, dfkg = _rowwise(
        "fox_prep_bwd", fox_prep_bwd_body, n_tok,
        [(proj, *_seg("fq")), (proj, *_seg("fk")), (dfqn, 512, 0), (dfkn, 512, 0), (dfv, 512, 0)],
        [w["fq_g"], w["fk_g"]], [(512, bf16)] * 3, [((1, LANE), f32)] * 2)
    grads["fox_q_norm"], grads["fox_k_norm"] = dfqg[0, :FOX_DIM], dfkg[0, :FOX_DIM]

    dqn, dkn, dv_mla = _attn_bwd("mla_attn_bwd", sv["qn"], sv["kn"], sv["v_mla"], sv["y_mla"], sv["lse_mla"], dy_mla,
                                 None, mla=True, n_tok=n_tok)

    def mla_prep_bwd_body(cq_ref, ckv_ref, kpe_ref, c_ref, s1_ref, s2_ref, dqn_ref, dkn_ref, dv_ref,
                          qa_ref, kva_ref, wq_ref, wkv_ref, qn_g_ref, kn_g_ref,
                          dcq_ref, dckv_ref, dkpe_ref, dwq_ref, dwkv_ref, dqa_ref, dkva_ref, dqng_ref, dkng_ref):
        c, s1, s2 = c_ref[...], s1_ref[...], s2_ref[...]
        cq, ckv = cq_ref[...], ckv_ref[...]
        cqn_b, r_cq, ckvn_b, r_ckv, q_raw, kv_raw, kpe_rot = _mla_recompute(
            cq, ckv, kpe_ref[...], c, s1, s2, qa_ref[...], kva_ref[...], wq_ref[...], wkv_ref[...])
        lane = lax.broadcasted_iota(jnp.int32, (TOK, LANE), 1)
        dq_raw, dk_raw = [], []
        dkpe_rot = jnp.zeros((TOK, LANE), f32)
        dqng = jnp.zeros((1, LANE), f32)
        dkng = jnp.zeros((1, LANE), f32)
        for hd in range(HEADS):
            sl = slice(hd * LANE, (hd + 1) * LANE)
            q_rot = _rope(q_raw[:, sl], c, s1, s2)
            r = lax.rsqrt(jnp.sum(q_rot * q_rot, axis=-1, keepdims=True) * (1.0 / MLA_QK) + EPS)
            dx, dg = _rms_bwd(dqn_ref[:, sl], q_rot, r, qn_g_ref[...], MLA_QK)
            dqng = dqng + dg
            dq_raw.append(_rope_t(dx, c, s1, s2))
            k_full = kv_raw[:, sl] + kpe_rot
            r = lax.rsqrt(jnp.sum(k_full * k_full, axis=-1, keepdims=True) * (1.0 / MLA_QK) + EPS)
            dx, dg = _rms_bwd(dkn_ref[:, sl], k_full, r, kn_g_ref[...], MLA_QK)
            dkng = dkng + dg
            dk_raw.append(jnp.where(lane < 64, dx, 0.0))
            dkpe_rot = dkpe_rot + dx
        dkpe = _rope_t(dkpe_rot, c, s1, s2)
        dkpe_ref[...] = jnp.where(jnp.logical_and(lane >= 64, lane < 64 + ROPE), dkpe, 0.0).astype(bf16)
        dq_raw = jnp.concatenate(dq_raw, axis=1).astype(bf16)
        dkv_raw = jnp.concatenate(dk_raw + [dv_ref[...]], axis=1).astype(bf16)
        dcqn = _nt(dq_raw, wq_ref[...])
        dckvn = _nt(dkv_raw, wkv_ref[...])
        dx, dg = _rms_bwd(dcqn, cq, r_cq, qa_ref[...], MLA_Q_RANK)
        dcq_ref[...] = dx.astype(bf16)
        _accumulate(dqa_ref, dg)
        dx, dg = _rms_bwd(dckvn, ckv, r_ckv, kva_ref[...], MLA_KV_RANK)
        dckv_ref[...] = dx.astype(bf16)
        _accumulate(dkva_ref, dg)
        _accumulate(dwq_ref, _tn(cqn_b, dq_raw))
        _accumulate(dwkv_ref, _tn(ckvn_b, dkv_raw))
        _accumulate(dqng_ref, dqng)
        _accumulate(dkng_ref, dkng)

    dcq, dckv, dkpe, dwq, dwkv, dqa, dkva, dqng, dkng = _rowwise(
        "mla_prep_bwd", mla_prep_bwd_body, n_tok,
        [(proj, *_seg("cq")), (proj, *_seg("ckv")), (proj, *_seg("kpe")), (c_tab, LANE, 0), (s1_tab, LANE, 0), (s2_tab, LANE, 0),
         (dqn, HEADS * LANE, 0), (dkn, HEADS * LANE, 0), (dv_mla, 512, 0)],
        [w["qa_g"], w["kva_g"], w["wq"], w["wkv"], w["qn_g"], w["kn_g"]],
        [(MLA_Q_RANK, bf16), (LANE, bf16), (LANE, bf16)],
        [((MLA_Q_RANK, HEADS * LANE), f32), ((MLA_KV_RANK, HEADS * LANE + 512), f32), ((1, MLA_Q_RANK), f32),
         ((1, MLA_KV_RANK), f32), ((1, LANE), f32), ((1, LANE), f32)])
    grads["mla_w_q_up"] = dwq.reshape(MLA_Q_RANK, HEADS, LANE)[:, :, :MLA_QK].reshape(MLA_Q_RANK, HEADS * MLA_QK)
    dwk = dwkv[:, :HEADS * LANE].reshape(MLA_KV_RANK, HEADS, LANE)[:, :, :64]
    dwv = dwkv[:, HEADS * LANE:].reshape(MLA_KV_RANK, HEADS, 64)
    grads["mla_w_kv_up"] = jnp.concatenate([dwk, dwv], axis=2).reshape(MLA_KV_RANK, HEADS * 128)
    grads["mla_q_a_norm"], grads["mla_kv_a_norm"] = dqa.reshape(-1), dkva.reshape(-1)
    grads["mla_q_norm"], grads["mla_k_norm"] = dqng[0, :MLA_QK], dkng[0, :MLA_QK]

    _, _, shard_c, shard_cp = _BIG_SHARD["w_in"]
    kpe0 = _PAD["kpe"][2]
    dproj = jnp.concatenate([dcq, dckv, dkpe[:, kpe0:kpe0 + ROPE], dfq, dfk, dfvb, dff[:, :HEADS], ds5u, dg_mla, dg_fox, dg_s5,
                             dm_mla, dm_fox, dm_s5], axis=1)
    gap = jnp.zeros((n_tok, shard_cp - shard_c), bf16)
    dproj = jnp.concatenate([p for j in range(N_CHIPS) for p in (dproj[:, j * shard_c:(j + 1) * shard_c], gap)], axis=1)
    ct = 256
    per = shard_cp // ct
    dh = _mm("in_proj_dgrad", dproj, w["w_in_shards"], mode="nt", grid=(1, 1, N_CHIPS * per),
             a_spec=pl.BlockSpec((n_tok, ct), lambda i, j, kk: (0, kk)),
             b_spec=pl.BlockSpec((None, D_MODEL, ct), lambda i, j, kk: (kk // per, 0, kk % per)),
             o_spec=pl.BlockSpec((n_tok, D_MODEL), lambda i, j, kk: (0, 0)),
             out_shape=jax.ShapeDtypeStruct((n_tok, D_MODEL), f32), acc_shape=(n_tok, D_MODEL))
    grads["w_in"] = _mm("in_proj_wgrad", sv["h"], dproj, mode="tn", grid=(1, N_CHIPS * per, 1),
                        a_spec=pl.BlockSpec((n_tok, D_MODEL), lambda i, j, kk: (0, 0)),
                        b_spec=pl.BlockSpec((n_tok, ct), lambda i, j, kk: (0, j)),
                        o_spec=pl.BlockSpec((None, D_MODEL, ct), lambda i, j, kk: (j // per, 0, j % per)),
                        out_shape=jax.ShapeDtypeStruct((N_CHIPS, D_MODEL, shard_cp), f32), acc_shape=(D_MODEL, ct))

    def norm_bwd_body(dh_ref, x_ref, do_ref, g_ref, dx_ref, dg_ref):
        xv = x_ref[...]
        r = lax.rsqrt(jnp.sum(xv * xv, axis=-1, keepdims=True) * (1.0 / D_MODEL) + EPS)
        dx, dg = _rms_bwd(dh_ref[...], xv, r, g_ref[...], D_MODEL)
        dx_ref[...] = do_ref[...] + dx
        _accumulate(dg_ref, dg)

    dx, dng = _rowwise("norm_bwd", norm_bwd_body, n_tok, [(dh, D_MODEL, 0), (x, D_MODEL, 0), (dout, D_MODEL, 0)],
                       [w["norm_g"]], [(D_MODEL, f32)], [((1, D_MODEL), f32)])
    grads["norm_g"] = dng.reshape(D_MODEL)
    return dx, grads


def _rope_tables(positions):
    inv = 1.0 / (ROPE_THETA ** (jnp.arange(0, ROPE, 2, dtype=f32) / ROPE))
    ang = positions.astype(f32).reshape(-1, 1) * inv
    cos, sin = jnp.cos(ang), jnp.sin(ang)
    n = ang.shape[0]
    z16, z32, z64 = jnp.zeros((n, 16), f32), jnp.zeros((n, 32), f32), jnp.zeros((n, 64), f32)
    c = jnp.concatenate([jnp.ones((n, 64), f32), cos, cos, z32], axis=1)
    s1 = jnp.concatenate([z64, -sin, z16, z32], axis=1)
    s2 = jnp.concatenate([z64, z16, sin, z32], axis=1)
    return c, s1, s2


BIG = ("w_in", "mla_w_q_up", "mla_w_kv_up", "s5_w_glu", "w_branch_out", "w_out")
SMALL = ("norm_g", "mla_q_a_norm", "mla_kv_a_norm", "mla_q_norm", "mla_k_norm", "fox_b_f", "fox_q_norm", "fox_k_norm",
         "s5_lambda_re", "s5_lambda_im", "s5_log_dt", "s5_b_re", "s5_b_im", "s5_c_re", "s5_c_im", "s5_d", "s5_b_glu")
WEIGHTS = ("norm_g", "w_in", "mla_q_a_norm", "mla_w_q_up", "mla_kv_a_norm", "mla_w_kv_up", "mla_q_norm", "mla_k_norm",
           "fox_b_f", "fox_q_norm", "fox_k_norm", "s5_lambda_re", "s5_lambda_im", "s5_log_dt", "s5_b_re", "s5_b_im",
           "s5_c_re", "s5_c_im", "s5_d", "s5_w_glu", "s5_b_glu", "w_branch_out", "w_out")


def _local_step(x, positions, loss_target, small, big):
    n_tok = x.shape[0]
    tabs = _rope_tables(positions)
    ws, saves = [], []
    hcur = x
    stacked = _prep_weights(small, big)
    for l in range(DEPTH):
        w = {k: v[l] for k, v in stacked.items()}
        hcur, sv = _layer_fwd(hcur, w, tabs, n_tok)
        ws.append(w)
        saves.append(sv)

    def loss_body(y_ref, t_ref, d_ref, l_ref):
        err = y_ref[...] - t_ref[...]
        d_ref[...] = err * (1.0 / D_MODEL)
        tot = jnp.sum(jnp.sum(err * err, axis=-1, keepdims=True), axis=0, keepdims=True)
        _accumulate(l_ref, jnp.broadcast_to(tot * (0.5 / D_MODEL), (1, LANE)))

    dcur, loss = _rowwise("loss", loss_body, n_tok, [(hcur, D_MODEL, 0), (loss_target, D_MODEL, 0)], [], [(D_MODEL, f32)],
                          [((1, LANE), f32)])
    layer_grads = [None] * DEPTH
    for l in reversed(range(DEPTH)):
        dcur, layer_grads[l] = _layer_bwd(dcur, ws[l], saves[l], tabs, n_tok)
    grads = {n: jnp.stack([layer_grads[l][n] for l in range(DEPTH)]) for n in WEIGHTS}
    return loss[0, 0], dcur, grads


N_DEV = 8
_ANY = pl.BlockSpec(memory_space=pl.ANY)
_MESH = pl.DeviceIdType.MESH


def _all_gather8(name, blk):
    m = blk.shape[0]

    def body(x_ref, out_ref, send_sems, recv_sems, local_sem):
        x, y, c = lax.axis_index("x"), lax.axis_index("y"), lax.axis_index("c")
        me, sibling = (x, y, c), (x, y, 1 - c)
        chips = [(1 - x, y), (x, 1 - y), (1 - x, 1 - y)]

        def slot(px, py, pc):
            return out_ref.at[4 * px + 2 * py + pc]

        def copy(k, block, to, src=None):
            return pltpu.make_async_remote_copy(
                src_ref=slot(*block) if src is None else src, dst_ref=slot(*block),
                send_sem=send_sems.at[k], recv_sem=recv_sems.at[k], device_id=to, device_id_type=_MESH)

        mine = pltpu.make_async_copy(x_ref, slot(*me), local_sem)
        mine.start()
        first = [copy(0, me, sibling, src=x_ref)]
        first += [copy(1 + j, me, (*chip, c), src=x_ref) for j, chip in enumerate(chips)]
        for cp in first:
            cp.start()
        passed = [copy(4 + j, (*chip, c), sibling) for j, chip in enumerate(chips)]
        for j, chip in enumerate(chips):
            copy(1 + j, (*chip, c), me).wait_recv()
            passed[j].start()
        copy(0, sibling, me).wait_recv()
        for j, chip in enumerate(chips):
            copy(4 + j, (*chip, 1 - c), me).wait_recv()
        for cp in first + passed:
            cp.wait_send()
        mine.wait()

    return pl.pallas_call(
        body, out_shape=jax.ShapeDtypeStruct((N_DEV, m, LANE), blk.dtype), in_specs=[_ANY], out_specs=_ANY, name=name,
        scratch_shapes=[pltpu.SemaphoreType.DMA((7,)), pltpu.SemaphoreType.DMA((7,)), pltpu.SemaphoreType.DMA],
    )(blk)


def _gather_layers(name, shards):
    n = len(shards)

    def body(*refs):
        x_refs, out_refs = refs[:n], refs[n:2 * n]
        send_sems, recv_sems, local_sems = refs[2 * n:]
        x, y, c = lax.axis_index("x"), lax.axis_index("y"), lax.axis_index("c")
        me, sibling = (x, y, c), (x, y, 1 - c)
        xn, yn, dg = (1 - x, y, c), (x, 1 - y, c), (1 - x, 1 - y, c)
        relay_from = (x + (1 - c) * (1 - 2 * x), y + c * (1 - 2 * y), c)
        relay_to = (x + c * (1 - 2 * x), y + (1 - c) * (1 - 2 * y), c)

        def copy(w, k, block, to, src=None):
            px, py, pc = block
            slot = out_refs[w].at[pc, 2 * px + py]
            return pltpu.make_async_remote_copy(
                src_ref=slot if src is None else src, dst_ref=slot, send_sem=send_sems.at[7 * w + k],
                recv_sem=recv_sems.at[7 * w + k], device_id=to, device_id_type=_MESH)

        started, local = [], []
        for w in range(n):
            src = x_refs[w].at[c]
            mine = pltpu.make_async_copy(src, out_refs[w].at[c, 2 * x + y], local_sems.at[w])
            mine.start()
            local.append(mine)
            first = [copy(w, 0, me, sibling, src=src), copy(w, 1, me, xn, src=src), copy(w, 2, me, yn, src=src)]
            for cp in first:
                cp.start()
            started += first
        for w in range(n):
            copy(w, 1, xn, me).wait_recv()
            copy(w, 2, yn, me).wait_recv()
            onward = [copy(w, 3, relay_from, relay_to), copy(w, 4, xn, sibling), copy(w, 5, yn, sibling)]
            for cp in onward:
                cp.start()
            started += onward
        for w in range(n):
            copy(w, 3, dg, me).wait_recv()
            onward = copy(w, 6, dg, sibling)
            onward.start()
            started.append(onward)
        for w in range(n):
            copy(w, 0, sibling, me).wait_recv()
            for k, chip in ((4, xn), (5, yn), (6, dg)):
                copy(w, k, (chip[0], chip[1], 1 - c), me).wait_recv()
        for cp in started:
            cp.wait_send()
        for cp in local:
            cp.wait()

    return pl.pallas_call(
        body, out_shape=[jax.ShapeDtypeStruct((2, N_CHIPS) + s.shape[1:], s.dtype) for s in shards],
        in_specs=[_ANY] * n, out_specs=[_ANY] * n, name=name,
        scratch_shapes=[pltpu.SemaphoreType.DMA((7 * n,)), pltpu.SemaphoreType.DMA((7 * n,)), pltpu.SemaphoreType.DMA((n,))],
    )(*shards)


def _swap_layers(name, parts):
    n = len(parts)

    def body(*refs):
        p_refs, got_refs = refs[:n], refs[n:2 * n]
        send_sems, recv_sems = refs[2 * n:]
        x, y, c = lax.axis_index("x"), lax.axis_index("y"), lax.axis_index("c")
        copies = []
        for w in range(n):
            cp = pltpu.make_async_remote_copy(
                src_ref=p_refs[w].at[1 - c], dst_ref=got_refs[w], send_sem=send_sems.at[w], recv_sem=recv_sems.at[w],
                device_id=(x, y, 1 - c), device_id_type=_MESH)
            cp.start()
            copies.append(cp)
        for cp in copies:
            cp.wait()

    return pl.pallas_call(
        body, out_shape=[jax.ShapeDtypeStruct(p.shape[1:], p.dtype) for p in parts], in_specs=[_ANY] * n, out_specs=[_ANY] * n,
        name=name, scratch_shapes=[pltpu.SemaphoreType.DMA((n,)), pltpu.SemaphoreType.DMA((n,))],
    )(*parts)


def _scatter_to_chips(name, parts):
    n = len(parts)

    def body(*refs):
        p_refs, out_refs = refs[:n], refs[n:2 * n]
        send_sems, recv_sems, local_sems = refs[2 * n:]
        x, y, c = lax.axis_index("x"), lax.axis_index("y"), lax.axis_index("c")
        jme = 2 * x + y
        chips = [(1 - x, y), (x, 1 - y), (1 - x, 1 - y)]
        sends, local = [], []
        for w in range(n):
            mine = pltpu.make_async_copy(p_refs[w].at[jme], out_refs[w].at[jme], local_sems.at[w])
            mine.start()
            local.append(mine)
            for k, (tx, ty) in enumerate(chips):
                cp = pltpu.make_async_remote_copy(
                    src_ref=p_refs[w].at[2 * tx + ty], dst_ref=out_refs[w].at[jme], send_sem=send_sems.at[3 * w + k],
                    recv_sem=recv_sems.at[3 * w + k], device_id=(tx, ty, c), device_id_type=_MESH)
                cp.start()
                sends.append(cp)
        for w in range(n):
            for k, (tx, ty) in enumerate(chips):
                pltpu.make_async_remote_copy(
                    src_ref=p_refs[w].at[jme], dst_ref=out_refs[w].at[2 * tx + ty], send_sem=send_sems.at[3 * w + k],
                    recv_sem=recv_sems.at[3 * w + k], device_id=(tx, ty, c), device_id_type=_MESH).wait_recv()
        for cp in sends:
            cp.wait_send()
        for cp in local:
            cp.wait()

    return pl.pallas_call(
        body, out_shape=[jax.ShapeDtypeStruct(p.shape, p.dtype) for p in parts], in_specs=[_ANY] * n, out_specs=[_ANY] * n, name=name,
        scratch_shapes=[pltpu.SemaphoreType.DMA((3 * n,)), pltpu.SemaphoreType.DMA((3 * n,)), pltpu.SemaphoreType.DMA((n,))],
    )(*parts)


def _share_layers(name, bufs):
    n = len(bufs)

    def body(*refs):
        out_refs = refs[n:2 * n]
        send_sems, recv_sems = refs[2 * n:]
        x, y, c = lax.axis_index("x"), lax.axis_index("y"), lax.axis_index("c")
        copies = []
        for w in range(n):
            cp = pltpu.make_async_remote_copy(src_ref=out_refs[w].at[c], dst_ref=out_refs[w].at[c], send_sem=send_sems.at[w],
                                              recv_sem=recv_sems.at[w], device_id=(x, y, 1 - c), device_id_type=_MESH)
            cp.start()
            copies.append(cp)
        for w in range(n):
            pltpu.make_async_remote_copy(src_ref=out_refs[w].at[c], dst_ref=out_refs[w].at[1 - c], send_sem=send_sems.at[w],
                                         recv_sem=recv_sems.at[w], device_id=(x, y, 1 - c), device_id_type=_MESH).wait_recv()
        for cp in copies:
            cp.wait_send()

    return pl.pallas_call(
        body, out_shape=[jax.ShapeDtypeStruct(b.shape, b.dtype) for b in bufs], in_specs=[_ANY] * n, out_specs=[_ANY] * n,
        input_output_aliases={w: w for w in range(n)}, name=name,
        scratch_shapes=[pltpu.SemaphoreType.DMA((n,)), pltpu.SemaphoreType.DMA((n,))],
    )(*bufs)


def _row_tile(rows, cols):
    best = 16
    for t in range(16, rows + 1, 16):
        if rows % t == 0 and t * cols * 4 <= 2 * 1024 * 1024:
            best = t
    return best


def _add_pair(name, core, parts, got, out_dtype):
    _, _, r, c = parts.shape
    t = _row_tile(r, c)

    def body(core_ref, a_ref, b_ref, o_ref):
        o_ref[...] = (a_ref[...] + b_ref[...]).astype(o_ref.dtype)

    spec = pl.BlockSpec((None, t, c), lambda j, i, core_ref: (j, i, 0))
    grid_spec = pltpu.PrefetchScalarGridSpec(
        num_scalar_prefetch=1, grid=(N_CHIPS, r // t),
        in_specs=[pl.BlockSpec((None, None, t, c), lambda j, i, core_ref: (core_ref[0], j, i, 0)), spec], out_specs=spec)
    return pl.pallas_call(body, grid_spec=grid_spec, out_shape=jax.ShapeDtypeStruct(got.shape, out_dtype), name=name,
                          compiler_params=pltpu.CompilerParams(dimension_semantics=("arbitrary", "arbitrary")))(core, parts, got)


def _add_four(name, core, a):
    _, r, c = a.shape
    t = _row_tile(r, c)

    def body(core_ref, a0, a1, a2, a3, o_ref):
        o_ref[...] = ((a0[...].astype(f32) + a1[...].astype(f32)) + a2[...].astype(f32)) + a3[...].astype(f32)

    specs = [pl.BlockSpec((None, t, c), functools.partial(lambda i, core_ref, k: (k, i, 0), k=k)) for k in range(N_CHIPS)]
    grid_spec = pltpu.PrefetchScalarGridSpec(
        num_scalar_prefetch=1, grid=(r // t,), in_specs=specs,
        out_specs=pl.BlockSpec((None, t, c), lambda i, core_ref: (core_ref[0], i, 0)))
    return pl.pallas_call(body, grid_spec=grid_spec, out_shape=jax.ShapeDtypeStruct((2, r, c), f32), name=name,
                          compiler_params=pltpu.CompilerParams(dimension_semantics=("arbitrary",)))(core, a, a, a, a)


def _adamw(name, w, g, m, v, row_tile=None):
    c1 = 1.0 - ADAM_B1 ** ADAM_STEP
    c2 = 1.0 - ADAM_B2 ** ADAM_STEP

    def body(w_ref, g_ref, m_ref, v_ref, d_ref, nm_ref, nv_ref):
        gv = g_ref[...]
        nm = ADAM_B1 * m_ref[...] + (1.0 - ADAM_B1) * gv
        nv = ADAM_B2 * v_ref[...] + (1.0 - ADAM_B2) * (gv * gv)
        m_hat = nm / c1
        v_hat = nv / c2
        d_ref[...] = -ADAM_LR * (m_hat / (jnp.sqrt(v_hat) + ADAM_EPS) + ADAM_WD * w_ref[...])
        nm_ref[...] = nm
        nv_ref[...] = nv

    sds = jax.ShapeDtypeStruct(w.shape, f32)
    if row_tile is None:
        return pl.pallas_call(body, out_shape=[sds] * 3, name=name)(w, g, m, v)
    _, r, c = w.shape
    spec = pl.BlockSpec((None, row_tile, c), lambda l, i: (l, i, 0))
    return pl.pallas_call(body, grid=(DEPTH, r // row_tile), in_specs=[spec] * 4, out_specs=[spec] * 3, out_shape=[sds] * 3, name=name,
                          compiler_params=pltpu.CompilerParams(dimension_semantics=("arbitrary", "arbitrary"), vmem_limit_bytes=VMEM_LIMIT),
                          )(w, g, m, v)


def _pad_rows(flat, rows):
    return jnp.pad(flat, (0, rows * LANE - flat.shape[0])).reshape(rows, LANE)


def kernel(x, positions, norm_g, w_in, mla_q_a_norm, mla_w_q_up, mla_kv_a_norm, mla_w_kv_up, mla_q_norm, mla_k_norm, fox_b_f, fox_q_norm, fox_k_norm, s5_lambda_re, s5_lambda_im, s5_log_dt, s5_b_re, s5_b_im, s5_c_re, s5_c_im, s5_d, s5_w_glu, s5_b_glu, w_branch_out, w_out, loss_target, m_norm_g, m_w_in, m_mla_q_a_norm, m_mla_w_q_up, m_mla_kv_a_norm, m_mla_w_kv_up, m_mla_q_norm, m_mla_k_norm, m_fox_b_f, m_fox_q_norm, m_fox_k_norm, m_s5_lambda_re, m_s5_lambda_im, m_s5_log_dt, m_s5_b_re, m_s5_b_im, m_s5_c_re, m_s5_c_im, m_s5_d, m_s5_w_glu, m_s5_b_glu, m_w_branch_out, m_w_out, v_norm_g, v_w_in, v_mla_q_a_norm, v_mla_w_q_up, v_mla_kv_a_norm, v_mla_w_kv_up, v_mla_q_norm, v_mla_k_norm, v_fox_b_f, v_fox_q_norm, v_fox_k_norm, v_s5_lambda_re, v_s5_lambda_im, v_s5_log_dt, v_s5_b_re, v_s5_b_im, v_s5_c_re, v_s5_c_im, v_s5_d, v_s5_w_glu, v_s5_b_glu, v_w_branch_out, v_w_out):
    given = dict(locals())
    wts = {n: given[n] for n in WEIGHTS}
    mom1 = {n: given["m_" + n] for n in WEIGHTS}
    mom2 = {n: given["v_" + n] for n in WEIGHTS}

    def lanes(n, a):
        _, _, c, cp = _BIG_SHARD[n]
        return jnp.pad(a, ((0, 0), (0, 0), (0, cp - c)))

    gathered = _gather_layers("gather_weights", [lanes(n, wts[n].astype(bf16)) for n in BIG])
    big = dict(zip(BIG, gathered))
    small = {n: wts[n] for n in SMALL}

    loss_local, grad_x, grads = _local_step(x[0], positions, loss_target[0], small, big)
    loss = lax.psum(loss_local, ("x", "y", "c"))

    small_flat = jnp.concatenate([grads[n].reshape(-1) for n in SMALL])
    small_rows = -(-small_flat.shape[0] // (N_DEV * 16 * LANE)) * 16
    parts = [grads[n] if n == "w_in" else jnp.stack([_to_shards(n, grads[n][l]) for l in range(DEPTH)]) for n in BIG]
    parts.append(jnp.swapaxes(_pad_rows(small_flat, N_DEV * small_rows).reshape(N_CHIPS, 2, small_rows, LANE), 0, 1))
    core = lax.axis_index("c")
    core1 = core.reshape(1).astype(jnp.int32)
    got = _swap_layers("grads_to_sibling", parts)
    hop = [bf16] * len(BIG) + [f32]
    pair = [_add_pair("grads_pair_sum_%d" % i, core1, a, b, dt) for i, (a, b, dt) in enumerate(zip(parts, got, hop))]
    landed = _scatter_to_chips("grads_to_chips", pair)
    total = [_add_four("grads_chip_sum_%d" % i, core1, a) for i, a in enumerate(landed)]
    shared = _share_layers("grads_share", total[:-1])
    small_mine = lax.dynamic_index_in_dim(total[-1], core, 0, keepdims=False)
    small_all = _all_gather8("gather_small_grads", small_mine).reshape(-1)

    g_out = {n: s[:, :, :_BIG_SHARD[n][2]] for n, s in zip(BIG, shared)}
    pos = 0
    for n in SMALL:
        g_out[n] = small_all[pos:pos + wts[n].size].reshape(wts[n].shape)
        pos += wts[n].size

    delta, new_m, new_v = {}, {}, {}
    for n in WEIGHTS:
        row_tile = _row_tile(*wts[n].shape[1:]) if n in BIG else None
        delta[n], new_m[n], new_v[n] = _adamw("adamw_" + n, wts[n], g_out[n], mom1[n], mom2[n], row_tile)

    return (loss, grad_x[None], *[g_out[n] for n in WEIGHTS], *[delta[n] for n in WEIGHTS],
            *[new_m[n] for n in WEIGHTS], *[new_v[n] for n in WEIGHTS])
```

```python
import functools
import math

import jax
import jax.numpy as jnp
from jax import lax
from jax.experimental import pallas as pl
from jax.experimental.pallas import tpu as pltpu

f32 = jnp.float32
bf16 = jnp.bfloat16

D_MODEL = 1024
DEPTH = 2
EPS = 1e-6
HEADS = 8
MLA_QK = 96
MLA_Q_RANK = 256
MLA_KV_RANK = 128
ROPE = 32
ROPE_THETA = 10000.0
FOX_DIM = 64
S5_GROUPS = 32
S5_GROUP = 16
S5_STATE = 64
S5_LANES = S5_GROUPS * S5_STATE
LANE = 128
S5_BLOCKS = S5_LANES // LANE
IN_WIDTH = 7080
TOK = 256
VMEM_LIMIT = 56 * 1024 * 1024

ADAM_LR = 0.001
ADAM_B1 = 0.9
ADAM_B2 = 0.999
ADAM_EPS = 1e-08
ADAM_WD = 0.01
ADAM_STEP = 10

_ORIG = {}
_off = 0
for _n, _w in (("cq", 256), ("ckv", 128), ("kpe", 32), ("fq", 512), ("fk", 512), ("fv", 512), ("ff", 8), ("s5u", 512),
               ("g_mla", 512), ("g_fox", 512), ("g_s5", 512), ("m_mla", 1024), ("m_fox", 1024), ("m_s5", 1024)):
    _ORIG[_n] = (_off, _w)
    _off += _w
_PAD = {"m_mla": (0, 1024, 0), "m_fox": (1024, 1024, 0), "m_s5": (2048, 1024, 0),
        "fq": (3072, 512, 0), "fk": (3584, 512, 0), "fv": (4096, 512, 0), "s5u": (4608, 512, 0),
        "g_mla": (5120, 512, 0), "g_fox": (5632, 512, 0), "g_s5": (6144, 512, 0),
        "cq": (6656, 256, 0), "ckv": (6912, 128, 0), "kpe": (7040, 128, 64), "ff": (7168, 128, 0)}
NP = 7680
_PAD_ORDER = ("m_mla", "m_fox", "m_s5", "fq", "fk", "fv", "s5u", "g_mla", "g_fox", "g_s5", "cq", "ckv", "kpe", "ff")


def _seg(name):
    start, width, _ = _PAD[name]
    return width, start // width


def _nn(a, b):
    return lax.dot_general(a.astype(bf16), b.astype(bf16), (((1,), (0,)), ((), ())), preferred_element_type=f32)


def _nt(a, b):
    return lax.dot_general(a.astype(bf16), b.astype(bf16), (((1,), (1,)), ((), ())), preferred_element_type=f32)


def _tn(a, b):
    return lax.dot_general(a.astype(bf16), b.astype(bf16), (((0,), (0,)), ((), ())), preferred_element_type=f32)


def _rms(x, g, n):
    r = lax.rsqrt(jnp.sum(x * x, axis=-1, keepdims=True) * (1.0 / n) + EPS)
    return x * r * g, r


def _rms_bwd(dy, x, r, g, n):
    xh = x * r
    dg = jnp.sum(dy * xh, axis=0, keepdims=True)
    dxh = dy * g
    dx = r * (dxh - xh * (jnp.sum(dxh * xh, axis=-1, keepdims=True) * (1.0 / n)))
    return dx, dg


def _sigmoid(x):
    return 1.0 / (1.0 + jnp.exp(-x))


_GELU_C = math.sqrt(2.0 / math.pi)


def _gelu(x):
    t = jnp.tanh(_GELU_C * (x + 0.044715 * x * x * x))
    return 0.5 * x * (1.0 + t), t


def _gelu_grad(x, t):
    return 0.5 * (1.0 + t) + 0.5 * x * (1.0 - t * t) * _GELU_C * (1.0 + 3.0 * 0.044715 * x * x)


def _accumulate(ref, val):
    i = pl.program_id(0)

    @pl.when(i == 0)
    def _():
        ref[...] = val

    @pl.when(i > 0)
    def _():
        ref[...] += val


def _rope(x, c, s1, s2):
    return x * c + pltpu.roll(x, LANE - 16, 1) * s1 + pltpu.roll(x, 16, 1) * s2


def _rope_t(d, c, s1, s2):
    return d * c + pltpu.roll(d * s1, 16, 1) + pltpu.roll(d * s2, LANE - 16, 1)


def _const_map(ndim):
    return lambda *_: (0,) * ndim


def _rowwise(name, body, n_tok, tiled_in, full_in, tiled_out, acc_out, tile=TOK):
    in_specs, args = [], []
    for arr, width, blk in tiled_in:
        in_specs.append(pl.BlockSpec((tile, width), functools.partial(lambda i, b: (i, b), b=blk)))
        args.append(arr)
    for arr in full_in:
        in_specs.append(pl.BlockSpec(arr.shape, _const_map(arr.ndim)))
        args.append(arr)
    out_specs, out_shape = [], []
    for width, dt in tiled_out:
        out_specs.append(pl.BlockSpec((tile, width), lambda i: (i, 0)))
        out_shape.append(jax.ShapeDtypeStruct((n_tok, width), dt))
    for shape, dt in acc_out:
        out_specs.append(pl.BlockSpec(shape, _const_map(len(shape))))
        out_shape.append(jax.ShapeDtypeStruct(shape, dt))
    return pl.pallas_call(
        body, grid=(n_tok // tile,), in_specs=in_specs, out_specs=out_specs, out_shape=out_shape, name=name,
        compiler_params=pltpu.CompilerParams(dimension_semantics=("arbitrary",), vmem_limit_bytes=VMEM_LIMIT),
    )(*args)


def _mm(name, a, b, *, mode, grid, a_spec, b_spec, o_spec, out_shape, acc_shape, add=None, add_spec=None):
    nk = grid[2]

    def body(*refs):
        if add is None:
            a_ref, b_ref, o_ref, acc_ref = refs
        else:
            a_ref, b_ref, add_ref, o_ref, acc_ref = refs
        k = pl.program_id(2)

        @pl.when(k == 0)
        def _():
            acc_ref[...] = jnp.zeros_like(acc_ref)

        acc_ref[...] += {"nn": _nn, "nt": _nt, "tn": _tn}[mode](a_ref[...], b_ref[...])

        @pl.when(k == nk - 1)
        def _():
            r = acc_ref[...]
            if add is not None:
                r = r + add_ref[...]
            o_ref[...] = r.astype(o_ref.dtype)

    in_specs = [a_spec, b_spec] + ([add_spec] if add is not None else [])
    args = (a, b) + ((add,) if add is not None else ())
    return pl.pallas_call(
        body, grid=grid, in_specs=in_specs, out_specs=o_spec, out_shape=out_shape, name=name,
        scratch_shapes=[pltpu.VMEM(acc_shape, f32)],
        compiler_params=pltpu.CompilerParams(dimension_semantics=("arbitrary", "arbitrary", "arbitrary"), vmem_limit_bytes=VMEM_LIMIT),
    )(*args)


def _mm_nn(name, a, b, *, m, n, k, tm, tn, tk, out_dtype=f32, a_koff=0):
    return _mm(name, a, b, mode="nn", grid=(m // tm, n // tn, k // tk),
               a_spec=pl.BlockSpec((tm, tk), lambda i, j, kk: (i, kk + a_koff)),
               b_spec=pl.BlockSpec((tk, tn), lambda i, j, kk: (kk, j)),
               o_spec=pl.BlockSpec((tm, tn), lambda i, j, kk: (i, j)),
               out_shape=jax.ShapeDtypeStruct((m, n), out_dtype), acc_shape=(tm, tn))


def _mm_tn(name, a, b, *, m, n, k, tm, tn, tk, a_moff=0):
    return _mm(name, a, b, mode="tn", grid=(m // tm, n // tn, k // tk),
               a_spec=pl.BlockSpec((tk, tm), lambda i, j, kk: (kk, i + a_moff)),
               b_spec=pl.BlockSpec((tk, tn), lambda i, j, kk: (kk, j)),
               o_spec=pl.BlockSpec((tm, tn), lambda i, j, kk: (i, j)),
               out_shape=jax.ShapeDtypeStruct((m, n), f32), acc_shape=(tm, tn))


ATT_KV = 256
ATT_Q = 512


def _attn_common(mla, n_tok):
    qw = 2 * LANE if mla else LANE
    scale = 1.0 / math.sqrt(MLA_QK if mla else FOX_DIM)
    return qw, scale, min(ATT_Q, n_tok)


def _attn_heads(q_ref, mla):
    out = []
    if mla:
        for e in (0, 1):
            qe = q_ref[:, e * LANE:(e + 1) * LANE]
            out.append((qe.astype(f32).T.astype(bf16), qe))
        return out
    q = q_ref[...]
    tq = q.shape[0]
    qt = q.astype(f32).T
    row = lax.broadcasted_iota(jnp.int32, (LANE, tq), 0)
    lane = lax.broadcasted_iota(jnp.int32, (tq, LANE), 1)
    for e in (0, 1):
        out.append((jnp.where((row >= 64) == bool(e), qt, 0.0).astype(bf16),
                    jnp.where((lane >= 64) == bool(e), q, jnp.zeros((), bf16))))
    return out


def _attn_allowed(off, i, tq, mla):
    kpos = off + lax.broadcasted_iota(jnp.int32, (ATT_KV, tq), 0)
    qpos = i * tq + lax.broadcasted_iota(jnp.int32, (ATT_KV, tq), 1)
    return ((kpos // 64) <= (qpos // 64)) if mla else (kpos <= qpos)


def _attn_fwd(name, q, k, v, cum_b, *, mla, n_tok):
    qw, scale, tq = _attn_common(mla, n_tok)
    nq = n_tok // tq
    nkv = n_tok // ATT_KV
    has_bias = cum_b is not None

    def body(*refs):
        if has_bias:
            q_ref, k_ref, v_ref, cb_ref, o_ref, lse_ref, vt_ref = refs
        else:
            q_ref, k_ref, v_ref, o_ref, lse_ref, vt_ref = refs
        i = pl.program_id(1)

        @pl.when(i == 0)
        def _():
            for jb in range(nkv):
                vt_ref[jb] = v_ref[jb * ATT_KV:(jb + 1) * ATT_KV, :].astype(f32).T.astype(bf16)

        heads = _attn_heads(q_ref, mla)

        def step(j, carry, masked):
            off = pl.multiple_of(j * ATT_KV, ATT_KV)
            allowed = _attn_allowed(off, i, tq, mla) if masked else None
            vt = vt_ref[j]
            sts = []
            for e in (0, 1):
                kb = k_ref[pl.ds(off, ATT_KV), e * LANE:(e + 1) * LANE] if mla else k_ref[pl.ds(off, ATT_KV), :]
                sts.append(_nn(kb, heads[e][0]))
            stats = []
            for e in (0, 1):
                m, l, _ = carry[e]
                st = sts[e] * scale
                if has_bias:
                    st = st - jnp.tile(cb_ref[e, pl.ds(off, ATT_KV), :], (1, tq // LANE))
                if masked:
                    st = jnp.where(allowed, st, -1e30)
                m_new = jnp.maximum(m, jnp.max(st, axis=0, keepdims=True))
                alpha = jnp.exp(m - m_new)
                pt = jnp.exp(st - m_new)
                stats.append((m_new, alpha * l + jnp.sum(pt, axis=0, keepdims=True), alpha, pt.astype(bf16)))
            new = []
            for e in (0, 1):
                m_new, l, alpha, pt = stats[e]
                new.append((m_new, l, alpha * carry[e][2] + _nn(vt[64 * e:64 * e + 64, :], pt)))
            return tuple(new)

        init = tuple((jnp.full((1, tq), -1e30, f32), jnp.zeros((1, tq), f32), jnp.zeros((64, tq), f32)) for _ in (0, 1))
        n_full = i * (tq // ATT_KV)
        carry = lax.fori_loop(0, n_full, functools.partial(step, masked=False), init)
        for d in range(tq // ATT_KV):
            carry = step(n_full + d, carry, True)
        o_ref[...] = jnp.concatenate([carry[e][2] / carry[e][1] for e in (0, 1)], axis=0).T
        lse_ref[...] = jnp.zeros_like(lse_ref)
        for e in (0, 1):
            lse_ref[e:e + 1, :] = carry[e][0] + jnp.log(carry[e][1])

    in_specs = [pl.BlockSpec((tq, qw), lambda p, i: (i, p)),
                pl.BlockSpec((n_tok, qw), lambda p, i: (0, p)),
                pl.BlockSpec((n_tok, LANE), lambda p, i: (0, p))]
    args = [q, k, v]
    if has_bias:
        in_specs.append(pl.BlockSpec((2, n_tok, LANE), lambda p, i: (p, 0, 0)))
        args.append(cum_b)
    return pl.pallas_call(
        body, grid=(4, nq), in_specs=in_specs,
        out_specs=[pl.BlockSpec((tq, LANE), lambda p, i: (i, p)), pl.BlockSpec((None, 8, tq), lambda p, i: (p, 0, i))],
        out_shape=[jax.ShapeDtypeStruct((n_tok, 512), f32), jax.ShapeDtypeStruct((4, 8, n_tok), f32)], name=name,
        scratch_shapes=[pltpu.VMEM((nkv, LANE, ATT_KV), bf16)],
        compiler_params=pltpu.CompilerParams(dimension_semantics=("arbitrary", "arbitrary"), vmem_limit_bytes=VMEM_LIMIT),
    )(*args)


def _attn_bwd(name, q, k, v, o, lse, do, cum_b, *, mla, n_tok):
    qw, scale, tq = _attn_common(mla, n_tok)
    nq = n_tok // tq
    nkv = n_tok // ATT_KV
    has_bias = cum_b is not None

    def body(*refs):
        if has_bias:
            q_ref, k_ref, v_ref, o_ref, lse_ref, do_ref, cb_ref, dq_ref, dk_ref, dv_ref, dck_ref, dcq_ref, kt_ref = refs
        else:
            q_ref, k_ref, v_ref, o_ref, lse_ref, do_ref, dq_ref, dk_ref, dv_ref, kt_ref = refs
        p = pl.program_id(0)
        i = pl.program_id(1)

        @pl.when(i == 0)
        def _():
            dk_ref[...] = jnp.zeros_like(dk_ref)
            dv_ref[...] = jnp.zeros_like(dv_ref)
            for jb in range(nkv):
                for c0 in range(0, qw, LANE):
                    kt_ref[jb, c0:c0 + LANE, :] = k_ref[jb * ATT_KV:(jb + 1) * ATT_KV, c0:c0 + LANE].astype(f32).T.astype(bf16)

        if has_bias:
            @pl.when(jnp.logical_and(i == 0, p == 0))
            def _():
                dck_ref[...] = jnp.zeros_like(dck_ref)

        heads = _attn_heads(q_ref, mla)
        do = do_ref[...]
        do_t = do.T
        prod_t = (do * o_ref[...]).T
        row = lax.broadcasted_iota(jnp.int32, (LANE, tq), 0)
        lane = lax.broadcasted_iota(jnp.int32, (tq, LANE), 1)
        lane_k = lax.broadcasted_iota(jnp.int32, (ATT_KV, LANE), 1)
        per_head = []
        for e in (0, 1):
            sel_r = (row >= 64) == bool(e)
            per_head.append((jnp.where(sel_r, do_t, 0.0).astype(bf16),
                             jnp.where((lane >= 64) == bool(e), do, 0.0).astype(bf16),
                             jnp.sum(jnp.where(sel_r, prod_t, 0.0), axis=0, keepdims=True),
                             lse_ref[e:e + 1, :]))
        dq_rows = LANE if mla else 64

        def step(j, carry, masked):
            off = pl.multiple_of(j * ATT_KV, ATT_KV)
            allowed = _attn_allowed(off, i, tq, mla) if masked else None
            vb = v_ref[pl.ds(off, ATT_KV), :]
            kt = kt_ref[j]
            cols = [slice(e * LANE, (e + 1) * LANE) if mla else slice(None) for e in (0, 1)]
            sts = [_nn(k_ref[pl.ds(off, ATT_KV), cols[e]], heads[e][0]) for e in (0, 1)]
            dpts = [_nn(vb, per_head[e][0]) for e in (0, 1)]
            mids = []
            for e in (0, 1):
                _, _, delta, lse_e = per_head[e]
                st = sts[e] * scale
                if has_bias:
                    st = st - jnp.tile(cb_ref[e, pl.ds(off, ATT_KV), :], (1, tq // LANE))
                pt = jnp.exp(st - lse_e)
                if masked:
                    pt = jnp.where(allowed, pt, 0.0)
                dst = pt * (dpts[e] - delta)
                qsum = carry[e][1]
                if has_bias:
                    rs = jnp.sum(dst, axis=1, keepdims=True)
                    dck_ref[pl.ds(off, ATT_KV), :] += jnp.where(lane_k == 2 * p + e, -rs, 0.0)
                    qsum = qsum + jnp.sum(dst, axis=0, keepdims=True)
                mids.append((pt.astype(bf16), dst.astype(bf16), qsum))
            new = []
            for e in (0, 1):
                pt, dst, qsum = mids[e]
                kt_e = kt[e * LANE:(e + 1) * LANE, :] if mla else kt[64 * e:64 * e + 64, :]
                new.append((carry[e][0] + _nn(kt_e, dst) * scale, qsum))
                dk_ref[pl.ds(off, ATT_KV), cols[e]] += _nn(dst, heads[e][1]) * scale
                dv_ref[pl.ds(off, ATT_KV), :] += _nn(pt, per_head[e][1])
            return tuple(new)

        init = tuple((jnp.zeros((dq_rows, tq), f32), jnp.zeros((1, tq), f32)) for _ in (0, 1))
        n_full = i * (tq // ATT_KV)
        carry = lax.fori_loop(0, n_full, functools.partial(step, masked=False), init)
        for d in range(tq // ATT_KV):
            carry = step(n_full + d, carry, True)
        if mla:
            for e in (0, 1):
                dq_ref[:, e * LANE:(e + 1) * LANE] = carry[e][0].T
        else:
            dq_ref[...] = jnp.concatenate([carry[0][0], carry[1][0]], axis=0).T
        if has_bias:
            dcq_ref[...] = jnp.zeros_like(dcq_ref)
            for e in (0, 1):
                dcq_ref[e:e + 1, :] = carry[e][1]

    tile_q = pl.BlockSpec((tq, qw), lambda p, i: (i, p))
    tile_v = pl.BlockSpec((tq, LANE), lambda p, i: (i, p))
    full_k = pl.BlockSpec((n_tok, qw), lambda p, i: (0, p))
    full_v = pl.BlockSpec((n_tok, LANE), lambda p, i: (0, p))
    in_specs = [tile_q, full_k, full_v, tile_v, pl.BlockSpec((None, 8, tq), lambda p, i: (p, 0, i)), tile_v]
    args = [q, k, v, o, lse, do]
    out_specs = [tile_q, full_k, full_v]
    out_shape = [jax.ShapeDtypeStruct((n_tok, 4 * qw), f32), jax.ShapeDtypeStruct((n_tok, 4 * qw), f32),
                 jax.ShapeDtypeStruct((n_tok, 512), f32)]
    if has_bias:
        in_specs.append(pl.BlockSpec((2, n_tok, LANE), lambda p, i: (p, 0, 0)))
        args.append(cum_b)
        out_specs += [pl.BlockSpec((n_tok, LANE), _const_map(2)), pl.BlockSpec((None, 8, tq), lambda p, i: (p, 0, i))]
        out_shape += [jax.ShapeDtypeStruct((n_tok, LANE), f32), jax.ShapeDtypeStruct((4, 8, n_tok), f32)]
    return pl.pallas_call(
        body, grid=(4, nq), in_specs=in_specs, out_specs=out_specs, out_shape=out_shape, name=name,
        scratch_shapes=[pltpu.VMEM((nkv, qw, ATT_KV), bf16)],
        compiler_params=pltpu.CompilerParams(dimension_semantics=("arbitrary", "arbitrary"), vmem_limit_bytes=VMEM_LIMIT),
    )(*args)


def _s5_disc(lr, li, ldt):
    dt = jnp.exp(ldt)
    mag = jnp.exp(lr * dt)
    a_re = mag * jnp.cos(li * dt)
    a_im = mag * jnp.sin(li * dt)
    den = lr * lr + li * li
    f_re = ((a_re - 1.0) * lr + a_im * li) / den
    f_im = (a_im * lr - (a_re - 1.0) * li) / den
    return a_re, a_im, f_re, f_im


def _s5_param_fwd(lr, li, ldt, b_re, b_im):
    def body(lr_ref, li_ref, ldt_ref, br_ref, bi_ref, ar_ref, ai_ref, bbr_ref, bbi_ref):
        a_re, a_im, f_re, f_im = _s5_disc(lr_ref[...], li_ref[...], ldt_ref[...])
        ar_ref[...] = a_re
        ai_ref[...] = a_im
        br, bi = br_ref[...], bi_ref[...]
        bbr_ref[...] = f_re * br - f_im * bi
        bbi_ref[...] = f_re * bi + f_im * br

    col = jax.ShapeDtypeStruct(lr.shape, f32)
    mat = jax.ShapeDtypeStruct(b_re.shape, f32)
    return pl.pallas_call(body, out_shape=[col, col, mat, mat], name="s5_param_fwd")(lr, li, ldt, b_re, b_im)


def _s5_param_bwd(lr, li, ldt, b_re, b_im, da_re, da_im, dbb_re, dbb_im):
    def body(lr_ref, li_ref, ldt_ref, br_ref, bi_ref, dar_ref, dai_ref, gbr_ref, gbi_ref,
             dlr_ref, dli_ref, dldt_ref, dbr_ref, dbi_ref):
        (a_re, a_im, f_re, f_im), vjp = jax.vjp(_s5_disc, lr_ref[...], li_ref[...], ldt_ref[...])
        br, bi, gr, gi = br_ref[...], bi_ref[...], gbr_ref[...], gbi_ref[...]
        dbr_ref[...] = f_re * gr + f_im * gi
        dbi_ref[...] = f_re * gi - f_im * gr
        dfr = jnp.sum(br * gr + bi * gi, axis=-1, keepdims=True)
        dfi = jnp.sum(br * gi - bi * gr, axis=-1, keepdims=True)
        dlr, dli, dldt = vjp((dar_ref[...], dai_ref[...], dfr, dfi))
        dlr_ref[...] = dlr
        dli_ref[...] = dli
        dldt_ref[...] = jnp.sum(dldt.reshape(S5_GROUPS, S5_STATE, 1), axis=1)

    col = jax.ShapeDtypeStruct((S5_LANES, 1), f32)
    mat = jax.ShapeDtypeStruct((S5_LANES, S5_GROUP), f32)
    return pl.pallas_call(body, out_shape=[col, col, jax.ShapeDtypeStruct((S5_GROUPS, 1), f32), mat, mat],
                          name="s5_param_bwd")(lr, li, ldt, b_re, b_im, da_re, da_im, dbb_re, dbb_im)


_SCAN_NB = 4


def _to_streams(a):
    s, c = a.shape
    return jnp.swapaxes(a.reshape(8, s // 8, c), 0, 1).reshape(s, c)


def _from_streams(a):
    s, c = a.shape
    return jnp.swapaxes(a.reshape(s // 8, 8, c), 0, 1).reshape(s, c)


def _s5_scan(name, bu, a_re8, a_im8, *, reverse, n_tok):
    rows = n_tok // 8
    nb = _SCAN_NB

    def body(bu_ref, ar_ref, ai_ref, x_ref):
        a_r = [ar_ref[b] for b in range(nb)]
        a_i = [ai_ref[b] for b in range(nb)]
        zero = jnp.zeros((8, LANE), f32)
        one = jnp.ones((8, LANE), f32)

        def rows_at(r):
            rr = (rows - 1 - r) if reverse else r
            return pl.ds(pl.multiple_of(rr * 8, 8), 8)

        def pass1(r, carry):
            out = []
            sl = rows_at(r)
            for b in range(nb):
                xr, xi, mr, mi = carry[b]
                nr = a_r[b] * xr - a_i[b] * xi + bu_ref[0, b, sl, :]
                ni = a_r[b] * xi + a_i[b] * xr + bu_ref[1, b, sl, :]
                x_ref[0, b, sl, :] = nr
                x_ref[1, b, sl, :] = ni
                out.append((nr, ni, a_r[b] * mr - a_i[b] * mi, a_r[b] * mi + a_i[b] * mr))
            return tuple(out)

        carry = lax.fori_loop(0, rows, pass1, tuple((zero, zero, one, zero) for _ in range(nb)))
        sub = lax.broadcasted_iota(jnp.int32, (8, LANE), 0)
        feed = []
        for b in range(nb):
            lr_, li_, pr, pi = carry[b]
            fr, fi = zero, zero
            for _ in range(7):
                tr = lr_ + pr * fr - pi * fi
                ti = li_ + pr * fi + pi * fr
                if reverse:
                    fr = jnp.where(sub < 7, pltpu.roll(tr, 7, 0), 0.0)
                    fi = jnp.where(sub < 7, pltpu.roll(ti, 7, 0), 0.0)
                else:
                    fr = jnp.where(sub > 0, pltpu.roll(tr, 1, 0), 0.0)
                    fi = jnp.where(sub > 0, pltpu.roll(ti, 1, 0), 0.0)
            feed.append((fr, fi))

        def pass2(r, carry):
            out = []
            sl = rows_at(r)
            for b in range(nb):
                mr, mi = carry[b]
                fr, fi = feed[b]
                x_ref[0, b, sl, :] += mr * fr - mi * fi
                x_ref[1, b, sl, :] += mr * fi + mi * fr
                out.append((a_r[b] * mr - a_i[b] * mi, a_r[b] * mi + a_i[b] * mr))
            return tuple(out)

        lax.fori_loop(0, rows, pass2, tuple((a_r[b], a_i[b]) for b in range(nb)))

    blk = pl.BlockSpec((2, nb, n_tok, LANE), lambda g: (0, g, 0, 0))
    ablk = pl.BlockSpec((nb, 8, LANE), lambda g: (g, 0, 0))
    return pl.pallas_call(
        body, grid=(S5_BLOCKS // nb,), in_specs=[blk, ablk, ablk], out_specs=blk,
        out_shape=jax.ShapeDtypeStruct((2, S5_BLOCKS, n_tok, LANE), f32), name=name,
        compiler_params=pltpu.CompilerParams(dimension_semantics=("arbitrary",), vmem_limit_bytes=VMEM_LIMIT),
    )(bu, a_re8, a_im8)


def _s5_da(xs, gx, *, n_tok):
    def body(x_ref, g_ref, o_ref):
        t = lax.broadcasted_iota(jnp.int32, (n_tok, LANE), 0)
        sub = lax.broadcasted_iota(jnp.int32, (8, LANE), 0)

        def prev(v):
            return (jnp.where(t >= 8, pltpu.roll(v, 8, 0), 0.0),
                    jnp.where(sub > 0, pltpu.roll(v[n_tok - 8:, :], 1, 0), 0.0))

        (xr, hr), (xi, hi) = prev(x_ref[0, 0]), prev(x_ref[1, 0])
        gr, gi = g_ref[0, 0], g_ref[1, 0]
        gr0, gi0 = gr[0:8, :], gi[0:8, :]
        o_ref[0, 0:1, :] = (jnp.sum(xr * gr + xi * gi, axis=0, keepdims=True)
                            + jnp.sum(hr * gr0 + hi * gi0, axis=0, keepdims=True))
        o_ref[0, 1:2, :] = (jnp.sum(xr * gi - xi * gr, axis=0, keepdims=True)
                            + jnp.sum(hr * gi0 - hi * gr0, axis=0, keepdims=True))

    blk = pl.BlockSpec((2, 1, n_tok, LANE), lambda g: (0, g, 0, 0))
    return pl.pallas_call(
        body, grid=(S5_BLOCKS,), in_specs=[blk, blk], out_specs=pl.BlockSpec((1, 2, LANE), lambda g: (g, 0, 0)),
        out_shape=jax.ShapeDtypeStruct((S5_BLOCKS, 2, LANE), f32), name="s5_da",
        compiler_params=pltpu.CompilerParams(dimension_semantics=("arbitrary",), vmem_limit_bytes=VMEM_LIMIT),
    )(xs, gx)


S5_Q = 4


def _bd8(t):
    _, a, b = t.shape
    t = t.reshape(S5_Q, 8, a, 1, b)
    eye = jnp.eye(8, dtype=jnp.bool_).reshape(1, 8, 1, 8, 1)
    return jnp.where(eye, jnp.broadcast_to(t, (S5_Q, 8, a, 8, b)), jnp.zeros((), t.dtype)).reshape(S5_Q, 8 * a, 8 * b)


def _bd8_diag(m, a, b):
    m = m.reshape(S5_Q, 8, a, 8, b)
    eye = jnp.eye(8, dtype=jnp.bool_).reshape(1, 8, 1, 8, 1)
    return jnp.sum(jnp.where(eye, m, 0.0), axis=3).reshape(S5_GROUPS, a, b)


def _s5_expand(name, a, a_blk0, wq, *, n_tok):
    def body(a_ref, w_ref, o_ref):
        r = _nn(a_ref[...], w_ref[...])
        for k in range(4):
            o_ref[k] = r[:, k * LANE:(k + 1) * LANE]

    return pl.pallas_call(
        body, grid=(2, S5_Q),
        in_specs=[pl.BlockSpec((n_tok, LANE), lambda ri, q: (0, a_blk0 + q)),
                  pl.BlockSpec((None, None, LANE, 512), lambda ri, q: (ri, q, 0, 0))],
        out_specs=pl.BlockSpec((None, 4, n_tok, LANE), lambda ri, q: (ri, q, 0, 0)),
        out_shape=jax.ShapeDtypeStruct((2, S5_BLOCKS, n_tok, LANE), f32), name=name,
        compiler_params=pltpu.CompilerParams(dimension_semantics=("arbitrary", "arbitrary"), vmem_limit_bytes=VMEM_LIMIT),
    )(a, wq)


def _s5_contract(name, xs, wq, add, out_dtype, *, n_tok):
    def body(*refs):
        if add is None:
            x_ref, w_ref, o_ref, acc_ref = refs
        else:
            x_ref, w_ref, add_ref, o_ref, acc_ref = refs
        ri = pl.program_id(1)
        r = _nn(x_ref[0], w_ref[0:LANE, :])
        for k in range(1, 4):
            r = r + _nn(x_ref[k], w_ref[k * LANE:(k + 1) * LANE, :])

        @pl.when(ri == 0)
        def _():
            acc_ref[...] = r

        @pl.when(ri == 1)
        def _():
            tot = acc_ref[...] + r
            if add is not None:
                tot = tot + add_ref[...]
            o_ref[...] = tot.astype(o_ref.dtype)

    col = pl.BlockSpec((n_tok, LANE), lambda q, ri: (0, q))
    in_specs = [pl.BlockSpec((None, 4, n_tok, LANE), lambda q, ri: (ri, q, 0, 0)),
                pl.BlockSpec((None, None, 512, LANE), lambda q, ri: (ri, q, 0, 0))]
    args = [xs, wq]
    if add is not None:
        in_specs.append(col)
        args.append(add)
    return pl.pallas_call(
        body, grid=(S5_Q, 2), in_specs=in_specs, out_specs=col, out_shape=jax.ShapeDtypeStruct((n_tok, 512), out_dtype), name=name,
        scratch_shapes=[pltpu.VMEM((n_tok, LANE), f32)],
        compiler_params=pltpu.CompilerParams(dimension_semantics=("arbitrary", "arbitrary"), vmem_limit_bytes=VMEM_LIMIT),
    )(*args)


def _s5_wgrad_states(name, xs, d, *, n_tok):
    def body(x_ref, d_ref, o_ref):
        for k in range(4):
            o_ref[k * LANE:(k + 1) * LANE, :] = _tn(x_ref[k], d_ref[...])

    return pl.pallas_call(
        body, grid=(2, S5_Q),
        in_specs=[pl.BlockSpec((None, 4, n_tok, LANE), lambda ri, q: (ri, q, 0, 0)), pl.BlockSpec((n_tok, LANE), lambda ri, q: (0, q))],
        out_specs=pl.BlockSpec((None, None, 512, LANE), lambda ri, q: (ri, q, 0, 0)),
        out_shape=jax.ShapeDtypeStruct((2, S5_Q, 512, LANE), f32), name=name,
        compiler_params=pltpu.CompilerParams(dimension_semantics=("arbitrary", "arbitrary"), vmem_limit_bytes=VMEM_LIMIT),
    )(xs, d)


def _s5_wgrad_channels(name, a, a_blk0, gx, *, n_tok):
    def body(a_ref, g_ref, o_ref):
        for k in range(4):
            o_ref[:, k * LANE:(k + 1) * LANE] = _tn(a_ref[...], g_ref[k])

    return pl.pallas_call(
        body, grid=(2, S5_Q),
        in_specs=[pl.BlockSpec((n_tok, LANE), lambda ri, q: (0, a_blk0 + q)), pl.BlockSpec((None, 4, n_tok, LANE), lambda ri, q: (ri, q, 0, 0))],
        out_specs=pl.BlockSpec((None, None, LANE, 512), lambda ri, q: (ri, q, 0, 0)),
        out_shape=jax.ShapeDtypeStruct((2, S5_Q, LANE, 512), f32), name=name,
        compiler_params=pltpu.CompilerParams(dimension_semantics=("arbitrary", "arbitrary"), vmem_limit_bytes=VMEM_LIMIT),
    )(a, gx)


N_CHIPS = 4
_BIG_SHARD = {"w_in": (1, 1024, 1770, 1792), "mla_w_q_up": (1, 256, 192, 256), "mla_w_kv_up": (1, 128, 256, 256),
              "s5_w_glu": (0, 128, 512, 512), "w_branch_out": (0, 384, 1024, 1024), "w_out": (0, 256, 1024, 1024)}


def _to_shards(name, m):
    axis, r, c, cp = _BIG_SHARD[name]
    if axis == 0:
        return m.reshape(N_CHIPS, r, c)
    return jnp.stack([jnp.pad(m[:, j * c:(j + 1) * c], ((0, 0), (0, cp - c))) for j in range(N_CHIPS)])


def _from_shards(name, s):
    axis, r, c, cp = _BIG_SHARD[name]
    if axis == 0:
        return s.reshape(N_CHIPS * r, c)
    return jnp.concatenate([s[j, :, :c] for j in range(N_CHIPS)], axis=1)


def _pad_w_in(w):
    pieces, pos = [], 0
    for name in _PAD_ORDER:
        start, width, inner = _PAD[name]
        o0, ow = _ORIG[name]
        if start + inner > pos:
            pieces.append(jnp.zeros((w.shape[0], start + inner - pos), w.dtype))
        pieces.append(w[:, o0:o0 + ow])
        pos = start + inner + ow
    pieces.append(jnp.zeros((w.shape[0], NP - pos), w.dtype))
    return jnp.concatenate(pieces, axis=1)


def _prep_weights(small, big):
    per_layer = jax.vmap
    w = {}
    w["w_in_shards"] = big["w_in"]
    w["w_in"] = per_layer(lambda s: _pad_w_in(_from_shards("w_in", s)))(big["w_in"])

    def q_up(s):
        wq = _from_shards("mla_w_q_up", s).reshape(MLA_Q_RANK, HEADS, MLA_QK)
        return jnp.pad(wq, ((0, 0), (0, 0), (0, LANE - MLA_QK))).reshape(MLA_Q_RANK, HEADS * LANE)

    def kv_up(s):
        wkv = _from_shards("mla_w_kv_up", s).reshape(MLA_KV_RANK, HEADS, 128)
        wk = jnp.pad(wkv[:, :, :64], ((0, 0), (0, 0), (0, 64))).reshape(MLA_KV_RANK, HEADS * LANE)
        return jnp.concatenate([wk, wkv[:, :, 64:].reshape(MLA_KV_RANK, 512)], axis=1)

    w["wq"] = per_layer(q_up)(big["mla_w_q_up"])
    w["wkv"] = per_layer(kv_up)(big["mla_w_kv_up"])
    for name, key in (("w_glu", "s5_w_glu"), ("wo", "w_branch_out"), ("w_out", "w_out")):
        w[name] = per_layer(functools.partial(_from_shards, key))(big[key])
    row = lambda a: a.astype(f32)[:, None, :]
    lanes = lambda a, n: jnp.pad(row(a), ((0, 0), (0, 0), (0, LANE - n)))
    w["norm_g"] = row(small["norm_g"])
    w["qa_g"] = row(small["mla_q_a_norm"])
    w["kva_g"] = row(small["mla_kv_a_norm"])
    w["qn_g"] = lanes(small["mla_q_norm"], MLA_QK)
    w["kn_g"] = lanes(small["mla_k_norm"], MLA_QK)
    w["fq_g"] = jnp.tile(row(small["fox_q_norm"]), (1, 1, 2))
    w["fk_g"] = jnp.tile(row(small["fox_k_norm"]), (1, 1, 2))
    w["b_f"] = lanes(small["fox_b_f"], HEADS)
    w["lr"] = small["s5_lambda_re"].reshape(DEPTH, S5_LANES, 1)
    w["li"] = small["s5_lambda_im"].reshape(DEPTH, S5_LANES, 1)
    w["ldt"] = jnp.repeat(small["s5_log_dt"], S5_STATE, axis=1).reshape(DEPTH, S5_LANES, 1)
    w["b_re"] = small["s5_b_re"].reshape(DEPTH, S5_LANES, S5_GROUP)
    w["b_im"] = small["s5_b_im"].reshape(DEPTH, S5_LANES, S5_GROUP)
    w["s5_d"] = row(small["s5_d"])
    w["b_glu"] = row(small["s5_b_glu"])
    a_re, a_im, bb_re, bb_im = _s5_param_fwd(w["lr"], w["li"], w["ldt"], w["b_re"], w["b_im"])
    per_group = lambda m: m.reshape(S5_GROUPS, S5_STATE, S5_GROUP)
    pair = lambda f: per_layer(lambda re, im: jnp.stack([f(re), f(im)]).astype(bf16))
    c_re, c_im = small["s5_c_re"], -small["s5_c_im"]
    w["b_cn"] = pair(lambda m: _bd8(jnp.swapaxes(per_group(m), 1, 2)))(bb_re, bb_im)
    w["b_nc"] = pair(lambda m: _bd8(per_group(m)))(bb_re, bb_im)
    w["c_nc"] = pair(lambda m: _bd8(jnp.swapaxes(m, 1, 2)))(c_re, c_im)
    w["c_cn"] = pair(_bd8)(c_re, c_im)
    sublanes = lambda a: jnp.broadcast_to(a.reshape(DEPTH, S5_BLOCKS, 1, LANE), (DEPTH, S5_BLOCKS, 8, LANE))
    w["a_re8"], w["a_im8"], w["a_im8_neg"] = sublanes(a_re), sublanes(a_im), sublanes(-a_im)
    return w


def _fox_halves(x, lane):
    sq = x * x
    lo = jnp.sum(jnp.where(lane < 64, sq, 0.0), axis=-1, keepdims=True)
    hi = jnp.sum(sq, axis=-1, keepdims=True) - lo
    return jnp.where(lane < 64, lax.rsqrt(lo * (1.0 / 64) + EPS), lax.rsqrt(hi * (1.0 / 64) + EPS))


def _fox_halves_bwd(dy, x, r, g, lane):
    xh = x * r
    dxh = dy * g
    pr = dxh * xh
    lo = jnp.sum(jnp.where(lane < 64, pr, 0.0), axis=-1, keepdims=True)
    hi = jnp.sum(pr, axis=-1, keepdims=True) - lo
    mean = jnp.where(lane < 64, lo, hi) * (1.0 / 64)
    return r * (dxh - xh * mean), jnp.sum(dy * xh, axis=0, keepdims=True)


def _mla_recompute(cq, ckv, kpe, c, s1, s2, qa_g, kva_g, wq, wkv):
    cqn, r_cq = _rms(cq, qa_g, MLA_Q_RANK)
    ckvn, r_ckv = _rms(ckv, kva_g, MLA_KV_RANK)
    cqn_b = cqn.astype(bf16)
    ckvn_b = ckvn.astype(bf16)
    q_raw = _nn(cqn_b, wq)
    kv_raw = _nn(ckvn_b, wkv)
    kpe_rot = _rope(kpe, c, s1, s2)
    return cqn_b, r_cq, ckvn_b, r_ckv, q_raw, kv_raw, kpe_rot


def _layer_fwd(x, w, rope_tabs, n_tok):
    c_tab, s1_tab, s2_tab = rope_tabs
    saved = {"x": x}

    def norm_body(x_ref, g_ref, h_ref):
        h_ref[...] = _rms(x_ref[...], g_ref[...], D_MODEL)[0].astype(bf16)

    (h,) = _rowwise("norm_fwd", norm_body, n_tok, [(x, D_MODEL, 0)], [w["norm_g"]], [(D_MODEL, bf16)], [])
    proj = _mm_nn("in_proj", h, w["w_in"], m=n_tok, n=NP, k=D_MODEL, tm=n_tok, tn=512, tk=D_MODEL)
    saved["h"], saved["proj"] = h, proj

    def mla_prep_body(cq_ref, ckv_ref, kpe_ref, c_ref, s1_ref, s2_ref, qa_ref, kva_ref, wq_ref, wkv_ref, qn_g_ref, kn_g_ref,
                      qn_ref, kn_ref, v_ref):
        c, s1, s2 = c_ref[...], s1_ref[...], s2_ref[...]
        _, _, _, _, q_raw, kv_raw, kpe_rot = _mla_recompute(cq_ref[...], ckv_ref[...], kpe_ref[...], c, s1, s2,
                                                            qa_ref[...], kva_ref[...], wq_ref[...], wkv_ref[...])
        for hd in range(HEADS):
            sl = slice(hd * LANE, (hd + 1) * LANE)
            qn_ref[:, sl] = _rms(_rope(q_raw[:, sl], c, s1, s2), qn_g_ref[...], MLA_QK)[0].astype(bf16)
            kn_ref[:, sl] = _rms(kv_raw[:, sl] + kpe_rot, kn_g_ref[...], MLA_QK)[0].astype(bf16)
        v_ref[...] = kv_raw[:, HEADS * LANE:].astype(bf16)

    qn, kn, v_mla = _rowwise(
        "mla_prep", mla_prep_body, n_tok,
        [(proj, *_seg("cq")), (proj, *_seg("ckv")), (proj, *_seg("kpe")), (c_tab, LANE, 0), (s1_tab, LANE, 0), (s2_tab, LANE, 0)],
        [w["qa_g"], w["kva_g"], w["wq"], w["wkv"], w["qn_g"], w["kn_g"]],
        [(HEADS * LANE, bf16), (HEADS * LANE, bf16), (512, bf16)], [])
    y_mla, lse_mla = _attn_fwd("mla_attn_fwd", qn, kn, v_mla, None, mla=True, n_tok=n_tok)
    saved.update(qn=qn, kn=kn, v_mla=v_mla, y_mla=y_mla, lse_mla=lse_mla)

    def fox_prep_body(fq_ref, fk_ref, fv_ref, ff_ref, qg_ref, kg_ref, bf_ref, fqn_ref, fkn_ref, fvb_ref, logf_ref):
        lane = lax.broadcasted_iota(jnp.int32, (TOK, LANE), 1)
        for blk in range(4):
            sl = slice(blk * LANE, (blk + 1) * LANE)
            xq = fq_ref[:, sl]
            fqn_ref[:, sl] = (xq * _fox_halves(xq, lane) * qg_ref[...]).astype(bf16)
            xk = fk_ref[:, sl]
            fkn_ref[:, sl] = (xk * _fox_halves(xk, lane) * kg_ref[...]).astype(bf16)
        fvb_ref[...] = fv_ref[...].astype(bf16)
        z = ff_ref[...] + bf_ref[...]
        logf_ref[...] = jnp.minimum(z, 0.0) - jnp.log(1.0 + jnp.exp(-jnp.abs(z)))

    fqn, fkn, fvb, logf = _rowwise(
        "fox_prep", fox_prep_body, n_tok,
        [(proj, *_seg("fq")), (proj, *_seg("fk")), (proj, *_seg("fv")), (proj, *_seg("ff"))],
        [w["fq_g"], w["fk_g"], w["b_f"]],
        [(512, bf16), (512, bf16), (512, bf16), (LANE, f32)], [])

    def cum_body(x_ref, cum_ref):
        x = x_ref[...]
        t = lax.broadcasted_iota(jnp.int32, x.shape, 0)
        s = 1
        while s < n_tok:
            x = x + jnp.where(t >= s, pltpu.roll(x, s, 0), 0.0)
            s *= 2
        for hd in range(HEADS):
            cum_ref[hd] = jnp.broadcast_to(x[:, hd:hd + 1], (n_tok, LANE))

    cum_b = pl.pallas_call(cum_body, out_shape=jax.ShapeDtypeStruct((HEADS, n_tok, LANE), f32), name="fox_cum")(logf)
    y_fox, lse_fox = _attn_fwd("fox_attn_fwd", fqn, fkn, fvb, cum_b, mla=False, n_tok=n_tok)
    saved.update(fqn=fqn, fkn=fkn, fvb=fvb, cum_b=cum_b, y_fox=y_fox, lse_fox=lse_fox)

    u_w, u_blk = _seg("s5u")
    u_streams = _to_streams(proj[:, u_blk * u_w:(u_blk + 1) * u_w])
    bu = _s5_expand("s5_bu", u_streams, 0, w["b_cn"], n_tok=n_tok)
    xs = _s5_scan("s5_scan_fwd", bu, w["a_re8"], w["a_im8"], reverse=False, n_tok=n_tok)
    ylin = _from_streams(_s5_contract("s5_y", xs, w["c_nc"], None, f32, n_tok=n_tok))

    def s5_post_body(yl_ref, u_ref, d_ref, wg_ref, bg_ref, out_ref):
        y = yl_ref[...] + d_ref[...] * u_ref[...]
        z, _ = _gelu(y)
        out_ref[...] = z * _sigmoid(_nn(z, wg_ref[...]) + bg_ref[...])

    (y_s5,) = _rowwise("s5_post", s5_post_body, n_tok, [(ylin, 512, 0), (proj, u_w, u_blk)],
                       [w["s5_d"], w["w_glu"], w["b_glu"]], [(512, f32)], [])
    saved.update(xs=xs, ylin=ylin, y_s5=y_s5, u_streams=u_streams)

    def merge_body(ym_ref, yf_ref, ys_ref, gm_ref, gf_ref, gs_ref, mm_ref, mf_ref, ms_ref, x_ref, wo_ref, wout_ref, out_ref):
        merged = jnp.zeros((TOK, D_MODEL), f32)
        for b, (y_ref, g_ref, m_ref) in enumerate(((ym_ref, gm_ref, mm_ref), (yf_ref, gf_ref, mf_ref), (ys_ref, gs_ref, ms_ref))):
            g = g_ref[...]
            a = y_ref[...] * (g * _sigmoid(g))
            merged = merged + _sigmoid(m_ref[...]) * _nn(a, wo_ref[b * 512:(b + 1) * 512, :])
        out_ref[...] = x_ref[...] + _nn(merged, wout_ref[...])

    (out,) = _rowwise(
        "merge_fwd", merge_body, n_tok,
        [(y_mla, 512, 0), (y_fox, 512, 0), (y_s5, 512, 0), (proj, *_seg("g_mla")), (proj, *_seg("g_fox")), (proj, *_seg("g_s5")),
         (proj, *_seg("m_mla")), (proj, *_seg("m_fox")), (proj, *_seg("m_s5")), (x, D_MODEL, 0)],
        [w["wo"], w["w_out"]], [(D_MODEL, f32)], [])
    return out, saved


def _layer_bwd(dout, w, sv, rope_tabs, n_tok):
    c_tab, s1_tab, s2_tab = rope_tabs
    proj, x = sv["proj"], sv["x"]
    grads = {}

    def merge_bwd_body(ym_ref, yf_ref, ys_ref, gm_ref, gf_ref, gs_ref, mm_ref, mf_ref, ms_ref, do_ref, wo_ref, wout_ref,
                       dym_ref, dyf_ref, dys_ref, dgm_ref, dgf_ref, dgs_ref, dmm_ref, dmf_ref, dms_ref, dwo_ref, dwout_ref):
        do = do_ref[...]
        branches = ((ym_ref, gm_ref, mm_ref, dym_ref, dgm_ref, dmm_ref), (yf_ref, gf_ref, mf_ref, dyf_ref, dgf_ref, dmf_ref),
                    (ys_ref, gs_ref, ms_ref, dys_ref, dgs_ref, dms_ref))
        acts, outs, sigs = [], [], []
        merged = jnp.zeros((TOK, D_MODEL), f32)
        for b, (y_ref, g_ref, m_ref, _, _, _) in enumerate(branches):
            g = g_ref[...]
            a = (y_ref[...] * (g * _sigmoid(g))).astype(bf16)
            o = _nn(a, wo_ref[b * 512:(b + 1) * 512, :])
            s = _sigmoid(m_ref[...])
            merged = merged + s * o
            acts.append(a)
            outs.append(o)
            sigs.append(s)
        dmerged = _nt(do, wout_ref[...])
        _accumulate(dwout_ref, _tn(merged, do))
        dwo = []
        for b, (y_ref, g_ref, m_ref, dy_ref, dg_ref, dm_ref) in enumerate(branches):
            s, o = sigs[b], outs[b]
            dm_ref[...] = (dmerged * o * s * (1.0 - s)).astype(bf16)
            d_o = dmerged * s
            da = _nt(d_o, wo_ref[b * 512:(b + 1) * 512, :])
            dwo.append(_tn(acts[b], d_o))
            g = g_ref[...]
            sg = _sigmoid(g)
            dy_ref[...] = da * (g * sg)
            dg_ref[...] = (da * y_ref[...] * (sg * (1.0 + g * (1.0 - sg)))).astype(bf16)
        _accumulate(dwo_ref, jnp.concatenate(dwo, axis=0))

    (dy_mla, dy_fox, dy_s5, dg_mla, dg_fox, dg_s5, dm_mla, dm_fox, dm_s5, dwo, dwout) = _rowwise(
        "merge_bwd", merge_bwd_body, n_tok,
        [(sv["y_mla"], 512, 0), (sv["y_fox"], 512, 0), (sv["y_s5"], 512, 0), (proj, *_seg("g_mla")), (proj, *_seg("g_fox")),
         (proj, *_seg("g_s5")), (proj, *_seg("m_mla")), (proj, *_seg("m_fox")), (proj, *_seg("m_s5")), (dout, D_MODEL, 0)],
        [w["wo"], w["w_out"]],
        [(512, f32)] * 3 + [(512, bf16)] * 3 + [(D_MODEL, bf16)] * 3, [((1536, D_MODEL), f32), ((D_MODEL, D_MODEL), f32)])
    grads["w_branch_out"], grads["w_out"] = dwo, dwout

    u_w, u_blk = _seg("s5u")

    def s5_post_bwd_body(yl_ref, u_ref, do_ref, d_ref, wg_ref, bg_ref, dyl_ref, dus_ref, dd_ref, dwg_ref, dbg_ref):
        u = u_ref[...]
        y = yl_ref[...] + d_ref[...] * u
        z, t = _gelu(y)
        s = _sigmoid(_nn(z, wg_ref[...]) + bg_ref[...])
        do = do_ref[...]
        dgl = do * z * s * (1.0 - s)
        dz = do * s + _nt(dgl, wg_ref[...])
        dy = dz * _gelu_grad(y, t)
        dyl_ref[...] = dy.astype(bf16)
        dus_ref[...] = dy * d_ref[...]
        _accumulate(dd_ref, jnp.sum(dy * u, axis=0, keepdims=True))
        _accumulate(dwg_ref, _tn(z, dgl))
        _accumulate(dbg_ref, jnp.sum(dgl, axis=0, keepdims=True))

    dylin, du_skip, dd, dwglu, dbglu = _rowwise(
        "s5_post_bwd", s5_post_bwd_body, n_tok, [(sv["ylin"], 512, 0), (proj, u_w, u_blk), (dy_s5, 512, 0)],
        [w["s5_d"], w["w_glu"], w["b_glu"]], [(512, bf16), (512, f32)], [((1, 512), f32), ((512, 512), f32), ((1, 512), f32)])
    grads["s5_d"], grads["s5_w_glu"], grads["s5_b_glu"] = dd.reshape(512), dwglu, dbglu.reshape(512)

    dylin = _to_streams(dylin)
    dxs = _s5_expand("s5_dxs", dylin, 0, w["c_cn"], n_tok=n_tok)
    dc_nc = _s5_wgrad_states("s5_dc", sv["xs"], dylin, n_tok=n_tok)
    gx = _s5_scan("s5_scan_bwd", dxs, w["a_re8"], w["a_im8_neg"], reverse=True, n_tok=n_tok)
    da = _s5_da(sv["xs"], gx, n_tok=n_tok)
    ds5u = _from_streams(_s5_contract("s5_du", gx, w["b_nc"], _to_streams(du_skip), bf16, n_tok=n_tok))
    db_cn = _s5_wgrad_channels("s5_db", sv["u_streams"], 0, gx, n_tok=n_tok)
    diag_b = lambda m: jnp.swapaxes(_bd8_diag(m, S5_GROUP, S5_STATE), 1, 2).reshape(S5_LANES, S5_GROUP)
    diag_c = lambda m: jnp.swapaxes(_bd8_diag(m, S5_STATE, S5_GROUP), 1, 2)
    dlr, dli, dldt, db_re, db_im = _s5_param_bwd(
        w["lr"], w["li"], w["ldt"], w["b_re"], w["b_im"], da[:, 0, :].reshape(S5_LANES, 1), da[:, 1, :].reshape(S5_LANES, 1),
        diag_b(db_cn[0]), diag_b(db_cn[1]))
    grads["s5_lambda_re"] = dlr.reshape(S5_GROUPS, S5_STATE)
    grads["s5_lambda_im"] = dli.reshape(S5_GROUPS, S5_STATE)
    grads["s5_log_dt"] = dldt.reshape(S5_GROUPS)
    grads["s5_b_re"] = db_re.reshape(S5_GROUPS, S5_STATE, S5_GROUP)
    grads["s5_b_im"] = db_im.reshape(S5_GROUPS, S5_STATE, S5_GROUP)
    grads["s5_c_re"] = diag_c(dc_nc[0])
    grads["s5_c_im"] = -diag_c(dc_nc[1])

    dfqn, dfkn, dfv, dck, dcq = _attn_bwd("fox_attn_bwd", sv["fqn"], sv["fkn"], sv["fvb"], sv["y_fox"], sv["lse_fox"], dy_fox,
                                          sv["cum_b"], mla=False, n_tok=n_tok)
    dcq = jnp.pad(dcq[:, :2, :].reshape(HEADS, n_tok).T, ((0, 0), (0, LANE - HEADS)))

    def fox_gate_bwd_body(dk_ref, dq_ref, ff_ref, bf_ref, dff_ref, dbf_ref):
        xg = dk_ref[...] + dq_ref[...]
        t = lax.broadcasted_iota(jnp.int32, xg.shape, 0)
        s = 1
        while s < n_tok:
            xg = xg + jnp.where(t < n_tok - s, pltpu.roll(xg, n_tok - s, 0), 0.0)
            s *= 2
        dff = xg * _sigmoid(-(ff_ref[...] + bf_ref[...]))
        dff_ref[...] = dff.astype(bf16)
        dbf_ref[...] = jnp.sum(dff, axis=0, keepdims=True)

    ff_w, ff_blk = _seg("ff")
    dff, dbf = pl.pallas_call(
        fox_gate_bwd_body, grid=(1,),
        in_specs=[pl.BlockSpec((n_tok, LANE), lambda i: (0, 0)), pl.BlockSpec((n_tok, LANE), lambda i: (0, 0)),
                  pl.BlockSpec((n_tok, ff_w), lambda i: (0, ff_blk)), pl.BlockSpec((1, LANE), lambda i: (0, 0))],
        out_specs=[pl.BlockSpec((n_tok, LANE), lambda i: (0, 0)), pl.BlockSpec((1, LANE), lambda i: (0, 0))],
        out_shape=[jax.ShapeDtypeStruct((n_tok, LANE), bf16), jax.ShapeDtypeStruct((1, LANE), f32)], name="fox_gate_bwd",
    )(dck, dcq, proj, w["b_f"])
    grads["fox_b_f"] = dbf[0, :HEADS]

    def fox_prep_bwd_body(fq_ref, fk_ref, dqn_ref, dkn_ref, dv_ref, qg_ref, kg_ref, dfq_ref, dfk_ref, dfv_ref, dqg_ref, dkg_ref):
        lane = lax.broadcasted_iota(jnp.int32, (TOK, LANE), 1)
        dqg = jnp.zeros((1, LANE), f32)
        dkg = jnp.zeros((1, LANE), f32)
        for blk in range(4):
            sl = slice(blk * LANE, (blk + 1) * LANE)
            xq = fq_ref[:, sl]
            dx, dg = _fox_halves_bwd(dqn_ref[:, sl], xq, _fox_halves(xq, lane), qg_ref[...], lane)
            dfq_ref[:, sl] = dx.astype(bf16)
            dqg = dqg + dg
            xk = fk_ref[:, sl]
            dx, dg = _fox_halves_bwd(dkn_ref[:, sl], xk, _fox_halves(xk, lane), kg_ref[...], lane)
            dfk_ref[:, sl] = dx.astype(bf16)
            dkg = dkg + dg
        dfv_ref[...] = dv_ref[...].astype(bf16)
        _accumulate(dqg_ref, dqg + pltpu.roll(dqg, 64, 1))
        _accumulate(dkg_ref, dkg + pltpu.roll(dkg, 64, 1))

    dfq, dfk, dfvb, dfqg, dfkg = _rowwise(
        "fox_prep_bwd", fox_prep_bwd_body, n_tok,
        [(proj, *_seg("fq")), (proj, *_seg("fk")), (dfqn, 512, 0), (dfkn, 512, 0), (dfv, 512, 0)],
        [w["fq_g"], w["fk_g"]], [(512, bf16)] * 3, [((1, LANE), f32)] * 2)
    grads["fox_q_norm"], grads["fox_k_norm"] = dfqg[0, :FOX_DIM], dfkg[0, :FOX_DIM]

    dqn, dkn, dv_mla = _attn_bwd("mla_attn_bwd", sv["qn"], sv["kn"], sv["v_mla"], sv["y_mla"], sv["lse_mla"], dy_mla,
                                 None, mla=True, n_tok=n_tok)

    def mla_prep_bwd_body(cq_ref, ckv_ref, kpe_ref, c_ref, s1_ref, s2_ref, dqn_ref, dkn_ref, dv_ref,
                          qa_ref, kva_ref, wq_ref, wkv_ref, qn_g_ref, kn_g_ref,
                          dcq_ref, dckv_ref, dkpe_ref, dwq_ref, dwkv_ref, dqa_ref, dkva_ref, dqng_ref, dkng_ref):
        c, s1, s2 = c_ref[...], s1_ref[...], s2_ref[...]
        cq, ckv = cq_ref[...], ckv_ref[...]
        cqn_b, r_cq, ckvn_b, r_ckv, q_raw, kv_raw, kpe_rot = _mla_recompute(
            cq, ckv, kpe_ref[...], c, s1, s2, qa_ref[...], kva_ref[...], wq_ref[...], wkv_ref[...])
        lane = lax.broadcasted_iota(jnp.int32, (TOK, LANE), 1)
        dq_raw, dk_raw = [], []
        dkpe_rot = jnp.zeros((TOK, LANE), f32)
        dqng = jnp.zeros((1, LANE), f32)
        dkng = jnp.zeros((1, LANE), f32)
        for hd in range(HEADS):
            sl = slice(hd * LANE, (hd + 1) * LANE)
            q_rot = _rope(q_raw[:, sl], c, s1, s2)
            r = lax.rsqrt(jnp.sum(q_rot * q_rot, axis=-1, keepdims=True) * (1.0 / MLA_QK) + EPS)
            dx, dg = _rms_bwd(dqn_ref[:, sl], q_rot, r, qn_g_ref[...], MLA_QK)
            dqng = dqng + dg
            dq_raw.append(_rope_t(dx, c, s1, s2))
            k_full = kv_raw[:, sl] + kpe_rot
            r = lax.rsqrt(jnp.sum(k_full * k_full, axis=-1, keepdims=True) * (1.0 / MLA_QK) + EPS)
            dx, dg = _rms_bwd(dkn_ref[:, sl], k_full, r, kn_g_ref[...], MLA_QK)
            dkng = dkng + dg
            dk_raw.append(jnp.where(lane < 64, dx, 0.0))
            dkpe_rot = dkpe_rot + dx
        dkpe = _rope_t(dkpe_rot, c, s1, s2)
        dkpe_ref[...] = jnp.where(jnp.logical_and(lane >= 64, lane < 64 + ROPE), dkpe, 0.0).astype(bf16)
        dq_raw = jnp.concatenate(dq_raw, axis=1).astype(bf16)
        dkv_raw = jnp.concatenate(dk_raw + [dv_ref[...]], axis=1).astype(bf16)
        dcqn = _nt(dq_raw, wq_ref[...])
        dckvn = _nt(dkv_raw, wkv_ref[...])
        dx, dg = _rms_bwd(dcqn, cq, r_cq, qa_ref[...], MLA_Q_RANK)
        dcq_ref[...] = dx.astype(bf16)
        _accumulate(dqa_ref, dg)
        dx, dg = _rms_bwd(dckvn, ckv, r_ckv, kva_ref[...], MLA_KV_RANK)
        dckv_ref[...] = dx.astype(bf16)
        _accumulate(dkva_ref, dg)
        _accumulate(dwq_ref, _tn(cqn_b, dq_raw))
        _accumulate(dwkv_ref, _tn(ckvn_b, dkv_raw))
        _accumulate(dqng_ref, dqng)
        _accumulate(dkng_ref, dkng)

    dcq, dckv, dkpe, dwq, dwkv, dqa, dkva, dqng, dkng = _rowwise(
        "mla_prep_bwd", mla_prep_bwd_body, n_tok,
        [(proj, *_seg("cq")), (proj, *_seg("ckv")), (proj, *_seg("kpe")), (c_tab, LANE, 0), (s1_tab, LANE, 0), (s2_tab, LANE, 0),
         (dqn, HEADS * LANE, 0), (dkn, HEADS * LANE, 0), (dv_mla, 512, 0)],
        [w["qa_g"], w["kva_g"], w["wq"], w["wkv"], w["qn_g"], w["kn_g"]],
        [(MLA_Q_RANK, bf16), (LANE, bf16), (LANE, bf16)],
        [((MLA_Q_RANK, HEADS * LANE), f32), ((MLA_KV_RANK, HEADS * LANE + 512), f32), ((1, MLA_Q_RANK), f32),
         ((1, MLA_KV_RANK), f32), ((1, LANE), f32), ((1, LANE), f32)])
    grads["mla_w_q_up"] = dwq.reshape(MLA_Q_RANK, HEADS, LANE)[:, :, :MLA_QK].reshape(MLA_Q_RANK, HEADS * MLA_QK)
    dwk = dwkv[:, :HEADS * LANE].reshape(MLA_KV_RANK, HEADS, LANE)[:, :, :64]
    dwv = dwkv[:, HEADS * LANE:].reshape(MLA_KV_RANK, HEADS, 64)
    grads["mla_w_kv_up"] = jnp.concatenate([dwk, dwv], axis=2).reshape(MLA_KV_RANK, HEADS * 128)
    grads["mla_q_a_norm"], grads["mla_kv_a_norm"] = dqa.reshape(-1), dkva.reshape(-1)
    grads["mla_q_norm"], grads["mla_k_norm"] = dqng[0, :MLA_QK], dkng[0, :MLA_QK]

    _, _, shard_c, shard_cp = _BIG_SHARD["w_in"]
    kpe0 = _PAD["kpe"][2]
    pieces = [dcq, dckv, dkpe[:, kpe0:kpe0 + ROPE], dfq, dfk, dfvb, dff[:, :HEADS], ds5u, dg_mla, dg_fox, dg_s5,
              dm_mla, dm_fox, dm_s5]
    gap = jnp.zeros((n_tok, shard_cp - shard_c), bf16)
    cut, pos = [], 0
    for p in pieces:
        start = 0
        while start < p.shape[1]:
            take = min(p.shape[1] - start, shard_c - pos % shard_c)
            cut.append(p[:, start:start + take])
            start, pos = start + take, pos + take
            if pos % shard_c == 0:
                cut.append(gap)
    dproj = jnp.concatenate(cut, axis=1)
    ct = 256
    per = shard_cp // ct
    dh = _mm("in_proj_dgrad", dproj, w["w_in_shards"], mode="nt", grid=(1, 1, N_CHIPS * per),
             a_spec=pl.BlockSpec((n_tok, ct), lambda i, j, kk: (0, kk)),
             b_spec=pl.BlockSpec((None, D_MODEL, ct), lambda i, j, kk: (kk // per, 0, kk % per)),
             o_spec=pl.BlockSpec((n_tok, D_MODEL), lambda i, j, kk: (0, 0)),
             out_shape=jax.ShapeDtypeStruct((n_tok, D_MODEL), f32), acc_shape=(n_tok, D_MODEL))
    grads["w_in"] = _mm("in_proj_wgrad", sv["h"], dproj, mode="tn", grid=(1, N_CHIPS * per, 1),
                        a_spec=pl.BlockSpec((n_tok, D_MODEL), lambda i, j, kk: (0, 0)),
                        b_spec=pl.BlockSpec((n_tok, ct), lambda i, j, kk: (0, j)),
                        o_spec=pl.BlockSpec((None, D_MODEL, ct), lambda i, j, kk: (j // per, 0, j % per)),
                        out_shape=jax.ShapeDtypeStruct((N_CHIPS, D_MODEL, shard_cp), f32), acc_shape=(D_MODEL, ct))

    def norm_bwd_body(dh_ref, x_ref, do_ref, g_ref, dx_ref, dg_ref):
        xv = x_ref[...]
        r = lax.rsqrt(jnp.sum(xv * xv, axis=-1, keepdims=True) * (1.0 / D_MODEL) + EPS)
        dx, dg = _rms_bwd(dh_ref[...], xv, r, g_ref[...], D_MODEL)
        dx_ref[...] = do_ref[...] + dx
        _accumulate(dg_ref, dg)

    dx, dng = _rowwise("norm_bwd", norm_bwd_body, n_tok, [(dh, D_MODEL, 0), (x, D_MODEL, 0), (dout, D_MODEL, 0)],
                       [w["norm_g"]], [(D_MODEL, f32)], [((1, D_MODEL), f32)])
    grads["norm_g"] = dng.reshape(D_MODEL)
    return dx, grads


def _rope_tables(positions):
    inv = 1.0 / (ROPE_THETA ** (jnp.arange(0, ROPE, 2, dtype=f32) / ROPE))
    ang = positions.astype(f32).reshape(-1, 1) * inv
    cos, sin = jnp.cos(ang), jnp.sin(ang)
    n = ang.shape[0]
    z16, z32, z64 = jnp.zeros((n, 16), f32), jnp.zeros((n, 32), f32), jnp.zeros((n, 64), f32)
    c = jnp.concatenate([jnp.ones((n, 64), f32), cos, cos, z32], axis=1)
    s1 = jnp.concatenate([z64, -sin, z16, z32], axis=1)
    s2 = jnp.concatenate([z64, z16, sin, z32], axis=1)
    return c, s1, s2


BIG = ("w_in", "mla_w_q_up", "mla_w_kv_up", "s5_w_glu", "w_branch_out", "w_out")
SMALL = ("norm_g", "mla_q_a_norm", "mla_kv_a_norm", "mla_q_norm", "mla_k_norm", "fox_b_f", "fox_q_norm", "fox_k_norm",
         "s5_lambda_re", "s5_lambda_im", "s5_log_dt", "s5_b_re", "s5_b_im", "s5_c_re", "s5_c_im", "s5_d", "s5_b_glu")
WEIGHTS = ("norm_g", "w_in", "mla_q_a_norm", "mla_w_q_up", "mla_kv_a_norm", "mla_w_kv_up", "mla_q_norm", "mla_k_norm",
           "fox_b_f", "fox_q_norm", "fox_k_norm", "s5_lambda_re", "s5_lambda_im", "s5_log_dt", "s5_b_re", "s5_b_im",
           "s5_c_re", "s5_c_im", "s5_d", "s5_w_glu", "s5_b_glu", "w_branch_out", "w_out")


def _local_step(x, positions, loss_target, small, big):
    n_tok = x.shape[0]
    tabs = _rope_tables(positions)
    ws, saves = [], []
    hcur = x
    stacked = _prep_weights(small, big)
    for l in range(DEPTH):
        w = {k: v[l] for k, v in stacked.items()}
        hcur, sv = _layer_fwd(hcur, w, tabs, n_tok)
        ws.append(w)
        saves.append(sv)

    def loss_body(y_ref, t_ref, d_ref, l_ref):
        err = y_ref[...] - t_ref[...]
        d_ref[...] = err * (1.0 / D_MODEL)
        tot = jnp.sum(jnp.sum(err * err, axis=-1, keepdims=True), axis=0, keepdims=True)
        _accumulate(l_ref, jnp.broadcast_to(tot * (0.5 / D_MODEL), (1, LANE)))

    dcur, loss = _rowwise("loss", loss_body, n_tok, [(hcur, D_MODEL, 0), (loss_target, D_MODEL, 0)], [], [(D_MODEL, f32)],
                          [((1, LANE), f32)])
    layer_grads = [None] * DEPTH
    for l in reversed(range(DEPTH)):
        dcur, layer_grads[l] = _layer_bwd(dcur, ws[l], saves[l], tabs, n_tok)
    grads = {n: jnp.stack([layer_grads[l][n] for l in range(DEPTH)]) for n in WEIGHTS}
    return loss[0, 0], dcur, grads


N_DEV = 8
_ANY = pl.BlockSpec(memory_space=pl.ANY)
_MESH = pl.DeviceIdType.MESH


def _all_gather8(name, blk):
    m = blk.shape[0]

    def body(x_ref, out_ref, send_sems, recv_sems, local_sem):
        x, y, c = lax.axis_index("x"), lax.axis_index("y"), lax.axis_index("c")
        me, sibling = (x, y, c), (x, y, 1 - c)
        chips = [(1 - x, y), (x, 1 - y), (1 - x, 1 - y)]

        def slot(px, py, pc):
            return out_ref.at[4 * px + 2 * py + pc]

        def copy(k, block, to, src=None):
            return pltpu.make_async_remote_copy(
                src_ref=slot(*block) if src is None else src, dst_ref=slot(*block),
                send_sem=send_sems.at[k], recv_sem=recv_sems.at[k], device_id=to, device_id_type=_MESH)

        mine = pltpu.make_async_copy(x_ref, slot(*me), local_sem)
        mine.start()
        first = [copy(0, me, sibling, src=x_ref)]
        first += [copy(1 + j, me, (*chip, c), src=x_ref) for j, chip in enumerate(chips)]
        for cp in first:
            cp.start()
        passed = [copy(4 + j, (*chip, c), sibling) for j, chip in enumerate(chips)]
        for j, chip in enumerate(chips):
            copy(1 + j, (*chip, c), me).wait_recv()
            passed[j].start()
        copy(0, sibling, me).wait_recv()
        for j, chip in enumerate(chips):
            copy(4 + j, (*chip, 1 - c), me).wait_recv()
        for cp in first + passed:
            cp.wait_send()
        mine.wait()

    return pl.pallas_call(
        body, out_shape=jax.ShapeDtypeStruct((N_DEV, m, LANE), blk.dtype), in_specs=[_ANY], out_specs=_ANY, name=name,
        scratch_shapes=[pltpu.SemaphoreType.DMA((7,)), pltpu.SemaphoreType.DMA((7,)), pltpu.SemaphoreType.DMA],
    )(blk)


def _gather_layers(name, shards):
    n = len(shards)

    def body(*refs):
        x_refs, out_refs = refs[:n], refs[n:2 * n]
        send_sems, recv_sems, local_sems = refs[2 * n:]
        x, y, c = lax.axis_index("x"), lax.axis_index("y"), lax.axis_index("c")
        me, sibling = (x, y, c), (x, y, 1 - c)
        xn, yn, dg = (1 - x, y, c), (x, 1 - y, c), (1 - x, 1 - y, c)
        relay_from = (x + (1 - c) * (1 - 2 * x), y + c * (1 - 2 * y), c)
        relay_to = (x + c * (1 - 2 * x), y + (1 - c) * (1 - 2 * y), c)

        def copy(w, k, block, to, src=None):
            px, py, pc = block
            slot = out_refs[w].at[pc, 2 * px + py]
            return pltpu.make_async_remote_copy(
                src_ref=slot if src is None else src, dst_ref=slot, send_sem=send_sems.at[7 * w + k],
                recv_sem=recv_sems.at[7 * w + k], device_id=to, device_id_type=_MESH)

        started, local = [], []
        for w in range(n):
            src = x_refs[w].at[c]
            mine = pltpu.make_async_copy(src, out_refs[w].at[c, 2 * x + y], local_sems.at[w])
            mine.start()
            local.append(mine)
            first = [copy(w, 0, me, sibling, src=src), copy(w, 1, me, xn, src=src), copy(w, 2, me, yn, src=src)]
            for cp in first:
                cp.start()
            started += first
        for w in range(n):
            copy(w, 1, xn, me).wait_recv()
            copy(w, 2, yn, me).wait_recv()
            onward = [copy(w, 3, relay_from, relay_to), copy(w, 4, xn, sibling), copy(w, 5, yn, sibling)]
            for cp in onward:
                cp.start()
            started += onward
        for w in range(n):
            copy(w, 3, dg, me).wait_recv()
            onward = copy(w, 6, dg, sibling)
            onward.start()
            started.append(onward)
        for w in range(n):
            copy(w, 0, sibling, me).wait_recv()
            for k, chip in ((4, xn), (5, yn), (6, dg)):
                copy(w, k, (chip[0], chip[1], 1 - c), me).wait_recv()
        for cp in started:
            cp.wait_send()
        for cp in local:
            cp.wait()

    return pl.pallas_call(
        body, out_shape=[jax.ShapeDtypeStruct((2, N_CHIPS) + s.shape[1:], s.dtype) for s in shards],
        in_specs=[_ANY] * n, out_specs=[_ANY] * n, name=name,
        scratch_shapes=[pltpu.SemaphoreType.DMA((7 * n,)), pltpu.SemaphoreType.DMA((7 * n,)), pltpu.SemaphoreType.DMA((n,))],
    )(*shards)


def _swap_layers(name, parts):
    n = len(parts)

    def body(*refs):
        p_refs, got_refs = refs[:n], refs[n:2 * n]
        send_sems, recv_sems = refs[2 * n:]
        x, y, c = lax.axis_index("x"), lax.axis_index("y"), lax.axis_index("c")
        copies = []
        for w in range(n):
            cp = pltpu.make_async_remote_copy(
                src_ref=p_refs[w].at[1 - c], dst_ref=got_refs[w], send_sem=send_sems.at[w], recv_sem=recv_sems.at[w],
                device_id=(x, y, 1 - c), device_id_type=_MESH)
            cp.start()
            copies.append(cp)
        for cp in copies:
            cp.wait()

    return pl.pallas_call(
        body, out_shape=[jax.ShapeDtypeStruct(p.shape[1:], p.dtype) for p in parts], in_specs=[_ANY] * n, out_specs=[_ANY] * n,
        name=name, scratch_shapes=[pltpu.SemaphoreType.DMA((n,)), pltpu.SemaphoreType.DMA((n,))],
    )(*parts)


def _scatter_to_chips(name, parts):
    n = len(parts)

    def body(*refs):
        p_refs, out_refs = refs[:n], refs[n:2 * n]
        send_sems, recv_sems, local_sems = refs[2 * n:]
        x, y, c = lax.axis_index("x"), lax.axis_index("y"), lax.axis_index("c")
        jme = 2 * x + y
        chips = [(1 - x, y), (x, 1 - y), (1 - x, 1 - y)]
        sends, local = [], []
        for w in range(n):
            mine = pltpu.make_async_copy(p_refs[w].at[jme], out_refs[w].at[jme], local_sems.at[w])
            mine.start()
            local.append(mine)
            for k, (tx, ty) in enumerate(chips):
                cp = pltpu.make_async_remote_copy(
                    src_ref=p_refs[w].at[2 * tx + ty], dst_ref=out_refs[w].at[jme], send_sem=send_sems.at[3 * w + k],
                    recv_sem=recv_sems.at[3 * w + k], device_id=(tx, ty, c), device_id_type=_MESH)
                cp.start()
                sends.append(cp)
        for w in range(n):
            for k, (tx, ty) in enumerate(chips):
                pltpu.make_async_remote_copy(
                    src_ref=p_refs[w].at[jme], dst_ref=out_refs[w].at[2 * tx + ty], send_sem=send_sems.at[3 * w + k],
                    recv_sem=recv_sems.at[3 * w + k], device_id=(tx, ty, c), device_id_type=_MESH).wait_recv()
        for cp in sends:
            cp.wait_send()
        for cp in local:
            cp.wait()

    return pl.pallas_call(
        body, out_shape=[jax.ShapeDtypeStruct(p.shape, p.dtype) for p in parts], in_specs=[_ANY] * n, out_specs=[_ANY] * n, name=name,
        scratch_shapes=[pltpu.SemaphoreType.DMA((3 * n,)), pltpu.SemaphoreType.DMA((3 * n,)), pltpu.SemaphoreType.DMA((n,))],
    )(*parts)


def _share_layers(name, bufs):
    n = len(bufs)

    def body(*refs):
        out_refs = refs[n:2 * n]
        send_sems, recv_sems = refs[2 * n:]
        x, y, c = lax.axis_index("x"), lax.axis_index("y"), lax.axis_index("c")
        copies = []
        for w in range(n):
            cp = pltpu.make_async_remote_copy(src_ref=out_refs[w].at[c], dst_ref=out_refs[w].at[c], send_sem=send_sems.at[w],
                                              recv_sem=recv_sems.at[w], device_id=(x, y, 1 - c), device_id_type=_MESH)
            cp.start()
            copies.append(cp)
        for w in range(n):
            pltpu.make_async_remote_copy(src_ref=out_refs[w].at[c], dst_ref=out_refs[w].at[1 - c], send_sem=send_sems.at[w],
                                         recv_sem=recv_sems.at[w], device_id=(x, y, 1 - c), device_id_type=_MESH).wait_recv()
        for cp in copies:
            cp.wait_send()

    return pl.pallas_call(
        body, out_shape=[jax.ShapeDtypeStruct(b.shape, b.dtype) for b in bufs], in_specs=[_ANY] * n, out_specs=[_ANY] * n,
        input_output_aliases={w: w for w in range(n)}, name=name,
        scratch_shapes=[pltpu.SemaphoreType.DMA((n,)), pltpu.SemaphoreType.DMA((n,))],
    )(*bufs)


def _row_tile(rows, cols):
    best = 16
    for t in range(16, rows + 1, 16):
        if rows % t == 0 and t * cols * 4 <= 2 * 1024 * 1024:
            best = t
    return best


def _add_pair(name, core, parts, got, out_dtype):
    _, _, r, c = parts.shape
    t = _row_tile(r, c)

    def body(core_ref, a_ref, b_ref, o_ref):
        o_ref[...] = (a_ref[...] + b_ref[...]).astype(o_ref.dtype)

    spec = pl.BlockSpec((None, t, c), lambda j, i, core_ref: (j, i, 0))
    grid_spec = pltpu.PrefetchScalarGridSpec(
        num_scalar_prefetch=1, grid=(N_CHIPS, r // t),
        in_specs=[pl.BlockSpec((None, None, t, c), lambda j, i, core_ref: (core_ref[0], j, i, 0)), spec], out_specs=spec)
    return pl.pallas_call(body, grid_spec=grid_spec, out_shape=jax.ShapeDtypeStruct(got.shape, out_dtype), name=name,
                          compiler_params=pltpu.CompilerParams(dimension_semantics=("arbitrary", "arbitrary")))(core, parts, got)


def _add_four(name, core, a):
    _, r, c = a.shape
    t = _row_tile(r, c)

    def body(core_ref, a0, a1, a2, a3, o_ref):
        o_ref[...] = ((a0[...].astype(f32) + a1[...].astype(f32)) + a2[...].astype(f32)) + a3[...].astype(f32)

    specs = [pl.BlockSpec((None, t, c), functools.partial(lambda i, core_ref, k: (k, i, 0), k=k)) for k in range(N_CHIPS)]
    grid_spec = pltpu.PrefetchScalarGridSpec(
        num_scalar_prefetch=1, grid=(r // t,), in_specs=specs,
        out_specs=pl.BlockSpec((None, t, c), lambda i, core_ref: (core_ref[0], i, 0)))
    return pl.pallas_call(body, grid_spec=grid_spec, out_shape=jax.ShapeDtypeStruct((2, r, c), f32), name=name,
                          compiler_params=pltpu.CompilerParams(dimension_semantics=("arbitrary",)))(core, a, a, a, a)


def _adamw(name, w, g, m, v, row_tile=None):
    c1 = 1.0 - ADAM_B1 ** ADAM_STEP
    c2 = 1.0 - ADAM_B2 ** ADAM_STEP

    def body(w_ref, g_ref, m_ref, v_ref, d_ref, nm_ref, nv_ref):
        gv = g_ref[...]
        nm = ADAM_B1 * m_ref[...] + (1.0 - ADAM_B1) * gv
        nv = ADAM_B2 * v_ref[...] + (1.0 - ADAM_B2) * (gv * gv)
        m_hat = nm / c1
        v_hat = nv / c2
        d_ref[...] = -ADAM_LR * (m_hat / (jnp.sqrt(v_hat) + ADAM_EPS) + ADAM_WD * w_ref[...])
        nm_ref[...] = nm
        nv_ref[...] = nv

    sds = jax.ShapeDtypeStruct(w.shape, f32)
    if row_tile is None:
        return pl.pallas_call(body, out_shape=[sds] * 3, name=name)(w, g, m, v)
    _, r, c = w.shape
    spec = pl.BlockSpec((None, row_tile, c), lambda l, i: (l, i, 0))
    return pl.pallas_call(body, grid=(DEPTH, r // row_tile), in_specs=[spec] * 4, out_specs=[spec] * 3, out_shape=[sds] * 3, name=name,
                          compiler_params=pltpu.CompilerParams(dimension_semantics=("arbitrary", "arbitrary"), vmem_limit_bytes=VMEM_LIMIT),
                          )(w, g, m, v)


def _pad_rows(flat, rows):
    return jnp.pad(flat, (0, rows * LANE - flat.shape[0])).reshape(rows, LANE)


def kernel(x, positions, norm_g, w_in, mla_q_a_norm, mla_w_q_up, mla_kv_a_norm, mla_w_kv_up, mla_q_norm, mla_k_norm, fox_b_f, fox_q_norm, fox_k_norm, s5_lambda_re, s5_lambda_im, s5_log_dt, s5_b_re, s5_b_im, s5_c_re, s5_c_im, s5_d, s5_w_glu, s5_b_glu, w_branch_out, w_out, loss_target, m_norm_g, m_w_in, m_mla_q_a_norm, m_mla_w_q_up, m_mla_kv_a_norm, m_mla_w_kv_up, m_mla_q_norm, m_mla_k_norm, m_fox_b_f, m_fox_q_norm, m_fox_k_norm, m_s5_lambda_re, m_s5_lambda_im, m_s5_log_dt, m_s5_b_re, m_s5_b_im, m_s5_c_re, m_s5_c_im, m_s5_d, m_s5_w_glu, m_s5_b_glu, m_w_branch_out, m_w_out, v_norm_g, v_w_in, v_mla_q_a_norm, v_mla_w_q_up, v_mla_kv_a_norm, v_mla_w_kv_up, v_mla_q_norm, v_mla_k_norm, v_fox_b_f, v_fox_q_norm, v_fox_k_norm, v_s5_lambda_re, v_s5_lambda_im, v_s5_log_dt, v_s5_b_re, v_s5_b_im, v_s5_c_re, v_s5_c_im, v_s5_d, v_s5_w_glu, v_s5_b_glu, v_w_branch_out, v_w_out):
    given = dict(locals())
    wts = {n: given[n] for n in WEIGHTS}
    mom1 = {n: given["m_" + n] for n in WEIGHTS}
    mom2 = {n: given["v_" + n] for n in WEIGHTS}

    def lanes(n, a):
        _, _, c, cp = _BIG_SHARD[n]
        return jnp.pad(a, ((0, 0), (0, 0), (0, cp - c)))

    gathered = _gather_layers("gather_weights", [lanes(n, wts[n].astype(bf16)) for n in BIG])
    big = dict(zip(BIG, gathered))
    small = {n: wts[n] for n in SMALL}

    loss_local, grad_x, grads = _local_step(x[0], positions, loss_target[0], small, big)
    loss = lax.psum(loss_local, ("x", "y", "c"))

    small_flat = jnp.concatenate([grads[n].reshape(-1) for n in SMALL])
    small_rows = -(-small_flat.shape[0] // (N_DEV * 16 * LANE)) * 16
    parts = [grads[n] if n == "w_in" else jnp.stack([_to_shards(n, grads[n][l]) for l in range(DEPTH)]) for n in BIG]
    parts.append(jnp.swapaxes(_pad_rows(small_flat, N_DEV * small_rows).reshape(N_CHIPS, 2, small_rows, LANE), 0, 1))
    core = lax.axis_index("c")
    core1 = core.reshape(1).astype(jnp.int32)
    got = _swap_layers("grads_to_sibling", parts)
    hop = [bf16] * len(BIG) + [f32]
    pair = [_add_pair("grads_pair_sum_%d" % i, core1, a, b, dt) for i, (a, b, dt) in enumerate(zip(parts, got, hop))]
    landed = _scatter_to_chips("grads_to_chips", pair)
    total = [_add_four("grads_chip_sum_%d" % i, core1, a) for i, a in enumerate(landed)]
    shared = _share_layers("grads_share", total[:-1])
    small_mine = lax.dynamic_index_in_dim(total[-1], core, 0, keepdims=False)
    small_all = _all_gather8("gather_small_grads", small_mine).reshape(-1)

    g_out = {n: s[:, :, :_BIG_SHARD[n][2]] for n, s in zip(BIG, shared)}
    pos = 0
    for n in SMALL:
        g_out[n] = small_all[pos:pos + wts[n].size].reshape(wts[n].shape)
        pos += wts[n].size

    delta, new_m, new_v = {}, {}, {}
    for n in WEIGHTS:
        row_tile = _row_tile(*wts[n].shape[1:]) if n in BIG else None
        delta[n], new_m[n], new_v[n] = _adamw("adamw_" + n, wts[n], g_out[n], mom1[n], mom2[n], row_tile)

    return (loss, grad_x[None], *[g_out[n] for n in WEIGHTS], *[delta[n] for n in WEIGHTS],
            *[new_m[n] for n in WEIGHTS], *[new_v[n] for n in WEIGHTS])
```

```python
import functools
import math

import jax
import jax.numpy as jnp
from jax import lax
from jax.experimental import pallas as pl
from jax.experimental.pallas import tpu as pltpu

f32 = jnp.float32
bf16 = jnp.bfloat16

D_MODEL = 1024
DEPTH = 2
EPS = 1e-6
HEADS = 8
MLA_QK = 96
MLA_Q_RANK = 256
MLA_KV_RANK = 128
ROPE = 32
ROPE_THETA = 10000.0
FOX_DIM = 64
S5_GROUPS = 32
S5_GROUP = 16
S5_STATE = 64
S5_LANES = S5_GROUPS * S5_STATE
LANE = 128
S5_BLOCKS = S5_LANES // LANE
TOK = 256
VMEM_LIMIT = 56 * 1024 * 1024

ADAM_LR = 0.001
ADAM_B1 = 0.9
ADAM_B2 = 0.999
ADAM_EPS = 1e-08
ADAM_WD = 0.01
ADAM_STEP = 10

_ORIG = {}
_off = 0
for _n, _w in (("cq", 256), ("ckv", 128), ("kpe", 32), ("fq", 512), ("fk", 512), ("fv", 512), ("ff", 8), ("s5u", 512),
               ("g_mla", 512), ("g_fox", 512), ("g_s5", 512), ("m_mla", 1024), ("m_fox", 1024), ("m_s5", 1024)):
    _ORIG[_n] = (_off, _w)
    _off += _w
_PAD = {"m_mla": (0, 1024, 0), "m_fox": (1024, 1024, 0), "m_s5": (2048, 1024, 0),
        "fq": (3072, 512, 0), "fk": (3584, 512, 0), "fv": (4096, 512, 0), "s5u": (4608, 512, 0),
        "g_mla": (5120, 512, 0), "g_fox": (5632, 512, 0), "g_s5": (6144, 512, 0),
        "cq": (6656, 256, 0), "ckv": (6912, 128, 0), "kpe": (7040, 128, 64), "ff": (7168, 128, 0)}
NP = 7680
_PAD_ORDER = ("m_mla", "m_fox", "m_s5", "fq", "fk", "fv", "s5u", "g_mla", "g_fox", "g_s5", "cq", "ckv", "kpe", "ff")


def _seg(name):
    start, width, _ = _PAD[name]
    return width, start // width


def _nn(a, b):
    return lax.dot_general(a.astype(bf16), b.astype(bf16), (((1,), (0,)), ((), ())), preferred_element_type=f32)


def _nt(a, b):
    return lax.dot_general(a.astype(bf16), b.astype(bf16), (((1,), (1,)), ((), ())), preferred_element_type=f32)


def _tn(a, b):
    return lax.dot_general(a.astype(bf16), b.astype(bf16), (((0,), (0,)), ((), ())), preferred_element_type=f32)


def _rms(x, g, n):
    r = lax.rsqrt(jnp.sum(x * x, axis=-1, keepdims=True) * (1.0 / n) + EPS)
    return x * r * g, r


def _rms_bwd(dy, x, r, g, n):
    xh = x * r
    dg = jnp.sum(dy * xh, axis=0, keepdims=True)
    dxh = dy * g
    dx = r * (dxh - xh * (jnp.sum(dxh * xh, axis=-1, keepdims=True) * (1.0 / n)))
    return dx, dg


def _sigmoid(x):
    return 1.0 / (1.0 + jnp.exp(-x))


_GELU_C = math.sqrt(2.0 / math.pi)


def _gelu(x):
    t = jnp.tanh(_GELU_C * (x + 0.044715 * x * x * x))
    return 0.5 * x * (1.0 + t), t


def _gelu_grad(x, t):
    return 0.5 * (1.0 + t) + 0.5 * x * (1.0 - t * t) * _GELU_C * (1.0 + 3.0 * 0.044715 * x * x)


def _accumulate(ref, val):
    i = pl.program_id(0)

    @pl.when(i == 0)
    def _():
        ref[...] = val

    @pl.when(i > 0)
    def _():
        ref[...] += val


def _rope(x, c, s1, s2):
    return x * c + pltpu.roll(x, LANE - 16, 1) * s1 + pltpu.roll(x, 16, 1) * s2


def _rope_t(d, c, s1, s2):
    return d * c + pltpu.roll(d * s1, 16, 1) + pltpu.roll(d * s2, LANE - 16, 1)


def _const_map(ndim):
    return lambda *_: (0,) * ndim


def _rowwise(name, body, n_tok, tiled_in, full_in, tiled_out, acc_out, tile=TOK):
    in_specs, args = [], []
    for arr, width, blk in tiled_in:
        in_specs.append(pl.BlockSpec((tile, width), functools.partial(lambda i, b: (i, b), b=blk)))
        args.append(arr)
    for arr in full_in:
        in_specs.append(pl.BlockSpec(arr.shape, _const_map(arr.ndim)))
        args.append(arr)
    out_specs, out_shape = [], []
    for width, dt in tiled_out:
        out_specs.append(pl.BlockSpec((tile, width), lambda i: (i, 0)))
        out_shape.append(jax.ShapeDtypeStruct((n_tok, width), dt))
    for shape, dt in acc_out:
        out_specs.append(pl.BlockSpec(shape, _const_map(len(shape))))
        out_shape.append(jax.ShapeDtypeStruct(shape, dt))
    return pl.pallas_call(
        body, grid=(n_tok // tile,), in_specs=in_specs, out_specs=out_specs, out_shape=out_shape, name=name,
        compiler_params=pltpu.CompilerParams(dimension_semantics=("arbitrary",), vmem_limit_bytes=VMEM_LIMIT),
    )(*args)


def _mm(name, a, b, *, mode, grid, a_spec, b_spec, o_spec, out_shape, acc_shape, add=None, add_spec=None):
    nk = grid[2]

    def body(*refs):
        if add is None:
            a_ref, b_ref, o_ref, acc_ref = refs
        else:
            a_ref, b_ref, add_ref, o_ref, acc_ref = refs
        k = pl.program_id(2)

        @pl.when(k == 0)
        def _():
            acc_ref[...] = jnp.zeros_like(acc_ref)

        acc_ref[...] += {"nn": _nn, "nt": _nt, "tn": _tn}[mode](a_ref[...], b_ref[...])

        @pl.when(k == nk - 1)
        def _():
            r = acc_ref[...]
            if add is not None:
                r = r + add_ref[...]
            o_ref[...] = r.astype(o_ref.dtype)

    in_specs = [a_spec, b_spec] + ([add_spec] if add is not None else [])
    args = (a, b) + ((add,) if add is not None else ())
    return pl.pallas_call(
        body, grid=grid, in_specs=in_specs, out_specs=o_spec, out_shape=out_shape, name=name,
        scratch_shapes=[pltpu.VMEM(acc_shape, f32)],
        compiler_params=pltpu.CompilerParams(dimension_semantics=("arbitrary", "arbitrary", "arbitrary"), vmem_limit_bytes=VMEM_LIMIT),
    )(*args)


def _mm_nn(name, a, b, *, m, n, k, tm, tn, tk, out_dtype=f32, a_koff=0):
    return _mm(name, a, b, mode="nn", grid=(m // tm, n // tn, k // tk),
               a_spec=pl.BlockSpec((tm, tk), lambda i, j, kk: (i, kk + a_koff)),
               b_spec=pl.BlockSpec((tk, tn), lambda i, j, kk: (kk, j)),
               o_spec=pl.BlockSpec((tm, tn), lambda i, j, kk: (i, j)),
               out_shape=jax.ShapeDtypeStruct((m, n), out_dtype), acc_shape=(tm, tn))


ATT_KV = 256
ATT_Q = 512


def _attn_common(mla, n_tok):
    qw = 2 * LANE if mla else LANE
    scale = 1.0 / math.sqrt(MLA_QK if mla else FOX_DIM)
    return qw, scale, min(ATT_Q, n_tok)


def _attn_heads(q_ref, mla):
    out = []
    if mla:
        for e in (0, 1):
            qe = q_ref[:, e * LANE:(e + 1) * LANE]
            out.append((qe.astype(f32).T.astype(bf16), qe))
        return out
    q = q_ref[...]
    tq = q.shape[0]
    qt = q.astype(f32).T
    row = lax.broadcasted_iota(jnp.int32, (LANE, tq), 0)
    lane = lax.broadcasted_iota(jnp.int32, (tq, LANE), 1)
    for e in (0, 1):
        out.append((jnp.where((row >= 64) == bool(e), qt, 0.0).astype(bf16),
                    jnp.where((lane >= 64) == bool(e), q, jnp.zeros((), bf16))))
    return out


def _attn_allowed(off, i, tq, mla):
    kpos = off + lax.broadcasted_iota(jnp.int32, (ATT_KV, tq), 0)
    qpos = i * tq + lax.broadcasted_iota(jnp.int32, (ATT_KV, tq), 1)
    return ((kpos // 64) <= (qpos // 64)) if mla else (kpos <= qpos)


def _attn_fwd(name, q, k, v, cum_b, *, mla, n_tok):
    qw, scale, tq = _attn_common(mla, n_tok)
    nq = n_tok // tq
    nkv = n_tok // ATT_KV
    has_bias = cum_b is not None

    def body(*refs):
        if has_bias:
            q_ref, k_ref, v_ref, cb_ref, o_ref, lse_ref, vt_ref = refs
        else:
            q_ref, k_ref, v_ref, o_ref, lse_ref, vt_ref = refs
        i = pl.program_id(1)

        @pl.when(i == 0)
        def _():
            for jb in range(nkv):
                vt_ref[jb] = v_ref[jb * ATT_KV:(jb + 1) * ATT_KV, :].astype(f32).T.astype(bf16)

        heads = _attn_heads(q_ref, mla)

        def step(j, carry, masked):
            off = pl.multiple_of(j * ATT_KV, ATT_KV)
            allowed = _attn_allowed(off, i, tq, mla) if masked else None
            vt = vt_ref[j]
            sts = []
            for e in (0, 1):
                kb = k_ref[pl.ds(off, ATT_KV), e * LANE:(e + 1) * LANE] if mla else k_ref[pl.ds(off, ATT_KV), :]
                sts.append(_nn(kb, heads[e][0]))
            stats = []
            for e in (0, 1):
                m, l, _ = carry[e]
                st = sts[e] * scale
                if has_bias:
                    st = st - jnp.tile(cb_ref[e, pl.ds(off, ATT_KV), :], (1, tq // LANE))
                if masked:
                    st = jnp.where(allowed, st, -1e30)
                m_new = jnp.maximum(m, jnp.max(st, axis=0, keepdims=True))
                alpha = jnp.exp(m - m_new)
                pt = jnp.exp(st - m_new)
                stats.append((m_new, alpha * l + jnp.sum(pt, axis=0, keepdims=True), alpha, pt.astype(bf16)))
            new = []
            for e in (0, 1):
                m_new, l, alpha, pt = stats[e]
                new.append((m_new, l, alpha * carry[e][2] + _nn(vt[64 * e:64 * e + 64, :], pt)))
            return tuple(new)

        init = tuple((jnp.full((1, tq), -1e30, f32), jnp.zeros((1, tq), f32), jnp.zeros((64, tq), f32)) for _ in (0, 1))
        n_full = i * (tq // ATT_KV)
        carry = lax.fori_loop(0, n_full, functools.partial(step, masked=False), init)
        for d in range(tq // ATT_KV):
            carry = step(n_full + d, carry, True)
        o_ref[...] = jnp.concatenate([carry[e][2] / carry[e][1] for e in (0, 1)], axis=0).T
        lse_ref[...] = jnp.zeros_like(lse_ref)
        for e in (0, 1):
            lse_ref[e:e + 1, :] = carry[e][0] + jnp.log(carry[e][1])

    in_specs = [pl.BlockSpec((tq, qw), lambda p, i: (i, p)),
                pl.BlockSpec((n_tok, qw), lambda p, i: (0, p)),
                pl.BlockSpec((n_tok, LANE), lambda p, i: (0, p))]
    args = [q, k, v]
    if has_bias:
        in_specs.append(pl.BlockSpec((2, n_tok, LANE), lambda p, i: (p, 0, 0)))
        args.append(cum_b)
    return pl.pallas_call(
        body, grid=(4, nq), in_specs=in_specs,
        out_specs=[pl.BlockSpec((tq, LANE), lambda p, i: (i, p)), pl.BlockSpec((None, 8, tq), lambda p, i: (p, 0, i))],
        out_shape=[jax.ShapeDtypeStruct((n_tok, 512), f32), jax.ShapeDtypeStruct((4, 8, n_tok), f32)], name=name,
        scratch_shapes=[pltpu.VMEM((nkv, LANE, ATT_KV), bf16)],
        compiler_params=pltpu.CompilerParams(dimension_semantics=("arbitrary", "arbitrary"), vmem_limit_bytes=VMEM_LIMIT),
    )(*args)


def _attn_bwd(name, q, k, v, o, lse, do, cum_b, *, mla, n_tok):
    qw, scale, tq = _attn_common(mla, n_tok)
    nq = n_tok // tq
    nkv = n_tok // ATT_KV
    has_bias = cum_b is not None

    def body(*refs):
        if has_bias:
            q_ref, k_ref, v_ref, o_ref, lse_ref, do_ref, cb_ref, dq_ref, dk_ref, dv_ref, dck_ref, dcq_ref, kt_ref = refs
        else:
            q_ref, k_ref, v_ref, o_ref, lse_ref, do_ref, dq_ref, dk_ref, dv_ref, kt_ref = refs
        p = pl.program_id(0)
        i = pl.program_id(1)

        @pl.when(i == 0)
        def _():
            dk_ref[...] = jnp.zeros_like(dk_ref)
            dv_ref[...] = jnp.zeros_like(dv_ref)
            for jb in range(nkv):
                for c0 in range(0, qw, LANE):
                    kt_ref[jb, c0:c0 + LANE, :] = k_ref[jb * ATT_KV:(jb + 1) * ATT_KV, c0:c0 + LANE].astype(f32).T.astype(bf16)

        if has_bias:
            @pl.when(jnp.logical_and(i == 0, p == 0))
            def _():
                dck_ref[...] = jnp.zeros_like(dck_ref)

        heads = _attn_heads(q_ref, mla)
        do = do_ref[...]
        do_t = do.T
        prod_t = (do * o_ref[...]).T
        row = lax.broadcasted_iota(jnp.int32, (LANE, tq), 0)
        lane = lax.broadcasted_iota(jnp.int32, (tq, LANE), 1)
        lane_k = lax.broadcasted_iota(jnp.int32, (ATT_KV, LANE), 1)
        per_head = []
        for e in (0, 1):
            sel_r = (row >= 64) == bool(e)
            per_head.append((jnp.where(sel_r, do_t, 0.0).astype(bf16),
                             jnp.where((lane >= 64) == bool(e), do, 0.0).astype(bf16),
                             jnp.sum(jnp.where(sel_r, prod_t, 0.0), axis=0, keepdims=True),
                             lse_ref[e:e + 1, :]))
        dq_rows = LANE if mla else 64

        def step(j, carry, masked):
            off = pl.multiple_of(j * ATT_KV, ATT_KV)
            allowed = _attn_allowed(off, i, tq, mla) if masked else None
            vb = v_ref[pl.ds(off, ATT_KV), :]
            kt = kt_ref[j]
            cols = [slice(e * LANE, (e + 1) * LANE) if mla else slice(None) for e in (0, 1)]
            sts = [_nn(k_ref[pl.ds(off, ATT_KV), cols[e]], heads[e][0]) for e in (0, 1)]
            dpts = [_nn(vb, per_head[e][0]) for e in (0, 1)]
            mids = []
            for e in (0, 1):
                _, _, delta, lse_e = per_head[e]
                st = sts[e] * scale
                if has_bias:
                    st = st - jnp.tile(cb_ref[e, pl.ds(off, ATT_KV), :], (1, tq // LANE))
                pt = jnp.exp(st - lse_e)
                if masked:
                    pt = jnp.where(allowed, pt, 0.0)
                dst = pt * (dpts[e] - delta)
                qsum = carry[e][1]
                if has_bias:
                    rs = jnp.sum(dst, axis=1, keepdims=True)
                    dck_ref[pl.ds(off, ATT_KV), :] += jnp.where(lane_k == 2 * p + e, -rs, 0.0)
                    qsum = qsum + jnp.sum(dst, axis=0, keepdims=True)
                mids.append((pt.astype(bf16), dst.astype(bf16), qsum))
            new = []
            for e in (0, 1):
                pt, dst, qsum = mids[e]
                kt_e = kt[e * LANE:(e + 1) * LANE, :] if mla else kt[64 * e:64 * e + 64, :]
                new.append((carry[e][0] + _nn(kt_e, dst) * scale, qsum))
                dk_ref[pl.ds(off, ATT_KV), cols[e]] += _nn(dst, heads[e][1]) * scale
                dv_ref[pl.ds(off, ATT_KV), :] += _nn(pt, per_head[e][1])
            return tuple(new)

        init = tuple((jnp.zeros((dq_rows, tq), f32), jnp.zeros((1, tq), f32)) for _ in (0, 1))
        n_full = i * (tq // ATT_KV)
        carry = lax.fori_loop(0, n_full, functools.partial(step, masked=False), init)
        for d in range(tq // ATT_KV):
            carry = step(n_full + d, carry, True)
        if mla:
            for e in (0, 1):
                dq_ref[:, e * LANE:(e + 1) * LANE] = carry[e][0].T
        else:
            dq_ref[...] = jnp.concatenate([carry[0][0], carry[1][0]], axis=0).T
        if has_bias:
            dcq_ref[...] = jnp.zeros_like(dcq_ref)
            for e in (0, 1):
                dcq_ref[e:e + 1, :] = carry[e][1]

    tile_q = pl.BlockSpec((tq, qw), lambda p, i: (i, p))
    tile_v = pl.BlockSpec((tq, LANE), lambda p, i: (i, p))
    full_k = pl.BlockSpec((n_tok, qw), lambda p, i: (0, p))
    full_v = pl.BlockSpec((n_tok, LANE), lambda p, i: (0, p))
    in_specs = [tile_q, full_k, full_v, tile_v, pl.BlockSpec((None, 8, tq), lambda p, i: (p, 0, i)), tile_v]
    args = [q, k, v, o, lse, do]
    out_specs = [tile_q, full_k, full_v]
    out_shape = [jax.ShapeDtypeStruct((n_tok, 4 * qw), f32), jax.ShapeDtypeStruct((n_tok, 4 * qw), f32),
                 jax.ShapeDtypeStruct((n_tok, 512), f32)]
    if has_bias:
        in_specs.append(pl.BlockSpec((2, n_tok, LANE), lambda p, i: (p, 0, 0)))
        args.append(cum_b)
        out_specs += [pl.BlockSpec((n_tok, LANE), _const_map(2)), pl.BlockSpec((None, 8, tq), lambda p, i: (p, 0, i))]
        out_shape += [jax.ShapeDtypeStruct((n_tok, LANE), f32), jax.ShapeDtypeStruct((4, 8, n_tok), f32)]
    return pl.pallas_call(
        body, grid=(4, nq), in_specs=in_specs, out_specs=out_specs, out_shape=out_shape, name=name,
        scratch_shapes=[pltpu.VMEM((nkv, qw, ATT_KV), bf16)],
        compiler_params=pltpu.CompilerParams(dimension_semantics=("arbitrary", "arbitrary"), vmem_limit_bytes=VMEM_LIMIT),
    )(*args)


def _s5_disc(lr, li, ldt):
    dt = jnp.exp(ldt)
    mag = jnp.exp(lr * dt)
    a_re = mag * jnp.cos(li * dt)
    a_im = mag * jnp.sin(li * dt)
    den = lr * lr + li * li
    f_re = ((a_re - 1.0) * lr + a_im * li) / den
    f_im = (a_im * lr - (a_re - 1.0) * li) / den
    return a_re, a_im, f_re, f_im


def _s5_param_fwd(lr, li, ldt, b_re, b_im):
    def body(lr_ref, li_ref, ldt_ref, br_ref, bi_ref, ar_ref, ai_ref, bbr_ref, bbi_ref):
        a_re, a_im, f_re, f_im = _s5_disc(lr_ref[...], li_ref[...], ldt_ref[...])
        ar_ref[...] = a_re
        ai_ref[...] = a_im
        br, bi = br_ref[...], bi_ref[...]
        bbr_ref[...] = f_re * br - f_im * bi
        bbi_ref[...] = f_re * bi + f_im * br

    col = jax.ShapeDtypeStruct(lr.shape, f32)
    mat = jax.ShapeDtypeStruct(b_re.shape, f32)
    return pl.pallas_call(body, out_shape=[col, col, mat, mat], name="s5_param_fwd")(lr, li, ldt, b_re, b_im)


def _s5_param_bwd(lr, li, ldt, b_re, b_im, da_re, da_im, dbb_re, dbb_im):
    def body(lr_ref, li_ref, ldt_ref, br_ref, bi_ref, dar_ref, dai_ref, gbr_ref, gbi_ref,
             dlr_ref, dli_ref, dldt_ref, dbr_ref, dbi_ref):
        (a_re, a_im, f_re, f_im), vjp = jax.vjp(_s5_disc, lr_ref[...], li_ref[...], ldt_ref[...])
        br, bi, gr, gi = br_ref[...], bi_ref[...], gbr_ref[...], gbi_ref[...]
        dbr_ref[...] = f_re * gr + f_im * gi
        dbi_ref[...] = f_re * gi - f_im * gr
        dfr = jnp.sum(br * gr + bi * gi, axis=-1, keepdims=True)
        dfi = jnp.sum(br * gi - bi * gr, axis=-1, keepdims=True)
        dlr, dli, dldt = vjp((dar_ref[...], dai_ref[...], dfr, dfi))
        dlr_ref[...] = dlr
        dli_ref[...] = dli
        dldt_ref[...] = jnp.sum(dldt.reshape(S5_GROUPS, S5_STATE, 1), axis=1)

    col = jax.ShapeDtypeStruct((S5_LANES, 1), f32)
    mat = jax.ShapeDtypeStruct((S5_LANES, S5_GROUP), f32)
    return pl.pallas_call(body, out_shape=[col, col, jax.ShapeDtypeStruct((S5_GROUPS, 1), f32), mat, mat],
                          name="s5_param_bwd")(lr, li, ldt, b_re, b_im, da_re, da_im, dbb_re, dbb_im)


_SCAN_NB = 4


def _to_streams(a):
    s, c = a.shape
    return jnp.swapaxes(a.reshape(8, s // 8, c), 0, 1).reshape(s, c)


def _from_streams(a):
    s, c = a.shape
    return jnp.swapaxes(a.reshape(s // 8, 8, c), 0, 1).reshape(s, c)


def _s5_scan(name, src, wq, wy, add, y_dtype, a_re8, a_im8, *, reverse, n_tok):
    rows = n_tok // 8
    nb = _SCAN_NB
    assert nb == 4

    def scan_body(src_ref, w_ref, wy_ref, add_ref, ar_ref, ai_ref, x_ref, y_ref):
        for ri in (0, 1):
            bu = _nn(src_ref[...], w_ref[ri])
            for b in range(nb):
                x_ref[ri, b] = bu[:, b * LANE:(b + 1) * LANE]
        bu_ref = x_ref
        a_r = [ar_ref[b] for b in range(nb)]
        a_i = [ai_ref[b] for b in range(nb)]
        zero = jnp.zeros((8, LANE), f32)
        one = jnp.ones((8, LANE), f32)

        def rows_at(r):
            rr = (rows - 1 - r) if reverse else r
            return pl.ds(pl.multiple_of(rr * 8, 8), 8)

        def pass1(r, carry):
            out = []
            sl = rows_at(r)
            for b in range(nb):
                xr, xi, mr, mi = carry[b]
                nr = a_r[b] * xr - a_i[b] * xi + bu_ref[0, b, sl, :]
                ni = a_r[b] * xi + a_i[b] * xr + bu_ref[1, b, sl, :]
                x_ref[0, b, sl, :] = nr
                x_ref[1, b, sl, :] = ni
                out.append((nr, ni, a_r[b] * mr - a_i[b] * mi, a_r[b] * mi + a_i[b] * mr))
            return tuple(out)

        carry = lax.fori_loop(0, rows, pass1, tuple((zero, zero, one, zero) for _ in range(nb)))
        sub = lax.broadcasted_iota(jnp.int32, (8, LANE), 0)
        feed = []
        for b in range(nb):
            lr_, li_, pr, pi = carry[b]
            fr, fi = zero, zero
            for _ in range(7):
                tr = lr_ + pr * fr - pi * fi
                ti = li_ + pr * fi + pi * fr
                if reverse:
                    fr = jnp.where(sub < 7, pltpu.roll(tr, 7, 0), 0.0)
                    fi = jnp.where(sub < 7, pltpu.roll(ti, 7, 0), 0.0)
                else:
                    fr = jnp.where(sub > 0, pltpu.roll(tr, 1, 0), 0.0)
                    fi = jnp.where(sub > 0, pltpu.roll(ti, 1, 0), 0.0)
            feed.append((fr, fi))

        def pass2(r, carry):
            out = []
            sl = rows_at(r)
            for b in range(nb):
                mr, mi = carry[b]
                fr, fi = feed[b]
                x_ref[0, b, sl, :] += mr * fr - mi * fi
                x_ref[1, b, sl, :] += mr * fi + mi * fr
                out.append((a_r[b] * mr - a_i[b] * mi, a_r[b] * mi + a_i[b] * mr))
            return tuple(out)

        lax.fori_loop(0, rows, pass2, tuple((a_r[b], a_i[b]) for b in range(nb)))

        y = None
        for ri in (0, 1):
            for b in range(nb):
                t = _nn(x_ref[ri, b], wy_ref[ri, b * LANE:(b + 1) * LANE, :])
                y = t if y is None else y + t
        if add is not None:
            y = y + add_ref[...]
        y_ref[...] = y.astype(y_ref.dtype)

    def body(*refs):
        if add is None:
            src_ref, w_ref, wy_ref, ar_ref, ai_ref, x_ref, y_ref = refs
            add_ref = None
        else:
            src_ref, w_ref, wy_ref, add_ref, ar_ref, ai_ref, x_ref, y_ref = refs
        scan_body(src_ref, w_ref, wy_ref, add_ref, ar_ref, ai_ref, x_ref, y_ref)

    blk = pl.BlockSpec((2, nb, n_tok, LANE), lambda g: (0, g, 0, 0))
    ablk = pl.BlockSpec((nb, 8, LANE), lambda g: (g, 0, 0))
    col = pl.BlockSpec((n_tok, LANE), lambda g: (0, g))
    in_specs = [col, pl.BlockSpec((2, None, LANE, 512), lambda g: (0, g, 0, 0)), pl.BlockSpec((2, None, 512, LANE), lambda g: (0, g, 0, 0))]
    args = [src, wq, wy]
    if add is not None:
        in_specs.append(col)
        args.append(add)
    return pl.pallas_call(
        body, grid=(S5_BLOCKS // nb,), in_specs=in_specs + [ablk, ablk], out_specs=[blk, col],
        out_shape=[jax.ShapeDtypeStruct((2, S5_BLOCKS, n_tok, LANE), f32), jax.ShapeDtypeStruct((n_tok, 512), y_dtype)], name=name,
        compiler_params=pltpu.CompilerParams(dimension_semantics=("arbitrary",), vmem_limit_bytes=VMEM_LIMIT),
    )(*args, a_re8, a_im8)


def _s5_da(xs, gx, *, n_tok):
    def body(x_ref, g_ref, o_ref):
        t = lax.broadcasted_iota(jnp.int32, (n_tok, LANE), 0)
        sub = lax.broadcasted_iota(jnp.int32, (8, LANE), 0)

        def prev(v):
            return (jnp.where(t >= 8, pltpu.roll(v, 8, 0), 0.0),
                    jnp.where(sub > 0, pltpu.roll(v[n_tok - 8:, :], 1, 0), 0.0))

        (xr, hr), (xi, hi) = prev(x_ref[0, 0]), prev(x_ref[1, 0])
        gr, gi = g_ref[0, 0], g_ref[1, 0]
        gr0, gi0 = gr[0:8, :], gi[0:8, :]
        o_ref[0, 0:1, :] = (jnp.sum(xr * gr + xi * gi, axis=0, keepdims=True)
                            + jnp.sum(hr * gr0 + hi * gi0, axis=0, keepdims=True))
        o_ref[0, 1:2, :] = (jnp.sum(xr * gi - xi * gr, axis=0, keepdims=True)
                            + jnp.sum(hr * gi0 - hi * gr0, axis=0, keepdims=True))

    blk = pl.BlockSpec((2, 1, n_tok, LANE), lambda g: (0, g, 0, 0))
    return pl.pallas_call(
        body, grid=(S5_BLOCKS,), in_specs=[blk, blk], out_specs=pl.BlockSpec((1, 2, LANE), lambda g: (g, 0, 0)),
        out_shape=jax.ShapeDtypeStruct((S5_BLOCKS, 2, LANE), f32), name="s5_da",
        compiler_params=pltpu.CompilerParams(dimension_semantics=("arbitrary",), vmem_limit_bytes=VMEM_LIMIT),
    )(xs, gx)


S5_Q = 4


def _bd8(t):
    _, a, b = t.shape
    t = t.reshape(S5_Q, 8, a, 1, b)
    eye = jnp.eye(8, dtype=jnp.bool_).reshape(1, 8, 1, 8, 1)
    return jnp.where(eye, jnp.broadcast_to(t, (S5_Q, 8, a, 8, b)), jnp.zeros((), t.dtype)).reshape(S5_Q, 8 * a, 8 * b)


def _bd8_diag(m, a, b):
    m = m.reshape(S5_Q, 8, a, 8, b)
    eye = jnp.eye(8, dtype=jnp.bool_).reshape(1, 8, 1, 8, 1)
    return jnp.sum(jnp.where(eye, m, 0.0), axis=3).reshape(S5_GROUPS, a, b)


def _s5_wgrad_states(name, xs, d, *, n_tok):
    def body(x_ref, d_ref, o_ref):
        for k in range(4):
            o_ref[k * LANE:(k + 1) * LANE, :] = _tn(x_ref[k], d_ref[...])

    return pl.pallas_call(
        body, grid=(2, S5_Q),
        in_specs=[pl.BlockSpec((None, 4, n_tok, LANE), lambda ri, q: (ri, q, 0, 0)), pl.BlockSpec((n_tok, LANE), lambda ri, q: (0, q))],
        out_specs=pl.BlockSpec((None, None, 512, LANE), lambda ri, q: (ri, q, 0, 0)),
        out_shape=jax.ShapeDtypeStruct((2, S5_Q, 512, LANE), f32), name=name,
        compiler_params=pltpu.CompilerParams(dimension_semantics=("arbitrary", "arbitrary"), vmem_limit_bytes=VMEM_LIMIT),
    )(xs, d)


def _s5_wgrad_channels(name, a, a_blk0, gx, *, n_tok):
    def body(a_ref, g_ref, o_ref):
        for k in range(4):
            o_ref[:, k * LANE:(k + 1) * LANE] = _tn(a_ref[...], g_ref[k])

    return pl.pallas_call(
        body, grid=(2, S5_Q),
        in_specs=[pl.BlockSpec((n_tok, LANE), lambda ri, q: (0, a_blk0 + q)), pl.BlockSpec((None, 4, n_tok, LANE), lambda ri, q: (ri, q, 0, 0))],
        out_specs=pl.BlockSpec((None, None, LANE, 512), lambda ri, q: (ri, q, 0, 0)),
        out_shape=jax.ShapeDtypeStruct((2, S5_Q, LANE, 512), f32), name=name,
        compiler_params=pltpu.CompilerParams(dimension_semantics=("arbitrary", "arbitrary"), vmem_limit_bytes=VMEM_LIMIT),
    )(a, gx)


N_CHIPS = 4
_BIG_SHARD = {"w_in": (1, 1024, 1770, 1792), "mla_w_q_up": (1, 256, 192, 256), "mla_w_kv_up": (1, 128, 256, 256),
              "s5_w_glu": (0, 128, 512, 512), "w_branch_out": (0, 384, 1024, 1024), "w_out": (0, 256, 1024, 1024)}


def _to_shards(name, m):
    axis, r, c, cp = _BIG_SHARD[name]
    if axis == 0:
        return m.reshape(N_CHIPS, r, c)
    return jnp.stack([jnp.pad(m[:, j * c:(j + 1) * c], ((0, 0), (0, cp - c))) for j in range(N_CHIPS)])


def _from_shards(name, s):
    axis, r, c, cp = _BIG_SHARD[name]
    if axis == 0:
        return s.reshape(N_CHIPS * r, c)
    return jnp.concatenate([s[j, :, :c] for j in range(N_CHIPS)], axis=1)


def _pad_w_in(w):
    pieces, pos = [], 0
    for name in _PAD_ORDER:
        start, width, inner = _PAD[name]
        o0, ow = _ORIG[name]
        if start + inner > pos:
            pieces.append(jnp.zeros((w.shape[0], start + inner - pos), w.dtype))
        pieces.append(w[:, o0:o0 + ow])
        pos = start + inner + ow
    pieces.append(jnp.zeros((w.shape[0], NP - pos), w.dtype))
    return jnp.concatenate(pieces, axis=1)


def _prep_weights(small, big):
    per_layer = jax.vmap
    w = {}
    w["w_in_shards"] = big["w_in"]
    w["w_in"] = per_layer(lambda s: _pad_w_in(_from_shards("w_in", s)))(big["w_in"])

    def q_up(s):
        wq = _from_shards("mla_w_q_up", s).reshape(MLA_Q_RANK, HEADS, MLA_QK)
        return jnp.pad(wq, ((0, 0), (0, 0), (0, LANE - MLA_QK))).reshape(MLA_Q_RANK, HEADS * LANE)

    def kv_up(s):
        wkv = _from_shards("mla_w_kv_up", s).reshape(MLA_KV_RANK, HEADS, 128)
        wk = jnp.pad(wkv[:, :, :64], ((0, 0), (0, 0), (0, 64))).reshape(MLA_KV_RANK, HEADS * LANE)
        return jnp.concatenate([wk, wkv[:, :, 64:].reshape(MLA_KV_RANK, 512)], axis=1)

    w["wq"] = per_layer(q_up)(big["mla_w_q_up"])
    w["wkv"] = per_layer(kv_up)(big["mla_w_kv_up"])
    for name, key in (("w_glu", "s5_w_glu"), ("wo", "w_branch_out"), ("w_out", "w_out")):
        w[name] = per_layer(functools.partial(_from_shards, key))(big[key])
    row = lambda a: a.astype(f32)[:, None, :]
    lanes = lambda a, n: jnp.pad(row(a), ((0, 0), (0, 0), (0, LANE - n)))
    w["norm_g"] = row(small["norm_g"])
    w["qa_g"] = row(small["mla_q_a_norm"])
    w["kva_g"] = row(small["mla_kv_a_norm"])
    w["qn_g"] = lanes(small["mla_q_norm"], MLA_QK)
    w["kn_g"] = lanes(small["mla_k_norm"], MLA_QK)
    w["fq_g"] = jnp.tile(row(small["fox_q_norm"]), (1, 1, 2))
    w["fk_g"] = jnp.tile(row(small["fox_k_norm"]), (1, 1, 2))
    w["b_f"] = lanes(small["fox_b_f"], HEADS)
    w["lr"] = small["s5_lambda_re"].reshape(DEPTH, S5_LANES, 1)
    w["li"] = small["s5_lambda_im"].reshape(DEPTH, S5_LANES, 1)
    w["ldt"] = jnp.repeat(small["s5_log_dt"], S5_STATE, axis=1).reshape(DEPTH, S5_LANES, 1)
    w["b_re"] = small["s5_b_re"].reshape(DEPTH, S5_LANES, S5_GROUP)
    w["b_im"] = small["s5_b_im"].reshape(DEPTH, S5_LANES, S5_GROUP)
    w["s5_d"] = row(small["s5_d"])
    w["b_glu"] = row(small["s5_b_glu"])
    a_re, a_im, bb_re, bb_im = _s5_param_fwd(w["lr"], w["li"], w["ldt"], w["b_re"], w["b_im"])
    per_group = lambda m: m.reshape(S5_GROUPS, S5_STATE, S5_GROUP)
    pair = lambda f: per_layer(lambda re, im: jnp.stack([f(re), f(im)]).astype(bf16))
    c_re, c_im = small["s5_c_re"], -small["s5_c_im"]
    w["b_cn"] = pair(lambda m: _bd8(jnp.swapaxes(per_group(m), 1, 2)))(bb_re, bb_im)
    w["b_nc"] = pair(lambda m: _bd8(per_group(m)))(bb_re, bb_im)
    w["c_nc"] = pair(lambda m: _bd8(jnp.swapaxes(m, 1, 2)))(c_re, c_im)
    w["c_cn"] = pair(_bd8)(c_re, c_im)
    sublanes = lambda a: jnp.broadcast_to(a.reshape(DEPTH, S5_BLOCKS, 1, LANE), (DEPTH, S5_BLOCKS, 8, LANE))
    w["a_re8"], w["a_im8"], w["a_im8_neg"] = sublanes(a_re), sublanes(a_im), sublanes(-a_im)
    return w


def _fox_halves(x, lane):
    sq = x * x
    lo = jnp.sum(jnp.where(lane < 64, sq, 0.0), axis=-1, keepdims=True)
    hi = jnp.sum(sq, axis=-1, keepdims=True) - lo
    return jnp.where(lane < 64, lax.rsqrt(lo * (1.0 / 64) + EPS), lax.rsqrt(hi * (1.0 / 64) + EPS))


def _fox_halves_bwd(dy, x, r, g, lane):
    xh = x * r
    dxh = dy * g
    pr = dxh * xh
    lo = jnp.sum(jnp.where(lane < 64, pr, 0.0), axis=-1, keepdims=True)
    hi = jnp.sum(pr, axis=-1, keepdims=True) - lo
    mean = jnp.where(lane < 64, lo, hi) * (1.0 / 64)
    return r * (dxh - xh * mean), jnp.sum(dy * xh, axis=0, keepdims=True)


def _mla_recompute(cq, ckv, kpe, c, s1, s2, qa_g, kva_g, wq, wkv):
    cqn, r_cq = _rms(cq, qa_g, MLA_Q_RANK)
    ckvn, r_ckv = _rms(ckv, kva_g, MLA_KV_RANK)
    cqn_b = cqn.astype(bf16)
    ckvn_b = ckvn.astype(bf16)
    q_raw = _nn(cqn_b, wq)
    kv_raw = _nn(ckvn_b, wkv)
    kpe_rot = _rope(kpe, c, s1, s2)
    return cqn_b, r_cq, ckvn_b, r_ckv, q_raw, kv_raw, kpe_rot


def _layer_fwd(x, w, rope_tabs, n_tok):
    c_tab, s1_tab, s2_tab = rope_tabs
    saved = {"x": x}

    def norm_body(x_ref, g_ref, h_ref):
        h_ref[...] = _rms(x_ref[...], g_ref[...], D_MODEL)[0].astype(bf16)

    (h,) = _rowwise("norm_fwd", norm_body, n_tok, [(x, D_MODEL, 0)], [w["norm_g"]], [(D_MODEL, bf16)], [])
    proj = _mm_nn("in_proj", h, w["w_in"], m=n_tok, n=NP, k=D_MODEL, tm=n_tok, tn=512, tk=D_MODEL)
    saved["h"], saved["proj"] = h, proj

    def mla_prep_body(cq_ref, ckv_ref, kpe_ref, c_ref, s1_ref, s2_ref, qa_ref, kva_ref, wq_ref, wkv_ref, qn_g_ref, kn_g_ref,
                      qn_ref, kn_ref, v_ref):
        c, s1, s2 = c_ref[...], s1_ref[...], s2_ref[...]
        _, _, _, _, q_raw, kv_raw, kpe_rot = _mla_recompute(cq_ref[...], ckv_ref[...], kpe_ref[...], c, s1, s2,
                                                            qa_ref[...], kva_ref[...], wq_ref[...], wkv_ref[...])
        for hd in range(HEADS):
            sl = slice(hd * LANE, (hd + 1) * LANE)
            qn_ref[:, sl] = _rms(_rope(q_raw[:, sl], c, s1, s2), qn_g_ref[...], MLA_QK)[0].astype(bf16)
            kn_ref[:, sl] = _rms(kv_raw[:, sl] + kpe_rot, kn_g_ref[...], MLA_QK)[0].astype(bf16)
        v_ref[...] = kv_raw[:, HEADS * LANE:].astype(bf16)

    qn, kn, v_mla = _rowwise(
        "mla_prep", mla_prep_body, n_tok,
        [(proj, *_seg("cq")), (proj, *_seg("ckv")), (proj, *_seg("kpe")), (c_tab, LANE, 0), (s1_tab, LANE, 0), (s2_tab, LANE, 0)],
        [w["qa_g"], w["kva_g"], w["wq"], w["wkv"], w["qn_g"], w["kn_g"]],
        [(HEADS * LANE, bf16), (HEADS * LANE, bf16), (512, bf16)], [])
    y_mla, lse_mla = _attn_fwd("mla_attn_fwd", qn, kn, v_mla, None, mla=True, n_tok=n_tok)
    saved.update(qn=qn, kn=kn, v_mla=v_mla, y_mla=y_mla, lse_mla=lse_mla)

    def fox_prep_body(fq_ref, fk_ref, fv_ref, ff_ref, qg_ref, kg_ref, bf_ref, fqn_ref, fkn_ref, fvb_ref, logf_ref):
        lane = lax.broadcasted_iota(jnp.int32, (TOK, LANE), 1)
        for blk in range(4):
            sl = slice(blk * LANE, (blk + 1) * LANE)
            xq = fq_ref[:, sl]
            fqn_ref[:, sl] = (xq * _fox_halves(xq, lane) * qg_ref[...]).astype(bf16)
            xk = fk_ref[:, sl]
            fkn_ref[:, sl] = (xk * _fox_halves(xk, lane) * kg_ref[...]).astype(bf16)
        fvb_ref[...] = fv_ref[...].astype(bf16)
        z = ff_ref[...] + bf_ref[...]
        logf_ref[...] = jnp.minimum(z, 0.0) - jnp.log(1.0 + jnp.exp(-jnp.abs(z)))

    fqn, fkn, fvb, logf = _rowwise(
        "fox_prep", fox_prep_body, n_tok,
        [(proj, *_seg("fq")), (proj, *_seg("fk")), (proj, *_seg("fv")), (proj, *_seg("ff"))],
        [w["fq_g"], w["fk_g"], w["b_f"]],
        [(512, bf16), (512, bf16), (512, bf16), (LANE, f32)], [])

    def cum_body(x_ref, cum_ref):
        x = x_ref[...]
        t = lax.broadcasted_iota(jnp.int32, x.shape, 0)
        s = 1
        while s < n_tok:
            x = x + jnp.where(t >= s, pltpu.roll(x, s, 0), 0.0)
            s *= 2
        for hd in range(HEADS):
            cum_ref[hd] = jnp.broadcast_to(x[:, hd:hd + 1], (n_tok, LANE))

    cum_b = pl.pallas_call(cum_body, out_shape=jax.ShapeDtypeStruct((HEADS, n_tok, LANE), f32), name="fox_cum")(logf)
    y_fox, lse_fox = _attn_fwd("fox_attn_fwd", fqn, fkn, fvb, cum_b, mla=False, n_tok=n_tok)
    saved.update(fqn=fqn, fkn=fkn, fvb=fvb, cum_b=cum_b, y_fox=y_fox, lse_fox=lse_fox)

    u_w, u_blk = _seg("s5u")
    u_streams = _to_streams(proj[:, u_blk * u_w:(u_blk + 1) * u_w])
    xs, ylin = _s5_scan("s5_scan_fwd", u_streams, w["b_cn"], w["c_nc"], None, f32, w["a_re8"], w["a_im8"], reverse=False, n_tok=n_tok)
    ylin = _from_streams(ylin)

    def s5_post_body(yl_ref, u_ref, d_ref, wg_ref, bg_ref, out_ref):
        y = yl_ref[...] + d_ref[...] * u_ref[...]
        z, _ = _gelu(y)
        out_ref[...] = z * _sigmoid(_nn(z, wg_ref[...]) + bg_ref[...])

    (y_s5,) = _rowwise("s5_post", s5_post_body, n_tok, [(ylin, 512, 0), (proj, u_w, u_blk)],
                       [w["s5_d"], w["w_glu"], w["b_glu"]], [(512, f32)], [])
    saved.update(xs=xs, ylin=ylin, y_s5=y_s5, u_streams=u_streams)

    def merge_body(ym_ref, yf_ref, ys_ref, gm_ref, gf_ref, gs_ref, mm_ref, mf_ref, ms_ref, x_ref, wo_ref, wout_ref, out_ref):
        merged = jnp.zeros((TOK, D_MODEL), f32)
        for b, (y_ref, g_ref, m_ref) in enumerate(((ym_ref, gm_ref, mm_ref), (yf_ref, gf_ref, mf_ref), (ys_ref, gs_ref, ms_ref))):
            g = g_ref[...]
            a = y_ref[...] * (g * _sigmoid(g))
            merged = merged + _sigmoid(m_ref[...]) * _nn(a, wo_ref[b * 512:(b + 1) * 512, :])
        out_ref[...] = x_ref[...] + _nn(merged, wout_ref[...])

    (out,) = _rowwise(
        "merge_fwd", merge_body, n_tok,
        [(y_mla, 512, 0), (y_fox, 512, 0), (y_s5, 512, 0), (proj, *_seg("g_mla")), (proj, *_seg("g_fox")), (proj, *_seg("g_s5")),
         (proj, *_seg("m_mla")), (proj, *_seg("m_fox")), (proj, *_seg("m_s5")), (x, D_MODEL, 0)],
        [w["wo"], w["w_out"]], [(D_MODEL, f32)], [])
    return out, saved


def _layer_bwd(dout, w, sv, rope_tabs, n_tok):
    c_tab, s1_tab, s2_tab = rope_tabs
    proj, x = sv["proj"], sv["x"]
    grads = {}

    def merge_bwd_body(ym_ref, yf_ref, ys_ref, gm_ref, gf_ref, gs_ref, mm_ref, mf_ref, ms_ref, do_ref, wo_ref, wout_ref,
                       dym_ref, dyf_ref, dys_ref, dgm_ref, dgf_ref, dgs_ref, dmm_ref, dmf_ref, dms_ref, dwo_ref, dwout_ref):
        do = do_ref[...]
        branches = ((ym_ref, gm_ref, mm_ref, dym_ref, dgm_ref, dmm_ref), (yf_ref, gf_ref, mf_ref, dyf_ref, dgf_ref, dmf_ref),
                    (ys_ref, gs_ref, ms_ref, dys_ref, dgs_ref, dms_ref))
        acts, outs, sigs = [], [], []
        merged = jnp.zeros((TOK, D_MODEL), f32)
        for b, (y_ref, g_ref, m_ref, _, _, _) in enumerate(branches):
            g = g_ref[...]
            a = (y_ref[...] * (g * _sigmoid(g))).astype(bf16)
            o = _nn(a, wo_ref[b * 512:(b + 1) * 512, :])
            s = _sigmoid(m_ref[...])
            merged = merged + s * o
            acts.append(a)
            outs.append(o)
            sigs.append(s)
        dmerged = _nt(do, wout_ref[...])
        _accumulate(dwout_ref, _tn(merged, do))
        dwo = []
        for b, (y_ref, g_ref, m_ref, dy_ref, dg_ref, dm_ref) in enumerate(branches):
            s, o = sigs[b], outs[b]
            dm_ref[...] = (dmerged * o * s * (1.0 - s)).astype(bf16)
            d_o = dmerged * s
            da = _nt(d_o, wo_ref[b * 512:(b + 1) * 512, :])
            dwo.append(_tn(acts[b], d_o))
            g = g_ref[...]
            sg = _sigmoid(g)
            dy_ref[...] = da * (g * sg)
            dg_ref[...] = (da * y_ref[...] * (sg * (1.0 + g * (1.0 - sg)))).astype(bf16)
        _accumulate(dwo_ref, jnp.concatenate(dwo, axis=0))

    (dy_mla, dy_fox, dy_s5, dg_mla, dg_fox, dg_s5, dm_mla, dm_fox, dm_s5, dwo, dwout) = _rowwise(
        "merge_bwd", merge_bwd_body, n_tok,
        [(sv["y_mla"], 512, 0), (sv["y_fox"], 512, 0), (sv["y_s5"], 512, 0), (proj, *_seg("g_mla")), (proj, *_seg("g_fox")),
         (proj, *_seg("g_s5")), (proj, *_seg("m_mla")), (proj, *_seg("m_fox")), (proj, *_seg("m_s5")), (dout, D_MODEL, 0)],
        [w["wo"], w["w_out"]],
        [(512, f32)] * 3 + [(512, bf16)] * 3 + [(D_MODEL, bf16)] * 3, [((1536, D_MODEL), f32), ((D_MODEL, D_MODEL), f32)])
    grads["w_branch_out"], grads["w_out"] = dwo, dwout

    u_w, u_blk = _seg("s5u")

    def s5_post_bwd_body(yl_ref, u_ref, do_ref, d_ref, wg_ref, bg_ref, dyl_ref, dus_ref, dd_ref, dwg_ref, dbg_ref):
        u = u_ref[...]
        y = yl_ref[...] + d_ref[...] * u
        z, t = _gelu(y)
        s = _sigmoid(_nn(z, wg_ref[...]) + bg_ref[...])
        do = do_ref[...]
        dgl = do * z * s * (1.0 - s)
        dz = do * s + _nt(dgl, wg_ref[...])
        dy = dz * _gelu_grad(y, t)
        dyl_ref[...] = dy.astype(bf16)
        dus_ref[...] = dy * d_ref[...]
        _accumulate(dd_ref, jnp.sum(dy * u, axis=0, keepdims=True))
        _accumulate(dwg_ref, _tn(z, dgl))
        _accumulate(dbg_ref, jnp.sum(dgl, axis=0, keepdims=True))

    dylin, du_skip, dd, dwglu, dbglu = _rowwise(
        "s5_post_bwd", s5_post_bwd_body, n_tok, [(sv["ylin"], 512, 0), (proj, u_w, u_blk), (dy_s5, 512, 0)],
        [w["s5_d"], w["w_glu"], w["b_glu"]], [(512, bf16), (512, f32)], [((1, 512), f32), ((512, 512), f32), ((1, 512), f32)])
    grads["s5_d"], grads["s5_w_glu"], grads["s5_b_glu"] = dd.reshape(512), dwglu, dbglu.reshape(512)

    dylin = _to_streams(dylin)
    dc_nc = _s5_wgrad_states("s5_dc", sv["xs"], dylin, n_tok=n_tok)
    gx, ds5u = _s5_scan("s5_scan_bwd", dylin, w["c_cn"], w["b_nc"], _to_streams(du_skip), bf16, w["a_re8"], w["a_im8_neg"],
                        reverse=True, n_tok=n_tok)
    da = _s5_da(sv["xs"], gx, n_tok=n_tok)
    ds5u = _from_streams(ds5u)
    db_cn = _s5_wgrad_channels("s5_db", sv["u_streams"], 0, gx, n_tok=n_tok)
    diag_b = lambda m: jnp.swapaxes(_bd8_diag(m, S5_GROUP, S5_STATE), 1, 2).reshape(S5_LANES, S5_GROUP)
    diag_c = lambda m: jnp.swapaxes(_bd8_diag(m, S5_STATE, S5_GROUP), 1, 2)
    dlr, dli, dldt, db_re, db_im = _s5_param_bwd(
        w["lr"], w["li"], w["ldt"], w["b_re"], w["b_im"], da[:, 0, :].reshape(S5_LANES, 1), da[:, 1, :].reshape(S5_LANES, 1),
        diag_b(db_cn[0]), diag_b(db_cn[1]))
    grads["s5_lambda_re"] = dlr.reshape(S5_GROUPS, S5_STATE)
    grads["s5_lambda_im"] = dli.reshape(S5_GROUPS, S5_STATE)
    grads["s5_log_dt"] = dldt.reshape(S5_GROUPS)
    grads["s5_b_re"] = db_re.reshape(S5_GROUPS, S5_STATE, S5_GROUP)
    grads["s5_b_im"] = db_im.reshape(S5_GROUPS, S5_STATE, S5_GROUP)
    grads["s5_c_re"] = diag_c(dc_nc[0])
    grads["s5_c_im"] = -diag_c(dc_nc[1])

    dfqn, dfkn, dfv, dck, dcq = _attn_bwd("fox_attn_bwd", sv["fqn"], sv["fkn"], sv["fvb"], sv["y_fox"], sv["lse_fox"], dy_fox,
                                          sv["cum_b"], mla=False, n_tok=n_tok)
    dcq = jnp.pad(dcq[:, :2, :].reshape(HEADS, n_tok).T, ((0, 0), (0, LANE - HEADS)))

    def fox_gate_bwd_body(dk_ref, dq_ref, ff_ref, bf_ref, dff_ref, dbf_ref):
        xg = dk_ref[...] + dq_ref[...]
        t = lax.broadcasted_iota(jnp.int32, xg.shape, 0)
        s = 1
        while s < n_tok:
            xg = xg + jnp.where(t < n_tok - s, pltpu.roll(xg, n_tok - s, 0), 0.0)
            s *= 2
        dff = xg * _sigmoid(-(ff_ref[...] + bf_ref[...]))
        dff_ref[...] = dff.astype(bf16)
        dbf_ref[...] = jnp.sum(dff, axis=0, keepdims=True)

    ff_w, ff_blk = _seg("ff")
    dff, dbf = pl.pallas_call(
        fox_gate_bwd_body, grid=(1,),
        in_specs=[pl.BlockSpec((n_tok, LANE), lambda i: (0, 0)), pl.BlockSpec((n_tok, LANE), lambda i: (0, 0)),
                  pl.BlockSpec((n_tok, ff_w), lambda i: (0, ff_blk)), pl.BlockSpec((1, LANE), lambda i: (0, 0))],
        out_specs=[pl.BlockSpec((n_tok, LANE), lambda i: (0, 0)), pl.BlockSpec((1, LANE), lambda i: (0, 0))],
        out_shape=[jax.ShapeDtypeStruct((n_tok, LANE), bf16), jax.ShapeDtypeStruct((1, LANE), f32)], name="fox_gate_bwd",
    )(dck, dcq, proj, w["b_f"])
    grads["fox_b_f"] = dbf[0, :HEADS]

    def fox_prep_bwd_body(fq_ref, fk_ref, dqn_ref, dkn_ref, dv_ref, qg_ref, kg_ref, dfq_ref, dfk_ref, dfv_ref, dqg_ref, dkg_ref):
        lane = lax.broadcasted_iota(jnp.int32, (TOK, LANE), 1)
        dqg = jnp.zeros((1, LANE), f32)
        dkg = jnp.zeros((1, LANE), f32)
        for blk in range(4):
            sl = slice(blk * LANE, (blk + 1) * LANE)
            xq = fq_ref[:, sl]
            dx, dg = _fox_halves_bwd(dqn_ref[:, sl], xq, _fox_halves(xq, lane), qg_ref[...], lane)
            dfq_ref[:, sl] = dx.astype(bf16)
            dqg = dqg + dg
            xk = fk_ref[:, sl]
            dx, dg = _fox_halves_bwd(dkn_ref[:, sl], xk, _fox_halves(xk, lane), kg_ref[...], lane)
            dfk_ref[:, sl] = dx.astype(bf16)
            dkg = dkg + dg
        dfv_ref[...] = dv_ref[...].astype(bf16)
        _accumulate(dqg_ref, dqg + pltpu.roll(dqg, 64, 1))
        _accumulate(dkg_ref, dkg + pltpu.roll(dkg, 64, 1))

    dfq, dfk, dfvb, dfqg, dfkg = _rowwise(
        "fox_prep_bwd", fox_prep_bwd_body, n_tok,
        [(proj, *_seg("fq")), (proj, *_seg("fk")), (dfqn, 512, 0), (dfkn, 512, 0), (dfv, 512, 0)],
        [w["fq_g"], w["fk_g"]], [(512, bf16)] * 3, [((1, LANE), f32)] * 2)
    grads["fox_q_norm"], grads["fox_k_norm"] = dfqg[0, :FOX_DIM], dfkg[0, :FOX_DIM]

    dqn, dkn, dv_mla = _attn_bwd("mla_attn_bwd", sv["qn"], sv["kn"], sv["v_mla"], sv["y_mla"], sv["lse_mla"], dy_mla,
                                 None, mla=True, n_tok=n_tok)

    def mla_prep_bwd_body(cq_ref, ckv_ref, kpe_ref, c_ref, s1_ref, s2_ref, dqn_ref, dkn_ref, dv_ref,
                          qa_ref, kva_ref, wq_ref, wkv_ref, qn_g_ref, kn_g_ref,
                          dcq_ref, dckv_ref, dkpe_ref, dwq_ref, dwkv_ref, dqa_ref, dkva_ref, dqng_ref, dkng_ref):
        c, s1, s2 = c_ref[...], s1_ref[...], s2_ref[...]
        cq, ckv = cq_ref[...], ckv_ref[...]
        cqn_b, r_cq, ckvn_b, r_ckv, q_raw, kv_raw, kpe_rot = _mla_recompute(
            cq, ckv, kpe_ref[...], c, s1, s2, qa_ref[...], kva_ref[...], wq_ref[...], wkv_ref[...])
        lane = lax.broadcasted_iota(jnp.int32, (TOK, LANE), 1)
        dq_raw, dk_raw = [], []
        dkpe_rot = jnp.zeros((TOK, LANE), f32)
        dqng = jnp.zeros((1, LANE), f32)
        dkng = jnp.zeros((1, LANE), f32)
        for hd in range(HEADS):
            sl = slice(hd * LANE, (hd + 1) * LANE)
            q_rot = _rope(q_raw[:, sl], c, s1, s2)
            r = lax.rsqrt(jnp.sum(q_rot * q_rot, axis=-1, keepdims=True) * (1.0 / MLA_QK) + EPS)
            dx, dg = _rms_bwd(dqn_ref[:, sl], q_rot, r, qn_g_ref[...], MLA_QK)
            dqng = dqng + dg
            dq_raw.append(_rope_t(dx, c, s1, s2))
            k_full = kv_raw[:, sl] + kpe_rot
            r = lax.rsqrt(jnp.sum(k_full * k_full, axis=-1, keepdims=True) * (1.0 / MLA_QK) + EPS)
            dx, dg = _rms_bwd(dkn_ref[:, sl], k_full, r, kn_g_ref[...], MLA_QK)
            dkng = dkng + dg
            dk_raw.append(jnp.where(lane < 64, dx, 0.0))
            dkpe_rot = dkpe_rot + dx
        dkpe = _rope_t(dkpe_rot, c, s1, s2)
        dkpe_ref[...] = jnp.where(jnp.logical_and(lane >= 64, lane < 64 + ROPE), dkpe, 0.0).astype(bf16)
        dq_raw = jnp.concatenate(dq_raw, axis=1).astype(bf16)
        dkv_raw = jnp.concatenate(dk_raw + [dv_ref[...]], axis=1).astype(bf16)
        dcqn = _nt(dq_raw, wq_ref[...])
        dckvn = _nt(dkv_raw, wkv_ref[...])
        dx, dg = _rms_bwd(dcqn, cq, r_cq, qa_ref[...], MLA_Q_RANK)
        dcq_ref[...] = dx.astype(bf16)
        _accumulate(dqa_ref, dg)
        dx, dg = _rms_bwd(dckvn, ckv, r_ckv, kva_ref[...], MLA_KV_RANK)
        dckv_ref[...] = dx.astype(bf16)
        _accumulate(dkva_ref, dg)
        _accumulate(dwq_ref, _tn(cqn_b, dq_raw))
        _accumulate(dwkv_ref, _tn(ckvn_b, dkv_raw))
        _accumulate(dqng_ref, dqng)
        _accumulate(dkng_ref, dkng)

    dcq, dckv, dkpe, dwq, dwkv, dqa, dkva, dqng, dkng = _rowwise(
        "mla_prep_bwd", mla_prep_bwd_body, n_tok,
        [(proj, *_seg("cq")), (proj, *_seg("ckv")), (proj, *_seg("kpe")), (c_tab, LANE, 0), (s1_tab, LANE, 0), (s2_tab, LANE, 0),
         (dqn, HEADS * LANE, 0), (dkn, HEADS * LANE, 0), (dv_mla, 512, 0)],
        [w["qa_g"], w["kva_g"], w["wq"], w["wkv"], w["qn_g"], w["kn_g"]],
        [(MLA_Q_RANK, bf16), (LANE, bf16), (LANE, bf16)],
        [((MLA_Q_RANK, HEADS * LANE), f32), ((MLA_KV_RANK, HEADS * LANE + 512), f32), ((1, MLA_Q_RANK), f32),
         ((1, MLA_KV_RANK), f32), ((1, LANE), f32), ((1, LANE), f32)])
    grads["mla_w_q_up"] = dwq.reshape(MLA_Q_RANK, HEADS, LANE)[:, :, :MLA_QK].reshape(MLA_Q_RANK, HEADS * MLA_QK)
    dwk = dwkv[:, :HEADS * LANE].reshape(MLA_KV_RANK, HEADS, LANE)[:, :, :64]
    dwv = dwkv[:, HEADS * LANE:].reshape(MLA_KV_RANK, HEADS, 64)
    grads["mla_w_kv_up"] = jnp.concatenate([dwk, dwv], axis=2).reshape(MLA_KV_RANK, HEADS * 128)
    grads["mla_q_a_norm"], grads["mla_kv_a_norm"] = dqa.reshape(-1), dkva.reshape(-1)
    grads["mla_q_norm"], grads["mla_k_norm"] = dqng[0, :MLA_QK], dkng[0, :MLA_QK]

    _, _, shard_c, shard_cp = _BIG_SHARD["w_in"]
    kpe0 = _PAD["kpe"][2]
    pieces = [dcq, dckv, dkpe[:, kpe0:kpe0 + ROPE], dfq, dfk, dfvb, dff[:, :HEADS], ds5u, dg_mla, dg_fox, dg_s5,
              dm_mla, dm_fox, dm_s5]
    gap = jnp.zeros((n_tok, shard_cp - shard_c), bf16)
    cut, pos = [], 0
    for p in pieces:
        start = 0
        while start < p.shape[1]:
            take = min(p.shape[1] - start, shard_c - pos % shard_c)
            cut.append(p[:, start:start + take])
            start, pos = start + take, pos + take
            if pos % shard_c == 0:
                cut.append(gap)
    dproj = jnp.concatenate(cut, axis=1)
    ct = 256
    per = shard_cp // ct
    dh = _mm("in_proj_dgrad", dproj, w["w_in_shards"], mode="nt", grid=(1, 1, N_CHIPS * per),
             a_spec=pl.BlockSpec((n_tok, ct), lambda i, j, kk: (0, kk)),
             b_spec=pl.BlockSpec((None, D_MODEL, ct), lambda i, j, kk: (kk // per, 0, kk % per)),
             o_spec=pl.BlockSpec((n_tok, D_MODEL), lambda i, j, kk: (0, 0)),
             out_shape=jax.ShapeDtypeStruct((n_tok, D_MODEL), f32), acc_shape=(n_tok, D_MODEL))
    grads["w_in"] = _mm("in_proj_wgrad", sv["h"], dproj, mode="tn", grid=(1, N_CHIPS * per, 1),
                        a_spec=pl.BlockSpec((n_tok, D_MODEL), lambda i, j, kk: (0, 0)),
                        b_spec=pl.BlockSpec((n_tok, ct), lambda i, j, kk: (0, j)),
                        o_spec=pl.BlockSpec((None, D_MODEL, ct), lambda i, j, kk: (j // per, 0, j % per)),
                        out_shape=jax.ShapeDtypeStruct((N_CHIPS, D_MODEL, shard_cp), f32), acc_shape=(D_MODEL, ct))

    def norm_bwd_body(dh_ref, x_ref, do_ref, g_ref, dx_ref, dg_ref):
        xv = x_ref[...]
        r = lax.rsqrt(jnp.sum(xv * xv, axis=-1, keepdims=True) * (1.0 / D_MODEL) + EPS)
        dx, dg = _rms_bwd(dh_ref[...], xv, r, g_ref[...], D_MODEL)
        dx_ref[...] = do_ref[...] + dx
        _accumulate(dg_ref, dg)

    dx, dng = _rowwise("norm_bwd", norm_bwd_body, n_tok, [(dh, D_MODEL, 0), (x, D_MODEL, 0), (dout, D_MODEL, 0)],
                       [w["norm_g"]], [(D_MODEL, f32)], [((1, D_MODEL), f32)])
    grads["norm_g"] = dng.reshape(D_MODEL)
    return dx, grads


def _rope_tables(positions):
    inv = 1.0 / (ROPE_THETA ** (jnp.arange(0, ROPE, 2, dtype=f32) / ROPE))
    ang = positions.astype(f32).reshape(-1, 1) * inv
    cos, sin = jnp.cos(ang), jnp.sin(ang)
    n = ang.shape[0]
    z16, z32, z64 = jnp.zeros((n, 16), f32), jnp.zeros((n, 32), f32), jnp.zeros((n, 64), f32)
    c = jnp.concatenate([jnp.ones((n, 64), f32), cos, cos, z32], axis=1)
    s1 = jnp.concatenate([z64, -sin, z16, z32], axis=1)
    s2 = jnp.concatenate([z64, z16, sin, z32], axis=1)
    return c, s1, s2


BIG = ("w_in", "mla_w_q_up", "mla_w_kv_up", "s5_w_glu", "w_branch_out", "w_out")
SMALL = ("norm_g", "mla_q_a_norm", "mla_kv_a_norm", "mla_q_norm", "mla_k_norm", "fox_b_f", "fox_q_norm", "fox_k_norm",
         "s5_lambda_re", "s5_lambda_im", "s5_log_dt", "s5_b_re", "s5_b_im", "s5_c_re", "s5_c_im", "s5_d", "s5_b_glu")
WEIGHTS = ("norm_g", "w_in", "mla_q_a_norm", "mla_w_q_up", "mla_kv_a_norm", "mla_w_kv_up", "mla_q_norm", "mla_k_norm",
           "fox_b_f", "fox_q_norm", "fox_k_norm", "s5_lambda_re", "s5_lambda_im", "s5_log_dt", "s5_b_re", "s5_b_im",
           "s5_c_re", "s5_c_im", "s5_d", "s5_w_glu", "s5_b_glu", "w_branch_out", "w_out")


def _local_step(x, positions, loss_target, small, big):
    n_tok = x.shape[0]
    tabs = _rope_tables(positions)
    ws, saves = [], []
    hcur = x
    stacked = _prep_weights(small, big)
    for l in range(DEPTH):
        w = {k: v[l] for k, v in stacked.items()}
        hcur, sv = _layer_fwd(hcur, w, tabs, n_tok)
        ws.append(w)
        saves.append(sv)

    def loss_body(y_ref, t_ref, d_ref, l_ref):
        err = y_ref[...] - t_ref[...]
        d_ref[...] = err * (1.0 / D_MODEL)
        tot = jnp.sum(jnp.sum(err * err, axis=-1, keepdims=True), axis=0, keepdims=True)
        _accumulate(l_ref, jnp.broadcast_to(tot * (0.5 / D_MODEL), (1, LANE)))

    dcur, loss = _rowwise("loss", loss_body, n_tok, [(hcur, D_MODEL, 0), (loss_target, D_MODEL, 0)], [], [(D_MODEL, f32)],
                          [((1, LANE), f32)])
    layer_grads = [None] * DEPTH
    for l in reversed(range(DEPTH)):
        dcur, layer_grads[l] = _layer_bwd(dcur, ws[l], saves[l], tabs, n_tok)
    grads = {n: jnp.stack([layer_grads[l][n] for l in range(DEPTH)]) for n in WEIGHTS}
    return loss[0, 0], dcur, grads


N_DEV = 8
_ANY = pl.BlockSpec(memory_space=pl.ANY)
_MESH = pl.DeviceIdType.MESH


def _all_gather8(name, blk):
    m = blk.shape[0]

    def body(x_ref, out_ref, send_sems, recv_sems, local_sem):
        x, y, c = lax.axis_index("x"), lax.axis_index("y"), lax.axis_index("c")
        me, sibling = (x, y, c), (x, y, 1 - c)
        chips = [(1 - x, y), (x, 1 - y), (1 - x, 1 - y)]

        def slot(px, py, pc):
            return out_ref.at[4 * px + 2 * py + pc]

        def copy(k, block, to, src=None):
            return pltpu.make_async_remote_copy(
                src_ref=slot(*block) if src is None else src, dst_ref=slot(*block),
                send_sem=send_sems.at[k], recv_sem=recv_sems.at[k], device_id=to, device_id_type=_MESH)

        mine = pltpu.make_async_copy(x_ref, slot(*me), local_sem)
        mine.start()
        first = [copy(0, me, sibling, src=x_ref)]
        first += [copy(1 + j, me, (*chip, c), src=x_ref) for j, chip in enumerate(chips)]
        for cp in first:
            cp.start()
        passed = [copy(4 + j, (*chip, c), sibling) for j, chip in enumerate(chips)]
        for j, chip in enumerate(chips):
            copy(1 + j, (*chip, c), me).wait_recv()
            passed[j].start()
        copy(0, sibling, me).wait_recv()
        for j, chip in enumerate(chips):
            copy(4 + j, (*chip, 1 - c), me).wait_recv()
        for cp in first + passed:
            cp.wait_send()
        mine.wait()

    return pl.pallas_call(
        body, out_shape=jax.ShapeDtypeStruct((N_DEV, m, LANE), blk.dtype), in_specs=[_ANY], out_specs=_ANY, name=name,
        scratch_shapes=[pltpu.SemaphoreType.DMA((7,)), pltpu.SemaphoreType.DMA((7,)), pltpu.SemaphoreType.DMA],
    )(blk)


def _gather_layers(name, shards):
    n = len(shards)

    def body(*refs):
        x_refs, out_refs = refs[:n], refs[n:2 * n]
        send_sems, recv_sems, local_sems = refs[2 * n:]
        x, y, c = lax.axis_index("x"), lax.axis_index("y"), lax.axis_index("c")
        me, sibling = (x, y, c), (x, y, 1 - c)
        xn, yn, dg = (1 - x, y, c), (x, 1 - y, c), (1 - x, 1 - y, c)
        relay_from = (x + (1 - c) * (1 - 2 * x), y + c * (1 - 2 * y), c)
        relay_to = (x + c * (1 - 2 * x), y + (1 - c) * (1 - 2 * y), c)

        def copy(w, k, block, to, src=None):
            px, py, pc = block
            slot = out_refs[w].at[pc, 2 * px + py]
            return pltpu.make_async_remote_copy(
                src_ref=slot if src is None else src, dst_ref=slot, send_sem=send_sems.at[7 * w + k],
                recv_sem=recv_sems.at[7 * w + k], device_id=to, device_id_type=_MESH)

        started, local = [], []
        for w in range(n):
            src = x_refs[w].at[c]
            mine = pltpu.make_async_copy(src, out_refs[w].at[c, 2 * x + y], local_sems.at[w])
            mine.start()
            local.append(mine)
            first = [copy(w, 0, me, sibling, src=src), copy(w, 1, me, xn, src=src), copy(w, 2, me, yn, src=src)]
            for cp in first:
                cp.start()
            started += first
        for w in range(n):
            copy(w, 1, xn, me).wait_recv()
            copy(w, 2, yn, me).wait_recv()
            onward = [copy(w, 3, relay_from, relay_to), copy(w, 4, xn, sibling), copy(w, 5, yn, sibling)]
            for cp in onward:
                cp.start()
            started += onward
        for w in range(n):
            copy(w, 3, dg, me).wait_recv()
            onward = copy(w, 6, dg, sibling)
            onward.start()
            started.append(onward)
        for w in range(n):
            copy(w, 0, sibling, me).wait_recv()
            for k, chip in ((4, xn), (5, yn), (6, dg)):
                copy(w, k, (chip[0], chip[1], 1 - c), me).wait_recv()
        for cp in started:
            cp.wait_send()
        for cp in local:
            cp.wait()

    return pl.pallas_call(
        body, out_shape=[jax.ShapeDtypeStruct((2, N_CHIPS) + s.shape[1:], s.dtype) for s in shards],
        in_specs=[_ANY] * n, out_specs=[_ANY] * n, name=name,
        scratch_shapes=[pltpu.SemaphoreType.DMA((7 * n,)), pltpu.SemaphoreType.DMA((7 * n,)), pltpu.SemaphoreType.DMA((n,))],
    )(*shards)


def _swap_layers(name, parts):
    n = len(parts)

    def body(*refs):
        p_refs, got_refs = refs[:n], refs[n:2 * n]
        send_sems, recv_sems = refs[2 * n:]
        x, y, c = lax.axis_index("x"), lax.axis_index("y"), lax.axis_index("c")
        copies = []
        for w in range(n):
            cp = pltpu.make_async_remote_copy(
                src_ref=p_refs[w].at[1 - c], dst_ref=got_refs[w], send_sem=send_sems.at[w], recv_sem=recv_sems.at[w],
                device_id=(x, y, 1 - c), device_id_type=_MESH)
            cp.start()
            copies.append(cp)
        for cp in copies:
            cp.wait()

    return pl.pallas_call(
        body, out_shape=[jax.ShapeDtypeStruct(p.shape[1:], p.dtype) for p in parts], in_specs=[_ANY] * n, out_specs=[_ANY] * n,
        name=name, scratch_shapes=[pltpu.SemaphoreType.DMA((n,)), pltpu.SemaphoreType.DMA((n,))],
    )(*parts)


def _scatter_to_chips(name, parts):
    n = len(parts)

    def body(*refs):
        p_refs, out_refs = refs[:n], refs[n:2 * n]
        send_sems, recv_sems, local_sems = refs[2 * n:]
        x, y, c = lax.axis_index("x"), lax.axis_index("y"), lax.axis_index("c")
        jme = 2 * x + y
        chips = [(1 - x, y), (x, 1 - y), (1 - x, 1 - y)]
        sends, local = [], []
        for w in range(n):
            mine = pltpu.make_async_copy(p_refs[w].at[jme], out_refs[w].at[jme], local_sems.at[w])
            mine.start()
            local.append(mine)
            for k, (tx, ty) in enumerate(chips):
                cp = pltpu.make_async_remote_copy(
                    src_ref=p_refs[w].at[2 * tx + ty], dst_ref=out_refs[w].at[jme], send_sem=send_sems.at[3 * w + k],
                    recv_sem=recv_sems.at[3 * w + k], device_id=(tx, ty, c), device_id_type=_MESH)
                cp.start()
                sends.append(cp)
        for w in range(n):
            for k, (tx, ty) in enumerate(chips):
                pltpu.make_async_remote_copy(
                    src_ref=p_refs[w].at[jme], dst_ref=out_refs[w].at[2 * tx + ty], send_sem=send_sems.at[3 * w + k],
                    recv_sem=recv_sems.at[3 * w + k], device_id=(tx, ty, c), device_id_type=_MESH).wait_recv()
        for cp in sends:
            cp.wait_send()
        for cp in local:
            cp.wait()

    return pl.pallas_call(
        body, out_shape=[jax.ShapeDtypeStruct(p.shape, p.dtype) for p in parts], in_specs=[_ANY] * n, out_specs=[_ANY] * n, name=name,
        scratch_shapes=[pltpu.SemaphoreType.DMA((3 * n,)), pltpu.SemaphoreType.DMA((3 * n,)), pltpu.SemaphoreType.DMA((n,))],
    )(*parts)


def _share_layers(name, bufs):
    n = len(bufs)

    def body(*refs):
        out_refs = refs[n:2 * n]
        send_sems, recv_sems = refs[2 * n:]
        x, y, c = lax.axis_index("x"), lax.axis_index("y"), lax.axis_index("c")
        copies = []
        for w in range(n):
            cp = pltpu.make_async_remote_copy(src_ref=out_refs[w].at[c], dst_ref=out_refs[w].at[c], send_sem=send_sems.at[w],
                                              recv_sem=recv_sems.at[w], device_id=(x, y, 1 - c), device_id_type=_MESH)
            cp.start()
            copies.append(cp)
        for w in range(n):
            pltpu.make_async_remote_copy(src_ref=out_refs[w].at[c], dst_ref=out_refs[w].at[1 - c], send_sem=send_sems.at[w],
                                         recv_sem=recv_sems.at[w], device_id=(x, y, 1 - c), device_id_type=_MESH).wait_recv()
        for cp in copies:
            cp.wait_send()

    return pl.pallas_call(
        body, out_shape=[jax.ShapeDtypeStruct(b.shape, b.dtype) for b in bufs], in_specs=[_ANY] * n, out_specs=[_ANY] * n,
        input_output_aliases={w: w for w in range(n)}, name=name,
        scratch_shapes=[pltpu.SemaphoreType.DMA((n,)), pltpu.SemaphoreType.DMA((n,))],
    )(*bufs)


def _row_tile(rows, cols):
    best = 16
    for t in range(16, rows + 1, 16):
        if rows % t == 0 and t * cols * 4 <= 2 * 1024 * 1024:
            best = t
    return best


def _add_pair(name, core, parts, got, out_dtype):
    _, _, r, c = parts.shape
    t = _row_tile(r, c)

    def body(core_ref, a_ref, b_ref, o_ref):
        o_ref[...] = (a_ref[...] + b_ref[...]).astype(o_ref.dtype)

    spec = pl.BlockSpec((None, t, c), lambda j, i, core_ref: (j, i, 0))
    grid_spec = pltpu.PrefetchScalarGridSpec(
        num_scalar_prefetch=1, grid=(N_CHIPS, r // t),
        in_specs=[pl.BlockSpec((None, None, t, c), lambda j, i, core_ref: (core_ref[0], j, i, 0)), spec], out_specs=spec)
    return pl.pallas_call(body, grid_spec=grid_spec, out_shape=jax.ShapeDtypeStruct(got.shape, out_dtype), name=name,
                          compiler_params=pltpu.CompilerParams(dimension_semantics=("arbitrary", "arbitrary")))(core, parts, got)


def _add_four(name, core, a):
    _, r, c = a.shape
    t = _row_tile(r, c)

    def body(core_ref, a0, a1, a2, a3, o_ref):
        o_ref[...] = ((a0[...].astype(f32) + a1[...].astype(f32)) + a2[...].astype(f32)) + a3[...].astype(f32)

    specs = [pl.BlockSpec((None, t, c), functools.partial(lambda i, core_ref, k: (k, i, 0), k=k)) for k in range(N_CHIPS)]
    grid_spec = pltpu.PrefetchScalarGridSpec(
        num_scalar_prefetch=1, grid=(r // t,), in_specs=specs,
        out_specs=pl.BlockSpec((None, t, c), lambda i, core_ref: (core_ref[0], i, 0)))
    return pl.pallas_call(body, grid_spec=grid_spec, out_shape=jax.ShapeDtypeStruct((2, r, c), f32), name=name,
                          compiler_params=pltpu.CompilerParams(dimension_semantics=("arbitrary",)))(core, a, a, a, a)


def _adamw(name, w, g, m, v, row_tile=None):
    c1 = 1.0 - ADAM_B1 ** ADAM_STEP
    c2 = 1.0 - ADAM_B2 ** ADAM_STEP

    def body(w_ref, g_ref, m_ref, v_ref, d_ref, nm_ref, nv_ref):
        gv = g_ref[...]
        nm = ADAM_B1 * m_ref[...] + (1.0 - ADAM_B1) * gv
        nv = ADAM_B2 * v_ref[...] + (1.0 - ADAM_B2) * (gv * gv)
        m_hat = nm / c1
        v_hat = nv / c2
        d_ref[...] = -ADAM_LR * (m_hat / (jnp.sqrt(v_hat) + ADAM_EPS) + ADAM_WD * w_ref[...])
        nm_ref[...] = nm
        nv_ref[...] = nv

    sds = jax.ShapeDtypeStruct(w.shape, f32)
    if row_tile is None:
        return pl.pallas_call(body, out_shape=[sds] * 3, name=name)(w, g, m, v)
    _, r, c = w.shape
    spec = pl.BlockSpec((None, row_tile, c), lambda l, i: (l, i, 0))
    return pl.pallas_call(body, grid=(DEPTH, r // row_tile), in_specs=[spec] * 4, out_specs=[spec] * 3, out_shape=[sds] * 3, name=name,
                          compiler_params=pltpu.CompilerParams(dimension_semantics=("arbitrary", "arbitrary"), vmem_limit_bytes=VMEM_LIMIT),
                          )(w, g, m, v)


def _pad_rows(flat, rows):
    return jnp.pad(flat, (0, rows * LANE - flat.shape[0])).reshape(rows, LANE)


def kernel(x, positions, norm_g, w_in, mla_q_a_norm, mla_w_q_up, mla_kv_a_norm, mla_w_kv_up, mla_q_norm, mla_k_norm, fox_b_f, fox_q_norm, fox_k_norm, s5_lambda_re, s5_lambda_im, s5_log_dt, s5_b_re, s5_b_im, s5_c_re, s5_c_im, s5_d, s5_w_glu, s5_b_glu, w_branch_out, w_out, loss_target, m_norm_g, m_w_in, m_mla_q_a_norm, m_mla_w_q_up, m_mla_kv_a_norm, m_mla_w_kv_up, m_mla_q_norm, m_mla_k_norm, m_fox_b_f, m_fox_q_norm, m_fox_k_norm, m_s5_lambda_re, m_s5_lambda_im, m_s5_log_dt, m_s5_b_re, m_s5_b_im, m_s5_c_re, m_s5_c_im, m_s5_d, m_s5_w_glu, m_s5_b_glu, m_w_branch_out, m_w_out, v_norm_g, v_w_in, v_mla_q_a_norm, v_mla_w_q_up, v_mla_kv_a_norm, v_mla_w_kv_up, v_mla_q_norm, v_mla_k_norm, v_fox_b_f, v_fox_q_norm, v_fox_k_norm, v_s5_lambda_re, v_s5_lambda_im, v_s5_log_dt, v_s5_b_re, v_s5_b_im, v_s5_c_re, v_s5_c_im, v_s5_d, v_s5_w_glu, v_s5_b_glu, v_w_branch_out, v_w_out):
    given = dict(locals())
    wts = {n: given[n] for n in WEIGHTS}
    mom1 = {n: given["m_" + n] for n in WEIGHTS}
    mom2 = {n: given["v_" + n] for n in WEIGHTS}

    def lanes(n, a):
        _, _, c, cp = _BIG_SHARD[n]
        return jnp.pad(a, ((0, 0), (0, 0), (0, cp - c)))

    gathered = _gather_layers("gather_weights", [lanes(n, wts[n].astype(bf16)) for n in BIG])
    big = dict(zip(BIG, gathered))
    small = {n: wts[n] for n in SMALL}

    loss_local, grad_x, grads = _local_step(x[0], positions, loss_target[0], small, big)
    loss = lax.psum(loss_local, ("x", "y", "c"))

    small_flat = jnp.concatenate([grads[n].reshape(-1) for n in SMALL])
    small_rows = -(-small_flat.shape[0] // (N_DEV * 16 * LANE)) * 16
    parts = [grads[n] if n == "w_in" else jnp.stack([_to_shards(n, grads[n][l]) for l in range(DEPTH)]) for n in BIG]
    parts.append(jnp.swapaxes(_pad_rows(small_flat, N_DEV * small_rows).reshape(N_CHIPS, 2, small_rows, LANE), 0, 1))
    core = lax.axis_index("c")
    core1 = core.reshape(1).astype(jnp.int32)
    got = _swap_layers("grads_to_sibling", parts)
    hop = [bf16] * len(BIG) + [f32]
    pair = [_add_pair("grads_pair_sum_%d" % i, core1, a, b, dt) for i, (a, b, dt) in enumerate(zip(parts, got, hop))]
    landed = _scatter_to_chips("grads_to_chips", pair)
    total = [_add_four("grads_chip_sum_%d" % i, core1, a) for i, a in enumerate(landed)]
    shared = _share_layers("grads_share", total[:-1])
    small_mine = lax.dynamic_index_in_dim(total[-1], core, 0, keepdims=False)
    small_all = _all_gather8("gather_small_grads", small_mine).reshape(-1)

    g_out = {n: s[:, :, :_BIG_SHARD[n][2]] for n, s in zip(BIG, shared)}
    pos = 0
    for n in SMALL:
        g_out[n] = small_all[pos:pos + wts[n].size].reshape(wts[n].shape)
        pos += wts[n].size

    delta, new_m, new_v = {}, {}, {}
    for n in WEIGHTS:
        row_tile = _row_tile(*wts[n].shape[1:]) if n in BIG else None
        delta[n], new_m[n], new_v[n] = _adamw("adamw_" + n, wts[n], g_out[n], mom1[n], mom2[n], row_tile)

    return (loss, grad_x[None], *[g_out[n] for n in WEIGHTS], *[delta[n] for n in WEIGHTS],
            *[new_m[n] for n in WEIGHTS], *[new_v[n] for n in WEIGHTS])
```

```python
import functools
import math

import jax
import jax.numpy as jnp
from jax import lax
from jax.experimental import pallas as pl
from jax.experimental.pallas import tpu as pltpu

f32 = jnp.float32
bf16 = jnp.bfloat16

D_MODEL = 1024
DEPTH = 2
EPS = 1e-6
HEADS = 8
MLA_QK = 96
MLA_Q_RANK = 256
MLA_KV_RANK = 128
ROPE = 32
ROPE_THETA = 10000.0
FOX_DIM = 64
S5_GROUPS = 32
S5_GROUP = 16
S5_STATE = 64
S5_LANES = S5_GROUPS * S5_STATE
LANE = 128
S5_BLOCKS = S5_LANES // LANE
TOK = 256
VMEM_LIMIT = 56 * 1024 * 1024

ADAM_LR = 0.001
ADAM_B1 = 0.9
ADAM_B2 = 0.999
ADAM_EPS = 1e-08
ADAM_WD = 0.01
ADAM_STEP = 10

_ORIG = {}
_off = 0
for _n, _w in (("cq", 256), ("ckv", 128), ("kpe", 32), ("fq", 512), ("fk", 512), ("fv", 512), ("ff", 8), ("s5u", 512),
               ("g_mla", 512), ("g_fox", 512), ("g_s5", 512), ("m_mla", 1024), ("m_fox", 1024), ("m_s5", 1024)):
    _ORIG[_n] = (_off, _w)
    _off += _w
_PAD = {"m_mla": (0, 1024, 0), "m_fox": (1024, 1024, 0), "m_s5": (2048, 1024, 0),
        "fq": (3072, 512, 0), "fk": (3584, 512, 0), "fv": (4096, 512, 0), "s5u": (4608, 512, 0),
        "g_mla": (5120, 512, 0), "g_fox": (5632, 512, 0), "g_s5": (6144, 512, 0),
        "cq": (6656, 256, 0), "ckv": (6912, 128, 0), "kpe": (7040, 128, 64), "ff": (7168, 128, 0)}
NP = 7680
_PAD_ORDER = ("m_mla", "m_fox", "m_s5", "fq", "fk", "fv", "s5u", "g_mla", "g_fox", "g_s5", "cq", "ckv", "kpe", "ff")


def _seg(name):
    start, width, _ = _PAD[name]
    return width, start // width


def _nn(a, b):
    return lax.dot_general(a.astype(bf16), b.astype(bf16), (((1,), (0,)), ((), ())), preferred_element_type=f32)


def _nt(a, b):
    return lax.dot_general(a.astype(bf16), b.astype(bf16), (((1,), (1,)), ((), ())), preferred_element_type=f32)


def _tn(a, b):
    return lax.dot_general(a.astype(bf16), b.astype(bf16), (((0,), (0,)), ((), ())), preferred_element_type=f32)


def _rms(x, g, n):
    r = lax.rsqrt(jnp.sum(x * x, axis=-1, keepdims=True) * (1.0 / n) + EPS)
    return x * r * g, r


def _rms_bwd(dy, x, r, g, n):
    xh = x * r
    dg = jnp.sum(dy * xh, axis=0, keepdims=True)
    dxh = dy * g
    dx = r * (dxh - xh * (jnp.sum(dxh * xh, axis=-1, keepdims=True) * (1.0 / n)))
    return dx, dg


def _sigmoid(x):
    return 1.0 / (1.0 + jnp.exp(-x))


_GELU_C = math.sqrt(2.0 / math.pi)


def _gelu(x):
    t = jnp.tanh(_GELU_C * (x + 0.044715 * x * x * x))
    return 0.5 * x * (1.0 + t), t


def _gelu_grad(x, t):
    return 0.5 * (1.0 + t) + 0.5 * x * (1.0 - t * t) * _GELU_C * (1.0 + 3.0 * 0.044715 * x * x)


def _accumulate(ref, val):
    i = pl.program_id(0)

    @pl.when(i == 0)
    def _():
        ref[...] = val

    @pl.when(i > 0)
    def _():
        ref[...] += val


def _rope(x, c, s1, s2):
    return x * c + pltpu.roll(x, LANE - 16, 1) * s1 + pltpu.roll(x, 16, 1) * s2


def _rope_t(d, c, s1, s2):
    return d * c + pltpu.roll(d * s1, 16, 1) + pltpu.roll(d * s2, LANE - 16, 1)


def _const_map(ndim):
    return lambda *_: (0,) * ndim


def _rowwise(name, body, n_tok, tiled_in, full_in, tiled_out, acc_out, tile=TOK):
    in_specs, args = [], []
    for arr, width, blk in tiled_in:
        in_specs.append(pl.BlockSpec((tile, width), functools.partial(lambda i, b: (i, b), b=blk)))
        args.append(arr)
    for arr in full_in:
        in_specs.append(pl.BlockSpec(arr.shape, _const_map(arr.ndim)))
        args.append(arr)
    out_specs, out_shape = [], []
    for width, dt in tiled_out:
        out_specs.append(pl.BlockSpec((tile, width), lambda i: (i, 0)))
        out_shape.append(jax.ShapeDtypeStruct((n_tok, width), dt))
    for shape, dt in acc_out:
        out_specs.append(pl.BlockSpec(shape, _const_map(len(shape))))
        out_shape.append(jax.ShapeDtypeStruct(shape, dt))
    return pl.pallas_call(
        body, grid=(n_tok // tile,), in_specs=in_specs, out_specs=out_specs, out_shape=out_shape, name=name,
        compiler_params=pltpu.CompilerParams(dimension_semantics=("arbitrary",), vmem_limit_bytes=VMEM_LIMIT),
    )(*args)


def _mm(name, a, b, *, mode, grid, a_spec, b_spec, o_spec, out_shape, acc_shape, add=None, add_spec=None):
    nk = grid[2]

    def body(*refs):
        if add is None:
            a_ref, b_ref, o_ref, acc_ref = refs
        else:
            a_ref, b_ref, add_ref, o_ref, acc_ref = refs
        k = pl.program_id(2)

        @pl.when(k == 0)
        def _():
            acc_ref[...] = jnp.zeros_like(acc_ref)

        acc_ref[...] += {"nn": _nn, "nt": _nt, "tn": _tn}[mode](a_ref[...], b_ref[...])

        @pl.when(k == nk - 1)
        def _():
            r = acc_ref[...]
            if add is not None:
                r = r + add_ref[...]
            o_ref[...] = r.astype(o_ref.dtype)

    in_specs = [a_spec, b_spec] + ([add_spec] if add is not None else [])
    args = (a, b) + ((add,) if add is not None else ())
    return pl.pallas_call(
        body, grid=grid, in_specs=in_specs, out_specs=o_spec, out_shape=out_shape, name=name,
        scratch_shapes=[pltpu.VMEM(acc_shape, f32)],
        compiler_params=pltpu.CompilerParams(dimension_semantics=("arbitrary", "arbitrary", "arbitrary"), vmem_limit_bytes=VMEM_LIMIT),
    )(*args)


def _mm_nn(name, a, b, *, m, n, k, tm, tn, tk, out_dtype=f32, a_koff=0):
    return _mm(name, a, b, mode="nn", grid=(m // tm, n // tn, k // tk),
               a_spec=pl.BlockSpec((tm, tk), lambda i, j, kk: (i, kk + a_koff)),
               b_spec=pl.BlockSpec((tk, tn), lambda i, j, kk: (kk, j)),
               o_spec=pl.BlockSpec((tm, tn), lambda i, j, kk: (i, j)),
               out_shape=jax.ShapeDtypeStruct((m, n), out_dtype), acc_shape=(tm, tn))


ATT_KV = 256
ATT_Q = 512


def _attn_common(mla, n_tok):
    qw = 2 * LANE if mla else LANE
    scale = 1.0 / math.sqrt(MLA_QK if mla else FOX_DIM)
    return qw, scale, min(ATT_Q, n_tok)


def _attn_heads(q_ref, mla):
    out = []
    if mla:
        for e in (0, 1):
            qe = q_ref[:, e * LANE:(e + 1) * LANE]
            out.append((qe.astype(f32).T.astype(bf16), qe))
        return out
    q = q_ref[...]
    tq = q.shape[0]
    qt = q.astype(f32).T
    row = lax.broadcasted_iota(jnp.int32, (LANE, tq), 0)
    lane = lax.broadcasted_iota(jnp.int32, (tq, LANE), 1)
    for e in (0, 1):
        out.append((jnp.where((row >= 64) == bool(e), qt, 0.0).astype(bf16),
                    jnp.where((lane >= 64) == bool(e), q, jnp.zeros((), bf16))))
    return out


def _attn_allowed(off, i, tq, mla):
    kpos = off + lax.broadcasted_iota(jnp.int32, (ATT_KV, tq), 0)
    qpos = i * tq + lax.broadcasted_iota(jnp.int32, (ATT_KV, tq), 1)
    return ((kpos // 64) <= (qpos // 64)) if mla else (kpos <= qpos)


def _attn_fwd(name, q, k, v, cum_b, *, mla, n_tok):
    qw, scale, tq = _attn_common(mla, n_tok)
    nq = n_tok // tq
    nkv = n_tok // ATT_KV
    has_bias = cum_b is not None

    def body(*refs):
        if has_bias:
            q_ref, k_ref, v_ref, cb_ref, o_ref, lse_ref, vt_ref = refs
        else:
            q_ref, k_ref, v_ref, o_ref, lse_ref, vt_ref = refs
        i = pl.program_id(1)

        @pl.when(i == 0)
        def _():
            for jb in range(nkv):
                vt_ref[jb] = v_ref[jb * ATT_KV:(jb + 1) * ATT_KV, :].astype(f32).T.astype(bf16)

        heads = _attn_heads(q_ref, mla)

        def step(j, carry, masked):
            off = pl.multiple_of(j * ATT_KV, ATT_KV)
            allowed = _attn_allowed(off, i, tq, mla) if masked else None
            vt = vt_ref[j]
            sts = []
            for e in (0, 1):
                kb = k_ref[pl.ds(off, ATT_KV), e * LANE:(e + 1) * LANE] if mla else k_ref[pl.ds(off, ATT_KV), :]
                sts.append(_nn(kb, heads[e][0]))
            stats = []
            for e in (0, 1):
                m, l, _ = carry[e]
                st = sts[e] * scale
                if has_bias:
                    st = st - jnp.tile(cb_ref[e, pl.ds(off, ATT_KV), :], (1, tq // LANE))
                if masked:
                    st = jnp.where(allowed, st, -1e30)
                m_new = jnp.maximum(m, jnp.max(st, axis=0, keepdims=True))
                alpha = jnp.exp(m - m_new)
                pt = jnp.exp(st - m_new)
                stats.append((m_new, alpha * l + jnp.sum(pt, axis=0, keepdims=True), alpha, pt.astype(bf16)))
            new = []
            for e in (0, 1):
                m_new, l, alpha, pt = stats[e]
                new.append((m_new, l, alpha * carry[e][2] + _nn(vt[64 * e:64 * e + 64, :], pt)))
            return tuple(new)

        init = tuple((jnp.full((1, tq), -1e30, f32), jnp.zeros((1, tq), f32), jnp.zeros((64, tq), f32)) for _ in (0, 1))
        n_full = i * (tq // ATT_KV)
        carry = lax.fori_loop(0, n_full, functools.partial(step, masked=False), init)
        for d in range(tq // ATT_KV):
            carry = step(n_full + d, carry, True)
        o_ref[...] = jnp.concatenate([carry[e][2] / carry[e][1] for e in (0, 1)], axis=0).T
        lse_ref[...] = jnp.zeros_like(lse_ref)
        for e in (0, 1):
            lse_ref[e:e + 1, :] = carry[e][0] + jnp.log(carry[e][1])

    in_specs = [pl.BlockSpec((tq, qw), lambda p, i: (i, p)),
                pl.BlockSpec((n_tok, qw), lambda p, i: (0, p)),
                pl.BlockSpec((n_tok, LANE), lambda p, i: (0, p))]
    args = [q, k, v]
    if has_bias:
        in_specs.append(pl.BlockSpec((2, n_tok, LANE), lambda p, i: (p, 0, 0)))
        args.append(cum_b)
    return pl.pallas_call(
        body, grid=(4, nq), in_specs=in_specs,
        out_specs=[pl.BlockSpec((tq, LANE), lambda p, i: (i, p)), pl.BlockSpec((None, 8, tq), lambda p, i: (p, 0, i))],
        out_shape=[jax.ShapeDtypeStruct((n_tok, 512), f32), jax.ShapeDtypeStruct((4, 8, n_tok), f32)], name=name,
        scratch_shapes=[pltpu.VMEM((nkv, LANE, ATT_KV), bf16)],
        compiler_params=pltpu.CompilerParams(dimension_semantics=("arbitrary", "arbitrary"), vmem_limit_bytes=VMEM_LIMIT),
    )(*args)


def _attn_bwd(name, q, k, v, o, lse, do, cum_b, *, mla, n_tok):
    qw, scale, tq = _attn_common(mla, n_tok)
    nq = n_tok // tq
    nkv = n_tok // ATT_KV
    has_bias = cum_b is not None

    def body(*refs):
        if has_bias:
            q_ref, k_ref, v_ref, o_ref, lse_ref, do_ref, cb_ref, dq_ref, dk_ref, dv_ref, dck_ref, dcq_ref, kt_ref = refs
        else:
            q_ref, k_ref, v_ref, o_ref, lse_ref, do_ref, dq_ref, dk_ref, dv_ref, kt_ref = refs
        p = pl.program_id(0)
        i = pl.program_id(1)

        @pl.when(i == 0)
        def _():
            dk_ref[...] = jnp.zeros_like(dk_ref)
            dv_ref[...] = jnp.zeros_like(dv_ref)
            for jb in range(nkv):
                for c0 in range(0, qw, LANE):
                    kt_ref[jb, c0:c0 + LANE, :] = k_ref[jb * ATT_KV:(jb + 1) * ATT_KV, c0:c0 + LANE].astype(f32).T.astype(bf16)

        if has_bias:
            @pl.when(jnp.logical_and(i == 0, p == 0))
            def _():
                dck_ref[...] = jnp.zeros_like(dck_ref)

        heads = _attn_heads(q_ref, mla)
        do = do_ref[...]
        do_t = do.T
        prod_t = (do * o_ref[...]).T
        row = lax.broadcasted_iota(jnp.int32, (LANE, tq), 0)
        lane = lax.broadcasted_iota(jnp.int32, (tq, LANE), 1)
        lane_k = lax.broadcasted_iota(jnp.int32, (ATT_KV, LANE), 1)
        per_head = []
        for e in (0, 1):
            sel_r = (row >= 64) == bool(e)
            per_head.append((jnp.where(sel_r, do_t, 0.0).astype(bf16),
                             jnp.where((lane >= 64) == bool(e), do, 0.0).astype(bf16),
                             jnp.sum(jnp.where(sel_r, prod_t, 0.0), axis=0, keepdims=True),
                             lse_ref[e:e + 1, :]))
        dq_rows = LANE if mla else 64

        def step(j, carry, masked):
            off = pl.multiple_of(j * ATT_KV, ATT_KV)
            allowed = _attn_allowed(off, i, tq, mla) if masked else None
            vb = v_ref[pl.ds(off, ATT_KV), :]
            kt = kt_ref[j]
            cols = [slice(e * LANE, (e + 1) * LANE) if mla else slice(None) for e in (0, 1)]
            sts = [_nn(k_ref[pl.ds(off, ATT_KV), cols[e]], heads[e][0]) for e in (0, 1)]
            dpts = [_nn(vb, per_head[e][0]) for e in (0, 1)]
            mids = []
            for e in (0, 1):
                _, _, delta, lse_e = per_head[e]
                st = sts[e] * scale
                if has_bias:
                    st = st - jnp.tile(cb_ref[e, pl.ds(off, ATT_KV), :], (1, tq // LANE))
                pt = jnp.exp(st - lse_e)
                if masked:
                    pt = jnp.where(allowed, pt, 0.0)
                dst = pt * (dpts[e] - delta)
                qsum = carry[e][1]
                if has_bias:
                    rs = jnp.sum(dst, axis=1, keepdims=True)
                    dck_ref[pl.ds(off, ATT_KV), :] += jnp.where(lane_k == 2 * p + e, -rs, 0.0)
                    qsum = qsum + jnp.sum(dst, axis=0, keepdims=True)
                mids.append((pt.astype(bf16), dst.astype(bf16), qsum))
            new = []
            for e in (0, 1):
                pt, dst, qsum = mids[e]
                kt_e = kt[e * LANE:(e + 1) * LANE, :] if mla else kt[64 * e:64 * e + 64, :]
                new.append((carry[e][0] + _nn(kt_e, dst) * scale, qsum))
                dk_ref[pl.ds(off, ATT_KV), cols[e]] += _nn(dst, heads[e][1]) * scale
                dv_ref[pl.ds(off, ATT_KV), :] += _nn(pt, per_head[e][1])
            return tuple(new)

        init = tuple((jnp.zeros((dq_rows, tq), f32), jnp.zeros((1, tq), f32)) for _ in (0, 1))
        n_full = i * (tq // ATT_KV)
        carry = lax.fori_loop(0, n_full, functools.partial(step, masked=False), init)
        for d in range(tq // ATT_KV):
            carry = step(n_full + d, carry, True)
        if mla:
            for e in (0, 1):
                dq_ref[:, e * LANE:(e + 1) * LANE] = carry[e][0].T
        else:
            dq_ref[...] = jnp.concatenate([carry[0][0], carry[1][0]], axis=0).T
        if has_bias:
            dcq_ref[...] = jnp.zeros_like(dcq_ref)
            for e in (0, 1):
                dcq_ref[e:e + 1, :] = carry[e][1]

    tile_q = pl.BlockSpec((tq, qw), lambda p, i: (i, p))
    tile_v = pl.BlockSpec((tq, LANE), lambda p, i: (i, p))
    full_k = pl.BlockSpec((n_tok, qw), lambda p, i: (0, p))
    full_v = pl.BlockSpec((n_tok, LANE), lambda p, i: (0, p))
    in_specs = [tile_q, full_k, full_v, tile_v, pl.BlockSpec((None, 8, tq), lambda p, i: (p, 0, i)), tile_v]
    args = [q, k, v, o, lse, do]
    out_specs = [tile_q, full_k, full_v]
    out_shape = [jax.ShapeDtypeStruct((n_tok, 4 * qw), f32), jax.ShapeDtypeStruct((n_tok, 4 * qw), f32),
                 jax.ShapeDtypeStruct((n_tok, 512), f32)]
    if has_bias:
        in_specs.append(pl.BlockSpec((2, n_tok, LANE), lambda p, i: (p, 0, 0)))
        args.append(cum_b)
        out_specs += [pl.BlockSpec((n_tok, LANE), _const_map(2)), pl.BlockSpec((None, 8, tq), lambda p, i: (p, 0, i))]
        out_shape += [jax.ShapeDtypeStruct((n_tok, LANE), f32), jax.ShapeDtypeStruct((4, 8, n_tok), f32)]
    return pl.pallas_call(
        body, grid=(4, nq), in_specs=in_specs, out_specs=out_specs, out_shape=out_shape, name=name,
        scratch_shapes=[pltpu.VMEM((nkv, qw, ATT_KV), bf16)],
        compiler_params=pltpu.CompilerParams(dimension_semantics=("arbitrary", "arbitrary"), vmem_limit_bytes=VMEM_LIMIT),
    )(*args)


def _s5_disc(lr, li, ldt):
    dt = jnp.exp(ldt)
    mag = jnp.exp(lr * dt)
    a_re = mag * jnp.cos(li * dt)
    a_im = mag * jnp.sin(li * dt)
    den = lr * lr + li * li
    f_re = ((a_re - 1.0) * lr + a_im * li) / den
    f_im = (a_im * lr - (a_re - 1.0) * li) / den
    return a_re, a_im, f_re, f_im


def _s5_param_fwd(lr, li, ldt, b_re, b_im):
    def body(lr_ref, li_ref, ldt_ref, br_ref, bi_ref, ar_ref, ai_ref, bbr_ref, bbi_ref):
        a_re, a_im, f_re, f_im = _s5_disc(lr_ref[...], li_ref[...], ldt_ref[...])
        ar_ref[...] = a_re
        ai_ref[...] = a_im
        br, bi = br_ref[...], bi_ref[...]
        bbr_ref[...] = f_re * br - f_im * bi
        bbi_ref[...] = f_re * bi + f_im * br

    col = jax.ShapeDtypeStruct(lr.shape, f32)
    mat = jax.ShapeDtypeStruct(b_re.shape, f32)
    return pl.pallas_call(body, out_shape=[col, col, mat, mat], name="s5_param_fwd")(lr, li, ldt, b_re, b_im)


def _s5_param_bwd(lr, li, ldt, b_re, b_im, da_re, da_im, dbb_re, dbb_im):
    def body(lr_ref, li_ref, ldt_ref, br_ref, bi_ref, dar_ref, dai_ref, gbr_ref, gbi_ref,
             dlr_ref, dli_ref, dldt_ref, dbr_ref, dbi_ref):
        (a_re, a_im, f_re, f_im), vjp = jax.vjp(_s5_disc, lr_ref[...], li_ref[...], ldt_ref[...])
        br, bi, gr, gi = br_ref[...], bi_ref[...], gbr_ref[...], gbi_ref[...]
        dbr_ref[...] = f_re * gr + f_im * gi
        dbi_ref[...] = f_re * gi - f_im * gr
        dfr = jnp.sum(br * gr + bi * gi, axis=-1, keepdims=True)
        dfi = jnp.sum(br * gi - bi * gr, axis=-1, keepdims=True)
        dlr, dli, dldt = vjp((dar_ref[...], dai_ref[...], dfr, dfi))
        dlr_ref[...] = dlr
        dli_ref[...] = dli
        dldt_ref[...] = jnp.sum(dldt.reshape(S5_GROUPS, S5_STATE, 1), axis=1)

    col = jax.ShapeDtypeStruct((S5_LANES, 1), f32)
    mat = jax.ShapeDtypeStruct((S5_LANES, S5_GROUP), f32)
    return pl.pallas_call(body, out_shape=[col, col, jax.ShapeDtypeStruct((S5_GROUPS, 1), f32), mat, mat],
                          name="s5_param_bwd")(lr, li, ldt, b_re, b_im, da_re, da_im, dbb_re, dbb_im)


_SCAN_NB = 4


def _to_streams(a):
    s, c = a.shape
    return jnp.swapaxes(a.reshape(8, s // 8, c), 0, 1).reshape(s, c)


def _from_streams(a):
    s, c = a.shape
    return jnp.swapaxes(a.reshape(s // 8, 8, c), 0, 1).reshape(s, c)


def _s5_scan(name, src, wq, wy, add, y_dtype, a_re8, a_im8, *, reverse, n_tok, grads_of=None):
    rows = n_tok // 8
    nb = _SCAN_NB
    assert nb == 4

    def scan_body(src_ref, w_ref, wy_ref, add_ref, ar_ref, ai_ref, x_ref, y_ref):
        for ri in (0, 1):
            bu = _nn(src_ref[...], w_ref[ri])
            for b in range(nb):
                x_ref[ri, b] = bu[:, b * LANE:(b + 1) * LANE]
        bu_ref = x_ref
        a_r = [ar_ref[b] for b in range(nb)]
        a_i = [ai_ref[b] for b in range(nb)]
        zero = jnp.zeros((8, LANE), f32)
        one = jnp.ones((8, LANE), f32)

        def rows_at(r):
            rr = (rows - 1 - r) if reverse else r
            return pl.ds(pl.multiple_of(rr * 8, 8), 8)

        def pass1(r, carry):
            out = []
            sl = rows_at(r)
            for b in range(nb):
                xr, xi, mr, mi = carry[b]
                nr = a_r[b] * xr - a_i[b] * xi + bu_ref[0, b, sl, :]
                ni = a_r[b] * xi + a_i[b] * xr + bu_ref[1, b, sl, :]
                x_ref[0, b, sl, :] = nr
                x_ref[1, b, sl, :] = ni
                out.append((nr, ni, a_r[b] * mr - a_i[b] * mi, a_r[b] * mi + a_i[b] * mr))
            return tuple(out)

        carry = lax.fori_loop(0, rows, pass1, tuple((zero, zero, one, zero) for _ in range(nb)))
        sub = lax.broadcasted_iota(jnp.int32, (8, LANE), 0)
        feed = []
        for b in range(nb):
            lr_, li_, pr, pi = carry[b]
            fr, fi = zero, zero
            for _ in range(7):
                tr = lr_ + pr * fr - pi * fi
                ti = li_ + pr * fi + pi * fr
                if reverse:
                    fr = jnp.where(sub < 7, pltpu.roll(tr, 7, 0), 0.0)
                    fi = jnp.where(sub < 7, pltpu.roll(ti, 7, 0), 0.0)
                else:
                    fr = jnp.where(sub > 0, pltpu.roll(tr, 1, 0), 0.0)
                    fi = jnp.where(sub > 0, pltpu.roll(ti, 1, 0), 0.0)
            feed.append((fr, fi))

        def pass2(r, carry):
            out = []
            sl = rows_at(r)
            for b in range(nb):
                mr, mi = carry[b]
                fr, fi = feed[b]
                x_ref[0, b, sl, :] += mr * fr - mi * fi
                x_ref[1, b, sl, :] += mr * fi + mi * fr
                out.append((a_r[b] * mr - a_i[b] * mi, a_r[b] * mi + a_i[b] * mr))
            return tuple(out)

        lax.fori_loop(0, rows, pass2, tuple((a_r[b], a_i[b]) for b in range(nb)))

        y = None
        for ri in (0, 1):
            for b in range(nb):
                t = _nn(x_ref[ri, b], wy_ref[ri, b * LANE:(b + 1) * LANE, :])
                y = t if y is None else y + t
        if add is not None:
            y = y + add_ref[...]
        y_ref[...] = y.astype(y_ref.dtype)

    def grads_body(src_ref, xs_ref, u_ref, g_ref, da_ref, dc_ref, db_ref):
        t = lax.broadcasted_iota(jnp.int32, (n_tok, LANE), 0)
        sub = lax.broadcasted_iota(jnp.int32, (8, LANE), 0)

        def prev(v):
            return (jnp.where(t >= 8, pltpu.roll(v, 8, 0), 0.0),
                    jnp.where(sub > 0, pltpu.roll(v[n_tok - 8:, :], 1, 0), 0.0))

        for b in range(nb):
            (xr, hr), (xi, hi) = prev(xs_ref[0, b]), prev(xs_ref[1, b])
            gr, gi = g_ref[0, b], g_ref[1, b]
            gr0, gi0 = gr[0:8, :], gi[0:8, :]
            da_ref[b, 0:1, :] = (jnp.sum(xr * gr + xi * gi, axis=0, keepdims=True)
                                 + jnp.sum(hr * gr0 + hi * gi0, axis=0, keepdims=True))
            da_ref[b, 1:2, :] = (jnp.sum(xr * gi - xi * gr, axis=0, keepdims=True)
                                 + jnp.sum(hr * gi0 - hi * gr0, axis=0, keepdims=True))
            for ri in (0, 1):
                dc_ref[ri, b * LANE:(b + 1) * LANE, :] = _tn(xs_ref[ri, b], src_ref[...])
                db_ref[ri, :, b * LANE:(b + 1) * LANE] = _tn(u_ref[...], g_ref[ri, b])

    n_in = 3 + (add is not None) + 2 * (grads_of is not None)

    def body(*refs):
        ins, rest = list(refs[:n_in]), refs[n_in:]
        src_ref, w_ref, wy_ref = ins[:3]
        add_ref = ins[3] if add is not None else None
        ar_ref, ai_ref = rest[:2]
        if grads_of is None:
            x_ref, y_ref = rest[2:]
            scan_body(src_ref, w_ref, wy_ref, add_ref, ar_ref, ai_ref, x_ref, y_ref)
        else:
            xs_ref, u_ref = ins[-2:]
            y_ref, da_ref, dc_ref, db_ref, x_ref = rest[2:]
            scan_body(src_ref, w_ref, wy_ref, add_ref, ar_ref, ai_ref, x_ref, y_ref)
            grads_body(src_ref, xs_ref, u_ref, x_ref, da_ref, dc_ref, db_ref)

    blk = pl.BlockSpec((2, nb, n_tok, LANE), lambda g: (0, g, 0, 0))
    ablk = pl.BlockSpec((nb, 8, LANE), lambda g: (g, 0, 0))
    col = pl.BlockSpec((n_tok, LANE), lambda g: (0, g))
    in_specs = [col, pl.BlockSpec((2, None, LANE, 512), lambda g: (0, g, 0, 0)), pl.BlockSpec((2, None, 512, LANE), lambda g: (0, g, 0, 0))]
    args = [src, wq, wy]
    if add is not None:
        in_specs.append(col)
        args.append(add)
    y_shape = jax.ShapeDtypeStruct((n_tok, 512), y_dtype)
    x_shape = (2, S5_BLOCKS, n_tok, LANE)
    params = pltpu.CompilerParams(dimension_semantics=("arbitrary",), vmem_limit_bytes=VMEM_LIMIT)
    if grads_of is None:
        return pl.pallas_call(
            body, grid=(S5_BLOCKS // nb,), in_specs=in_specs + [ablk, ablk], out_specs=[blk, col],
            out_shape=[jax.ShapeDtypeStruct(x_shape, f32), y_shape], name=name, compiler_params=params,
        )(*args, a_re8, a_im8)
    return pl.pallas_call(
        body, grid=(S5_BLOCKS // nb,), in_specs=in_specs + [blk, col, ablk, ablk],
        out_specs=[col, pl.BlockSpec((nb, 2, LANE), lambda g: (g, 0, 0)), pl.BlockSpec((2, None, 512, LANE), lambda g: (0, g, 0, 0)),
                   pl.BlockSpec((2, None, LANE, 512), lambda g: (0, g, 0, 0))],
        out_shape=[y_shape, jax.ShapeDtypeStruct((S5_BLOCKS, 2, LANE), f32), jax.ShapeDtypeStruct((2, S5_Q, 512, LANE), f32),
                   jax.ShapeDtypeStruct((2, S5_Q, LANE, 512), f32)],
        scratch_shapes=[pltpu.VMEM((2, nb, n_tok, LANE), f32)], name=name, compiler_params=params,
    )(*args, *grads_of, a_re8, a_im8)


S5_Q = 4


def _bd8(t):
    _, a, b = t.shape
    t = t.reshape(S5_Q, 8, a, 1, b)
    eye = jnp.eye(8, dtype=jnp.bool_).reshape(1, 8, 1, 8, 1)
    return jnp.where(eye, jnp.broadcast_to(t, (S5_Q, 8, a, 8, b)), jnp.zeros((), t.dtype)).reshape(S5_Q, 8 * a, 8 * b)


def _bd8_diag(m, a, b):
    m = m.reshape(S5_Q, 8, a, 8, b)
    eye = jnp.eye(8, dtype=jnp.bool_).reshape(1, 8, 1, 8, 1)
    return jnp.sum(jnp.where(eye, m, 0.0), axis=3).reshape(S5_GROUPS, a, b)


N_CHIPS = 4
_BIG_SHARD = {"w_in": (1, 1024, 1770, 1792), "mla_w_q_up": (1, 256, 192, 256), "mla_w_kv_up": (1, 128, 256, 256),
              "s5_w_glu": (0, 128, 512, 512), "w_branch_out": (0, 384, 1024, 1024), "w_out": (0, 256, 1024, 1024)}


def _to_shards(name, m):
    axis, r, c, cp = _BIG_SHARD[name]
    if axis == 0:
        return m.reshape(N_CHIPS, r, c)
    return jnp.stack([jnp.pad(m[:, j * c:(j + 1) * c], ((0, 0), (0, cp - c))) for j in range(N_CHIPS)])


def _from_shards(name, s):
    axis, r, c, cp = _BIG_SHARD[name]
    if axis == 0:
        return s.reshape(N_CHIPS * r, c)
    return jnp.concatenate([s[j, :, :c] for j in range(N_CHIPS)], axis=1)


def _pad_w_in(w):
    pieces, pos = [], 0
    for name in _PAD_ORDER:
        start, width, inner = _PAD[name]
        o0, ow = _ORIG[name]
        if start + inner > pos:
            pieces.append(jnp.zeros((w.shape[0], start + inner - pos), w.dtype))
        pieces.append(w[:, o0:o0 + ow])
        pos = start + inner + ow
    pieces.append(jnp.zeros((w.shape[0], NP - pos), w.dtype))
    return jnp.concatenate(pieces, axis=1)


def _prep_weights(small, big):
    per_layer = jax.vmap
    w = {}
    w["w_in_shards"] = big["w_in"]
    w["w_in"] = per_layer(lambda s: _pad_w_in(_from_shards("w_in", s)))(big["w_in"])

    def q_up(s):
        wq = _from_shards("mla_w_q_up", s).reshape(MLA_Q_RANK, HEADS, MLA_QK)
        return jnp.pad(wq, ((0, 0), (0, 0), (0, LANE - MLA_QK))).reshape(MLA_Q_RANK, HEADS * LANE)

    def kv_up(s):
        wkv = _from_shards("mla_w_kv_up", s).reshape(MLA_KV_RANK, HEADS, 128)
        wk = jnp.pad(wkv[:, :, :64], ((0, 0), (0, 0), (0, 64))).reshape(MLA_KV_RANK, HEADS * LANE)
        return jnp.concatenate([wk, wkv[:, :, 64:].reshape(MLA_KV_RANK, 512)], axis=1)

    w["wq"] = per_layer(q_up)(big["mla_w_q_up"])
    w["wkv"] = per_layer(kv_up)(big["mla_w_kv_up"])
    for name, key in (("w_glu", "s5_w_glu"), ("wo", "w_branch_out"), ("w_out", "w_out")):
        w[name] = per_layer(functools.partial(_from_shards, key))(big[key])
    row = lambda a: a.astype(f32)[:, None, :]
    lanes = lambda a, n: jnp.pad(row(a), ((0, 0), (0, 0), (0, LANE - n)))
    w["norm_g"] = row(small["norm_g"])
    w["qa_g"] = row(small["mla_q_a_norm"])
    w["kva_g"] = row(small["mla_kv_a_norm"])
    w["qn_g"] = lanes(small["mla_q_norm"], MLA_QK)
    w["kn_g"] = lanes(small["mla_k_norm"], MLA_QK)
    w["fq_g"] = jnp.tile(row(small["fox_q_norm"]), (1, 1, 2))
    w["fk_g"] = jnp.tile(row(small["fox_k_norm"]), (1, 1, 2))
    w["b_f"] = lanes(small["fox_b_f"], HEADS)
    w["lr"] = small["s5_lambda_re"].reshape(DEPTH, S5_LANES, 1)
    w["li"] = small["s5_lambda_im"].reshape(DEPTH, S5_LANES, 1)
    w["ldt"] = jnp.repeat(small["s5_log_dt"], S5_STATE, axis=1).reshape(DEPTH, S5_LANES, 1)
    w["b_re"] = small["s5_b_re"].reshape(DEPTH, S5_LANES, S5_GROUP)
    w["b_im"] = small["s5_b_im"].reshape(DEPTH, S5_LANES, S5_GROUP)
    w["s5_d"] = row(small["s5_d"])
    w["b_glu"] = row(small["s5_b_glu"])
    a_re, a_im, bb_re, bb_im = _s5_param_fwd(w["lr"], w["li"], w["ldt"], w["b_re"], w["b_im"])
    per_group = lambda m: m.reshape(S5_GROUPS, S5_STATE, S5_GROUP)
    pair = lambda f: per_layer(lambda re, im: jnp.stack([f(re), f(im)]).astype(bf16))
    c_re, c_im = small["s5_c_re"], -small["s5_c_im"]
    w["b_cn"] = pair(lambda m: _bd8(jnp.swapaxes(per_group(m), 1, 2)))(bb_re, bb_im)
    w["b_nc"] = pair(lambda m: _bd8(per_group(m)))(bb_re, bb_im)
    w["c_nc"] = pair(lambda m: _bd8(jnp.swapaxes(m, 1, 2)))(c_re, c_im)
    w["c_cn"] = pair(_bd8)(c_re, c_im)
    sublanes = lambda a: jnp.broadcast_to(a.reshape(DEPTH, S5_BLOCKS, 1, LANE), (DEPTH, S5_BLOCKS, 8, LANE))
    w["a_re8"], w["a_im8"], w["a_im8_neg"] = sublanes(a_re), sublanes(a_im), sublanes(-a_im)
    return w


def _fox_halves(x, lane):
    sq = x * x
    lo = jnp.sum(jnp.where(lane < 64, sq, 0.0), axis=-1, keepdims=True)
    hi = jnp.sum(sq, axis=-1, keepdims=True) - lo
    return jnp.where(lane < 64, lax.rsqrt(lo * (1.0 / 64) + EPS), lax.rsqrt(hi * (1.0 / 64) + EPS))


def _fox_halves_bwd(dy, x, r, g, lane):
    xh = x * r
    dxh = dy * g
    pr = dxh * xh
    lo = jnp.sum(jnp.where(lane < 64, pr, 0.0), axis=-1, keepdims=True)
    hi = jnp.sum(pr, axis=-1, keepdims=True) - lo
    mean = jnp.where(lane < 64, lo, hi) * (1.0 / 64)
    return r * (dxh - xh * mean), jnp.sum(dy * xh, axis=0, keepdims=True)


def _mla_recompute(cq, ckv, kpe, c, s1, s2, qa_g, kva_g, wq, wkv):
    cqn, r_cq = _rms(cq, qa_g, MLA_Q_RANK)
    ckvn, r_ckv = _rms(ckv, kva_g, MLA_KV_RANK)
    cqn_b = cqn.astype(bf16)
    ckvn_b = ckvn.astype(bf16)
    q_raw = _nn(cqn_b, wq)
    kv_raw = _nn(ckvn_b, wkv)
    kpe_rot = _rope(kpe, c, s1, s2)
    return cqn_b, r_cq, ckvn_b, r_ckv, q_raw, kv_raw, kpe_rot


def _layer_fwd(x, w, rope_tabs, n_tok):
    c_tab, s1_tab, s2_tab = rope_tabs
    saved = {"x": x}

    def norm_body(x_ref, g_ref, h_ref):
        h_ref[...] = _rms(x_ref[...], g_ref[...], D_MODEL)[0].astype(bf16)

    (h,) = _rowwise("norm_fwd", norm_body, n_tok, [(x, D_MODEL, 0)], [w["norm_g"]], [(D_MODEL, bf16)], [])
    proj = _mm_nn("in_proj", h, w["w_in"], m=n_tok, n=NP, k=D_MODEL, tm=n_tok, tn=512, tk=D_MODEL)
    saved["h"], saved["proj"] = h, proj

    def mla_prep_body(cq_ref, ckv_ref, kpe_ref, c_ref, s1_ref, s2_ref, qa_ref, kva_ref, wq_ref, wkv_ref, qn_g_ref, kn_g_ref,
                      qn_ref, kn_ref, v_ref):
        c, s1, s2 = c_ref[...], s1_ref[...], s2_ref[...]
        _, _, _, _, q_raw, kv_raw, kpe_rot = _mla_recompute(cq_ref[...], ckv_ref[...], kpe_ref[...], c, s1, s2,
                                                            qa_ref[...], kva_ref[...], wq_ref[...], wkv_ref[...])
        for hd in range(HEADS):
            sl = slice(hd * LANE, (hd + 1) * LANE)
            qn_ref[:, sl] = _rms(_rope(q_raw[:, sl], c, s1, s2), qn_g_ref[...], MLA_QK)[0].astype(bf16)
            kn_ref[:, sl] = _rms(kv_raw[:, sl] + kpe_rot, kn_g_ref[...], MLA_QK)[0].astype(bf16)
        v_ref[...] = kv_raw[:, HEADS * LANE:].astype(bf16)

    qn, kn, v_mla = _rowwise(
        "mla_prep", mla_prep_body, n_tok,
        [(proj, *_seg("cq")), (proj, *_seg("ckv")), (proj, *_seg("kpe")), (c_tab, LANE, 0), (s1_tab, LANE, 0), (s2_tab, LANE, 0)],
        [w["qa_g"], w["kva_g"], w["wq"], w["wkv"], w["qn_g"], w["kn_g"]],
        [(HEADS * LANE, bf16), (HEADS * LANE, bf16), (512, bf16)], [])
    y_mla, lse_mla = _attn_fwd("mla_attn_fwd", qn, kn, v_mla, None, mla=True, n_tok=n_tok)
    saved.update(qn=qn, kn=kn, v_mla=v_mla, y_mla=y_mla, lse_mla=lse_mla)

    def fox_prep_body(fq_ref, fk_ref, fv_ref, ff_ref, qg_ref, kg_ref, bf_ref, fqn_ref, fkn_ref, fvb_ref, logf_ref):
        lane = lax.broadcasted_iota(jnp.int32, (TOK, LANE), 1)
        for blk in range(4):
            sl = slice(blk * LANE, (blk + 1) * LANE)
            xq = fq_ref[:, sl]
            fqn_ref[:, sl] = (xq * _fox_halves(xq, lane) * qg_ref[...]).astype(bf16)
            xk = fk_ref[:, sl]
            fkn_ref[:, sl] = (xk * _fox_halves(xk, lane) * kg_ref[...]).astype(bf16)
        fvb_ref[...] = fv_ref[...].astype(bf16)
        z = ff_ref[...] + bf_ref[...]
        logf_ref[...] = jnp.minimum(z, 0.0) - jnp.log(1.0 + jnp.exp(-jnp.abs(z)))

    fqn, fkn, fvb, logf = _rowwise(
        "fox_prep", fox_prep_body, n_tok,
        [(proj, *_seg("fq")), (proj, *_seg("fk")), (proj, *_seg("fv")), (proj, *_seg("ff"))],
        [w["fq_g"], w["fk_g"], w["b_f"]],
        [(512, bf16), (512, bf16), (512, bf16), (LANE, f32)], [])

    def cum_body(x_ref, cum_ref):
        x = x_ref[...]
        t = lax.broadcasted_iota(jnp.int32, x.shape, 0)
        s = 1
        while s < n_tok:
            x = x + jnp.where(t >= s, pltpu.roll(x, s, 0), 0.0)
            s *= 2
        for hd in range(HEADS):
            cum_ref[hd] = jnp.broadcast_to(x[:, hd:hd + 1], (n_tok, LANE))

    cum_b = pl.pallas_call(cum_body, out_shape=jax.ShapeDtypeStruct((HEADS, n_tok, LANE), f32), name="fox_cum")(logf)
    y_fox, lse_fox = _attn_fwd("fox_attn_fwd", fqn, fkn, fvb, cum_b, mla=False, n_tok=n_tok)
    saved.update(fqn=fqn, fkn=fkn, fvb=fvb, cum_b=cum_b, y_fox=y_fox, lse_fox=lse_fox)

    u_w, u_blk = _seg("s5u")
    u_streams = _to_streams(proj[:, u_blk * u_w:(u_blk + 1) * u_w])
    xs, ylin = _s5_scan("s5_scan_fwd", u_streams, w["b_cn"], w["c_nc"], None, f32, w["a_re8"], w["a_im8"], reverse=False, n_tok=n_tok)
    ylin = _from_streams(ylin)

    def s5_post_body(yl_ref, u_ref, d_ref, wg_ref, bg_ref, out_ref):
        y = yl_ref[...] + d_ref[...] * u_ref[...]
        z, _ = _gelu(y)
        out_ref[...] = z * _sigmoid(_nn(z, wg_ref[...]) + bg_ref[...])

    (y_s5,) = _rowwise("s5_post", s5_post_body, n_tok, [(ylin, 512, 0), (proj, u_w, u_blk)],
                       [w["s5_d"], w["w_glu"], w["b_glu"]], [(512, f32)], [])
    saved.update(xs=xs, ylin=ylin, y_s5=y_s5, u_streams=u_streams)

    def merge_body(ym_ref, yf_ref, ys_ref, gm_ref, gf_ref, gs_ref, mm_ref, mf_ref, ms_ref, x_ref, wo_ref, wout_ref, out_ref):
        merged = jnp.zeros((TOK, D_MODEL), f32)
        for b, (y_ref, g_ref, m_ref) in enumerate(((ym_ref, gm_ref, mm_ref), (yf_ref, gf_ref, mf_ref), (ys_ref, gs_ref, ms_ref))):
            g = g_ref[...]
            a = y_ref[...] * (g * _sigmoid(g))
            merged = merged + _sigmoid(m_ref[...]) * _nn(a, wo_ref[b * 512:(b + 1) * 512, :])
        out_ref[...] = x_ref[...] + _nn(merged, wout_ref[...])

    (out,) = _rowwise(
        "merge_fwd", merge_body, n_tok,
        [(y_mla, 512, 0), (y_fox, 512, 0), (y_s5, 512, 0), (proj, *_seg("g_mla")), (proj, *_seg("g_fox")), (proj, *_seg("g_s5")),
         (proj, *_seg("m_mla")), (proj, *_seg("m_fox")), (proj, *_seg("m_s5")), (x, D_MODEL, 0)],
        [w["wo"], w["w_out"]], [(D_MODEL, f32)], [])
    return out, saved


def _layer_bwd(dout, w, sv, rope_tabs, n_tok):
    c_tab, s1_tab, s2_tab = rope_tabs
    proj, x = sv["proj"], sv["x"]
    grads = {}

    def merge_bwd_body(ym_ref, yf_ref, ys_ref, gm_ref, gf_ref, gs_ref, mm_ref, mf_ref, ms_ref, do_ref, wo_ref, wout_ref,
                       dym_ref, dyf_ref, dys_ref, dgm_ref, dgf_ref, dgs_ref, dmm_ref, dmf_ref, dms_ref, dwo_ref, dwout_ref):
        do = do_ref[...]
        branches = ((ym_ref, gm_ref, mm_ref, dym_ref, dgm_ref, dmm_ref), (yf_ref, gf_ref, mf_ref, dyf_ref, dgf_ref, dmf_ref),
                    (ys_ref, gs_ref, ms_ref, dys_ref, dgs_ref, dms_ref))
        acts, outs, sigs = [], [], []
        merged = jnp.zeros((TOK, D_MODEL), f32)
        for b, (y_ref, g_ref, m_ref, _, _, _) in enumerate(branches):
            g = g_ref[...]
            a = (y_ref[...] * (g * _sigmoid(g))).astype(bf16)
            o = _nn(a, wo_ref[b * 512:(b + 1) * 512, :])
            s = _sigmoid(m_ref[...])
            merged = merged + s * o
            acts.append(a)
            outs.append(o)
            sigs.append(s)
        dmerged = _nt(do, wout_ref[...])
        _accumulate(dwout_ref, _tn(merged, do))
        dwo = []
        for b, (y_ref, g_ref, m_ref, dy_ref, dg_ref, dm_ref) in enumerate(branches):
            s, o = sigs[b], outs[b]
            dm_ref[...] = (dmerged * o * s * (1.0 - s)).astype(bf16)
            d_o = dmerged * s
            da = _nt(d_o, wo_ref[b * 512:(b + 1) * 512, :])
            dwo.append(_tn(acts[b], d_o))
            g = g_ref[...]
            sg = _sigmoid(g)
            dy_ref[...] = da * (g * sg)
            dg_ref[...] = (da * y_ref[...] * (sg * (1.0 + g * (1.0 - sg)))).astype(bf16)
        _accumulate(dwo_ref, jnp.concatenate(dwo, axis=0))

    (dy_mla, dy_fox, dy_s5, dg_mla, dg_fox, dg_s5, dm_mla, dm_fox, dm_s5, dwo, dwout) = _rowwise(
        "merge_bwd", merge_bwd_body, n_tok,
        [(sv["y_mla"], 512, 0), (sv["y_fox"], 512, 0), (sv["y_s5"], 512, 0), (proj, *_seg("g_mla")), (proj, *_seg("g_fox")),
         (proj, *_seg("g_s5")), (proj, *_seg("m_mla")), (proj, *_seg("m_fox")), (proj, *_seg("m_s5")), (dout, D_MODEL, 0)],
        [w["wo"], w["w_out"]],
        [(512, f32)] * 3 + [(512, bf16)] * 3 + [(D_MODEL, bf16)] * 3, [((1536, D_MODEL), f32), ((D_MODEL, D_MODEL), f32)])
    grads["w_branch_out"], grads["w_out"] = dwo, dwout

    u_w, u_blk = _seg("s5u")

    def s5_post_bwd_body(yl_ref, u_ref, do_ref, d_ref, wg_ref, bg_ref, dyl_ref, dus_ref, dd_ref, dwg_ref, dbg_ref):
        u = u_ref[...]
        y = yl_ref[...] + d_ref[...] * u
        z, t = _gelu(y)
        s = _sigmoid(_nn(z, wg_ref[...]) + bg_ref[...])
        do = do_ref[...]
        dgl = do * z * s * (1.0 - s)
        dz = do * s + _nt(dgl, wg_ref[...])
        dy = dz * _gelu_grad(y, t)
        dyl_ref[...] = dy.astype(bf16)
        dus_ref[...] = dy * d_ref[...]
        _accumulate(dd_ref, jnp.sum(dy * u, axis=0, keepdims=True))
        _accumulate(dwg_ref, _tn(z, dgl))
        _accumulate(dbg_ref, jnp.sum(dgl, axis=0, keepdims=True))

    dylin, du_skip, dd, dwglu, dbglu = _rowwise(
        "s5_post_bwd", s5_post_bwd_body, n_tok, [(sv["ylin"], 512, 0), (proj, u_w, u_blk), (dy_s5, 512, 0)],
        [w["s5_d"], w["w_glu"], w["b_glu"]], [(512, bf16), (512, f32)], [((1, 512), f32), ((512, 512), f32), ((1, 512), f32)])
    grads["s5_d"], grads["s5_w_glu"], grads["s5_b_glu"] = dd.reshape(512), dwglu, dbglu.reshape(512)

    dylin = _to_streams(dylin)
    ds5u, da, dc_nc, db_cn = _s5_scan("s5_scan_bwd", dylin, w["c_cn"], w["b_nc"], _to_streams(du_skip), bf16, w["a_re8"],
                                      w["a_im8_neg"], reverse=True, n_tok=n_tok, grads_of=(sv["xs"], sv["u_streams"]))
    ds5u = _from_streams(ds5u)
    diag_b = lambda m: jnp.swapaxes(_bd8_diag(m, S5_GROUP, S5_STATE), 1, 2).reshape(S5_LANES, S5_GROUP)
    diag_c = lambda m: jnp.swapaxes(_bd8_diag(m, S5_STATE, S5_GROUP), 1, 2)
    dlr, dli, dldt, db_re, db_im = _s5_param_bwd(
        w["lr"], w["li"], w["ldt"], w["b_re"], w["b_im"], da[:, 0, :].reshape(S5_LANES, 1), da[:, 1, :].reshape(S5_LANES, 1),
        diag_b(db_cn[0]), diag_b(db_cn[1]))
    grads["s5_lambda_re"] = dlr.reshape(S5_GROUPS, S5_STATE)
    grads["s5_lambda_im"] = dli.reshape(S5_GROUPS, S5_STATE)
    grads["s5_log_dt"] = dldt.reshape(S5_GROUPS)
    grads["s5_b_re"] = db_re.reshape(S5_GROUPS, S5_STATE, S5_GROUP)
    grads["s5_b_im"] = db_im.reshape(S5_GROUPS, S5_STATE, S5_GROUP)
    grads["s5_c_re"] = diag_c(dc_nc[0])
    grads["s5_c_im"] = -diag_c(dc_nc[1])

    dfqn, dfkn, dfv, dck, dcq = _attn_bwd("fox_attn_bwd", sv["fqn"], sv["fkn"], sv["fvb"], sv["y_fox"], sv["lse_fox"], dy_fox,
                                          sv["cum_b"], mla=False, n_tok=n_tok)
    dcq = jnp.pad(dcq[:, :2, :].reshape(HEADS, n_tok).T, ((0, 0), (0, LANE - HEADS)))

    def fox_gate_bwd_body(dk_ref, dq_ref, ff_ref, bf_ref, dff_ref, dbf_ref):
        xg = dk_ref[...] + dq_ref[...]
        t = lax.broadcasted_iota(jnp.int32, xg.shape, 0)
        s = 1
        while s < n_tok:
            xg = xg + jnp.where(t < n_tok - s, pltpu.roll(xg, n_tok - s, 0), 0.0)
            s *= 2
        dff = xg * _sigmoid(-(ff_ref[...] + bf_ref[...]))
        dff_ref[...] = dff.astype(bf16)
        dbf_ref[...] = jnp.sum(dff, axis=0, keepdims=True)

    ff_w, ff_blk = _seg("ff")
    dff, dbf = pl.pallas_call(
        fox_gate_bwd_body, grid=(1,),
        in_specs=[pl.BlockSpec((n_tok, LANE), lambda i: (0, 0)), pl.BlockSpec((n_tok, LANE), lambda i: (0, 0)),
                  pl.BlockSpec((n_tok, ff_w), lambda i: (0, ff_blk)), pl.BlockSpec((1, LANE), lambda i: (0, 0))],
        out_specs=[pl.BlockSpec((n_tok, LANE), lambda i: (0, 0)), pl.BlockSpec((1, LANE), lambda i: (0, 0))],
        out_shape=[jax.ShapeDtypeStruct((n_tok, LANE), bf16), jax.ShapeDtypeStruct((1, LANE), f32)], name="fox_gate_bwd",
    )(dck, dcq, proj, w["b_f"])
    grads["fox_b_f"] = dbf[0, :HEADS]

    def fox_prep_bwd_body(fq_ref, fk_ref, dqn_ref, dkn_ref, dv_ref, qg_ref, kg_ref, dfq_ref, dfk_ref, dfv_ref, dqg_ref, dkg_ref):
        lane = lax.broadcasted_iota(jnp.int32, (TOK, LANE), 1)
        dqg = jnp.zeros((1, LANE), f32)
        dkg = jnp.zeros((1, LANE), f32)
        for blk in range(4):
            sl = slice(blk * LANE, (blk + 1) * LANE)
            xq = fq_ref[:, sl]
            dx, dg = _fox_halves_bwd(dqn_ref[:, sl], xq, _fox_halves(xq, lane), qg_ref[...], lane)
            dfq_ref[:, sl] = dx.astype(bf16)
            dqg = dqg + dg
            xk = fk_ref[:, sl]
            dx, dg = _fox_halves_bwd(dkn_ref[:, sl], xk, _fox_halves(xk, lane), kg_ref[...], lane)
            dfk_ref[:, sl] = dx.astype(bf16)
            dkg = dkg + dg
        dfv_ref[...] = dv_ref[...].astype(bf16)
        _accumulate(dqg_ref, dqg + pltpu.roll(dqg, 64, 1))
        _accumulate(dkg_ref, dkg + pltpu.roll(dkg, 64, 1))

    dfq, dfk, dfvb, dfqg, dfkg = _rowwise(
        "fox_prep_bwd", fox_prep_bwd_body, n_tok,
        [(proj, *_seg("fq")), (proj, *_seg("fk")), (dfqn, 512, 0), (dfkn, 512, 0), (dfv, 512, 0)],
        [w["fq_g"], w["fk_g"]], [(512, bf16)] * 3, [((1, LANE), f32)] * 2)
    grads["fox_q_norm"], grads["fox_k_norm"] = dfqg[0, :FOX_DIM], dfkg[0, :FOX_DIM]

    dqn, dkn, dv_mla = _attn_bwd("mla_attn_bwd", sv["qn"], sv["kn"], sv["v_mla"], sv["y_mla"], sv["lse_mla"], dy_mla,
                                 None, mla=True, n_tok=n_tok)

    def mla_prep_bwd_body(cq_ref, ckv_ref, kpe_ref, c_ref, s1_ref, s2_ref, dqn_ref, dkn_ref, dv_ref,
                          qa_ref, kva_ref, wq_ref, wkv_ref, qn_g_ref, kn_g_ref,
                          dcq_ref, dckv_ref, dkpe_ref, dwq_ref, dwkv_ref, dqa_ref, dkva_ref, dqng_ref, dkng_ref):
        c, s1, s2 = c_ref[...], s1_ref[...], s2_ref[...]
        cq, ckv = cq_ref[...], ckv_ref[...]
        cqn_b, r_cq, ckvn_b, r_ckv, q_raw, kv_raw, kpe_rot = _mla_recompute(
            cq, ckv, kpe_ref[...], c, s1, s2, qa_ref[...], kva_ref[...], wq_ref[...], wkv_ref[...])
        lane = lax.broadcasted_iota(jnp.int32, (TOK, LANE), 1)
        dq_raw, dk_raw = [], []
        dkpe_rot = jnp.zeros((TOK, LANE), f32)
        dqng = jnp.zeros((1, LANE), f32)
        dkng = jnp.zeros((1, LANE), f32)
        for hd in range(HEADS):
            sl = slice(hd * LANE, (hd + 1) * LANE)
            q_rot = _rope(q_raw[:, sl], c, s1, s2)
            r = lax.rsqrt(jnp.sum(q_rot * q_rot, axis=-1, keepdims=True) * (1.0 / MLA_QK) + EPS)
            dx, dg = _rms_bwd(dqn_ref[:, sl], q_rot, r, qn_g_ref[...], MLA_QK)
            dqng = dqng + dg
            dq_raw.append(_rope_t(dx, c, s1, s2))
            k_full = kv_raw[:, sl] + kpe_rot
            r = lax.rsqrt(jnp.sum(k_full * k_full, axis=-1, keepdims=True) * (1.0 / MLA_QK) + EPS)
            dx, dg = _rms_bwd(dkn_ref[:, sl], k_full, r, kn_g_ref[...], MLA_QK)
            dkng = dkng + dg
            dk_raw.append(jnp.where(lane < 64, dx, 0.0))
            dkpe_rot = dkpe_rot + dx
        dkpe = _rope_t(dkpe_rot, c, s1, s2)
        dkpe_ref[...] = jnp.where(jnp.logical_and(lane >= 64, lane < 64 + ROPE), dkpe, 0.0).astype(bf16)
        dq_raw = jnp.concatenate(dq_raw, axis=1).astype(bf16)
        dkv_raw = jnp.concatenate(dk_raw + [dv_ref[...]], axis=1).astype(bf16)
        dcqn = _nt(dq_raw, wq_ref[...])
        dckvn = _nt(dkv_raw, wkv_ref[...])
        dx, dg = _rms_bwd(dcqn, cq, r_cq, qa_ref[...], MLA_Q_RANK)
        dcq_ref[...] = dx.astype(bf16)
        _accumulate(dqa_ref, dg)
        dx, dg = _rms_bwd(dckvn, ckv, r_ckv, kva_ref[...], MLA_KV_RANK)
        dckv_ref[...] = dx.astype(bf16)
        _accumulate(dkva_ref, dg)
        _accumulate(dwq_ref, _tn(cqn_b, dq_raw))
        _accumulate(dwkv_ref, _tn(ckvn_b, dkv_raw))
        _accumulate(dqng_ref, dqng)
        _accumulate(dkng_ref, dkng)

    dcq, dckv, dkpe, dwq, dwkv, dqa, dkva, dqng, dkng = _rowwise(
        "mla_prep_bwd", mla_prep_bwd_body, n_tok,
        [(proj, *_seg("cq")), (proj, *_seg("ckv")), (proj, *_seg("kpe")), (c_tab, LANE, 0), (s1_tab, LANE, 0), (s2_tab, LANE, 0),
         (dqn, HEADS * LANE, 0), (dkn, HEADS * LANE, 0), (dv_mla, 512, 0)],
        [w["qa_g"], w["kva_g"], w["wq"], w["wkv"], w["qn_g"], w["kn_g"]],
        [(MLA_Q_RANK, bf16), (LANE, bf16), (LANE, bf16)],
        [((MLA_Q_RANK, HEADS * LANE), f32), ((MLA_KV_RANK, HEADS * LANE + 512), f32), ((1, MLA_Q_RANK), f32),
         ((1, MLA_KV_RANK), f32), ((1, LANE), f32), ((1, LANE), f32)])
    grads["mla_w_q_up"] = dwq.reshape(MLA_Q_RANK, HEADS, LANE)[:, :, :MLA_QK].reshape(MLA_Q_RANK, HEADS * MLA_QK)
    dwk = dwkv[:, :HEADS * LANE].reshape(MLA_KV_RANK, HEADS, LANE)[:, :, :64]
    dwv = dwkv[:, HEADS * LANE:].reshape(MLA_KV_RANK, HEADS, 64)
    grads["mla_w_kv_up"] = jnp.concatenate([dwk, dwv], axis=2).reshape(MLA_KV_RANK, HEADS * 128)
    grads["mla_q_a_norm"], grads["mla_kv_a_norm"] = dqa.reshape(-1), dkva.reshape(-1)
    grads["mla_q_norm"], grads["mla_k_norm"] = dqng[0, :MLA_QK], dkng[0, :MLA_QK]

    _, _, shard_c, shard_cp = _BIG_SHARD["w_in"]
    kpe0 = _PAD["kpe"][2]
    pieces = [dcq, dckv, dkpe[:, kpe0:kpe0 + ROPE], dfq, dfk, dfvb, dff[:, :HEADS], ds5u, dg_mla, dg_fox, dg_s5,
              dm_mla, dm_fox, dm_s5]
    gap = jnp.zeros((n_tok, shard_cp - shard_c), bf16)
    cut, pos = [], 0
    for p in pieces:
        start = 0
        while start < p.shape[1]:
            take = min(p.shape[1] - start, shard_c - pos % shard_c)
            cut.append(p[:, start:start + take])
            start, pos = start + take, pos + take
            if pos % shard_c == 0:
                cut.append(gap)
    dproj = jnp.concatenate(cut, axis=1)
    ct = 256
    per = shard_cp // ct
    dh = _mm("in_proj_dgrad", dproj, w["w_in_shards"], mode="nt", grid=(1, 1, N_CHIPS * per),
             a_spec=pl.BlockSpec((n_tok, ct), lambda i, j, kk: (0, kk)),
             b_spec=pl.BlockSpec((None, D_MODEL, ct), lambda i, j, kk: (kk // per, 0, kk % per)),
             o_spec=pl.BlockSpec((n_tok, D_MODEL), lambda i, j, kk: (0, 0)),
             out_shape=jax.ShapeDtypeStruct((n_tok, D_MODEL), f32), acc_shape=(n_tok, D_MODEL))
    grads["w_in"] = _mm("in_proj_wgrad", sv["h"], dproj, mode="tn", grid=(1, N_CHIPS * per, 1),
                        a_spec=pl.BlockSpec((n_tok, D_MODEL), lambda i, j, kk: (0, 0)),
                        b_spec=pl.BlockSpec((n_tok, ct), lambda i, j, kk: (0, j)),
                        o_spec=pl.BlockSpec((None, D_MODEL, ct), lambda i, j, kk: (j // per, 0, j % per)),
                        out_shape=jax.ShapeDtypeStruct((N_CHIPS, D_MODEL, shard_cp), f32), acc_shape=(D_MODEL, ct))

    def norm_bwd_body(dh_ref, x_ref, do_ref, g_ref, dx_ref, dg_ref):
        xv = x_ref[...]
        r = lax.rsqrt(jnp.sum(xv * xv, axis=-1, keepdims=True) * (1.0 / D_MODEL) + EPS)
        dx, dg = _rms_bwd(dh_ref[...], xv, r, g_ref[...], D_MODEL)
        dx_ref[...] = do_ref[...] + dx
        _accumulate(dg_ref, dg)

    dx, dng = _rowwise("norm_bwd", norm_bwd_body, n_tok, [(dh, D_MODEL, 0), (x, D_MODEL, 0), (dout, D_MODEL, 0)],
                       [w["norm_g"]], [(D_MODEL, f32)], [((1, D_MODEL), f32)])
    grads["norm_g"] = dng.reshape(D_MODEL)
    return dx, grads


def _rope_tables(positions):
    inv = 1.0 / (ROPE_THETA ** (jnp.arange(0, ROPE, 2, dtype=f32) / ROPE))
    ang = positions.astype(f32).reshape(-1, 1) * inv
    cos, sin = jnp.cos(ang), jnp.sin(ang)
    n = ang.shape[0]
    z16, z32, z64 = jnp.zeros((n, 16), f32), jnp.zeros((n, 32), f32), jnp.zeros((n, 64), f32)
    c = jnp.concatenate([jnp.ones((n, 64), f32), cos, cos, z32], axis=1)
    s1 = jnp.concatenate([z64, -sin, z16, z32], axis=1)
    s2 = jnp.concatenate([z64, z16, sin, z32], axis=1)
    return c, s1, s2


BIG = ("w_in", "mla_w_q_up", "mla_w_kv_up", "s5_w_glu", "w_branch_out", "w_out")
SMALL = ("norm_g", "mla_q_a_norm", "mla_kv_a_norm", "mla_q_norm", "mla_k_norm", "fox_b_f", "fox_q_norm", "fox_k_norm",
         "s5_lambda_re", "s5_lambda_im", "s5_log_dt", "s5_b_re", "s5_b_im", "s5_c_re", "s5_c_im", "s5_d", "s5_b_glu")
WEIGHTS = ("norm_g", "w_in", "mla_q_a_norm", "mla_w_q_up", "mla_kv_a_norm", "mla_w_kv_up", "mla_q_norm", "mla_k_norm",
           "fox_b_f", "fox_q_norm", "fox_k_norm", "s5_lambda_re", "s5_lambda_im", "s5_log_dt", "s5_b_re", "s5_b_im",
           "s5_c_re", "s5_c_im", "s5_d", "s5_w_glu", "s5_b_glu", "w_branch_out", "w_out")


def _local_step(x, positions, loss_target, small, big):
    n_tok = x.shape[0]
    tabs = _rope_tables(positions)
    ws, saves = [], []
    hcur = x
    stacked = _prep_weights(small, big)
    for l in range(DEPTH):
        w = {k: v[l] for k, v in stacked.items()}
        hcur, sv = _layer_fwd(hcur, w, tabs, n_tok)
        ws.append(w)
        saves.append(sv)

    def loss_body(y_ref, t_ref, d_ref, l_ref):
        err = y_ref[...] - t_ref[...]
        d_ref[...] = err * (1.0 / D_MODEL)
        tot = jnp.sum(jnp.sum(err * err, axis=-1, keepdims=True), axis=0, keepdims=True)
        _accumulate(l_ref, jnp.broadcast_to(tot * (0.5 / D_MODEL), (1, LANE)))

    dcur, loss = _rowwise("loss", loss_body, n_tok, [(hcur, D_MODEL, 0), (loss_target, D_MODEL, 0)], [], [(D_MODEL, f32)],
                          [((1, LANE), f32)])
    layer_grads = [None] * DEPTH
    for l in reversed(range(DEPTH)):
        dcur, layer_grads[l] = _layer_bwd(dcur, ws[l], saves[l], tabs, n_tok)
    grads = {n: jnp.stack([layer_grads[l][n] for l in range(DEPTH)]) for n in WEIGHTS}
    return loss[0, 0], dcur, grads


N_DEV = 8
_ANY = pl.BlockSpec(memory_space=pl.ANY)
_MESH = pl.DeviceIdType.MESH


def _all_gather8(name, blk):
    m = blk.shape[0]

    def body(x_ref, out_ref, send_sems, recv_sems, local_sem):
        x, y, c = lax.axis_index("x"), lax.axis_index("y"), lax.axis_index("c")
        me, sibling = (x, y, c), (x, y, 1 - c)
        chips = [(1 - x, y), (x, 1 - y), (1 - x, 1 - y)]

        def slot(px, py, pc):
            return out_ref.at[4 * px + 2 * py + pc]

        def copy(k, block, to, src=None):
            return pltpu.make_async_remote_copy(
                src_ref=slot(*block) if src is None else src, dst_ref=slot(*block),
                send_sem=send_sems.at[k], recv_sem=recv_sems.at[k], device_id=to, device_id_type=_MESH)

        mine = pltpu.make_async_copy(x_ref, slot(*me), local_sem)
        mine.start()
        first = [copy(0, me, sibling, src=x_ref)]
        first += [copy(1 + j, me, (*chip, c), src=x_ref) for j, chip in enumerate(chips)]
        for cp in first:
            cp.start()
        passed = [copy(4 + j, (*chip, c), sibling) for j, chip in enumerate(chips)]
        for j, chip in enumerate(chips):
            copy(1 + j, (*chip, c), me).wait_recv()
            passed[j].start()
        copy(0, sibling, me).wait_recv()
        for j, chip in enumerate(chips):
            copy(4 + j, (*chip, 1 - c), me).wait_recv()
        for cp in first + passed:
            cp.wait_send()
        mine.wait()

    return pl.pallas_call(
        body, out_shape=jax.ShapeDtypeStruct((N_DEV, m, LANE), blk.dtype), in_specs=[_ANY], out_specs=_ANY, name=name,
        scratch_shapes=[pltpu.SemaphoreType.DMA((7,)), pltpu.SemaphoreType.DMA((7,)), pltpu.SemaphoreType.DMA],
    )(blk)


def _gather_layers(name, shards):
    n = len(shards)

    def body(*refs):
        x_refs, out_refs = refs[:n], refs[n:2 * n]
        send_sems, recv_sems, local_sems = refs[2 * n:]
        x, y, c = lax.axis_index("x"), lax.axis_index("y"), lax.axis_index("c")
        me, sibling = (x, y, c), (x, y, 1 - c)
        xn, yn, dg = (1 - x, y, c), (x, 1 - y, c), (1 - x, 1 - y, c)
        relay_from = (x + (1 - c) * (1 - 2 * x), y + c * (1 - 2 * y), c)
        relay_to = (x + c * (1 - 2 * x), y + (1 - c) * (1 - 2 * y), c)

        def copy(w, k, block, to, src=None):
            px, py, pc = block
            slot = out_refs[w].at[pc, 2 * px + py]
            return pltpu.make_async_remote_copy(
                src_ref=slot if src is None else src, dst_ref=slot, send_sem=send_sems.at[7 * w + k],
                recv_sem=recv_sems.at[7 * w + k], device_id=to, device_id_type=_MESH)

        started, local = [], []
        for w in range(n):
            src = x_refs[w].at[c]
            mine = pltpu.make_async_copy(src, out_refs[w].at[c, 2 * x + y], local_sems.at[w])
            mine.start()
            local.append(mine)
            first = [copy(w, 0, me, sibling, src=src), copy(w, 1, me, xn, src=src), copy(w, 2, me, yn, src=src)]
            for cp in first:
                cp.start()
            started += first
        for w in range(n):
            copy(w, 1, xn, me).wait_recv()
            copy(w, 2, yn, me).wait_recv()
            onward = [copy(w, 3, relay_from, relay_to), copy(w, 4, xn, sibling), copy(w, 5, yn, sibling)]
            for cp in onward:
                cp.start()
            started += onward
        for w in range(n):
            copy(w, 3, dg, me).wait_recv()
            onward = copy(w, 6, dg, sibling)
            onward.start()
            started.append(onward)
        for w in range(n):
            copy(w, 0, sibling, me).wait_recv()
            for k, chip in ((4, xn), (5, yn), (6, dg)):
                copy(w, k, (chip[0], chip[1], 1 - c), me).wait_recv()
        for cp in started:
            cp.wait_send()
        for cp in local:
            cp.wait()

    return pl.pallas_call(
        body, out_shape=[jax.ShapeDtypeStruct((2, N_CHIPS) + s.shape[1:], s.dtype) for s in shards],
        in_specs=[_ANY] * n, out_specs=[_ANY] * n, name=name,
        scratch_shapes=[pltpu.SemaphoreType.DMA((7 * n,)), pltpu.SemaphoreType.DMA((7 * n,)), pltpu.SemaphoreType.DMA((n,))],
    )(*shards)


def _swap_layers(name, parts):
    n = len(parts)

    def body(*refs):
        p_refs, got_refs = refs[:n], refs[n:2 * n]
        send_sems, recv_sems = refs[2 * n:]
        x, y, c = lax.axis_index("x"), lax.axis_index("y"), lax.axis_index("c")
        copies = []
        for w in range(n):
            cp = pltpu.make_async_remote_copy(
                src_ref=p_refs[w].at[1 - c], dst_ref=got_refs[w], send_sem=send_sems.at[w], recv_sem=recv_sems.at[w],
                device_id=(x, y, 1 - c), device_id_type=_MESH)
            cp.start()
            copies.append(cp)
        for cp in copies:
            cp.wait()

    return pl.pallas_call(
        body, out_shape=[jax.ShapeDtypeStruct(p.shape[1:], p.dtype) for p in parts], in_specs=[_ANY] * n, out_specs=[_ANY] * n,
        name=name, scratch_shapes=[pltpu.SemaphoreType.DMA((n,)), pltpu.SemaphoreType.DMA((n,))],
    )(*parts)


def _scatter_to_chips(name, parts):
    n = len(parts)

    def body(*refs):
        p_refs, out_refs = refs[:n], refs[n:2 * n]
        send_sems, recv_sems, local_sems = refs[2 * n:]
        x, y, c = lax.axis_index("x"), lax.axis_index("y"), lax.axis_index("c")
        jme = 2 * x + y
        chips = [(1 - x, y), (x, 1 - y), (1 - x, 1 - y)]
        sends, local = [], []
        for w in range(n):
            mine = pltpu.make_async_copy(p_refs[w].at[jme], out_refs[w].at[jme], local_sems.at[w])
            mine.start()
            local.append(mine)
            for k, (tx, ty) in enumerate(chips):
                cp = pltpu.make_async_remote_copy(
                    src_ref=p_refs[w].at[2 * tx + ty], dst_ref=out_refs[w].at[jme], send_sem=send_sems.at[3 * w + k],
                    recv_sem=recv_sems.at[3 * w + k], device_id=(tx, ty, c), device_id_type=_MESH)
                cp.start()
                sends.append(cp)
        for w in range(n):
            for k, (tx, ty) in enumerate(chips):
                pltpu.make_async_remote_copy(
                    src_ref=p_refs[w].at[jme], dst_ref=out_refs[w].at[2 * tx + ty], send_sem=send_sems.at[3 * w + k],
                    recv_sem=recv_sems.at[3 * w + k], device_id=(tx, ty, c), device_id_type=_MESH).wait_recv()
        for cp in sends:
            cp.wait_send()
        for cp in local:
            cp.wait()

    return pl.pallas_call(
        body, out_shape=[jax.ShapeDtypeStruct(p.shape, p.dtype) for p in parts], in_specs=[_ANY] * n, out_specs=[_ANY] * n, name=name,
        scratch_shapes=[pltpu.SemaphoreType.DMA((3 * n,)), pltpu.SemaphoreType.DMA((3 * n,)), pltpu.SemaphoreType.DMA((n,))],
    )(*parts)


def _share_layers(name, bufs):
    n = len(bufs)

    def body(*refs):
        out_refs = refs[n:2 * n]
        send_sems, recv_sems = refs[2 * n:]
        x, y, c = lax.axis_index("x"), lax.axis_index("y"), lax.axis_index("c")
        copies = []
        for w in range(n):
            cp = pltpu.make_async_remote_copy(src_ref=out_refs[w].at[c], dst_ref=out_refs[w].at[c], send_sem=send_sems.at[w],
                                              recv_sem=recv_sems.at[w], device_id=(x, y, 1 - c), device_id_type=_MESH)
            cp.start()
            copies.append(cp)
        for w in range(n):
            pltpu.make_async_remote_copy(src_ref=out_refs[w].at[c], dst_ref=out_refs[w].at[1 - c], send_sem=send_sems.at[w],
                                         recv_sem=recv_sems.at[w], device_id=(x, y, 1 - c), device_id_type=_MESH).wait_recv()
        for cp in copies:
            cp.wait_send()

    return pl.pallas_call(
        body, out_shape=[jax.ShapeDtypeStruct(b.shape, b.dtype) for b in bufs], in_specs=[_ANY] * n, out_specs=[_ANY] * n,
        input_output_aliases={w: w for w in range(n)}, name=name,
        scratch_shapes=[pltpu.SemaphoreType.DMA((n,)), pltpu.SemaphoreType.DMA((n,))],
    )(*bufs)


def _row_tile(rows, cols):
    best = 16
    for t in range(16, rows + 1, 16):
        if rows % t == 0 and t * cols * 4 <= 2 * 1024 * 1024:
            best = t
    return best


def _add_pair(name, core, parts, got, out_dtype):
    _, _, r, c = parts.shape
    t = _row_tile(r, c)

    def body(core_ref, a_ref, b_ref, o_ref):
        o_ref[...] = (a_ref[...] + b_ref[...]).astype(o_ref.dtype)

    spec = pl.BlockSpec((None, t, c), lambda j, i, core_ref: (j, i, 0))
    grid_spec = pltpu.PrefetchScalarGridSpec(
        num_scalar_prefetch=1, grid=(N_CHIPS, r // t),
        in_specs=[pl.BlockSpec((None, None, t, c), lambda j, i, core_ref: (core_ref[0], j, i, 0)), spec], out_specs=spec)
    return pl.pallas_call(body, grid_spec=grid_spec, out_shape=jax.ShapeDtypeStruct(got.shape, out_dtype), name=name,
                          compiler_params=pltpu.CompilerParams(dimension_semantics=("arbitrary", "arbitrary")))(core, parts, got)


def _add_four(name, core, a):
    _, r, c = a.shape
    t = _row_tile(r, c)

    def body(core_ref, a0, a1, a2, a3, o_ref):
        o_ref[...] = ((a0[...].astype(f32) + a1[...].astype(f32)) + a2[...].astype(f32)) + a3[...].astype(f32)

    specs = [pl.BlockSpec((None, t, c), functools.partial(lambda i, core_ref, k: (k, i, 0), k=k)) for k in range(N_CHIPS)]
    grid_spec = pltpu.PrefetchScalarGridSpec(
        num_scalar_prefetch=1, grid=(r // t,), in_specs=specs,
        out_specs=pl.BlockSpec((None, t, c), lambda i, core_ref: (core_ref[0], i, 0)))
    return pl.pallas_call(body, grid_spec=grid_spec, out_shape=jax.ShapeDtypeStruct((2, r, c), f32), name=name,
                          compiler_params=pltpu.CompilerParams(dimension_semantics=("arbitrary",)))(core, a, a, a, a)


def _adamw(name, w, g, m, v, row_tile=None):
    c1 = 1.0 - ADAM_B1 ** ADAM_STEP
    c2 = 1.0 - ADAM_B2 ** ADAM_STEP

    def body(w_ref, g_ref, m_ref, v_ref, d_ref, nm_ref, nv_ref):
        gv = g_ref[...]
        nm = ADAM_B1 * m_ref[...] + (1.0 - ADAM_B1) * gv
        nv = ADAM_B2 * v_ref[...] + (1.0 - ADAM_B2) * (gv * gv)
        m_hat = nm / c1
        v_hat = nv / c2
        d_ref[...] = -ADAM_LR * (m_hat / (jnp.sqrt(v_hat) + ADAM_EPS) + ADAM_WD * w_ref[...])
        nm_ref[...] = nm
        nv_ref[...] = nv

    sds = jax.ShapeDtypeStruct(w.shape, f32)
    if row_tile is None:
        return pl.pallas_call(body, out_shape=[sds] * 3, name=name)(w, g, m, v)
    _, r, c = w.shape
    spec = pl.BlockSpec((None, row_tile, c), lambda l, i: (l, i, 0))
    return pl.pallas_call(body, grid=(DEPTH, r // row_tile), in_specs=[spec] * 4, out_specs=[spec] * 3, out_shape=[sds] * 3, name=name,
                          compiler_params=pltpu.CompilerParams(dimension_semantics=("arbitrary", "arbitrary"), vmem_limit_bytes=VMEM_LIMIT),
                          )(w, g, m, v)


def _pad_rows(flat, rows):
    return jnp.pad(flat, (0, rows * LANE - flat.shape[0])).reshape(rows, LANE)


def kernel(x, positions, norm_g, w_in, mla_q_a_norm, mla_w_q_up, mla_kv_a_norm, mla_w_kv_up, mla_q_norm, mla_k_norm, fox_b_f, fox_q_norm, fox_k_norm, s5_lambda_re, s5_lambda_im, s5_log_dt, s5_b_re, s5_b_im, s5_c_re, s5_c_im, s5_d, s5_w_glu, s5_b_glu, w_branch_out, w_out, loss_target, m_norm_g, m_w_in, m_mla_q_a_norm, m_mla_w_q_up, m_mla_kv_a_norm, m_mla_w_kv_up, m_mla_q_norm, m_mla_k_norm, m_fox_b_f, m_fox_q_norm, m_fox_k_norm, m_s5_lambda_re, m_s5_lambda_im, m_s5_log_dt, m_s5_b_re, m_s5_b_im, m_s5_c_re, m_s5_c_im, m_s5_d, m_s5_w_glu, m_s5_b_glu, m_w_branch_out, m_w_out, v_norm_g, v_w_in, v_mla_q_a_norm, v_mla_w_q_up, v_mla_kv_a_norm, v_mla_w_kv_up, v_mla_q_norm, v_mla_k_norm, v_fox_b_f, v_fox_q_norm, v_fox_k_norm, v_s5_lambda_re, v_s5_lambda_im, v_s5_log_dt, v_s5_b_re, v_s5_b_im, v_s5_c_re, v_s5_c_im, v_s5_d, v_s5_w_glu, v_s5_b_glu, v_w_branch_out, v_w_out):
    given = dict(locals())
    wts = {n: given[n] for n in WEIGHTS}
    mom1 = {n: given["m_" + n] for n in WEIGHTS}
    mom2 = {n: given["v_" + n] for n in WEIGHTS}

    def lanes(n, a):
        _, _, c, cp = _BIG_SHARD[n]
        return jnp.pad(a, ((0, 0), (0, 0), (0, cp - c)))

    gathered = _gather_layers("gather_weights", [lanes(n, wts[n].astype(bf16)) for n in BIG])
    big = dict(zip(BIG, gathered))
    small = {n: wts[n] for n in SMALL}

    loss_local, grad_x, grads = _local_step(x[0], positions, loss_target[0], small, big)
    loss = lax.psum(loss_local, ("x", "y", "c"))

    small_flat = jnp.concatenate([grads[n].reshape(-1) for n in SMALL])
    small_rows = -(-small_flat.shape[0] // (N_DEV * 16 * LANE)) * 16
    parts = [grads[n] if n == "w_in" else jnp.stack([_to_shards(n, grads[n][l]) for l in range(DEPTH)]) for n in BIG]
    parts.append(jnp.swapaxes(_pad_rows(small_flat, N_DEV * small_rows).reshape(N_CHIPS, 2, small_rows, LANE), 0, 1))
    core = lax.axis_index("c")
    core1 = core.reshape(1).astype(jnp.int32)
    got = _swap_layers("grads_to_sibling", parts)
    hop = [bf16] * len(BIG) + [f32]
    pair = [_add_pair("grads_pair_sum_%d" % i, core1, a, b, dt) for i, (a, b, dt) in enumerate(zip(parts, got, hop))]
    landed = _scatter_to_chips("grads_to_chips", pair)
    total = [_add_four("grads_chip_sum_%d" % i, core1, a) for i, a in enumerate(landed)]
    shared = _share_layers("grads_share", total[:-1])
    small_mine = lax.dynamic_index_in_dim(total[-1], core, 0, keepdims=False)
    small_all = _all_gather8("gather_small_grads", small_mine).reshape(-1)

    g_out = {n: s[:, :, :_BIG_SHARD[n][2]] for n, s in zip(BIG, shared)}
    pos = 0
    for n in SMALL:
        g_out[n] = small_all[pos:pos + wts[n].size].reshape(wts[n].shape)
        pos += wts[n].size

    delta, new_m, new_v = {}, {}, {}
    for n in WEIGHTS:
        row_tile = _row_tile(*wts[n].shape[1:]) if n in BIG else None
        delta[n], new_m[n], new_v[n] = _adamw("adamw_" + n, wts[n], g_out[n], mom1[n], mom2[n], row_tile)

    return (loss, grad_x[None], *[g_out[n] for n in WEIGHTS], *[delta[n] for n in WEIGHTS],
            *[new_m[n] for n in WEIGHTS], *[new_v[n] for n in WEIGHTS])
```

```python
import functools
import math

import jax
import jax.numpy as jnp
from jax import lax
from jax.experimental import pallas as pl
from jax.experimental.pallas import tpu as pltpu

f32 = jnp.float32
bf16 = jnp.bfloat16

D_MODEL = 1024
DEPTH = 2
EPS = 1e-6
HEADS = 8
MLA_QK = 96
MLA_Q_RANK = 256
MLA_KV_RANK = 128
ROPE = 32
ROPE_THETA = 10000.0
FOX_DIM = 64
S5_GROUPS = 32
S5_GROUP = 16
S5_STATE = 64
S5_LANES = S5_GROUPS * S5_STATE
LANE = 128
S5_BLOCKS = S5_LANES // LANE
TOK = 256
VMEM_LIMIT = 56 * 1024 * 1024

ADAM_LR = 0.001
ADAM_B1 = 0.9
ADAM_B2 = 0.999
ADAM_EPS = 1e-08
ADAM_WD = 0.01
ADAM_STEP = 10

_ORIG = {}
_off = 0
for _n, _w in (("cq", 256), ("ckv", 128), ("kpe", 32), ("fq", 512), ("fk", 512), ("fv", 512), ("ff", 8), ("s5u", 512),
               ("g_mla", 512), ("g_fox", 512), ("g_s5", 512), ("m_mla", 1024), ("m_fox", 1024), ("m_s5", 1024)):
    _ORIG[_n] = (_off, _w)
    _off += _w
_PAD = {"m_mla": (0, 1024, 0), "m_fox": (1024, 1024, 0), "m_s5": (2048, 1024, 0),
        "fq": (3072, 512, 0), "fk": (3584, 512, 0), "fv": (4096, 512, 0), "s5u": (4608, 512, 0),
        "g_mla": (5120, 512, 0), "g_fox": (5632, 512, 0), "g_s5": (6144, 512, 0),
        "cq": (6656, 256, 0), "ckv": (6912, 128, 0), "kpe": (7040, 128, 64), "ff": (7168, 128, 0)}
NP = 7680
_PAD_ORDER = ("m_mla", "m_fox", "m_s5", "fq", "fk", "fv", "s5u", "g_mla", "g_fox", "g_s5", "cq", "ckv", "kpe", "ff")


def _seg(name):
    start, width, _ = _PAD[name]
    return width, start // width


def _nn(a, b):
    return lax.dot_general(a.astype(bf16), b.astype(bf16), (((1,), (0,)), ((), ())), preferred_element_type=f32)


def _nt(a, b):
    return lax.dot_general(a.astype(bf16), b.astype(bf16), (((1,), (1,)), ((), ())), preferred_element_type=f32)


def _tn(a, b):
    return lax.dot_general(a.astype(bf16), b.astype(bf16), (((0,), (0,)), ((), ())), preferred_element_type=f32)


def _rms(x, g, n):
    r = lax.rsqrt(jnp.sum(x * x, axis=-1, keepdims=True) * (1.0 / n) + EPS)
    return x * r * g, r


def _rms_bwd(dy, x, r, g, n):
    xh = x * r
    dg = jnp.sum(dy * xh, axis=0, keepdims=True)
    dxh = dy * g
    dx = r * (dxh - xh * (jnp.sum(dxh * xh, axis=-1, keepdims=True) * (1.0 / n)))
    return dx, dg


def _sigmoid(x):
    return 1.0 / (1.0 + jnp.exp(-x))


_GELU_C = math.sqrt(2.0 / math.pi)


def _gelu(x):
    t = jnp.tanh(_GELU_C * (x + 0.044715 * x * x * x))
    return 0.5 * x * (1.0 + t), t


def _gelu_grad(x, t):
    return 0.5 * (1.0 + t) + 0.5 * x * (1.0 - t * t) * _GELU_C * (1.0 + 3.0 * 0.044715 * x * x)


def _accumulate(ref, val):
    i = pl.program_id(0)

    @pl.when(i == 0)
    def _():
        ref[...] = val

    @pl.when(i > 0)
    def _():
        ref[...] += val


def _rope(x, c, s1, s2):
    return x * c + pltpu.roll(x, LANE - 16, 1) * s1 + pltpu.roll(x, 16, 1) * s2


def _rope_t(d, c, s1, s2):
    return d * c + pltpu.roll(d * s1, 16, 1) + pltpu.roll(d * s2, LANE - 16, 1)


def _const_map(ndim):
    return lambda *_: (0,) * ndim


def _rowwise(name, body, n_tok, tiled_in, full_in, tiled_out, acc_out, tile=TOK):
    in_specs, args = [], []
    for arr, width, blk in tiled_in:
        in_specs.append(pl.BlockSpec((tile, width), functools.partial(lambda i, b: (i, b), b=blk)))
        args.append(arr)
    for arr in full_in:
        in_specs.append(pl.BlockSpec(arr.shape, _const_map(arr.ndim)))
        args.append(arr)
    out_specs, out_shape = [], []
    for width, dt in tiled_out:
        out_specs.append(pl.BlockSpec((tile, width), lambda i: (i, 0)))
        out_shape.append(jax.ShapeDtypeStruct((n_tok, width), dt))
    for shape, dt in acc_out:
        out_specs.append(pl.BlockSpec(shape, _const_map(len(shape))))
        out_shape.append(jax.ShapeDtypeStruct(shape, dt))
    return pl.pallas_call(
        body, grid=(n_tok // tile,), in_specs=in_specs, out_specs=out_specs, out_shape=out_shape, name=name,
        compiler_params=pltpu.CompilerParams(dimension_semantics=("arbitrary",), vmem_limit_bytes=VMEM_LIMIT),
    )(*args)


def _mm(name, a, b, *, mode, grid, a_spec, b_spec, o_spec, out_shape, acc_shape, add=None, add_spec=None):
    nk = grid[2]

    def body(*refs):
        if add is None:
            a_ref, b_ref, o_ref, acc_ref = refs
        else:
            a_ref, b_ref, add_ref, o_ref, acc_ref = refs
        k = pl.program_id(2)

        @pl.when(k == 0)
        def _():
            acc_ref[...] = jnp.zeros_like(acc_ref)

        acc_ref[...] += {"nn": _nn, "nt": _nt, "tn": _tn}[mode](a_ref[...], b_ref[...])

        @pl.when(k == nk - 1)
        def _():
            r = acc_ref[...]
            if add is not None:
                r = r + add_ref[...]
            o_ref[...] = r.astype(o_ref.dtype)

    in_specs = [a_spec, b_spec] + ([add_spec] if add is not None else [])
    args = (a, b) + ((add,) if add is not None else ())
    return pl.pallas_call(
        body, grid=grid, in_specs=in_specs, out_specs=o_spec, out_shape=out_shape, name=name,
        scratch_shapes=[pltpu.VMEM(acc_shape, f32)],
        compiler_params=pltpu.CompilerParams(dimension_semantics=("arbitrary", "arbitrary", "arbitrary"), vmem_limit_bytes=VMEM_LIMIT),
    )(*args)


def _mm_nn(name, a, b, *, m, n, k, tm, tn, tk, out_dtype=f32, a_koff=0):
    return _mm(name, a, b, mode="nn", grid=(m // tm, n // tn, k // tk),
               a_spec=pl.BlockSpec((tm, tk), lambda i, j, kk: (i, kk + a_koff)),
               b_spec=pl.BlockSpec((tk, tn), lambda i, j, kk: (kk, j)),
               o_spec=pl.BlockSpec((tm, tn), lambda i, j, kk: (i, j)),
               out_shape=jax.ShapeDtypeStruct((m, n), out_dtype), acc_shape=(tm, tn))


ATT_KV = 256
ATT_Q = 512


def _attn_common(mla, n_tok):
    qw = 2 * LANE if mla else LANE
    scale = 1.0 / math.sqrt(MLA_QK if mla else FOX_DIM)
    return qw, scale, min(ATT_Q, n_tok)


def _attn_heads(q_ref, mla):
    out = []
    if mla:
        for e in (0, 1):
            qe = q_ref[:, e * LANE:(e + 1) * LANE]
            out.append((qe.astype(f32).T.astype(bf16), qe))
        return out
    q = q_ref[...]
    tq = q.shape[0]
    qt = q.astype(f32).T
    row = lax.broadcasted_iota(jnp.int32, (LANE, tq), 0)
    lane = lax.broadcasted_iota(jnp.int32, (tq, LANE), 1)
    for e in (0, 1):
        out.append((jnp.where((row >= 64) == bool(e), qt, 0.0).astype(bf16),
                    jnp.where((lane >= 64) == bool(e), q, jnp.zeros((), bf16))))
    return out


def _attn_allowed(off, i, tq, mla):
    kpos = off + lax.broadcasted_iota(jnp.int32, (ATT_KV, tq), 0)
    qpos = i * tq + lax.broadcasted_iota(jnp.int32, (ATT_KV, tq), 1)
    return ((kpos // 64) <= (qpos // 64)) if mla else (kpos <= qpos)


def _attn_fwd(name, q, k, v, cum_b, *, mla, n_tok):
    qw, scale, tq = _attn_common(mla, n_tok)
    nq = n_tok // tq
    nkv = n_tok // ATT_KV
    has_bias = cum_b is not None

    def body(*refs):
        if has_bias:
            q_ref, k_ref, v_ref, cb_ref, o_ref, lse_ref, vt_ref = refs
        else:
            q_ref, k_ref, v_ref, o_ref, lse_ref, vt_ref = refs
        i = pl.program_id(1)

        @pl.when(i == 0)
        def _():
            for jb in range(nkv):
                vt_ref[jb] = v_ref[jb * ATT_KV:(jb + 1) * ATT_KV, :].astype(f32).T.astype(bf16)

        heads = _attn_heads(q_ref, mla)

        def step(j, carry, masked):
            off = pl.multiple_of(j * ATT_KV, ATT_KV)
            allowed = _attn_allowed(off, i, tq, mla) if masked else None
            vt = vt_ref[j]
            sts = []
            for e in (0, 1):
                kb = k_ref[pl.ds(off, ATT_KV), e * LANE:(e + 1) * LANE] if mla else k_ref[pl.ds(off, ATT_KV), :]
                sts.append(_nn(kb, heads[e][0]))
            stats = []
            for e in (0, 1):
                m, l, _ = carry[e]
                st = sts[e] * scale
                if has_bias:
                    st = st - jnp.tile(cb_ref[e, pl.ds(off, ATT_KV), :], (1, tq // LANE))
                if masked:
                    st = jnp.where(allowed, st, -1e30)
                m_new = jnp.maximum(m, jnp.max(st, axis=0, keepdims=True))
                alpha = jnp.exp(m - m_new)
                pt = jnp.exp(st - m_new)
                stats.append((m_new, alpha * l + jnp.sum(pt, axis=0, keepdims=True), alpha, pt.astype(bf16)))
            new = []
            for e in (0, 1):
                m_new, l, alpha, pt = stats[e]
                new.append((m_new, l, alpha * carry[e][2] + _nn(vt[64 * e:64 * e + 64, :], pt)))
            return tuple(new)

        init = tuple((jnp.full((1, tq), -1e30, f32), jnp.zeros((1, tq), f32), jnp.zeros((64, tq), f32)) for _ in (0, 1))
        n_full = i * (tq // ATT_KV)
        carry = lax.fori_loop(0, n_full, functools.partial(step, masked=False), init)
        for d in range(tq // ATT_KV):
            carry = step(n_full + d, carry, True)
        o_ref[...] = jnp.concatenate([carry[e][2] / carry[e][1] for e in (0, 1)], axis=0).T
        lse_ref[...] = jnp.zeros_like(lse_ref)
        for e in (0, 1):
            lse_ref[e:e + 1, :] = carry[e][0] + jnp.log(carry[e][1])

    in_specs = [pl.BlockSpec((tq, qw), lambda p, i: (i, p)),
                pl.BlockSpec((n_tok, qw), lambda p, i: (0, p)),
                pl.BlockSpec((n_tok, LANE), lambda p, i: (0, p))]
    args = [q, k, v]
    if has_bias:
        in_specs.append(pl.BlockSpec((2, n_tok, LANE), lambda p, i: (p, 0, 0)))
        args.append(cum_b)
    return pl.pallas_call(
        body, grid=(4, nq), in_specs=in_specs,
        out_specs=[pl.BlockSpec((tq, LANE), lambda p, i: (i, p)), pl.BlockSpec((None, 8, tq), lambda p, i: (p, 0, i))],
        out_shape=[jax.ShapeDtypeStruct((n_tok, 512), f32), jax.ShapeDtypeStruct((4, 8, n_tok), f32)], name=name,
        scratch_shapes=[pltpu.VMEM((nkv, LANE, ATT_KV), bf16)],
        compiler_params=pltpu.CompilerParams(dimension_semantics=("arbitrary", "arbitrary"), vmem_limit_bytes=VMEM_LIMIT),
    )(*args)


def _attn_bwd(name, q, k, v, o, lse, do, cum_b, *, mla, n_tok):
    qw, scale, tq = _attn_common(mla, n_tok)
    nq = n_tok // tq
    nkv = n_tok // ATT_KV
    has_bias = cum_b is not None

    def body(*refs):
        if has_bias:
            q_ref, k_ref, v_ref, o_ref, lse_ref, do_ref, cb_ref, dq_ref, dk_ref, dv_ref, dck_ref, dcq_ref, kt_ref = refs
        else:
            q_ref, k_ref, v_ref, o_ref, lse_ref, do_ref, dq_ref, dk_ref, dv_ref, kt_ref = refs
        p = pl.program_id(0)
        i = pl.program_id(1)

        @pl.when(i == 0)
        def _():
            dk_ref[...] = jnp.zeros_like(dk_ref)
            dv_ref[...] = jnp.zeros_like(dv_ref)
            for jb in range(nkv):
                for c0 in range(0, qw, LANE):
                    kt_ref[jb, c0:c0 + LANE, :] = k_ref[jb * ATT_KV:(jb + 1) * ATT_KV, c0:c0 + LANE].astype(f32).T.astype(bf16)

        if has_bias:
            @pl.when(jnp.logical_and(i == 0, p == 0))
            def _():
                dck_ref[...] = jnp.zeros_like(dck_ref)

        heads = _attn_heads(q_ref, mla)
        do = do_ref[...]
        do_t = do.T
        prod_t = (do * o_ref[...]).T
        row = lax.broadcasted_iota(jnp.int32, (LANE, tq), 0)
        lane = lax.broadcasted_iota(jnp.int32, (tq, LANE), 1)
        lane_k = lax.broadcasted_iota(jnp.int32, (ATT_KV, LANE), 1)
        per_head = []
        for e in (0, 1):
            sel_r = (row >= 64) == bool(e)
            per_head.append((jnp.where(sel_r, do_t, 0.0).astype(bf16),
                             jnp.where((lane >= 64) == bool(e), do, 0.0).astype(bf16),
                             jnp.sum(jnp.where(sel_r, prod_t, 0.0), axis=0, keepdims=True),
                             lse_ref[e:e + 1, :]))
        dq_rows = LANE if mla else 64

        def step(j, carry, masked):
            off = pl.multiple_of(j * ATT_KV, ATT_KV)
            allowed = _attn_allowed(off, i, tq, mla) if masked else None
            vb = v_ref[pl.ds(off, ATT_KV), :]
            kt = kt_ref[j]
            cols = [slice(e * LANE, (e + 1) * LANE) if mla else slice(None) for e in (0, 1)]
            sts = [_nn(k_ref[pl.ds(off, ATT_KV), cols[e]], heads[e][0]) for e in (0, 1)]
            dpts = [_nn(vb, per_head[e][0]) for e in (0, 1)]
            mids = []
            for e in (0, 1):
                _, _, delta, lse_e = per_head[e]
                st = sts[e] * scale
                if has_bias:
                    st = st - jnp.tile(cb_ref[e, pl.ds(off, ATT_KV), :], (1, tq // LANE))
                pt = jnp.exp(st - lse_e)
                if masked:
                    pt = jnp.where(allowed, pt, 0.0)
                dst = pt * (dpts[e] - delta)
                qsum = carry[e][1]
                if has_bias:
                    rs = jnp.sum(dst, axis=1, keepdims=True)
                    dck_ref[pl.ds(off, ATT_KV), :] += jnp.where(lane_k == 2 * p + e, -rs, 0.0)
                    qsum = qsum + jnp.sum(dst, axis=0, keepdims=True)
                mids.append((pt.astype(bf16), dst.astype(bf16), qsum))
            new = []
            for e in (0, 1):
                pt, dst, qsum = mids[e]
                kt_e = kt[e * LANE:(e + 1) * LANE, :] if mla else kt[64 * e:64 * e + 64, :]
                new.append((carry[e][0] + _nn(kt_e, dst) * scale, qsum))
                dk_ref[pl.ds(off, ATT_KV), cols[e]] += _nn(dst, heads[e][1]) * scale
                dv_ref[pl.ds(off, ATT_KV), :] += _nn(pt, per_head[e][1])
            return tuple(new)

        init = tuple((jnp.zeros((dq_rows, tq), f32), jnp.zeros((1, tq), f32)) for _ in (0, 1))
        n_full = i * (tq // ATT_KV)
        carry = lax.fori_loop(0, n_full, functools.partial(step, masked=False), init)
        for d in range(tq // ATT_KV):
            carry = step(n_full + d, carry, True)
        if mla:
            for e in (0, 1):
                dq_ref[:, e * LANE:(e + 1) * LANE] = carry[e][0].T
        else:
            dq_ref[...] = jnp.concatenate([carry[0][0], carry[1][0]], axis=0).T
        if has_bias:
            dcq_ref[...] = jnp.zeros_like(dcq_ref)
            for e in (0, 1):
                dcq_ref[e:e + 1, :] = carry[e][1]

    tile_q = pl.BlockSpec((tq, qw), lambda p, i: (i, p))
    tile_v = pl.BlockSpec((tq, LANE), lambda p, i: (i, p))
    full_k = pl.BlockSpec((n_tok, qw), lambda p, i: (0, p))
    full_v = pl.BlockSpec((n_tok, LANE), lambda p, i: (0, p))
    in_specs = [tile_q, full_k, full_v, tile_v, pl.BlockSpec((None, 8, tq), lambda p, i: (p, 0, i)), tile_v]
    args = [q, k, v, o, lse, do]
    out_specs = [tile_q, full_k, full_v]
    out_shape = [jax.ShapeDtypeStruct((n_tok, 4 * qw), f32), jax.ShapeDtypeStruct((n_tok, 4 * qw), f32),
                 jax.ShapeDtypeStruct((n_tok, 512), f32)]
    if has_bias:
        in_specs.append(pl.BlockSpec((2, n_tok, LANE), lambda p, i: (p, 0, 0)))
        args.append(cum_b)
        out_specs += [pl.BlockSpec((n_tok, LANE), _const_map(2)), pl.BlockSpec((None, 8, tq), lambda p, i: (p, 0, i))]
        out_shape += [jax.ShapeDtypeStruct((n_tok, LANE), f32), jax.ShapeDtypeStruct((4, 8, n_tok), f32)]
    return pl.pallas_call(
        body, grid=(4, nq), in_specs=in_specs, out_specs=out_specs, out_shape=out_shape, name=name,
        scratch_shapes=[pltpu.VMEM((nkv, qw, ATT_KV), bf16)],
        compiler_params=pltpu.CompilerParams(dimension_semantics=("arbitrary", "arbitrary"), vmem_limit_bytes=VMEM_LIMIT),
    )(*args)


def _s5_disc(lr, li, ldt):
    dt = jnp.exp(ldt)
    mag = jnp.exp(lr * dt)
    a_re = mag * jnp.cos(li * dt)
    a_im = mag * jnp.sin(li * dt)
    den = lr * lr + li * li
    f_re = ((a_re - 1.0) * lr + a_im * li) / den
    f_im = (a_im * lr - (a_re - 1.0) * li) / den
    return a_re, a_im, f_re, f_im


def _s5_param_fwd(lr, li, ldt, b_re, b_im):
    def body(lr_ref, li_ref, ldt_ref, br_ref, bi_ref, ar_ref, ai_ref, bbr_ref, bbi_ref):
        a_re, a_im, f_re, f_im = _s5_disc(lr_ref[...], li_ref[...], ldt_ref[...])
        ar_ref[...] = a_re
        ai_ref[...] = a_im
        br, bi = br_ref[...], bi_ref[...]
        bbr_ref[...] = f_re * br - f_im * bi
        bbi_ref[...] = f_re * bi + f_im * br

    col = jax.ShapeDtypeStruct(lr.shape, f32)
    mat = jax.ShapeDtypeStruct(b_re.shape, f32)
    return pl.pallas_call(body, out_shape=[col, col, mat, mat], name="s5_param_fwd")(lr, li, ldt, b_re, b_im)


def _s5_param_bwd(lr, li, ldt, b_re, b_im, da_re, da_im, dbb_re, dbb_im):
    def body(lr_ref, li_ref, ldt_ref, br_ref, bi_ref, dar_ref, dai_ref, gbr_ref, gbi_ref,
             dlr_ref, dli_ref, dldt_ref, dbr_ref, dbi_ref):
        (a_re, a_im, f_re, f_im), vjp = jax.vjp(_s5_disc, lr_ref[...], li_ref[...], ldt_ref[...])
        br, bi, gr, gi = br_ref[...], bi_ref[...], gbr_ref[...], gbi_ref[...]
        dbr_ref[...] = f_re * gr + f_im * gi
        dbi_ref[...] = f_re * gi - f_im * gr
        dfr = jnp.sum(br * gr + bi * gi, axis=-1, keepdims=True)
        dfi = jnp.sum(br * gi - bi * gr, axis=-1, keepdims=True)
        dlr, dli, dldt = vjp((dar_ref[...], dai_ref[...], dfr, dfi))
        dlr_ref[...] = dlr
        dli_ref[...] = dli
        dldt_ref[...] = jnp.sum(dldt.reshape(S5_GROUPS, S5_STATE, 1), axis=1)

    col = jax.ShapeDtypeStruct((S5_LANES, 1), f32)
    mat = jax.ShapeDtypeStruct((S5_LANES, S5_GROUP), f32)
    return pl.pallas_call(body, out_shape=[col, col, jax.ShapeDtypeStruct((S5_GROUPS, 1), f32), mat, mat],
                          name="s5_param_bwd")(lr, li, ldt, b_re, b_im, da_re, da_im, dbb_re, dbb_im)


_SCAN_NB = 4


def _to_streams(a):
    s, c = a.shape
    return jnp.swapaxes(a.reshape(8, s // 8, c), 0, 1).reshape(s, c)


def _from_streams(a):
    s, c = a.shape
    return jnp.swapaxes(a.reshape(s // 8, 8, c), 0, 1).reshape(s, c)


def _s5_scan(name, src, wq, wy, add, y_dtype, a_re8, a_im8, *, reverse, n_tok, grads_of=None):
    rows = n_tok // 8
    nb = _SCAN_NB
    assert nb == 4

    def scan_body(src_ref, w_ref, wy_ref, add_ref, ar_ref, ai_ref, x_ref, y_ref):
        for ri in (0, 1):
            bu = _nn(src_ref[...], w_ref[ri])
            for b in range(nb):
                x_ref[ri, b] = bu[:, b * LANE:(b + 1) * LANE]
        bu_ref = x_ref
        a_r = [ar_ref[b] for b in range(nb)]
        a_i = [ai_ref[b] for b in range(nb)]
        zero = jnp.zeros((8, LANE), f32)
        one = jnp.ones((8, LANE), f32)

        def rows_at(r):
            rr = (rows - 1 - r) if reverse else r
            return pl.ds(pl.multiple_of(rr * 8, 8), 8)

        def pass1(r, carry):
            out = []
            sl = rows_at(r)
            for b in range(nb):
                xr, xi, mr, mi = carry[b]
                nr = a_r[b] * xr - a_i[b] * xi + bu_ref[0, b, sl, :]
                ni = a_r[b] * xi + a_i[b] * xr + bu_ref[1, b, sl, :]
                x_ref[0, b, sl, :] = nr
                x_ref[1, b, sl, :] = ni
                out.append((nr, ni, a_r[b] * mr - a_i[b] * mi, a_r[b] * mi + a_i[b] * mr))
            return tuple(out)

        carry = lax.fori_loop(0, rows, pass1, tuple((zero, zero, one, zero) for _ in range(nb)))
        sub = lax.broadcasted_iota(jnp.int32, (8, LANE), 0)
        feed = []
        for b in range(nb):
            lr_, li_, pr, pi = carry[b]
            fr, fi = zero, zero
            for _ in range(7):
                tr = lr_ + pr * fr - pi * fi
                ti = li_ + pr * fi + pi * fr
                if reverse:
                    fr = jnp.where(sub < 7, pltpu.roll(tr, 7, 0), 0.0)
                    fi = jnp.where(sub < 7, pltpu.roll(ti, 7, 0), 0.0)
                else:
                    fr = jnp.where(sub > 0, pltpu.roll(tr, 1, 0), 0.0)
                    fi = jnp.where(sub > 0, pltpu.roll(ti, 1, 0), 0.0)
            feed.append((fr, fi))

        def pass2(r, carry):
            out = []
            sl = rows_at(r)
            for b in range(nb):
                mr, mi = carry[b]
                fr, fi = feed[b]
                x_ref[0, b, sl, :] += mr * fr - mi * fi
                x_ref[1, b, sl, :] += mr * fi + mi * fr
                out.append((a_r[b] * mr - a_i[b] * mi, a_r[b] * mi + a_i[b] * mr))
            return tuple(out)

        lax.fori_loop(0, rows, pass2, tuple((a_r[b], a_i[b]) for b in range(nb)))

        y = None
        for ri in (0, 1):
            for b in range(nb):
                t = _nn(x_ref[ri, b], wy_ref[ri, b * LANE:(b + 1) * LANE, :])
                y = t if y is None else y + t
        if add is not None:
            y = y + add_ref[...]
        y_ref[...] = y.astype(y_ref.dtype)

    def grads_body(src_ref, xs_ref, u_ref, g_ref, da_ref, dc_ref, db_ref):
        t = lax.broadcasted_iota(jnp.int32, (n_tok, LANE), 0)
        sub = lax.broadcasted_iota(jnp.int32, (8, LANE), 0)

        def prev(v):
            return (jnp.where(t >= 8, pltpu.roll(v, 8, 0), 0.0),
                    jnp.where(sub > 0, pltpu.roll(v[n_tok - 8:, :], 1, 0), 0.0))

        for b in range(nb):
            (xr, hr), (xi, hi) = prev(xs_ref[0, b]), prev(xs_ref[1, b])
            gr, gi = g_ref[0, b], g_ref[1, b]
            gr0, gi0 = gr[0:8, :], gi[0:8, :]
            da_ref[b, 0:1, :] = (jnp.sum(xr * gr + xi * gi, axis=0, keepdims=True)
                                 + jnp.sum(hr * gr0 + hi * gi0, axis=0, keepdims=True))
            da_ref[b, 1:2, :] = (jnp.sum(xr * gi - xi * gr, axis=0, keepdims=True)
                                 + jnp.sum(hr * gi0 - hi * gr0, axis=0, keepdims=True))
            for ri in (0, 1):
                dc_ref[ri, b * LANE:(b + 1) * LANE, :] = _tn(xs_ref[ri, b], src_ref[...])
                db_ref[ri, :, b * LANE:(b + 1) * LANE] = _tn(u_ref[...], g_ref[ri, b])

    n_in = 3 + (add is not None) + 2 * (grads_of is not None)

    def body(*refs):
        ins, rest = list(refs[:n_in]), refs[n_in:]
        src_ref, w_ref, wy_ref = ins[:3]
        add_ref = ins[3] if add is not None else None
        ar_ref, ai_ref = rest[:2]
        if grads_of is None:
            x_ref, y_ref = rest[2:]
            scan_body(src_ref, w_ref, wy_ref, add_ref, ar_ref, ai_ref, x_ref, y_ref)
        else:
            xs_ref, u_ref = ins[-2:]
            y_ref, da_ref, dc_ref, db_ref, x_ref = rest[2:]
            scan_body(src_ref, w_ref, wy_ref, add_ref, ar_ref, ai_ref, x_ref, y_ref)
            grads_body(src_ref, xs_ref, u_ref, x_ref, da_ref, dc_ref, db_ref)

    blk = pl.BlockSpec((2, nb, n_tok, LANE), lambda g: (0, g, 0, 0))
    ablk = pl.BlockSpec((nb, 8, LANE), lambda g: (g, 0, 0))
    col = pl.BlockSpec((n_tok, LANE), lambda g: (0, g))
    in_specs = [col, pl.BlockSpec((2, None, LANE, 512), lambda g: (0, g, 0, 0)), pl.BlockSpec((2, None, 512, LANE), lambda g: (0, g, 0, 0))]
    args = [src, wq, wy]
    if add is not None:
        in_specs.append(col)
        args.append(add)
    y_shape = jax.ShapeDtypeStruct((n_tok, 512), y_dtype)
    x_shape = (2, S5_BLOCKS, n_tok, LANE)
    params = pltpu.CompilerParams(dimension_semantics=("arbitrary",), vmem_limit_bytes=VMEM_LIMIT)
    if grads_of is None:
        return pl.pallas_call(
            body, grid=(S5_BLOCKS // nb,), in_specs=in_specs + [ablk, ablk], out_specs=[blk, col],
            out_shape=[jax.ShapeDtypeStruct(x_shape, f32), y_shape], name=name, compiler_params=params,
        )(*args, a_re8, a_im8)
    return pl.pallas_call(
        body, grid=(S5_BLOCKS // nb,), in_specs=in_specs + [blk, col, ablk, ablk],
        out_specs=[col, pl.BlockSpec((nb, 2, LANE), lambda g: (g, 0, 0)), pl.BlockSpec((2, None, 512, LANE), lambda g: (0, g, 0, 0)),
                   pl.BlockSpec((2, None, LANE, 512), lambda g: (0, g, 0, 0))],
        out_shape=[y_shape, jax.ShapeDtypeStruct((S5_BLOCKS, 2, LANE), f32), jax.ShapeDtypeStruct((2, S5_Q, 512, LANE), f32),
                   jax.ShapeDtypeStruct((2, S5_Q, LANE, 512), f32)],
        scratch_shapes=[pltpu.VMEM((2, nb, n_tok, LANE), f32)], name=name, compiler_params=params,
    )(*args, *grads_of, a_re8, a_im8)


S5_Q = 4


def _bd8(t):
    _, a, b = t.shape
    t = t.reshape(S5_Q, 8, a, 1, b)
    eye = jnp.eye(8, dtype=jnp.bool_).reshape(1, 8, 1, 8, 1)
    return jnp.where(eye, jnp.broadcast_to(t, (S5_Q, 8, a, 8, b)), jnp.zeros((), t.dtype)).reshape(S5_Q, 8 * a, 8 * b)


def _bd8_diag(m, a, b):
    m = m.reshape(S5_Q, 8, a, 8, b)
    eye = jnp.eye(8, dtype=jnp.bool_).reshape(1, 8, 1, 8, 1)
    return jnp.sum(jnp.where(eye, m, 0.0), axis=3).reshape(S5_GROUPS, a, b)


N_CHIPS = 4
_BIG_SHARD = {"w_in": (1, 1024, 1770, 1792), "mla_w_q_up": (1, 256, 192, 256), "mla_w_kv_up": (1, 128, 256, 256),
              "s5_w_glu": (0, 128, 512, 512), "w_branch_out": (0, 384, 1024, 1024), "w_out": (0, 256, 1024, 1024)}


def _to_shards(name, m):
    axis, r, c, cp = _BIG_SHARD[name]
    if axis == 0:
        return m.reshape(N_CHIPS, r, c)
    return jnp.stack([jnp.pad(m[:, j * c:(j + 1) * c], ((0, 0), (0, cp - c))) for j in range(N_CHIPS)])


def _from_shards(name, s):
    axis, r, c, cp = _BIG_SHARD[name]
    if axis == 0:
        return s.reshape(N_CHIPS * r, c)
    return jnp.concatenate([s[j, :, :c] for j in range(N_CHIPS)], axis=1)


def _pad_w_in(w):
    pieces, pos = [], 0
    for name in _PAD_ORDER:
        start, width, inner = _PAD[name]
        o0, ow = _ORIG[name]
        if start + inner > pos:
            pieces.append(jnp.zeros((w.shape[0], start + inner - pos), w.dtype))
        pieces.append(w[:, o0:o0 + ow])
        pos = start + inner + ow
    pieces.append(jnp.zeros((w.shape[0], NP - pos), w.dtype))
    return jnp.concatenate(pieces, axis=1)


def _prep_weights(small, big):
    per_layer = jax.vmap
    w = {}
    w["w_in_shards"] = big["w_in"]
    w["w_in"] = per_layer(lambda s: _pad_w_in(_from_shards("w_in", s)))(big["w_in"])

    def q_up(s):
        wq = _from_shards("mla_w_q_up", s).reshape(MLA_Q_RANK, HEADS, MLA_QK)
        return jnp.pad(wq, ((0, 0), (0, 0), (0, LANE - MLA_QK))).reshape(MLA_Q_RANK, HEADS * LANE)

    def kv_up(s):
        wkv = _from_shards("mla_w_kv_up", s).reshape(MLA_KV_RANK, HEADS, 128)
        wk = jnp.pad(wkv[:, :, :64], ((0, 0), (0, 0), (0, 64))).reshape(MLA_KV_RANK, HEADS * LANE)
        return jnp.concatenate([wk, wkv[:, :, 64:].reshape(MLA_KV_RANK, 512)], axis=1)

    w["wq"] = per_layer(q_up)(big["mla_w_q_up"])
    w["wkv"] = per_layer(kv_up)(big["mla_w_kv_up"])
    for name, key in (("w_glu", "s5_w_glu"), ("wo", "w_branch_out"), ("w_out", "w_out")):
        w[name] = per_layer(functools.partial(_from_shards, key))(big[key])
    row = lambda a: a.astype(f32)[:, None, :]
    lanes = lambda a, n: jnp.pad(row(a), ((0, 0), (0, 0), (0, LANE - n)))
    w["norm_g"] = row(small["norm_g"])
    w["qa_g"] = row(small["mla_q_a_norm"])
    w["kva_g"] = row(small["mla_kv_a_norm"])
    w["qn_g"] = lanes(small["mla_q_norm"], MLA_QK)
    w["kn_g"] = lanes(small["mla_k_norm"], MLA_QK)
    w["fq_g"] = jnp.tile(row(small["fox_q_norm"]), (1, 1, 2))
    w["fk_g"] = jnp.tile(row(small["fox_k_norm"]), (1, 1, 2))
    w["b_f"] = lanes(small["fox_b_f"], HEADS)
    w["lr"] = small["s5_lambda_re"].reshape(DEPTH, S5_LANES, 1)
    w["li"] = small["s5_lambda_im"].reshape(DEPTH, S5_LANES, 1)
    w["ldt"] = jnp.repeat(small["s5_log_dt"], S5_STATE, axis=1).reshape(DEPTH, S5_LANES, 1)
    w["b_re"] = small["s5_b_re"].reshape(DEPTH, S5_LANES, S5_GROUP)
    w["b_im"] = small["s5_b_im"].reshape(DEPTH, S5_LANES, S5_GROUP)
    w["s5_d"] = row(small["s5_d"])
    w["b_glu"] = row(small["s5_b_glu"])
    a_re, a_im, bb_re, bb_im = _s5_param_fwd(w["lr"], w["li"], w["ldt"], w["b_re"], w["b_im"])
    per_group = lambda m: m.reshape(S5_GROUPS, S5_STATE, S5_GROUP)
    pair = lambda f: per_layer(lambda re, im: jnp.stack([f(re), f(im)]).astype(bf16))
    c_re, c_im = small["s5_c_re"], -small["s5_c_im"]
    w["b_cn"] = pair(lambda m: _bd8(jnp.swapaxes(per_group(m), 1, 2)))(bb_re, bb_im)
    w["b_nc"] = pair(lambda m: _bd8(per_group(m)))(bb_re, bb_im)
    w["c_nc"] = pair(lambda m: _bd8(jnp.swapaxes(m, 1, 2)))(c_re, c_im)
    w["c_cn"] = pair(_bd8)(c_re, c_im)
    sublanes = lambda a: jnp.broadcast_to(a.reshape(DEPTH, S5_BLOCKS, 1, LANE), (DEPTH, S5_BLOCKS, 8, LANE))
    w["a_re8"], w["a_im8"], w["a_im8_neg"] = sublanes(a_re), sublanes(a_im), sublanes(-a_im)
    return w


def _fox_halves(x, lane):
    sq = x * x
    lo = jnp.sum(jnp.where(lane < 64, sq, 0.0), axis=-1, keepdims=True)
    hi = jnp.sum(sq, axis=-1, keepdims=True) - lo
    return jnp.where(lane < 64, lax.rsqrt(lo * (1.0 / 64) + EPS), lax.rsqrt(hi * (1.0 / 64) + EPS))


def _fox_halves_bwd(dy, x, r, g, lane):
    xh = x * r
    dxh = dy * g
    pr = dxh * xh
    lo = jnp.sum(jnp.where(lane < 64, pr, 0.0), axis=-1, keepdims=True)
    hi = jnp.sum(pr, axis=-1, keepdims=True) - lo
    mean = jnp.where(lane < 64, lo, hi) * (1.0 / 64)
    return r * (dxh - xh * mean), jnp.sum(dy * xh, axis=0, keepdims=True)


def _mla_recompute(cq, ckv, kpe, c, s1, s2, qa_g, kva_g, wq, wkv):
    cqn, r_cq = _rms(cq, qa_g, MLA_Q_RANK)
    ckvn, r_ckv = _rms(ckv, kva_g, MLA_KV_RANK)
    cqn_b = cqn.astype(bf16)
    ckvn_b = ckvn.astype(bf16)
    q_raw = _nn(cqn_b, wq)
    kv_raw = _nn(ckvn_b, wkv)
    kpe_rot = _rope(kpe, c, s1, s2)
    return cqn_b, r_cq, ckvn_b, r_ckv, q_raw, kv_raw, kpe_rot


def _layer_fwd(x, w, rope_tabs, n_tok):
    c_tab, s1_tab, s2_tab = rope_tabs
    saved = {"x": x}

    def norm_body(x_ref, g_ref, h_ref):
        h_ref[...] = _rms(x_ref[...], g_ref[...], D_MODEL)[0].astype(bf16)

    (h,) = _rowwise("norm_fwd", norm_body, n_tok, [(x, D_MODEL, 0)], [w["norm_g"]], [(D_MODEL, bf16)], [])
    proj = _mm_nn("in_proj", h, w["w_in"], m=n_tok, n=NP, k=D_MODEL, tm=n_tok, tn=512, tk=D_MODEL)
    saved["h"], saved["proj"] = h, proj

    def mla_prep_body(cq_ref, ckv_ref, kpe_ref, c_ref, s1_ref, s2_ref, qa_ref, kva_ref, wq_ref, wkv_ref, qn_g_ref, kn_g_ref,
                      qn_ref, kn_ref, v_ref):
        c, s1, s2 = c_ref[...], s1_ref[...], s2_ref[...]
        _, _, _, _, q_raw, kv_raw, kpe_rot = _mla_recompute(cq_ref[...], ckv_ref[...], kpe_ref[...], c, s1, s2,
                                                            qa_ref[...], kva_ref[...], wq_ref[...], wkv_ref[...])
        for hd in range(HEADS):
            sl = slice(hd * LANE, (hd + 1) * LANE)
            qn_ref[:, sl] = _rms(_rope(q_raw[:, sl], c, s1, s2), qn_g_ref[...], MLA_QK)[0].astype(bf16)
            kn_ref[:, sl] = _rms(kv_raw[:, sl] + kpe_rot, kn_g_ref[...], MLA_QK)[0].astype(bf16)
        v_ref[...] = kv_raw[:, HEADS * LANE:].astype(bf16)

    qn, kn, v_mla = _rowwise(
        "mla_prep", mla_prep_body, n_tok,
        [(proj, *_seg("cq")), (proj, *_seg("ckv")), (proj, *_seg("kpe")), (c_tab, LANE, 0), (s1_tab, LANE, 0), (s2_tab, LANE, 0)],
        [w["qa_g"], w["kva_g"], w["wq"], w["wkv"], w["qn_g"], w["kn_g"]],
        [(HEADS * LANE, bf16), (HEADS * LANE, bf16), (512, bf16)], [])
    y_mla, lse_mla = _attn_fwd("mla_attn_fwd", qn, kn, v_mla, None, mla=True, n_tok=n_tok)
    saved.update(qn=qn, kn=kn, v_mla=v_mla, y_mla=y_mla, lse_mla=lse_mla)

    def fox_prep_body(fq_ref, fk_ref, fv_ref, ff_ref, qg_ref, kg_ref, bf_ref, fqn_ref, fkn_ref, fvb_ref, logf_ref):
        lane = lax.broadcasted_iota(jnp.int32, (TOK, LANE), 1)
        for blk in range(4):
            sl = slice(blk * LANE, (blk + 1) * LANE)
            xq = fq_ref[:, sl]
            fqn_ref[:, sl] = (xq * _fox_halves(xq, lane) * qg_ref[...]).astype(bf16)
            xk = fk_ref[:, sl]
            fkn_ref[:, sl] = (xk * _fox_halves(xk, lane) * kg_ref[...]).astype(bf16)
        fvb_ref[...] = fv_ref[...].astype(bf16)
        z = ff_ref[...] + bf_ref[...]
        logf_ref[...] = jnp.minimum(z, 0.0) - jnp.log(1.0 + jnp.exp(-jnp.abs(z)))

    fqn, fkn, fvb, logf = _rowwise(
        "fox_prep", fox_prep_body, n_tok,
        [(proj, *_seg("fq")), (proj, *_seg("fk")), (proj, *_seg("fv")), (proj, *_seg("ff"))],
        [w["fq_g"], w["fk_g"], w["b_f"]],
        [(512, bf16), (512, bf16), (512, bf16), (LANE, f32)], [])

    def cum_body(x_ref, cum_ref):
        x = x_ref[...]
        t = lax.broadcasted_iota(jnp.int32, x.shape, 0)
        s = 1
        while s < n_tok:
            x = x + jnp.where(t >= s, pltpu.roll(x, s, 0), 0.0)
            s *= 2
        for hd in range(HEADS):
            cum_ref[hd] = jnp.broadcast_to(x[:, hd:hd + 1], (n_tok, LANE))

    cum_b = pl.pallas_call(cum_body, out_shape=jax.ShapeDtypeStruct((HEADS, n_tok, LANE), f32), name="fox_cum")(logf)
    y_fox, lse_fox = _attn_fwd("fox_attn_fwd", fqn, fkn, fvb, cum_b, mla=False, n_tok=n_tok)
    saved.update(fqn=fqn, fkn=fkn, fvb=fvb, cum_b=cum_b, y_fox=y_fox, lse_fox=lse_fox)

    u_w, u_blk = _seg("s5u")
    u_streams = _to_streams(proj[:, u_blk * u_w:(u_blk + 1) * u_w])
    xs, ylin = _s5_scan("s5_scan_fwd", u_streams, w["b_cn"], w["c_nc"], None, f32, w["a_re8"], w["a_im8"], reverse=False, n_tok=n_tok)
    ylin = _from_streams(ylin)

    def s5_post_body(yl_ref, u_ref, d_ref, wg_ref, bg_ref, out_ref):
        y = yl_ref[...] + d_ref[...] * u_ref[...]
        z, _ = _gelu(y)
        out_ref[...] = z * _sigmoid(_nn(z, wg_ref[...]) + bg_ref[...])

    (y_s5,) = _rowwise("s5_post", s5_post_body, n_tok, [(ylin, 512, 0), (proj, u_w, u_blk)],
                       [w["s5_d"], w["w_glu"], w["b_glu"]], [(512, f32)], [])
    saved.update(xs=xs, ylin=ylin, y_s5=y_s5, u_streams=u_streams)

    def merge_body(ym_ref, yf_ref, ys_ref, gm_ref, gf_ref, gs_ref, mm_ref, mf_ref, ms_ref, x_ref, wo_ref, wout_ref, out_ref):
        merged = jnp.zeros((TOK, D_MODEL), f32)
        for b, (y_ref, g_ref, m_ref) in enumerate(((ym_ref, gm_ref, mm_ref), (yf_ref, gf_ref, mf_ref), (ys_ref, gs_ref, ms_ref))):
            g = g_ref[...]
            a = y_ref[...] * (g * _sigmoid(g))
            merged = merged + _sigmoid(m_ref[...]) * _nn(a, wo_ref[b * 512:(b + 1) * 512, :])
        out_ref[...] = x_ref[...] + _nn(merged, wout_ref[...])

    (out,) = _rowwise(
        "merge_fwd", merge_body, n_tok,
        [(y_mla, 512, 0), (y_fox, 512, 0), (y_s5, 512, 0), (proj, *_seg("g_mla")), (proj, *_seg("g_fox")), (proj, *_seg("g_s5")),
         (proj, *_seg("m_mla")), (proj, *_seg("m_fox")), (proj, *_seg("m_s5")), (x, D_MODEL, 0)],
        [w["wo"], w["w_out"]], [(D_MODEL, f32)], [])
    return out, saved


def _layer_bwd(dout, w, sv, rope_tabs, n_tok):
    c_tab, s1_tab, s2_tab = rope_tabs
    proj, x = sv["proj"], sv["x"]
    grads = {}

    def merge_bwd_body(ym_ref, yf_ref, ys_ref, gm_ref, gf_ref, gs_ref, mm_ref, mf_ref, ms_ref, do_ref, wo_ref, wout_ref,
                       dym_ref, dyf_ref, dys_ref, dgm_ref, dgf_ref, dgs_ref, dmm_ref, dmf_ref, dms_ref, dwo_ref, dwout_ref):
        do = do_ref[...]
        branches = ((ym_ref, gm_ref, mm_ref, dym_ref, dgm_ref, dmm_ref), (yf_ref, gf_ref, mf_ref, dyf_ref, dgf_ref, dmf_ref),
                    (ys_ref, gs_ref, ms_ref, dys_ref, dgs_ref, dms_ref))
        acts, outs, sigs = [], [], []
        merged = jnp.zeros((TOK, D_MODEL), f32)
        for b, (y_ref, g_ref, m_ref, _, _, _) in enumerate(branches):
            g = g_ref[...]
            a = (y_ref[...] * (g * _sigmoid(g))).astype(bf16)
            o = _nn(a, wo_ref[b * 512:(b + 1) * 512, :])
            s = _sigmoid(m_ref[...])
            merged = merged + s * o
            acts.append(a)
            outs.append(o)
            sigs.append(s)
        dmerged = _nt(do, wout_ref[...])
        _accumulate(dwout_ref, _tn(merged, do))
        dwo = []
        for b, (y_ref, g_ref, m_ref, dy_ref, dg_ref, dm_ref) in enumerate(branches):
            s, o = sigs[b], outs[b]
            dm_ref[...] = (dmerged * o * s * (1.0 - s)).astype(bf16)
            d_o = dmerged * s
            da = _nt(d_o, wo_ref[b * 512:(b + 1) * 512, :])
            dwo.append(_tn(acts[b], d_o))
            g = g_ref[...]
            sg = _sigmoid(g)
            dy_ref[...] = da * (g * sg)
            dg_ref[...] = (da * y_ref[...] * (sg * (1.0 + g * (1.0 - sg)))).astype(bf16)
        _accumulate(dwo_ref, jnp.concatenate(dwo, axis=0))

    (dy_mla, dy_fox, dy_s5, dg_mla, dg_fox, dg_s5, dm_mla, dm_fox, dm_s5, dwo, dwout) = _rowwise(
        "merge_bwd", merge_bwd_body, n_tok,
        [(sv["y_mla"], 512, 0), (sv["y_fox"], 512, 0), (sv["y_s5"], 512, 0), (proj, *_seg("g_mla")), (proj, *_seg("g_fox")),
         (proj, *_seg("g_s5")), (proj, *_seg("m_mla")), (proj, *_seg("m_fox")), (proj, *_seg("m_s5")), (dout, D_MODEL, 0)],
        [w["wo"], w["w_out"]],
        [(512, f32)] * 3 + [(512, bf16)] * 3 + [(D_MODEL, bf16)] * 3, [((1536, D_MODEL), f32), ((D_MODEL, D_MODEL), f32)])
    grads["w_branch_out"], grads["w_out"] = dwo, dwout

    u_w, u_blk = _seg("s5u")

    def s5_post_bwd_body(yl_ref, u_ref, do_ref, d_ref, wg_ref, bg_ref, dyl_ref, dus_ref, dd_ref, dwg_ref, dbg_ref):
        u = u_ref[...]
        y = yl_ref[...] + d_ref[...] * u
        z, t = _gelu(y)
        s = _sigmoid(_nn(z, wg_ref[...]) + bg_ref[...])
        do = do_ref[...]
        dgl = do * z * s * (1.0 - s)
        dz = do * s + _nt(dgl, wg_ref[...])
        dy = dz * _gelu_grad(y, t)
        dyl_ref[...] = dy.astype(bf16)
        dus_ref[...] = dy * d_ref[...]
        _accumulate(dd_ref, jnp.sum(dy * u, axis=0, keepdims=True))
        _accumulate(dwg_ref, _tn(z, dgl))
        _accumulate(dbg_ref, jnp.sum(dgl, axis=0, keepdims=True))

    dylin, du_skip, dd, dwglu, dbglu = _rowwise(
        "s5_post_bwd", s5_post_bwd_body, n_tok, [(sv["ylin"], 512, 0), (proj, u_w, u_blk), (dy_s5, 512, 0)],
        [w["s5_d"], w["w_glu"], w["b_glu"]], [(512, bf16), (512, f32)], [((1, 512), f32), ((512, 512), f32), ((1, 512), f32)])
    grads["s5_d"], grads["s5_w_glu"], grads["s5_b_glu"] = dd.reshape(512), dwglu, dbglu.reshape(512)

    dylin = _to_streams(dylin)
    ds5u, da, dc_nc, db_cn = _s5_scan("s5_scan_bwd", dylin, w["c_cn"], w["b_nc"], _to_streams(du_skip), bf16, w["a_re8"],
                                      w["a_im8_neg"], reverse=True, n_tok=n_tok, grads_of=(sv["xs"], sv["u_streams"]))
    ds5u = _from_streams(ds5u)
    diag_b = lambda m: jnp.swapaxes(_bd8_diag(m, S5_GROUP, S5_STATE), 1, 2).reshape(S5_LANES, S5_GROUP)
    diag_c = lambda m: jnp.swapaxes(_bd8_diag(m, S5_STATE, S5_GROUP), 1, 2)
    dlr, dli, dldt, db_re, db_im = _s5_param_bwd(
        w["lr"], w["li"], w["ldt"], w["b_re"], w["b_im"], da[:, 0, :].reshape(S5_LANES, 1), da[:, 1, :].reshape(S5_LANES, 1),
        diag_b(db_cn[0]), diag_b(db_cn[1]))
    grads["s5_lambda_re"] = dlr.reshape(S5_GROUPS, S5_STATE)
    grads["s5_lambda_im"] = dli.reshape(S5_GROUPS, S5_STATE)
    grads["s5_log_dt"] = dldt.reshape(S5_GROUPS)
    grads["s5_b_re"] = db_re.reshape(S5_GROUPS, S5_STATE, S5_GROUP)
    grads["s5_b_im"] = db_im.reshape(S5_GROUPS, S5_STATE, S5_GROUP)
    grads["s5_c_re"] = diag_c(dc_nc[0])
    grads["s5_c_im"] = -diag_c(dc_nc[1])

    dfqn, dfkn, dfv, dck, dcq = _attn_bwd("fox_attn_bwd", sv["fqn"], sv["fkn"], sv["fvb"], sv["y_fox"], sv["lse_fox"], dy_fox,
                                          sv["cum_b"], mla=False, n_tok=n_tok)
    dcq = jnp.pad(dcq[:, :2, :].reshape(HEADS, n_tok).T, ((0, 0), (0, LANE - HEADS)))

    def fox_gate_bwd_body(dk_ref, dq_ref, ff_ref, bf_ref, dff_ref, dbf_ref):
        xg = dk_ref[...] + dq_ref[...]
        t = lax.broadcasted_iota(jnp.int32, xg.shape, 0)
        s = 1
        while s < n_tok:
            xg = xg + jnp.where(t < n_tok - s, pltpu.roll(xg, n_tok - s, 0), 0.0)
            s *= 2
        dff = xg * _sigmoid(-(ff_ref[...] + bf_ref[...]))
        dff_ref[...] = dff.astype(bf16)
        dbf_ref[...] = jnp.sum(dff, axis=0, keepdims=True)

    ff_w, ff_blk = _seg("ff")
    dff, dbf = pl.pallas_call(
        fox_gate_bwd_body, grid=(1,),
        in_specs=[pl.BlockSpec((n_tok, LANE), lambda i: (0, 0)), pl.BlockSpec((n_tok, LANE), lambda i: (0, 0)),
                  pl.BlockSpec((n_tok, ff_w), lambda i: (0, ff_blk)), pl.BlockSpec((1, LANE), lambda i: (0, 0))],
        out_specs=[pl.BlockSpec((n_tok, LANE), lambda i: (0, 0)), pl.BlockSpec((1, LANE), lambda i: (0, 0))],
        out_shape=[jax.ShapeDtypeStruct((n_tok, LANE), bf16), jax.ShapeDtypeStruct((1, LANE), f32)], name="fox_gate_bwd",
    )(dck, dcq, proj, w["b_f"])
    grads["fox_b_f"] = dbf[0, :HEADS]

    def fox_prep_bwd_body(fq_ref, fk_ref, dqn_ref, dkn_ref, dv_ref, qg_ref, kg_ref, dfq_ref, dfk_ref, dfv_ref, dqg_ref, dkg_ref):
        lane = lax.broadcasted_iota(jnp.int32, (TOK, LANE), 1)
        dqg = jnp.zeros((1, LANE), f32)
        dkg = jnp.zeros((1, LANE), f32)
        for blk in range(4):
            sl = slice(blk * LANE, (blk + 1) * LANE)
            xq = fq_ref[:, sl]
            dx, dg = _fox_halves_bwd(dqn_ref[:, sl], xq, _fox_halves(xq, lane), qg_ref[...], lane)
            dfq_ref[:, sl] = dx.astype(bf16)
            dqg = dqg + dg
            xk = fk_ref[:, sl]
            dx, dg = _fox_halves_bwd(dkn_ref[:, sl], xk, _fox_halves(xk, lane), kg_ref[...], lane)
            dfk_ref[:, sl] = dx.astype(bf16)
            dkg = dkg + dg
        dfv_ref[...] = dv_ref[...].astype(bf16)
        _accumulate(dqg_ref, dqg + pltpu.roll(dqg, 64, 1))
        _accumulate(dkg_ref, dkg + pltpu.roll(dkg, 64, 1))

    dfq, dfk, dfvb, dfqg, dfkg = _rowwise(
        "fox_prep_bwd", fox_prep_bwd_body, n_tok,
        [(proj, *_seg("fq")), (proj, *_seg("fk")), (dfqn, 512, 0), (dfkn, 512, 0), (dfv, 512, 0)],
        [w["fq_g"], w["fk_g"]], [(512, bf16)] * 3, [((1, LANE), f32)] * 2)
    grads["fox_q_norm"], grads["fox_k_norm"] = dfqg[0, :FOX_DIM], dfkg[0, :FOX_DIM]

    dqn, dkn, dv_mla = _attn_bwd("mla_attn_bwd", sv["qn"], sv["kn"], sv["v_mla"], sv["y_mla"], sv["lse_mla"], dy_mla,
                                 None, mla=True, n_tok=n_tok)

    def mla_prep_bwd_body(cq_ref, ckv_ref, kpe_ref, c_ref, s1_ref, s2_ref, dqn_ref, dkn_ref, dv_ref,
                          qa_ref, kva_ref, wq_ref, wkv_ref, qn_g_ref, kn_g_ref,
                          dcq_ref, dckv_ref, dkpe_ref, dwq_ref, dwkv_ref, dqa_ref, dkva_ref, dqng_ref, dkng_ref):
        c, s1, s2 = c_ref[...], s1_ref[...], s2_ref[...]
        cq, ckv = cq_ref[...], ckv_ref[...]
        cqn_b, r_cq, ckvn_b, r_ckv, q_raw, kv_raw, kpe_rot = _mla_recompute(
            cq, ckv, kpe_ref[...], c, s1, s2, qa_ref[...], kva_ref[...], wq_ref[...], wkv_ref[...])
        lane = lax.broadcasted_iota(jnp.int32, (TOK, LANE), 1)
        dq_raw, dk_raw = [], []
        dkpe_rot = jnp.zeros((TOK, LANE), f32)
        dqng = jnp.zeros((1, LANE), f32)
        dkng = jnp.zeros((1, LANE), f32)
        for hd in range(HEADS):
            sl = slice(hd * LANE, (hd + 1) * LANE)
            q_rot = _rope(q_raw[:, sl], c, s1, s2)
            r = lax.rsqrt(jnp.sum(q_rot * q_rot, axis=-1, keepdims=True) * (1.0 / MLA_QK) + EPS)
            dx, dg = _rms_bwd(dqn_ref[:, sl], q_rot, r, qn_g_ref[...], MLA_QK)
            dqng = dqng + dg
            dq_raw.append(_rope_t(dx, c, s1, s2))
            k_full = kv_raw[:, sl] + kpe_rot
            r = lax.rsqrt(jnp.sum(k_full * k_full, axis=-1, keepdims=True) * (1.0 / MLA_QK) + EPS)
            dx, dg = _rms_bwd(dkn_ref[:, sl], k_full, r, kn_g_ref[...], MLA_QK)
            dkng = dkng + dg
            dk_raw.append(jnp.where(lane < 64, dx, 0.0))
            dkpe_rot = dkpe_rot + dx
        dkpe = _rope_t(dkpe_rot, c, s1, s2)
        dkpe_ref[...] = jnp.where(jnp.logical_and(lane >= 64, lane < 64 + ROPE), dkpe, 0.0).astype(bf16)
        dq_raw = jnp.concatenate(dq_raw, axis=1).astype(bf16)
        dkv_raw = jnp.concatenate(dk_raw + [dv_ref[...]], axis=1).astype(bf16)
        dcqn = _nt(dq_raw, wq_ref[...])
        dckvn = _nt(dkv_raw, wkv_ref[...])
        dx, dg = _rms_bwd(dcqn, cq, r_cq, qa_ref[...], MLA_Q_RANK)
        dcq_ref[...] = dx.astype(bf16)
        _accumulate(dqa_ref, dg)
        dx, dg = _rms_bwd(dckvn, ckv, r_ckv, kva_ref[...], MLA_KV_RANK)
        dckv_ref[...] = dx.astype(bf16)
        _accumulate(dkva_ref, dg)
        _accumulate(dwq_ref, _tn(cqn_b, dq_raw))
        _accumulate(dwkv_ref, _tn(ckvn_b, dkv_raw))
        _accumulate(dqng_ref, dqng)
        _accumulate(dkng_ref, dkng)

    dcq, dckv, dkpe, dwq, dwkv, dqa, dkva, dqng, dkng = _rowwise(
        "mla_prep_bwd", mla_prep_bwd_body, n_tok,
        [(proj, *_seg("cq")), (proj, *_seg("ckv")), (proj, *_seg("kpe")), (c_tab, LANE, 0), (s1_tab, LANE, 0), (s2_tab, LANE, 0),
         (dqn, HEADS * LANE, 0), (dkn, HEADS * LANE, 0), (dv_mla, 512, 0)],
        [w["qa_g"], w["kva_g"], w["wq"], w["wkv"], w["qn_g"], w["kn_g"]],
        [(MLA_Q_RANK, bf16), (LANE, bf16), (LANE, bf16)],
        [((MLA_Q_RANK, HEADS * LANE), f32), ((MLA_KV_RANK, HEADS * LANE + 512), f32), ((1, MLA_Q_RANK), f32),
         ((1, MLA_KV_RANK), f32), ((1, LANE), f32), ((1, LANE), f32)])
    grads["mla_w_q_up"] = dwq.reshape(MLA_Q_RANK, HEADS, LANE)[:, :, :MLA_QK].reshape(MLA_Q_RANK, HEADS * MLA_QK)
    dwk = dwkv[:, :HEADS * LANE].reshape(MLA_KV_RANK, HEADS, LANE)[:, :, :64]
    dwv = dwkv[:, HEADS * LANE:].reshape(MLA_KV_RANK, HEADS, 64)
    grads["mla_w_kv_up"] = jnp.concatenate([dwk, dwv], axis=2).reshape(MLA_KV_RANK, HEADS * 128)
    grads["mla_q_a_norm"], grads["mla_kv_a_norm"] = dqa.reshape(-1), dkva.reshape(-1)
    grads["mla_q_norm"], grads["mla_k_norm"] = dqng[0, :MLA_QK], dkng[0, :MLA_QK]

    _, _, shard_c, shard_cp = _BIG_SHARD["w_in"]
    kpe0 = _PAD["kpe"][2]
    pieces = [dcq, dckv, dkpe[:, kpe0:kpe0 + ROPE], dfq, dfk, dfvb, dff[:, :HEADS], ds5u, dg_mla, dg_fox, dg_s5,
              dm_mla, dm_fox, dm_s5]
    gap = jnp.zeros((n_tok, shard_cp - shard_c), bf16)
    cut, pos = [], 0
    for p in pieces:
        start = 0
        while start < p.shape[1]:
            take = min(p.shape[1] - start, shard_c - pos % shard_c)
            cut.append(p[:, start:start + take])
            start, pos = start + take, pos + take
            if pos % shard_c == 0:
                cut.append(gap)
    dproj = jnp.concatenate(cut, axis=1)
    ct = 256
    per = shard_cp // ct
    dh = _mm("in_proj_dgrad", dproj, w["w_in_shards"], mode="nt", grid=(1, 1, N_CHIPS * per),
             a_spec=pl.BlockSpec((n_tok, ct), lambda i, j, kk: (0, kk)),
             b_spec=pl.BlockSpec((None, D_MODEL, ct), lambda i, j, kk: (kk // per, 0, kk % per)),
             o_spec=pl.BlockSpec((n_tok, D_MODEL), lambda i, j, kk: (0, 0)),
             out_shape=jax.ShapeDtypeStruct((n_tok, D_MODEL), f32), acc_shape=(n_tok, D_MODEL))
    grads["w_in"] = _mm("in_proj_wgrad", sv["h"], dproj, mode="tn", grid=(1, N_CHIPS * per, 1),
                        a_spec=pl.BlockSpec((n_tok, D_MODEL), lambda i, j, kk: (0, 0)),
                        b_spec=pl.BlockSpec((n_tok, ct), lambda i, j, kk: (0, j)),
                        o_spec=pl.BlockSpec((None, D_MODEL, ct), lambda i, j, kk: (j // per, 0, j % per)),
                        out_shape=jax.ShapeDtypeStruct((N_CHIPS, D_MODEL, shard_cp), f32), acc_shape=(D_MODEL, ct))

    def norm_bwd_body(dh_ref, x_ref, do_ref, g_ref, dx_ref, dg_ref):
        xv = x_ref[...]
        r = lax.rsqrt(jnp.sum(xv * xv, axis=-1, keepdims=True) * (1.0 / D_MODEL) + EPS)
        dx, dg = _rms_bwd(dh_ref[...], xv, r, g_ref[...], D_MODEL)
        dx_ref[...] = do_ref[...] + dx
        _accumulate(dg_ref, dg)

    dx, dng = _rowwise("norm_bwd", norm_bwd_body, n_tok, [(dh, D_MODEL, 0), (x, D_MODEL, 0), (dout, D_MODEL, 0)],
                       [w["norm_g"]], [(D_MODEL, f32)], [((1, D_MODEL), f32)])
    grads["norm_g"] = dng.reshape(D_MODEL)
    return dx, grads


def _rope_tables(positions):
    inv = 1.0 / (ROPE_THETA ** (jnp.arange(0, ROPE, 2, dtype=f32) / ROPE))
    ang = positions.astype(f32).reshape(-1, 1) * inv
    cos, sin = jnp.cos(ang), jnp.sin(ang)
    n = ang.shape[0]
    z16, z32, z64 = jnp.zeros((n, 16), f32), jnp.zeros((n, 32), f32), jnp.zeros((n, 64), f32)
    c = jnp.concatenate([jnp.ones((n, 64), f32), cos, cos, z32], axis=1)
    s1 = jnp.concatenate([z64, -sin, z16, z32], axis=1)
    s2 = jnp.concatenate([z64, z16, sin, z32], axis=1)
    return c, s1, s2


BIG = ("w_in", "mla_w_q_up", "mla_w_kv_up", "s5_w_glu", "w_branch_out", "w_out")
SMALL = ("norm_g", "mla_q_a_norm", "mla_kv_a_norm", "mla_q_norm", "mla_k_norm", "fox_b_f", "fox_q_norm", "fox_k_norm",
         "s5_lambda_re", "s5_lambda_im", "s5_log_dt", "s5_b_re", "s5_b_im", "s5_c_re", "s5_c_im", "s5_d", "s5_b_glu")
WEIGHTS = ("norm_g", "w_in", "mla_q_a_norm", "mla_w_q_up", "mla_kv_a_norm", "mla_w_kv_up", "mla_q_norm", "mla_k_norm",
           "fox_b_f", "fox_q_norm", "fox_k_norm", "s5_lambda_re", "s5_lambda_im", "s5_log_dt", "s5_b_re", "s5_b_im",
           "s5_c_re", "s5_c_im", "s5_d", "s5_w_glu", "s5_b_glu", "w_branch_out", "w_out")


def _local_step(x, positions, loss_target, small, big):
    n_tok = x.shape[0]
    tabs = _rope_tables(positions)
    ws, saves = [], []
    hcur = x
    stacked = _prep_weights(small, big)
    for l in range(DEPTH):
        w = {k: v[l] for k, v in stacked.items()}
        hcur, sv = _layer_fwd(hcur, w, tabs, n_tok)
        ws.append(w)
        saves.append(sv)

    def loss_body(y_ref, t_ref, d_ref, l_ref):
        err = y_ref[...] - t_ref[...]
        d_ref[...] = err * (1.0 / D_MODEL)
        tot = jnp.sum(jnp.sum(err * err, axis=-1, keepdims=True), axis=0, keepdims=True)
        _accumulate(l_ref, jnp.broadcast_to(tot * (0.5 / D_MODEL), (1, LANE)))

    dcur, loss = _rowwise("loss", loss_body, n_tok, [(hcur, D_MODEL, 0), (loss_target, D_MODEL, 0)], [], [(D_MODEL, f32)],
                          [((1, LANE), f32)])
    layer_grads = [None] * DEPTH
    for l in reversed(range(DEPTH)):
        dcur, layer_grads[l] = _layer_bwd(dcur, ws[l], saves[l], tabs, n_tok)
    grads = {n: jnp.stack([layer_grads[l][n] for l in range(DEPTH)]) for n in WEIGHTS}
    return loss[0, 0], dcur, grads


N_DEV = 8
_ANY = pl.BlockSpec(memory_space=pl.ANY)
_MESH = pl.DeviceIdType.MESH


def _all_gather8(name, blk):
    m = blk.shape[0]

    def body(x_ref, out_ref, send_sems, recv_sems, local_sem):
        x, y, c = lax.axis_index("x"), lax.axis_index("y"), lax.axis_index("c")
        me, sibling = (x, y, c), (x, y, 1 - c)
        chips = [(1 - x, y), (x, 1 - y), (1 - x, 1 - y)]

        def slot(px, py, pc):
            return out_ref.at[4 * px + 2 * py + pc]

        def copy(k, block, to, src=None):
            return pltpu.make_async_remote_copy(
                src_ref=slot(*block) if src is None else src, dst_ref=slot(*block),
                send_sem=send_sems.at[k], recv_sem=recv_sems.at[k], device_id=to, device_id_type=_MESH)

        mine = pltpu.make_async_copy(x_ref, slot(*me), local_sem)
        mine.start()
        first = [copy(0, me, sibling, src=x_ref)]
        first += [copy(1 + j, me, (*chip, c), src=x_ref) for j, chip in enumerate(chips)]
        for cp in first:
            cp.start()
        passed = [copy(4 + j, (*chip, c), sibling) for j, chip in enumerate(chips)]
        for j, chip in enumerate(chips):
            copy(1 + j, (*chip, c), me).wait_recv()
            passed[j].start()
        copy(0, sibling, me).wait_recv()
        for j, chip in enumerate(chips):
            copy(4 + j, (*chip, 1 - c), me).wait_recv()
        for cp in first + passed:
            cp.wait_send()
        mine.wait()

    return pl.pallas_call(
        body, out_shape=jax.ShapeDtypeStruct((N_DEV, m, LANE), blk.dtype), in_specs=[_ANY], out_specs=_ANY, name=name,
        scratch_shapes=[pltpu.SemaphoreType.DMA((7,)), pltpu.SemaphoreType.DMA((7,)), pltpu.SemaphoreType.DMA],
    )(blk)


def _gather_layers(name, shards):
    n = len(shards)

    def body(*refs):
        x_refs, out_refs = refs[:n], refs[n:2 * n]
        send_sems, recv_sems, local_sems = refs[2 * n:]
        x, y, c = lax.axis_index("x"), lax.axis_index("y"), lax.axis_index("c")
        me, sibling = (x, y, c), (x, y, 1 - c)
        xn, yn, dg = (1 - x, y, c), (x, 1 - y, c), (1 - x, 1 - y, c)
        relay_from = (x + (1 - c) * (1 - 2 * x), y + c * (1 - 2 * y), c)
        relay_to = (x + c * (1 - 2 * x), y + (1 - c) * (1 - 2 * y), c)

        def copy(w, k, block, to, src=None):
            px, py, pc = block
            slot = out_refs[w].at[pc, 2 * px + py]
            return pltpu.make_async_remote_copy(
                src_ref=slot if src is None else src, dst_ref=slot, send_sem=send_sems.at[7 * w + k],
                recv_sem=recv_sems.at[7 * w + k], device_id=to, device_id_type=_MESH)

        started, local = [], []
        for w in range(n):
            src = x_refs[w].at[c]
            mine = pltpu.make_async_copy(src, out_refs[w].at[c, 2 * x + y], local_sems.at[w])
            mine.start()
            local.append(mine)
            first = [copy(w, 0, me, sibling, src=src), copy(w, 1, me, xn, src=src), copy(w, 2, me, yn, src=src)]
            for cp in first:
                cp.start()
            started += first
        for w in range(n):
            copy(w, 1, xn, me).wait_recv()
            copy(w, 2, yn, me).wait_recv()
            onward = [copy(w, 3, relay_from, relay_to), copy(w, 4, xn, sibling), copy(w, 5, yn, sibling)]
            for cp in onward:
                cp.start()
            started += onward
        for w in range(n):
            copy(w, 3, dg, me).wait_recv()
            onward = copy(w, 6, dg, sibling)
            onward.start()
            started.append(onward)
        for w in range(n):
            copy(w, 0, sibling, me).wait_recv()
            for k, chip in ((4, xn), (5, yn), (6, dg)):
                copy(w, k, (chip[0], chip[1], 1 - c), me).wait_recv()
        for cp in started:
            cp.wait_send()
        for cp in local:
            cp.wait()

    return pl.pallas_call(
        body, out_shape=[jax.ShapeDtypeStruct((2, N_CHIPS) + s.shape[1:], s.dtype) for s in shards],
        in_specs=[_ANY] * n, out_specs=[_ANY] * n, name=name,
        scratch_shapes=[pltpu.SemaphoreType.DMA((7 * n,)), pltpu.SemaphoreType.DMA((7 * n,)), pltpu.SemaphoreType.DMA((n,))],
    )(*shards)


def _swap_layers(name, parts):
    n = len(parts)

    def body(*refs):
        p_refs, got_refs = refs[:n], refs[n:2 * n]
        send_sems, recv_sems = refs[2 * n:]
        x, y, c = lax.axis_index("x"), lax.axis_index("y"), lax.axis_index("c")
        copies = []
        for w in range(n):
            cp = pltpu.make_async_remote_copy(
                src_ref=p_refs[w].at[1 - c], dst_ref=got_refs[w], send_sem=send_sems.at[w], recv_sem=recv_sems.at[w],
                device_id=(x, y, 1 - c), device_id_type=_MESH)
            cp.start()
            copies.append(cp)
        for cp in copies:
            cp.wait()

    return pl.pallas_call(
        body, out_shape=[jax.ShapeDtypeStruct(p.shape[1:], p.dtype) for p in parts], in_specs=[_ANY] * n, out_specs=[_ANY] * n,
        name=name, scratch_shapes=[pltpu.SemaphoreType.DMA((n,)), pltpu.SemaphoreType.DMA((n,))],
    )(*parts)


def _scatter_to_chips(name, parts):
    n = len(parts)

    def body(*refs):
        p_refs, out_refs = refs[:n], refs[n:2 * n]
        send_sems, recv_sems, local_sems = refs[2 * n:]
        x, y, c = lax.axis_index("x"), lax.axis_index("y"), lax.axis_index("c")
        jme = 2 * x + y
        chips = [(1 - x, y), (x, 1 - y), (1 - x, 1 - y)]
        sends, local = [], []
        for w in range(n):
            mine = pltpu.make_async_copy(p_refs[w].at[jme], out_refs[w].at[jme], local_sems.at[w])
            mine.start()
            local.append(mine)
            for k, (tx, ty) in enumerate(chips):
                cp = pltpu.make_async_remote_copy(
                    src_ref=p_refs[w].at[2 * tx + ty], dst_ref=out_refs[w].at[jme], send_sem=send_sems.at[3 * w + k],
                    recv_sem=recv_sems.at[3 * w + k], device_id=(tx, ty, c), device_id_type=_MESH)
                cp.start()
                sends.append(cp)
        for w in range(n):
            for k, (tx, ty) in enumerate(chips):
                pltpu.make_async_remote_copy(
                    src_ref=p_refs[w].at[jme], dst_ref=out_refs[w].at[2 * tx + ty], send_sem=send_sems.at[3 * w + k],
                    recv_sem=recv_sems.at[3 * w + k], device_id=(tx, ty, c), device_id_type=_MESH).wait_recv()
        for cp in sends:
            cp.wait_send()
        for cp in local:
            cp.wait()

    return pl.pallas_call(
        body, out_shape=[jax.ShapeDtypeStruct(p.shape, p.dtype) for p in parts], in_specs=[_ANY] * n, out_specs=[_ANY] * n, name=name,
        scratch_shapes=[pltpu.SemaphoreType.DMA((3 * n,)), pltpu.SemaphoreType.DMA((3 * n,)), pltpu.SemaphoreType.DMA((n,))],
    )(*parts)


def _share_layers(name, bufs):
    n = len(bufs)

    def body(*refs):
        out_refs = refs[n:2 * n]
        send_sems, recv_sems = refs[2 * n:]
        x, y, c = lax.axis_index("x"), lax.axis_index("y"), lax.axis_index("c")
        copies = []
        for w in range(n):
            cp = pltpu.make_async_remote_copy(src_ref=out_refs[w].at[c], dst_ref=out_refs[w].at[c], send_sem=send_sems.at[w],
                                              recv_sem=recv_sems.at[w], device_id=(x, y, 1 - c), device_id_type=_MESH)
            cp.start()
            copies.append(cp)
        for w in range(n):
            pltpu.make_async_remote_copy(src_ref=out_refs[w].at[c], dst_ref=out_refs[w].at[1 - c], send_sem=send_sems.at[w],
                                         recv_sem=recv_sems.at[w], device_id=(x, y, 1 - c), device_id_type=_MESH).wait_recv()
        for cp in copies:
            cp.wait_send()

    return pl.pallas_call(
        body, out_shape=[jax.ShapeDtypeStruct(b.shape, b.dtype) for b in bufs], in_specs=[_ANY] * n, out_specs=[_ANY] * n,
        input_output_aliases={w: w for w in range(n)}, name=name,
        scratch_shapes=[pltpu.SemaphoreType.DMA((n,)), pltpu.SemaphoreType.DMA((n,))],
    )(*bufs)


def _row_tile(rows, cols):
    best = 16
    for t in range(16, rows + 1, 16):
        if rows % t == 0 and t * cols * 4 <= 2 * 1024 * 1024:
            best = t
    return best


def _add_pair(name, core, parts, got, out_dtype):
    _, _, r, c = parts.shape
    t = _row_tile(r, c)

    def body(core_ref, a_ref, b_ref, o_ref):
        o_ref[...] = (a_ref[...] + b_ref[...]).astype(o_ref.dtype)

    spec = pl.BlockSpec((None, t, c), lambda j, i, core_ref: (j, i, 0))
    grid_spec = pltpu.PrefetchScalarGridSpec(
        num_scalar_prefetch=1, grid=(N_CHIPS, r // t),
        in_specs=[pl.BlockSpec((None, None, t, c), lambda j, i, core_ref: (core_ref[0], j, i, 0)), spec], out_specs=spec)
    return pl.pallas_call(body, grid_spec=grid_spec, out_shape=jax.ShapeDtypeStruct(got.shape, out_dtype), name=name,
                          compiler_params=pltpu.CompilerParams(dimension_semantics=("arbitrary", "arbitrary")))(core, parts, got)


def _add_four(name, core, a):
    _, r, c = a.shape
    t = _row_tile(r, c)

    def body(core_ref, a0, a1, a2, a3, o_ref):
        o_ref[...] = ((a0[...].astype(f32) + a1[...].astype(f32)) + a2[...].astype(f32)) + a3[...].astype(f32)

    specs = [pl.BlockSpec((None, t, c), functools.partial(lambda i, core_ref, k: (k, i, 0), k=k)) for k in range(N_CHIPS)]
    grid_spec = pltpu.PrefetchScalarGridSpec(
        num_scalar_prefetch=1, grid=(r // t,), in_specs=specs,
        out_specs=pl.BlockSpec((None, t, c), lambda i, core_ref: (core_ref[0], i, 0)))
    return pl.pallas_call(body, grid_spec=grid_spec, out_shape=jax.ShapeDtypeStruct((2, r, c), f32), name=name,
                          compiler_params=pltpu.CompilerParams(dimension_semantics=("arbitrary",)))(core, a, a, a, a)


def _adamw(name, w, g, m, v, row_tile=None, lead_tile=None):
    c1 = 1.0 - ADAM_B1 ** ADAM_STEP
    c2 = 1.0 - ADAM_B2 ** ADAM_STEP

    def body(w_ref, g_ref, m_ref, v_ref, d_ref, nm_ref, nv_ref):
        gv = g_ref[...]
        nm = ADAM_B1 * m_ref[...] + (1.0 - ADAM_B1) * gv
        nv = ADAM_B2 * v_ref[...] + (1.0 - ADAM_B2) * (gv * gv)
        m_hat = nm / c1
        v_hat = nv / c2
        d_ref[...] = -ADAM_LR * (m_hat / (jnp.sqrt(v_hat) + ADAM_EPS) + ADAM_WD * w_ref[...])
        nm_ref[...] = nm
        nv_ref[...] = nv

    sds = jax.ShapeDtypeStruct(w.shape, f32)
    if lead_tile is not None:
        spec = pl.BlockSpec((lead_tile,) + w.shape[1:], lambda i: (i, 0, 0))
        return pl.pallas_call(body, grid=(w.shape[0] // lead_tile,), in_specs=[spec] * 4, out_specs=[spec] * 3, out_shape=[sds] * 3,
                              name=name, compiler_params=pltpu.CompilerParams(dimension_semantics=("arbitrary",), vmem_limit_bytes=VMEM_LIMIT),
                              )(w, g, m, v)
    if row_tile is None:
        return pl.pallas_call(body, out_shape=[sds] * 3, name=name)(w, g, m, v)
    _, r, c = w.shape
    spec = pl.BlockSpec((None, row_tile, c), lambda l, i: (l, i, 0))
    return pl.pallas_call(body, grid=(DEPTH, r // row_tile), in_specs=[spec] * 4, out_specs=[spec] * 3, out_shape=[sds] * 3, name=name,
                          compiler_params=pltpu.CompilerParams(dimension_semantics=("arbitrary", "arbitrary"), vmem_limit_bytes=VMEM_LIMIT),
                          )(w, g, m, v)


def _pad_rows(flat, rows):
    return jnp.pad(flat, (0, rows * LANE - flat.shape[0])).reshape(rows, LANE)


def kernel(x, positions, norm_g, w_in, mla_q_a_norm, mla_w_q_up, mla_kv_a_norm, mla_w_kv_up, mla_q_norm, mla_k_norm, fox_b_f, fox_q_norm, fox_k_norm, s5_lambda_re, s5_lambda_im, s5_log_dt, s5_b_re, s5_b_im, s5_c_re, s5_c_im, s5_d, s5_w_glu, s5_b_glu, w_branch_out, w_out, loss_target, m_norm_g, m_w_in, m_mla_q_a_norm, m_mla_w_q_up, m_mla_kv_a_norm, m_mla_w_kv_up, m_mla_q_norm, m_mla_k_norm, m_fox_b_f, m_fox_q_norm, m_fox_k_norm, m_s5_lambda_re, m_s5_lambda_im, m_s5_log_dt, m_s5_b_re, m_s5_b_im, m_s5_c_re, m_s5_c_im, m_s5_d, m_s5_w_glu, m_s5_b_glu, m_w_branch_out, m_w_out, v_norm_g, v_w_in, v_mla_q_a_norm, v_mla_w_q_up, v_mla_kv_a_norm, v_mla_w_kv_up, v_mla_q_norm, v_mla_k_norm, v_fox_b_f, v_fox_q_norm, v_fox_k_norm, v_s5_lambda_re, v_s5_lambda_im, v_s5_log_dt, v_s5_b_re, v_s5_b_im, v_s5_c_re, v_s5_c_im, v_s5_d, v_s5_w_glu, v_s5_b_glu, v_w_branch_out, v_w_out):
    given = dict(locals())
    wts = {n: given[n] for n in WEIGHTS}
    mom1 = {n: given["m_" + n] for n in WEIGHTS}
    mom2 = {n: given["v_" + n] for n in WEIGHTS}

    def lanes(n, a):
        _, _, c, cp = _BIG_SHARD[n]
        return jnp.pad(a, ((0, 0), (0, 0), (0, cp - c)))

    gathered = _gather_layers("gather_weights", [lanes(n, wts[n].astype(bf16)) for n in BIG])
    big = dict(zip(BIG, gathered))
    small = {n: wts[n] for n in SMALL}

    loss_local, grad_x, grads = _local_step(x[0], positions, loss_target[0], small, big)
    loss = lax.psum(loss_local, ("x", "y", "c"))

    small_flat = jnp.concatenate([grads[n].reshape(-1) for n in SMALL])
    small_rows = -(-small_flat.shape[0] // (N_DEV * 16 * LANE)) * 16
    parts = [grads[n] if n == "w_in" else jnp.stack([_to_shards(n, grads[n][l]) for l in range(DEPTH)]) for n in BIG]
    parts.append(jnp.swapaxes(_pad_rows(small_flat, N_DEV * small_rows).reshape(N_CHIPS, 2, small_rows, LANE), 0, 1))
    core = lax.axis_index("c")
    core1 = core.reshape(1).astype(jnp.int32)
    got = _swap_layers("grads_to_sibling", parts)
    hop = [bf16] * len(BIG) + [f32]
    pair = [_add_pair("grads_pair_sum_%d" % i, core1, a, b, dt) for i, (a, b, dt) in enumerate(zip(parts, got, hop))]
    landed = _scatter_to_chips("grads_to_chips", pair)
    total = [_add_four("grads_chip_sum_%d" % i, core1, a) for i, a in enumerate(landed)]
    shared = _share_layers("grads_share", total[:-1])
    small_mine = lax.dynamic_index_in_dim(total[-1], core, 0, keepdims=False)
    small_all = _all_gather8("gather_small_grads", small_mine).reshape(-1)

    g_out = {n: s[:, :, :_BIG_SHARD[n][2]] for n, s in zip(BIG, shared)}
    pos = 0
    for n in SMALL:
        g_out[n] = small_all[pos:pos + wts[n].size].reshape(wts[n].shape)
        pos += wts[n].size

    delta, new_m, new_v = {}, {}, {}
    for n in WEIGHTS:
        if n == "w_in":
            cols_first = lambda a: jnp.transpose(a, (2, 0, 1))
            res = _adamw("adamw_" + n, *[cols_first(a) for a in (wts[n], g_out[n], mom1[n], mom2[n])], lead_tile=177)
            delta[n], new_m[n], new_v[n] = [jnp.transpose(a, (1, 2, 0)) for a in res]
            continue
        row_tile = _row_tile(*wts[n].shape[1:]) if n in BIG else None
        delta[n], new_m[n], new_v[n] = _adamw("adamw_" + n, wts[n], g_out[n], mom1[n], mom2[n], row_tile)

    return (loss, grad_x[None], *[g_out[n] for n in WEIGHTS], *[delta[n] for n in WEIGHTS],
            *[new_m[n] for n in WEIGHTS], *[new_v[n] for n in WEIGHTS])
```

```python
import functools
import math

import jax
import jax.numpy as jnp
from jax import lax
from jax.experimental import pallas as pl
from jax.experimental.pallas import tpu as pltpu

f32 = jnp.float32
bf16 = jnp.bfloat16

D_MODEL = 1024
DEPTH = 2
EPS = 1e-6
HEADS = 8
MLA_QK = 96
MLA_Q_RANK = 256
MLA_KV_RANK = 128
ROPE = 32
ROPE_THETA = 10000.0
FOX_DIM = 64
S5_GROUPS = 32
S5_GROUP = 16
S5_STATE = 64
S5_LANES = S5_GROUPS * S5_STATE
LANE = 128
S5_BLOCKS = S5_LANES // LANE
TOK = 256
VMEM_LIMIT = 56 * 1024 * 1024

ADAM_LR = 0.001
ADAM_B1 = 0.9
ADAM_B2 = 0.999
ADAM_EPS = 1e-08
ADAM_WD = 0.01
ADAM_STEP = 10

_ORIG = {}
_off = 0
for _n, _w in (("cq", 256), ("ckv", 128), ("kpe", 32), ("fq", 512), ("fk", 512), ("fv", 512), ("ff", 8), ("s5u", 512),
               ("g_mla", 512), ("g_fox", 512), ("g_s5", 512), ("m_mla", 1024), ("m_fox", 1024), ("m_s5", 1024)):
    _ORIG[_n] = (_off, _w)
    _off += _w
_PAD = {"m_mla": (0, 1024, 0), "m_fox": (1024, 1024, 0), "m_s5": (2048, 1024, 0),
        "fq": (3072, 512, 0), "fk": (3584, 512, 0), "fv": (4096, 512, 0), "s5u": (4608, 512, 0),
        "g_mla": (5120, 512, 0), "g_fox": (5632, 512, 0), "g_s5": (6144, 512, 0),
        "cq": (6656, 256, 0), "ckv": (6912, 128, 0), "kpe": (7040, 128, 64), "ff": (7168, 128, 0)}
NP = 7680
_PAD_ORDER = ("m_mla", "m_fox", "m_s5", "fq", "fk", "fv", "s5u", "g_mla", "g_fox", "g_s5", "cq", "ckv", "kpe", "ff")


def _seg(name):
    start, width, _ = _PAD[name]
    return width, start // width


def _nn(a, b):
    return lax.dot_general(a.astype(bf16), b.astype(bf16), (((1,), (0,)), ((), ())), preferred_element_type=f32)


def _nt(a, b):
    return lax.dot_general(a.astype(bf16), b.astype(bf16), (((1,), (1,)), ((), ())), preferred_element_type=f32)


def _tn(a, b):
    return lax.dot_general(a.astype(bf16), b.astype(bf16), (((0,), (0,)), ((), ())), preferred_element_type=f32)


def _rms(x, g, n):
    r = lax.rsqrt(jnp.sum(x * x, axis=-1, keepdims=True) * (1.0 / n) + EPS)
    return x * r * g, r


def _rms_bwd(dy, x, r, g, n):
    xh = x * r
    dg = jnp.sum(dy * xh, axis=0, keepdims=True)
    dxh = dy * g
    dx = r * (dxh - xh * (jnp.sum(dxh * xh, axis=-1, keepdims=True) * (1.0 / n)))
    return dx, dg


def _sigmoid(x):
    return 1.0 / (1.0 + jnp.exp(-x))


_GELU_C = math.sqrt(2.0 / math.pi)


def _gelu(x):
    t = jnp.tanh(_GELU_C * (x + 0.044715 * x * x * x))
    return 0.5 * x * (1.0 + t), t


def _gelu_grad(x, t):
    return 0.5 * (1.0 + t) + 0.5 * x * (1.0 - t * t) * _GELU_C * (1.0 + 3.0 * 0.044715 * x * x)


def _accumulate(ref, val):
    i = pl.program_id(0)

    @pl.when(i == 0)
    def _():
        ref[...] = val

    @pl.when(i > 0)
    def _():
        ref[...] += val


def _rope(x, c, s1, s2):
    return x * c + pltpu.roll(x, LANE - 16, 1) * s1 + pltpu.roll(x, 16, 1) * s2


def _rope_t(d, c, s1, s2):
    return d * c + pltpu.roll(d * s1, 16, 1) + pltpu.roll(d * s2, LANE - 16, 1)


def _const_map(ndim):
    return lambda *_: (0,) * ndim


def _rowwise(name, body, n_tok, tiled_in, full_in, tiled_out, acc_out, tile=TOK):
    in_specs, args = [], []
    for arr, width, blk in tiled_in:
        in_specs.append(pl.BlockSpec((tile, width), functools.partial(lambda i, b: (i, b), b=blk)))
        args.append(arr)
    for arr in full_in:
        in_specs.append(pl.BlockSpec(arr.shape, _const_map(arr.ndim)))
        args.append(arr)
    out_specs, out_shape = [], []
    for width, dt in tiled_out:
        out_specs.append(pl.BlockSpec((tile, width), lambda i: (i, 0)))
        out_shape.append(jax.ShapeDtypeStruct((n_tok, width), dt))
    for shape, dt in acc_out:
        out_specs.append(pl.BlockSpec(shape, _const_map(len(shape))))
        out_shape.append(jax.ShapeDtypeStruct(shape, dt))
    return pl.pallas_call(
        body, grid=(n_tok // tile,), in_specs=in_specs, out_specs=out_specs, out_shape=out_shape, name=name,
        compiler_params=pltpu.CompilerParams(dimension_semantics=("arbitrary",), vmem_limit_bytes=VMEM_LIMIT),
    )(*args)


def _mm(name, a, b, *, mode, grid, a_spec, b_spec, o_spec, out_shape, acc_shape, add=None, add_spec=None):
    nk = grid[2]

    def body(*refs):
        if add is None:
            a_ref, b_ref, o_ref, acc_ref = refs
        else:
            a_ref, b_ref, add_ref, o_ref, acc_ref = refs
        k = pl.program_id(2)

        @pl.when(k == 0)
        def _():
            acc_ref[...] = jnp.zeros_like(acc_ref)

        acc_ref[...] += {"nn": _nn, "nt": _nt, "tn": _tn}[mode](a_ref[...], b_ref[...])

        @pl.when(k == nk - 1)
        def _():
            r = acc_ref[...]
            if add is not None:
                r = r + add_ref[...]
            o_ref[...] = r.astype(o_ref.dtype)

    in_specs = [a_spec, b_spec] + ([add_spec] if add is not None else [])
    args = (a, b) + ((add,) if add is not None else ())
    return pl.pallas_call(
        body, grid=grid, in_specs=in_specs, out_specs=o_spec, out_shape=out_shape, name=name,
        scratch_shapes=[pltpu.VMEM(acc_shape, f32)],
        compiler_params=pltpu.CompilerParams(dimension_semantics=("arbitrary", "arbitrary", "arbitrary"), vmem_limit_bytes=VMEM_LIMIT),
    )(*args)


def _mm_nn(name, a, b, *, m, n, k, tm, tn, tk, out_dtype=f32, a_koff=0):
    return _mm(name, a, b, mode="nn", grid=(m // tm, n // tn, k // tk),
               a_spec=pl.BlockSpec((tm, tk), lambda i, j, kk: (i, kk + a_koff)),
               b_spec=pl.BlockSpec((tk, tn), lambda i, j, kk: (kk, j)),
               o_spec=pl.BlockSpec((tm, tn), lambda i, j, kk: (i, j)),
               out_shape=jax.ShapeDtypeStruct((m, n), out_dtype), acc_shape=(tm, tn))


ATT_KV = 256
ATT_Q = 512


def _attn_common(mla, n_tok):
    qw = 2 * LANE if mla else LANE
    scale = 1.0 / math.sqrt(MLA_QK if mla else FOX_DIM)
    return qw, scale, min(ATT_Q, n_tok)


def _attn_heads(q_ref, mla):
    out = []
    if mla:
        for e in (0, 1):
            qe = q_ref[:, e * LANE:(e + 1) * LANE]
            out.append((qe.astype(f32).T.astype(bf16), qe))
        return out
    q = q_ref[...]
    tq = q.shape[0]
    qt = q.astype(f32).T
    row = lax.broadcasted_iota(jnp.int32, (LANE, tq), 0)
    lane = lax.broadcasted_iota(jnp.int32, (tq, LANE), 1)
    for e in (0, 1):
        out.append((jnp.where((row >= 64) == bool(e), qt, 0.0).astype(bf16),
                    jnp.where((lane >= 64) == bool(e), q, jnp.zeros((), bf16))))
    return out


def _attn_allowed(off, i, tq, mla):
    kpos = off + lax.broadcasted_iota(jnp.int32, (ATT_KV, tq), 0)
    qpos = i * tq + lax.broadcasted_iota(jnp.int32, (ATT_KV, tq), 1)
    return ((kpos // 64) <= (qpos // 64)) if mla else (kpos <= qpos)


def _attn_fwd(name, q, k, v, cum_b, *, mla, n_tok):
    qw, scale, tq = _attn_common(mla, n_tok)
    nq = n_tok // tq
    nkv = n_tok // ATT_KV
    has_bias = cum_b is not None

    def body(*refs):
        if has_bias:
            q_ref, k_ref, v_ref, cb_ref, o_ref, lse_ref, vt_ref = refs
        else:
            q_ref, k_ref, v_ref, o_ref, lse_ref, vt_ref = refs
        i = pl.program_id(1)

        @pl.when(i == 0)
        def _():
            for jb in range(nkv):
                vt_ref[jb] = v_ref[jb * ATT_KV:(jb + 1) * ATT_KV, :].astype(f32).T.astype(bf16)

        heads = _attn_heads(q_ref, mla)

        def step(j, carry, masked, q_lo=0):
            off = pl.multiple_of(j * ATT_KV, ATT_KV)
            allowed = _attn_allowed(off, i, tq, mla)[:, q_lo:] if masked else None
            keep = lambda old, part: part if q_lo == 0 else jnp.concatenate([old[:, :q_lo], part], axis=1)
            vt = vt_ref[j]
            sts = []
            for e in (0, 1):
                kb = k_ref[pl.ds(off, ATT_KV), e * LANE:(e + 1) * LANE] if mla else k_ref[pl.ds(off, ATT_KV), :]
                sts.append(_nn(kb, heads[e][0][:, q_lo:]))
            stats = []
            for e in (0, 1):
                m, l = carry[e][0][:, q_lo:], carry[e][1][:, q_lo:]
                st = sts[e] * scale
                if has_bias:
                    st = st - jnp.tile(cb_ref[e, pl.ds(off, ATT_KV), :], (1, (tq - q_lo) // LANE))
                if masked:
                    st = jnp.where(allowed, st, -1e30)
                m_new = jnp.maximum(m, jnp.max(st, axis=0, keepdims=True))
                alpha = jnp.exp(m - m_new)
                pt = jnp.exp(st - m_new)
                stats.append((m_new, alpha * l + jnp.sum(pt, axis=0, keepdims=True), alpha, pt.astype(bf16)))
            new = []
            for e in (0, 1):
                m_new, l, alpha, pt = stats[e]
                acc = alpha * carry[e][2][:, q_lo:] + _nn(vt[64 * e:64 * e + 64, :], pt)
                new.append((keep(carry[e][0], m_new), keep(carry[e][1], l), keep(carry[e][2], acc)))
            return tuple(new)

        init = tuple((jnp.full((1, tq), -1e30, f32), jnp.zeros((1, tq), f32), jnp.zeros((64, tq), f32)) for _ in (0, 1))
        n_full = i * (tq // ATT_KV)
        carry = lax.fori_loop(0, n_full, functools.partial(step, masked=False), init)
        for d in range(tq // ATT_KV):
            carry = step(n_full + d, carry, True, q_lo=d * ATT_KV)
        o_ref[...] = jnp.concatenate([carry[e][2] / carry[e][1] for e in (0, 1)], axis=0).T
        lse_ref[...] = jnp.zeros_like(lse_ref)
        for e in (0, 1):
            lse_ref[e:e + 1, :] = carry[e][0] + jnp.log(carry[e][1])

    in_specs = [pl.BlockSpec((tq, qw), lambda p, i: (i, p)),
                pl.BlockSpec((n_tok, qw), lambda p, i: (0, p)),
                pl.BlockSpec((n_tok, LANE), lambda p, i: (0, p))]
    args = [q, k, v]
    if has_bias:
        in_specs.append(pl.BlockSpec((2, n_tok, LANE), lambda p, i: (p, 0, 0)))
        args.append(cum_b)
    return pl.pallas_call(
        body, grid=(4, nq), in_specs=in_specs,
        out_specs=[pl.BlockSpec((tq, LANE), lambda p, i: (i, p)), pl.BlockSpec((None, 8, tq), lambda p, i: (p, 0, i))],
        out_shape=[jax.ShapeDtypeStruct((n_tok, 512), f32), jax.ShapeDtypeStruct((4, 8, n_tok), f32)], name=name,
        scratch_shapes=[pltpu.VMEM((nkv, LANE, ATT_KV), bf16)],
        compiler_params=pltpu.CompilerParams(dimension_semantics=("arbitrary", "arbitrary"), vmem_limit_bytes=VMEM_LIMIT),
    )(*args)


def _attn_bwd(name, q, k, v, o, lse, do, cum_b, *, mla, n_tok):
    qw, scale, tq = _attn_common(mla, n_tok)
    nq = n_tok // tq
    nkv = n_tok // ATT_KV
    has_bias = cum_b is not None

    def body(*refs):
        if has_bias:
            q_ref, k_ref, v_ref, o_ref, lse_ref, do_ref, cb_ref, dq_ref, dk_ref, dv_ref, dck_ref, dcq_ref, kt_ref = refs
        else:
            q_ref, k_ref, v_ref, o_ref, lse_ref, do_ref, dq_ref, dk_ref, dv_ref, kt_ref = refs
        p = pl.program_id(0)
        i = pl.program_id(1)

        @pl.when(i == 0)
        def _():
            dk_ref[...] = jnp.zeros_like(dk_ref)
            dv_ref[...] = jnp.zeros_like(dv_ref)
            for jb in range(nkv):
                for c0 in range(0, qw, LANE):
                    kt_ref[jb, c0:c0 + LANE, :] = k_ref[jb * ATT_KV:(jb + 1) * ATT_KV, c0:c0 + LANE].astype(f32).T.astype(bf16)

        if has_bias:
            @pl.when(jnp.logical_and(i == 0, p == 0))
            def _():
                dck_ref[...] = jnp.zeros_like(dck_ref)

        heads = _attn_heads(q_ref, mla)
        do = do_ref[...]
        do_t = do.T
        prod_t = (do * o_ref[...]).T
        row = lax.broadcasted_iota(jnp.int32, (LANE, tq), 0)
        lane = lax.broadcasted_iota(jnp.int32, (tq, LANE), 1)
        lane_k = lax.broadcasted_iota(jnp.int32, (ATT_KV, LANE), 1)
        per_head = []
        for e in (0, 1):
            sel_r = (row >= 64) == bool(e)
            per_head.append((jnp.where(sel_r, do_t, 0.0).astype(bf16),
                             jnp.where((lane >= 64) == bool(e), do, 0.0).astype(bf16),
                             jnp.sum(jnp.where(sel_r, prod_t, 0.0), axis=0, keepdims=True),
                             lse_ref[e:e + 1, :]))
        dq_rows = LANE if mla else 64

        def step(j, carry, masked, q_lo=0):
            off = pl.multiple_of(j * ATT_KV, ATT_KV)
            allowed = _attn_allowed(off, i, tq, mla)[:, q_lo:] if masked else None
            keep = lambda old, part: part if q_lo == 0 else jnp.concatenate([old[:, :q_lo], part], axis=1)
            vb = v_ref[pl.ds(off, ATT_KV), :]
            kt = kt_ref[j]
            cols = [slice(e * LANE, (e + 1) * LANE) if mla else slice(None) for e in (0, 1)]
            sts = [_nn(k_ref[pl.ds(off, ATT_KV), cols[e]], heads[e][0][:, q_lo:]) for e in (0, 1)]
            dpts = [_nn(vb, per_head[e][0][:, q_lo:]) for e in (0, 1)]
            mids = []
            for e in (0, 1):
                _, _, delta, lse_e = per_head[e]
                st = sts[e] * scale
                if has_bias:
                    st = st - jnp.tile(cb_ref[e, pl.ds(off, ATT_KV), :], (1, (tq - q_lo) // LANE))
                pt = jnp.exp(st - lse_e[:, q_lo:])
                if masked:
                    pt = jnp.where(allowed, pt, 0.0)
                dst = pt * (dpts[e] - delta[:, q_lo:])
                qsum = carry[e][1][:, q_lo:]
                if has_bias:
                    rs = jnp.sum(dst, axis=1, keepdims=True)
                    dck_ref[pl.ds(off, ATT_KV), :] += jnp.where(lane_k == 2 * p + e, -rs, 0.0)
                    qsum = qsum + jnp.sum(dst, axis=0, keepdims=True)
                mids.append((pt.astype(bf16), dst.astype(bf16), qsum))
            new = []
            for e in (0, 1):
                pt, dst, qsum = mids[e]
                kt_e = kt[e * LANE:(e + 1) * LANE, :] if mla else kt[64 * e:64 * e + 64, :]
                dq = carry[e][0][:, q_lo:] + _nn(kt_e, dst) * scale
                new.append((keep(carry[e][0], dq), keep(carry[e][1], qsum)))
                dk_ref[pl.ds(off, ATT_KV), cols[e]] += _nn(dst, heads[e][1][q_lo:, :]) * scale
                dv_ref[pl.ds(off, ATT_KV), :] += _nn(pt, per_head[e][1][q_lo:, :])
            return tuple(new)

        init = tuple((jnp.zeros((dq_rows, tq), f32), jnp.zeros((1, tq), f32)) for _ in (0, 1))
        n_full = i * (tq // ATT_KV)
        carry = lax.fori_loop(0, n_full, functools.partial(step, masked=False), init)
        for d in range(tq // ATT_KV):
            carry = step(n_full + d, carry, True, q_lo=d * ATT_KV)
        if mla:
            for e in (0, 1):
                dq_ref[:, e * LANE:(e + 1) * LANE] = carry[e][0].T
        else:
            dq_ref[...] = jnp.concatenate([carry[0][0], carry[1][0]], axis=0).T
        if has_bias:
            dcq_ref[...] = jnp.zeros_like(dcq_ref)
            for e in (0, 1):
                dcq_ref[e:e + 1, :] = carry[e][1]

    tile_q = pl.BlockSpec((tq, qw), lambda p, i: (i, p))
    tile_v = pl.BlockSpec((tq, LANE), lambda p, i: (i, p))
    full_k = pl.BlockSpec((n_tok, qw), lambda p, i: (0, p))
    full_v = pl.BlockSpec((n_tok, LANE), lambda p, i: (0, p))
    in_specs = [tile_q, full_k, full_v, tile_v, pl.BlockSpec((None, 8, tq), lambda p, i: (p, 0, i)), tile_v]
    args = [q, k, v, o, lse, do]
    out_specs = [tile_q, full_k, full_v]
    out_shape = [jax.ShapeDtypeStruct((n_tok, 4 * qw), f32), jax.ShapeDtypeStruct((n_tok, 4 * qw), f32),
                 jax.ShapeDtypeStruct((n_tok, 512), f32)]
    if has_bias:
        in_specs.append(pl.BlockSpec((2, n_tok, LANE), lambda p, i: (p, 0, 0)))
        args.append(cum_b)
        out_specs += [pl.BlockSpec((n_tok, LANE), _const_map(2)), pl.BlockSpec((None, 8, tq), lambda p, i: (p, 0, i))]
        out_shape += [jax.ShapeDtypeStruct((n_tok, LANE), f32), jax.ShapeDtypeStruct((4, 8, n_tok), f32)]
    return pl.pallas_call(
        body, grid=(4, nq), in_specs=in_specs, out_specs=out_specs, out_shape=out_shape, name=name,
        scratch_shapes=[pltpu.VMEM((nkv, qw, ATT_KV), bf16)],
        compiler_params=pltpu.CompilerParams(dimension_semantics=("arbitrary", "arbitrary"), vmem_limit_bytes=VMEM_LIMIT),
    )(*args)


def _s5_disc(lr, li, ldt):
    dt = jnp.exp(ldt)
    mag = jnp.exp(lr * dt)
    a_re = mag * jnp.cos(li * dt)
    a_im = mag * jnp.sin(li * dt)
    den = lr * lr + li * li
    f_re = ((a_re - 1.0) * lr + a_im * li) / den
    f_im = (a_im * lr - (a_re - 1.0) * li) / den
    return a_re, a_im, f_re, f_im


def _s5_param_fwd(lr, li, ldt, b_re, b_im):
    def body(lr_ref, li_ref, ldt_ref, br_ref, bi_ref, ar_ref, ai_ref, bbr_ref, bbi_ref):
        a_re, a_im, f_re, f_im = _s5_disc(lr_ref[...], li_ref[...], ldt_ref[...])
        ar_ref[...] = a_re
        ai_ref[...] = a_im
        br, bi = br_ref[...], bi_ref[...]
        bbr_ref[...] = f_re * br - f_im * bi
        bbi_ref[...] = f_re * bi + f_im * br

    col = jax.ShapeDtypeStruct(lr.shape, f32)
    mat = jax.ShapeDtypeStruct(b_re.shape, f32)
    return pl.pallas_call(body, out_shape=[col, col, mat, mat], name="s5_param_fwd")(lr, li, ldt, b_re, b_im)


def _s5_param_bwd(lr, li, ldt, b_re, b_im, da_re, da_im, dbb_re, dbb_im):
    def body(lr_ref, li_ref, ldt_ref, br_ref, bi_ref, dar_ref, dai_ref, gbr_ref, gbi_ref,
             dlr_ref, dli_ref, dldt_ref, dbr_ref, dbi_ref):
        (a_re, a_im, f_re, f_im), vjp = jax.vjp(_s5_disc, lr_ref[...], li_ref[...], ldt_ref[...])
        br, bi, gr, gi = br_ref[...], bi_ref[...], gbr_ref[...], gbi_ref[...]
        dbr_ref[...] = f_re * gr + f_im * gi
        dbi_ref[...] = f_re * gi - f_im * gr
        dfr = jnp.sum(br * gr + bi * gi, axis=-1, keepdims=True)
        dfi = jnp.sum(br * gi - bi * gr, axis=-1, keepdims=True)
        dlr, dli, dldt = vjp((dar_ref[...], dai_ref[...], dfr, dfi))
        dlr_ref[...] = dlr
        dli_ref[...] = dli
        dldt_ref[...] = jnp.sum(dldt.reshape(S5_GROUPS, S5_STATE, 1), axis=1)

    col = jax.ShapeDtypeStruct((S5_LANES, 1), f32)
    mat = jax.ShapeDtypeStruct((S5_LANES, S5_GROUP), f32)
    return pl.pallas_call(body, out_shape=[col, col, jax.ShapeDtypeStruct((S5_GROUPS, 1), f32), mat, mat],
                          name="s5_param_bwd")(lr, li, ldt, b_re, b_im, da_re, da_im, dbb_re, dbb_im)


_SCAN_NB = 4


def _to_streams(a):
    s, c = a.shape
    return jnp.swapaxes(a.reshape(8, s // 8, c), 0, 1).reshape(s, c)


def _from_streams(a):
    s, c = a.shape
    return jnp.swapaxes(a.reshape(s // 8, 8, c), 0, 1).reshape(s, c)


def _s5_scan(name, src, wq, wy, add, y_dtype, a_re8, a_im8, *, reverse, n_tok, grads_of=None):
    rows = n_tok // 8
    nb = _SCAN_NB
    assert nb == 4

    def scan_body(src_ref, w_ref, wy_ref, add_ref, ar_ref, ai_ref, x_ref, y_ref):
        for ri in (0, 1):
            bu = _nn(src_ref[...], w_ref[ri])
            for b in range(nb):
                x_ref[ri, b] = bu[:, b * LANE:(b + 1) * LANE]
        bu_ref = x_ref
        a_r = [ar_ref[b] for b in range(nb)]
        a_i = [ai_ref[b] for b in range(nb)]
        zero = jnp.zeros((8, LANE), f32)
        one = jnp.ones((8, LANE), f32)

        def rows_at(r):
            rr = (rows - 1 - r) if reverse else r
            return pl.ds(pl.multiple_of(rr * 8, 8), 8)

        def pass1(r, carry):
            out = []
            sl = rows_at(r)
            for b in range(nb):
                xr, xi, mr, mi = carry[b]
                nr = a_r[b] * xr - a_i[b] * xi + bu_ref[0, b, sl, :]
                ni = a_r[b] * xi + a_i[b] * xr + bu_ref[1, b, sl, :]
                x_ref[0, b, sl, :] = nr
                x_ref[1, b, sl, :] = ni
                out.append((nr, ni, a_r[b] * mr - a_i[b] * mi, a_r[b] * mi + a_i[b] * mr))
            return tuple(out)

        carry = lax.fori_loop(0, rows, pass1, tuple((zero, zero, one, zero) for _ in range(nb)))
        sub = lax.broadcasted_iota(jnp.int32, (8, LANE), 0)
        feed = []
        for b in range(nb):
            lr_, li_, pr, pi = carry[b]
            fr, fi = zero, zero
            for _ in range(7):
                tr = lr_ + pr * fr - pi * fi
                ti = li_ + pr * fi + pi * fr
                if reverse:
                    fr = jnp.where(sub < 7, pltpu.roll(tr, 7, 0), 0.0)
                    fi = jnp.where(sub < 7, pltpu.roll(ti, 7, 0), 0.0)
                else:
                    fr = jnp.where(sub > 0, pltpu.roll(tr, 1, 0), 0.0)
                    fi = jnp.where(sub > 0, pltpu.roll(ti, 1, 0), 0.0)
            feed.append((fr, fi))

        def pass2(r, carry):
            out = []
            sl = rows_at(r)
            for b in range(nb):
                mr, mi = carry[b]
                fr, fi = feed[b]
                x_ref[0, b, sl, :] += mr * fr - mi * fi
                x_ref[1, b, sl, :] += mr * fi + mi * fr
                out.append((a_r[b] * mr - a_i[b] * mi, a_r[b] * mi + a_i[b] * mr))
            return tuple(out)

        lax.fori_loop(0, rows, pass2, tuple((a_r[b], a_i[b]) for b in range(nb)))

        y = None
        for ri in (0, 1):
            for b in range(nb):
                t = _nn(x_ref[ri, b], wy_ref[ri, b * LANE:(b + 1) * LANE, :])
                y = t if y is None else y + t
        if add is not None:
            y = y + add_ref[...]
        y_ref[...] = y.astype(y_ref.dtype)

    def grads_body(src_ref, xs_ref, u_ref, g_ref, da_ref, dc_ref, db_ref):
        t = lax.broadcasted_iota(jnp.int32, (n_tok, LANE), 0)
        sub = lax.broadcasted_iota(jnp.int32, (8, LANE), 0)

        def prev(v):
            return (jnp.where(t >= 8, pltpu.roll(v, 8, 0), 0.0),
                    jnp.where(sub > 0, pltpu.roll(v[n_tok - 8:, :], 1, 0), 0.0))

        for b in range(nb):
            (xr, hr), (xi, hi) = prev(xs_ref[0, b]), prev(xs_ref[1, b])
            gr, gi = g_ref[0, b], g_ref[1, b]
            gr0, gi0 = gr[0:8, :], gi[0:8, :]
            da_ref[b, 0:1, :] = (jnp.sum(xr * gr + xi * gi, axis=0, keepdims=True)
                                 + jnp.sum(hr * gr0 + hi * gi0, axis=0, keepdims=True))
            da_ref[b, 1:2, :] = (jnp.sum(xr * gi - xi * gr, axis=0, keepdims=True)
                                 + jnp.sum(hr * gi0 - hi * gr0, axis=0, keepdims=True))
            for ri in (0, 1):
                dc_ref[ri, b * LANE:(b + 1) * LANE, :] = _tn(xs_ref[ri, b], src_ref[...])
                db_ref[ri, :, b * LANE:(b + 1) * LANE] = _tn(u_ref[...], g_ref[ri, b])

    n_in = 3 + (add is not None) + 2 * (grads_of is not None)

    def body(*refs):
        ins, rest = list(refs[:n_in]), refs[n_in:]
        src_ref, w_ref, wy_ref = ins[:3]
        add_ref = ins[3] if add is not None else None
        ar_ref, ai_ref = rest[:2]
        if grads_of is None:
            x_ref, y_ref = rest[2:]
            scan_body(src_ref, w_ref, wy_ref, add_ref, ar_ref, ai_ref, x_ref, y_ref)
        else:
            xs_ref, u_ref = ins[-2:]
            y_ref, da_ref, dc_ref, db_ref, x_ref = rest[2:]
            scan_body(src_ref, w_ref, wy_ref, add_ref, ar_ref, ai_ref, x_ref, y_ref)
            grads_body(src_ref, xs_ref, u_ref, x_ref, da_ref, dc_ref, db_ref)

    blk = pl.BlockSpec((2, nb, n_tok, LANE), lambda g: (0, g, 0, 0))
    ablk = pl.BlockSpec((nb, 8, LANE), lambda g: (g, 0, 0))
    col = pl.BlockSpec((n_tok, LANE), lambda g: (0, g))
    in_specs = [col, pl.BlockSpec((2, None, LANE, 512), lambda g: (0, g, 0, 0)), pl.BlockSpec((2, None, 512, LANE), lambda g: (0, g, 0, 0))]
    args = [src, wq, wy]
    if add is not None:
        in_specs.append(col)
        args.append(add)
    y_shape = jax.ShapeDtypeStruct((n_tok, 512), y_dtype)
    x_shape = (2, S5_BLOCKS, n_tok, LANE)
    params = pltpu.CompilerParams(dimension_semantics=("arbitrary",), vmem_limit_bytes=VMEM_LIMIT)
    if grads_of is None:
        return pl.pallas_call(
            body, grid=(S5_BLOCKS // nb,), in_specs=in_specs + [ablk, ablk], out_specs=[blk, col],
            out_shape=[jax.ShapeDtypeStruct(x_shape, f32), y_shape], name=name, compiler_params=params,
        )(*args, a_re8, a_im8)
    return pl.pallas_call(
        body, grid=(S5_BLOCKS // nb,), in_specs=in_specs + [blk, col, ablk, ablk],
        out_specs=[col, pl.BlockSpec((nb, 2, LANE), lambda g: (g, 0, 0)), pl.BlockSpec((2, None, 512, LANE), lambda g: (0, g, 0, 0)),
                   pl.BlockSpec((2, None, LANE, 512), lambda g: (0, g, 0, 0))],
        out_shape=[y_shape, jax.ShapeDtypeStruct((S5_BLOCKS, 2, LANE), f32), jax.ShapeDtypeStruct((2, S5_Q, 512, LANE), f32),
                   jax.ShapeDtypeStruct((2, S5_Q, LANE, 512), f32)],
        scratch_shapes=[pltpu.VMEM((2, nb, n_tok, LANE), f32)], name=name, compiler_params=params,
    )(*args, *grads_of, a_re8, a_im8)


S5_Q = 4


def _bd8(t):
    _, a, b = t.shape
    t = t.reshape(S5_Q, 8, a, 1, b)
    eye = jnp.eye(8, dtype=jnp.bool_).reshape(1, 8, 1, 8, 1)
    return jnp.where(eye, jnp.broadcast_to(t, (S5_Q, 8, a, 8, b)), jnp.zeros((), t.dtype)).reshape(S5_Q, 8 * a, 8 * b)


def _bd8_diag(m, a, b):
    m = m.reshape(S5_Q, 8, a, 8, b)
    eye = jnp.eye(8, dtype=jnp.bool_).reshape(1, 8, 1, 8, 1)
    return jnp.sum(jnp.where(eye, m, 0.0), axis=3).reshape(S5_GROUPS, a, b)


N_CHIPS = 4
_BIG_SHARD = {"w_in": (1, 1024, 1770, 1792), "mla_w_q_up": (1, 256, 192, 256), "mla_w_kv_up": (1, 128, 256, 256),
              "s5_w_glu": (0, 128, 512, 512), "w_branch_out": (0, 384, 1024, 1024), "w_out": (0, 256, 1024, 1024)}


def _to_shards(name, m):
    axis, r, c, cp = _BIG_SHARD[name]
    if axis == 0:
        return m.reshape(N_CHIPS, r, c)
    return jnp.stack([jnp.pad(m[:, j * c:(j + 1) * c], ((0, 0), (0, cp - c))) for j in range(N_CHIPS)])


def _from_shards(name, s):
    axis, r, c, cp = _BIG_SHARD[name]
    if axis == 0:
        return s.reshape(N_CHIPS * r, c)
    return jnp.concatenate([s[j, :, :c] for j in range(N_CHIPS)], axis=1)


def _pad_w_in(w):
    pieces, pos = [], 0
    for name in _PAD_ORDER:
        start, width, inner = _PAD[name]
        o0, ow = _ORIG[name]
        if start + inner > pos:
            pieces.append(jnp.zeros((w.shape[0], start + inner - pos), w.dtype))
        pieces.append(w[:, o0:o0 + ow])
        pos = start + inner + ow
    pieces.append(jnp.zeros((w.shape[0], NP - pos), w.dtype))
    return jnp.concatenate(pieces, axis=1)


def _prep_weights(small, big):
    per_layer = jax.vmap
    w = {}
    w["w_in_shards"] = big["w_in"]
    w["w_in"] = per_layer(lambda s: _pad_w_in(_from_shards("w_in", s)))(big["w_in"])

    def q_up(s):
        wq = _from_shards("mla_w_q_up", s).reshape(MLA_Q_RANK, HEADS, MLA_QK)
        return jnp.pad(wq, ((0, 0), (0, 0), (0, LANE - MLA_QK))).reshape(MLA_Q_RANK, HEADS * LANE)

    def kv_up(s):
        wkv = _from_shards("mla_w_kv_up", s).reshape(MLA_KV_RANK, HEADS, 128)
        wk = jnp.pad(wkv[:, :, :64], ((0, 0), (0, 0), (0, 64))).reshape(MLA_KV_RANK, HEADS * LANE)
        return jnp.concatenate([wk, wkv[:, :, 64:].reshape(MLA_KV_RANK, 512)], axis=1)

    w["wq"] = per_layer(q_up)(big["mla_w_q_up"])
    w["wkv"] = per_layer(kv_up)(big["mla_w_kv_up"])
    for name, key in (("w_glu", "s5_w_glu"), ("wo", "w_branch_out"), ("w_out", "w_out")):
        w[name] = per_layer(functools.partial(_from_shards, key))(big[key])
    row = lambda a: a.astype(f32)[:, None, :]
    lanes = lambda a, n: jnp.pad(row(a), ((0, 0), (0, 0), (0, LANE - n)))
    w["norm_g"] = row(small["norm_g"])
    w["qa_g"] = row(small["mla_q_a_norm"])
    w["kva_g"] = row(small["mla_kv_a_norm"])
    w["qn_g"] = lanes(small["mla_q_norm"], MLA_QK)
    w["kn_g"] = lanes(small["mla_k_norm"], MLA_QK)
    w["fq_g"] = jnp.tile(row(small["fox_q_norm"]), (1, 1, 2))
    w["fk_g"] = jnp.tile(row(small["fox_k_norm"]), (1, 1, 2))
    w["b_f"] = lanes(small["fox_b_f"], HEADS)
    w["lr"] = small["s5_lambda_re"].reshape(DEPTH, S5_LANES, 1)
    w["li"] = small["s5_lambda_im"].reshape(DEPTH, S5_LANES, 1)
    w["ldt"] = jnp.repeat(small["s5_log_dt"], S5_STATE, axis=1).reshape(DEPTH, S5_LANES, 1)
    w["b_re"] = small["s5_b_re"].reshape(DEPTH, S5_LANES, S5_GROUP)
    w["b_im"] = small["s5_b_im"].reshape(DEPTH, S5_LANES, S5_GROUP)
    w["s5_d"] = row(small["s5_d"])
    w["b_glu"] = row(small["s5_b_glu"])
    a_re, a_im, bb_re, bb_im = _s5_param_fwd(w["lr"], w["li"], w["ldt"], w["b_re"], w["b_im"])
    per_group = lambda m: m.reshape(S5_GROUPS, S5_STATE, S5_GROUP)
    pair = lambda f: per_layer(lambda re, im: jnp.stack([f(re), f(im)]).astype(bf16))
    c_re, c_im = small["s5_c_re"], -small["s5_c_im"]
    w["b_cn"] = pair(lambda m: _bd8(jnp.swapaxes(per_group(m), 1, 2)))(bb_re, bb_im)
    w["b_nc"] = pair(lambda m: _bd8(per_group(m)))(bb_re, bb_im)
    w["c_nc"] = pair(lambda m: _bd8(jnp.swapaxes(m, 1, 2)))(c_re, c_im)
    w["c_cn"] = pair(_bd8)(c_re, c_im)
    sublanes = lambda a: jnp.broadcast_to(a.reshape(DEPTH, S5_BLOCKS, 1, LANE), (DEPTH, S5_BLOCKS, 8, LANE))
    w["a_re8"], w["a_im8"], w["a_im8_neg"] = sublanes(a_re), sublanes(a_im), sublanes(-a_im)
    return w


def _fox_halves(x, lane):
    sq = x * x
    lo = jnp.sum(jnp.where(lane < 64, sq, 0.0), axis=-1, keepdims=True)
    hi = jnp.sum(sq, axis=-1, keepdims=True) - lo
    return jnp.where(lane < 64, lax.rsqrt(lo * (1.0 / 64) + EPS), lax.rsqrt(hi * (1.0 / 64) + EPS))


def _fox_halves_bwd(dy, x, r, g, lane):
    xh = x * r
    dxh = dy * g
    pr = dxh * xh
    lo = jnp.sum(jnp.where(lane < 64, pr, 0.0), axis=-1, keepdims=True)
    hi = jnp.sum(pr, axis=-1, keepdims=True) - lo
    mean = jnp.where(lane < 64, lo, hi) * (1.0 / 64)
    return r * (dxh - xh * mean), jnp.sum(dy * xh, axis=0, keepdims=True)


def _mla_recompute(cq, ckv, kpe, c, s1, s2, qa_g, kva_g, wq, wkv):
    cqn, r_cq = _rms(cq, qa_g, MLA_Q_RANK)
    ckvn, r_ckv = _rms(ckv, kva_g, MLA_KV_RANK)
    cqn_b = cqn.astype(bf16)
    ckvn_b = ckvn.astype(bf16)
    q_raw = _nn(cqn_b, wq)
    kv_raw = _nn(ckvn_b, wkv)
    kpe_rot = _rope(kpe, c, s1, s2)
    return cqn_b, r_cq, ckvn_b, r_ckv, q_raw, kv_raw, kpe_rot


def _layer_fwd(x, w, rope_tabs, n_tok):
    c_tab, s1_tab, s2_tab = rope_tabs
    saved = {"x": x}

    def norm_body(x_ref, g_ref, h_ref):
        h_ref[...] = _rms(x_ref[...], g_ref[...], D_MODEL)[0].astype(bf16)

    (h,) = _rowwise("norm_fwd", norm_body, n_tok, [(x, D_MODEL, 0)], [w["norm_g"]], [(D_MODEL, bf16)], [])
    proj = _mm_nn("in_proj", h, w["w_in"], m=n_tok, n=NP, k=D_MODEL, tm=n_tok, tn=512, tk=D_MODEL)
    saved["h"], saved["proj"] = h, proj

    def mla_prep_body(cq_ref, ckv_ref, kpe_ref, c_ref, s1_ref, s2_ref, qa_ref, kva_ref, wq_ref, wkv_ref, qn_g_ref, kn_g_ref,
                      qn_ref, kn_ref, v_ref):
        c, s1, s2 = c_ref[...], s1_ref[...], s2_ref[...]
        _, _, _, _, q_raw, kv_raw, kpe_rot = _mla_recompute(cq_ref[...], ckv_ref[...], kpe_ref[...], c, s1, s2,
                                                            qa_ref[...], kva_ref[...], wq_ref[...], wkv_ref[...])
        for hd in range(HEADS):
            sl = slice(hd * LANE, (hd + 1) * LANE)
            qn_ref[:, sl] = _rms(_rope(q_raw[:, sl], c, s1, s2), qn_g_ref[...], MLA_QK)[0].astype(bf16)
            kn_ref[:, sl] = _rms(kv_raw[:, sl] + kpe_rot, kn_g_ref[...], MLA_QK)[0].astype(bf16)
        v_ref[...] = kv_raw[:, HEADS * LANE:].astype(bf16)

    qn, kn, v_mla = _rowwise(
        "mla_prep", mla_prep_body, n_tok,
        [(proj, *_seg("cq")), (proj, *_seg("ckv")), (proj, *_seg("kpe")), (c_tab, LANE, 0), (s1_tab, LANE, 0), (s2_tab, LANE, 0)],
        [w["qa_g"], w["kva_g"], w["wq"], w["wkv"], w["qn_g"], w["kn_g"]],
        [(HEADS * LANE, bf16), (HEADS * LANE, bf16), (512, bf16)], [])
    y_mla, lse_mla = _attn_fwd("mla_attn_fwd", qn, kn, v_mla, None, mla=True, n_tok=n_tok)
    saved.update(qn=qn, kn=kn, v_mla=v_mla, y_mla=y_mla, lse_mla=lse_mla)

    def fox_prep_body(fq_ref, fk_ref, fv_ref, ff_ref, qg_ref, kg_ref, bf_ref, fqn_ref, fkn_ref, fvb_ref, logf_ref):
        lane = lax.broadcasted_iota(jnp.int32, (TOK, LANE), 1)
        for blk in range(4):
            sl = slice(blk * LANE, (blk + 1) * LANE)
            xq = fq_ref[:, sl]
            fqn_ref[:, sl] = (xq * _fox_halves(xq, lane) * qg_ref[...]).astype(bf16)
            xk = fk_ref[:, sl]
            fkn_ref[:, sl] = (xk * _fox_halves(xk, lane) * kg_ref[...]).astype(bf16)
        fvb_ref[...] = fv_ref[...].astype(bf16)
        z = ff_ref[...] + bf_ref[...]
        logf_ref[...] = jnp.minimum(z, 0.0) - jnp.log(1.0 + jnp.exp(-jnp.abs(z)))

    fqn, fkn, fvb, logf = _rowwise(
        "fox_prep", fox_prep_body, n_tok,
        [(proj, *_seg("fq")), (proj, *_seg("fk")), (proj, *_seg("fv")), (proj, *_seg("ff"))],
        [w["fq_g"], w["fk_g"], w["b_f"]],
        [(512, bf16), (512, bf16), (512, bf16), (LANE, f32)], [])

    def cum_body(x_ref, cum_ref):
        x = x_ref[...]
        t = lax.broadcasted_iota(jnp.int32, x.shape, 0)
        s = 1
        while s < n_tok:
            x = x + jnp.where(t >= s, pltpu.roll(x, s, 0), 0.0)
            s *= 2
        for hd in range(HEADS):
            cum_ref[hd] = jnp.broadcast_to(x[:, hd:hd + 1], (n_tok, LANE))

    cum_b = pl.pallas_call(cum_body, out_shape=jax.ShapeDtypeStruct((HEADS, n_tok, LANE), f32), name="fox_cum")(logf)
    y_fox, lse_fox = _attn_fwd("fox_attn_fwd", fqn, fkn, fvb, cum_b, mla=False, n_tok=n_tok)
    saved.update(fqn=fqn, fkn=fkn, fvb=fvb, cum_b=cum_b, y_fox=y_fox, lse_fox=lse_fox)

    u_w, u_blk = _seg("s5u")
    u_streams = _to_streams(proj[:, u_blk * u_w:(u_blk + 1) * u_w])
    xs, ylin = _s5_scan("s5_scan_fwd", u_streams, w["b_cn"], w["c_nc"], None, f32, w["a_re8"], w["a_im8"], reverse=False, n_tok=n_tok)
    ylin = _from_streams(ylin)

    def s5_post_body(yl_ref, u_ref, d_ref, wg_ref, bg_ref, out_ref):
        y = yl_ref[...] + d_ref[...] * u_ref[...]
        z, _ = _gelu(y)
        out_ref[...] = z * _sigmoid(_nn(z, wg_ref[...]) + bg_ref[...])

    (y_s5,) = _rowwise("s5_post", s5_post_body, n_tok, [(ylin, 512, 0), (proj, u_w, u_blk)],
                       [w["s5_d"], w["w_glu"], w["b_glu"]], [(512, f32)], [])
    saved.update(xs=xs, ylin=ylin, y_s5=y_s5, u_streams=u_streams)

    def merge_body(ym_ref, yf_ref, ys_ref, gm_ref, gf_ref, gs_ref, mm_ref, mf_ref, ms_ref, x_ref, wo_ref, wout_ref, out_ref):
        merged = jnp.zeros((TOK, D_MODEL), f32)
        for b, (y_ref, g_ref, m_ref) in enumerate(((ym_ref, gm_ref, mm_ref), (yf_ref, gf_ref, mf_ref), (ys_ref, gs_ref, ms_ref))):
            g = g_ref[...]
            a = y_ref[...] * (g * _sigmoid(g))
            merged = merged + _sigmoid(m_ref[...]) * _nn(a, wo_ref[b * 512:(b + 1) * 512, :])
        out_ref[...] = x_ref[...] + _nn(merged, wout_ref[...])

    (out,) = _rowwise(
        "merge_fwd", merge_body, n_tok,
        [(y_mla, 512, 0), (y_fox, 512, 0), (y_s5, 512, 0), (proj, *_seg("g_mla")), (proj, *_seg("g_fox")), (proj, *_seg("g_s5")),
         (proj, *_seg("m_mla")), (proj, *_seg("m_fox")), (proj, *_seg("m_s5")), (x, D_MODEL, 0)],
        [w["wo"], w["w_out"]], [(D_MODEL, f32)], [])
    return out, saved


def _layer_bwd(dout, w, sv, rope_tabs, n_tok):
    c_tab, s1_tab, s2_tab = rope_tabs
    proj, x = sv["proj"], sv["x"]
    grads = {}

    def merge_bwd_body(ym_ref, yf_ref, ys_ref, gm_ref, gf_ref, gs_ref, mm_ref, mf_ref, ms_ref, do_ref, wo_ref, wout_ref,
                       dym_ref, dyf_ref, dys_ref, dgm_ref, dgf_ref, dgs_ref, dmm_ref, dmf_ref, dms_ref, dwo_ref, dwout_ref):
        do = do_ref[...]
        branches = ((ym_ref, gm_ref, mm_ref, dym_ref, dgm_ref, dmm_ref), (yf_ref, gf_ref, mf_ref, dyf_ref, dgf_ref, dmf_ref),
                    (ys_ref, gs_ref, ms_ref, dys_ref, dgs_ref, dms_ref))
        acts, outs, sigs = [], [], []
        merged = jnp.zeros((TOK, D_MODEL), f32)
        for b, (y_ref, g_ref, m_ref, _, _, _) in enumerate(branches):
            g = g_ref[...]
            a = (y_ref[...] * (g * _sigmoid(g))).astype(bf16)
            o = _nn(a, wo_ref[b * 512:(b + 1) * 512, :])
            s = _sigmoid(m_ref[...])
            merged = merged + s * o
            acts.append(a)
            outs.append(o)
            sigs.append(s)
        dmerged = _nt(do, wout_ref[...])
        _accumulate(dwout_ref, _tn(merged, do))
        dwo = []
        for b, (y_ref, g_ref, m_ref, dy_ref, dg_ref, dm_ref) in enumerate(branches):
            s, o = sigs[b], outs[b]
            dm_ref[...] = (dmerged * o * s * (1.0 - s)).astype(bf16)
            d_o = dmerged * s
            da = _nt(d_o, wo_ref[b * 512:(b + 1) * 512, :])
            dwo.append(_tn(acts[b], d_o))
            g = g_ref[...]
            sg = _sigmoid(g)
            dy_ref[...] = da * (g * sg)
            dg_ref[...] = (da * y_ref[...] * (sg * (1.0 + g * (1.0 - sg)))).astype(bf16)
        _accumulate(dwo_ref, jnp.concatenate(dwo, axis=0))

    (dy_mla, dy_fox, dy_s5, dg_mla, dg_fox, dg_s5, dm_mla, dm_fox, dm_s5, dwo, dwout) = _rowwise(
        "merge_bwd", merge_bwd_body, n_tok,
        [(sv["y_mla"], 512, 0), (sv["y_fox"], 512, 0), (sv["y_s5"], 512, 0), (proj, *_seg("g_mla")), (proj, *_seg("g_fox")),
         (proj, *_seg("g_s5")), (proj, *_seg("m_mla")), (proj, *_seg("m_fox")), (proj, *_seg("m_s5")), (dout, D_MODEL, 0)],
        [w["wo"], w["w_out"]],
        [(512, f32)] * 3 + [(512, bf16)] * 3 + [(D_MODEL, bf16)] * 3, [((1536, D_MODEL), f32), ((D_MODEL, D_MODEL), f32)])
    grads["w_branch_out"], grads["w_out"] = dwo, dwout

    u_w, u_blk = _seg("s5u")

    def s5_post_bwd_body(yl_ref, u_ref, do_ref, d_ref, wg_ref, bg_ref, dyl_ref, dus_ref, dd_ref, dwg_ref, dbg_ref):
        u = u_ref[...]
        y = yl_ref[...] + d_ref[...] * u
        z, t = _gelu(y)
        s = _sigmoid(_nn(z, wg_ref[...]) + bg_ref[...])
        do = do_ref[...]
        dgl = do * z * s * (1.0 - s)
        dz = do * s + _nt(dgl, wg_ref[...])
        dy = dz * _gelu_grad(y, t)
        dyl_ref[...] = dy.astype(bf16)
        dus_ref[...] = dy * d_ref[...]
        _accumulate(dd_ref, jnp.sum(dy * u, axis=0, keepdims=True))
        _accumulate(dwg_ref, _tn(z, dgl))
        _accumulate(dbg_ref, jnp.sum(dgl, axis=0, keepdims=True))

    dylin, du_skip, dd, dwglu, dbglu = _rowwise(
        "s5_post_bwd", s5_post_bwd_body, n_tok, [(sv["ylin"], 512, 0), (proj, u_w, u_blk), (dy_s5, 512, 0)],
        [w["s5_d"], w["w_glu"], w["b_glu"]], [(512, bf16), (512, f32)], [((1, 512), f32), ((512, 512), f32), ((1, 512), f32)])
    grads["s5_d"], grads["s5_w_glu"], grads["s5_b_glu"] = dd.reshape(512), dwglu, dbglu.reshape(512)

    dylin = _to_streams(dylin)
    ds5u, da, dc_nc, db_cn = _s5_scan("s5_scan_bwd", dylin, w["c_cn"], w["b_nc"], _to_streams(du_skip), bf16, w["a_re8"],
                                      w["a_im8_neg"], reverse=True, n_tok=n_tok, grads_of=(sv["xs"], sv["u_streams"]))
    ds5u = _from_streams(ds5u)
    diag_b = lambda m: jnp.swapaxes(_bd8_diag(m, S5_GROUP, S5_STATE), 1, 2).reshape(S5_LANES, S5_GROUP)
    diag_c = lambda m: jnp.swapaxes(_bd8_diag(m, S5_STATE, S5_GROUP), 1, 2)
    dlr, dli, dldt, db_re, db_im = _s5_param_bwd(
        w["lr"], w["li"], w["ldt"], w["b_re"], w["b_im"], da[:, 0, :].reshape(S5_LANES, 1), da[:, 1, :].reshape(S5_LANES, 1),
        diag_b(db_cn[0]), diag_b(db_cn[1]))
    grads["s5_lambda_re"] = dlr.reshape(S5_GROUPS, S5_STATE)
    grads["s5_lambda_im"] = dli.reshape(S5_GROUPS, S5_STATE)
    grads["s5_log_dt"] = dldt.reshape(S5_GROUPS)
    grads["s5_b_re"] = db_re.reshape(S5_GROUPS, S5_STATE, S5_GROUP)
    grads["s5_b_im"] = db_im.reshape(S5_GROUPS, S5_STATE, S5_GROUP)
    grads["s5_c_re"] = diag_c(dc_nc[0])
    grads["s5_c_im"] = -diag_c(dc_nc[1])

    dfqn, dfkn, dfv, dck, dcq = _attn_bwd("fox_attn_bwd", sv["fqn"], sv["fkn"], sv["fvb"], sv["y_fox"], sv["lse_fox"], dy_fox,
                                          sv["cum_b"], mla=False, n_tok=n_tok)
    dcq = jnp.pad(dcq[:, :2, :].reshape(HEADS, n_tok).T, ((0, 0), (0, LANE - HEADS)))

    def fox_gate_bwd_body(dk_ref, dq_ref, ff_ref, bf_ref, dff_ref, dbf_ref):
        xg = dk_ref[...] + dq_ref[...]
        t = lax.broadcasted_iota(jnp.int32, xg.shape, 0)
        s = 1
        while s < n_tok:
            xg = xg + jnp.where(t < n_tok - s, pltpu.roll(xg, n_tok - s, 0), 0.0)
            s *= 2
        dff = xg * _sigmoid(-(ff_ref[...] + bf_ref[...]))
        dff_ref[...] = dff.astype(bf16)
        dbf_ref[...] = jnp.sum(dff, axis=0, keepdims=True)

    ff_w, ff_blk = _seg("ff")
    dff, dbf = pl.pallas_call(
        fox_gate_bwd_body, grid=(1,),
        in_specs=[pl.BlockSpec((n_tok, LANE), lambda i: (0, 0)), pl.BlockSpec((n_tok, LANE), lambda i: (0, 0)),
                  pl.BlockSpec((n_tok, ff_w), lambda i: (0, ff_blk)), pl.BlockSpec((1, LANE), lambda i: (0, 0))],
        out_specs=[pl.BlockSpec((n_tok, LANE), lambda i: (0, 0)), pl.BlockSpec((1, LANE), lambda i: (0, 0))],
        out_shape=[jax.ShapeDtypeStruct((n_tok, LANE), bf16), jax.ShapeDtypeStruct((1, LANE), f32)], name="fox_gate_bwd",
    )(dck, dcq, proj, w["b_f"])
    grads["fox_b_f"] = dbf[0, :HEADS]

    def fox_prep_bwd_body(fq_ref, fk_ref, dqn_ref, dkn_ref, dv_ref, qg_ref, kg_ref, dfq_ref, dfk_ref, dfv_ref, dqg_ref, dkg_ref):
        lane = lax.broadcasted_iota(jnp.int32, (TOK, LANE), 1)
        dqg = jnp.zeros((1, LANE), f32)
        dkg = jnp.zeros((1, LANE), f32)
        for blk in range(4):
            sl = slice(blk * LANE, (blk + 1) * LANE)
            xq = fq_ref[:, sl]
            dx, dg = _fox_halves_bwd(dqn_ref[:, sl], xq, _fox_halves(xq, lane), qg_ref[...], lane)
            dfq_ref[:, sl] = dx.astype(bf16)
            dqg = dqg + dg
            xk = fk_ref[:, sl]
            dx, dg = _fox_halves_bwd(dkn_ref[:, sl], xk, _fox_halves(xk, lane), kg_ref[...], lane)
            dfk_ref[:, sl] = dx.astype(bf16)
            dkg = dkg + dg
        dfv_ref[...] = dv_ref[...].astype(bf16)
        _accumulate(dqg_ref, dqg + pltpu.roll(dqg, 64, 1))
        _accumulate(dkg_ref, dkg + pltpu.roll(dkg, 64, 1))

    dfq, dfk, dfvb, dfqg, dfkg = _rowwise(
        "fox_prep_bwd", fox_prep_bwd_body, n_tok,
        [(proj, *_seg("fq")), (proj, *_seg("fk")), (dfqn, 512, 0), (dfkn, 512, 0), (dfv, 512, 0)],
        [w["fq_g"], w["fk_g"]], [(512, bf16)] * 3, [((1, LANE), f32)] * 2)
    grads["fox_q_norm"], grads["fox_k_norm"] = dfqg[0, :FOX_DIM], dfkg[0, :FOX_DIM]

    dqn, dkn, dv_mla = _attn_bwd("mla_attn_bwd", sv["qn"], sv["kn"], sv["v_mla"], sv["y_mla"], sv["lse_mla"], dy_mla,
                                 None, mla=True, n_tok=n_tok)

    def mla_prep_bwd_body(cq_ref, ckv_ref, kpe_ref, c_ref, s1_ref, s2_ref, dqn_ref, dkn_ref, dv_ref,
                          qa_ref, kva_ref, wq_ref, wkv_ref, qn_g_ref, kn_g_ref,
                          dcq_ref, dckv_ref, dkpe_ref, dwq_ref, dwkv_ref, dqa_ref, dkva_ref, dqng_ref, dkng_ref):
        c, s1, s2 = c_ref[...], s1_ref[...], s2_ref[...]
        cq, ckv = cq_ref[...], ckv_ref[...]
        cqn_b, r_cq, ckvn_b, r_ckv, q_raw, kv_raw, kpe_rot = _mla_recompute(
            cq, ckv, kpe_ref[...], c, s1, s2, qa_ref[...], kva_ref[...], wq_ref[...], wkv_ref[...])
        lane = lax.broadcasted_iota(jnp.int32, (TOK, LANE), 1)
        dq_raw, dk_raw = [], []
        dkpe_rot = jnp.zeros((TOK, LANE), f32)
        dqng = jnp.zeros((1, LANE), f32)
        dkng = jnp.zeros((1, LANE), f32)
        for hd in range(HEADS):
            sl = slice(hd * LANE, (hd + 1) * LANE)
            q_rot = _rope(q_raw[:, sl], c, s1, s2)
            r = lax.rsqrt(jnp.sum(q_rot * q_rot, axis=-1, keepdims=True) * (1.0 / MLA_QK) + EPS)
            dx, dg = _rms_bwd(dqn_ref[:, sl], q_rot, r, qn_g_ref[...], MLA_QK)
            dqng = dqng + dg
            dq_raw.append(_rope_t(dx, c, s1, s2))
            k_full = kv_raw[:, sl] + kpe_rot
            r = lax.rsqrt(jnp.sum(k_full * k_full, axis=-1, keepdims=True) * (1.0 / MLA_QK) + EPS)
            dx, dg = _rms_bwd(dkn_ref[:, sl], k_full, r, kn_g_ref[...], MLA_QK)
            dkng = dkng + dg
            dk_raw.append(jnp.where(lane < 64, dx, 0.0))
            dkpe_rot = dkpe_rot + dx
        dkpe = _rope_t(dkpe_rot, c, s1, s2)
        dkpe_ref[...] = jnp.where(jnp.logical_and(lane >= 64, lane < 64 + ROPE), dkpe, 0.0).astype(bf16)
        dq_raw = jnp.concatenate(dq_raw, axis=1).astype(bf16)
        dkv_raw = jnp.concatenate(dk_raw + [dv_ref[...]], axis=1).astype(bf16)
        dcqn = _nt(dq_raw, wq_ref[...])
        dckvn = _nt(dkv_raw, wkv_ref[...])
        dx, dg = _rms_bwd(dcqn, cq, r_cq, qa_ref[...], MLA_Q_RANK)
        dcq_ref[...] = dx.astype(bf16)
        _accumulate(dqa_ref, dg)
        dx, dg = _rms_bwd(dckvn, ckv, r_ckv, kva_ref[...], MLA_KV_RANK)
        dckv_ref[...] = dx.astype(bf16)
        _accumulate(dkva_ref, dg)
        _accumulate(dwq_ref, _tn(cqn_b, dq_raw))
        _accumulate(dwkv_ref, _tn(ckvn_b, dkv_raw))
        _accumulate(dqng_ref, dqng)
        _accumulate(dkng_ref, dkng)

    dcq, dckv, dkpe, dwq, dwkv, dqa, dkva, dqng, dkng = _rowwise(
        "mla_prep_bwd", mla_prep_bwd_body, n_tok,
        [(proj, *_seg("cq")), (proj, *_seg("ckv")), (proj, *_seg("kpe")), (c_tab, LANE, 0), (s1_tab, LANE, 0), (s2_tab, LANE, 0),
         (dqn, HEADS * LANE, 0), (dkn, HEADS * LANE, 0), (dv_mla, 512, 0)],
        [w["qa_g"], w["kva_g"], w["wq"], w["wkv"], w["qn_g"], w["kn_g"]],
        [(MLA_Q_RANK, bf16), (LANE, bf16), (LANE, bf16)],
        [((MLA_Q_RANK, HEADS * LANE), f32), ((MLA_KV_RANK, HEADS * LANE + 512), f32), ((1, MLA_Q_RANK), f32),
         ((1, MLA_KV_RANK), f32), ((1, LANE), f32), ((1, LANE), f32)])
    grads["mla_w_q_up"] = dwq.reshape(MLA_Q_RANK, HEADS, LANE)[:, :, :MLA_QK].reshape(MLA_Q_RANK, HEADS * MLA_QK)
    dwk = dwkv[:, :HEADS * LANE].reshape(MLA_KV_RANK, HEADS, LANE)[:, :, :64]
    dwv = dwkv[:, HEADS * LANE:].reshape(MLA_KV_RANK, HEADS, 64)
    grads["mla_w_kv_up"] = jnp.concatenate([dwk, dwv], axis=2).reshape(MLA_KV_RANK, HEADS * 128)
    grads["mla_q_a_norm"], grads["mla_kv_a_norm"] = dqa.reshape(-1), dkva.reshape(-1)
    grads["mla_q_norm"], grads["mla_k_norm"] = dqng[0, :MLA_QK], dkng[0, :MLA_QK]

    _, _, shard_c, shard_cp = _BIG_SHARD["w_in"]
    kpe0 = _PAD["kpe"][2]
    pieces = [dcq, dckv, dkpe[:, kpe0:kpe0 + ROPE], dfq, dfk, dfvb, dff[:, :HEADS], ds5u, dg_mla, dg_fox, dg_s5,
              dm_mla, dm_fox, dm_s5]
    gap = jnp.zeros((n_tok, shard_cp - shard_c), bf16)
    cut, pos = [], 0
    for p in pieces:
        start = 0
        while start < p.shape[1]:
            take = min(p.shape[1] - start, shard_c - pos % shard_c)
            cut.append(p[:, start:start + take])
            start, pos = start + take, pos + take
            if pos % shard_c == 0:
                cut.append(gap)
    dproj = jnp.concatenate(cut, axis=1)
    ct = 256
    per = shard_cp // ct
    dh = _mm("in_proj_dgrad", dproj, w["w_in_shards"], mode="nt", grid=(1, 1, N_CHIPS * per),
             a_spec=pl.BlockSpec((n_tok, ct), lambda i, j, kk: (0, kk)),
             b_spec=pl.BlockSpec((None, D_MODEL, ct), lambda i, j, kk: (kk // per, 0, kk % per)),
             o_spec=pl.BlockSpec((n_tok, D_MODEL), lambda i, j, kk: (0, 0)),
             out_shape=jax.ShapeDtypeStruct((n_tok, D_MODEL), f32), acc_shape=(n_tok, D_MODEL))
    grads["w_in"] = _mm("in_proj_wgrad", sv["h"], dproj, mode="tn", grid=(1, N_CHIPS * per, 1),
                        a_spec=pl.BlockSpec((n_tok, D_MODEL), lambda i, j, kk: (0, 0)),
                        b_spec=pl.BlockSpec((n_tok, ct), lambda i, j, kk: (0, j)),
                        o_spec=pl.BlockSpec((None, D_MODEL, ct), lambda i, j, kk: (j // per, 0, j % per)),
                        out_shape=jax.ShapeDtypeStruct((N_CHIPS, D_MODEL, shard_cp), f32), acc_shape=(D_MODEL, ct))

    def norm_bwd_body(dh_ref, x_ref, do_ref, g_ref, dx_ref, dg_ref):
        xv = x_ref[...]
        r = lax.rsqrt(jnp.sum(xv * xv, axis=-1, keepdims=True) * (1.0 / D_MODEL) + EPS)
        dx, dg = _rms_bwd(dh_ref[...], xv, r, g_ref[...], D_MODEL)
        dx_ref[...] = do_ref[...] + dx
        _accumulate(dg_ref, dg)

    dx, dng = _rowwise("norm_bwd", norm_bwd_body, n_tok, [(dh, D_MODEL, 0), (x, D_MODEL, 0), (dout, D_MODEL, 0)],
                       [w["norm_g"]], [(D_MODEL, f32)], [((1, D_MODEL), f32)])
    grads["norm_g"] = dng.reshape(D_MODEL)
    return dx, grads


def _rope_tables(positions):
    inv = 1.0 / (ROPE_THETA ** (jnp.arange(0, ROPE, 2, dtype=f32) / ROPE))
    ang = positions.astype(f32).reshape(-1, 1) * inv
    cos, sin = jnp.cos(ang), jnp.sin(ang)
    n = ang.shape[0]
    z16, z32, z64 = jnp.zeros((n, 16), f32), jnp.zeros((n, 32), f32), jnp.zeros((n, 64), f32)
    c = jnp.concatenate([jnp.ones((n, 64), f32), cos, cos, z32], axis=1)
    s1 = jnp.concatenate([z64, -sin, z16, z32], axis=1)
    s2 = jnp.concatenate([z64, z16, sin, z32], axis=1)
    return c, s1, s2


BIG = ("w_in", "mla_w_q_up", "mla_w_kv_up", "s5_w_glu", "w_branch_out", "w_out")
SMALL = ("norm_g", "mla_q_a_norm", "mla_kv_a_norm", "mla_q_norm", "mla_k_norm", "fox_b_f", "fox_q_norm", "fox_k_norm",
         "s5_lambda_re", "s5_lambda_im", "s5_log_dt", "s5_b_re", "s5_b_im", "s5_c_re", "s5_c_im", "s5_d", "s5_b_glu")
WEIGHTS = ("norm_g", "w_in", "mla_q_a_norm", "mla_w_q_up", "mla_kv_a_norm", "mla_w_kv_up", "mla_q_norm", "mla_k_norm",
           "fox_b_f", "fox_q_norm", "fox_k_norm", "s5_lambda_re", "s5_lambda_im", "s5_log_dt", "s5_b_re", "s5_b_im",
           "s5_c_re", "s5_c_im", "s5_d", "s5_w_glu", "s5_b_glu", "w_branch_out", "w_out")


def _local_step(x, positions, loss_target, small, big):
    n_tok = x.shape[0]
    tabs = _rope_tables(positions)
    ws, saves = [], []
    hcur = x
    stacked = _prep_weights(small, big)
    for l in range(DEPTH):
        w = {k: v[l] for k, v in stacked.items()}
        hcur, sv = _layer_fwd(hcur, w, tabs, n_tok)
        ws.append(w)
        saves.append(sv)

    def loss_body(y_ref, t_ref, d_ref, l_ref):
        err = y_ref[...] - t_ref[...]
        d_ref[...] = err * (1.0 / D_MODEL)
        tot = jnp.sum(jnp.sum(err * err, axis=-1, keepdims=True), axis=0, keepdims=True)
        _accumulate(l_ref, jnp.broadcast_to(tot * (0.5 / D_MODEL), (1, LANE)))

    dcur, loss = _rowwise("loss", loss_body, n_tok, [(hcur, D_MODEL, 0), (loss_target, D_MODEL, 0)], [], [(D_MODEL, f32)],
                          [((1, LANE), f32)])
    layer_grads = [None] * DEPTH
    for l in reversed(range(DEPTH)):
        dcur, layer_grads[l] = _layer_bwd(dcur, ws[l], saves[l], tabs, n_tok)
    grads = {n: jnp.stack([layer_grads[l][n] for l in range(DEPTH)]) for n in WEIGHTS}
    return loss[0, 0], dcur, grads


N_DEV = 8
_ANY = pl.BlockSpec(memory_space=pl.ANY)
_MESH = pl.DeviceIdType.MESH


def _all_gather8(name, blk):
    m = blk.shape[0]

    def body(x_ref, out_ref, send_sems, recv_sems, local_sem):
        x, y, c = lax.axis_index("x"), lax.axis_index("y"), lax.axis_index("c")
        me, sibling = (x, y, c), (x, y, 1 - c)
        chips = [(1 - x, y), (x, 1 - y), (1 - x, 1 - y)]

        def slot(px, py, pc):
            return out_ref.at[4 * px + 2 * py + pc]

        def copy(k, block, to, src=None):
            return pltpu.make_async_remote_copy(
                src_ref=slot(*block) if src is None else src, dst_ref=slot(*block),
                send_sem=send_sems.at[k], recv_sem=recv_sems.at[k], device_id=to, device_id_type=_MESH)

        mine = pltpu.make_async_copy(x_ref, slot(*me), local_sem)
        mine.start()
        first = [copy(0, me, sibling, src=x_ref)]
        first += [copy(1 + j, me, (*chip, c), src=x_ref) for j, chip in enumerate(chips)]
        for cp in first:
            cp.start()
        passed = [copy(4 + j, (*chip, c), sibling) for j, chip in enumerate(chips)]
        for j, chip in enumerate(chips):
            copy(1 + j, (*chip, c), me).wait_recv()
            passed[j].start()
        copy(0, sibling, me).wait_recv()
        for j, chip in enumerate(chips):
            copy(4 + j, (*chip, 1 - c), me).wait_recv()
        for cp in first + passed:
            cp.wait_send()
        mine.wait()

    return pl.pallas_call(
        body, out_shape=jax.ShapeDtypeStruct((N_DEV, m, LANE), blk.dtype), in_specs=[_ANY], out_specs=_ANY, name=name,
        scratch_shapes=[pltpu.SemaphoreType.DMA((7,)), pltpu.SemaphoreType.DMA((7,)), pltpu.SemaphoreType.DMA],
    )(blk)


def _gather_layers(name, shards):
    n = len(shards)

    def body(*refs):
        x_refs, out_refs = refs[:n], refs[n:2 * n]
        send_sems, recv_sems, local_sems = refs[2 * n:]
        x, y, c = lax.axis_index("x"), lax.axis_index("y"), lax.axis_index("c")
        me, sibling = (x, y, c), (x, y, 1 - c)
        xn, yn, dg = (1 - x, y, c), (x, 1 - y, c), (1 - x, 1 - y, c)
        relay_from = (x + (1 - c) * (1 - 2 * x), y + c * (1 - 2 * y), c)
        relay_to = (x + c * (1 - 2 * x), y + (1 - c) * (1 - 2 * y), c)

        def copy(w, k, block, to, src=None):
            px, py, pc = block
            slot = out_refs[w].at[pc, 2 * px + py]
            return pltpu.make_async_remote_copy(
                src_ref=slot if src is None else src, dst_ref=slot, send_sem=send_sems.at[7 * w + k],
                recv_sem=recv_sems.at[7 * w + k], device_id=to, device_id_type=_MESH)

        started, local = [], []
        for w in range(n):
            src = x_refs[w].at[c]
            mine = pltpu.make_async_copy(src, out_refs[w].at[c, 2 * x + y], local_sems.at[w])
            mine.start()
            local.append(mine)
            first = [copy(w, 0, me, sibling, src=src), copy(w, 1, me, xn, src=src), copy(w, 2, me, yn, src=src)]
            for cp in first:
                cp.start()
            started += first
        for w in range(n):
            copy(w, 1, xn, me).wait_recv()
            copy(w, 2, yn, me).wait_recv()
            onward = [copy(w, 3, relay_from, relay_to), copy(w, 4, xn, sibling), copy(w, 5, yn, sibling)]
            for cp in onward:
                cp.start()
            started += onward
        for w in range(n):
            copy(w, 3, dg, me).wait_recv()
            onward = copy(w, 6, dg, sibling)
            onward.start()
            started.append(onward)
        for w in range(n):
            copy(w, 0, sibling, me).wait_recv()
            for k, chip in ((4, xn), (5, yn), (6, dg)):
                copy(w, k, (chip[0], chip[1], 1 - c), me).wait_recv()
        for cp in started:
            cp.wait_send()
        for cp in local:
            cp.wait()

    return pl.pallas_call(
        body, out_shape=[jax.ShapeDtypeStruct((2, N_CHIPS) + s.shape[1:], s.dtype) for s in shards],
        in_specs=[_ANY] * n, out_specs=[_ANY] * n, name=name,
        scratch_shapes=[pltpu.SemaphoreType.DMA((7 * n,)), pltpu.SemaphoreType.DMA((7 * n,)), pltpu.SemaphoreType.DMA((n,))],
    )(*shards)


def _swap_layers(name, parts):
    n = len(parts)

    def body(*refs):
        p_refs, got_refs = refs[:n], refs[n:2 * n]
        send_sems, recv_sems = refs[2 * n:]
        x, y, c = lax.axis_index("x"), lax.axis_index("y"), lax.axis_index("c")
        copies = []
        for w in range(n):
            cp = pltpu.make_async_remote_copy(
                src_ref=p_refs[w].at[1 - c], dst_ref=got_refs[w], send_sem=send_sems.at[w], recv_sem=recv_sems.at[w],
                device_id=(x, y, 1 - c), device_id_type=_MESH)
            cp.start()
            copies.append(cp)
        for cp in copies:
            cp.wait()

    return pl.pallas_call(
        body, out_shape=[jax.ShapeDtypeStruct(p.shape[1:], p.dtype) for p in parts], in_specs=[_ANY] * n, out_specs=[_ANY] * n,
        name=name, scratch_shapes=[pltpu.SemaphoreType.DMA((n,)), pltpu.SemaphoreType.DMA((n,))],
    )(*parts)


def _scatter_to_chips(name, parts):
    n = len(parts)

    def body(*refs):
        p_refs, out_refs = refs[:n], refs[n:2 * n]
        send_sems, recv_sems, local_sems = refs[2 * n:]
        x, y, c = lax.axis_index("x"), lax.axis_index("y"), lax.axis_index("c")
        jme = 2 * x + y
        chips = [(1 - x, y), (x, 1 - y), (1 - x, 1 - y)]
        sends, local = [], []
        for w in range(n):
            mine = pltpu.make_async_copy(p_refs[w].at[jme], out_refs[w].at[jme], local_sems.at[w])
            mine.start()
            local.append(mine)
            for k, (tx, ty) in enumerate(chips):
                cp = pltpu.make_async_remote_copy(
                    src_ref=p_refs[w].at[2 * tx + ty], dst_ref=out_refs[w].at[jme], send_sem=send_sems.at[3 * w + k],
                    recv_sem=recv_sems.at[3 * w + k], device_id=(tx, ty, c), device_id_type=_MESH)
                cp.start()
                sends.append(cp)
        for w in range(n):
            for k, (tx, ty) in enumerate(chips):
                pltpu.make_async_remote_copy(
                    src_ref=p_refs[w].at[jme], dst_ref=out_refs[w].at[2 * tx + ty], send_sem=send_sems.at[3 * w + k],
                    recv_sem=recv_sems.at[3 * w + k], device_id=(tx, ty, c), device_id_type=_MESH).wait_recv()
        for cp in sends:
            cp.wait_send()
        for cp in local:
            cp.wait()

    return pl.pallas_call(
        body, out_shape=[jax.ShapeDtypeStruct(p.shape, p.dtype) for p in parts], in_specs=[_ANY] * n, out_specs=[_ANY] * n, name=name,
        scratch_shapes=[pltpu.SemaphoreType.DMA((3 * n,)), pltpu.SemaphoreType.DMA((3 * n,)), pltpu.SemaphoreType.DMA((n,))],
    )(*parts)


def _share_layers(name, bufs):
    n = len(bufs)

    def body(*refs):
        out_refs = refs[n:2 * n]
        send_sems, recv_sems = refs[2 * n:]
        x, y, c = lax.axis_index("x"), lax.axis_index("y"), lax.axis_index("c")
        copies = []
        for w in range(n):
            cp = pltpu.make_async_remote_copy(src_ref=out_refs[w].at[c], dst_ref=out_refs[w].at[c], send_sem=send_sems.at[w],
                                              recv_sem=recv_sems.at[w], device_id=(x, y, 1 - c), device_id_type=_MESH)
            cp.start()
            copies.append(cp)
        for w in range(n):
            pltpu.make_async_remote_copy(src_ref=out_refs[w].at[c], dst_ref=out_refs[w].at[1 - c], send_sem=send_sems.at[w],
                                         recv_sem=recv_sems.at[w], device_id=(x, y, 1 - c), device_id_type=_MESH).wait_recv()
        for cp in copies:
            cp.wait_send()

    return pl.pallas_call(
        body, out_shape=[jax.ShapeDtypeStruct(b.shape, b.dtype) for b in bufs], in_specs=[_ANY] * n, out_specs=[_ANY] * n,
        input_output_aliases={w: w for w in range(n)}, name=name,
        scratch_shapes=[pltpu.SemaphoreType.DMA((n,)), pltpu.SemaphoreType.DMA((n,))],
    )(*bufs)


def _row_tile(rows, cols):
    best = 16
    for t in range(16, rows + 1, 16):
        if rows % t == 0 and t * cols * 4 <= 2 * 1024 * 1024:
            best = t
    return best


def _add_pair(name, core, parts, got, out_dtype):
    _, _, r, c = parts.shape
    t = _row_tile(r, c)

    def body(core_ref, a_ref, b_ref, o_ref):
        o_ref[...] = (a_ref[...] + b_ref[...]).astype(o_ref.dtype)

    spec = pl.BlockSpec((None, t, c), lambda j, i, core_ref: (j, i, 0))
    grid_spec = pltpu.PrefetchScalarGridSpec(
        num_scalar_prefetch=1, grid=(N_CHIPS, r // t),
        in_specs=[pl.BlockSpec((None, None, t, c), lambda j, i, core_ref: (core_ref[0], j, i, 0)), spec], out_specs=spec)
    return pl.pallas_call(body, grid_spec=grid_spec, out_shape=jax.ShapeDtypeStruct(got.shape, out_dtype), name=name,
                          compiler_params=pltpu.CompilerParams(dimension_semantics=("arbitrary", "arbitrary")))(core, parts, got)


def _add_four(name, core, a):
    _, r, c = a.shape
    t = _row_tile(r, c)

    def body(core_ref, a0, a1, a2, a3, o_ref):
        o_ref[...] = ((a0[...].astype(f32) + a1[...].astype(f32)) + a2[...].astype(f32)) + a3[...].astype(f32)

    specs = [pl.BlockSpec((None, t, c), functools.partial(lambda i, core_ref, k: (k, i, 0), k=k)) for k in range(N_CHIPS)]
    grid_spec = pltpu.PrefetchScalarGridSpec(
        num_scalar_prefetch=1, grid=(r // t,), in_specs=specs,
        out_specs=pl.BlockSpec((None, t, c), lambda i, core_ref: (core_ref[0], i, 0)))
    return pl.pallas_call(body, grid_spec=grid_spec, out_shape=jax.ShapeDtypeStruct((2, r, c), f32), name=name,
                          compiler_params=pltpu.CompilerParams(dimension_semantics=("arbitrary",)))(core, a, a, a, a)


def _adamw(name, w, g, m, v, row_tile=None, lead_tile=None):
    c1 = 1.0 - ADAM_B1 ** ADAM_STEP
    c2 = 1.0 - ADAM_B2 ** ADAM_STEP

    def body(w_ref, g_ref, m_ref, v_ref, d_ref, nm_ref, nv_ref):
        gv = g_ref[...]
        nm = ADAM_B1 * m_ref[...] + (1.0 - ADAM_B1) * gv
        nv = ADAM_B2 * v_ref[...] + (1.0 - ADAM_B2) * (gv * gv)
        m_hat = nm / c1
        v_hat = nv / c2
        d_ref[...] = -ADAM_LR * (m_hat / (jnp.sqrt(v_hat) + ADAM_EPS) + ADAM_WD * w_ref[...])
        nm_ref[...] = nm
        nv_ref[...] = nv

    sds = jax.ShapeDtypeStruct(w.shape, f32)
    if lead_tile is not None:
        spec = pl.BlockSpec((lead_tile,) + w.shape[1:], lambda i: (i, 0, 0))
        return pl.pallas_call(body, grid=(w.shape[0] // lead_tile,), in_specs=[spec] * 4, out_specs=[spec] * 3, out_shape=[sds] * 3,
                              name=name, compiler_params=pltpu.CompilerParams(dimension_semantics=("arbitrary",), vmem_limit_bytes=VMEM_LIMIT),
                              )(w, g, m, v)
    if row_tile is None:
        return pl.pallas_call(body, out_shape=[sds] * 3, name=name)(w, g, m, v)
    _, r, c = w.shape
    spec = pl.BlockSpec((None, row_tile, c), lambda l, i: (l, i, 0))
    return pl.pallas_call(body, grid=(DEPTH, r // row_tile), in_specs=[spec] * 4, out_specs=[spec] * 3, out_shape=[sds] * 3, name=name,
                          compiler_params=pltpu.CompilerParams(dimension_semantics=("arbitrary", "arbitrary"), vmem_limit_bytes=VMEM_LIMIT),
                          )(w, g, m, v)


def _pad_rows(flat, rows):
    return jnp.pad(flat, (0, rows * LANE - flat.shape[0])).reshape(rows, LANE)


def kernel(x, positions, norm_g, w_in, mla_q_a_norm, mla_w_q_up, mla_kv_a_norm, mla_w_kv_up, mla_q_norm, mla_k_norm, fox_b_f, fox_q_norm, fox_k_norm, s5_lambda_re, s5_lambda_im, s5_log_dt, s5_b_re, s5_b_im, s5_c_re, s5_c_im, s5_d, s5_w_glu, s5_b_glu, w_branch_out, w_out, loss_target, m_norm_g, m_w_in, m_mla_q_a_norm, m_mla_w_q_up, m_mla_kv_a_norm, m_mla_w_kv_up, m_mla_q_norm, m_mla_k_norm, m_fox_b_f, m_fox_q_norm, m_fox_k_norm, m_s5_lambda_re, m_s5_lambda_im, m_s5_log_dt, m_s5_b_re, m_s5_b_im, m_s5_c_re, m_s5_c_im, m_s5_d, m_s5_w_glu, m_s5_b_glu, m_w_branch_out, m_w_out, v_norm_g, v_w_in, v_mla_q_a_norm, v_mla_w_q_up, v_mla_kv_a_norm, v_mla_w_kv_up, v_mla_q_norm, v_mla_k_norm, v_fox_b_f, v_fox_q_norm, v_fox_k_norm, v_s5_lambda_re, v_s5_lambda_im, v_s5_log_dt, v_s5_b_re, v_s5_b_im, v_s5_c_re, v_s5_c_im, v_s5_d, v_s5_w_glu, v_s5_b_glu, v_w_branch_out, v_w_out):
    given = dict(locals())
    wts = {n: given[n] for n in WEIGHTS}
    mom1 = {n: given["m_" + n] for n in WEIGHTS}
    mom2 = {n: given["v_" + n] for n in WEIGHTS}

    def lanes(n, a):
        _, _, c, cp = _BIG_SHARD[n]
        return jnp.pad(a, ((0, 0), (0, 0), (0, cp - c)))

    gathered = _gather_layers("gather_weights", [lanes(n, wts[n].astype(bf16)) for n in BIG])
    big = dict(zip(BIG, gathered))
    small = {n: wts[n] for n in SMALL}

    loss_local, grad_x, grads = _local_step(x[0], positions, loss_target[0], small, big)
    loss = lax.psum(loss_local, ("x", "y", "c"))

    small_flat = jnp.concatenate([grads[n].reshape(-1) for n in SMALL])
    small_rows = -(-small_flat.shape[0] // (N_DEV * 16 * LANE)) * 16
    parts = [grads[n] if n == "w_in" else jnp.stack([_to_shards(n, grads[n][l]) for l in range(DEPTH)]) for n in BIG]
    parts.append(jnp.swapaxes(_pad_rows(small_flat, N_DEV * small_rows).reshape(N_CHIPS, 2, small_rows, LANE), 0, 1))
    core = lax.axis_index("c")
    core1 = core.reshape(1).astype(jnp.int32)
    got = _swap_layers("grads_to_sibling", parts)
    hop = [bf16] * len(BIG) + [f32]
    pair = [_add_pair("grads_pair_sum_%d" % i, core1, a, b, dt) for i, (a, b, dt) in enumerate(zip(parts, got, hop))]
    landed = _scatter_to_chips("grads_to_chips", pair)
    total = [_add_four("grads_chip_sum_%d" % i, core1, a) for i, a in enumerate(landed)]
    shared = _share_layers("grads_share", total[:-1])
    small_mine = lax.dynamic_index_in_dim(total[-1], core, 0, keepdims=False)
    small_all = _all_gather8("gather_small_grads", small_mine).reshape(-1)

    g_out = {n: s[:, :, :_BIG_SHARD[n][2]] for n, s in zip(BIG, shared)}
    pos = 0
    for n in SMALL:
        g_out[n] = small_all[pos:pos + wts[n].size].reshape(wts[n].shape)
        pos += wts[n].size

    delta, new_m, new_v = {}, {}, {}
    for n in WEIGHTS:
        if n == "w_in":
            cols_first = lambda a: jnp.transpose(a, (2, 0, 1))
            res = _adamw("adamw_" + n, *[cols_first(a) for a in (wts[n], g_out[n], mom1[n], mom2[n])], lead_tile=177)
            delta[n], new_m[n], new_v[n] = [jnp.transpose(a, (1, 2, 0)) for a in res]
            continue
        row_tile = _row_tile(*wts[n].shape[1:]) if n in BIG else None
        delta[n], new_m[n], new_v[n] = _adamw("adamw_" + n, wts[n], g_out[n], mom1[n], mom2[n], row_tile)

    return (loss, grad_x[None], *[g_out[n] for n in WEIGHTS], *[delta[n] for n in WEIGHTS],
            *[new_m[n] for n in WEIGHTS], *[new_v[n] for n in WEIGHTS])
```

```python
import functools
import math

import jax
import jax.numpy as jnp
from jax import lax
from jax.experimental import pallas as pl
from jax.experimental.pallas import tpu as pltpu

f32 = jnp.float32
bf16 = jnp.bfloat16

D_MODEL = 1024
DEPTH = 2
EPS = 1e-6
HEADS = 8
MLA_QK = 96
MLA_Q_RANK = 256
MLA_KV_RANK = 128
ROPE = 32
ROPE_THETA = 10000.0
FOX_DIM = 64
S5_GROUPS = 32
S5_GROUP = 16
S5_STATE = 64
S5_LANES = S5_GROUPS * S5_STATE
LANE = 128
S5_BLOCKS = S5_LANES // LANE
TOK = 256
VMEM_LIMIT = 56 * 1024 * 1024

ADAM_LR = 0.001
ADAM_B1 = 0.9
ADAM_B2 = 0.999
ADAM_EPS = 1e-08
ADAM_WD = 0.01
ADAM_STEP = 10

_ORIG = {}
_off = 0
for _n, _w in (("cq", 256), ("ckv", 128), ("kpe", 32), ("fq", 512), ("fk", 512), ("fv", 512), ("ff", 8), ("s5u", 512),
               ("g_mla", 512), ("g_fox", 512), ("g_s5", 512), ("m_mla", 1024), ("m_fox", 1024), ("m_s5", 1024)):
    _ORIG[_n] = (_off, _w)
    _off += _w
_PAD = {"m_mla": (0, 1024, 0), "m_fox": (1024, 1024, 0), "m_s5": (2048, 1024, 0),
        "fq": (3072, 512, 0), "fk": (3584, 512, 0), "fv": (4096, 512, 0), "s5u": (4608, 512, 0),
        "g_mla": (5120, 512, 0), "g_fox": (5632, 512, 0), "g_s5": (6144, 512, 0),
        "cq": (6656, 256, 0), "ckv": (6912, 128, 0), "kpe": (7040, 128, 64), "ff": (7168, 128, 0)}
NP = 7680
_PAD_ORDER = ("m_mla", "m_fox", "m_s5", "fq", "fk", "fv", "s5u", "g_mla", "g_fox", "g_s5", "cq", "ckv", "kpe", "ff")


def _seg(name):
    start, width, _ = _PAD[name]
    return width, start // width


def _nn(a, b):
    return lax.dot_general(a.astype(bf16), b.astype(bf16), (((1,), (0,)), ((), ())), preferred_element_type=f32)


def _nt(a, b):
    return lax.dot_general(a.astype(bf16), b.astype(bf16), (((1,), (1,)), ((), ())), preferred_element_type=f32)


def _tn(a, b):
    return lax.dot_general(a.astype(bf16), b.astype(bf16), (((0,), (0,)), ((), ())), preferred_element_type=f32)


def _rms(x, g, n):
    r = lax.rsqrt(jnp.sum(x * x, axis=-1, keepdims=True) * (1.0 / n) + EPS)
    return x * r * g, r


def _rms_bwd(dy, x, r, g, n):
    xh = x * r
    dg = jnp.sum(dy * xh, axis=0, keepdims=True)
    dxh = dy * g
    dx = r * (dxh - xh * (jnp.sum(dxh * xh, axis=-1, keepdims=True) * (1.0 / n)))
    return dx, dg


def _sigmoid(x):
    return 1.0 / (1.0 + jnp.exp(-x))


_GELU_C = math.sqrt(2.0 / math.pi)


def _gelu(x):
    t = jnp.tanh(_GELU_C * (x + 0.044715 * x * x * x))
    return 0.5 * x * (1.0 + t), t


def _gelu_grad(x, t):
    return 0.5 * (1.0 + t) + 0.5 * x * (1.0 - t * t) * _GELU_C * (1.0 + 3.0 * 0.044715 * x * x)


def _accumulate(ref, val):
    i = pl.program_id(0)

    @pl.when(i == 0)
    def _():
        ref[...] = val

    @pl.when(i > 0)
    def _():
        ref[...] += val


def _rope(x, c, s1, s2):
    return x * c + pltpu.roll(x, LANE - 16, 1) * s1 + pltpu.roll(x, 16, 1) * s2


def _rope_t(d, c, s1, s2):
    return d * c + pltpu.roll(d * s1, 16, 1) + pltpu.roll(d * s2, LANE - 16, 1)


def _const_map(ndim):
    return lambda *_: (0,) * ndim


def _rowwise(name, body, n_tok, tiled_in, full_in, tiled_out, acc_out, tile=TOK):
    in_specs, args = [], []
    for arr, width, blk in tiled_in:
        in_specs.append(pl.BlockSpec((tile, width), functools.partial(lambda i, b: (i, b), b=blk)))
        args.append(arr)
    for arr in full_in:
        in_specs.append(pl.BlockSpec(arr.shape, _const_map(arr.ndim)))
        args.append(arr)
    out_specs, out_shape = [], []
    for width, dt in tiled_out:
        out_specs.append(pl.BlockSpec((tile, width), lambda i: (i, 0)))
        out_shape.append(jax.ShapeDtypeStruct((n_tok, width), dt))
    for shape, dt in acc_out:
        out_specs.append(pl.BlockSpec(shape, _const_map(len(shape))))
        out_shape.append(jax.ShapeDtypeStruct(shape, dt))
    return pl.pallas_call(
        body, grid=(n_tok // tile,), in_specs=in_specs, out_specs=out_specs, out_shape=out_shape, name=name,
        compiler_params=pltpu.CompilerParams(dimension_semantics=("arbitrary",), vmem_limit_bytes=VMEM_LIMIT),
    )(*args)


def _mm(name, a, b, *, mode, grid, a_spec, b_spec, o_spec, out_shape, acc_shape, add=None, add_spec=None):
    nk = grid[2]

    def body(*refs):
        if add is None:
            a_ref, b_ref, o_ref, acc_ref = refs
        else:
            a_ref, b_ref, add_ref, o_ref, acc_ref = refs
        k = pl.program_id(2)

        @pl.when(k == 0)
        def _():
            acc_ref[...] = jnp.zeros_like(acc_ref)

        acc_ref[...] += {"nn": _nn, "nt": _nt, "tn": _tn}[mode](a_ref[...], b_ref[...])

        @pl.when(k == nk - 1)
        def _():
            r = acc_ref[...]
            if add is not None:
                r = r + add_ref[...]
            o_ref[...] = r.astype(o_ref.dtype)

    in_specs = [a_spec, b_spec] + ([add_spec] if add is not None else [])
    args = (a, b) + ((add,) if add is not None else ())
    return pl.pallas_call(
        body, grid=grid, in_specs=in_specs, out_specs=o_spec, out_shape=out_shape, name=name,
        scratch_shapes=[pltpu.VMEM(acc_shape, f32)],
        compiler_params=pltpu.CompilerParams(dimension_semantics=("arbitrary", "arbitrary", "arbitrary"), vmem_limit_bytes=VMEM_LIMIT),
    )(*args)


def _mm_nn(name, a, b, *, m, n, k, tm, tn, tk, out_dtype=f32, a_koff=0):
    return _mm(name, a, b, mode="nn", grid=(m // tm, n // tn, k // tk),
               a_spec=pl.BlockSpec((tm, tk), lambda i, j, kk: (i, kk + a_koff)),
               b_spec=pl.BlockSpec((tk, tn), lambda i, j, kk: (kk, j)),
               o_spec=pl.BlockSpec((tm, tn), lambda i, j, kk: (i, j)),
               out_shape=jax.ShapeDtypeStruct((m, n), out_dtype), acc_shape=(tm, tn))


ATT_KV = 256
ATT_Q = 1024


def _attn_common(mla, n_tok):
    qw = 2 * LANE if mla else LANE
    scale = 1.0 / math.sqrt(MLA_QK if mla else FOX_DIM)
    return qw, scale, min(ATT_Q, n_tok)


def _attn_heads(q_ref, mla):
    out = []
    if mla:
        for e in (0, 1):
            qe = q_ref[:, e * LANE:(e + 1) * LANE]
            out.append((qe.astype(f32).T.astype(bf16), qe))
        return out
    q = q_ref[...]
    tq = q.shape[0]
    qt = q.astype(f32).T
    row = lax.broadcasted_iota(jnp.int32, (LANE, tq), 0)
    lane = lax.broadcasted_iota(jnp.int32, (tq, LANE), 1)
    for e in (0, 1):
        out.append((jnp.where((row >= 64) == bool(e), qt, 0.0).astype(bf16),
                    jnp.where((lane >= 64) == bool(e), q, jnp.zeros((), bf16))))
    return out


def _attn_allowed(off, i, tq, mla):
    kpos = off + lax.broadcasted_iota(jnp.int32, (ATT_KV, tq), 0)
    qpos = i * tq + lax.broadcasted_iota(jnp.int32, (ATT_KV, tq), 1)
    return ((kpos // 64) <= (qpos // 64)) if mla else (kpos <= qpos)


def _attn_fwd(name, q, k, v, cum_b, *, mla, n_tok):
    qw, scale, tq = _attn_common(mla, n_tok)
    nq = n_tok // tq
    nkv = n_tok // ATT_KV
    has_bias = cum_b is not None

    def body(*refs):
        if has_bias:
            q_ref, k_ref, v_ref, cb_ref, o_ref, lse_ref, vt_ref = refs
        else:
            q_ref, k_ref, v_ref, o_ref, lse_ref, vt_ref = refs
        i = pl.program_id(1)

        @pl.when(i == 0)
        def _():
            for jb in range(nkv):
                vt_ref[jb] = v_ref[jb * ATT_KV:(jb + 1) * ATT_KV, :].astype(f32).T.astype(bf16)

        heads = _attn_heads(q_ref, mla)

        def step(j, carry, masked, q_lo=0):
            off = pl.multiple_of(j * ATT_KV, ATT_KV)
            allowed = _attn_allowed(off, i, tq, mla)[:, q_lo:] if masked else None
            keep = lambda old, part: part if q_lo == 0 else jnp.concatenate([old[:, :q_lo], part], axis=1)
            vt = vt_ref[j]
            sts = []
            for e in (0, 1):
                kb = k_ref[pl.ds(off, ATT_KV), e * LANE:(e + 1) * LANE] if mla else k_ref[pl.ds(off, ATT_KV), :]
                sts.append(_nn(kb, heads[e][0][:, q_lo:]))
            stats = []
            for e in (0, 1):
                m, l = carry[e][0][:, q_lo:], carry[e][1][:, q_lo:]
                st = sts[e] * scale
                if has_bias:
                    st = st - jnp.tile(cb_ref[e, pl.ds(off, ATT_KV), :], (1, (tq - q_lo) // LANE))
                if masked:
                    st = jnp.where(allowed, st, -1e30)
                m_new = jnp.maximum(m, jnp.max(st, axis=0, keepdims=True))
                alpha = jnp.exp(m - m_new)
                pt = jnp.exp(st - m_new)
                stats.append((m_new, alpha * l + jnp.sum(pt, axis=0, keepdims=True), alpha, pt.astype(bf16)))
            new = []
            for e in (0, 1):
                m_new, l, alpha, pt = stats[e]
                acc = alpha * carry[e][2][:, q_lo:] + _nn(vt[64 * e:64 * e + 64, :], pt)
                new.append((keep(carry[e][0], m_new), keep(carry[e][1], l), keep(carry[e][2], acc)))
            return tuple(new)

        init = tuple((jnp.full((1, tq), -1e30, f32), jnp.zeros((1, tq), f32), jnp.zeros((64, tq), f32)) for _ in (0, 1))
        n_full = i * (tq // ATT_KV)
        carry = lax.fori_loop(0, n_full, functools.partial(step, masked=False), init)
        for d in range(tq // ATT_KV):
            carry = step(n_full + d, carry, True, q_lo=d * ATT_KV)
        o_ref[...] = jnp.concatenate([carry[e][2] / carry[e][1] for e in (0, 1)], axis=0).T
        lse_ref[...] = jnp.zeros_like(lse_ref)
        for e in (0, 1):
            lse_ref[e:e + 1, :] = carry[e][0] + jnp.log(carry[e][1])

    in_specs = [pl.BlockSpec((tq, qw), lambda p, i: (i, p)),
                pl.BlockSpec((n_tok, qw), lambda p, i: (0, p)),
                pl.BlockSpec((n_tok, LANE), lambda p, i: (0, p))]
    args = [q, k, v]
    if has_bias:
        in_specs.append(pl.BlockSpec((2, n_tok, LANE), lambda p, i: (p, 0, 0)))
        args.append(cum_b)
    return pl.pallas_call(
        body, grid=(4, nq), in_specs=in_specs,
        out_specs=[pl.BlockSpec((tq, LANE), lambda p, i: (i, p)), pl.BlockSpec((None, 8, tq), lambda p, i: (p, 0, i))],
        out_shape=[jax.ShapeDtypeStruct((n_tok, 512), f32), jax.ShapeDtypeStruct((4, 8, n_tok), f32)], name=name,
        scratch_shapes=[pltpu.VMEM((nkv, LANE, ATT_KV), bf16)],
        compiler_params=pltpu.CompilerParams(dimension_semantics=("arbitrary", "arbitrary"), vmem_limit_bytes=VMEM_LIMIT),
    )(*args)


def _attn_bwd(name, q, k, v, o, lse, do, cum_b, *, mla, n_tok):
    qw, scale, tq = _attn_common(mla, n_tok)
    nq = n_tok // tq
    nkv = n_tok // ATT_KV
    has_bias = cum_b is not None

    def body(*refs):
        if has_bias:
            q_ref, k_ref, v_ref, o_ref, lse_ref, do_ref, cb_ref, dq_ref, dk_ref, dv_ref, dck_ref, dcq_ref, kt_ref = refs
        else:
            q_ref, k_ref, v_ref, o_ref, lse_ref, do_ref, dq_ref, dk_ref, dv_ref, kt_ref = refs
        p = pl.program_id(0)
        i = pl.program_id(1)

        @pl.when(i == 0)
        def _():
            dk_ref[...] = jnp.zeros_like(dk_ref)
            dv_ref[...] = jnp.zeros_like(dv_ref)
            for jb in range(nkv):
                for c0 in range(0, qw, LANE):
                    kt_ref[jb, c0:c0 + LANE, :] = k_ref[jb * ATT_KV:(jb + 1) * ATT_KV, c0:c0 + LANE].astype(f32).T.astype(bf16)

        if has_bias:
            @pl.when(jnp.logical_and(i == 0, p == 0))
            def _():
                dck_ref[...] = jnp.zeros_like(dck_ref)

        heads = _attn_heads(q_ref, mla)
        do = do_ref[...]
        do_t = do.T
        prod_t = (do * o_ref[...]).T
        row = lax.broadcasted_iota(jnp.int32, (LANE, tq), 0)
        lane = lax.broadcasted_iota(jnp.int32, (tq, LANE), 1)
        lane_k = lax.broadcasted_iota(jnp.int32, (ATT_KV, LANE), 1)
        per_head = []
        for e in (0, 1):
            sel_r = (row >= 64) == bool(e)
            per_head.append((jnp.where(sel_r, do_t, 0.0).astype(bf16),
                             jnp.where((lane >= 64) == bool(e), do, 0.0).astype(bf16),
                             jnp.sum(jnp.where(sel_r, prod_t, 0.0), axis=0, keepdims=True),
                             lse_ref[e:e + 1, :]))
        dq_rows = LANE if mla else 64

        def step(j, carry, masked, q_lo=0):
            off = pl.multiple_of(j * ATT_KV, ATT_KV)
            allowed = _attn_allowed(off, i, tq, mla)[:, q_lo:] if masked else None
            keep = lambda old, part: part if q_lo == 0 else jnp.concatenate([old[:, :q_lo], part], axis=1)
            vb = v_ref[pl.ds(off, ATT_KV), :]
            kt = kt_ref[j]
            cols = [slice(e * LANE, (e + 1) * LANE) if mla else slice(None) for e in (0, 1)]
            sts = [_nn(k_ref[pl.ds(off, ATT_KV), cols[e]], heads[e][0][:, q_lo:]) for e in (0, 1)]
            dpts = [_nn(vb, per_head[e][0][:, q_lo:]) for e in (0, 1)]
            mids = []
            for e in (0, 1):
                _, _, delta, lse_e = per_head[e]
                st = sts[e] * scale
                if has_bias:
                    st = st - jnp.tile(cb_ref[e, pl.ds(off, ATT_KV), :], (1, (tq - q_lo) // LANE))
                pt = jnp.exp(st - lse_e[:, q_lo:])
                if masked:
                    pt = jnp.where(allowed, pt, 0.0)
                dst = pt * (dpts[e] - delta[:, q_lo:])
                qsum = carry[e][1][:, q_lo:]
                if has_bias:
                    rs = jnp.sum(dst, axis=1, keepdims=True)
                    dck_ref[pl.ds(off, ATT_KV), :] += jnp.where(lane_k == 2 * p + e, -rs, 0.0)
                    qsum = qsum + jnp.sum(dst, axis=0, keepdims=True)
                mids.append((pt.astype(bf16), dst.astype(bf16), qsum))
            new = []
            for e in (0, 1):
                pt, dst, qsum = mids[e]
                kt_e = kt[e * LANE:(e + 1) * LANE, :] if mla else kt[64 * e:64 * e + 64, :]
                dq = carry[e][0][:, q_lo:] + _nn(kt_e, dst) * scale
                new.append((keep(carry[e][0], dq), keep(carry[e][1], qsum)))
                dk_ref[pl.ds(off, ATT_KV), cols[e]] += _nn(dst, heads[e][1][q_lo:, :]) * scale
                dv_ref[pl.ds(off, ATT_KV), :] += _nn(pt, per_head[e][1][q_lo:, :])
            return tuple(new)

        init = tuple((jnp.zeros((dq_rows, tq), f32), jnp.zeros((1, tq), f32)) for _ in (0, 1))
        n_full = i * (tq // ATT_KV)
        carry = lax.fori_loop(0, n_full, functools.partial(step, masked=False), init)
        for d in range(tq // ATT_KV):
            carry = step(n_full + d, carry, True, q_lo=d * ATT_KV)
        if mla:
            for e in (0, 1):
                dq_ref[:, e * LANE:(e + 1) * LANE] = carry[e][0].T
        else:
            dq_ref[...] = jnp.concatenate([carry[0][0], carry[1][0]], axis=0).T
        if has_bias:
            dcq_ref[...] = jnp.zeros_like(dcq_ref)
            for e in (0, 1):
                dcq_ref[e:e + 1, :] = carry[e][1]

    tile_q = pl.BlockSpec((tq, qw), lambda p, i: (i, p))
    tile_v = pl.BlockSpec((tq, LANE), lambda p, i: (i, p))
    full_k = pl.BlockSpec((n_tok, qw), lambda p, i: (0, p))
    full_v = pl.BlockSpec((n_tok, LANE), lambda p, i: (0, p))
    in_specs = [tile_q, full_k, full_v, tile_v, pl.BlockSpec((None, 8, tq), lambda p, i: (p, 0, i)), tile_v]
    args = [q, k, v, o, lse, do]
    out_specs = [tile_q, full_k, full_v]
    out_shape = [jax.ShapeDtypeStruct((n_tok, 4 * qw), f32), jax.ShapeDtypeStruct((n_tok, 4 * qw), f32),
                 jax.ShapeDtypeStruct((n_tok, 512), f32)]
    if has_bias:
        in_specs.append(pl.BlockSpec((2, n_tok, LANE), lambda p, i: (p, 0, 0)))
        args.append(cum_b)
        out_specs += [pl.BlockSpec((n_tok, LANE), _const_map(2)), pl.BlockSpec((None, 8, tq), lambda p, i: (p, 0, i))]
        out_shape += [jax.ShapeDtypeStruct((n_tok, LANE), f32), jax.ShapeDtypeStruct((4, 8, n_tok), f32)]
    return pl.pallas_call(
        body, grid=(4, nq), in_specs=in_specs, out_specs=out_specs, out_shape=out_shape, name=name,
        scratch_shapes=[pltpu.VMEM((nkv, qw, ATT_KV), bf16)],
        compiler_params=pltpu.CompilerParams(dimension_semantics=("arbitrary", "arbitrary"), vmem_limit_bytes=VMEM_LIMIT),
    )(*args)


def _s5_disc(lr, li, ldt):
    dt = jnp.exp(ldt)
    mag = jnp.exp(lr * dt)
    a_re = mag * jnp.cos(li * dt)
    a_im = mag * jnp.sin(li * dt)
    den = lr * lr + li * li
    f_re = ((a_re - 1.0) * lr + a_im * li) / den
    f_im = (a_im * lr - (a_re - 1.0) * li) / den
    return a_re, a_im, f_re, f_im


def _s5_param_fwd(lr, li, ldt, b_re, b_im):
    def body(lr_ref, li_ref, ldt_ref, br_ref, bi_ref, ar_ref, ai_ref, bbr_ref, bbi_ref):
        a_re, a_im, f_re, f_im = _s5_disc(lr_ref[...], li_ref[...], ldt_ref[...])
        ar_ref[...] = a_re
        ai_ref[...] = a_im
        br, bi = br_ref[...], bi_ref[...]
        bbr_ref[...] = f_re * br - f_im * bi
        bbi_ref[...] = f_re * bi + f_im * br

    col = jax.ShapeDtypeStruct(lr.shape, f32)
    mat = jax.ShapeDtypeStruct(b_re.shape, f32)
    return pl.pallas_call(body, out_shape=[col, col, mat, mat], name="s5_param_fwd")(lr, li, ldt, b_re, b_im)


def _s5_param_bwd(lr, li, ldt, b_re, b_im, da_re, da_im, dbb_re, dbb_im):
    def body(lr_ref, li_ref, ldt_ref, br_ref, bi_ref, dar_ref, dai_ref, gbr_ref, gbi_ref,
             dlr_ref, dli_ref, dldt_ref, dbr_ref, dbi_ref):
        (a_re, a_im, f_re, f_im), vjp = jax.vjp(_s5_disc, lr_ref[...], li_ref[...], ldt_ref[...])
        br, bi, gr, gi = br_ref[...], bi_ref[...], gbr_ref[...], gbi_ref[...]
        dbr_ref[...] = f_re * gr + f_im * gi
        dbi_ref[...] = f_re * gi - f_im * gr
        dfr = jnp.sum(br * gr + bi * gi, axis=-1, keepdims=True)
        dfi = jnp.sum(br * gi - bi * gr, axis=-1, keepdims=True)
        dlr, dli, dldt = vjp((dar_ref[...], dai_ref[...], dfr, dfi))
        dlr_ref[...] = dlr
        dli_ref[...] = dli
        dldt_ref[...] = jnp.sum(dldt.reshape(S5_GROUPS, S5_STATE, 1), axis=1)

    col = jax.ShapeDtypeStruct((S5_LANES, 1), f32)
    mat = jax.ShapeDtypeStruct((S5_LANES, S5_GROUP), f32)
    return pl.pallas_call(body, out_shape=[col, col, jax.ShapeDtypeStruct((S5_GROUPS, 1), f32), mat, mat],
                          name="s5_param_bwd")(lr, li, ldt, b_re, b_im, da_re, da_im, dbb_re, dbb_im)


_SCAN_NB = 4


def _to_streams(a):
    s, c = a.shape
    return jnp.swapaxes(a.reshape(8, s // 8, c), 0, 1).reshape(s, c)


def _from_streams(a):
    s, c = a.shape
    return jnp.swapaxes(a.reshape(s // 8, 8, c), 0, 1).reshape(s, c)


def _s5_scan(name, src, wq, wy, add, y_dtype, a_re8, a_im8, *, reverse, n_tok, grads_of=None):
    rows = n_tok // 8
    nb = _SCAN_NB
    assert nb == 4

    def scan_body(src_ref, w_ref, wy_ref, add_ref, ar_ref, ai_ref, x_ref, y_ref):
        for ri in (0, 1):
            bu = _nn(src_ref[...], w_ref[ri])
            for b in range(nb):
                x_ref[ri, b] = bu[:, b * LANE:(b + 1) * LANE]
        bu_ref = x_ref
        a_r = [ar_ref[b] for b in range(nb)]
        a_i = [ai_ref[b] for b in range(nb)]
        zero = jnp.zeros((8, LANE), f32)
        one = jnp.ones((8, LANE), f32)

        def rows_at(r):
            rr = (rows - 1 - r) if reverse else r
            return pl.ds(pl.multiple_of(rr * 8, 8), 8)

        def pass1(r, carry):
            out = []
            sl = rows_at(r)
            for b in range(nb):
                xr, xi, mr, mi = carry[b]
                nr = a_r[b] * xr - a_i[b] * xi + bu_ref[0, b, sl, :]
                ni = a_r[b] * xi + a_i[b] * xr + bu_ref[1, b, sl, :]
                x_ref[0, b, sl, :] = nr
                x_ref[1, b, sl, :] = ni
                out.append((nr, ni, a_r[b] * mr - a_i[b] * mi, a_r[b] * mi + a_i[b] * mr))
            return tuple(out)

        carry = lax.fori_loop(0, rows, pass1, tuple((zero, zero, one, zero) for _ in range(nb)))
        sub = lax.broadcasted_iota(jnp.int32, (8, LANE), 0)
        feed = []
        for b in range(nb):
            lr_, li_, pr, pi = carry[b]
            fr, fi = zero, zero
            for _ in range(7):
                tr = lr_ + pr * fr - pi * fi
                ti = li_ + pr * fi + pi * fr
                if reverse:
                    fr = jnp.where(sub < 7, pltpu.roll(tr, 7, 0), 0.0)
                    fi = jnp.where(sub < 7, pltpu.roll(ti, 7, 0), 0.0)
                else:
                    fr = jnp.where(sub > 0, pltpu.roll(tr, 1, 0), 0.0)
                    fi = jnp.where(sub > 0, pltpu.roll(ti, 1, 0), 0.0)
            feed.append((fr, fi))

        def pass2(r, carry):
            out = []
            sl = rows_at(r)
            for b in range(nb):
                mr, mi = carry[b]
                fr, fi = feed[b]
                x_ref[0, b, sl, :] += mr * fr - mi * fi
                x_ref[1, b, sl, :] += mr * fi + mi * fr
                out.append((a_r[b] * mr - a_i[b] * mi, a_r[b] * mi + a_i[b] * mr))
            return tuple(out)

        lax.fori_loop(0, rows, pass2, tuple((a_r[b], a_i[b]) for b in range(nb)))

        y = None
        for ri in (0, 1):
            for b in range(nb):
                t = _nn(x_ref[ri, b], wy_ref[ri, b * LANE:(b + 1) * LANE, :])
                y = t if y is None else y + t
        if add is not None:
            y = y + add_ref[...]
        y_ref[...] = y.astype(y_ref.dtype)

    def grads_body(src_ref, xs_ref, u_ref, g_ref, da_ref, dc_ref, db_ref):
        t = lax.broadcasted_iota(jnp.int32, (n_tok, LANE), 0)
        sub = lax.broadcasted_iota(jnp.int32, (8, LANE), 0)

        def prev(v):
            return (jnp.where(t >= 8, pltpu.roll(v, 8, 0), 0.0),
                    jnp.where(sub > 0, pltpu.roll(v[n_tok - 8:, :], 1, 0), 0.0))

        for b in range(nb):
            (xr, hr), (xi, hi) = prev(xs_ref[0, b]), prev(xs_ref[1, b])
            gr, gi = g_ref[0, b], g_ref[1, b]
            gr0, gi0 = gr[0:8, :], gi[0:8, :]
            da_ref[b, 0:1, :] = (jnp.sum(xr * gr + xi * gi, axis=0, keepdims=True)
                                 + jnp.sum(hr * gr0 + hi * gi0, axis=0, keepdims=True))
            da_ref[b, 1:2, :] = (jnp.sum(xr * gi - xi * gr, axis=0, keepdims=True)
                                 + jnp.sum(hr * gi0 - hi * gr0, axis=0, keepdims=True))
            for ri in (0, 1):
                dc_ref[ri, b * LANE:(b + 1) * LANE, :] = _tn(xs_ref[ri, b], src_ref[...])
                db_ref[ri, :, b * LANE:(b + 1) * LANE] = _tn(u_ref[...], g_ref[ri, b])

    n_in = 3 + (add is not None) + 2 * (grads_of is not None)

    def body(*refs):
        ins, rest = list(refs[:n_in]), refs[n_in:]
        src_ref, w_ref, wy_ref = ins[:3]
        add_ref = ins[3] if add is not None else None
        ar_ref, ai_ref = rest[:2]
        if grads_of is None:
            x_ref, y_ref = rest[2:]
            scan_body(src_ref, w_ref, wy_ref, add_ref, ar_ref, ai_ref, x_ref, y_ref)
        else:
            xs_ref, u_ref = ins[-2:]
            y_ref, da_ref, dc_ref, db_ref, x_ref = rest[2:]
            scan_body(src_ref, w_ref, wy_ref, add_ref, ar_ref, ai_ref, x_ref, y_ref)
            grads_body(src_ref, xs_ref, u_ref, x_ref, da_ref, dc_ref, db_ref)

    blk = pl.BlockSpec((2, nb, n_tok, LANE), lambda g: (0, g, 0, 0))
    ablk = pl.BlockSpec((nb, 8, LANE), lambda g: (g, 0, 0))
    col = pl.BlockSpec((n_tok, LANE), lambda g: (0, g))
    in_specs = [col, pl.BlockSpec((2, None, LANE, 512), lambda g: (0, g, 0, 0)), pl.BlockSpec((2, None, 512, LANE), lambda g: (0, g, 0, 0))]
    args = [src, wq, wy]
    if add is not None:
        in_specs.append(col)
        args.append(add)
    y_shape = jax.ShapeDtypeStruct((n_tok, 512), y_dtype)
    x_shape = (2, S5_BLOCKS, n_tok, LANE)
    params = pltpu.CompilerParams(dimension_semantics=("arbitrary",), vmem_limit_bytes=VMEM_LIMIT)
    if grads_of is None:
        return pl.pallas_call(
            body, grid=(S5_BLOCKS // nb,), in_specs=in_specs + [ablk, ablk], out_specs=[blk, col],
            out_shape=[jax.ShapeDtypeStruct(x_shape, f32), y_shape], name=name, compiler_params=params,
        )(*args, a_re8, a_im8)
    return pl.pallas_call(
        body, grid=(S5_BLOCKS // nb,), in_specs=in_specs + [blk, col, ablk, ablk],
        out_specs=[col, pl.BlockSpec((nb, 2, LANE), lambda g: (g, 0, 0)), pl.BlockSpec((2, None, 512, LANE), lambda g: (0, g, 0, 0)),
                   pl.BlockSpec((2, None, LANE, 512), lambda g: (0, g, 0, 0))],
        out_shape=[y_shape, jax.ShapeDtypeStruct((S5_BLOCKS, 2, LANE), f32), jax.ShapeDtypeStruct((2, S5_Q, 512, LANE), f32),
                   jax.ShapeDtypeStruct((2, S5_Q, LANE, 512), f32)],
        scratch_shapes=[pltpu.VMEM((2, nb, n_tok, LANE), f32)], name=name, compiler_params=params,
    )(*args, *grads_of, a_re8, a_im8)


S5_Q = 4


def _bd8(t):
    _, a, b = t.shape
    t = t.reshape(S5_Q, 8, a, 1, b)
    eye = jnp.eye(8, dtype=jnp.bool_).reshape(1, 8, 1, 8, 1)
    return jnp.where(eye, jnp.broadcast_to(t, (S5_Q, 8, a, 8, b)), jnp.zeros((), t.dtype)).reshape(S5_Q, 8 * a, 8 * b)


def _bd8_diag(m, a, b):
    m = m.reshape(S5_Q, 8, a, 8, b)
    eye = jnp.eye(8, dtype=jnp.bool_).reshape(1, 8, 1, 8, 1)
    return jnp.sum(jnp.where(eye, m, 0.0), axis=3).reshape(S5_GROUPS, a, b)


N_CHIPS = 4
_BIG_SHARD = {"w_in": (1, 1024, 1770, 1792), "mla_w_q_up": (1, 256, 192, 256), "mla_w_kv_up": (1, 128, 256, 256),
              "s5_w_glu": (0, 128, 512, 512), "w_branch_out": (0, 384, 1024, 1024), "w_out": (0, 256, 1024, 1024)}


def _to_shards(name, m):
    axis, r, c, cp = _BIG_SHARD[name]
    if axis == 0:
        return m.reshape(N_CHIPS, r, c)
    return jnp.stack([jnp.pad(m[:, j * c:(j + 1) * c], ((0, 0), (0, cp - c))) for j in range(N_CHIPS)])


def _from_shards(name, s):
    axis, r, c, cp = _BIG_SHARD[name]
    if axis == 0:
        return s.reshape(N_CHIPS * r, c)
    return jnp.concatenate([s[j, :, :c] for j in range(N_CHIPS)], axis=1)


def _pad_w_in(w):
    pieces, pos = [], 0
    for name in _PAD_ORDER:
        start, width, inner = _PAD[name]
        o0, ow = _ORIG[name]
        if start + inner > pos:
            pieces.append(jnp.zeros((w.shape[0], start + inner - pos), w.dtype))
        pieces.append(w[:, o0:o0 + ow])
        pos = start + inner + ow
    pieces.append(jnp.zeros((w.shape[0], NP - pos), w.dtype))
    return jnp.concatenate(pieces, axis=1)


def _prep_weights(small, big):
    per_layer = jax.vmap
    w = {}
    w["w_in_shards"] = big["w_in"]
    w["w_in"] = per_layer(lambda s: _pad_w_in(_from_shards("w_in", s)))(big["w_in"])

    def q_up(s):
        wq = _from_shards("mla_w_q_up", s).reshape(MLA_Q_RANK, HEADS, MLA_QK)
        return jnp.pad(wq, ((0, 0), (0, 0), (0, LANE - MLA_QK))).reshape(MLA_Q_RANK, HEADS * LANE)

    def kv_up(s):
        wkv = _from_shards("mla_w_kv_up", s).reshape(MLA_KV_RANK, HEADS, 128)
        wk = jnp.pad(wkv[:, :, :64], ((0, 0), (0, 0), (0, 64))).reshape(MLA_KV_RANK, HEADS * LANE)
        return jnp.concatenate([wk, wkv[:, :, 64:].reshape(MLA_KV_RANK, 512)], axis=1)

    w["wq"] = per_layer(q_up)(big["mla_w_q_up"])
    w["wkv"] = per_layer(kv_up)(big["mla_w_kv_up"])
    for name, key in (("w_glu", "s5_w_glu"), ("wo", "w_branch_out"), ("w_out", "w_out")):
        w[name] = per_layer(functools.partial(_from_shards, key))(big[key])
    row = lambda a: a.astype(f32)[:, None, :]
    lanes = lambda a, n: jnp.pad(row(a), ((0, 0), (0, 0), (0, LANE - n)))
    w["norm_g"] = row(small["norm_g"])
    w["qa_g"] = row(small["mla_q_a_norm"])
    w["kva_g"] = row(small["mla_kv_a_norm"])
    w["qn_g"] = lanes(small["mla_q_norm"], MLA_QK)
    w["kn_g"] = lanes(small["mla_k_norm"], MLA_QK)
    w["fq_g"] = jnp.tile(row(small["fox_q_norm"]), (1, 1, 2))
    w["fk_g"] = jnp.tile(row(small["fox_k_norm"]), (1, 1, 2))
    w["b_f"] = lanes(small["fox_b_f"], HEADS)
    w["lr"] = small["s5_lambda_re"].reshape(DEPTH, S5_LANES, 1)
    w["li"] = small["s5_lambda_im"].reshape(DEPTH, S5_LANES, 1)
    w["ldt"] = jnp.repeat(small["s5_log_dt"], S5_STATE, axis=1).reshape(DEPTH, S5_LANES, 1)
    w["b_re"] = small["s5_b_re"].reshape(DEPTH, S5_LANES, S5_GROUP)
    w["b_im"] = small["s5_b_im"].reshape(DEPTH, S5_LANES, S5_GROUP)
    w["s5_d"] = row(small["s5_d"])
    w["b_glu"] = row(small["s5_b_glu"])
    a_re, a_im, bb_re, bb_im = _s5_param_fwd(w["lr"], w["li"], w["ldt"], w["b_re"], w["b_im"])
    per_group = lambda m: m.reshape(S5_GROUPS, S5_STATE, S5_GROUP)
    pair = lambda f: per_layer(lambda re, im: jnp.stack([f(re), f(im)]).astype(bf16))
    c_re, c_im = small["s5_c_re"], -small["s5_c_im"]
    w["b_cn"] = pair(lambda m: _bd8(jnp.swapaxes(per_group(m), 1, 2)))(bb_re, bb_im)
    w["b_nc"] = pair(lambda m: _bd8(per_group(m)))(bb_re, bb_im)
    w["c_nc"] = pair(lambda m: _bd8(jnp.swapaxes(m, 1, 2)))(c_re, c_im)
    w["c_cn"] = pair(_bd8)(c_re, c_im)
    sublanes = lambda a: jnp.broadcast_to(a.reshape(DEPTH, S5_BLOCKS, 1, LANE), (DEPTH, S5_BLOCKS, 8, LANE))
    w["a_re8"], w["a_im8"], w["a_im8_neg"] = sublanes(a_re), sublanes(a_im), sublanes(-a_im)
    return w


def _fox_halves(x, lane):
    sq = x * x
    lo = jnp.sum(jnp.where(lane < 64, sq, 0.0), axis=-1, keepdims=True)
    hi = jnp.sum(sq, axis=-1, keepdims=True) - lo
    return jnp.where(lane < 64, lax.rsqrt(lo * (1.0 / 64) + EPS), lax.rsqrt(hi * (1.0 / 64) + EPS))


def _fox_halves_bwd(dy, x, r, g, lane):
    xh = x * r
    dxh = dy * g
    pr = dxh * xh
    lo = jnp.sum(jnp.where(lane < 64, pr, 0.0), axis=-1, keepdims=True)
    hi = jnp.sum(pr, axis=-1, keepdims=True) - lo
    mean = jnp.where(lane < 64, lo, hi) * (1.0 / 64)
    return r * (dxh - xh * mean), jnp.sum(dy * xh, axis=0, keepdims=True)


def _mla_recompute(cq, ckv, kpe, c, s1, s2, qa_g, kva_g, wq, wkv):
    cqn, r_cq = _rms(cq, qa_g, MLA_Q_RANK)
    ckvn, r_ckv = _rms(ckv, kva_g, MLA_KV_RANK)
    cqn_b = cqn.astype(bf16)
    ckvn_b = ckvn.astype(bf16)
    q_raw = _nn(cqn_b, wq)
    kv_raw = _nn(ckvn_b, wkv)
    kpe_rot = _rope(kpe, c, s1, s2)
    return cqn_b, r_cq, ckvn_b, r_ckv, q_raw, kv_raw, kpe_rot


def _layer_fwd(x, w, rope_tabs, n_tok):
    c_tab, s1_tab, s2_tab = rope_tabs
    saved = {"x": x}

    def norm_body(x_ref, g_ref, h_ref):
        h_ref[...] = _rms(x_ref[...], g_ref[...], D_MODEL)[0].astype(bf16)

    (h,) = _rowwise("norm_fwd", norm_body, n_tok, [(x, D_MODEL, 0)], [w["norm_g"]], [(D_MODEL, bf16)], [])
    proj = _mm_nn("in_proj", h, w["w_in"], m=n_tok, n=NP, k=D_MODEL, tm=n_tok, tn=512, tk=D_MODEL)
    saved["h"], saved["proj"] = h, proj

    def mla_prep_body(cq_ref, ckv_ref, kpe_ref, c_ref, s1_ref, s2_ref, qa_ref, kva_ref, wq_ref, wkv_ref, qn_g_ref, kn_g_ref,
                      qn_ref, kn_ref, v_ref):
        c, s1, s2 = c_ref[...], s1_ref[...], s2_ref[...]
        _, _, _, _, q_raw, kv_raw, kpe_rot = _mla_recompute(cq_ref[...], ckv_ref[...], kpe_ref[...], c, s1, s2,
                                                            qa_ref[...], kva_ref[...], wq_ref[...], wkv_ref[...])
        for hd in range(HEADS):
            sl = slice(hd * LANE, (hd + 1) * LANE)
            qn_ref[:, sl] = _rms(_rope(q_raw[:, sl], c, s1, s2), qn_g_ref[...], MLA_QK)[0].astype(bf16)
            kn_ref[:, sl] = _rms(kv_raw[:, sl] + kpe_rot, kn_g_ref[...], MLA_QK)[0].astype(bf16)
        v_ref[...] = kv_raw[:, HEADS * LANE:].astype(bf16)

    qn, kn, v_mla = _rowwise(
        "mla_prep", mla_prep_body, n_tok,
        [(proj, *_seg("cq")), (proj, *_seg("ckv")), (proj, *_seg("kpe")), (c_tab, LANE, 0), (s1_tab, LANE, 0), (s2_tab, LANE, 0)],
        [w["qa_g"], w["kva_g"], w["wq"], w["wkv"], w["qn_g"], w["kn_g"]],
        [(HEADS * LANE, bf16), (HEADS * LANE, bf16), (512, bf16)], [])
    y_mla, lse_mla = _attn_fwd("mla_attn_fwd", qn, kn, v_mla, None, mla=True, n_tok=n_tok)
    saved.update(qn=qn, kn=kn, v_mla=v_mla, y_mla=y_mla, lse_mla=lse_mla)

    def fox_prep_body(fq_ref, fk_ref, fv_ref, ff_ref, qg_ref, kg_ref, bf_ref, fqn_ref, fkn_ref, fvb_ref, logf_ref):
        lane = lax.broadcasted_iota(jnp.int32, (TOK, LANE), 1)
        for blk in range(4):
            sl = slice(blk * LANE, (blk + 1) * LANE)
            xq = fq_ref[:, sl]
            fqn_ref[:, sl] = (xq * _fox_halves(xq, lane) * qg_ref[...]).astype(bf16)
            xk = fk_ref[:, sl]
            fkn_ref[:, sl] = (xk * _fox_halves(xk, lane) * kg_ref[...]).astype(bf16)
        fvb_ref[...] = fv_ref[...].astype(bf16)
        z = ff_ref[...] + bf_ref[...]
        logf_ref[...] = jnp.minimum(z, 0.0) - jnp.log(1.0 + jnp.exp(-jnp.abs(z)))

    fqn, fkn, fvb, logf = _rowwise(
        "fox_prep", fox_prep_body, n_tok,
        [(proj, *_seg("fq")), (proj, *_seg("fk")), (proj, *_seg("fv")), (proj, *_seg("ff"))],
        [w["fq_g"], w["fk_g"], w["b_f"]],
        [(512, bf16), (512, bf16), (512, bf16), (LANE, f32)], [])

    def cum_body(x_ref, cum_ref):
        x = x_ref[...]
        t = lax.broadcasted_iota(jnp.int32, x.shape, 0)
        s = 1
        while s < n_tok:
            x = x + jnp.where(t >= s, pltpu.roll(x, s, 0), 0.0)
            s *= 2
        for hd in range(HEADS):
            cum_ref[hd] = jnp.broadcast_to(x[:, hd:hd + 1], (n_tok, LANE))

    cum_b = pl.pallas_call(cum_body, out_shape=jax.ShapeDtypeStruct((HEADS, n_tok, LANE), f32), name="fox_cum")(logf)
    y_fox, lse_fox = _attn_fwd("fox_attn_fwd", fqn, fkn, fvb, cum_b, mla=False, n_tok=n_tok)
    saved.update(fqn=fqn, fkn=fkn, fvb=fvb, cum_b=cum_b, y_fox=y_fox, lse_fox=lse_fox)

    u_w, u_blk = _seg("s5u")
    u_streams = _to_streams(proj[:, u_blk * u_w:(u_blk + 1) * u_w])
    xs, ylin = _s5_scan("s5_scan_fwd", u_streams, w["b_cn"], w["c_nc"], None, f32, w["a_re8"], w["a_im8"], reverse=False, n_tok=n_tok)
    ylin = _from_streams(ylin)

    def s5_post_body(yl_ref, u_ref, d_ref, wg_ref, bg_ref, out_ref):
        y = yl_ref[...] + d_ref[...] * u_ref[...]
        z, _ = _gelu(y)
        out_ref[...] = z * _sigmoid(_nn(z, wg_ref[...]) + bg_ref[...])

    (y_s5,) = _rowwise("s5_post", s5_post_body, n_tok, [(ylin, 512, 0), (proj, u_w, u_blk)],
                       [w["s5_d"], w["w_glu"], w["b_glu"]], [(512, f32)], [])
    saved.update(xs=xs, ylin=ylin, y_s5=y_s5, u_streams=u_streams)

    def merge_body(ym_ref, yf_ref, ys_ref, gm_ref, gf_ref, gs_ref, mm_ref, mf_ref, ms_ref, x_ref, wo_ref, wout_ref, out_ref):
        merged = jnp.zeros((TOK, D_MODEL), f32)
        for b, (y_ref, g_ref, m_ref) in enumerate(((ym_ref, gm_ref, mm_ref), (yf_ref, gf_ref, mf_ref), (ys_ref, gs_ref, ms_ref))):
            g = g_ref[...]
            a = y_ref[...] * (g * _sigmoid(g))
            merged = merged + _sigmoid(m_ref[...]) * _nn(a, wo_ref[b * 512:(b + 1) * 512, :])
        out_ref[...] = x_ref[...] + _nn(merged, wout_ref[...])

    (out,) = _rowwise(
        "merge_fwd", merge_body, n_tok,
        [(y_mla, 512, 0), (y_fox, 512, 0), (y_s5, 512, 0), (proj, *_seg("g_mla")), (proj, *_seg("g_fox")), (proj, *_seg("g_s5")),
         (proj, *_seg("m_mla")), (proj, *_seg("m_fox")), (proj, *_seg("m_s5")), (x, D_MODEL, 0)],
        [w["wo"], w["w_out"]], [(D_MODEL, f32)], [])
    return out, saved


def _layer_bwd(dout, w, sv, rope_tabs, n_tok):
    c_tab, s1_tab, s2_tab = rope_tabs
    proj, x = sv["proj"], sv["x"]
    grads = {}

    def merge_bwd_body(ym_ref, yf_ref, ys_ref, gm_ref, gf_ref, gs_ref, mm_ref, mf_ref, ms_ref, do_ref, wo_ref, wout_ref,
                       dym_ref, dyf_ref, dys_ref, dgm_ref, dgf_ref, dgs_ref, dmm_ref, dmf_ref, dms_ref, dwo_ref, dwout_ref):
        do = do_ref[...]
        branches = ((ym_ref, gm_ref, mm_ref, dym_ref, dgm_ref, dmm_ref), (yf_ref, gf_ref, mf_ref, dyf_ref, dgf_ref, dmf_ref),
                    (ys_ref, gs_ref, ms_ref, dys_ref, dgs_ref, dms_ref))
        acts, outs, sigs = [], [], []
        merged = jnp.zeros((TOK, D_MODEL), f32)
        for b, (y_ref, g_ref, m_ref, _, _, _) in enumerate(branches):
            g = g_ref[...]
            a = (y_ref[...] * (g * _sigmoid(g))).astype(bf16)
            o = _nn(a, wo_ref[b * 512:(b + 1) * 512, :])
            s = _sigmoid(m_ref[...])
            merged = merged + s * o
            acts.append(a)
            outs.append(o)
            sigs.append(s)
        dmerged = _nt(do, wout_ref[...])
        _accumulate(dwout_ref, _tn(merged, do))
        dwo = []
        for b, (y_ref, g_ref, m_ref, dy_ref, dg_ref, dm_ref) in enumerate(branches):
            s, o = sigs[b], outs[b]
            dm_ref[...] = (dmerged * o * s * (1.0 - s)).astype(bf16)
            d_o = dmerged * s
            da = _nt(d_o, wo_ref[b * 512:(b + 1) * 512, :])
            dwo.append(_tn(acts[b], d_o))
            g = g_ref[...]
            sg = _sigmoid(g)
            dy_ref[...] = da * (g * sg)
            dg_ref[...] = (da * y_ref[...] * (sg * (1.0 + g * (1.0 - sg)))).astype(bf16)
        _accumulate(dwo_ref, jnp.concatenate(dwo, axis=0))

    (dy_mla, dy_fox, dy_s5, dg_mla, dg_fox, dg_s5, dm_mla, dm_fox, dm_s5, dwo, dwout) = _rowwise(
        "merge_bwd", merge_bwd_body, n_tok,
        [(sv["y_mla"], 512, 0), (sv["y_fox"], 512, 0), (sv["y_s5"], 512, 0), (proj, *_seg("g_mla")), (proj, *_seg("g_fox")),
         (proj, *_seg("g_s5")), (proj, *_seg("m_mla")), (proj, *_seg("m_fox")), (proj, *_seg("m_s5")), (dout, D_MODEL, 0)],
        [w["wo"], w["w_out"]],
        [(512, f32)] * 3 + [(512, bf16)] * 3 + [(D_MODEL, bf16)] * 3, [((1536, D_MODEL), f32), ((D_MODEL, D_MODEL), f32)])
    grads["w_branch_out"], grads["w_out"] = dwo, dwout

    u_w, u_blk = _seg("s5u")

    def s5_post_bwd_body(yl_ref, u_ref, do_ref, d_ref, wg_ref, bg_ref, dyl_ref, dus_ref, dd_ref, dwg_ref, dbg_ref):
        u = u_ref[...]
        y = yl_ref[...] + d_ref[...] * u
        z, t = _gelu(y)
        s = _sigmoid(_nn(z, wg_ref[...]) + bg_ref[...])
        do = do_ref[...]
        dgl = do * z * s * (1.0 - s)
        dz = do * s + _nt(dgl, wg_ref[...])
        dy = dz * _gelu_grad(y, t)
        dyl_ref[...] = dy.astype(bf16)
        dus_ref[...] = dy * d_ref[...]
        _accumulate(dd_ref, jnp.sum(dy * u, axis=0, keepdims=True))
        _accumulate(dwg_ref, _tn(z, dgl))
        _accumulate(dbg_ref, jnp.sum(dgl, axis=0, keepdims=True))

    dylin, du_skip, dd, dwglu, dbglu = _rowwise(
        "s5_post_bwd", s5_post_bwd_body, n_tok, [(sv["ylin"], 512, 0), (proj, u_w, u_blk), (dy_s5, 512, 0)],
        [w["s5_d"], w["w_glu"], w["b_glu"]], [(512, bf16), (512, f32)], [((1, 512), f32), ((512, 512), f32), ((1, 512), f32)])
    grads["s5_d"], grads["s5_w_glu"], grads["s5_b_glu"] = dd.reshape(512), dwglu, dbglu.reshape(512)

    dylin = _to_streams(dylin)
    ds5u, da, dc_nc, db_cn = _s5_scan("s5_scan_bwd", dylin, w["c_cn"], w["b_nc"], _to_streams(du_skip), bf16, w["a_re8"],
                                      w["a_im8_neg"], reverse=True, n_tok=n_tok, grads_of=(sv["xs"], sv["u_streams"]))
    ds5u = _from_streams(ds5u)
    diag_b = lambda m: jnp.swapaxes(_bd8_diag(m, S5_GROUP, S5_STATE), 1, 2).reshape(S5_LANES, S5_GROUP)
    diag_c = lambda m: jnp.swapaxes(_bd8_diag(m, S5_STATE, S5_GROUP), 1, 2)
    dlr, dli, dldt, db_re, db_im = _s5_param_bwd(
        w["lr"], w["li"], w["ldt"], w["b_re"], w["b_im"], da[:, 0, :].reshape(S5_LANES, 1), da[:, 1, :].reshape(S5_LANES, 1),
        diag_b(db_cn[0]), diag_b(db_cn[1]))
    grads["s5_lambda_re"] = dlr.reshape(S5_GROUPS, S5_STATE)
    grads["s5_lambda_im"] = dli.reshape(S5_GROUPS, S5_STATE)
    grads["s5_log_dt"] = dldt.reshape(S5_GROUPS)
    grads["s5_b_re"] = db_re.reshape(S5_GROUPS, S5_STATE, S5_GROUP)
    grads["s5_b_im"] = db_im.reshape(S5_GROUPS, S5_STATE, S5_GROUP)
    grads["s5_c_re"] = diag_c(dc_nc[0])
    grads["s5_c_im"] = -diag_c(dc_nc[1])

    dfqn, dfkn, dfv, dck, dcq = _attn_bwd("fox_attn_bwd", sv["fqn"], sv["fkn"], sv["fvb"], sv["y_fox"], sv["lse_fox"], dy_fox,
                                          sv["cum_b"], mla=False, n_tok=n_tok)
    dcq = jnp.pad(dcq[:, :2, :].reshape(HEADS, n_tok).T, ((0, 0), (0, LANE - HEADS)))

    def fox_gate_bwd_body(dk_ref, dq_ref, ff_ref, bf_ref, dff_ref, dbf_ref):
        xg = dk_ref[...] + dq_ref[...]
        t = lax.broadcasted_iota(jnp.int32, xg.shape, 0)
        s = 1
        while s < n_tok:
            xg = xg + jnp.where(t < n_tok - s, pltpu.roll(xg, n_tok - s, 0), 0.0)
            s *= 2
        dff = xg * _sigmoid(-(ff_ref[...] + bf_ref[...]))
        dff_ref[...] = dff.astype(bf16)
        dbf_ref[...] = jnp.sum(dff, axis=0, keepdims=True)

    ff_w, ff_blk = _seg("ff")
    dff, dbf = pl.pallas_call(
        fox_gate_bwd_body, grid=(1,),
        in_specs=[pl.BlockSpec((n_tok, LANE), lambda i: (0, 0)), pl.BlockSpec((n_tok, LANE), lambda i: (0, 0)),
                  pl.BlockSpec((n_tok, ff_w), lambda i: (0, ff_blk)), pl.BlockSpec((1, LANE), lambda i: (0, 0))],
        out_specs=[pl.BlockSpec((n_tok, LANE), lambda i: (0, 0)), pl.BlockSpec((1, LANE), lambda i: (0, 0))],
        out_shape=[jax.ShapeDtypeStruct((n_tok, LANE), bf16), jax.ShapeDtypeStruct((1, LANE), f32)], name="fox_gate_bwd",
    )(dck, dcq, proj, w["b_f"])
    grads["fox_b_f"] = dbf[0, :HEADS]

    def fox_prep_bwd_body(fq_ref, fk_ref, dqn_ref, dkn_ref, dv_ref, qg_ref, kg_ref, dfq_ref, dfk_ref, dfv_ref, dqg_ref, dkg_ref):
        lane = lax.broadcasted_iota(jnp.int32, (TOK, LANE), 1)
        dqg = jnp.zeros((1, LANE), f32)
        dkg = jnp.zeros((1, LANE), f32)
        for blk in range(4):
            sl = slice(blk * LANE, (blk + 1) * LANE)
            xq = fq_ref[:, sl]
            dx, dg = _fox_halves_bwd(dqn_ref[:, sl], xq, _fox_halves(xq, lane), qg_ref[...], lane)
            dfq_ref[:, sl] = dx.astype(bf16)
            dqg = dqg + dg
            xk = fk_ref[:, sl]
            dx, dg = _fox_halves_bwd(dkn_ref[:, sl], xk, _fox_halves(xk, lane), kg_ref[...], lane)
            dfk_ref[:, sl] = dx.astype(bf16)
            dkg = dkg + dg
        dfv_ref[...] = dv_ref[...].astype(bf16)
        _accumulate(dqg_ref, dqg + pltpu.roll(dqg, 64, 1))
        _accumulate(dkg_ref, dkg + pltpu.roll(dkg, 64, 1))

    dfq, dfk, dfvb, dfqg, dfkg = _rowwise(
        "fox_prep_bwd", fox_prep_bwd_body, n_tok,
        [(proj, *_seg("fq")), (proj, *_seg("fk")), (dfqn, 512, 0), (dfkn, 512, 0), (dfv, 512, 0)],
        [w["fq_g"], w["fk_g"]], [(512, bf16)] * 3, [((1, LANE), f32)] * 2)
    grads["fox_q_norm"], grads["fox_k_norm"] = dfqg[0, :FOX_DIM], dfkg[0, :FOX_DIM]

    dqn, dkn, dv_mla = _attn_bwd("mla_attn_bwd", sv["qn"], sv["kn"], sv["v_mla"], sv["y_mla"], sv["lse_mla"], dy_mla,
                                 None, mla=True, n_tok=n_tok)

    def mla_prep_bwd_body(cq_ref, ckv_ref, kpe_ref, c_ref, s1_ref, s2_ref, dqn_ref, dkn_ref, dv_ref,
                          qa_ref, kva_ref, wq_ref, wkv_ref, qn_g_ref, kn_g_ref,
                          dcq_ref, dckv_ref, dkpe_ref, dwq_ref, dwkv_ref, dqa_ref, dkva_ref, dqng_ref, dkng_ref):
        c, s1, s2 = c_ref[...], s1_ref[...], s2_ref[...]
        cq, ckv = cq_ref[...], ckv_ref[...]
        cqn_b, r_cq, ckvn_b, r_ckv, q_raw, kv_raw, kpe_rot = _mla_recompute(
            cq, ckv, kpe_ref[...], c, s1, s2, qa_ref[...], kva_ref[...], wq_ref[...], wkv_ref[...])
        lane = lax.broadcasted_iota(jnp.int32, (TOK, LANE), 1)
        dq_raw, dk_raw = [], []
        dkpe_rot = jnp.zeros((TOK, LANE), f32)
        dqng = jnp.zeros((1, LANE), f32)
        dkng = jnp.zeros((1, LANE), f32)
        for hd in range(HEADS):
            sl = slice(hd * LANE, (hd + 1) * LANE)
            q_rot = _rope(q_raw[:, sl], c, s1, s2)
            r = lax.rsqrt(jnp.sum(q_rot * q_rot, axis=-1, keepdims=True) * (1.0 / MLA_QK) + EPS)
            dx, dg = _rms_bwd(dqn_ref[:, sl], q_rot, r, qn_g_ref[...], MLA_QK)
            dqng = dqng + dg
            dq_raw.append(_rope_t(dx, c, s1, s2))
            k_full = kv_raw[:, sl] + kpe_rot
            r = lax.rsqrt(jnp.sum(k_full * k_full, axis=-1, keepdims=True) * (1.0 / MLA_QK) + EPS)
            dx, dg = _rms_bwd(dkn_ref[:, sl], k_full, r, kn_g_ref[...], MLA_QK)
            dkng = dkng + dg
            dk_raw.append(jnp.where(lane < 64, dx, 0.0))
            dkpe_rot = dkpe_rot + dx
        dkpe = _rope_t(dkpe_rot, c, s1, s2)
        dkpe_ref[...] = jnp.where(jnp.logical_and(lane >= 64, lane < 64 + ROPE), dkpe, 0.0).astype(bf16)
        dq_raw = jnp.concatenate(dq_raw, axis=1).astype(bf16)
        dkv_raw = jnp.concatenate(dk_raw + [dv_ref[...]], axis=1).astype(bf16)
        dcqn = _nt(dq_raw, wq_ref[...])
        dckvn = _nt(dkv_raw, wkv_ref[...])
        dx, dg = _rms_bwd(dcqn, cq, r_cq, qa_ref[...], MLA_Q_RANK)
        dcq_ref[...] = dx.astype(bf16)
        _accumulate(dqa_ref, dg)
        dx, dg = _rms_bwd(dckvn, ckv, r_ckv, kva_ref[...], MLA_KV_RANK)
        dckv_ref[...] = dx.astype(bf16)
        _accumulate(dkva_ref, dg)
        _accumulate(dwq_ref, _tn(cqn_b, dq_raw))
        _accumulate(dwkv_ref, _tn(ckvn_b, dkv_raw))
        _accumulate(dqng_ref, dqng)
        _accumulate(dkng_ref, dkng)

    dcq, dckv, dkpe, dwq, dwkv, dqa, dkva, dqng, dkng = _rowwise(
        "mla_prep_bwd", mla_prep_bwd_body, n_tok,
        [(proj, *_seg("cq")), (proj, *_seg("ckv")), (proj, *_seg("kpe")), (c_tab, LANE, 0), (s1_tab, LANE, 0), (s2_tab, LANE, 0),
         (dqn, HEADS * LANE, 0), (dkn, HEADS * LANE, 0), (dv_mla, 512, 0)],
        [w["qa_g"], w["kva_g"], w["wq"], w["wkv"], w["qn_g"], w["kn_g"]],
        [(MLA_Q_RANK, bf16), (LANE, bf16), (LANE, bf16)],
        [((MLA_Q_RANK, HEADS * LANE), f32), ((MLA_KV_RANK, HEADS * LANE + 512), f32), ((1, MLA_Q_RANK), f32),
         ((1, MLA_KV_RANK), f32), ((1, LANE), f32), ((1, LANE), f32)])
    grads["mla_w_q_up"] = dwq.reshape(MLA_Q_RANK, HEADS, LANE)[:, :, :MLA_QK].reshape(MLA_Q_RANK, HEADS * MLA_QK)
    dwk = dwkv[:, :HEADS * LANE].reshape(MLA_KV_RANK, HEADS, LANE)[:, :, :64]
    dwv = dwkv[:, HEADS * LANE:].reshape(MLA_KV_RANK, HEADS, 64)
    grads["mla_w_kv_up"] = jnp.concatenate([dwk, dwv], axis=2).reshape(MLA_KV_RANK, HEADS * 128)
    grads["mla_q_a_norm"], grads["mla_kv_a_norm"] = dqa.reshape(-1), dkva.reshape(-1)
    grads["mla_q_norm"], grads["mla_k_norm"] = dqng[0, :MLA_QK], dkng[0, :MLA_QK]

    _, _, shard_c, shard_cp = _BIG_SHARD["w_in"]
    kpe0 = _PAD["kpe"][2]
    pieces = [dcq, dckv, dkpe[:, kpe0:kpe0 + ROPE], dfq, dfk, dfvb, dff[:, :HEADS], ds5u, dg_mla, dg_fox, dg_s5,
              dm_mla, dm_fox, dm_s5]
    gap = jnp.zeros((n_tok, shard_cp - shard_c), bf16)
    cut, pos = [], 0
    for p in pieces:
        start = 0
        while start < p.shape[1]:
            take = min(p.shape[1] - start, shard_c - pos % shard_c)
            cut.append(p[:, start:start + take])
            start, pos = start + take, pos + take
            if pos % shard_c == 0:
                cut.append(gap)
    dproj = jnp.concatenate(cut, axis=1)
    ct = 256
    per = shard_cp // ct
    dh = _mm("in_proj_dgrad", dproj, w["w_in_shards"], mode="nt", grid=(1, 1, N_CHIPS * per),
             a_spec=pl.BlockSpec((n_tok, ct), lambda i, j, kk: (0, kk)),
             b_spec=pl.BlockSpec((None, D_MODEL, ct), lambda i, j, kk: (kk // per, 0, kk % per)),
             o_spec=pl.BlockSpec((n_tok, D_MODEL), lambda i, j, kk: (0, 0)),
             out_shape=jax.ShapeDtypeStruct((n_tok, D_MODEL), f32), acc_shape=(n_tok, D_MODEL))
    grads["w_in"] = _mm("in_proj_wgrad", sv["h"], dproj, mode="tn", grid=(1, N_CHIPS * per, 1),
                        a_spec=pl.BlockSpec((n_tok, D_MODEL), lambda i, j, kk: (0, 0)),
                        b_spec=pl.BlockSpec((n_tok, ct), lambda i, j, kk: (0, j)),
                        o_spec=pl.BlockSpec((None, D_MODEL, ct), lambda i, j, kk: (j // per, 0, j % per)),
                        out_shape=jax.ShapeDtypeStruct((N_CHIPS, D_MODEL, shard_cp), f32), acc_shape=(D_MODEL, ct))

    def norm_bwd_body(dh_ref, x_ref, do_ref, g_ref, dx_ref, dg_ref):
        xv = x_ref[...]
        r = lax.rsqrt(jnp.sum(xv * xv, axis=-1, keepdims=True) * (1.0 / D_MODEL) + EPS)
        dx, dg = _rms_bwd(dh_ref[...], xv, r, g_ref[...], D_MODEL)
        dx_ref[...] = do_ref[...] + dx
        _accumulate(dg_ref, dg)

    dx, dng = _rowwise("norm_bwd", norm_bwd_body, n_tok, [(dh, D_MODEL, 0), (x, D_MODEL, 0), (dout, D_MODEL, 0)],
                       [w["norm_g"]], [(D_MODEL, f32)], [((1, D_MODEL), f32)])
    grads["norm_g"] = dng.reshape(D_MODEL)
    return dx, grads


def _rope_tables(positions):
    inv = 1.0 / (ROPE_THETA ** (jnp.arange(0, ROPE, 2, dtype=f32) / ROPE))
    ang = positions.astype(f32).reshape(-1, 1) * inv
    cos, sin = jnp.cos(ang), jnp.sin(ang)
    n = ang.shape[0]
    z16, z32, z64 = jnp.zeros((n, 16), f32), jnp.zeros((n, 32), f32), jnp.zeros((n, 64), f32)
    c = jnp.concatenate([jnp.ones((n, 64), f32), cos, cos, z32], axis=1)
    s1 = jnp.concatenate([z64, -sin, z16, z32], axis=1)
    s2 = jnp.concatenate([z64, z16, sin, z32], axis=1)
    return c, s1, s2


BIG = ("w_in", "mla_w_q_up", "mla_w_kv_up", "s5_w_glu", "w_branch_out", "w_out")
SMALL = ("norm_g", "mla_q_a_norm", "mla_kv_a_norm", "mla_q_norm", "mla_k_norm", "fox_b_f", "fox_q_norm", "fox_k_norm",
         "s5_lambda_re", "s5_lambda_im", "s5_log_dt", "s5_b_re", "s5_b_im", "s5_c_re", "s5_c_im", "s5_d", "s5_b_glu")
WEIGHTS = ("norm_g", "w_in", "mla_q_a_norm", "mla_w_q_up", "mla_kv_a_norm", "mla_w_kv_up", "mla_q_norm", "mla_k_norm",
           "fox_b_f", "fox_q_norm", "fox_k_norm", "s5_lambda_re", "s5_lambda_im", "s5_log_dt", "s5_b_re", "s5_b_im",
           "s5_c_re", "s5_c_im", "s5_d", "s5_w_glu", "s5_b_glu", "w_branch_out", "w_out")


def _local_step(x, positions, loss_target, small, big):
    n_tok = x.shape[0]
    tabs = _rope_tables(positions)
    ws, saves = [], []
    hcur = x
    stacked = _prep_weights(small, big)
    for l in range(DEPTH):
        w = {k: v[l] for k, v in stacked.items()}
        hcur, sv = _layer_fwd(hcur, w, tabs, n_tok)
        ws.append(w)
        saves.append(sv)

    def loss_body(y_ref, t_ref, d_ref, l_ref):
        err = y_ref[...] - t_ref[...]
        d_ref[...] = err * (1.0 / D_MODEL)
        tot = jnp.sum(jnp.sum(err * err, axis=-1, keepdims=True), axis=0, keepdims=True)
        _accumulate(l_ref, jnp.broadcast_to(tot * (0.5 / D_MODEL), (1, LANE)))

    dcur, loss = _rowwise("loss", loss_body, n_tok, [(hcur, D_MODEL, 0), (loss_target, D_MODEL, 0)], [], [(D_MODEL, f32)],
                          [((1, LANE), f32)])
    layer_grads = [None] * DEPTH
    for l in reversed(range(DEPTH)):
        dcur, layer_grads[l] = _layer_bwd(dcur, ws[l], saves[l], tabs, n_tok)
    grads = {n: jnp.stack([layer_grads[l][n] for l in range(DEPTH)]) for n in WEIGHTS}
    return loss[0, 0], dcur, grads


N_DEV = 8
_ANY = pl.BlockSpec(memory_space=pl.ANY)
_MESH = pl.DeviceIdType.MESH


def _all_gather8(name, blk):
    m = blk.shape[0]

    def body(x_ref, out_ref, send_sems, recv_sems, local_sem):
        x, y, c = lax.axis_index("x"), lax.axis_index("y"), lax.axis_index("c")
        me, sibling = (x, y, c), (x, y, 1 - c)
        chips = [(1 - x, y), (x, 1 - y), (1 - x, 1 - y)]

        def slot(px, py, pc):
            return out_ref.at[4 * px + 2 * py + pc]

        def copy(k, block, to, src=None):
            return pltpu.make_async_remote_copy(
                src_ref=slot(*block) if src is None else src, dst_ref=slot(*block),
                send_sem=send_sems.at[k], recv_sem=recv_sems.at[k], device_id=to, device_id_type=_MESH)

        mine = pltpu.make_async_copy(x_ref, slot(*me), local_sem)
        mine.start()
        first = [copy(0, me, sibling, src=x_ref)]
        first += [copy(1 + j, me, (*chip, c), src=x_ref) for j, chip in enumerate(chips)]
        for cp in first:
            cp.start()
        passed = [copy(4 + j, (*chip, c), sibling) for j, chip in enumerate(chips)]
        for j, chip in enumerate(chips):
            copy(1 + j, (*chip, c), me).wait_recv()
            passed[j].start()
        copy(0, sibling, me).wait_recv()
        for j, chip in enumerate(chips):
            copy(4 + j, (*chip, 1 - c), me).wait_recv()
        for cp in first + passed:
            cp.wait_send()
        mine.wait()

    return pl.pallas_call(
        body, out_shape=jax.ShapeDtypeStruct((N_DEV, m, LANE), blk.dtype), in_specs=[_ANY], out_specs=_ANY, name=name,
        scratch_shapes=[pltpu.SemaphoreType.DMA((7,)), pltpu.SemaphoreType.DMA((7,)), pltpu.SemaphoreType.DMA],
    )(blk)


def _gather_layers(name, shards):
    n = len(shards)

    def body(*refs):
        x_refs, out_refs = refs[:n], refs[n:2 * n]
        send_sems, recv_sems, local_sems = refs[2 * n:]
        x, y, c = lax.axis_index("x"), lax.axis_index("y"), lax.axis_index("c")
        me, sibling = (x, y, c), (x, y, 1 - c)
        xn, yn, dg = (1 - x, y, c), (x, 1 - y, c), (1 - x, 1 - y, c)
        relay_from = (x + (1 - c) * (1 - 2 * x), y + c * (1 - 2 * y), c)
        relay_to = (x + c * (1 - 2 * x), y + (1 - c) * (1 - 2 * y), c)

        def copy(w, k, block, to, src=None):
            px, py, pc = block
            slot = out_refs[w].at[pc, 2 * px + py]
            return pltpu.make_async_remote_copy(
                src_ref=slot if src is None else src, dst_ref=slot, send_sem=send_sems.at[7 * w + k],
                recv_sem=recv_sems.at[7 * w + k], device_id=to, device_id_type=_MESH)

        started, local = [], []
        for w in range(n):
            src = x_refs[w].at[c]
            mine = pltpu.make_async_copy(src, out_refs[w].at[c, 2 * x + y], local_sems.at[w])
            mine.start()
            local.append(mine)
            first = [copy(w, 0, me, sibling, src=src), copy(w, 1, me, xn, src=src), copy(w, 2, me, yn, src=src)]
            for cp in first:
                cp.start()
            started += first
        for w in range(n):
            copy(w, 1, xn, me).wait_recv()
            copy(w, 2, yn, me).wait_recv()
            onward = [copy(w, 3, relay_from, relay_to), copy(w, 4, xn, sibling), copy(w, 5, yn, sibling)]
            for cp in onward:
                cp.start()
            started += onward
        for w in range(n):
            copy(w, 3, dg, me).wait_recv()
            onward = copy(w, 6, dg, sibling)
            onward.start()
            started.append(onward)
        for w in range(n):
            copy(w, 0, sibling, me).wait_recv()
            for k, chip in ((4, xn), (5, yn), (6, dg)):
                copy(w, k, (chip[0], chip[1], 1 - c), me).wait_recv()
        for cp in started:
            cp.wait_send()
        for cp in local:
            cp.wait()

    return pl.pallas_call(
        body, out_shape=[jax.ShapeDtypeStruct((2, N_CHIPS) + s.shape[1:], s.dtype) for s in shards],
        in_specs=[_ANY] * n, out_specs=[_ANY] * n, name=name,
        scratch_shapes=[pltpu.SemaphoreType.DMA((7 * n,)), pltpu.SemaphoreType.DMA((7 * n,)), pltpu.SemaphoreType.DMA((n,))],
    )(*shards)


def _swap_layers(name, parts):
    n = len(parts)

    def body(*refs):
        p_refs, got_refs = refs[:n], refs[n:2 * n]
        send_sems, recv_sems = refs[2 * n:]
        x, y, c = lax.axis_index("x"), lax.axis_index("y"), lax.axis_index("c")
        copies = []
        for w in range(n):
            cp = pltpu.make_async_remote_copy(
                src_ref=p_refs[w].at[1 - c], dst_ref=got_refs[w], send_sem=send_sems.at[w], recv_sem=recv_sems.at[w],
                device_id=(x, y, 1 - c), device_id_type=_MESH)
            cp.start()
            copies.append(cp)
        for cp in copies:
            cp.wait()

    return pl.pallas_call(
        body, out_shape=[jax.ShapeDtypeStruct(p.shape[1:], p.dtype) for p in parts], in_specs=[_ANY] * n, out_specs=[_ANY] * n,
        name=name, scratch_shapes=[pltpu.SemaphoreType.DMA((n,)), pltpu.SemaphoreType.DMA((n,))],
    )(*parts)


def _scatter_to_chips(name, parts):
    n = len(parts)

    def body(*refs):
        p_refs, out_refs = refs[:n], refs[n:2 * n]
        send_sems, recv_sems, local_sems = refs[2 * n:]
        x, y, c = lax.axis_index("x"), lax.axis_index("y"), lax.axis_index("c")
        jme = 2 * x + y
        chips = [(1 - x, y), (x, 1 - y), (1 - x, 1 - y)]
        sends, local = [], []
        for w in range(n):
            mine = pltpu.make_async_copy(p_refs[w].at[jme], out_refs[w].at[jme], local_sems.at[w])
            mine.start()
            local.append(mine)
            for k, (tx, ty) in enumerate(chips):
                cp = pltpu.make_async_remote_copy(
                    src_ref=p_refs[w].at[2 * tx + ty], dst_ref=out_refs[w].at[jme], send_sem=send_sems.at[3 * w + k],
                    recv_sem=recv_sems.at[3 * w + k], device_id=(tx, ty, c), device_id_type=_MESH)
                cp.start()
                sends.append(cp)
        for w in range(n):
            for k, (tx, ty) in enumerate(chips):
                pltpu.make_async_remote_copy(
                    src_ref=p_refs[w].at[jme], dst_ref=out_refs[w].at[2 * tx + ty], send_sem=send_sems.at[3 * w + k],
                    recv_sem=recv_sems.at[3 * w + k], device_id=(tx, ty, c), device_id_type=_MESH).wait_recv()
        for cp in sends:
            cp.wait_send()
        for cp in local:
            cp.wait()

    return pl.pallas_call(
        body, out_shape=[jax.ShapeDtypeStruct(p.shape, p.dtype) for p in parts], in_specs=[_ANY] * n, out_specs=[_ANY] * n, name=name,
        scratch_shapes=[pltpu.SemaphoreType.DMA((3 * n,)), pltpu.SemaphoreType.DMA((3 * n,)), pltpu.SemaphoreType.DMA((n,))],
    )(*parts)


def _share_layers(name, bufs):
    n = len(bufs)

    def body(*refs):
        out_refs = refs[n:2 * n]
        send_sems, recv_sems = refs[2 * n:]
        x, y, c = lax.axis_index("x"), lax.axis_index("y"), lax.axis_index("c")
        copies = []
        for w in range(n):
            cp = pltpu.make_async_remote_copy(src_ref=out_refs[w].at[c], dst_ref=out_refs[w].at[c], send_sem=send_sems.at[w],
                                              recv_sem=recv_sems.at[w], device_id=(x, y, 1 - c), device_id_type=_MESH)
            cp.start()
            copies.append(cp)
        for w in range(n):
            pltpu.make_async_remote_copy(src_ref=out_refs[w].at[c], dst_ref=out_refs[w].at[1 - c], send_sem=send_sems.at[w],
                                         recv_sem=recv_sems.at[w], device_id=(x, y, 1 - c), device_id_type=_MESH).wait_recv()
        for cp in copies:
            cp.wait_send()

    return pl.pallas_call(
        body, out_shape=[jax.ShapeDtypeStruct(b.shape, b.dtype) for b in bufs], in_specs=[_ANY] * n, out_specs=[_ANY] * n,
        input_output_aliases={w: w for w in range(n)}, name=name,
        scratch_shapes=[pltpu.SemaphoreType.DMA((n,)), pltpu.SemaphoreType.DMA((n,))],
    )(*bufs)


def _row_tile(rows, cols):
    best = 16
    for t in range(16, rows + 1, 16):
        if rows % t == 0 and t * cols * 4 <= 2 * 1024 * 1024:
            best = t
    return best


def _add_pair(name, core, parts, got, out_dtype):
    _, _, r, c = parts.shape
    t = _row_tile(r, c)

    def body(core_ref, a_ref, b_ref, o_ref):
        o_ref[...] = (a_ref[...] + b_ref[...]).astype(o_ref.dtype)

    spec = pl.BlockSpec((None, t, c), lambda j, i, core_ref: (j, i, 0))
    grid_spec = pltpu.PrefetchScalarGridSpec(
        num_scalar_prefetch=1, grid=(N_CHIPS, r // t),
        in_specs=[pl.BlockSpec((None, None, t, c), lambda j, i, core_ref: (core_ref[0], j, i, 0)), spec], out_specs=spec)
    return pl.pallas_call(body, grid_spec=grid_spec, out_shape=jax.ShapeDtypeStruct(got.shape, out_dtype), name=name,
                          compiler_params=pltpu.CompilerParams(dimension_semantics=("arbitrary", "arbitrary")))(core, parts, got)


def _add_four(name, core, a):
    _, r, c = a.shape
    t = _row_tile(r, c)

    def body(core_ref, a0, a1, a2, a3, o_ref):
        o_ref[...] = ((a0[...].astype(f32) + a1[...].astype(f32)) + a2[...].astype(f32)) + a3[...].astype(f32)

    specs = [pl.BlockSpec((None, t, c), functools.partial(lambda i, core_ref, k: (k, i, 0), k=k)) for k in range(N_CHIPS)]
    grid_spec = pltpu.PrefetchScalarGridSpec(
        num_scalar_prefetch=1, grid=(r // t,), in_specs=specs,
        out_specs=pl.BlockSpec((None, t, c), lambda i, core_ref: (core_ref[0], i, 0)))
    return pl.pallas_call(body, grid_spec=grid_spec, out_shape=jax.ShapeDtypeStruct((2, r, c), f32), name=name,
                          compiler_params=pltpu.CompilerParams(dimension_semantics=("arbitrary",)))(core, a, a, a, a)


def _adamw(name, w, g, m, v, row_tile=None, lead_tile=None):
    c1 = 1.0 - ADAM_B1 ** ADAM_STEP
    c2 = 1.0 - ADAM_B2 ** ADAM_STEP

    def body(w_ref, g_ref, m_ref, v_ref, d_ref, nm_ref, nv_ref):
        gv = g_ref[...]
        nm = ADAM_B1 * m_ref[...] + (1.0 - ADAM_B1) * gv
        nv = ADAM_B2 * v_ref[...] + (1.0 - ADAM_B2) * (gv * gv)
        m_hat = nm / c1
        v_hat = nv / c2
        d_ref[...] = -ADAM_LR * (m_hat / (jnp.sqrt(v_hat) + ADAM_EPS) + ADAM_WD * w_ref[...])
        nm_ref[...] = nm
        nv_ref[...] = nv

    sds = jax.ShapeDtypeStruct(w.shape, f32)
    if lead_tile is not None:
        spec = pl.BlockSpec((lead_tile,) + w.shape[1:], lambda i: (i, 0, 0))
        return pl.pallas_call(body, grid=(w.shape[0] // lead_tile,), in_specs=[spec] * 4, out_specs=[spec] * 3, out_shape=[sds] * 3,
                              name=name, compiler_params=pltpu.CompilerParams(dimension_semantics=("arbitrary",), vmem_limit_bytes=VMEM_LIMIT),
                              )(w, g, m, v)
    if row_tile is None:
        return pl.pallas_call(body, out_shape=[sds] * 3, name=name)(w, g, m, v)
    _, r, c = w.shape
    spec = pl.BlockSpec((None, row_tile, c), lambda l, i: (l, i, 0))
    return pl.pallas_call(body, grid=(DEPTH, r // row_tile), in_specs=[spec] * 4, out_specs=[spec] * 3, out_shape=[sds] * 3, name=name,
                          compiler_params=pltpu.CompilerParams(dimension_semantics=("arbitrary", "arbitrary"), vmem_limit_bytes=VMEM_LIMIT),
                          )(w, g, m, v)


def _pad_rows(flat, rows):
    return jnp.pad(flat, (0, rows * LANE - flat.shape[0])).reshape(rows, LANE)


def kernel(x, positions, norm_g, w_in, mla_q_a_norm, mla_w_q_up, mla_kv_a_norm, mla_w_kv_up, mla_q_norm, mla_k_norm, fox_b_f, fox_q_norm, fox_k_norm, s5_lambda_re, s5_lambda_im, s5_log_dt, s5_b_re, s5_b_im, s5_c_re, s5_c_im, s5_d, s5_w_glu, s5_b_glu, w_branch_out, w_out, loss_target, m_norm_g, m_w_in, m_mla_q_a_norm, m_mla_w_q_up, m_mla_kv_a_norm, m_mla_w_kv_up, m_mla_q_norm, m_mla_k_norm, m_fox_b_f, m_fox_q_norm, m_fox_k_norm, m_s5_lambda_re, m_s5_lambda_im, m_s5_log_dt, m_s5_b_re, m_s5_b_im, m_s5_c_re, m_s5_c_im, m_s5_d, m_s5_w_glu, m_s5_b_glu, m_w_branch_out, m_w_out, v_norm_g, v_w_in, v_mla_q_a_norm, v_mla_w_q_up, v_mla_kv_a_norm, v_mla_w_kv_up, v_mla_q_norm, v_mla_k_norm, v_fox_b_f, v_fox_q_norm, v_fox_k_norm, v_s5_lambda_re, v_s5_lambda_im, v_s5_log_dt, v_s5_b_re, v_s5_b_im, v_s5_c_re, v_s5_c_im, v_s5_d, v_s5_w_glu, v_s5_b_glu, v_w_branch_out, v_w_out):
    given = dict(locals())
    wts = {n: given[n] for n in WEIGHTS}
    mom1 = {n: given["m_" + n] for n in WEIGHTS}
    mom2 = {n: given["v_" + n] for n in WEIGHTS}

    def lanes(n, a):
        _, _, c, cp = _BIG_SHARD[n]
        return jnp.pad(a, ((0, 0), (0, 0), (0, cp - c)))

    gathered = _gather_layers("gather_weights", [lanes(n, wts[n].astype(bf16)) for n in BIG])
    big = dict(zip(BIG, gathered))
    small = {n: wts[n] for n in SMALL}

    loss_local, grad_x, grads = _local_step(x[0], positions, loss_target[0], small, big)
    loss = lax.psum(loss_local, ("x", "y", "c"))

    small_flat = jnp.concatenate([grads[n].reshape(-1) for n in SMALL])
    small_rows = -(-small_flat.shape[0] // (N_DEV * 16 * LANE)) * 16
    parts = [grads[n] if n == "w_in" else jnp.stack([_to_shards(n, grads[n][l]) for l in range(DEPTH)]) for n in BIG]
    parts.append(jnp.swapaxes(_pad_rows(small_flat, N_DEV * small_rows).reshape(N_CHIPS, 2, small_rows, LANE), 0, 1))
    core = lax.axis_index("c")
    core1 = core.reshape(1).astype(jnp.int32)
    got = _swap_layers("grads_to_sibling", parts)
    hop = [bf16] * len(BIG) + [f32]
    pair = [_add_pair("grads_pair_sum_%d" % i, core1, a, b, dt) for i, (a, b, dt) in enumerate(zip(parts, got, hop))]
    landed = _scatter_to_chips("grads_to_chips", pair)
    total = [_add_four("grads_chip_sum_%d" % i, core1, a) for i, a in enumerate(landed)]
    shared = _share_layers("grads_share", total[:-1])
    small_mine = lax.dynamic_index_in_dim(total[-1], core, 0, keepdims=False)
    small_all = _all_gather8("gather_small_grads", small_mine).reshape(-1)

    g_out = {n: s[:, :, :_BIG_SHARD[n][2]] for n, s in zip(BIG, shared)}
    pos = 0
    for n in SMALL:
        g_out[n] = small_all[pos:pos + wts[n].size].reshape(wts[n].shape)
        pos += wts[n].size

    delta, new_m, new_v = {}, {}, {}
    for n in WEIGHTS:
        if n == "w_in":
            cols_first = lambda a: jnp.transpose(a, (2, 0, 1))
            res = _adamw("adamw_" + n, *[cols_first(a) for a in (wts[n], g_out[n], mom1[n], mom2[n])], lead_tile=177)
            delta[n], new_m[n], new_v[n] = [jnp.transpose(a, (1, 2, 0)) for a in res]
            continue
        row_tile = _row_tile(*wts[n].shape[1:]) if n in BIG else None
        delta[n], new_m[n], new_v[n] = _adamw("adamw_" + n, wts[n], g_out[n], mom1[n], mom2[n], row_tile)

    return (loss, grad_x[None], *[g_out[n] for n in WEIGHTS], *[delta[n] for n in WEIGHTS],
            *[new_m[n] for n in WEIGHTS], *[new_v[n] for n in WEIGHTS])
```

```python
import functools
import math

import jax
import jax.numpy as jnp
from jax import lax
from jax.experimental import pallas as pl
from jax.experimental.pallas import tpu as pltpu

f32 = jnp.float32
bf16 = jnp.bfloat16

D_MODEL = 1024
DEPTH = 2
EPS = 1e-6
HEADS = 8
MLA_QK = 96
MLA_Q_RANK = 256
MLA_KV_RANK = 128
ROPE = 32
ROPE_THETA = 10000.0
FOX_DIM = 64
S5_GROUPS = 32
S5_GROUP = 16
S5_STATE = 64
S5_LANES = S5_GROUPS * S5_STATE
LANE = 128
S5_BLOCKS = S5_LANES // LANE
TOK = 256
VMEM_LIMIT = 56 * 1024 * 1024

ADAM_LR = 0.001
ADAM_B1 = 0.9
ADAM_B2 = 0.999
ADAM_EPS = 1e-08
ADAM_WD = 0.01
ADAM_STEP = 10

_ORIG = {}
_off = 0
for _n, _w in (("cq", 256), ("ckv", 128), ("kpe", 32), ("fq", 512), ("fk", 512), ("fv", 512), ("ff", 8), ("s5u", 512),
               ("g_mla", 512), ("g_fox", 512), ("g_s5", 512), ("m_mla", 1024), ("m_fox", 1024), ("m_s5", 1024)):
    _ORIG[_n] = (_off, _w)
    _off += _w
_PAD = {"m_mla": (0, 1024, 0), "m_fox": (1024, 1024, 0), "m_s5": (2048, 1024, 0),
        "fq": (3072, 512, 0), "fk": (3584, 512, 0), "fv": (4096, 512, 0), "s5u": (4608, 512, 0),
        "g_mla": (5120, 512, 0), "g_fox": (5632, 512, 0), "g_s5": (6144, 512, 0),
        "cq": (6656, 256, 0), "ckv": (6912, 128, 0), "kpe": (7040, 128, 64), "ff": (7168, 128, 0)}
NP = 7680
_PAD_ORDER = ("m_mla", "m_fox", "m_s5", "fq", "fk", "fv", "s5u", "g_mla", "g_fox", "g_s5", "cq", "ckv", "kpe", "ff")


def _seg(name):
    start, width, _ = _PAD[name]
    return width, start // width


def _nn(a, b):
    return lax.dot_general(a.astype(bf16), b.astype(bf16), (((1,), (0,)), ((), ())), preferred_element_type=f32)


def _nt(a, b):
    return lax.dot_general(a.astype(bf16), b.astype(bf16), (((1,), (1,)), ((), ())), preferred_element_type=f32)


def _tn(a, b):
    return lax.dot_general(a.astype(bf16), b.astype(bf16), (((0,), (0,)), ((), ())), preferred_element_type=f32)


def _rms(x, g, n):
    r = lax.rsqrt(jnp.sum(x * x, axis=-1, keepdims=True) * (1.0 / n) + EPS)
    return x * r * g, r


def _rms_bwd(dy, x, r, g, n):
    xh = x * r
    dg = jnp.sum(dy * xh, axis=0, keepdims=True)
    dxh = dy * g
    dx = r * (dxh - xh * (jnp.sum(dxh * xh, axis=-1, keepdims=True) * (1.0 / n)))
    return dx, dg


def _sigmoid(x):
    return 1.0 / (1.0 + jnp.exp(-x))


_GELU_C = math.sqrt(2.0 / math.pi)


def _gelu(x):
    t = jnp.tanh(_GELU_C * (x + 0.044715 * x * x * x))
    return 0.5 * x * (1.0 + t), t


def _gelu_grad(x, t):
    return 0.5 * (1.0 + t) + 0.5 * x * (1.0 - t * t) * _GELU_C * (1.0 + 3.0 * 0.044715 * x * x)


def _accumulate(ref, val):
    i = pl.program_id(0)

    @pl.when(i == 0)
    def _():
        ref[...] = val

    @pl.when(i > 0)
    def _():
        ref[...] += val


def _rope(x, c, s1, s2):
    return x * c + pltpu.roll(x, LANE - 16, 1) * s1 + pltpu.roll(x, 16, 1) * s2


def _rope_t(d, c, s1, s2):
    return d * c + pltpu.roll(d * s1, 16, 1) + pltpu.roll(d * s2, LANE - 16, 1)


def _const_map(ndim):
    return lambda *_: (0,) * ndim


def _rowwise(name, body, n_tok, tiled_in, full_in, tiled_out, acc_out, tile=TOK):
    in_specs, args = [], []
    for arr, width, blk in tiled_in:
        in_specs.append(pl.BlockSpec((tile, width), functools.partial(lambda i, b: (i, b), b=blk)))
        args.append(arr)
    for arr in full_in:
        in_specs.append(pl.BlockSpec(arr.shape, _const_map(arr.ndim)))
        args.append(arr)
    out_specs, out_shape = [], []
    for width, dt in tiled_out:
        out_specs.append(pl.BlockSpec((tile, width), lambda i: (i, 0)))
        out_shape.append(jax.ShapeDtypeStruct((n_tok, width), dt))
    for shape, dt in acc_out:
        out_specs.append(pl.BlockSpec(shape, _const_map(len(shape))))
        out_shape.append(jax.ShapeDtypeStruct(shape, dt))
    return pl.pallas_call(
        body, grid=(n_tok // tile,), in_specs=in_specs, out_specs=out_specs, out_shape=out_shape, name=name,
        compiler_params=pltpu.CompilerParams(dimension_semantics=("arbitrary",), vmem_limit_bytes=VMEM_LIMIT),
    )(*args)


def _mm(name, a, b, *, mode, grid, a_spec, b_spec, o_spec, out_shape, acc_shape, add=None, add_spec=None):
    nk = grid[2]

    def body(*refs):
        if add is None:
            a_ref, b_ref, o_ref, acc_ref = refs
        else:
            a_ref, b_ref, add_ref, o_ref, acc_ref = refs
        k = pl.program_id(2)

        @pl.when(k == 0)
        def _():
            acc_ref[...] = jnp.zeros_like(acc_ref)

        acc_ref[...] += {"nn": _nn, "nt": _nt, "tn": _tn}[mode](a_ref[...], b_ref[...])

        @pl.when(k == nk - 1)
        def _():
            r = acc_ref[...]
            if add is not None:
                r = r + add_ref[...]
            o_ref[...] = r.astype(o_ref.dtype)

    in_specs = [a_spec, b_spec] + ([add_spec] if add is not None else [])
    args = (a, b) + ((add,) if add is not None else ())
    return pl.pallas_call(
        body, grid=grid, in_specs=in_specs, out_specs=o_spec, out_shape=out_shape, name=name,
        scratch_shapes=[pltpu.VMEM(acc_shape, f32)],
        compiler_params=pltpu.CompilerParams(dimension_semantics=("arbitrary", "arbitrary", "arbitrary"), vmem_limit_bytes=VMEM_LIMIT),
    )(*args)


def _mm_nn(name, a, b, *, m, n, k, tm, tn, tk, out_dtype=f32, a_koff=0):
    return _mm(name, a, b, mode="nn", grid=(m // tm, n // tn, k // tk),
               a_spec=pl.BlockSpec((tm, tk), lambda i, j, kk: (i, kk + a_koff)),
               b_spec=pl.BlockSpec((tk, tn), lambda i, j, kk: (kk, j)),
               o_spec=pl.BlockSpec((tm, tn), lambda i, j, kk: (i, j)),
               out_shape=jax.ShapeDtypeStruct((m, n), out_dtype), acc_shape=(tm, tn))


ATT_KV = 256
ATT_Q = 2048


def _attn_common(mla, n_tok):
    qw = 2 * LANE if mla else LANE
    scale = 1.0 / math.sqrt(MLA_QK if mla else FOX_DIM)
    return qw, scale, min(ATT_Q, n_tok)


def _attn_heads(q_ref, mla):
    out = []
    if mla:
        for e in (0, 1):
            qe = q_ref[:, e * LANE:(e + 1) * LANE]
            out.append((qe.astype(f32).T.astype(bf16), qe))
        return out
    q = q_ref[...]
    tq = q.shape[0]
    qt = q.astype(f32).T
    row = lax.broadcasted_iota(jnp.int32, (LANE, tq), 0)
    lane = lax.broadcasted_iota(jnp.int32, (tq, LANE), 1)
    for e in (0, 1):
        out.append((jnp.where((row >= 64) == bool(e), qt, 0.0).astype(bf16),
                    jnp.where((lane >= 64) == bool(e), q, jnp.zeros((), bf16))))
    return out


def _attn_allowed(off, i, tq, mla):
    kpos = off + lax.broadcasted_iota(jnp.int32, (ATT_KV, tq), 0)
    qpos = i * tq + lax.broadcasted_iota(jnp.int32, (ATT_KV, tq), 1)
    return ((kpos // 64) <= (qpos // 64)) if mla else (kpos <= qpos)


def _attn_fwd(name, q, k, v, cum_b, *, mla, n_tok):
    qw, scale, tq = _attn_common(mla, n_tok)
    nq = n_tok // tq
    nkv = n_tok // ATT_KV
    has_bias = cum_b is not None

    def body(*refs):
        if has_bias:
            q_ref, k_ref, v_ref, cb_ref, o_ref, lse_ref, vt_ref = refs
        else:
            q_ref, k_ref, v_ref, o_ref, lse_ref, vt_ref = refs
        i = pl.program_id(1)

        @pl.when(i == 0)
        def _():
            for jb in range(nkv):
                vt_ref[jb] = v_ref[jb * ATT_KV:(jb + 1) * ATT_KV, :].astype(f32).T.astype(bf16)

        heads = _attn_heads(q_ref, mla)

        def step(j, carry, masked, q_lo=0):
            off = pl.multiple_of(j * ATT_KV, ATT_KV)
            allowed = _attn_allowed(off, i, tq, mla)[:, q_lo:] if masked else None
            keep = lambda old, part: part if q_lo == 0 else jnp.concatenate([old[:, :q_lo], part], axis=1)
            vt = vt_ref[j]
            sts = []
            for e in (0, 1):
                kb = k_ref[pl.ds(off, ATT_KV), e * LANE:(e + 1) * LANE] if mla else k_ref[pl.ds(off, ATT_KV), :]
                sts.append(_nn(kb, heads[e][0][:, q_lo:]))
            stats = []
            for e in (0, 1):
                m, l = carry[e][0][:, q_lo:], carry[e][1][:, q_lo:]
                st = sts[e] * scale
                if has_bias:
                    st = st - jnp.tile(cb_ref[e, pl.ds(off, ATT_KV), :], (1, (tq - q_lo) // LANE))
                if masked:
                    st = jnp.where(allowed, st, -1e30)
                m_new = jnp.maximum(m, jnp.max(st, axis=0, keepdims=True))
                alpha = jnp.exp(m - m_new)
                pt = jnp.exp(st - m_new)
                stats.append((m_new, alpha * l + jnp.sum(pt, axis=0, keepdims=True), alpha, pt.astype(bf16)))
            new = []
            for e in (0, 1):
                m_new, l, alpha, pt = stats[e]
                acc = alpha * carry[e][2][:, q_lo:] + _nn(vt[64 * e:64 * e + 64, :], pt)
                new.append((keep(carry[e][0], m_new), keep(carry[e][1], l), keep(carry[e][2], acc)))
            return tuple(new)

        init = tuple((jnp.full((1, tq), -1e30, f32), jnp.zeros((1, tq), f32), jnp.zeros((64, tq), f32)) for _ in (0, 1))
        n_full = i * (tq // ATT_KV)
        carry = lax.fori_loop(0, n_full, functools.partial(step, masked=False), init)
        for d in range(tq // ATT_KV):
            carry = step(n_full + d, carry, True, q_lo=d * ATT_KV)
        o_ref[...] = jnp.concatenate([carry[e][2] / carry[e][1] for e in (0, 1)], axis=0).T
        lse_ref[...] = jnp.zeros_like(lse_ref)
        for e in (0, 1):
            lse_ref[e:e + 1, :] = carry[e][0] + jnp.log(carry[e][1])

    in_specs = [pl.BlockSpec((tq, qw), lambda p, i: (i, p)),
                pl.BlockSpec((n_tok, qw), lambda p, i: (0, p)),
                pl.BlockSpec((n_tok, LANE), lambda p, i: (0, p))]
    args = [q, k, v]
    if has_bias:
        in_specs.append(pl.BlockSpec((2, n_tok, LANE), lambda p, i: (p, 0, 0)))
        args.append(cum_b)
    return pl.pallas_call(
        body, grid=(4, nq), in_specs=in_specs,
        out_specs=[pl.BlockSpec((tq, LANE), lambda p, i: (i, p)), pl.BlockSpec((None, 8, tq), lambda p, i: (p, 0, i))],
        out_shape=[jax.ShapeDtypeStruct((n_tok, 512), f32), jax.ShapeDtypeStruct((4, 8, n_tok), f32)], name=name,
        scratch_shapes=[pltpu.VMEM((nkv, LANE, ATT_KV), bf16)],
        compiler_params=pltpu.CompilerParams(dimension_semantics=("arbitrary", "arbitrary"), vmem_limit_bytes=VMEM_LIMIT),
    )(*args)


def _attn_bwd(name, q, k, v, o, lse, do, cum_b, *, mla, n_tok):
    qw, scale, tq = _attn_common(mla, n_tok)
    nq = n_tok // tq
    nkv = n_tok // ATT_KV
    has_bias = cum_b is not None

    def body(*refs):
        if has_bias:
            q_ref, k_ref, v_ref, o_ref, lse_ref, do_ref, cb_ref, dq_ref, dk_ref, dv_ref, dck_ref, dcq_ref, kt_ref = refs
        else:
            q_ref, k_ref, v_ref, o_ref, lse_ref, do_ref, dq_ref, dk_ref, dv_ref, kt_ref = refs
        p = pl.program_id(0)
        i = pl.program_id(1)

        @pl.when(i == 0)
        def _():
            dk_ref[...] = jnp.zeros_like(dk_ref)
            dv_ref[...] = jnp.zeros_like(dv_ref)
            for jb in range(nkv):
                for c0 in range(0, qw, LANE):
                    kt_ref[jb, c0:c0 + LANE, :] = k_ref[jb * ATT_KV:(jb + 1) * ATT_KV, c0:c0 + LANE].astype(f32).T.astype(bf16)

        if has_bias:
            @pl.when(jnp.logical_and(i == 0, p == 0))
            def _():
                dck_ref[...] = jnp.zeros_like(dck_ref)

        heads = _attn_heads(q_ref, mla)
        do = do_ref[...]
        do_t = do.T
        prod_t = (do * o_ref[...]).T
        row = lax.broadcasted_iota(jnp.int32, (LANE, tq), 0)
        lane = lax.broadcasted_iota(jnp.int32, (tq, LANE), 1)
        lane_k = lax.broadcasted_iota(jnp.int32, (ATT_KV, LANE), 1)
        per_head = []
        for e in (0, 1):
            sel_r = (row >= 64) == bool(e)
            per_head.append((jnp.where(sel_r, do_t, 0.0).astype(bf16),
                             jnp.where((lane >= 64) == bool(e), do, 0.0).astype(bf16),
                             jnp.sum(jnp.where(sel_r, prod_t, 0.0), axis=0, keepdims=True),
                             lse_ref[e:e + 1, :]))
        dq_rows = LANE if mla else 64

        def step(j, carry, masked, q_lo=0):
            off = pl.multiple_of(j * ATT_KV, ATT_KV)
            allowed = _attn_allowed(off, i, tq, mla)[:, q_lo:] if masked else None
            keep = lambda old, part: part if q_lo == 0 else jnp.concatenate([old[:, :q_lo], part], axis=1)
            vb = v_ref[pl.ds(off, ATT_KV), :]
            kt = kt_ref[j]
            cols = [slice(e * LANE, (e + 1) * LANE) if mla else slice(None) for e in (0, 1)]
            sts = [_nn(k_ref[pl.ds(off, ATT_KV), cols[e]], heads[e][0][:, q_lo:]) for e in (0, 1)]
            dpts = [_nn(vb, per_head[e][0][:, q_lo:]) for e in (0, 1)]
            mids = []
            for e in (0, 1):
                _, _, delta, lse_e = per_head[e]
                st = sts[e] * scale
                if has_bias:
                    st = st - jnp.tile(cb_ref[e, pl.ds(off, ATT_KV), :], (1, (tq - q_lo) // LANE))
                pt = jnp.exp(st - lse_e[:, q_lo:])
                if masked:
                    pt = jnp.where(allowed, pt, 0.0)
                dst = pt * (dpts[e] - delta[:, q_lo:])
                qsum = carry[e][1][:, q_lo:]
                if has_bias:
                    rs = jnp.sum(dst, axis=1, keepdims=True)
                    dck_ref[pl.ds(off, ATT_KV), :] += jnp.where(lane_k == 2 * p + e, -rs, 0.0)
                    qsum = qsum + jnp.sum(dst, axis=0, keepdims=True)
                mids.append((pt.astype(bf16), dst.astype(bf16), qsum))
            new = []
            for e in (0, 1):
                pt, dst, qsum = mids[e]
                kt_e = kt[e * LANE:(e + 1) * LANE, :] if mla else kt[64 * e:64 * e + 64, :]
                dq = carry[e][0][:, q_lo:] + _nn(kt_e, dst) * scale
                new.append((keep(carry[e][0], dq), keep(carry[e][1], qsum)))
                dk_ref[pl.ds(off, ATT_KV), cols[e]] += _nn(dst, heads[e][1][q_lo:, :]) * scale
                dv_ref[pl.ds(off, ATT_KV), :] += _nn(pt, per_head[e][1][q_lo:, :])
            return tuple(new)

        init = tuple((jnp.zeros((dq_rows, tq), f32), jnp.zeros((1, tq), f32)) for _ in (0, 1))
        n_full = i * (tq // ATT_KV)
        carry = lax.fori_loop(0, n_full, functools.partial(step, masked=False), init)
        for d in range(tq // ATT_KV):
            carry = step(n_full + d, carry, True, q_lo=d * ATT_KV)
        if mla:
            for e in (0, 1):
                dq_ref[:, e * LANE:(e + 1) * LANE] = carry[e][0].T
        else:
            dq_ref[...] = jnp.concatenate([carry[0][0], carry[1][0]], axis=0).T
        if has_bias:
            dcq_ref[...] = jnp.zeros_like(dcq_ref)
            for e in (0, 1):
                dcq_ref[e:e + 1, :] = carry[e][1]

    tile_q = pl.BlockSpec((tq, qw), lambda p, i: (i, p))
    tile_v = pl.BlockSpec((tq, LANE), lambda p, i: (i, p))
    full_k = pl.BlockSpec((n_tok, qw), lambda p, i: (0, p))
    full_v = pl.BlockSpec((n_tok, LANE), lambda p, i: (0, p))
    in_specs = [tile_q, full_k, full_v, tile_v, pl.BlockSpec((None, 8, tq), lambda p, i: (p, 0, i)), tile_v]
    args = [q, k, v, o, lse, do]
    out_specs = [tile_q, full_k, full_v]
    out_shape = [jax.ShapeDtypeStruct((n_tok, 4 * qw), f32), jax.ShapeDtypeStruct((n_tok, 4 * qw), f32),
                 jax.ShapeDtypeStruct((n_tok, 512), f32)]
    if has_bias:
        in_specs.append(pl.BlockSpec((2, n_tok, LANE), lambda p, i: (p, 0, 0)))
        args.append(cum_b)
        out_specs += [pl.BlockSpec((n_tok, LANE), _const_map(2)), pl.BlockSpec((None, 8, tq), lambda p, i: (p, 0, i))]
        out_shape += [jax.ShapeDtypeStruct((n_tok, LANE), f32), jax.ShapeDtypeStruct((4, 8, n_tok), f32)]
    return pl.pallas_call(
        body, grid=(4, nq), in_specs=in_specs, out_specs=out_specs, out_shape=out_shape, name=name,
        scratch_shapes=[pltpu.VMEM((nkv, qw, ATT_KV), bf16)],
        compiler_params=pltpu.CompilerParams(dimension_semantics=("arbitrary", "arbitrary"), vmem_limit_bytes=VMEM_LIMIT),
    )(*args)


def _s5_disc(lr, li, ldt):
    dt = jnp.exp(ldt)
    mag = jnp.exp(lr * dt)
    a_re = mag * jnp.cos(li * dt)
    a_im = mag * jnp.sin(li * dt)
    den = lr * lr + li * li
    f_re = ((a_re - 1.0) * lr + a_im * li) / den
    f_im = (a_im * lr - (a_re - 1.0) * li) / den
    return a_re, a_im, f_re, f_im


def _s5_param_fwd(lr, li, ldt, b_re, b_im):
    def body(lr_ref, li_ref, ldt_ref, br_ref, bi_ref, ar_ref, ai_ref, bbr_ref, bbi_ref):
        a_re, a_im, f_re, f_im = _s5_disc(lr_ref[...], li_ref[...], ldt_ref[...])
        ar_ref[...] = a_re
        ai_ref[...] = a_im
        br, bi = br_ref[...], bi_ref[...]
        bbr_ref[...] = f_re * br - f_im * bi
        bbi_ref[...] = f_re * bi + f_im * br

    col = jax.ShapeDtypeStruct(lr.shape, f32)
    mat = jax.ShapeDtypeStruct(b_re.shape, f32)
    return pl.pallas_call(body, out_shape=[col, col, mat, mat], name="s5_param_fwd")(lr, li, ldt, b_re, b_im)


def _s5_param_bwd(lr, li, ldt, b_re, b_im, da_re, da_im, dbb_re, dbb_im):
    def body(lr_ref, li_ref, ldt_ref, br_ref, bi_ref, dar_ref, dai_ref, gbr_ref, gbi_ref,
             dlr_ref, dli_ref, dldt_ref, dbr_ref, dbi_ref):
        (a_re, a_im, f_re, f_im), vjp = jax.vjp(_s5_disc, lr_ref[...], li_ref[...], ldt_ref[...])
        br, bi, gr, gi = br_ref[...], bi_ref[...], gbr_ref[...], gbi_ref[...]
        dbr_ref[...] = f_re * gr + f_im * gi
        dbi_ref[...] = f_re * gi - f_im * gr
        dfr = jnp.sum(br * gr + bi * gi, axis=-1, keepdims=True)
        dfi = jnp.sum(br * gi - bi * gr, axis=-1, keepdims=True)
        dlr, dli, dldt = vjp((dar_ref[...], dai_ref[...], dfr, dfi))
        dlr_ref[...] = dlr
        dli_ref[...] = dli
        dldt_ref[...] = jnp.sum(dldt.reshape(S5_GROUPS, S5_STATE, 1), axis=1)

    col = jax.ShapeDtypeStruct((S5_LANES, 1), f32)
    mat = jax.ShapeDtypeStruct((S5_LANES, S5_GROUP), f32)
    return pl.pallas_call(body, out_shape=[col, col, jax.ShapeDtypeStruct((S5_GROUPS, 1), f32), mat, mat],
                          name="s5_param_bwd")(lr, li, ldt, b_re, b_im, da_re, da_im, dbb_re, dbb_im)


_SCAN_NB = 4


def _to_streams(a):
    s, c = a.shape
    return jnp.swapaxes(a.reshape(8, s // 8, c), 0, 1).reshape(s, c)


def _from_streams(a):
    s, c = a.shape
    return jnp.swapaxes(a.reshape(s // 8, 8, c), 0, 1).reshape(s, c)


def _s5_scan(name, src, wq, wy, add, y_dtype, a_re8, a_im8, *, reverse, n_tok, grads_of=None):
    rows = n_tok // 8
    nb = _SCAN_NB
    assert nb == 4

    def scan_body(src_ref, w_ref, wy_ref, add_ref, ar_ref, ai_ref, x_ref, y_ref):
        for ri in (0, 1):
            bu = _nn(src_ref[...], w_ref[ri])
            for b in range(nb):
                x_ref[ri, b] = bu[:, b * LANE:(b + 1) * LANE]
        bu_ref = x_ref
        a_r = [ar_ref[b] for b in range(nb)]
        a_i = [ai_ref[b] for b in range(nb)]
        zero = jnp.zeros((8, LANE), f32)
        one = jnp.ones((8, LANE), f32)

        def rows_at(r):
            rr = (rows - 1 - r) if reverse else r
            return pl.ds(pl.multiple_of(rr * 8, 8), 8)

        def pass1(r, carry):
            out = []
            sl = rows_at(r)
            for b in range(nb):
                xr, xi, mr, mi = carry[b]
                nr = a_r[b] * xr - a_i[b] * xi + bu_ref[0, b, sl, :]
                ni = a_r[b] * xi + a_i[b] * xr + bu_ref[1, b, sl, :]
                x_ref[0, b, sl, :] = nr
                x_ref[1, b, sl, :] = ni
                out.append((nr, ni, a_r[b] * mr - a_i[b] * mi, a_r[b] * mi + a_i[b] * mr))
            return tuple(out)

        carry = lax.fori_loop(0, rows, pass1, tuple((zero, zero, one, zero) for _ in range(nb)))
        sub = lax.broadcasted_iota(jnp.int32, (8, LANE), 0)
        feed = []
        for b in range(nb):
            lr_, li_, pr, pi = carry[b]
            fr, fi = zero, zero
            for _ in range(7):
                tr = lr_ + pr * fr - pi * fi
                ti = li_ + pr * fi + pi * fr
                if reverse:
                    fr = jnp.where(sub < 7, pltpu.roll(tr, 7, 0), 0.0)
                    fi = jnp.where(sub < 7, pltpu.roll(ti, 7, 0), 0.0)
                else:
                    fr = jnp.where(sub > 0, pltpu.roll(tr, 1, 0), 0.0)
                    fi = jnp.where(sub > 0, pltpu.roll(ti, 1, 0), 0.0)
            feed.append((fr, fi))

        def pass2(r, carry):
            out = []
            sl = rows_at(r)
            for b in range(nb):
                mr, mi = carry[b]
                fr, fi = feed[b]
                x_ref[0, b, sl, :] += mr * fr - mi * fi
                x_ref[1, b, sl, :] += mr * fi + mi * fr
                out.append((a_r[b] * mr - a_i[b] * mi, a_r[b] * mi + a_i[b] * mr))
            return tuple(out)

        lax.fori_loop(0, rows, pass2, tuple((a_r[b], a_i[b]) for b in range(nb)))

        y = None
        for ri in (0, 1):
            for b in range(nb):
                t = _nn(x_ref[ri, b], wy_ref[ri, b * LANE:(b + 1) * LANE, :])
                y = t if y is None else y + t
        if add is not None:
            y = y + add_ref[...]
        y_ref[...] = y.astype(y_ref.dtype)

    def grads_body(src_ref, xs_ref, u_ref, g_ref, da_ref, dc_ref, db_ref):
        t = lax.broadcasted_iota(jnp.int32, (n_tok, LANE), 0)
        sub = lax.broadcasted_iota(jnp.int32, (8, LANE), 0)

        def prev(v):
            return (jnp.where(t >= 8, pltpu.roll(v, 8, 0), 0.0),
                    jnp.where(sub > 0, pltpu.roll(v[n_tok - 8:, :], 1, 0), 0.0))

        for b in range(nb):
            (xr, hr), (xi, hi) = prev(xs_ref[0, b]), prev(xs_ref[1, b])
            gr, gi = g_ref[0, b], g_ref[1, b]
            gr0, gi0 = gr[0:8, :], gi[0:8, :]
            da_ref[b, 0:1, :] = (jnp.sum(xr * gr + xi * gi, axis=0, keepdims=True)
                                 + jnp.sum(hr * gr0 + hi * gi0, axis=0, keepdims=True))
            da_ref[b, 1:2, :] = (jnp.sum(xr * gi - xi * gr, axis=0, keepdims=True)
                                 + jnp.sum(hr * gi0 - hi * gr0, axis=0, keepdims=True))
            for ri in (0, 1):
                dc_ref[ri, b * LANE:(b + 1) * LANE, :] = _tn(xs_ref[ri, b], src_ref[...])
                db_ref[ri, :, b * LANE:(b + 1) * LANE] = _tn(u_ref[...], g_ref[ri, b])

    n_in = 3 + (add is not None) + 2 * (grads_of is not None)

    def body(*refs):
        ins, rest = list(refs[:n_in]), refs[n_in:]
        src_ref, w_ref, wy_ref = ins[:3]
        add_ref = ins[3] if add is not None else None
        ar_ref, ai_ref = rest[:2]
        if grads_of is None:
            x_ref, y_ref = rest[2:]
            scan_body(src_ref, w_ref, wy_ref, add_ref, ar_ref, ai_ref, x_ref, y_ref)
        else:
            xs_ref, u_ref = ins[-2:]
            y_ref, da_ref, dc_ref, db_ref, x_ref = rest[2:]
            scan_body(src_ref, w_ref, wy_ref, add_ref, ar_ref, ai_ref, x_ref, y_ref)
            grads_body(src_ref, xs_ref, u_ref, x_ref, da_ref, dc_ref, db_ref)

    blk = pl.BlockSpec((2, nb, n_tok, LANE), lambda g: (0, g, 0, 0))
    ablk = pl.BlockSpec((nb, 8, LANE), lambda g: (g, 0, 0))
    col = pl.BlockSpec((n_tok, LANE), lambda g: (0, g))
    in_specs = [col, pl.BlockSpec((2, None, LANE, 512), lambda g: (0, g, 0, 0)), pl.BlockSpec((2, None, 512, LANE), lambda g: (0, g, 0, 0))]
    args = [src, wq, wy]
    if add is not None:
        in_specs.append(col)
        args.append(add)
    y_shape = jax.ShapeDtypeStruct((n_tok, 512), y_dtype)
    x_shape = (2, S5_BLOCKS, n_tok, LANE)
    params = pltpu.CompilerParams(dimension_semantics=("arbitrary",), vmem_limit_bytes=VMEM_LIMIT)
    if grads_of is None:
        return pl.pallas_call(
            body, grid=(S5_BLOCKS // nb,), in_specs=in_specs + [ablk, ablk], out_specs=[blk, col],
            out_shape=[jax.ShapeDtypeStruct(x_shape, f32), y_shape], name=name, compiler_params=params,
        )(*args, a_re8, a_im8)
    return pl.pallas_call(
        body, grid=(S5_BLOCKS // nb,), in_specs=in_specs + [blk, col, ablk, ablk],
        out_specs=[col, pl.BlockSpec((nb, 2, LANE), lambda g: (g, 0, 0)), pl.BlockSpec((2, None, 512, LANE), lambda g: (0, g, 0, 0)),
                   pl.BlockSpec((2, None, LANE, 512), lambda g: (0, g, 0, 0))],
        out_shape=[y_shape, jax.ShapeDtypeStruct((S5_BLOCKS, 2, LANE), f32), jax.ShapeDtypeStruct((2, S5_Q, 512, LANE), f32),
                   jax.ShapeDtypeStruct((2, S5_Q, LANE, 512), f32)],
        scratch_shapes=[pltpu.VMEM((2, nb, n_tok, LANE), f32)], name=name, compiler_params=params,
    )(*args, *grads_of, a_re8, a_im8)


S5_Q = 4


def _bd8(t):
    _, a, b = t.shape
    t = t.reshape(S5_Q, 8, a, 1, b)
    eye = jnp.eye(8, dtype=jnp.bool_).reshape(1, 8, 1, 8, 1)
    return jnp.where(eye, jnp.broadcast_to(t, (S5_Q, 8, a, 8, b)), jnp.zeros((), t.dtype)).reshape(S5_Q, 8 * a, 8 * b)


def _bd8_diag(m, a, b):
    m = m.reshape(S5_Q, 8, a, 8, b)
    eye = jnp.eye(8, dtype=jnp.bool_).reshape(1, 8, 1, 8, 1)
    return jnp.sum(jnp.where(eye, m, 0.0), axis=3).reshape(S5_GROUPS, a, b)


N_CHIPS = 4
_BIG_SHARD = {"w_in": (1, 1024, 1770, 1792), "mla_w_q_up": (1, 256, 192, 256), "mla_w_kv_up": (1, 128, 256, 256),
              "s5_w_glu": (0, 128, 512, 512), "w_branch_out": (0, 384, 1024, 1024), "w_out": (0, 256, 1024, 1024)}


def _to_shards(name, m):
    axis, r, c, cp = _BIG_SHARD[name]
    if axis == 0:
        return m.reshape(N_CHIPS, r, c)
    return jnp.stack([jnp.pad(m[:, j * c:(j + 1) * c], ((0, 0), (0, cp - c))) for j in range(N_CHIPS)])


def _from_shards(name, s):
    axis, r, c, cp = _BIG_SHARD[name]
    if axis == 0:
        return s.reshape(N_CHIPS * r, c)
    return jnp.concatenate([s[j, :, :c] for j in range(N_CHIPS)], axis=1)


def _pad_w_in(w):
    pieces, pos = [], 0
    for name in _PAD_ORDER:
        start, width, inner = _PAD[name]
        o0, ow = _ORIG[name]
        if start + inner > pos:
            pieces.append(jnp.zeros((w.shape[0], start + inner - pos), w.dtype))
        pieces.append(w[:, o0:o0 + ow])
        pos = start + inner + ow
    pieces.append(jnp.zeros((w.shape[0], NP - pos), w.dtype))
    return jnp.concatenate(pieces, axis=1)


def _prep_weights(small, big):
    per_layer = jax.vmap
    w = {}
    w["w_in_shards"] = big["w_in"]
    w["w_in"] = per_layer(lambda s: _pad_w_in(_from_shards("w_in", s)))(big["w_in"])

    def q_up(s):
        wq = _from_shards("mla_w_q_up", s).reshape(MLA_Q_RANK, HEADS, MLA_QK)
        return jnp.pad(wq, ((0, 0), (0, 0), (0, LANE - MLA_QK))).reshape(MLA_Q_RANK, HEADS * LANE)

    def kv_up(s):
        wkv = _from_shards("mla_w_kv_up", s).reshape(MLA_KV_RANK, HEADS, 128)
        wk = jnp.pad(wkv[:, :, :64], ((0, 0), (0, 0), (0, 64))).reshape(MLA_KV_RANK, HEADS * LANE)
        return jnp.concatenate([wk, wkv[:, :, 64:].reshape(MLA_KV_RANK, 512)], axis=1)

    w["wq"] = per_layer(q_up)(big["mla_w_q_up"])
    w["wkv"] = per_layer(kv_up)(big["mla_w_kv_up"])
    for name, key in (("w_glu", "s5_w_glu"), ("wo", "w_branch_out"), ("w_out", "w_out")):
        w[name] = per_layer(functools.partial(_from_shards, key))(big[key])
    row = lambda a: a.astype(f32)[:, None, :]
    lanes = lambda a, n: jnp.pad(row(a), ((0, 0), (0, 0), (0, LANE - n)))
    w["norm_g"] = row(small["norm_g"])
    w["qa_g"] = row(small["mla_q_a_norm"])
    w["kva_g"] = row(small["mla_kv_a_norm"])
    w["qn_g"] = lanes(small["mla_q_norm"], MLA_QK)
    w["kn_g"] = lanes(small["mla_k_norm"], MLA_QK)
    w["fq_g"] = jnp.tile(row(small["fox_q_norm"]), (1, 1, 2))
    w["fk_g"] = jnp.tile(row(small["fox_k_norm"]), (1, 1, 2))
    w["b_f"] = lanes(small["fox_b_f"], HEADS)
    w["lr"] = small["s5_lambda_re"].reshape(DEPTH, S5_LANES, 1)
    w["li"] = small["s5_lambda_im"].reshape(DEPTH, S5_LANES, 1)
    w["ldt"] = jnp.repeat(small["s5_log_dt"], S5_STATE, axis=1).reshape(DEPTH, S5_LANES, 1)
    w["b_re"] = small["s5_b_re"].reshape(DEPTH, S5_LANES, S5_GROUP)
    w["b_im"] = small["s5_b_im"].reshape(DEPTH, S5_LANES, S5_GROUP)
    w["s5_d"] = row(small["s5_d"])
    w["b_glu"] = row(small["s5_b_glu"])
    a_re, a_im, bb_re, bb_im = _s5_param_fwd(w["lr"], w["li"], w["ldt"], w["b_re"], w["b_im"])
    per_group = lambda m: m.reshape(S5_GROUPS, S5_STATE, S5_GROUP)
    pair = lambda f: per_layer(lambda re, im: jnp.stack([f(re), f(im)]).astype(bf16))
    c_re, c_im = small["s5_c_re"], -small["s5_c_im"]
    w["b_cn"] = pair(lambda m: _bd8(jnp.swapaxes(per_group(m), 1, 2)))(bb_re, bb_im)
    w["b_nc"] = pair(lambda m: _bd8(per_group(m)))(bb_re, bb_im)
    w["c_nc"] = pair(lambda m: _bd8(jnp.swapaxes(m, 1, 2)))(c_re, c_im)
    w["c_cn"] = pair(_bd8)(c_re, c_im)
    sublanes = lambda a: jnp.broadcast_to(a.reshape(DEPTH, S5_BLOCKS, 1, LANE), (DEPTH, S5_BLOCKS, 8, LANE))
    w["a_re8"], w["a_im8"], w["a_im8_neg"] = sublanes(a_re), sublanes(a_im), sublanes(-a_im)
    return w


def _fox_halves(x, lane):
    sq = x * x
    lo = jnp.sum(jnp.where(lane < 64, sq, 0.0), axis=-1, keepdims=True)
    hi = jnp.sum(sq, axis=-1, keepdims=True) - lo
    return jnp.where(lane < 64, lax.rsqrt(lo * (1.0 / 64) + EPS), lax.rsqrt(hi * (1.0 / 64) + EPS))


def _fox_halves_bwd(dy, x, r, g, lane):
    xh = x * r
    dxh = dy * g
    pr = dxh * xh
    lo = jnp.sum(jnp.where(lane < 64, pr, 0.0), axis=-1, keepdims=True)
    hi = jnp.sum(pr, axis=-1, keepdims=True) - lo
    mean = jnp.where(lane < 64, lo, hi) * (1.0 / 64)
    return r * (dxh - xh * mean), jnp.sum(dy * xh, axis=0, keepdims=True)


def _mla_recompute(cq, ckv, kpe, c, s1, s2, qa_g, kva_g, wq, wkv):
    cqn, r_cq = _rms(cq, qa_g, MLA_Q_RANK)
    ckvn, r_ckv = _rms(ckv, kva_g, MLA_KV_RANK)
    cqn_b = cqn.astype(bf16)
    ckvn_b = ckvn.astype(bf16)
    q_raw = _nn(cqn_b, wq)
    kv_raw = _nn(ckvn_b, wkv)
    kpe_rot = _rope(kpe, c, s1, s2)
    return cqn_b, r_cq, ckvn_b, r_ckv, q_raw, kv_raw, kpe_rot


def _layer_fwd(x, w, rope_tabs, n_tok):
    c_tab, s1_tab, s2_tab = rope_tabs
    saved = {"x": x}

    def norm_body(x_ref, g_ref, h_ref):
        h_ref[...] = _rms(x_ref[...], g_ref[...], D_MODEL)[0].astype(bf16)

    (h,) = _rowwise("norm_fwd", norm_body, n_tok, [(x, D_MODEL, 0)], [w["norm_g"]], [(D_MODEL, bf16)], [])
    proj = _mm_nn("in_proj", h, w["w_in"], m=n_tok, n=NP, k=D_MODEL, tm=n_tok, tn=512, tk=D_MODEL)
    saved["h"], saved["proj"] = h, proj

    def mla_prep_body(cq_ref, ckv_ref, kpe_ref, c_ref, s1_ref, s2_ref, qa_ref, kva_ref, wq_ref, wkv_ref, qn_g_ref, kn_g_ref,
                      qn_ref, kn_ref, v_ref):
        c, s1, s2 = c_ref[...], s1_ref[...], s2_ref[...]
        _, _, _, _, q_raw, kv_raw, kpe_rot = _mla_recompute(cq_ref[...], ckv_ref[...], kpe_ref[...], c, s1, s2,
                                                            qa_ref[...], kva_ref[...], wq_ref[...], wkv_ref[...])
        for hd in range(HEADS):
            sl = slice(hd * LANE, (hd + 1) * LANE)
            qn_ref[:, sl] = _rms(_rope(q_raw[:, sl], c, s1, s2), qn_g_ref[...], MLA_QK)[0].astype(bf16)
            kn_ref[:, sl] = _rms(kv_raw[:, sl] + kpe_rot, kn_g_ref[...], MLA_QK)[0].astype(bf16)
        v_ref[...] = kv_raw[:, HEADS * LANE:].astype(bf16)

    qn, kn, v_mla = _rowwise(
        "mla_prep", mla_prep_body, n_tok,
        [(proj, *_seg("cq")), (proj, *_seg("ckv")), (proj, *_seg("kpe")), (c_tab, LANE, 0), (s1_tab, LANE, 0), (s2_tab, LANE, 0)],
        [w["qa_g"], w["kva_g"], w["wq"], w["wkv"], w["qn_g"], w["kn_g"]],
        [(HEADS * LANE, bf16), (HEADS * LANE, bf16), (512, bf16)], [])
    y_mla, lse_mla = _attn_fwd("mla_attn_fwd", qn, kn, v_mla, None, mla=True, n_tok=n_tok)
    saved.update(qn=qn, kn=kn, v_mla=v_mla, y_mla=y_mla, lse_mla=lse_mla)

    def fox_prep_body(fq_ref, fk_ref, fv_ref, ff_ref, qg_ref, kg_ref, bf_ref, fqn_ref, fkn_ref, fvb_ref, logf_ref):
        lane = lax.broadcasted_iota(jnp.int32, (TOK, LANE), 1)
        for blk in range(4):
            sl = slice(blk * LANE, (blk + 1) * LANE)
            xq = fq_ref[:, sl]
            fqn_ref[:, sl] = (xq * _fox_halves(xq, lane) * qg_ref[...]).astype(bf16)
            xk = fk_ref[:, sl]
            fkn_ref[:, sl] = (xk * _fox_halves(xk, lane) * kg_ref[...]).astype(bf16)
        fvb_ref[...] = fv_ref[...].astype(bf16)
        z = ff_ref[...] + bf_ref[...]
        logf_ref[...] = jnp.minimum(z, 0.0) - jnp.log(1.0 + jnp.exp(-jnp.abs(z)))

    fqn, fkn, fvb, logf = _rowwise(
        "fox_prep", fox_prep_body, n_tok,
        [(proj, *_seg("fq")), (proj, *_seg("fk")), (proj, *_seg("fv")), (proj, *_seg("ff"))],
        [w["fq_g"], w["fk_g"], w["b_f"]],
        [(512, bf16), (512, bf16), (512, bf16), (LANE, f32)], [])

    def cum_body(x_ref, cum_ref):
        x = x_ref[...]
        t = lax.broadcasted_iota(jnp.int32, x.shape, 0)
        s = 1
        while s < n_tok:
            x = x + jnp.where(t >= s, pltpu.roll(x, s, 0), 0.0)
            s *= 2
        for hd in range(HEADS):
            cum_ref[hd] = jnp.broadcast_to(x[:, hd:hd + 1], (n_tok, LANE))

    cum_b = pl.pallas_call(cum_body, out_shape=jax.ShapeDtypeStruct((HEADS, n_tok, LANE), f32), name="fox_cum")(logf)
    y_fox, lse_fox = _attn_fwd("fox_attn_fwd", fqn, fkn, fvb, cum_b, mla=False, n_tok=n_tok)
    saved.update(fqn=fqn, fkn=fkn, fvb=fvb, cum_b=cum_b, y_fox=y_fox, lse_fox=lse_fox)

    u_w, u_blk = _seg("s5u")
    u_streams = _to_streams(proj[:, u_blk * u_w:(u_blk + 1) * u_w])
    xs, ylin = _s5_scan("s5_scan_fwd", u_streams, w["b_cn"], w["c_nc"], None, f32, w["a_re8"], w["a_im8"], reverse=False, n_tok=n_tok)
    ylin = _from_streams(ylin)

    def s5_post_body(yl_ref, u_ref, d_ref, wg_ref, bg_ref, out_ref):
        y = yl_ref[...] + d_ref[...] * u_ref[...]
        z, _ = _gelu(y)
        out_ref[...] = z * _sigmoid(_nn(z, wg_ref[...]) + bg_ref[...])

    (y_s5,) = _rowwise("s5_post", s5_post_body, n_tok, [(ylin, 512, 0), (proj, u_w, u_blk)],
                       [w["s5_d"], w["w_glu"], w["b_glu"]], [(512, f32)], [])
    saved.update(xs=xs, ylin=ylin, y_s5=y_s5, u_streams=u_streams)

    def merge_body(ym_ref, yf_ref, ys_ref, gm_ref, gf_ref, gs_ref, mm_ref, mf_ref, ms_ref, x_ref, wo_ref, wout_ref, out_ref):
        merged = jnp.zeros((TOK, D_MODEL), f32)
        for b, (y_ref, g_ref, m_ref) in enumerate(((ym_ref, gm_ref, mm_ref), (yf_ref, gf_ref, mf_ref), (ys_ref, gs_ref, ms_ref))):
            g = g_ref[...]
            a = y_ref[...] * (g * _sigmoid(g))
            merged = merged + _sigmoid(m_ref[...]) * _nn(a, wo_ref[b * 512:(b + 1) * 512, :])
        out_ref[...] = x_ref[...] + _nn(merged, wout_ref[...])

    (out,) = _rowwise(
        "merge_fwd", merge_body, n_tok,
        [(y_mla, 512, 0), (y_fox, 512, 0), (y_s5, 512, 0), (proj, *_seg("g_mla")), (proj, *_seg("g_fox")), (proj, *_seg("g_s5")),
         (proj, *_seg("m_mla")), (proj, *_seg("m_fox")), (proj, *_seg("m_s5")), (x, D_MODEL, 0)],
        [w["wo"], w["w_out"]], [(D_MODEL, f32)], [])
    return out, saved


def _layer_bwd(dout, w, sv, rope_tabs, n_tok):
    c_tab, s1_tab, s2_tab = rope_tabs
    proj, x = sv["proj"], sv["x"]
    grads = {}

    def merge_bwd_body(ym_ref, yf_ref, ys_ref, gm_ref, gf_ref, gs_ref, mm_ref, mf_ref, ms_ref, do_ref, wo_ref, wout_ref,
                       dym_ref, dyf_ref, dys_ref, dgm_ref, dgf_ref, dgs_ref, dmm_ref, dmf_ref, dms_ref, dwo_ref, dwout_ref):
        do = do_ref[...]
        branches = ((ym_ref, gm_ref, mm_ref, dym_ref, dgm_ref, dmm_ref), (yf_ref, gf_ref, mf_ref, dyf_ref, dgf_ref, dmf_ref),
                    (ys_ref, gs_ref, ms_ref, dys_ref, dgs_ref, dms_ref))
        acts, outs, sigs = [], [], []
        merged = jnp.zeros((TOK, D_MODEL), f32)
        for b, (y_ref, g_ref, m_ref, _, _, _) in enumerate(branches):
            g = g_ref[...]
            a = (y_ref[...] * (g * _sigmoid(g))).astype(bf16)
            o = _nn(a, wo_ref[b * 512:(b + 1) * 512, :])
            s = _sigmoid(m_ref[...])
            merged = merged + s * o
            acts.append(a)
            outs.append(o)
            sigs.append(s)
        dmerged = _nt(do, wout_ref[...])
        _accumulate(dwout_ref, _tn(merged, do))
        dwo = []
        for b, (y_ref, g_ref, m_ref, dy_ref, dg_ref, dm_ref) in enumerate(branches):
            s, o = sigs[b], outs[b]
            dm_ref[...] = (dmerged * o * s * (1.0 - s)).astype(bf16)
            d_o = dmerged * s
            da = _nt(d_o, wo_ref[b * 512:(b + 1) * 512, :])
            dwo.append(_tn(acts[b], d_o))
            g = g_ref[...]
            sg = _sigmoid(g)
            dy_ref[...] = da * (g * sg)
            dg_ref[...] = (da * y_ref[...] * (sg * (1.0 + g * (1.0 - sg)))).astype(bf16)
        _accumulate(dwo_ref, jnp.concatenate(dwo, axis=0))

    (dy_mla, dy_fox, dy_s5, dg_mla, dg_fox, dg_s5, dm_mla, dm_fox, dm_s5, dwo, dwout) = _rowwise(
        "merge_bwd", merge_bwd_body, n_tok,
        [(sv["y_mla"], 512, 0), (sv["y_fox"], 512, 0), (sv["y_s5"], 512, 0), (proj, *_seg("g_mla")), (proj, *_seg("g_fox")),
         (proj, *_seg("g_s5")), (proj, *_seg("m_mla")), (proj, *_seg("m_fox")), (proj, *_seg("m_s5")), (dout, D_MODEL, 0)],
        [w["wo"], w["w_out"]],
        [(512, f32)] * 3 + [(512, bf16)] * 3 + [(D_MODEL, bf16)] * 3, [((1536, D_MODEL), f32), ((D_MODEL, D_MODEL), f32)])
    grads["w_branch_out"], grads["w_out"] = dwo, dwout

    u_w, u_blk = _seg("s5u")

    def s5_post_bwd_body(yl_ref, u_ref, do_ref, d_ref, wg_ref, bg_ref, dyl_ref, dus_ref, dd_ref, dwg_ref, dbg_ref):
        u = u_ref[...]
        y = yl_ref[...] + d_ref[...] * u
        z, t = _gelu(y)
        s = _sigmoid(_nn(z, wg_ref[...]) + bg_ref[...])
        do = do_ref[...]
        dgl = do * z * s * (1.0 - s)
        dz = do * s + _nt(dgl, wg_ref[...])
        dy = dz * _gelu_grad(y, t)
        dyl_ref[...] = dy.astype(bf16)
        dus_ref[...] = dy * d_ref[...]
        _accumulate(dd_ref, jnp.sum(dy * u, axis=0, keepdims=True))
        _accumulate(dwg_ref, _tn(z, dgl))
        _accumulate(dbg_ref, jnp.sum(dgl, axis=0, keepdims=True))

    dylin, du_skip, dd, dwglu, dbglu = _rowwise(
        "s5_post_bwd", s5_post_bwd_body, n_tok, [(sv["ylin"], 512, 0), (proj, u_w, u_blk), (dy_s5, 512, 0)],
        [w["s5_d"], w["w_glu"], w["b_glu"]], [(512, bf16), (512, f32)], [((1, 512), f32), ((512, 512), f32), ((1, 512), f32)])
    grads["s5_d"], grads["s5_w_glu"], grads["s5_b_glu"] = dd.reshape(512), dwglu, dbglu.reshape(512)

    dylin = _to_streams(dylin)
    ds5u, da, dc_nc, db_cn = _s5_scan("s5_scan_bwd", dylin, w["c_cn"], w["b_nc"], _to_streams(du_skip), bf16, w["a_re8"],
                                      w["a_im8_neg"], reverse=True, n_tok=n_tok, grads_of=(sv["xs"], sv["u_streams"]))
    ds5u = _from_streams(ds5u)
    diag_b = lambda m: jnp.swapaxes(_bd8_diag(m, S5_GROUP, S5_STATE), 1, 2).reshape(S5_LANES, S5_GROUP)
    diag_c = lambda m: jnp.swapaxes(_bd8_diag(m, S5_STATE, S5_GROUP), 1, 2)
    dlr, dli, dldt, db_re, db_im = _s5_param_bwd(
        w["lr"], w["li"], w["ldt"], w["b_re"], w["b_im"], da[:, 0, :].reshape(S5_LANES, 1), da[:, 1, :].reshape(S5_LANES, 1),
        diag_b(db_cn[0]), diag_b(db_cn[1]))
    grads["s5_lambda_re"] = dlr.reshape(S5_GROUPS, S5_STATE)
    grads["s5_lambda_im"] = dli.reshape(S5_GROUPS, S5_STATE)
    grads["s5_log_dt"] = dldt.reshape(S5_GROUPS)
    grads["s5_b_re"] = db_re.reshape(S5_GROUPS, S5_STATE, S5_GROUP)
    grads["s5_b_im"] = db_im.reshape(S5_GROUPS, S5_STATE, S5_GROUP)
    grads["s5_c_re"] = diag_c(dc_nc[0])
    grads["s5_c_im"] = -diag_c(dc_nc[1])

    dfqn, dfkn, dfv, dck, dcq = _attn_bwd("fox_attn_bwd", sv["fqn"], sv["fkn"], sv["fvb"], sv["y_fox"], sv["lse_fox"], dy_fox,
                                          sv["cum_b"], mla=False, n_tok=n_tok)
    dcq = jnp.pad(dcq[:, :2, :].reshape(HEADS, n_tok).T, ((0, 0), (0, LANE - HEADS)))

    def fox_gate_bwd_body(dk_ref, dq_ref, ff_ref, bf_ref, dff_ref, dbf_ref):
        xg = dk_ref[...] + dq_ref[...]
        t = lax.broadcasted_iota(jnp.int32, xg.shape, 0)
        s = 1
        while s < n_tok:
            xg = xg + jnp.where(t < n_tok - s, pltpu.roll(xg, n_tok - s, 0), 0.0)
            s *= 2
        dff = xg * _sigmoid(-(ff_ref[...] + bf_ref[...]))
        dff_ref[...] = dff.astype(bf16)
        dbf_ref[...] = jnp.sum(dff, axis=0, keepdims=True)

    ff_w, ff_blk = _seg("ff")
    dff, dbf = pl.pallas_call(
        fox_gate_bwd_body, grid=(1,),
        in_specs=[pl.BlockSpec((n_tok, LANE), lambda i: (0, 0)), pl.BlockSpec((n_tok, LANE), lambda i: (0, 0)),
                  pl.BlockSpec((n_tok, ff_w), lambda i: (0, ff_blk)), pl.BlockSpec((1, LANE), lambda i: (0, 0))],
        out_specs=[pl.BlockSpec((n_tok, LANE), lambda i: (0, 0)), pl.BlockSpec((1, LANE), lambda i: (0, 0))],
        out_shape=[jax.ShapeDtypeStruct((n_tok, LANE), bf16), jax.ShapeDtypeStruct((1, LANE), f32)], name="fox_gate_bwd",
    )(dck, dcq, proj, w["b_f"])
    grads["fox_b_f"] = dbf[0, :HEADS]

    def fox_prep_bwd_body(fq_ref, fk_ref, dqn_ref, dkn_ref, dv_ref, qg_ref, kg_ref, dfq_ref, dfk_ref, dfv_ref, dqg_ref, dkg_ref):
        lane = lax.broadcasted_iota(jnp.int32, (TOK, LANE), 1)
        dqg = jnp.zeros((1, LANE), f32)
        dkg = jnp.zeros((1, LANE), f32)
        for blk in range(4):
            sl = slice(blk * LANE, (blk + 1) * LANE)
            xq = fq_ref[:, sl]
            dx, dg = _fox_halves_bwd(dqn_ref[:, sl], xq, _fox_halves(xq, lane), qg_ref[...], lane)
            dfq_ref[:, sl] = dx.astype(bf16)
            dqg = dqg + dg
            xk = fk_ref[:, sl]
            dx, dg = _fox_halves_bwd(dkn_ref[:, sl], xk, _fox_halves(xk, lane), kg_ref[...], lane)
            dfk_ref[:, sl] = dx.astype(bf16)
            dkg = dkg + dg
        dfv_ref[...] = dv_ref[...].astype(bf16)
        _accumulate(dqg_ref, dqg + pltpu.roll(dqg, 64, 1))
        _accumulate(dkg_ref, dkg + pltpu.roll(dkg, 64, 1))

    dfq, dfk, dfvb, dfqg, dfkg = _rowwise(
        "fox_prep_bwd", fox_prep_bwd_body, n_tok,
        [(proj, *_seg("fq")), (proj, *_seg("fk")), (dfqn, 512, 0), (dfkn, 512, 0), (dfv, 512, 0)],
        [w["fq_g"], w["fk_g"]], [(512, bf16)] * 3, [((1, LANE), f32)] * 2)
    grads["fox_q_norm"], grads["fox_k_norm"] = dfqg[0, :FOX_DIM], dfkg[0, :FOX_DIM]

    dqn, dkn, dv_mla = _attn_bwd("mla_attn_bwd", sv["qn"], sv["kn"], sv["v_mla"], sv["y_mla"], sv["lse_mla"], dy_mla,
                                 None, mla=True, n_tok=n_tok)

    def mla_prep_bwd_body(cq_ref, ckv_ref, kpe_ref, c_ref, s1_ref, s2_ref, dqn_ref, dkn_ref, dv_ref,
                          qa_ref, kva_ref, wq_ref, wkv_ref, qn_g_ref, kn_g_ref,
                          dcq_ref, dckv_ref, dkpe_ref, dwq_ref, dwkv_ref, dqa_ref, dkva_ref, dqng_ref, dkng_ref):
        c, s1, s2 = c_ref[...], s1_ref[...], s2_ref[...]
        cq, ckv = cq_ref[...], ckv_ref[...]
        cqn_b, r_cq, ckvn_b, r_ckv, q_raw, kv_raw, kpe_rot = _mla_recompute(
            cq, ckv, kpe_ref[...], c, s1, s2, qa_ref[...], kva_ref[...], wq_ref[...], wkv_ref[...])
        lane = lax.broadcasted_iota(jnp.int32, (TOK, LANE), 1)
        dq_raw, dk_raw = [], []
        dkpe_rot = jnp.zeros((TOK, LANE), f32)
        dqng = jnp.zeros((1, LANE), f32)
        dkng = jnp.zeros((1, LANE), f32)
        for hd in range(HEADS):
            sl = slice(hd * LANE, (hd + 1) * LANE)
            q_rot = _rope(q_raw[:, sl], c, s1, s2)
            r = lax.rsqrt(jnp.sum(q_rot * q_rot, axis=-1, keepdims=True) * (1.0 / MLA_QK) + EPS)
            dx, dg = _rms_bwd(dqn_ref[:, sl], q_rot, r, qn_g_ref[...], MLA_QK)
            dqng = dqng + dg
            dq_raw.append(_rope_t(dx, c, s1, s2))
            k_full = kv_raw[:, sl] + kpe_rot
            r = lax.rsqrt(jnp.sum(k_full * k_full, axis=-1, keepdims=True) * (1.0 / MLA_QK) + EPS)
            dx, dg = _rms_bwd(dkn_ref[:, sl], k_full, r, kn_g_ref[...], MLA_QK)
            dkng = dkng + dg
            dk_raw.append(jnp.where(lane < 64, dx, 0.0))
            dkpe_rot = dkpe_rot + dx
        dkpe = _rope_t(dkpe_rot, c, s1, s2)
        dkpe_ref[...] = jnp.where(jnp.logical_and(lane >= 64, lane < 64 + ROPE), dkpe, 0.0).astype(bf16)
        dq_raw = jnp.concatenate(dq_raw, axis=1).astype(bf16)
        dkv_raw = jnp.concatenate(dk_raw + [dv_ref[...]], axis=1).astype(bf16)
        dcqn = _nt(dq_raw, wq_ref[...])
        dckvn = _nt(dkv_raw, wkv_ref[...])
        dx, dg = _rms_bwd(dcqn, cq, r_cq, qa_ref[...], MLA_Q_RANK)
        dcq_ref[...] = dx.astype(bf16)
        _accumulate(dqa_ref, dg)
        dx, dg = _rms_bwd(dckvn, ckv, r_ckv, kva_ref[...], MLA_KV_RANK)
        dckv_ref[...] = dx.astype(bf16)
        _accumulate(dkva_ref, dg)
        _accumulate(dwq_ref, _tn(cqn_b, dq_raw))
        _accumulate(dwkv_ref, _tn(ckvn_b, dkv_raw))
        _accumulate(dqng_ref, dqng)
        _accumulate(dkng_ref, dkng)

    dcq, dckv, dkpe, dwq, dwkv, dqa, dkva, dqng, dkng = _rowwise(
        "mla_prep_bwd", mla_prep_bwd_body, n_tok,
        [(proj, *_seg("cq")), (proj, *_seg("ckv")), (proj, *_seg("kpe")), (c_tab, LANE, 0), (s1_tab, LANE, 0), (s2_tab, LANE, 0),
         (dqn, HEADS * LANE, 0), (dkn, HEADS * LANE, 0), (dv_mla, 512, 0)],
        [w["qa_g"], w["kva_g"], w["wq"], w["wkv"], w["qn_g"], w["kn_g"]],
        [(MLA_Q_RANK, bf16), (LANE, bf16), (LANE, bf16)],
        [((MLA_Q_RANK, HEADS * LANE), f32), ((MLA_KV_RANK, HEADS * LANE + 512), f32), ((1, MLA_Q_RANK), f32),
         ((1, MLA_KV_RANK), f32), ((1, LANE), f32), ((1, LANE), f32)])
    grads["mla_w_q_up"] = dwq.reshape(MLA_Q_RANK, HEADS, LANE)[:, :, :MLA_QK].reshape(MLA_Q_RANK, HEADS * MLA_QK)
    dwk = dwkv[:, :HEADS * LANE].reshape(MLA_KV_RANK, HEADS, LANE)[:, :, :64]
    dwv = dwkv[:, HEADS * LANE:].reshape(MLA_KV_RANK, HEADS, 64)
    grads["mla_w_kv_up"] = jnp.concatenate([dwk, dwv], axis=2).reshape(MLA_KV_RANK, HEADS * 128)
    grads["mla_q_a_norm"], grads["mla_kv_a_norm"] = dqa.reshape(-1), dkva.reshape(-1)
    grads["mla_q_norm"], grads["mla_k_norm"] = dqng[0, :MLA_QK], dkng[0, :MLA_QK]

    _, _, shard_c, shard_cp = _BIG_SHARD["w_in"]
    kpe0 = _PAD["kpe"][2]
    pieces = [dcq, dckv, dkpe[:, kpe0:kpe0 + ROPE], dfq, dfk, dfvb, dff[:, :HEADS], ds5u, dg_mla, dg_fox, dg_s5,
              dm_mla, dm_fox, dm_s5]
    gap = jnp.zeros((n_tok, shard_cp - shard_c), bf16)
    cut, pos = [], 0
    for p in pieces:
        start = 0
        while start < p.shape[1]:
            take = min(p.shape[1] - start, shard_c - pos % shard_c)
            cut.append(p[:, start:start + take])
            start, pos = start + take, pos + take
            if pos % shard_c == 0:
                cut.append(gap)
    dproj = jnp.concatenate(cut, axis=1)
    ct = 256
    per = shard_cp // ct
    dh = _mm("in_proj_dgrad", dproj, w["w_in_shards"], mode="nt", grid=(1, 1, N_CHIPS * per),
             a_spec=pl.BlockSpec((n_tok, ct), lambda i, j, kk: (0, kk)),
             b_spec=pl.BlockSpec((None, D_MODEL, ct), lambda i, j, kk: (kk // per, 0, kk % per)),
             o_spec=pl.BlockSpec((n_tok, D_MODEL), lambda i, j, kk: (0, 0)),
             out_shape=jax.ShapeDtypeStruct((n_tok, D_MODEL), f32), acc_shape=(n_tok, D_MODEL))
    grads["w_in"] = _mm("in_proj_wgrad", sv["h"], dproj, mode="tn", grid=(1, N_CHIPS * per, 1),
                        a_spec=pl.BlockSpec((n_tok, D_MODEL), lambda i, j, kk: (0, 0)),
                        b_spec=pl.BlockSpec((n_tok, ct), lambda i, j, kk: (0, j)),
                        o_spec=pl.BlockSpec((None, D_MODEL, ct), lambda i, j, kk: (j // per, 0, j % per)),
                        out_shape=jax.ShapeDtypeStruct((N_CHIPS, D_MODEL, shard_cp), f32), acc_shape=(D_MODEL, ct))

    def norm_bwd_body(dh_ref, x_ref, do_ref, g_ref, dx_ref, dg_ref):
        xv = x_ref[...]
        r = lax.rsqrt(jnp.sum(xv * xv, axis=-1, keepdims=True) * (1.0 / D_MODEL) + EPS)
        dx, dg = _rms_bwd(dh_ref[...], xv, r, g_ref[...], D_MODEL)
        dx_ref[...] = do_ref[...] + dx
        _accumulate(dg_ref, dg)

    dx, dng = _rowwise("norm_bwd", norm_bwd_body, n_tok, [(dh, D_MODEL, 0), (x, D_MODEL, 0), (dout, D_MODEL, 0)],
                       [w["norm_g"]], [(D_MODEL, f32)], [((1, D_MODEL), f32)])
    grads["norm_g"] = dng.reshape(D_MODEL)
    return dx, grads


def _rope_tables(positions):
    inv = 1.0 / (ROPE_THETA ** (jnp.arange(0, ROPE, 2, dtype=f32) / ROPE))
    ang = positions.astype(f32).reshape(-1, 1) * inv
    cos, sin = jnp.cos(ang), jnp.sin(ang)
    n = ang.shape[0]
    z16, z32, z64 = jnp.zeros((n, 16), f32), jnp.zeros((n, 32), f32), jnp.zeros((n, 64), f32)
    c = jnp.concatenate([jnp.ones((n, 64), f32), cos, cos, z32], axis=1)
    s1 = jnp.concatenate([z64, -sin, z16, z32], axis=1)
    s2 = jnp.concatenate([z64, z16, sin, z32], axis=1)
    return c, s1, s2


BIG = ("w_in", "mla_w_q_up", "mla_w_kv_up", "s5_w_glu", "w_branch_out", "w_out")
SMALL = ("norm_g", "mla_q_a_norm", "mla_kv_a_norm", "mla_q_norm", "mla_k_norm", "fox_b_f", "fox_q_norm", "fox_k_norm",
         "s5_lambda_re", "s5_lambda_im", "s5_log_dt", "s5_b_re", "s5_b_im", "s5_c_re", "s5_c_im", "s5_d", "s5_b_glu")
WEIGHTS = ("norm_g", "w_in", "mla_q_a_norm", "mla_w_q_up", "mla_kv_a_norm", "mla_w_kv_up", "mla_q_norm", "mla_k_norm",
           "fox_b_f", "fox_q_norm", "fox_k_norm", "s5_lambda_re", "s5_lambda_im", "s5_log_dt", "s5_b_re", "s5_b_im",
           "s5_c_re", "s5_c_im", "s5_d", "s5_w_glu", "s5_b_glu", "w_branch_out", "w_out")


def _local_step(x, positions, loss_target, small, big):
    n_tok = x.shape[0]
    tabs = _rope_tables(positions)
    ws, saves = [], []
    hcur = x
    stacked = _prep_weights(small, big)
    for l in range(DEPTH):
        w = {k: v[l] for k, v in stacked.items()}
        hcur, sv = _layer_fwd(hcur, w, tabs, n_tok)
        ws.append(w)
        saves.append(sv)

    def loss_body(y_ref, t_ref, d_ref, l_ref):
        err = y_ref[...] - t_ref[...]
        d_ref[...] = err * (1.0 / D_MODEL)
        tot = jnp.sum(jnp.sum(err * err, axis=-1, keepdims=True), axis=0, keepdims=True)
        _accumulate(l_ref, jnp.broadcast_to(tot * (0.5 / D_MODEL), (1, LANE)))

    dcur, loss = _rowwise("loss", loss_body, n_tok, [(hcur, D_MODEL, 0), (loss_target, D_MODEL, 0)], [], [(D_MODEL, f32)],
                          [((1, LANE), f32)])
    layer_grads = [None] * DEPTH
    for l in reversed(range(DEPTH)):
        dcur, layer_grads[l] = _layer_bwd(dcur, ws[l], saves[l], tabs, n_tok)
    grads = {n: jnp.stack([layer_grads[l][n] for l in range(DEPTH)]) for n in WEIGHTS}
    return loss[0, 0], dcur, grads


N_DEV = 8
_ANY = pl.BlockSpec(memory_space=pl.ANY)
_MESH = pl.DeviceIdType.MESH


def _all_gather8(name, blk):
    m = blk.shape[0]

    def body(x_ref, out_ref, send_sems, recv_sems, local_sem):
        x, y, c = lax.axis_index("x"), lax.axis_index("y"), lax.axis_index("c")
        me, sibling = (x, y, c), (x, y, 1 - c)
        chips = [(1 - x, y), (x, 1 - y), (1 - x, 1 - y)]

        def slot(px, py, pc):
            return out_ref.at[4 * px + 2 * py + pc]

        def copy(k, block, to, src=None):
            return pltpu.make_async_remote_copy(
                src_ref=slot(*block) if src is None else src, dst_ref=slot(*block),
                send_sem=send_sems.at[k], recv_sem=recv_sems.at[k], device_id=to, device_id_type=_MESH)

        mine = pltpu.make_async_copy(x_ref, slot(*me), local_sem)
        mine.start()
        first = [copy(0, me, sibling, src=x_ref)]
        first += [copy(1 + j, me, (*chip, c), src=x_ref) for j, chip in enumerate(chips)]
        for cp in first:
            cp.start()
        passed = [copy(4 + j, (*chip, c), sibling) for j, chip in enumerate(chips)]
        for j, chip in enumerate(chips):
            copy(1 + j, (*chip, c), me).wait_recv()
            passed[j].start()
        copy(0, sibling, me).wait_recv()
        for j, chip in enumerate(chips):
            copy(4 + j, (*chip, 1 - c), me).wait_recv()
        for cp in first + passed:
            cp.wait_send()
        mine.wait()

    return pl.pallas_call(
        body, out_shape=jax.ShapeDtypeStruct((N_DEV, m, LANE), blk.dtype), in_specs=[_ANY], out_specs=_ANY, name=name,
        scratch_shapes=[pltpu.SemaphoreType.DMA((7,)), pltpu.SemaphoreType.DMA((7,)), pltpu.SemaphoreType.DMA],
    )(blk)


def _gather_layers(name, shards):
    n = len(shards)

    def body(*refs):
        x_refs, out_refs = refs[:n], refs[n:2 * n]
        send_sems, recv_sems, local_sems = refs[2 * n:]
        x, y, c = lax.axis_index("x"), lax.axis_index("y"), lax.axis_index("c")
        me, sibling = (x, y, c), (x, y, 1 - c)
        xn, yn, dg = (1 - x, y, c), (x, 1 - y, c), (1 - x, 1 - y, c)
        relay_from = (x + (1 - c) * (1 - 2 * x), y + c * (1 - 2 * y), c)
        relay_to = (x + c * (1 - 2 * x), y + (1 - c) * (1 - 2 * y), c)

        def copy(w, k, block, to, src=None):
            px, py, pc = block
            slot = out_refs[w].at[pc, 2 * px + py]
            return pltpu.make_async_remote_copy(
                src_ref=slot if src is None else src, dst_ref=slot, send_sem=send_sems.at[7 * w + k],
                recv_sem=recv_sems.at[7 * w + k], device_id=to, device_id_type=_MESH)

        started, local = [], []
        for w in range(n):
            src = x_refs[w].at[c]
            mine = pltpu.make_async_copy(src, out_refs[w].at[c, 2 * x + y], local_sems.at[w])
            mine.start()
            local.append(mine)
            first = [copy(w, 0, me, sibling, src=src), copy(w, 1, me, xn, src=src), copy(w, 2, me, yn, src=src)]
            for cp in first:
                cp.start()
            started += first
        for w in range(n):
            copy(w, 1, xn, me).wait_recv()
            copy(w, 2, yn, me).wait_recv()
            onward = [copy(w, 3, relay_from, relay_to), copy(w, 4, xn, sibling), copy(w, 5, yn, sibling)]
            for cp in onward:
                cp.start()
            started += onward
        for w in range(n):
            copy(w, 3, dg, me).wait_recv()
            onward = copy(w, 6, dg, sibling)
            onward.start()
            started.append(onward)
        for w in range(n):
            copy(w, 0, sibling, me).wait_recv()
            for k, chip in ((4, xn), (5, yn), (6, dg)):
                copy(w, k, (chip[0], chip[1], 1 - c), me).wait_recv()
        for cp in started:
            cp.wait_send()
        for cp in local:
            cp.wait()

    return pl.pallas_call(
        body, out_shape=[jax.ShapeDtypeStruct((2, N_CHIPS) + s.shape[1:], s.dtype) for s in shards],
        in_specs=[_ANY] * n, out_specs=[_ANY] * n, name=name,
        scratch_shapes=[pltpu.SemaphoreType.DMA((7 * n,)), pltpu.SemaphoreType.DMA((7 * n,)), pltpu.SemaphoreType.DMA((n,))],
    )(*shards)


def _swap_layers(name, parts):
    n = len(parts)

    def body(*refs):
        p_refs, got_refs = refs[:n], refs[n:2 * n]
        send_sems, recv_sems = refs[2 * n:]
        x, y, c = lax.axis_index("x"), lax.axis_index("y"), lax.axis_index("c")
        copies = []
        for w in range(n):
            cp = pltpu.make_async_remote_copy(
                src_ref=p_refs[w].at[1 - c], dst_ref=got_refs[w], send_sem=send_sems.at[w], recv_sem=recv_sems.at[w],
                device_id=(x, y, 1 - c), device_id_type=_MESH)
            cp.start()
            copies.append(cp)
        for cp in copies:
            cp.wait()

    return pl.pallas_call(
        body, out_shape=[jax.ShapeDtypeStruct(p.shape[1:], p.dtype) for p in parts], in_specs=[_ANY] * n, out_specs=[_ANY] * n,
        name=name, scratch_shapes=[pltpu.SemaphoreType.DMA((n,)), pltpu.SemaphoreType.DMA((n,))],
    )(*parts)


def _scatter_to_chips(name, parts):
    n = len(parts)

    def body(*refs):
        p_refs, out_refs = refs[:n], refs[n:2 * n]
        send_sems, recv_sems, local_sems = refs[2 * n:]
        x, y, c = lax.axis_index("x"), lax.axis_index("y"), lax.axis_index("c")
        jme = 2 * x + y
        chips = [(1 - x, y), (x, 1 - y), (1 - x, 1 - y)]
        sends, local = [], []
        for w in range(n):
            mine = pltpu.make_async_copy(p_refs[w].at[jme], out_refs[w].at[jme], local_sems.at[w])
            mine.start()
            local.append(mine)
            for k, (tx, ty) in enumerate(chips):
                cp = pltpu.make_async_remote_copy(
                    src_ref=p_refs[w].at[2 * tx + ty], dst_ref=out_refs[w].at[jme], send_sem=send_sems.at[3 * w + k],
                    recv_sem=recv_sems.at[3 * w + k], device_id=(tx, ty, c), device_id_type=_MESH)
                cp.start()
                sends.append(cp)
        for w in range(n):
            for k, (tx, ty) in enumerate(chips):
                pltpu.make_async_remote_copy(
                    src_ref=p_refs[w].at[jme], dst_ref=out_refs[w].at[2 * tx + ty], send_sem=send_sems.at[3 * w + k],
                    recv_sem=recv_sems.at[3 * w + k], device_id=(tx, ty, c), device_id_type=_MESH).wait_recv()
        for cp in sends:
            cp.wait_send()
        for cp in local:
            cp.wait()

    return pl.pallas_call(
        body, out_shape=[jax.ShapeDtypeStruct(p.shape, p.dtype) for p in parts], in_specs=[_ANY] * n, out_specs=[_ANY] * n, name=name,
        scratch_shapes=[pltpu.SemaphoreType.DMA((3 * n,)), pltpu.SemaphoreType.DMA((3 * n,)), pltpu.SemaphoreType.DMA((n,))],
    )(*parts)


def _share_layers(name, bufs):
    n = len(bufs)

    def body(*refs):
        out_refs = refs[n:2 * n]
        send_sems, recv_sems = refs[2 * n:]
        x, y, c = lax.axis_index("x"), lax.axis_index("y"), lax.axis_index("c")
        copies = []
        for w in range(n):
            cp = pltpu.make_async_remote_copy(src_ref=out_refs[w].at[c], dst_ref=out_refs[w].at[c], send_sem=send_sems.at[w],
                                              recv_sem=recv_sems.at[w], device_id=(x, y, 1 - c), device_id_type=_MESH)
            cp.start()
            copies.append(cp)
        for w in range(n):
            pltpu.make_async_remote_copy(src_ref=out_refs[w].at[c], dst_ref=out_refs[w].at[1 - c], send_sem=send_sems.at[w],
                                         recv_sem=recv_sems.at[w], device_id=(x, y, 1 - c), device_id_type=_MESH).wait_recv()
        for cp in copies:
            cp.wait_send()

    return pl.pallas_call(
        body, out_shape=[jax.ShapeDtypeStruct(b.shape, b.dtype) for b in bufs], in_specs=[_ANY] * n, out_specs=[_ANY] * n,
        input_output_aliases={w: w for w in range(n)}, name=name,
        scratch_shapes=[pltpu.SemaphoreType.DMA((n,)), pltpu.SemaphoreType.DMA((n,))],
    )(*bufs)


def _row_tile(rows, cols):
    best = 16
    for t in range(16, rows + 1, 16):
        if rows % t == 0 and t * cols * 4 <= 2 * 1024 * 1024:
            best = t
    return best


def _add_pair(name, core, parts, got, out_dtype):
    _, _, r, c = parts.shape
    t = _row_tile(r, c)

    def body(core_ref, a_ref, b_ref, o_ref):
        o_ref[...] = (a_ref[...] + b_ref[...]).astype(o_ref.dtype)

    spec = pl.BlockSpec((None, t, c), lambda j, i, core_ref: (j, i, 0))
    grid_spec = pltpu.PrefetchScalarGridSpec(
        num_scalar_prefetch=1, grid=(N_CHIPS, r // t),
        in_specs=[pl.BlockSpec((None, None, t, c), lambda j, i, core_ref: (core_ref[0], j, i, 0)), spec], out_specs=spec)
    return pl.pallas_call(body, grid_spec=grid_spec, out_shape=jax.ShapeDtypeStruct(got.shape, out_dtype), name=name,
                          compiler_params=pltpu.CompilerParams(dimension_semantics=("arbitrary", "arbitrary")))(core, parts, got)


def _add_four(name, core, a):
    _, r, c = a.shape
    t = _row_tile(r, c)

    def body(core_ref, a0, a1, a2, a3, o_ref):
        o_ref[...] = ((a0[...].astype(f32) + a1[...].astype(f32)) + a2[...].astype(f32)) + a3[...].astype(f32)

    specs = [pl.BlockSpec((None, t, c), functools.partial(lambda i, core_ref, k: (k, i, 0), k=k)) for k in range(N_CHIPS)]
    grid_spec = pltpu.PrefetchScalarGridSpec(
        num_scalar_prefetch=1, grid=(r // t,), in_specs=specs,
        out_specs=pl.BlockSpec((None, t, c), lambda i, core_ref: (core_ref[0], i, 0)))
    return pl.pallas_call(body, grid_spec=grid_spec, out_shape=jax.ShapeDtypeStruct((2, r, c), f32), name=name,
                          compiler_params=pltpu.CompilerParams(dimension_semantics=("arbitrary",)))(core, a, a, a, a)


def _adamw(name, w, g, m, v, row_tile=None, lead_tile=None):
    c1 = 1.0 - ADAM_B1 ** ADAM_STEP
    c2 = 1.0 - ADAM_B2 ** ADAM_STEP

    def body(w_ref, g_ref, m_ref, v_ref, d_ref, nm_ref, nv_ref):
        gv = g_ref[...]
        nm = ADAM_B1 * m_ref[...] + (1.0 - ADAM_B1) * gv
        nv = ADAM_B2 * v_ref[...] + (1.0 - ADAM_B2) * (gv * gv)
        m_hat = nm / c1
        v_hat = nv / c2
        d_ref[...] = -ADAM_LR * (m_hat / (jnp.sqrt(v_hat) + ADAM_EPS) + ADAM_WD * w_ref[...])
        nm_ref[...] = nm
        nv_ref[...] = nv

    sds = jax.ShapeDtypeStruct(w.shape, f32)
    if lead_tile is not None:
        spec = pl.BlockSpec((lead_tile,) + w.shape[1:], lambda i: (i, 0, 0))
        return pl.pallas_call(body, grid=(w.shape[0] // lead_tile,), in_specs=[spec] * 4, out_specs=[spec] * 3, out_shape=[sds] * 3,
                              name=name, compiler_params=pltpu.CompilerParams(dimension_semantics=("arbitrary",), vmem_limit_bytes=VMEM_LIMIT),
                              )(w, g, m, v)
    if row_tile is None:
        return pl.pallas_call(body, out_shape=[sds] * 3, name=name)(w, g, m, v)
    _, r, c = w.shape
    spec = pl.BlockSpec((None, row_tile, c), lambda l, i: (l, i, 0))
    return pl.pallas_call(body, grid=(DEPTH, r // row_tile), in_specs=[spec] * 4, out_specs=[spec] * 3, out_shape=[sds] * 3, name=name,
                          compiler_params=pltpu.CompilerParams(dimension_semantics=("arbitrary", "arbitrary"), vmem_limit_bytes=VMEM_LIMIT),
                          )(w, g, m, v)


def _pad_rows(flat, rows):
    return jnp.pad(flat, (0, rows * LANE - flat.shape[0])).reshape(rows, LANE)


def kernel(x, positions, norm_g, w_in, mla_q_a_norm, mla_w_q_up, mla_kv_a_norm, mla_w_kv_up, mla_q_norm, mla_k_norm, fox_b_f, fox_q_norm, fox_k_norm, s5_lambda_re, s5_lambda_im, s5_log_dt, s5_b_re, s5_b_im, s5_c_re, s5_c_im, s5_d, s5_w_glu, s5_b_glu, w_branch_out, w_out, loss_target, m_norm_g, m_w_in, m_mla_q_a_norm, m_mla_w_q_up, m_mla_kv_a_norm, m_mla_w_kv_up, m_mla_q_norm, m_mla_k_norm, m_fox_b_f, m_fox_q_norm, m_fox_k_norm, m_s5_lambda_re, m_s5_lambda_im, m_s5_log_dt, m_s5_b_re, m_s5_b_im, m_s5_c_re, m_s5_c_im, m_s5_d, m_s5_w_glu, m_s5_b_glu, m_w_branch_out, m_w_out, v_norm_g, v_w_in, v_mla_q_a_norm, v_mla_w_q_up, v_mla_kv_a_norm, v_mla_w_kv_up, v_mla_q_norm, v_mla_k_norm, v_fox_b_f, v_fox_q_norm, v_fox_k_norm, v_s5_lambda_re, v_s5_lambda_im, v_s5_log_dt, v_s5_b_re, v_s5_b_im, v_s5_c_re, v_s5_c_im, v_s5_d, v_s5_w_glu, v_s5_b_glu, v_w_branch_out, v_w_out):
    given = dict(locals())
    wts = {n: given[n] for n in WEIGHTS}
    mom1 = {n: given["m_" + n] for n in WEIGHTS}
    mom2 = {n: given["v_" + n] for n in WEIGHTS}

    def lanes(n, a):
        _, _, c, cp = _BIG_SHARD[n]
        return jnp.pad(a, ((0, 0), (0, 0), (0, cp - c)))

    gathered = _gather_layers("gather_weights", [lanes(n, wts[n].astype(bf16)) for n in BIG])
    big = dict(zip(BIG, gathered))
    small = {n: wts[n] for n in SMALL}

    loss_local, grad_x, grads = _local_step(x[0], positions, loss_target[0], small, big)
    loss = lax.psum(loss_local, ("x", "y", "c"))

    small_flat = jnp.concatenate([grads[n].reshape(-1) for n in SMALL])
    small_rows = -(-small_flat.shape[0] // (N_DEV * 16 * LANE)) * 16
    parts = [grads[n] if n == "w_in" else jnp.stack([_to_shards(n, grads[n][l]) for l in range(DEPTH)]) for n in BIG]
    parts.append(jnp.swapaxes(_pad_rows(small_flat, N_DEV * small_rows).reshape(N_CHIPS, 2, small_rows, LANE), 0, 1))
    core = lax.axis_index("c")
    core1 = core.reshape(1).astype(jnp.int32)
    got = _swap_layers("grads_to_sibling", parts)
    hop = [bf16] * len(BIG) + [f32]
    pair = [_add_pair("grads_pair_sum_%d" % i, core1, a, b, dt) for i, (a, b, dt) in enumerate(zip(parts, got, hop))]
    landed = _scatter_to_chips("grads_to_chips", pair)
    total = [_add_four("grads_chip_sum_%d" % i, core1, a) for i, a in enumerate(landed)]
    shared = _share_layers("grads_share", total[:-1])
    small_mine = lax.dynamic_index_in_dim(total[-1], core, 0, keepdims=False)
    small_all = _all_gather8("gather_small_grads", small_mine).reshape(-1)

    g_out = {n: s[:, :, :_BIG_SHARD[n][2]] for n, s in zip(BIG, shared)}
    pos = 0
    for n in SMALL:
        g_out[n] = small_all[pos:pos + wts[n].size].reshape(wts[n].shape)
        pos += wts[n].size

    delta, new_m, new_v = {}, {}, {}
    for n in WEIGHTS:
        if n == "w_in":
            cols_first = lambda a: jnp.transpose(a, (2, 0, 1))
            res = _adamw("adamw_" + n, *[cols_first(a) for a in (wts[n], g_out[n], mom1[n], mom2[n])], lead_tile=177)
            delta[n], new_m[n], new_v[n] = [jnp.transpose(a, (1, 2, 0)) for a in res]
            continue
        row_tile = _row_tile(*wts[n].shape[1:]) if n in BIG else None
        delta[n], new_m[n], new_v[n] = _adamw("adamw_" + n, wts[n], g_out[n], mom1[n], mom2[n], row_tile)

    return (loss, grad_x[None], *[g_out[n] for n in WEIGHTS], *[delta[n] for n in WEIGHTS],
            *[new_m[n] for n in WEIGHTS], *[new_v[n] for n in WEIGHTS])
```

```python
import functools
import math

import jax
import jax.numpy as jnp
from jax import lax
from jax.experimental import pallas as pl
from jax.experimental.pallas import tpu as pltpu

f32 = jnp.float32
bf16 = jnp.bfloat16

D_MODEL = 1024
DEPTH = 2
EPS = 1e-6
HEADS = 8
MLA_QK = 96
MLA_Q_RANK = 256
MLA_KV_RANK = 128
ROPE = 32
ROPE_THETA = 10000.0
FOX_DIM = 64
S5_GROUPS = 32
S5_GROUP = 16
S5_STATE = 64
S5_LANES = S5_GROUPS * S5_STATE
LANE = 128
S5_BLOCKS = S5_LANES // LANE
TOK = 256
VMEM_LIMIT = 56 * 1024 * 1024

ADAM_LR = 0.001
ADAM_B1 = 0.9
ADAM_B2 = 0.999
ADAM_EPS = 1e-08
ADAM_WD = 0.01
ADAM_STEP = 10

_ORIG = {}
_off = 0
for _n, _w in (("cq", 256), ("ckv", 128), ("kpe", 32), ("fq", 512), ("fk", 512), ("fv", 512), ("ff", 8), ("s5u", 512),
               ("g_mla", 512), ("g_fox", 512), ("g_s5", 512), ("m_mla", 1024), ("m_fox", 1024), ("m_s5", 1024)):
    _ORIG[_n] = (_off, _w)
    _off += _w
_PAD = {"m_mla": (0, 1024, 0), "m_fox": (1024, 1024, 0), "m_s5": (2048, 1024, 0),
        "fq": (3072, 512, 0), "fk": (3584, 512, 0), "fv": (4096, 512, 0), "s5u": (4608, 512, 0),
        "g_mla": (5120, 512, 0), "g_fox": (5632, 512, 0), "g_s5": (6144, 512, 0),
        "cq": (6656, 256, 0), "ckv": (6912, 128, 0), "kpe": (7040, 128, 64), "ff": (7168, 128, 0)}
NP = 7680
_PAD_ORDER = ("m_mla", "m_fox", "m_s5", "fq", "fk", "fv", "s5u", "g_mla", "g_fox", "g_s5", "cq", "ckv", "kpe", "ff")


def _seg(name):
    start, width, _ = _PAD[name]
    return width, start // width


def _nn(a, b):
    return lax.dot_general(a.astype(bf16), b.astype(bf16), (((1,), (0,)), ((), ())), preferred_element_type=f32)


def _nt(a, b):
    return lax.dot_general(a.astype(bf16), b.astype(bf16), (((1,), (1,)), ((), ())), preferred_element_type=f32)


def _tn(a, b):
    return lax.dot_general(a.astype(bf16), b.astype(bf16), (((0,), (0,)), ((), ())), preferred_element_type=f32)


def _rms(x, g, n):
    r = lax.rsqrt(jnp.sum(x * x, axis=-1, keepdims=True) * (1.0 / n) + EPS)
    return x * r * g, r


def _rms_bwd(dy, x, r, g, n):
    xh = x * r
    dg = jnp.sum(dy * xh, axis=0, keepdims=True)
    dxh = dy * g
    dx = r * (dxh - xh * (jnp.sum(dxh * xh, axis=-1, keepdims=True) * (1.0 / n)))
    return dx, dg


def _sigmoid(x):
    return 1.0 / (1.0 + jnp.exp(-x))


_GELU_C = math.sqrt(2.0 / math.pi)


def _gelu(x):
    t = jnp.tanh(_GELU_C * (x + 0.044715 * x * x * x))
    return 0.5 * x * (1.0 + t), t


def _gelu_grad(x, t):
    return 0.5 * (1.0 + t) + 0.5 * x * (1.0 - t * t) * _GELU_C * (1.0 + 3.0 * 0.044715 * x * x)


def _accumulate(ref, val):
    i = pl.program_id(0)

    @pl.when(i == 0)
    def _():
        ref[...] = val

    @pl.when(i > 0)
    def _():
        ref[...] += val


def _rope(x, c, s1, s2):
    return x * c + pltpu.roll(x, LANE - 16, 1) * s1 + pltpu.roll(x, 16, 1) * s2


def _rope_t(d, c, s1, s2):
    return d * c + pltpu.roll(d * s1, 16, 1) + pltpu.roll(d * s2, LANE - 16, 1)


def _const_map(ndim):
    return lambda *_: (0,) * ndim


def _rowwise(name, body, n_tok, tiled_in, full_in, tiled_out, acc_out, tile=TOK):
    in_specs, args = [], []
    for arr, width, blk in tiled_in:
        in_specs.append(pl.BlockSpec((tile, width), functools.partial(lambda i, b: (i, b), b=blk)))
        args.append(arr)
    for arr in full_in:
        in_specs.append(pl.BlockSpec(arr.shape, _const_map(arr.ndim)))
        args.append(arr)
    out_specs, out_shape = [], []
    for width, dt in tiled_out:
        out_specs.append(pl.BlockSpec((tile, width), lambda i: (i, 0)))
        out_shape.append(jax.ShapeDtypeStruct((n_tok, width), dt))
    for shape, dt in acc_out:
        out_specs.append(pl.BlockSpec(shape, _const_map(len(shape))))
        out_shape.append(jax.ShapeDtypeStruct(shape, dt))
    return pl.pallas_call(
        body, grid=(n_tok // tile,), in_specs=in_specs, out_specs=out_specs, out_shape=out_shape, name=name,
        compiler_params=pltpu.CompilerParams(dimension_semantics=("arbitrary",), vmem_limit_bytes=VMEM_LIMIT),
    )(*args)


def _mm(name, a, b, *, mode, grid, a_spec, b_spec, o_spec, out_shape, acc_shape, add=None, add_spec=None):
    nk = grid[2]

    def body(*refs):
        if add is None:
            a_ref, b_ref, o_ref, acc_ref = refs
        else:
            a_ref, b_ref, add_ref, o_ref, acc_ref = refs
        k = pl.program_id(2)

        @pl.when(k == 0)
        def _():
            acc_ref[...] = jnp.zeros_like(acc_ref)

        acc_ref[...] += {"nn": _nn, "nt": _nt, "tn": _tn}[mode](a_ref[...], b_ref[...])

        @pl.when(k == nk - 1)
        def _():
            r = acc_ref[...]
            if add is not None:
                r = r + add_ref[...]
            o_ref[...] = r.astype(o_ref.dtype)

    in_specs = [a_spec, b_spec] + ([add_spec] if add is not None else [])
    args = (a, b) + ((add,) if add is not None else ())
    return pl.pallas_call(
        body, grid=grid, in_specs=in_specs, out_specs=o_spec, out_shape=out_shape, name=name,
        scratch_shapes=[pltpu.VMEM(acc_shape, f32)],
        compiler_params=pltpu.CompilerParams(dimension_semantics=("arbitrary", "arbitrary", "arbitrary"), vmem_limit_bytes=VMEM_LIMIT),
    )(*args)


def _mm_nn(name, a, b, *, m, n, k, tm, tn, tk, out_dtype=f32, a_koff=0):
    return _mm(name, a, b, mode="nn", grid=(m // tm, n // tn, k // tk),
               a_spec=pl.BlockSpec((tm, tk), lambda i, j, kk: (i, kk + a_koff)),
               b_spec=pl.BlockSpec((tk, tn), lambda i, j, kk: (kk, j)),
               o_spec=pl.BlockSpec((tm, tn), lambda i, j, kk: (i, j)),
               out_shape=jax.ShapeDtypeStruct((m, n), out_dtype), acc_shape=(tm, tn))


ATT_KV = 512
ATT_Q = 2048


def _attn_common(mla, n_tok):
    qw = 2 * LANE if mla else LANE
    scale = 1.0 / math.sqrt(MLA_QK if mla else FOX_DIM)
    return qw, scale, min(ATT_Q, n_tok)


def _attn_heads(q_ref, mla):
    out = []
    if mla:
        for e in (0, 1):
            qe = q_ref[:, e * LANE:(e + 1) * LANE]
            out.append((qe.astype(f32).T.astype(bf16), qe))
        return out
    q = q_ref[...]
    tq = q.shape[0]
    qt = q.astype(f32).T
    row = lax.broadcasted_iota(jnp.int32, (LANE, tq), 0)
    lane = lax.broadcasted_iota(jnp.int32, (tq, LANE), 1)
    for e in (0, 1):
        out.append((jnp.where((row >= 64) == bool(e), qt, 0.0).astype(bf16),
                    jnp.where((lane >= 64) == bool(e), q, jnp.zeros((), bf16))))
    return out


def _attn_allowed(off, i, tq, mla):
    kpos = off + lax.broadcasted_iota(jnp.int32, (ATT_KV, tq), 0)
    qpos = i * tq + lax.broadcasted_iota(jnp.int32, (ATT_KV, tq), 1)
    return ((kpos // 64) <= (qpos // 64)) if mla else (kpos <= qpos)


def _attn_fwd(name, q, k, v, cum_b, *, mla, n_tok):
    qw, scale, tq = _attn_common(mla, n_tok)
    nq = n_tok // tq
    nkv = n_tok // ATT_KV
    has_bias = cum_b is not None

    def body(*refs):
        if has_bias:
            q_ref, k_ref, v_ref, cb_ref, o_ref, lse_ref, vt_ref = refs
        else:
            q_ref, k_ref, v_ref, o_ref, lse_ref, vt_ref = refs
        i = pl.program_id(1)

        @pl.when(i == 0)
        def _():
            for jb in range(nkv):
                vt_ref[jb] = v_ref[jb * ATT_KV:(jb + 1) * ATT_KV, :].astype(f32).T.astype(bf16)

        heads = _attn_heads(q_ref, mla)

        def step(j, carry, masked, q_lo=0):
            off = pl.multiple_of(j * ATT_KV, ATT_KV)
            allowed = _attn_allowed(off, i, tq, mla)[:, q_lo:] if masked else None
            keep = lambda old, part: part if q_lo == 0 else jnp.concatenate([old[:, :q_lo], part], axis=1)
            vt = vt_ref[j]
            sts = []
            for e in (0, 1):
                kb = k_ref[pl.ds(off, ATT_KV), e * LANE:(e + 1) * LANE] if mla else k_ref[pl.ds(off, ATT_KV), :]
                sts.append(_nn(kb, heads[e][0][:, q_lo:]))
            stats = []
            for e in (0, 1):
                m, l = carry[e][0][:, q_lo:], carry[e][1][:, q_lo:]
                st = sts[e] * scale
                if has_bias:
                    st = st - jnp.tile(cb_ref[e, pl.ds(off, ATT_KV), :], (1, (tq - q_lo) // LANE))
                if masked:
                    st = jnp.where(allowed, st, -1e30)
                m_new = jnp.maximum(m, jnp.max(st, axis=0, keepdims=True))
                alpha = jnp.exp(m - m_new)
                pt = jnp.exp(st - m_new)
                stats.append((m_new, alpha * l + jnp.sum(pt, axis=0, keepdims=True), alpha, pt.astype(bf16)))
            new = []
            for e in (0, 1):
                m_new, l, alpha, pt = stats[e]
                acc = alpha * carry[e][2][:, q_lo:] + _nn(vt[64 * e:64 * e + 64, :], pt)
                new.append((keep(carry[e][0], m_new), keep(carry[e][1], l), keep(carry[e][2], acc)))
            return tuple(new)

        init = tuple((jnp.full((1, tq), -1e30, f32), jnp.zeros((1, tq), f32), jnp.zeros((64, tq), f32)) for _ in (0, 1))
        n_full = i * (tq // ATT_KV)
        carry = lax.fori_loop(0, n_full, functools.partial(step, masked=False), init)
        for d in range(tq // ATT_KV):
            carry = step(n_full + d, carry, True, q_lo=d * ATT_KV)
        o_ref[...] = jnp.concatenate([carry[e][2] / carry[e][1] for e in (0, 1)], axis=0).T
        lse_ref[...] = jnp.zeros_like(lse_ref)
        for e in (0, 1):
            lse_ref[e:e + 1, :] = carry[e][0] + jnp.log(carry[e][1])

    in_specs = [pl.BlockSpec((tq, qw), lambda p, i: (i, p)),
                pl.BlockSpec((n_tok, qw), lambda p, i: (0, p)),
                pl.BlockSpec((n_tok, LANE), lambda p, i: (0, p))]
    args = [q, k, v]
    if has_bias:
        in_specs.append(pl.BlockSpec((2, n_tok, LANE), lambda p, i: (p, 0, 0)))
        args.append(cum_b)
    return pl.pallas_call(
        body, grid=(4, nq), in_specs=in_specs,
        out_specs=[pl.BlockSpec((tq, LANE), lambda p, i: (i, p)), pl.BlockSpec((None, 8, tq), lambda p, i: (p, 0, i))],
        out_shape=[jax.ShapeDtypeStruct((n_tok, 512), f32), jax.ShapeDtypeStruct((4, 8, n_tok), f32)], name=name,
        scratch_shapes=[pltpu.VMEM((nkv, LANE, ATT_KV), bf16)],
        compiler_params=pltpu.CompilerParams(dimension_semantics=("arbitrary", "arbitrary"), vmem_limit_bytes=VMEM_LIMIT),
    )(*args)


def _attn_bwd(name, q, k, v, o, lse, do, cum_b, *, mla, n_tok):
    qw, scale, tq = _attn_common(mla, n_tok)
    nq = n_tok // tq
    nkv = n_tok // ATT_KV
    has_bias = cum_b is not None

    def body(*refs):
        if has_bias:
            q_ref, k_ref, v_ref, o_ref, lse_ref, do_ref, cb_ref, dq_ref, dk_ref, dv_ref, dck_ref, dcq_ref, kt_ref = refs
        else:
            q_ref, k_ref, v_ref, o_ref, lse_ref, do_ref, dq_ref, dk_ref, dv_ref, kt_ref = refs
        p = pl.program_id(0)
        i = pl.program_id(1)

        @pl.when(i == 0)
        def _():
            dk_ref[...] = jnp.zeros_like(dk_ref)
            dv_ref[...] = jnp.zeros_like(dv_ref)
            for jb in range(nkv):
                for c0 in range(0, qw, LANE):
                    kt_ref[jb, c0:c0 + LANE, :] = k_ref[jb * ATT_KV:(jb + 1) * ATT_KV, c0:c0 + LANE].astype(f32).T.astype(bf16)

        if has_bias:
            @pl.when(jnp.logical_and(i == 0, p == 0))
            def _():
                dck_ref[...] = jnp.zeros_like(dck_ref)

        heads = _attn_heads(q_ref, mla)
        do = do_ref[...]
        do_t = do.T
        prod_t = (do * o_ref[...]).T
        row = lax.broadcasted_iota(jnp.int32, (LANE, tq), 0)
        lane = lax.broadcasted_iota(jnp.int32, (tq, LANE), 1)
        lane_k = lax.broadcasted_iota(jnp.int32, (ATT_KV, LANE), 1)
        per_head = []
        for e in (0, 1):
            sel_r = (row >= 64) == bool(e)
            per_head.append((jnp.where(sel_r, do_t, 0.0).astype(bf16),
                             jnp.where((lane >= 64) == bool(e), do, 0.0).astype(bf16),
                             jnp.sum(jnp.where(sel_r, prod_t, 0.0), axis=0, keepdims=True),
                             lse_ref[e:e + 1, :]))
        dq_rows = LANE if mla else 64

        def step(j, carry, masked, q_lo=0):
            off = pl.multiple_of(j * ATT_KV, ATT_KV)
            allowed = _attn_allowed(off, i, tq, mla)[:, q_lo:] if masked else None
            keep = lambda old, part: part if q_lo == 0 else jnp.concatenate([old[:, :q_lo], part], axis=1)
            vb = v_ref[pl.ds(off, ATT_KV), :]
            kt = kt_ref[j]
            cols = [slice(e * LANE, (e + 1) * LANE) if mla else slice(None) for e in (0, 1)]
            sts = [_nn(k_ref[pl.ds(off, ATT_KV), cols[e]], heads[e][0][:, q_lo:]) for e in (0, 1)]
            dpts = [_nn(vb, per_head[e][0][:, q_lo:]) for e in (0, 1)]
            mids = []
            for e in (0, 1):
                _, _, delta, lse_e = per_head[e]
                st = sts[e] * scale
                if has_bias:
                    st = st - jnp.tile(cb_ref[e, pl.ds(off, ATT_KV), :], (1, (tq - q_lo) // LANE))
                pt = jnp.exp(st - lse_e[:, q_lo:])
                if masked:
                    pt = jnp.where(allowed, pt, 0.0)
                dst = pt * (dpts[e] - delta[:, q_lo:])
                qsum = carry[e][1][:, q_lo:]
                if has_bias:
                    rs = jnp.sum(dst, axis=1, keepdims=True)
                    dck_ref[pl.ds(off, ATT_KV), :] += jnp.where(lane_k == 2 * p + e, -rs, 0.0)
                    qsum = qsum + jnp.sum(dst, axis=0, keepdims=True)
                mids.append((pt.astype(bf16), dst.astype(bf16), qsum))
            new = []
            for e in (0, 1):
                pt, dst, qsum = mids[e]
                kt_e = kt[e * LANE:(e + 1) * LANE, :] if mla else kt[64 * e:64 * e + 64, :]
                dq = carry[e][0][:, q_lo:] + _nn(kt_e, dst) * scale
                new.append((keep(carry[e][0], dq), keep(carry[e][1], qsum)))
                dk_ref[pl.ds(off, ATT_KV), cols[e]] += _nn(dst, heads[e][1][q_lo:, :]) * scale
                dv_ref[pl.ds(off, ATT_KV), :] += _nn(pt, per_head[e][1][q_lo:, :])
            return tuple(new)

        init = tuple((jnp.zeros((dq_rows, tq), f32), jnp.zeros((1, tq), f32)) for _ in (0, 1))
        n_full = i * (tq // ATT_KV)
        carry = lax.fori_loop(0, n_full, functools.partial(step, masked=False), init)
        for d in range(tq // ATT_KV):
            carry = step(n_full + d, carry, True, q_lo=d * ATT_KV)
        if mla:
            for e in (0, 1):
                dq_ref[:, e * LANE:(e + 1) * LANE] = carry[e][0].T
        else:
            dq_ref[...] = jnp.concatenate([carry[0][0], carry[1][0]], axis=0).T
        if has_bias:
            dcq_ref[...] = jnp.zeros_like(dcq_ref)
            for e in (0, 1):
                dcq_ref[e:e + 1, :] = carry[e][1]

    tile_q = pl.BlockSpec((tq, qw), lambda p, i: (i, p))
    tile_v = pl.BlockSpec((tq, LANE), lambda p, i: (i, p))
    full_k = pl.BlockSpec((n_tok, qw), lambda p, i: (0, p))
    full_v = pl.BlockSpec((n_tok, LANE), lambda p, i: (0, p))
    in_specs = [tile_q, full_k, full_v, tile_v, pl.BlockSpec((None, 8, tq), lambda p, i: (p, 0, i)), tile_v]
    args = [q, k, v, o, lse, do]
    out_specs = [tile_q, full_k, full_v]
    out_shape = [jax.ShapeDtypeStruct((n_tok, 4 * qw), f32), jax.ShapeDtypeStruct((n_tok, 4 * qw), f32),
                 jax.ShapeDtypeStruct((n_tok, 512), f32)]
    if has_bias:
        in_specs.append(pl.BlockSpec((2, n_tok, LANE), lambda p, i: (p, 0, 0)))
        args.append(cum_b)
        out_specs += [pl.BlockSpec((n_tok, LANE), _const_map(2)), pl.BlockSpec((None, 8, tq), lambda p, i: (p, 0, i))]
        out_shape += [jax.ShapeDtypeStruct((n_tok, LANE), f32), jax.ShapeDtypeStruct((4, 8, n_tok), f32)]
    return pl.pallas_call(
        body, grid=(4, nq), in_specs=in_specs, out_specs=out_specs, out_shape=out_shape, name=name,
        scratch_shapes=[pltpu.VMEM((nkv, qw, ATT_KV), bf16)],
        compiler_params=pltpu.CompilerParams(dimension_semantics=("arbitrary", "arbitrary"), vmem_limit_bytes=VMEM_LIMIT),
    )(*args)


def _s5_disc(lr, li, ldt):
    dt = jnp.exp(ldt)
    mag = jnp.exp(lr * dt)
    a_re = mag * jnp.cos(li * dt)
    a_im = mag * jnp.sin(li * dt)
    den = lr * lr + li * li
    f_re = ((a_re - 1.0) * lr + a_im * li) / den
    f_im = (a_im * lr - (a_re - 1.0) * li) / den
    return a_re, a_im, f_re, f_im


def _s5_param_fwd(lr, li, ldt, b_re, b_im):
    def body(lr_ref, li_ref, ldt_ref, br_ref, bi_ref, ar_ref, ai_ref, bbr_ref, bbi_ref):
        a_re, a_im, f_re, f_im = _s5_disc(lr_ref[...], li_ref[...], ldt_ref[...])
        ar_ref[...] = a_re
        ai_ref[...] = a_im
        br, bi = br_ref[...], bi_ref[...]
        bbr_ref[...] = f_re * br - f_im * bi
        bbi_ref[...] = f_re * bi + f_im * br

    col = jax.ShapeDtypeStruct(lr.shape, f32)
    mat = jax.ShapeDtypeStruct(b_re.shape, f32)
    return pl.pallas_call(body, out_shape=[col, col, mat, mat], name="s5_param_fwd")(lr, li, ldt, b_re, b_im)


def _s5_param_bwd(lr, li, ldt, b_re, b_im, da_re, da_im, dbb_re, dbb_im):
    def body(lr_ref, li_ref, ldt_ref, br_ref, bi_ref, dar_ref, dai_ref, gbr_ref, gbi_ref,
             dlr_ref, dli_ref, dldt_ref, dbr_ref, dbi_ref):
        (a_re, a_im, f_re, f_im), vjp = jax.vjp(_s5_disc, lr_ref[...], li_ref[...], ldt_ref[...])
        br, bi, gr, gi = br_ref[...], bi_ref[...], gbr_ref[...], gbi_ref[...]
        dbr_ref[...] = f_re * gr + f_im * gi
        dbi_ref[...] = f_re * gi - f_im * gr
        dfr = jnp.sum(br * gr + bi * gi, axis=-1, keepdims=True)
        dfi = jnp.sum(br * gi - bi * gr, axis=-1, keepdims=True)
        dlr, dli, dldt = vjp((dar_ref[...], dai_ref[...], dfr, dfi))
        dlr_ref[...] = dlr
        dli_ref[...] = dli
        dldt_ref[...] = jnp.sum(dldt.reshape(S5_GROUPS, S5_STATE, 1), axis=1)

    col = jax.ShapeDtypeStruct((S5_LANES, 1), f32)
    mat = jax.ShapeDtypeStruct((S5_LANES, S5_GROUP), f32)
    return pl.pallas_call(body, out_shape=[col, col, jax.ShapeDtypeStruct((S5_GROUPS, 1), f32), mat, mat],
                          name="s5_param_bwd")(lr, li, ldt, b_re, b_im, da_re, da_im, dbb_re, dbb_im)


_SCAN_NB = 4


def _to_streams(a):
    s, c = a.shape
    return jnp.swapaxes(a.reshape(8, s // 8, c), 0, 1).reshape(s, c)


def _from_streams(a):
    s, c = a.shape
    return jnp.swapaxes(a.reshape(s // 8, 8, c), 0, 1).reshape(s, c)


def _s5_scan(name, src, wq, wy, add, y_dtype, a_re8, a_im8, *, reverse, n_tok, grads_of=None):
    rows = n_tok // 8
    nb = _SCAN_NB
    assert nb == 4

    def scan_body(src_ref, w_ref, wy_ref, add_ref, ar_ref, ai_ref, x_ref, y_ref):
        for ri in (0, 1):
            bu = _nn(src_ref[...], w_ref[ri])
            for b in range(nb):
                x_ref[ri, b] = bu[:, b * LANE:(b + 1) * LANE]
        bu_ref = x_ref
        a_r = [ar_ref[b] for b in range(nb)]
        a_i = [ai_ref[b] for b in range(nb)]
        zero = jnp.zeros((8, LANE), f32)
        one = jnp.ones((8, LANE), f32)

        def rows_at(r):
            rr = (rows - 1 - r) if reverse else r
            return pl.ds(pl.multiple_of(rr * 8, 8), 8)

        def pass1(r, carry):
            out = []
            sl = rows_at(r)
            for b in range(nb):
                xr, xi, mr, mi = carry[b]
                nr = a_r[b] * xr - a_i[b] * xi + bu_ref[0, b, sl, :]
                ni = a_r[b] * xi + a_i[b] * xr + bu_ref[1, b, sl, :]
                x_ref[0, b, sl, :] = nr
                x_ref[1, b, sl, :] = ni
                out.append((nr, ni, a_r[b] * mr - a_i[b] * mi, a_r[b] * mi + a_i[b] * mr))
            return tuple(out)

        carry = lax.fori_loop(0, rows, pass1, tuple((zero, zero, one, zero) for _ in range(nb)))
        sub = lax.broadcasted_iota(jnp.int32, (8, LANE), 0)
        feed = []
        for b in range(nb):
            lr_, li_, pr, pi = carry[b]
            fr, fi = zero, zero
            for _ in range(7):
                tr = lr_ + pr * fr - pi * fi
                ti = li_ + pr * fi + pi * fr
                if reverse:
                    fr = jnp.where(sub < 7, pltpu.roll(tr, 7, 0), 0.0)
                    fi = jnp.where(sub < 7, pltpu.roll(ti, 7, 0), 0.0)
                else:
                    fr = jnp.where(sub > 0, pltpu.roll(tr, 1, 0), 0.0)
                    fi = jnp.where(sub > 0, pltpu.roll(ti, 1, 0), 0.0)
            feed.append((fr, fi))

        def pass2(r, carry):
            out = []
            sl = rows_at(r)
            for b in range(nb):
                mr, mi = carry[b]
                fr, fi = feed[b]
                x_ref[0, b, sl, :] += mr * fr - mi * fi
                x_ref[1, b, sl, :] += mr * fi + mi * fr
                out.append((a_r[b] * mr - a_i[b] * mi, a_r[b] * mi + a_i[b] * mr))
            return tuple(out)

        lax.fori_loop(0, rows, pass2, tuple((a_r[b], a_i[b]) for b in range(nb)))

        y = None
        for ri in (0, 1):
            for b in range(nb):
                t = _nn(x_ref[ri, b], wy_ref[ri, b * LANE:(b + 1) * LANE, :])
                y = t if y is None else y + t
        if add is not None:
            y = y + add_ref[...]
        y_ref[...] = y.astype(y_ref.dtype)

    def grads_body(src_ref, xs_ref, u_ref, g_ref, da_ref, dc_ref, db_ref):
        t = lax.broadcasted_iota(jnp.int32, (n_tok, LANE), 0)
        sub = lax.broadcasted_iota(jnp.int32, (8, LANE), 0)

        def prev(v):
            return (jnp.where(t >= 8, pltpu.roll(v, 8, 0), 0.0),
                    jnp.where(sub > 0, pltpu.roll(v[n_tok - 8:, :], 1, 0), 0.0))

        for b in range(nb):
            (xr, hr), (xi, hi) = prev(xs_ref[0, b]), prev(xs_ref[1, b])
            gr, gi = g_ref[0, b], g_ref[1, b]
            gr0, gi0 = gr[0:8, :], gi[0:8, :]
            da_ref[b, 0:1, :] = (jnp.sum(xr * gr + xi * gi, axis=0, keepdims=True)
                                 + jnp.sum(hr * gr0 + hi * gi0, axis=0, keepdims=True))
            da_ref[b, 1:2, :] = (jnp.sum(xr * gi - xi * gr, axis=0, keepdims=True)
                                 + jnp.sum(hr * gi0 - hi * gr0, axis=0, keepdims=True))
            for ri in (0, 1):
                dc_ref[ri, b * LANE:(b + 1) * LANE, :] = _tn(xs_ref[ri, b], src_ref[...])
                db_ref[ri, :, b * LANE:(b + 1) * LANE] = _tn(u_ref[...], g_ref[ri, b])

    n_in = 3 + (add is not None) + 2 * (grads_of is not None)

    def body(*refs):
        ins, rest = list(refs[:n_in]), refs[n_in:]
        src_ref, w_ref, wy_ref = ins[:3]
        add_ref = ins[3] if add is not None else None
        ar_ref, ai_ref = rest[:2]
        if grads_of is None:
            x_ref, y_ref = rest[2:]
            scan_body(src_ref, w_ref, wy_ref, add_ref, ar_ref, ai_ref, x_ref, y_ref)
        else:
            xs_ref, u_ref = ins[-2:]
            y_ref, da_ref, dc_ref, db_ref, x_ref = rest[2:]
            scan_body(src_ref, w_ref, wy_ref, add_ref, ar_ref, ai_ref, x_ref, y_ref)
            grads_body(src_ref, xs_ref, u_ref, x_ref, da_ref, dc_ref, db_ref)

    blk = pl.BlockSpec((2, nb, n_tok, LANE), lambda g: (0, g, 0, 0))
    ablk = pl.BlockSpec((nb, 8, LANE), lambda g: (g, 0, 0))
    col = pl.BlockSpec((n_tok, LANE), lambda g: (0, g))
    in_specs = [col, pl.BlockSpec((2, None, LANE, 512), lambda g: (0, g, 0, 0)), pl.BlockSpec((2, None, 512, LANE), lambda g: (0, g, 0, 0))]
    args = [src, wq, wy]
    if add is not None:
        in_specs.append(col)
        args.append(add)
    y_shape = jax.ShapeDtypeStruct((n_tok, 512), y_dtype)
    x_shape = (2, S5_BLOCKS, n_tok, LANE)
    params = pltpu.CompilerParams(dimension_semantics=("arbitrary",), vmem_limit_bytes=VMEM_LIMIT)
    if grads_of is None:
        return pl.pallas_call(
            body, grid=(S5_BLOCKS // nb,), in_specs=in_specs + [ablk, ablk], out_specs=[blk, col],
            out_shape=[jax.ShapeDtypeStruct(x_shape, f32), y_shape], name=name, compiler_params=params,
        )(*args, a_re8, a_im8)
    return pl.pallas_call(
        body, grid=(S5_BLOCKS // nb,), in_specs=in_specs + [blk, col, ablk, ablk],
        out_specs=[col, pl.BlockSpec((nb, 2, LANE), lambda g: (g, 0, 0)), pl.BlockSpec((2, None, 512, LANE), lambda g: (0, g, 0, 0)),
                   pl.BlockSpec((2, None, LANE, 512), lambda g: (0, g, 0, 0))],
        out_shape=[y_shape, jax.ShapeDtypeStruct((S5_BLOCKS, 2, LANE), f32), jax.ShapeDtypeStruct((2, S5_Q, 512, LANE), f32),
                   jax.ShapeDtypeStruct((2, S5_Q, LANE, 512), f32)],
        scratch_shapes=[pltpu.VMEM((2, nb, n_tok, LANE), f32)], name=name, compiler_params=params,
    )(*args, *grads_of, a_re8, a_im8)


S5_Q = 4


def _bd8(t):
    _, a, b = t.shape
    t = t.reshape(S5_Q, 8, a, 1, b)
    eye = jnp.eye(8, dtype=jnp.bool_).reshape(1, 8, 1, 8, 1)
    return jnp.where(eye, jnp.broadcast_to(t, (S5_Q, 8, a, 8, b)), jnp.zeros((), t.dtype)).reshape(S5_Q, 8 * a, 8 * b)


def _bd8_diag(m, a, b):
    m = m.reshape(S5_Q, 8, a, 8, b)
    eye = jnp.eye(8, dtype=jnp.bool_).reshape(1, 8, 1, 8, 1)
    return jnp.sum(jnp.where(eye, m, 0.0), axis=3).reshape(S5_GROUPS, a, b)


N_CHIPS = 4
_BIG_SHARD = {"w_in": (1, 1024, 1770, 1792), "mla_w_q_up": (1, 256, 192, 256), "mla_w_kv_up": (1, 128, 256, 256),
              "s5_w_glu": (0, 128, 512, 512), "w_branch_out": (0, 384, 1024, 1024), "w_out": (0, 256, 1024, 1024)}


def _to_shards(name, m):
    axis, r, c, cp = _BIG_SHARD[name]
    if axis == 0:
        return m.reshape(N_CHIPS, r, c)
    return jnp.stack([jnp.pad(m[:, j * c:(j + 1) * c], ((0, 0), (0, cp - c))) for j in range(N_CHIPS)])


def _from_shards(name, s):
    axis, r, c, cp = _BIG_SHARD[name]
    if axis == 0:
        return s.reshape(N_CHIPS * r, c)
    return jnp.concatenate([s[j, :, :c] for j in range(N_CHIPS)], axis=1)


def _pad_w_in(w):
    pieces, pos = [], 0
    for name in _PAD_ORDER:
        start, width, inner = _PAD[name]
        o0, ow = _ORIG[name]
        if start + inner > pos:
            pieces.append(jnp.zeros((w.shape[0], start + inner - pos), w.dtype))
        pieces.append(w[:, o0:o0 + ow])
        pos = start + inner + ow
    pieces.append(jnp.zeros((w.shape[0], NP - pos), w.dtype))
    return jnp.concatenate(pieces, axis=1)


def _prep_weights(small, big):
    per_layer = jax.vmap
    w = {}
    w["w_in_shards"] = big["w_in"]
    w["w_in"] = per_layer(lambda s: _pad_w_in(_from_shards("w_in", s)))(big["w_in"])

    def q_up(s):
        wq = _from_shards("mla_w_q_up", s).reshape(MLA_Q_RANK, HEADS, MLA_QK)
        return jnp.pad(wq, ((0, 0), (0, 0), (0, LANE - MLA_QK))).reshape(MLA_Q_RANK, HEADS * LANE)

    def kv_up(s):
        wkv = _from_shards("mla_w_kv_up", s).reshape(MLA_KV_RANK, HEADS, 128)
        wk = jnp.pad(wkv[:, :, :64], ((0, 0), (0, 0), (0, 64))).reshape(MLA_KV_RANK, HEADS * LANE)
        return jnp.concatenate([wk, wkv[:, :, 64:].reshape(MLA_KV_RANK, 512)], axis=1)

    w["wq"] = per_layer(q_up)(big["mla_w_q_up"])
    w["wkv"] = per_layer(kv_up)(big["mla_w_kv_up"])
    for name, key in (("w_glu", "s5_w_glu"), ("wo", "w_branch_out"), ("w_out", "w_out")):
        w[name] = per_layer(functools.partial(_from_shards, key))(big[key])
    row = lambda a: a.astype(f32)[:, None, :]
    lanes = lambda a, n: jnp.pad(row(a), ((0, 0), (0, 0), (0, LANE - n)))
    w["norm_g"] = row(small["norm_g"])
    w["qa_g"] = row(small["mla_q_a_norm"])
    w["kva_g"] = row(small["mla_kv_a_norm"])
    w["qn_g"] = lanes(small["mla_q_norm"], MLA_QK)
    w["kn_g"] = lanes(small["mla_k_norm"], MLA_QK)
    w["fq_g"] = jnp.tile(row(small["fox_q_norm"]), (1, 1, 2))
    w["fk_g"] = jnp.tile(row(small["fox_k_norm"]), (1, 1, 2))
    w["b_f"] = lanes(small["fox_b_f"], HEADS)
    w["lr"] = small["s5_lambda_re"].reshape(DEPTH, S5_LANES, 1)
    w["li"] = small["s5_lambda_im"].reshape(DEPTH, S5_LANES, 1)
    w["ldt"] = jnp.repeat(small["s5_log_dt"], S5_STATE, axis=1).reshape(DEPTH, S5_LANES, 1)
    w["b_re"] = small["s5_b_re"].reshape(DEPTH, S5_LANES, S5_GROUP)
    w["b_im"] = small["s5_b_im"].reshape(DEPTH, S5_LANES, S5_GROUP)
    w["s5_d"] = row(small["s5_d"])
    w["b_glu"] = row(small["s5_b_glu"])
    a_re, a_im, bb_re, bb_im = _s5_param_fwd(w["lr"], w["li"], w["ldt"], w["b_re"], w["b_im"])
    per_group = lambda m: m.reshape(S5_GROUPS, S5_STATE, S5_GROUP)
    pair = lambda f: per_layer(lambda re, im: jnp.stack([f(re), f(im)]).astype(bf16))
    c_re, c_im = small["s5_c_re"], -small["s5_c_im"]
    w["b_cn"] = pair(lambda m: _bd8(jnp.swapaxes(per_group(m), 1, 2)))(bb_re, bb_im)
    w["b_nc"] = pair(lambda m: _bd8(per_group(m)))(bb_re, bb_im)
    w["c_nc"] = pair(lambda m: _bd8(jnp.swapaxes(m, 1, 2)))(c_re, c_im)
    w["c_cn"] = pair(_bd8)(c_re, c_im)
    sublanes = lambda a: jnp.broadcast_to(a.reshape(DEPTH, S5_BLOCKS, 1, LANE), (DEPTH, S5_BLOCKS, 8, LANE))
    w["a_re8"], w["a_im8"], w["a_im8_neg"] = sublanes(a_re), sublanes(a_im), sublanes(-a_im)
    return w


def _fox_halves(x, lane):
    sq = x * x
    lo = jnp.sum(jnp.where(lane < 64, sq, 0.0), axis=-1, keepdims=True)
    hi = jnp.sum(sq, axis=-1, keepdims=True) - lo
    return jnp.where(lane < 64, lax.rsqrt(lo * (1.0 / 64) + EPS), lax.rsqrt(hi * (1.0 / 64) + EPS))


def _fox_halves_bwd(dy, x, r, g, lane):
    xh = x * r
    dxh = dy * g
    pr = dxh * xh
    lo = jnp.sum(jnp.where(lane < 64, pr, 0.0), axis=-1, keepdims=True)
    hi = jnp.sum(pr, axis=-1, keepdims=True) - lo
    mean = jnp.where(lane < 64, lo, hi) * (1.0 / 64)
    return r * (dxh - xh * mean), jnp.sum(dy * xh, axis=0, keepdims=True)


def _mla_recompute(cq, ckv, kpe, c, s1, s2, qa_g, kva_g, wq, wkv):
    cqn, r_cq = _rms(cq, qa_g, MLA_Q_RANK)
    ckvn, r_ckv = _rms(ckv, kva_g, MLA_KV_RANK)
    cqn_b = cqn.astype(bf16)
    ckvn_b = ckvn.astype(bf16)
    q_raw = _nn(cqn_b, wq)
    kv_raw = _nn(ckvn_b, wkv)
    kpe_rot = _rope(kpe, c, s1, s2)
    return cqn_b, r_cq, ckvn_b, r_ckv, q_raw, kv_raw, kpe_rot


def _layer_fwd(x, w, rope_tabs, n_tok):
    c_tab, s1_tab, s2_tab = rope_tabs
    saved = {"x": x}

    def norm_body(x_ref, g_ref, h_ref):
        h_ref[...] = _rms(x_ref[...], g_ref[...], D_MODEL)[0].astype(bf16)

    (h,) = _rowwise("norm_fwd", norm_body, n_tok, [(x, D_MODEL, 0)], [w["norm_g"]], [(D_MODEL, bf16)], [])
    proj = _mm_nn("in_proj", h, w["w_in"], m=n_tok, n=NP, k=D_MODEL, tm=n_tok, tn=512, tk=D_MODEL)
    saved["h"], saved["proj"] = h, proj

    def mla_prep_body(cq_ref, ckv_ref, kpe_ref, c_ref, s1_ref, s2_ref, qa_ref, kva_ref, wq_ref, wkv_ref, qn_g_ref, kn_g_ref,
                      qn_ref, kn_ref, v_ref):
        c, s1, s2 = c_ref[...], s1_ref[...], s2_ref[...]
        _, _, _, _, q_raw, kv_raw, kpe_rot = _mla_recompute(cq_ref[...], ckv_ref[...], kpe_ref[...], c, s1, s2,
                                                            qa_ref[...], kva_ref[...], wq_ref[...], wkv_ref[...])
        for hd in range(HEADS):
            sl = slice(hd * LANE, (hd + 1) * LANE)
            qn_ref[:, sl] = _rms(_rope(q_raw[:, sl], c, s1, s2), qn_g_ref[...], MLA_QK)[0].astype(bf16)
            kn_ref[:, sl] = _rms(kv_raw[:, sl] + kpe_rot, kn_g_ref[...], MLA_QK)[0].astype(bf16)
        v_ref[...] = kv_raw[:, HEADS * LANE:].astype(bf16)

    qn, kn, v_mla = _rowwise(
        "mla_prep", mla_prep_body, n_tok,
        [(proj, *_seg("cq")), (proj, *_seg("ckv")), (proj, *_seg("kpe")), (c_tab, LANE, 0), (s1_tab, LANE, 0), (s2_tab, LANE, 0)],
        [w["qa_g"], w["kva_g"], w["wq"], w["wkv"], w["qn_g"], w["kn_g"]],
        [(HEADS * LANE, bf16), (HEADS * LANE, bf16), (512, bf16)], [])
    y_mla, lse_mla = _attn_fwd("mla_attn_fwd", qn, kn, v_mla, None, mla=True, n_tok=n_tok)
    saved.update(qn=qn, kn=kn, v_mla=v_mla, y_mla=y_mla, lse_mla=lse_mla)

    def fox_prep_body(fq_ref, fk_ref, fv_ref, ff_ref, qg_ref, kg_ref, bf_ref, fqn_ref, fkn_ref, fvb_ref, logf_ref):
        lane = lax.broadcasted_iota(jnp.int32, (TOK, LANE), 1)
        for blk in range(4):
            sl = slice(blk * LANE, (blk + 1) * LANE)
            xq = fq_ref[:, sl]
            fqn_ref[:, sl] = (xq * _fox_halves(xq, lane) * qg_ref[...]).astype(bf16)
            xk = fk_ref[:, sl]
            fkn_ref[:, sl] = (xk * _fox_halves(xk, lane) * kg_ref[...]).astype(bf16)
        fvb_ref[...] = fv_ref[...].astype(bf16)
        z = ff_ref[...] + bf_ref[...]
        logf_ref[...] = jnp.minimum(z, 0.0) - jnp.log(1.0 + jnp.exp(-jnp.abs(z)))

    fqn, fkn, fvb, logf = _rowwise(
        "fox_prep", fox_prep_body, n_tok,
        [(proj, *_seg("fq")), (proj, *_seg("fk")), (proj, *_seg("fv")), (proj, *_seg("ff"))],
        [w["fq_g"], w["fk_g"], w["b_f"]],
        [(512, bf16), (512, bf16), (512, bf16), (LANE, f32)], [])

    def cum_body(x_ref, cum_ref):
        x = x_ref[...]
        t = lax.broadcasted_iota(jnp.int32, x.shape, 0)
        s = 1
        while s < n_tok:
            x = x + jnp.where(t >= s, pltpu.roll(x, s, 0), 0.0)
            s *= 2
        for hd in range(HEADS):
            cum_ref[hd] = jnp.broadcast_to(x[:, hd:hd + 1], (n_tok, LANE))

    cum_b = pl.pallas_call(cum_body, out_shape=jax.ShapeDtypeStruct((HEADS, n_tok, LANE), f32), name="fox_cum")(logf)
    y_fox, lse_fox = _attn_fwd("fox_attn_fwd", fqn, fkn, fvb, cum_b, mla=False, n_tok=n_tok)
    saved.update(fqn=fqn, fkn=fkn, fvb=fvb, cum_b=cum_b, y_fox=y_fox, lse_fox=lse_fox)

    u_w, u_blk = _seg("s5u")
    u_streams = _to_streams(proj[:, u_blk * u_w:(u_blk + 1) * u_w])
    xs, ylin = _s5_scan("s5_scan_fwd", u_streams, w["b_cn"], w["c_nc"], None, f32, w["a_re8"], w["a_im8"], reverse=False, n_tok=n_tok)
    ylin = _from_streams(ylin)

    def s5_post_body(yl_ref, u_ref, d_ref, wg_ref, bg_ref, out_ref):
        y = yl_ref[...] + d_ref[...] * u_ref[...]
        z, _ = _gelu(y)
        out_ref[...] = z * _sigmoid(_nn(z, wg_ref[...]) + bg_ref[...])

    (y_s5,) = _rowwise("s5_post", s5_post_body, n_tok, [(ylin, 512, 0), (proj, u_w, u_blk)],
                       [w["s5_d"], w["w_glu"], w["b_glu"]], [(512, f32)], [])
    saved.update(xs=xs, ylin=ylin, y_s5=y_s5, u_streams=u_streams)

    def merge_body(ym_ref, yf_ref, ys_ref, gm_ref, gf_ref, gs_ref, mm_ref, mf_ref, ms_ref, x_ref, wo_ref, wout_ref, out_ref):
        merged = jnp.zeros((TOK, D_MODEL), f32)
        for b, (y_ref, g_ref, m_ref) in enumerate(((ym_ref, gm_ref, mm_ref), (yf_ref, gf_ref, mf_ref), (ys_ref, gs_ref, ms_ref))):
            g = g_ref[...]
            a = y_ref[...] * (g * _sigmoid(g))
            merged = merged + _sigmoid(m_ref[...]) * _nn(a, wo_ref[b * 512:(b + 1) * 512, :])
        out_ref[...] = x_ref[...] + _nn(merged, wout_ref[...])

    (out,) = _rowwise(
        "merge_fwd", merge_body, n_tok,
        [(y_mla, 512, 0), (y_fox, 512, 0), (y_s5, 512, 0), (proj, *_seg("g_mla")), (proj, *_seg("g_fox")), (proj, *_seg("g_s5")),
         (proj, *_seg("m_mla")), (proj, *_seg("m_fox")), (proj, *_seg("m_s5")), (x, D_MODEL, 0)],
        [w["wo"], w["w_out"]], [(D_MODEL, f32)], [])
    return out, saved


def _layer_bwd(dout, w, sv, rope_tabs, n_tok):
    c_tab, s1_tab, s2_tab = rope_tabs
    proj, x = sv["proj"], sv["x"]
    grads = {}

    def merge_bwd_body(ym_ref, yf_ref, ys_ref, gm_ref, gf_ref, gs_ref, mm_ref, mf_ref, ms_ref, do_ref, wo_ref, wout_ref,
                       dym_ref, dyf_ref, dys_ref, dgm_ref, dgf_ref, dgs_ref, dmm_ref, dmf_ref, dms_ref, dwo_ref, dwout_ref):
        do = do_ref[...]
        branches = ((ym_ref, gm_ref, mm_ref, dym_ref, dgm_ref, dmm_ref), (yf_ref, gf_ref, mf_ref, dyf_ref, dgf_ref, dmf_ref),
                    (ys_ref, gs_ref, ms_ref, dys_ref, dgs_ref, dms_ref))
        acts, outs, sigs = [], [], []
        merged = jnp.zeros((TOK, D_MODEL), f32)
        for b, (y_ref, g_ref, m_ref, _, _, _) in enumerate(branches):
            g = g_ref[...]
            a = (y_ref[...] * (g * _sigmoid(g))).astype(bf16)
            o = _nn(a, wo_ref[b * 512:(b + 1) * 512, :])
            s = _sigmoid(m_ref[...])
            merged = merged + s * o
            acts.append(a)
            outs.append(o)
            sigs.append(s)
        dmerged = _nt(do, wout_ref[...])
        _accumulate(dwout_ref, _tn(merged, do))
        dwo = []
        for b, (y_ref, g_ref, m_ref, dy_ref, dg_ref, dm_ref) in enumerate(branches):
            s, o = sigs[b], outs[b]
            dm_ref[...] = (dmerged * o * s * (1.0 - s)).astype(bf16)
            d_o = dmerged * s
            da = _nt(d_o, wo_ref[b * 512:(b + 1) * 512, :])
            dwo.append(_tn(acts[b], d_o))
            g = g_ref[...]
            sg = _sigmoid(g)
            dy_ref[...] = da * (g * sg)
            dg_ref[...] = (da * y_ref[...] * (sg * (1.0 + g * (1.0 - sg)))).astype(bf16)
        _accumulate(dwo_ref, jnp.concatenate(dwo, axis=0))

    (dy_mla, dy_fox, dy_s5, dg_mla, dg_fox, dg_s5, dm_mla, dm_fox, dm_s5, dwo, dwout) = _rowwise(
        "merge_bwd", merge_bwd_body, n_tok,
        [(sv["y_mla"], 512, 0), (sv["y_fox"], 512, 0), (sv["y_s5"], 512, 0), (proj, *_seg("g_mla")), (proj, *_seg("g_fox")),
         (proj, *_seg("g_s5")), (proj, *_seg("m_mla")), (proj, *_seg("m_fox")), (proj, *_seg("m_s5")), (dout, D_MODEL, 0)],
        [w["wo"], w["w_out"]],
        [(512, f32)] * 3 + [(512, bf16)] * 3 + [(D_MODEL, bf16)] * 3, [((1536, D_MODEL), f32), ((D_MODEL, D_MODEL), f32)])
    grads["w_branch_out"], grads["w_out"] = dwo, dwout

    u_w, u_blk = _seg("s5u")

    def s5_post_bwd_body(yl_ref, u_ref, do_ref, d_ref, wg_ref, bg_ref, dyl_ref, dus_ref, dd_ref, dwg_ref, dbg_ref):
        u = u_ref[...]
        y = yl_ref[...] + d_ref[...] * u
        z, t = _gelu(y)
        s = _sigmoid(_nn(z, wg_ref[...]) + bg_ref[...])
        do = do_ref[...]
        dgl = do * z * s * (1.0 - s)
        dz = do * s + _nt(dgl, wg_ref[...])
        dy = dz * _gelu_grad(y, t)
        dyl_ref[...] = dy.astype(bf16)
        dus_ref[...] = dy * d_ref[...]
        _accumulate(dd_ref, jnp.sum(dy * u, axis=0, keepdims=True))
        _accumulate(dwg_ref, _tn(z, dgl))
        _accumulate(dbg_ref, jnp.sum(dgl, axis=0, keepdims=True))

    dylin, du_skip, dd, dwglu, dbglu = _rowwise(
        "s5_post_bwd", s5_post_bwd_body, n_tok, [(sv["ylin"], 512, 0), (proj, u_w, u_blk), (dy_s5, 512, 0)],
        [w["s5_d"], w["w_glu"], w["b_glu"]], [(512, bf16), (512, f32)], [((1, 512), f32), ((512, 512), f32), ((1, 512), f32)])
    grads["s5_d"], grads["s5_w_glu"], grads["s5_b_glu"] = dd.reshape(512), dwglu, dbglu.reshape(512)

    dylin = _to_streams(dylin)
    ds5u, da, dc_nc, db_cn = _s5_scan("s5_scan_bwd", dylin, w["c_cn"], w["b_nc"], _to_streams(du_skip), bf16, w["a_re8"],
                                      w["a_im8_neg"], reverse=True, n_tok=n_tok, grads_of=(sv["xs"], sv["u_streams"]))
    ds5u = _from_streams(ds5u)
    diag_b = lambda m: jnp.swapaxes(_bd8_diag(m, S5_GROUP, S5_STATE), 1, 2).reshape(S5_LANES, S5_GROUP)
    diag_c = lambda m: jnp.swapaxes(_bd8_diag(m, S5_STATE, S5_GROUP), 1, 2)
    dlr, dli, dldt, db_re, db_im = _s5_param_bwd(
        w["lr"], w["li"], w["ldt"], w["b_re"], w["b_im"], da[:, 0, :].reshape(S5_LANES, 1), da[:, 1, :].reshape(S5_LANES, 1),
        diag_b(db_cn[0]), diag_b(db_cn[1]))
    grads["s5_lambda_re"] = dlr.reshape(S5_GROUPS, S5_STATE)
    grads["s5_lambda_im"] = dli.reshape(S5_GROUPS, S5_STATE)
    grads["s5_log_dt"] = dldt.reshape(S5_GROUPS)
    grads["s5_b_re"] = db_re.reshape(S5_GROUPS, S5_STATE, S5_GROUP)
    grads["s5_b_im"] = db_im.reshape(S5_GROUPS, S5_STATE, S5_GROUP)
    grads["s5_c_re"] = diag_c(dc_nc[0])
    grads["s5_c_im"] = -diag_c(dc_nc[1])

    dfqn, dfkn, dfv, dck, dcq = _attn_bwd("fox_attn_bwd", sv["fqn"], sv["fkn"], sv["fvb"], sv["y_fox"], sv["lse_fox"], dy_fox,
                                          sv["cum_b"], mla=False, n_tok=n_tok)
    dcq = jnp.pad(dcq[:, :2, :].reshape(HEADS, n_tok).T, ((0, 0), (0, LANE - HEADS)))

    def fox_gate_bwd_body(dk_ref, dq_ref, ff_ref, bf_ref, dff_ref, dbf_ref):
        xg = dk_ref[...] + dq_ref[...]
        t = lax.broadcasted_iota(jnp.int32, xg.shape, 0)
        s = 1
        while s < n_tok:
            xg = xg + jnp.where(t < n_tok - s, pltpu.roll(xg, n_tok - s, 0), 0.0)
            s *= 2
        dff = xg * _sigmoid(-(ff_ref[...] + bf_ref[...]))
        dff_ref[...] = dff.astype(bf16)
        dbf_ref[...] = jnp.sum(dff, axis=0, keepdims=True)

    ff_w, ff_blk = _seg("ff")
    dff, dbf = pl.pallas_call(
        fox_gate_bwd_body, grid=(1,),
        in_specs=[pl.BlockSpec((n_tok, LANE), lambda i: (0, 0)), pl.BlockSpec((n_tok, LANE), lambda i: (0, 0)),
                  pl.BlockSpec((n_tok, ff_w), lambda i: (0, ff_blk)), pl.BlockSpec((1, LANE), lambda i: (0, 0))],
        out_specs=[pl.BlockSpec((n_tok, LANE), lambda i: (0, 0)), pl.BlockSpec((1, LANE), lambda i: (0, 0))],
        out_shape=[jax.ShapeDtypeStruct((n_tok, LANE), bf16), jax.ShapeDtypeStruct((1, LANE), f32)], name="fox_gate_bwd",
    )(dck, dcq, proj, w["b_f"])
    grads["fox_b_f"] = dbf[0, :HEADS]

    def fox_prep_bwd_body(fq_ref, fk_ref, dqn_ref, dkn_ref, dv_ref, qg_ref, kg_ref, dfq_ref, dfk_ref, dfv_ref, dqg_ref, dkg_ref):
        lane = lax.broadcasted_iota(jnp.int32, (TOK, LANE), 1)
        dqg = jnp.zeros((1, LANE), f32)
        dkg = jnp.zeros((1, LANE), f32)
        for blk in range(4):
            sl = slice(blk * LANE, (blk + 1) * LANE)
            xq = fq_ref[:, sl]
            dx, dg = _fox_halves_bwd(dqn_ref[:, sl], xq, _fox_halves(xq, lane), qg_ref[...], lane)
            dfq_ref[:, sl] = dx.astype(bf16)
            dqg = dqg + dg
            xk = fk_ref[:, sl]
            dx, dg = _fox_halves_bwd(dkn_ref[:, sl], xk, _fox_halves(xk, lane), kg_ref[...], lane)
            dfk_ref[:, sl] = dx.astype(bf16)
            dkg = dkg + dg
        dfv_ref[...] = dv_ref[...].astype(bf16)
        _accumulate(dqg_ref, dqg + pltpu.roll(dqg, 64, 1))
        _accumulate(dkg_ref, dkg + pltpu.roll(dkg, 64, 1))

    dfq, dfk, dfvb, dfqg, dfkg = _rowwise(
        "fox_prep_bwd", fox_prep_bwd_body, n_tok,
        [(proj, *_seg("fq")), (proj, *_seg("fk")), (dfqn, 512, 0), (dfkn, 512, 0), (dfv, 512, 0)],
        [w["fq_g"], w["fk_g"]], [(512, bf16)] * 3, [((1, LANE), f32)] * 2)
    grads["fox_q_norm"], grads["fox_k_norm"] = dfqg[0, :FOX_DIM], dfkg[0, :FOX_DIM]

    dqn, dkn, dv_mla = _attn_bwd("mla_attn_bwd", sv["qn"], sv["kn"], sv["v_mla"], sv["y_mla"], sv["lse_mla"], dy_mla,
                                 None, mla=True, n_tok=n_tok)

    def mla_prep_bwd_body(cq_ref, ckv_ref, kpe_ref, c_ref, s1_ref, s2_ref, dqn_ref, dkn_ref, dv_ref,
                          qa_ref, kva_ref, wq_ref, wkv_ref, qn_g_ref, kn_g_ref,
                          dcq_ref, dckv_ref, dkpe_ref, dwq_ref, dwkv_ref, dqa_ref, dkva_ref, dqng_ref, dkng_ref):
        c, s1, s2 = c_ref[...], s1_ref[...], s2_ref[...]
        cq, ckv = cq_ref[...], ckv_ref[...]
        cqn_b, r_cq, ckvn_b, r_ckv, q_raw, kv_raw, kpe_rot = _mla_recompute(
            cq, ckv, kpe_ref[...], c, s1, s2, qa_ref[...], kva_ref[...], wq_ref[...], wkv_ref[...])
        lane = lax.broadcasted_iota(jnp.int32, (TOK, LANE), 1)
        dq_raw, dk_raw = [], []
        dkpe_rot = jnp.zeros((TOK, LANE), f32)
        dqng = jnp.zeros((1, LANE), f32)
        dkng = jnp.zeros((1, LANE), f32)
        for hd in range(HEADS):
            sl = slice(hd * LANE, (hd + 1) * LANE)
            q_rot = _rope(q_raw[:, sl], c, s1, s2)
            r = lax.rsqrt(jnp.sum(q_rot * q_rot, axis=-1, keepdims=True) * (1.0 / MLA_QK) + EPS)
            dx, dg = _rms_bwd(dqn_ref[:, sl], q_rot, r, qn_g_ref[...], MLA_QK)
            dqng = dqng + dg
            dq_raw.append(_rope_t(dx, c, s1, s2))
            k_full = kv_raw[:, sl] + kpe_rot
            r = lax.rsqrt(jnp.sum(k_full * k_full, axis=-1, keepdims=True) * (1.0 / MLA_QK) + EPS)
            dx, dg = _rms_bwd(dkn_ref[:, sl], k_full, r, kn_g_ref[...], MLA_QK)
            dkng = dkng + dg
            dk_raw.append(jnp.where(lane < 64, dx, 0.0))
            dkpe_rot = dkpe_rot + dx
        dkpe = _rope_t(dkpe_rot, c, s1, s2)
        dkpe_ref[...] = jnp.where(jnp.logical_and(lane >= 64, lane < 64 + ROPE), dkpe, 0.0).astype(bf16)
        dq_raw = jnp.concatenate(dq_raw, axis=1).astype(bf16)
        dkv_raw = jnp.concatenate(dk_raw + [dv_ref[...]], axis=1).astype(bf16)
        dcqn = _nt(dq_raw, wq_ref[...])
        dckvn = _nt(dkv_raw, wkv_ref[...])
        dx, dg = _rms_bwd(dcqn, cq, r_cq, qa_ref[...], MLA_Q_RANK)
        dcq_ref[...] = dx.astype(bf16)
        _accumulate(dqa_ref, dg)
        dx, dg = _rms_bwd(dckvn, ckv, r_ckv, kva_ref[...], MLA_KV_RANK)
        dckv_ref[...] = dx.astype(bf16)
        _accumulate(dkva_ref, dg)
        _accumulate(dwq_ref, _tn(cqn_b, dq_raw))
        _accumulate(dwkv_ref, _tn(ckvn_b, dkv_raw))
        _accumulate(dqng_ref, dqng)
        _accumulate(dkng_ref, dkng)

    dcq, dckv, dkpe, dwq, dwkv, dqa, dkva, dqng, dkng = _rowwise(
        "mla_prep_bwd", mla_prep_bwd_body, n_tok,
        [(proj, *_seg("cq")), (proj, *_seg("ckv")), (proj, *_seg("kpe")), (c_tab, LANE, 0), (s1_tab, LANE, 0), (s2_tab, LANE, 0),
         (dqn, HEADS * LANE, 0), (dkn, HEADS * LANE, 0), (dv_mla, 512, 0)],
        [w["qa_g"], w["kva_g"], w["wq"], w["wkv"], w["qn_g"], w["kn_g"]],
        [(MLA_Q_RANK, bf16), (LANE, bf16), (LANE, bf16)],
        [((MLA_Q_RANK, HEADS * LANE), f32), ((MLA_KV_RANK, HEADS * LANE + 512), f32), ((1, MLA_Q_RANK), f32),
         ((1, MLA_KV_RANK), f32), ((1, LANE), f32), ((1, LANE), f32)])
    grads["mla_w_q_up"] = dwq.reshape(MLA_Q_RANK, HEADS, LANE)[:, :, :MLA_QK].reshape(MLA_Q_RANK, HEADS * MLA_QK)
    dwk = dwkv[:, :HEADS * LANE].reshape(MLA_KV_RANK, HEADS, LANE)[:, :, :64]
    dwv = dwkv[:, HEADS * LANE:].reshape(MLA_KV_RANK, HEADS, 64)
    grads["mla_w_kv_up"] = jnp.concatenate([dwk, dwv], axis=2).reshape(MLA_KV_RANK, HEADS * 128)
    grads["mla_q_a_norm"], grads["mla_kv_a_norm"] = dqa.reshape(-1), dkva.reshape(-1)
    grads["mla_q_norm"], grads["mla_k_norm"] = dqng[0, :MLA_QK], dkng[0, :MLA_QK]

    _, _, shard_c, shard_cp = _BIG_SHARD["w_in"]
    kpe0 = _PAD["kpe"][2]
    pieces = [dcq, dckv, dkpe[:, kpe0:kpe0 + ROPE], dfq, dfk, dfvb, dff[:, :HEADS], ds5u, dg_mla, dg_fox, dg_s5,
              dm_mla, dm_fox, dm_s5]
    gap = jnp.zeros((n_tok, shard_cp - shard_c), bf16)
    cut, pos = [], 0
    for p in pieces:
        start = 0
        while start < p.shape[1]:
            take = min(p.shape[1] - start, shard_c - pos % shard_c)
            cut.append(p[:, start:start + take])
            start, pos = start + take, pos + take
            if pos % shard_c == 0:
                cut.append(gap)
    dproj = jnp.concatenate(cut, axis=1)
    ct = 256
    per = shard_cp // ct
    dh = _mm("in_proj_dgrad", dproj, w["w_in_shards"], mode="nt", grid=(1, 1, N_CHIPS * per),
             a_spec=pl.BlockSpec((n_tok, ct), lambda i, j, kk: (0, kk)),
             b_spec=pl.BlockSpec((None, D_MODEL, ct), lambda i, j, kk: (kk // per, 0, kk % per)),
             o_spec=pl.BlockSpec((n_tok, D_MODEL), lambda i, j, kk: (0, 0)),
             out_shape=jax.ShapeDtypeStruct((n_tok, D_MODEL), f32), acc_shape=(n_tok, D_MODEL))
    grads["w_in"] = _mm("in_proj_wgrad", sv["h"], dproj, mode="tn", grid=(1, N_CHIPS * per, 1),
                        a_spec=pl.BlockSpec((n_tok, D_MODEL), lambda i, j, kk: (0, 0)),
                        b_spec=pl.BlockSpec((n_tok, ct), lambda i, j, kk: (0, j)),
                        o_spec=pl.BlockSpec((None, D_MODEL, ct), lambda i, j, kk: (j // per, 0, j % per)),
                        out_shape=jax.ShapeDtypeStruct((N_CHIPS, D_MODEL, shard_cp), f32), acc_shape=(D_MODEL, ct))

    def norm_bwd_body(dh_ref, x_ref, do_ref, g_ref, dx_ref, dg_ref):
        xv = x_ref[...]
        r = lax.rsqrt(jnp.sum(xv * xv, axis=-1, keepdims=True) * (1.0 / D_MODEL) + EPS)
        dx, dg = _rms_bwd(dh_ref[...], xv, r, g_ref[...], D_MODEL)
        dx_ref[...] = do_ref[...] + dx
        _accumulate(dg_ref, dg)

    dx, dng = _rowwise("norm_bwd", norm_bwd_body, n_tok, [(dh, D_MODEL, 0), (x, D_MODEL, 0), (dout, D_MODEL, 0)],
                       [w["norm_g"]], [(D_MODEL, f32)], [((1, D_MODEL), f32)])
    grads["norm_g"] = dng.reshape(D_MODEL)
    return dx, grads


def _rope_tables(positions):
    inv = 1.0 / (ROPE_THETA ** (jnp.arange(0, ROPE, 2, dtype=f32) / ROPE))
    ang = positions.astype(f32).reshape(-1, 1) * inv
    cos, sin = jnp.cos(ang), jnp.sin(ang)
    n = ang.shape[0]
    z16, z32, z64 = jnp.zeros((n, 16), f32), jnp.zeros((n, 32), f32), jnp.zeros((n, 64), f32)
    c = jnp.concatenate([jnp.ones((n, 64), f32), cos, cos, z32], axis=1)
    s1 = jnp.concatenate([z64, -sin, z16, z32], axis=1)
    s2 = jnp.concatenate([z64, z16, sin, z32], axis=1)
    return c, s1, s2


BIG = ("w_in", "mla_w_q_up", "mla_w_kv_up", "s5_w_glu", "w_branch_out", "w_out")
SMALL = ("norm_g", "mla_q_a_norm", "mla_kv_a_norm", "mla_q_norm", "mla_k_norm", "fox_b_f", "fox_q_norm", "fox_k_norm",
         "s5_lambda_re", "s5_lambda_im", "s5_log_dt", "s5_b_re", "s5_b_im", "s5_c_re", "s5_c_im", "s5_d", "s5_b_glu")
WEIGHTS = ("norm_g", "w_in", "mla_q_a_norm", "mla_w_q_up", "mla_kv_a_norm", "mla_w_kv_up", "mla_q_norm", "mla_k_norm",
           "fox_b_f", "fox_q_norm", "fox_k_norm", "s5_lambda_re", "s5_lambda_im", "s5_log_dt", "s5_b_re", "s5_b_im",
           "s5_c_re", "s5_c_im", "s5_d", "s5_w_glu", "s5_b_glu", "w_branch_out", "w_out")


def _local_step(x, positions, loss_target, small, big):
    n_tok = x.shape[0]
    tabs = _rope_tables(positions)
    ws, saves = [], []
    hcur = x
    stacked = _prep_weights(small, big)
    for l in range(DEPTH):
        w = {k: v[l] for k, v in stacked.items()}
        hcur, sv = _layer_fwd(hcur, w, tabs, n_tok)
        ws.append(w)
        saves.append(sv)

    def loss_body(y_ref, t_ref, d_ref, l_ref):
        err = y_ref[...] - t_ref[...]
        d_ref[...] = err * (1.0 / D_MODEL)
        tot = jnp.sum(jnp.sum(err * err, axis=-1, keepdims=True), axis=0, keepdims=True)
        _accumulate(l_ref, jnp.broadcast_to(tot * (0.5 / D_MODEL), (1, LANE)))

    dcur, loss = _rowwise("loss", loss_body, n_tok, [(hcur, D_MODEL, 0), (loss_target, D_MODEL, 0)], [], [(D_MODEL, f32)],
                          [((1, LANE), f32)])
    layer_grads = [None] * DEPTH
    for l in reversed(range(DEPTH)):
        dcur, layer_grads[l] = _layer_bwd(dcur, ws[l], saves[l], tabs, n_tok)
    grads = {n: jnp.stack([layer_grads[l][n] for l in range(DEPTH)]) for n in WEIGHTS}
    return loss[0, 0], dcur, grads


N_DEV = 8
_ANY = pl.BlockSpec(memory_space=pl.ANY)
_MESH = pl.DeviceIdType.MESH


def _all_gather8(name, blk):
    m = blk.shape[0]

    def body(x_ref, out_ref, send_sems, recv_sems, local_sem):
        x, y, c = lax.axis_index("x"), lax.axis_index("y"), lax.axis_index("c")
        me, sibling = (x, y, c), (x, y, 1 - c)
        chips = [(1 - x, y), (x, 1 - y), (1 - x, 1 - y)]

        def slot(px, py, pc):
            return out_ref.at[4 * px + 2 * py + pc]

        def copy(k, block, to, src=None):
            return pltpu.make_async_remote_copy(
                src_ref=slot(*block) if src is None else src, dst_ref=slot(*block),
                send_sem=send_sems.at[k], recv_sem=recv_sems.at[k], device_id=to, device_id_type=_MESH)

        mine = pltpu.make_async_copy(x_ref, slot(*me), local_sem)
        mine.start()
        first = [copy(0, me, sibling, src=x_ref)]
        first += [copy(1 + j, me, (*chip, c), src=x_ref) for j, chip in enumerate(chips)]
        for cp in first:
            cp.start()
        passed = [copy(4 + j, (*chip, c), sibling) for j, chip in enumerate(chips)]
        for j, chip in enumerate(chips):
            copy(1 + j, (*chip, c), me).wait_recv()
            passed[j].start()
        copy(0, sibling, me).wait_recv()
        for j, chip in enumerate(chips):
            copy(4 + j, (*chip, 1 - c), me).wait_recv()
        for cp in first + passed:
            cp.wait_send()
        mine.wait()

    return pl.pallas_call(
        body, out_shape=jax.ShapeDtypeStruct((N_DEV, m, LANE), blk.dtype), in_specs=[_ANY], out_specs=_ANY, name=name,
        scratch_shapes=[pltpu.SemaphoreType.DMA((7,)), pltpu.SemaphoreType.DMA((7,)), pltpu.SemaphoreType.DMA],
    )(blk)


def _gather_layers(name, shards):
    n = len(shards)

    def body(*refs):
        x_refs, out_refs = refs[:n], refs[n:2 * n]
        send_sems, recv_sems, local_sems = refs[2 * n:]
        x, y, c = lax.axis_index("x"), lax.axis_index("y"), lax.axis_index("c")
        me, sibling = (x, y, c), (x, y, 1 - c)
        xn, yn, dg = (1 - x, y, c), (x, 1 - y, c), (1 - x, 1 - y, c)
        relay_from = (x + (1 - c) * (1 - 2 * x), y + c * (1 - 2 * y), c)
        relay_to = (x + c * (1 - 2 * x), y + (1 - c) * (1 - 2 * y), c)

        def copy(w, k, block, to, src=None):
            px, py, pc = block
            slot = out_refs[w].at[pc, 2 * px + py]
            return pltpu.make_async_remote_copy(
                src_ref=slot if src is None else src, dst_ref=slot, send_sem=send_sems.at[7 * w + k],
                recv_sem=recv_sems.at[7 * w + k], device_id=to, device_id_type=_MESH)

        started, local = [], []
        for w in range(n):
            src = x_refs[w].at[c]
            mine = pltpu.make_async_copy(src, out_refs[w].at[c, 2 * x + y], local_sems.at[w])
            mine.start()
            local.append(mine)
            first = [copy(w, 0, me, sibling, src=src), copy(w, 1, me, xn, src=src), copy(w, 2, me, yn, src=src)]
            for cp in first:
                cp.start()
            started += first
        for w in range(n):
            copy(w, 1, xn, me).wait_recv()
            copy(w, 2, yn, me).wait_recv()
            onward = [copy(w, 3, relay_from, relay_to), copy(w, 4, xn, sibling), copy(w, 5, yn, sibling)]
            for cp in onward:
                cp.start()
            started += onward
        for w in range(n):
            copy(w, 3, dg, me).wait_recv()
            onward = copy(w, 6, dg, sibling)
            onward.start()
            started.append(onward)
        for w in range(n):
            copy(w, 0, sibling, me).wait_recv()
            for k, chip in ((4, xn), (5, yn), (6, dg)):
                copy(w, k, (chip[0], chip[1], 1 - c), me).wait_recv()
        for cp in started:
            cp.wait_send()
        for cp in local:
            cp.wait()

    return pl.pallas_call(
        body, out_shape=[jax.ShapeDtypeStruct((2, N_CHIPS) + s.shape[1:], s.dtype) for s in shards],
        in_specs=[_ANY] * n, out_specs=[_ANY] * n, name=name,
        scratch_shapes=[pltpu.SemaphoreType.DMA((7 * n,)), pltpu.SemaphoreType.DMA((7 * n,)), pltpu.SemaphoreType.DMA((n,))],
    )(*shards)


def _swap_layers(name, parts):
    n = len(parts)

    def body(*refs):
        p_refs, got_refs = refs[:n], refs[n:2 * n]
        send_sems, recv_sems = refs[2 * n:]
        x, y, c = lax.axis_index("x"), lax.axis_index("y"), lax.axis_index("c")
        copies = []
        for w in range(n):
            cp = pltpu.make_async_remote_copy(
                src_ref=p_refs[w].at[1 - c], dst_ref=got_refs[w], send_sem=send_sems.at[w], recv_sem=recv_sems.at[w],
                device_id=(x, y, 1 - c), device_id_type=_MESH)
            cp.start()
            copies.append(cp)
        for cp in copies:
            cp.wait()

    return pl.pallas_call(
        body, out_shape=[jax.ShapeDtypeStruct(p.shape[1:], p.dtype) for p in parts], in_specs=[_ANY] * n, out_specs=[_ANY] * n,
        name=name, scratch_shapes=[pltpu.SemaphoreType.DMA((n,)), pltpu.SemaphoreType.DMA((n,))],
    )(*parts)


def _scatter_to_chips(name, parts):
    n = len(parts)

    def body(*refs):
        p_refs, out_refs = refs[:n], refs[n:2 * n]
        send_sems, recv_sems, local_sems = refs[2 * n:]
        x, y, c = lax.axis_index("x"), lax.axis_index("y"), lax.axis_index("c")
        jme = 2 * x + y
        chips = [(1 - x, y), (x, 1 - y), (1 - x, 1 - y)]
        sends, local = [], []
        for w in range(n):
            mine = pltpu.make_async_copy(p_refs[w].at[jme], out_refs[w].at[jme], local_sems.at[w])
            mine.start()
            local.append(mine)
            for k, (tx, ty) in enumerate(chips):
                cp = pltpu.make_async_remote_copy(
                    src_ref=p_refs[w].at[2 * tx + ty], dst_ref=out_refs[w].at[jme], send_sem=send_sems.at[3 * w + k],
                    recv_sem=recv_sems.at[3 * w + k], device_id=(tx, ty, c), device_id_type=_MESH)
                cp.start()
                sends.append(cp)
        for w in range(n):
            for k, (tx, ty) in enumerate(chips):
                pltpu.make_async_remote_copy(
                    src_ref=p_refs[w].at[jme], dst_ref=out_refs[w].at[2 * tx + ty], send_sem=send_sems.at[3 * w + k],
                    recv_sem=recv_sems.at[3 * w + k], device_id=(tx, ty, c), device_id_type=_MESH).wait_recv()
        for cp in sends:
            cp.wait_send()
        for cp in local:
            cp.wait()

    return pl.pallas_call(
        body, out_shape=[jax.ShapeDtypeStruct(p.shape, p.dtype) for p in parts], in_specs=[_ANY] * n, out_specs=[_ANY] * n, name=name,
        scratch_shapes=[pltpu.SemaphoreType.DMA((3 * n,)), pltpu.SemaphoreType.DMA((3 * n,)), pltpu.SemaphoreType.DMA((n,))],
    )(*parts)


def _share_layers(name, bufs):
    n = len(bufs)

    def body(*refs):
        out_refs = refs[n:2 * n]
        send_sems, recv_sems = refs[2 * n:]
        x, y, c = lax.axis_index("x"), lax.axis_index("y"), lax.axis_index("c")
        copies = []
        for w in range(n):
            cp = pltpu.make_async_remote_copy(src_ref=out_refs[w].at[c], dst_ref=out_refs[w].at[c], send_sem=send_sems.at[w],
                                              recv_sem=recv_sems.at[w], device_id=(x, y, 1 - c), device_id_type=_MESH)
            cp.start()
            copies.append(cp)
        for w in range(n):
            pltpu.make_async_remote_copy(src_ref=out_refs[w].at[c], dst_ref=out_refs[w].at[1 - c], send_sem=send_sems.at[w],
                                         recv_sem=recv_sems.at[w], device_id=(x, y, 1 - c), device_id_type=_MESH).wait_recv()
        for cp in copies:
            cp.wait_send()

    return pl.pallas_call(
        body, out_shape=[jax.ShapeDtypeStruct(b.shape, b.dtype) for b in bufs], in_specs=[_ANY] * n, out_specs=[_ANY] * n,
        input_output_aliases={w: w for w in range(n)}, name=name,
        scratch_shapes=[pltpu.SemaphoreType.DMA((n,)), pltpu.SemaphoreType.DMA((n,))],
    )(*bufs)


def _row_tile(rows, cols):
    best = 16
    for t in range(16, rows + 1, 16):
        if rows % t == 0 and t * cols * 4 <= 2 * 1024 * 1024:
            best = t
    return best


def _add_pair(name, core, parts, got, out_dtype):
    _, _, r, c = parts.shape
    t = _row_tile(r, c)

    def body(core_ref, a_ref, b_ref, o_ref):
        o_ref[...] = (a_ref[...] + b_ref[...]).astype(o_ref.dtype)

    spec = pl.BlockSpec((None, t, c), lambda j, i, core_ref: (j, i, 0))
    grid_spec = pltpu.PrefetchScalarGridSpec(
        num_scalar_prefetch=1, grid=(N_CHIPS, r // t),
        in_specs=[pl.BlockSpec((None, None, t, c), lambda j, i, core_ref: (core_ref[0], j, i, 0)), spec], out_specs=spec)
    return pl.pallas_call(body, grid_spec=grid_spec, out_shape=jax.ShapeDtypeStruct(got.shape, out_dtype), name=name,
                          compiler_params=pltpu.CompilerParams(dimension_semantics=("arbitrary", "arbitrary")))(core, parts, got)


def _add_four(name, core, a):
    _, r, c = a.shape
    t = _row_tile(r, c)

    def body(core_ref, a0, a1, a2, a3, o_ref):
        o_ref[...] = ((a0[...].astype(f32) + a1[...].astype(f32)) + a2[...].astype(f32)) + a3[...].astype(f32)

    specs = [pl.BlockSpec((None, t, c), functools.partial(lambda i, core_ref, k: (k, i, 0), k=k)) for k in range(N_CHIPS)]
    grid_spec = pltpu.PrefetchScalarGridSpec(
        num_scalar_prefetch=1, grid=(r // t,), in_specs=specs,
        out_specs=pl.BlockSpec((None, t, c), lambda i, core_ref: (core_ref[0], i, 0)))
    return pl.pallas_call(body, grid_spec=grid_spec, out_shape=jax.ShapeDtypeStruct((2, r, c), f32), name=name,
                          compiler_params=pltpu.CompilerParams(dimension_semantics=("arbitrary",)))(core, a, a, a, a)


def _adamw(name, w, g, m, v, row_tile=None, lead_tile=None):
    c1 = 1.0 - ADAM_B1 ** ADAM_STEP
    c2 = 1.0 - ADAM_B2 ** ADAM_STEP

    def body(w_ref, g_ref, m_ref, v_ref, d_ref, nm_ref, nv_ref):
        gv = g_ref[...]
        nm = ADAM_B1 * m_ref[...] + (1.0 - ADAM_B1) * gv
        nv = ADAM_B2 * v_ref[...] + (1.0 - ADAM_B2) * (gv * gv)
        m_hat = nm / c1
        v_hat = nv / c2
        d_ref[...] = -ADAM_LR * (m_hat / (jnp.sqrt(v_hat) + ADAM_EPS) + ADAM_WD * w_ref[...])
        nm_ref[...] = nm
        nv_ref[...] = nv

    sds = jax.ShapeDtypeStruct(w.shape, f32)
    if lead_tile is not None:
        spec = pl.BlockSpec((lead_tile,) + w.shape[1:], lambda i: (i, 0, 0))
        return pl.pallas_call(body, grid=(w.shape[0] // lead_tile,), in_specs=[spec] * 4, out_specs=[spec] * 3, out_shape=[sds] * 3,
                              name=name, compiler_params=pltpu.CompilerParams(dimension_semantics=("arbitrary",), vmem_limit_bytes=VMEM_LIMIT),
                              )(w, g, m, v)
    if row_tile is None:
        return pl.pallas_call(body, out_shape=[sds] * 3, name=name)(w, g, m, v)
    _, r, c = w.shape
    spec = pl.BlockSpec((None, row_tile, c), lambda l, i: (l, i, 0))
    return pl.pallas_call(body, grid=(DEPTH, r // row_tile), in_specs=[spec] * 4, out_specs=[spec] * 3, out_shape=[sds] * 3, name=name,
                          compiler_params=pltpu.CompilerParams(dimension_semantics=("arbitrary", "arbitrary"), vmem_limit_bytes=VMEM_LIMIT),
                          )(w, g, m, v)


def _pad_rows(flat, rows):
    return jnp.pad(flat, (0, rows * LANE - flat.shape[0])).reshape(rows, LANE)


def kernel(x, positions, norm_g, w_in, mla_q_a_norm, mla_w_q_up, mla_kv_a_norm, mla_w_kv_up, mla_q_norm, mla_k_norm, fox_b_f, fox_q_norm, fox_k_norm, s5_lambda_re, s5_lambda_im, s5_log_dt, s5_b_re, s5_b_im, s5_c_re, s5_c_im, s5_d, s5_w_glu, s5_b_glu, w_branch_out, w_out, loss_target, m_norm_g, m_w_in, m_mla_q_a_norm, m_mla_w_q_up, m_mla_kv_a_norm, m_mla_w_kv_up, m_mla_q_norm, m_mla_k_norm, m_fox_b_f, m_fox_q_norm, m_fox_k_norm, m_s5_lambda_re, m_s5_lambda_im, m_s5_log_dt, m_s5_b_re, m_s5_b_im, m_s5_c_re, m_s5_c_im, m_s5_d, m_s5_w_glu, m_s5_b_glu, m_w_branch_out, m_w_out, v_norm_g, v_w_in, v_mla_q_a_norm, v_mla_w_q_up, v_mla_kv_a_norm, v_mla_w_kv_up, v_mla_q_norm, v_mla_k_norm, v_fox_b_f, v_fox_q_norm, v_fox_k_norm, v_s5_lambda_re, v_s5_lambda_im, v_s5_log_dt, v_s5_b_re, v_s5_b_im, v_s5_c_re, v_s5_c_im, v_s5_d, v_s5_w_glu, v_s5_b_glu, v_w_branch_out, v_w_out):
    given = dict(locals())
    wts = {n: given[n] for n in WEIGHTS}
    mom1 = {n: given["m_" + n] for n in WEIGHTS}
    mom2 = {n: given["v_" + n] for n in WEIGHTS}

    def lanes(n, a):
        _, _, c, cp = _BIG_SHARD[n]
        return jnp.pad(a, ((0, 0), (0, 0), (0, cp - c)))

    gathered = _gather_layers("gather_weights", [lanes(n, wts[n].astype(bf16)) for n in BIG])
    big = dict(zip(BIG, gathered))
    small = {n: wts[n] for n in SMALL}

    loss_local, grad_x, grads = _local_step(x[0], positions, loss_target[0], small, big)
    loss = lax.psum(loss_local, ("x", "y", "c"))

    small_flat = jnp.concatenate([grads[n].reshape(-1) for n in SMALL])
    small_rows = -(-small_flat.shape[0] // (N_DEV * 16 * LANE)) * 16
    parts = [grads[n] if n == "w_in" else jnp.stack([_to_shards(n, grads[n][l]) for l in range(DEPTH)]) for n in BIG]
    parts.append(jnp.swapaxes(_pad_rows(small_flat, N_DEV * small_rows).reshape(N_CHIPS, 2, small_rows, LANE), 0, 1))
    core = lax.axis_index("c")
    core1 = core.reshape(1).astype(jnp.int32)
    got = _swap_layers("grads_to_sibling", parts)
    hop = [bf16] * len(BIG) + [f32]
    pair = [_add_pair("grads_pair_sum_%d" % i, core1, a, b, dt) for i, (a, b, dt) in enumerate(zip(parts, got, hop))]
    landed = _scatter_to_chips("grads_to_chips", pair)
    total = [_add_four("grads_chip_sum_%d" % i, core1, a) for i, a in enumerate(landed)]
    shared = _share_layers("grads_share", total[:-1])
    small_mine = lax.dynamic_index_in_dim(total[-1], core, 0, keepdims=False)
    small_all = _all_gather8("gather_small_grads", small_mine).reshape(-1)

    g_out = {n: s[:, :, :_BIG_SHARD[n][2]] for n, s in zip(BIG, shared)}
    pos = 0
    for n in SMALL:
        g_out[n] = small_all[pos:pos + wts[n].size].reshape(wts[n].shape)
        pos += wts[n].size

    delta, new_m, new_v = {}, {}, {}
    for n in WEIGHTS:
        if n == "w_in":
            cols_first = lambda a: jnp.transpose(a, (2, 0, 1))
            res = _adamw("adamw_" + n, *[cols_first(a) for a in (wts[n], g_out[n], mom1[n], mom2[n])], lead_tile=177)
            delta[n], new_m[n], new_v[n] = [jnp.transpose(a, (1, 2, 0)) for a in res]
            continue
        row_tile = _row_tile(*wts[n].shape[1:]) if n in BIG else None
        delta[n], new_m[n], new_v[n] = _adamw("adamw_" + n, wts[n], g_out[n], mom1[n], mom2[n], row_tile)

    return (loss, grad_x[None], *[g_out[n] for n in WEIGHTS], *[delta[n] for n in WEIGHTS],
            *[new_m[n] for n in WEIGHTS], *[new_v[n] for n in WEIGHTS])
```
